```python
import jax, jax.numpy as jnp
from jax import lax
import numpy as np

D_MODEL = 2048
BATCH = 8
SEQ = 4096
DEPTH = 1

MIX_WIDTH = D_MODEL
A_WIDTH = MIX_WIDTH // 2
B_WIDTH = MIX_WIDTH - A_WIDTH
CHUNK = 128
A_HEAD_DIM = 128
A_HEADS = A_WIDTH // A_HEAD_DIM
POOL_WINDOWS = (2, 4, 8, 16)
B_GROUPS = len(POOL_WINDOWS)
B_GROUP_DIM = B_WIDTH // B_GROUPS
IN_WIDTH = 2 * A_WIDTH + B_WIDTH
D_FF = 4 * D_MODEL
EPS = 1e-6

kernel_name = "hybrid_sgu_pool_block"


def rmsnorm(x, g):
    xf = x.astype(jnp.float32)
    y = xf * lax.rsqrt(jnp.mean(xf * xf, axis=-1, keepdims=True) + EPS)
    return (y * g.astype(jnp.float32)).astype(x.dtype)


def spatial_gating(u, v, w_s, b_s, g_v):
    bsz, s, _ = u.shape
    n_chunks = s // CHUNK
    vh = rmsnorm(v.reshape(bsz, s, A_HEADS, A_HEAD_DIM), g_v.reshape(A_HEADS, A_HEAD_DIM))
    vh = vh.reshape(bsz, n_chunks, CHUNK, A_HEADS, A_HEAD_DIM)
    causal = jnp.tril(jnp.ones((CHUNK, CHUNK), dtype=bool))
    w = jnp.where(causal[None], w_s, jnp.zeros_like(w_s))
    mixed = jnp.einsum('hts,bcshd->bcthd', w, vh)
    mixed = mixed + jnp.transpose(b_s)[None, None, :, :, None]
    return u * mixed.reshape(bsz, s, A_WIDTH)


def multiscale_pool(z, w_pool, pool_scale):
    bsz, s, _ = z.shape
    zf = z.astype(jnp.float32).reshape(bsz, s, B_GROUPS, B_GROUP_DIM)
    csum = jnp.cumsum(zf, axis=1)
    cpad = jnp.concatenate([jnp.zeros_like(csum[:, :1]), csum], axis=1)
    pos = jnp.arange(s, dtype=jnp.int32)
    outs = []
    for g, win in enumerate(POOL_WINDOWS):
        c = cpad[:, :, g]
        lag = jnp.pad(c, ((0, 0), (win - 1, 0), (0, 0)))[:, :s]
        count = jnp.minimum(pos + 1, win).astype(jnp.float32)[None, :, None]
        outs.append((c[:, 1:] - lag) / count - zf[:, :, g])
    pooled = jnp.stack(outs, axis=2).astype(z.dtype)
    y = jnp.einsum('bsgc,gcd->bsgd', pooled, w_pool).reshape(bsz, s, B_WIDTH)
    return y * pool_scale


def _fwd_setup_inputs(seed: int = 0) -> dict:
    key = jax.random.key(seed)
    ks = jax.random.split(key, 16)
    f32 = jnp.float32
    x = jax.random.normal(ks[0], (BATCH, SEQ, D_MODEL), f32)
    g_mix = 1.0 + 0.05 * jax.random.normal(ks[1], (DEPTH, D_MODEL), f32)
    w_in = jax.random.normal(ks[2], (DEPTH, D_MODEL, IN_WIDTH), f32) * D_MODEL ** -0.5
    g_v = 1.0 + 0.05 * jax.random.normal(ks[3], (DEPTH, A_WIDTH), f32)
    w_s = jax.random.normal(ks[4], (DEPTH, A_HEADS, CHUNK, CHUNK), f32) * (0.5 * CHUNK ** -0.5)
    b_s = 1.0 + 0.1 * jax.random.normal(ks[5], (DEPTH, A_HEADS, CHUNK), f32)
    w_pool = jax.random.normal(ks[6], (DEPTH, B_GROUPS, B_GROUP_DIM, B_GROUP_DIM), f32) * B_GROUP_DIM ** -0.5
    pool_scale = 0.5 + 0.1 * jax.random.normal(ks[7], (DEPTH, B_WIDTH), f32)
    w_out = jax.random.normal(ks[8], (DEPTH, MIX_WIDTH, D_MODEL), f32) * MIX_WIDTH ** -0.5
    g_ffn = 1.0 + 0.05 * jax.random.normal(ks[9], (DEPTH, D_MODEL), f32)
    w_up = jax.random.normal(ks[10], (DEPTH, D_MODEL, D_FF), f32) * D_MODEL ** -0.5
    w_down = jax.random.normal(ks[11], (DEPTH, D_FF, D_MODEL), f32) * D_FF ** -0.5
    g_final = 1.0 + 0.05 * jax.random.normal(ks[12], (D_MODEL,), f32)
    return {"x": x, "g_mix": g_mix, "w_in": w_in, "g_v": g_v, "w_s": w_s,
            "b_s": b_s, "w_pool": w_pool, "pool_scale": pool_scale,
            "w_out": w_out, "g_ffn": g_ffn, "w_up": w_up, "w_down": w_down,
            "g_final": g_final}


def _fwd_reference(x, g_mix, w_in, g_v, w_s, b_s, w_pool, pool_scale, w_out,
              g_ffn, w_up, w_down, g_final):
    for layer in range(DEPTH):
        h = rmsnorm(x, g_mix[layer])
        proj = jnp.einsum('bsd,de->bse', h, w_in[layer])
        u = jax.nn.gelu(proj[..., :A_WIDTH])
        v = jax.nn.gelu(proj[..., A_WIDTH:2 * A_WIDTH])
        z = proj[..., 2 * A_WIDTH:]
        out_a = spatial_gating(u, v, w_s[layer], b_s[layer], g_v[layer])
        out_b = multiscale_pool(z, w_pool[layer], pool_scale[layer])
        mixed = jnp.concatenate([out_a, out_b], axis=-1)
        x = x + jnp.einsum('bse,ed->bsd', mixed, w_out[layer])
        h = rmsnorm(x, g_ffn[layer])
        act = jnp.square(jax.nn.relu(jnp.einsum('bsd,df->bsf', h, w_up[layer])))
        x = x + jnp.einsum('bsf,fd->bsd', act, w_down[layer])
    return rmsnorm(x, g_final)


import jax as _jax
import jax.numpy as _jnp

TWIN_FORMAT = 'train_step'
FWD_PARAMS = ['x', 'g_mix', 'w_in', 'g_v', 'w_s', 'b_s', 'w_pool', 'pool_scale', 'w_out', 'g_ffn', 'w_up', 'w_down', 'g_final']
TWIN_WEIGHTS = ['g_mix', 'w_in', 'g_v', 'w_s', 'b_s', 'w_pool', 'pool_scale', 'w_out', 'g_ffn', 'w_up', 'w_down', 'g_final']
TWIN_DIFF_INPUT = 'x'
TWIN_INPUTS = ['x', 'g_mix', 'w_in', 'g_v', 'w_s', 'b_s', 'w_pool', 'pool_scale', 'w_out', 'g_ffn', 'w_up', 'w_down', 'g_final', 'loss_target', 'm_g_mix', 'm_w_in', 'm_g_v', 'm_w_s', 'm_b_s', 'm_w_pool', 'm_pool_scale', 'm_w_out', 'm_g_ffn', 'm_w_up', 'm_w_down', 'm_g_final', 'v_g_mix', 'v_w_in', 'v_g_v', 'v_w_s', 'v_b_s', 'v_w_pool', 'v_pool_scale', 'v_w_out', 'v_g_ffn', 'v_w_up', 'v_w_down', 'v_g_final']
TWIN_OUTPUTS = ['loss', 'grad_x', 'grad_g_mix', 'grad_w_in', 'grad_g_v', 'grad_w_s', 'grad_b_s', 'grad_w_pool', 'grad_pool_scale', 'grad_w_out', 'grad_g_ffn', 'grad_w_up', 'grad_w_down', 'grad_g_final', 'delta_g_mix', 'delta_w_in', 'delta_g_v', 'delta_w_s', 'delta_b_s', 'delta_w_pool', 'delta_pool_scale', 'delta_w_out', 'delta_g_ffn', 'delta_w_up', 'delta_w_down', 'delta_g_final', 'new_m_g_mix', 'new_m_w_in', 'new_m_g_v', 'new_m_w_s', 'new_m_b_s', 'new_m_w_pool', 'new_m_pool_scale', 'new_m_w_out', 'new_m_g_ffn', 'new_m_w_up', 'new_m_w_down', 'new_m_g_final', 'new_v_g_mix', 'new_v_w_in', 'new_v_g_v', 'new_v_w_s', 'new_v_b_s', 'new_v_w_pool', 'new_v_pool_scale', 'new_v_w_out', 'new_v_g_ffn', 'new_v_w_up', 'new_v_w_down', 'new_v_g_final']
TWIN_LEAF_KINDS = {'loss': 'loss', 'grad_x': 'grad_x', 'grad_g_mix': 'grad_w', 'grad_w_in': 'grad_w', 'grad_g_v': 'grad_w', 'grad_w_s': 'grad_w', 'grad_b_s': 'grad_w', 'grad_w_pool': 'grad_w', 'grad_pool_scale': 'grad_w', 'grad_w_out': 'grad_w', 'grad_g_ffn': 'grad_w', 'grad_w_up': 'grad_w', 'grad_w_down': 'grad_w', 'grad_g_final': 'grad_w', 'delta_g_mix': 'delta_w', 'delta_w_in': 'delta_w', 'delta_g_v': 'delta_w', 'delta_w_s': 'delta_w', 'delta_b_s': 'delta_w', 'delta_w_pool': 'delta_w', 'delta_pool_scale': 'delta_w', 'delta_w_out': 'delta_w', 'delta_g_ffn': 'delta_w', 'delta_w_up': 'delta_w', 'delta_w_down': 'delta_w', 'delta_g_final': 'delta_w', 'new_m_g_mix': 'new_m', 'new_m_w_in': 'new_m', 'new_m_g_v': 'new_m', 'new_m_w_s': 'new_m', 'new_m_b_s': 'new_m', 'new_m_w_pool': 'new_m', 'new_m_pool_scale': 'new_m', 'new_m_w_out': 'new_m', 'new_m_g_ffn': 'new_m', 'new_m_w_up': 'new_m', 'new_m_w_down': 'new_m', 'new_m_g_final': 'new_m', 'new_v_g_mix': 'new_v', 'new_v_w_in': 'new_v', 'new_v_g_v': 'new_v', 'new_v_w_s': 'new_v', 'new_v_b_s': 'new_v', 'new_v_w_pool': 'new_v', 'new_v_pool_scale': 'new_v', 'new_v_w_out': 'new_v', 'new_v_g_ffn': 'new_v', 'new_v_w_up': 'new_v', 'new_v_w_down': 'new_v', 'new_v_g_final': 'new_v'}


def _forward(args):
    return _fwd_reference(*[args[k] for k in FWD_PARAMS])


def _output_shape():
    def fwd():
        inp = _fwd_setup_inputs(0)
        return _fwd_reference(*[inp[k] for k in FWD_PARAMS])
    out = _jax.eval_shape(fwd)
    return out.shape, out.dtype

N_MICROBATCH = 1
ADAM_LR = 0.001
ADAM_B1 = 0.9
ADAM_B2 = 0.999
ADAM_EPS = 1e-08
ADAM_WD = 0.01
ADAM_STEP = 10
PER_EXAMPLE_BATCH_AXIS = {'x': 0, 'loss_target': 0}
SHARED_INPUTS = []
_WEIGHT_DTYPES = {'g_mix': _jnp.float32, 'w_in': _jnp.float32, 'g_v': _jnp.float32, 'w_s': _jnp.float32, 'b_s': _jnp.float32, 'w_pool': _jnp.float32, 'pool_scale': _jnp.float32, 'w_out': _jnp.float32, 'g_ffn': _jnp.float32, 'w_up': _jnp.float32, 'w_down': _jnp.float32, 'g_final': _jnp.float32}
MOMENT_SCALE = {'g_mix': 5.244859e-02, 'w_in': 4.283241e-02, 'g_v': 2.059339e-02, 'w_s': 3.860793e-02, 'b_s': 5.555563e-02, 'w_pool': 3.747354e-02, 'pool_scale': 7.878723e-02, 'w_out': 7.448395e-02, 'g_ffn': 8.012766e-02, 'w_up': 4.078914e-02, 'w_down': 1.422638e-01, 'g_final': 1.620207e+01}


def _to_microbatches(a, axis):
    t = _jnp.moveaxis(a, axis, 0)
    t = t.reshape((N_MICROBATCH, t.shape[0] // N_MICROBATCH) + t.shape[1:])
    return _jnp.moveaxis(t, 1, axis + 1)


def setup_inputs(seed: int = 0) -> dict:
    inp = _fwd_setup_inputs(seed)
    key = _jax.random.fold_in(_jax.random.key(seed), 7919)
    shape, _ = _output_shape()
    out = dict(inp)
    out["loss_target"] = _jax.random.normal(_jax.random.fold_in(key, 0), shape, _jnp.float32)
    for i, name in enumerate(TWIN_WEIGHTS):
        w = inp[name].astype(_jnp.float32)
        if MOMENT_SCALE is None:
            s = _jnp.sqrt(_jnp.mean(_jnp.square(w)) + 1e-30)
        else:
            s = MOMENT_SCALE[name]
        km, kv = _jax.random.split(_jax.random.fold_in(key, i + 1))
        out[name] = w
        out["m_" + name] = s * _jax.random.normal(km, w.shape, _jnp.float32)
        out["v_" + name] = (s * s) * _jax.random.uniform(kv, w.shape, _jnp.float32, 0.5, 1.5)
    if N_MICROBATCH > 1:
        for name, axis in PER_EXAMPLE_BATCH_AXIS.items():
            out[name] = _to_microbatches(out[name], axis)
    return {'x': out['x'], 'g_mix': out['g_mix'], 'w_in': out['w_in'], 'g_v': out['g_v'], 'w_s': out['w_s'], 'b_s': out['b_s'], 'w_pool': out['w_pool'], 'pool_scale': out['pool_scale'], 'w_out': out['w_out'], 'g_ffn': out['g_ffn'], 'w_up': out['w_up'], 'w_down': out['w_down'], 'g_final': out['g_final'], 'loss_target': out['loss_target'], 'm_g_mix': out['m_g_mix'], 'm_w_in': out['m_w_in'], 'm_g_v': out['m_g_v'], 'm_w_s': out['m_w_s'], 'm_b_s': out['m_b_s'], 'm_w_pool': out['m_w_pool'], 'm_pool_scale': out['m_pool_scale'], 'm_w_out': out['m_w_out'], 'm_g_ffn': out['m_g_ffn'], 'm_w_up': out['m_w_up'], 'm_w_down': out['m_w_down'], 'm_g_final': out['m_g_final'], 'v_g_mix': out['v_g_mix'], 'v_w_in': out['v_w_in'], 'v_g_v': out['v_g_v'], 'v_w_s': out['v_w_s'], 'v_b_s': out['v_b_s'], 'v_w_pool': out['v_w_pool'], 'v_pool_scale': out['v_pool_scale'], 'v_w_out': out['v_w_out'], 'v_g_ffn': out['v_g_ffn'], 'v_w_up': out['v_w_up'], 'v_w_down': out['v_w_down'], 'v_g_final': out['v_g_final']}


def _loss(weights, diff, rest, loss_target):
    with _jax.named_scope("forward"):
        args = {**rest, TWIN_DIFF_INPUT: diff, **{k: w.astype(_WEIGHT_DTYPES[k]) for k, w in weights.items()}}
        y = _forward(args)
    with _jax.named_scope("loss_head"):
        err = _jnp.square(y.astype(_jnp.float32) - loss_target)
        return 0.5 * _jnp.sum(_jnp.mean(err, axis=-1)) if err.ndim else 0.5 * err


def _adamw(w, g, m, v):
    m = ADAM_B1 * m + (1.0 - ADAM_B1) * g
    v = ADAM_B2 * v + (1.0 - ADAM_B2) * _jnp.square(g)
    m_hat = m / (1.0 - ADAM_B1 ** ADAM_STEP)
    v_hat = v / (1.0 - ADAM_B2 ** ADAM_STEP)
    delta = -ADAM_LR * (m_hat / (_jnp.sqrt(v_hat) + ADAM_EPS) + ADAM_WD * w)
    return delta, m, v


def reference(x, g_mix, w_in, g_v, w_s, b_s, w_pool, pool_scale, w_out, g_ffn, w_up, w_down, g_final, loss_target, m_g_mix, m_w_in, m_g_v, m_w_s, m_b_s, m_w_pool, m_pool_scale, m_w_out, m_g_ffn, m_w_up, m_w_down, m_g_final, v_g_mix, v_w_in, v_g_v, v_w_s, v_b_s, v_w_pool, v_pool_scale, v_w_out, v_g_ffn, v_w_up, v_w_down, v_g_final):
    given = dict(x=x, g_mix=g_mix, w_in=w_in, g_v=g_v, w_s=w_s, b_s=b_s, w_pool=w_pool, pool_scale=pool_scale, w_out=w_out, g_ffn=g_ffn, w_up=w_up, w_down=w_down, g_final=g_final, loss_target=loss_target, m_g_mix=m_g_mix, m_w_in=m_w_in, m_g_v=m_g_v, m_w_s=m_w_s, m_b_s=m_b_s, m_w_pool=m_w_pool, m_pool_scale=m_pool_scale, m_w_out=m_w_out, m_g_ffn=m_g_ffn, m_w_up=m_w_up, m_w_down=m_w_down, m_g_final=m_g_final, v_g_mix=v_g_mix, v_w_in=v_w_in, v_g_v=v_g_v, v_w_s=v_w_s, v_b_s=v_b_s, v_w_pool=v_w_pool, v_pool_scale=v_pool_scale, v_w_out=v_w_out, v_g_ffn=v_g_ffn, v_w_up=v_w_up, v_w_down=v_w_down, v_g_final=v_g_final)
    weights = {n: given[n] for n in TWIN_WEIGHTS}
    shared = {n: given[n] for n in SHARED_INPUTS}
    per_example = {n: given[n] for n in ['x']}
    grad_fn = _jax.value_and_grad(_loss, argnums=(0, 1))

    def one_microbatch(ex, loss_target):
        ex = dict(ex)
        diff = ex.pop(TWIN_DIFF_INPUT)
        return grad_fn(weights, diff, {**shared, **ex}, loss_target)

    if N_MICROBATCH == 1:
        loss, (grad_w, grad_x) = one_microbatch(per_example, given["loss_target"])
    else:
        def body(carry, xs):
            loss_sum, grad_sum = carry
            l_k, (gw_k, gx_k) = one_microbatch(xs[0], xs[1])
            with _jax.named_scope("update"):
                return (loss_sum + l_k, _jax.tree.map(_jnp.add, grad_sum, gw_k)), gx_k

        init = (_jnp.zeros((), _jnp.float32), _jax.tree.map(_jnp.zeros_like, weights))
        (loss, grad_w), grad_x = _jax.lax.scan(body, init, (per_example, given["loss_target"]))
    with _jax.named_scope("update"):
        delta_w, new_m, new_v = {}, {}, {}
        for n in TWIN_WEIGHTS:
            delta_w[n], new_m[n], new_v[n] = _adamw(weights[n], grad_w[n], given["m_" + n], given["v_" + n])
    return (loss, grad_x, *[grad_w[n] for n in TWIN_WEIGHTS], *[delta_w[n] for n in TWIN_WEIGHTS],
            *[new_m[n] for n in TWIN_WEIGHTS], *[new_v[n] for n in TWIN_WEIGHTS])
```

```python
import functools
import math

import jax
import jax.numpy as jnp
from jax import lax
from jax.experimental import pallas as pl
from jax.experimental.pallas import tpu as pltpu

F32 = jnp.float32
BF16 = jnp.bfloat16
MESH = pl.DeviceIdType.MESH

N_DEV = 8
EPS = 1e-6
CHUNK = 128
N_HEADS = 8
A_WIDTH = 1024
B_WIDTH = 1024
POOL_WINDOWS = (2, 4, 8, 16)
GROUP = 256
HALO = 16
LANES = 128

ADAM_LR = 0.001
ADAM_B1 = 0.9
ADAM_B2 = 0.999
ADAM_EPS = 1e-08
ADAM_WD = 0.01
ADAM_STEP = 10
ADAM_C1 = 1.0 - ADAM_B1 ** ADAM_STEP
ADAM_C2 = 1.0 - ADAM_B2 ** ADAM_STEP

VMEM_LIMIT = 56 * 1024 * 1024

_GELU_C = math.sqrt(2.0 / math.pi)


def _params(*sem):
    return pltpu.CompilerParams(dimension_semantics=sem, vmem_limit_bytes=VMEM_LIMIT)


def _gelu(x):
    return 0.5 * x * (1.0 + jnp.tanh(_GELU_C * (x + 0.044715 * x * x * x)))


def _gelu_and_grad(x):
    t = jnp.tanh(_GELU_C * (x + 0.044715 * x * x * x))
    g = 0.5 * x * (1.0 + t)
    dg = 0.5 * (1.0 + t) + 0.5 * x * (1.0 - t * t) * (_GELU_C * (1.0 + 3.0 * 0.044715 * x * x))
    return g, dg


def _dot_nn(a, b):
    return lax.dot_general(a, b, (((1,), (0,)), ((), ())), preferred_element_type=F32)


def _dot_nt(a, b):
    return lax.dot_general(a, b, (((1,), (1,)), ((), ())), preferred_element_type=F32)


def _dot_tn(a, b):
    return lax.dot_general(a, b, (((0,), (0,)), ((), ())), preferred_element_type=F32)


def _rms_rows(x):
    r = lax.rsqrt(jnp.mean(x * x, axis=-1, keepdims=True) + EPS)
    return x * r, r


def _rms_bwd_rows(dn, n, r):
    return r * (dn - n * jnp.mean(dn * n, axis=-1, keepdims=True))


def _tile(n, want):
    t = min(n, want)
    assert n % t == 0, (n, want)
    return t


def _proj_call(x, g_mix, w_in_g):
    t, d = x.shape
    nb, _, cb = w_in_g.shape
    tm = _tile(t, 1024)

    def body(x_ref, g_ref, w_ref, proj_ref, h_ref):
        @pl.when(pl.program_id(1) == 0)
        def _():
            n, _ = _rms_rows(x_ref[...])
            h_ref[...] = (n * g_ref[...]).astype(BF16)

        proj_ref[...] = _dot_nn(h_ref[...], w_ref[...])

    return pl.pallas_call(
        body, name="proj_fwd",
        grid=(t // tm, nb),
        in_specs=[pl.BlockSpec((tm, d), lambda i, j: (i, 0)),
                  pl.BlockSpec((1, d), lambda i, j: (0, 0)),
                  pl.BlockSpec((None, d, cb), lambda i, j: (j, 0, 0))],
        out_specs=[pl.BlockSpec((tm, cb), lambda i, j: (i, j)),
                   pl.BlockSpec((tm, d), lambda i, j: (i, 0))],
        out_shape=[jax.ShapeDtypeStruct((t, nb * cb), F32), jax.ShapeDtypeStruct((t, d), BF16)],
        compiler_params=_params("parallel", "arbitrary"),
    )(x, g_mix, w_in_g)


def _pool_counts(row0, rows, win):
    pos = row0 + lax.broadcasted_iota(jnp.int32, (rows, 1), 0)
    return jnp.minimum(pos + 1, win).astype(F32)


def _window_sum_back(ext, win):
    s = ext
    k = 1
    while k < win:
        s = s + pltpu.roll(s, k, 0)
        k *= 2
    return s


def _window_sum_fwd(ext, win):
    n = ext.shape[0]
    s = ext
    k = 1
    while k < win:
        s = s + pltpu.roll(s, n - k, 0)
        k *= 2
    return s


def _mixer_fwd_call(proj, w_s, bs_t, g_v, w_pool_g, pool_scale):
    t = proj.shape[0]
    tt = _tile(t, 512)
    nchunk = tt // CHUNK
    hb = tt // HALO

    def body(pu_ref, pv_ref, z_ref, zp_ref, ws_ref, bs_ref, gv_ref, wp_ref, ps_ref, out_ref):
        i = pl.program_id(0)
        tril = (lax.broadcasted_iota(jnp.int32, (CHUNK, CHUNK), 0)
                >= lax.broadcasted_iota(jnp.int32, (CHUNK, CHUNK), 1))
        for h in range(N_HEADS):
            cols = slice(h * CHUNK, (h + 1) * CHUNK)
            vhat, _ = _rms_rows(_gelu(pv_ref[:, cols]))
            vn = (vhat * gv_ref[:, cols]).astype(BF16)
            u = _gelu(pu_ref[:, cols])
            w = jnp.where(tril, ws_ref[h], 0.0).astype(BF16)
            bcol = bs_ref[:, h:h + 1]
            for c in range(nchunk):
                rows = slice(c * CHUNK, (c + 1) * CHUNK)
                mixed = _dot_nn(w, vn[rows]) + bcol
                out_ref[rows, cols] = (u[rows] * mixed).astype(BF16)

        zprev = jnp.where(i > 0, zp_ref[...], 0.0)
        ext = jnp.concatenate([zprev, z_ref[...]], axis=0)
        for g, win in enumerate(POOL_WINDOWS):
            cols = slice(g * GROUP, (g + 1) * GROUP)
            zg = ext[:, cols]
            s = _window_sum_back(zg, win)
            pooled = s[HALO:] / _pool_counts(i * tt, tt, win) - zg[HALO:]
            wp = wp_ref[:, g].reshape(GROUP, GROUP)
            y = _dot_nn(pooled.astype(BF16), wp)
            out_ref[:, A_WIDTH + g * GROUP:A_WIDTH + (g + 1) * GROUP] = (y * ps_ref[:, cols]).astype(BF16)

    return pl.pallas_call(
        body, name="mixer_fwd",
        grid=(t // tt,),
        in_specs=[pl.BlockSpec((tt, A_WIDTH), lambda i: (i, 0)),
                  pl.BlockSpec((tt, A_WIDTH), lambda i: (i, 1)),
                  pl.BlockSpec((tt, B_WIDTH), lambda i: (i, 2)),
                  pl.BlockSpec((HALO, B_WIDTH), lambda i: (jnp.maximum(i * hb - 1, 0), 2)),
                  pl.BlockSpec((N_HEADS, CHUNK, CHUNK), lambda i: (0, 0, 0)),
                  pl.BlockSpec((CHUNK, N_HEADS), lambda i: (0, 0)),
                  pl.BlockSpec((1, A_WIDTH), lambda i: (0, 0)),
                  pl.BlockSpec((N_DEV, 4, GROUP // N_DEV, GROUP), lambda i: (0, 0, 0, 0)),
                  pl.BlockSpec((1, B_WIDTH), lambda i: (0, 0))],
        out_specs=pl.BlockSpec((tt, A_WIDTH + B_WIDTH), lambda i: (i, 0)),
        out_shape=jax.ShapeDtypeStruct((t, A_WIDTH + B_WIDTH), BF16),
        compiler_params=_params("parallel"),
    )(proj, proj, proj, proj, w_s, bs_t, g_v, w_pool_g, pool_scale)


def _out_proj_call(mixed, w_out, x):
    t, d = x.shape
    k = mixed.shape[1]
    tm = _tile(t, 1024)
    tn = _tile(d, 1024)

    def body(a_ref, w_ref, x_ref, o_ref):
        o_ref[...] = x_ref[...] + _dot_nn(a_ref[...], w_ref[...])

    return pl.pallas_call(
        body, name="out_proj_fwd",
        grid=(t // tm, d // tn),
        in_specs=[pl.BlockSpec((tm, k), lambda i, j: (i, 0)),
                  pl.BlockSpec((k, tn), lambda i, j: (0, j)),
                  pl.BlockSpec((tm, tn), lambda i, j: (i, j))],
        out_specs=pl.BlockSpec((tm, tn), lambda i, j: (i, j)),
        out_shape=jax.ShapeDtypeStruct((t, d), F32),
        compiler_params=_params("parallel", "parallel"),
    )(mixed, w_out, x)


def _up_call(x2, g_ffn, w_up_g):
    t, d = x2.shape
    nb, _, fb = w_up_g.shape
    tm = _tile(t, 1024)
    tn = _tile(fb, 512)
    per = fb // tn

    def body(x_ref, g_ref, w_ref, act_ref, h_ref):
        @pl.when(pl.program_id(1) == 0)
        def _():
            n, _ = _rms_rows(x_ref[...])
            h_ref[...] = (n * g_ref[...]).astype(BF16)

        a = jnp.maximum(_dot_nn(h_ref[...], w_ref[...]), 0.0)
        act_ref[...] = (a * a).astype(BF16)

    return pl.pallas_call(
        body, name="up_fwd",
        grid=(t // tm, nb * per),
        in_specs=[pl.BlockSpec((tm, d), lambda i, j: (i, 0)),
                  pl.BlockSpec((1, d), lambda i, j: (0, 0)),
                  pl.BlockSpec((None, d, tn), lambda i, j: (j // per, 0, j % per))],
        out_specs=[pl.BlockSpec((tm, tn), lambda i, j: (i, j)),
                   pl.BlockSpec((tm, d), lambda i, j: (i, 0))],
        out_shape=[jax.ShapeDtypeStruct((t, nb * fb), BF16), jax.ShapeDtypeStruct((t, d), BF16)],
        compiler_params=_params("parallel", "arbitrary"),
    )(x2, g_ffn, w_up_g)


def _down_call(act, w_down, x2):
    t, f = act.shape
    d = x2.shape[1]
    tm = _tile(t, 1024)
    tn = _tile(d, 1024)
    tk = _tile(f, 1024)

    def body(a_ref, w_ref, x_ref, o_ref):
        @pl.when(pl.program_id(2) == 0)
        def _():
            o_ref[...] = x_ref[...]

        o_ref[...] += _dot_nn(a_ref[...], w_ref[...])

    return pl.pallas_call(
        body, name="down_fwd",
        grid=(t // tm, d // tn, f // tk),
        in_specs=[pl.BlockSpec((tm, tk), lambda i, j, k: (i, k)),
                  pl.BlockSpec((tk, tn), lambda i, j, k: (k, j)),
                  pl.BlockSpec((tm, tn), lambda i, j, k: (i, j))],
        out_specs=pl.BlockSpec((tm, tn), lambda i, j, k: (i, j)),
        out_shape=jax.ShapeDtypeStruct((t, d), F32),
        compiler_params=_params("parallel", "parallel", "arbitrary"),
    )(act, w_down, x2)


def _loss_call(x3, target, g_final):
    t, d = x3.shape
    tr = _tile(t, 256)

    def body(x_ref, tg_ref, g_ref, loss_ref, dx_ref, dxb_ref, dg_ref):
        @pl.when(pl.program_id(0) == 0)
        def _():
            loss_ref[...] = jnp.zeros_like(loss_ref)
            dg_ref[...] = jnp.zeros_like(dg_ref)

        n, r = _rms_rows(x_ref[...])
        err = n * g_ref[...] - tg_ref[...]
        loss_ref[...] += 0.5 * jnp.sum(jnp.mean(err * err, axis=-1, keepdims=True))
        dy = err * (1.0 / d)
        dg_ref[...] += jnp.sum(dy * n, axis=0, keepdims=True)
        dx = _rms_bwd_rows(dy * g_ref[...], n, r)
        dx_ref[...] = dx
        dxb_ref[...] = dx.astype(BF16)

    return pl.pallas_call(
        body, name="loss_head",
        grid=(t // tr,),
        in_specs=[pl.BlockSpec((tr, d), lambda i: (i, 0)),
                  pl.BlockSpec((tr, d), lambda i: (i, 0)),
                  pl.BlockSpec((1, d), lambda i: (0, 0))],
        out_specs=[pl.BlockSpec((8, LANES), lambda i: (0, 0)),
                   pl.BlockSpec((tr, d), lambda i: (i, 0)),
                   pl.BlockSpec((tr, d), lambda i: (i, 0)),
                   pl.BlockSpec((1, d), lambda i: (0, 0))],
        out_shape=[jax.ShapeDtypeStruct((8, LANES), F32), jax.ShapeDtypeStruct((t, d), F32),
                   jax.ShapeDtypeStruct((t, d), BF16), jax.ShapeDtypeStruct((1, d), F32)],
        compiler_params=_params("arbitrary"),
    )(x3, target, g_final)


def _norm_bwd_call(name, dh, x, dres, g, want_bf16):
    t, d = x.shape
    tr = _tile(t, 256)

    def body(dh_ref, x_ref, dres_ref, g_ref, dx_ref, *rest):
        dg_ref = rest[-1]

        @pl.when(pl.program_id(0) == 0)
        def _():
            dg_ref[...] = jnp.zeros_like(dg_ref)

        n, r = _rms_rows(x_ref[...])
        dh = dh_ref[...]
        dg_ref[...] += jnp.sum(dh * n, axis=0, keepdims=True)
        dx = dres_ref[...] + _rms_bwd_rows(dh * g_ref[...], n, r)
        dx_ref[...] = dx
        if want_bf16:
            rest[0][...] = dx.astype(BF16)

    row = pl.BlockSpec((tr, d), lambda i: (i, 0))
    vec = pl.BlockSpec((1, d), lambda i: (0, 0))
    out_specs = [row] + ([row] if want_bf16 else []) + [vec]
    out_shape = ([jax.ShapeDtypeStruct((t, d), F32)]
                 + ([jax.ShapeDtypeStruct((t, d), BF16)] if want_bf16 else [])
                 + [jax.ShapeDtypeStruct((1, d), F32)])
    return pl.pallas_call(
        body, name=name,
        grid=(t // tr,),
        in_specs=[row, row, row, vec],
        out_specs=out_specs, out_shape=out_shape,
        compiler_params=_params("arbitrary"),
    )(dh, x, dres, g)


def _dact_call(dx3b, w_down, act):
    t, d = dx3b.shape
    f = w_down.shape[0]
    tm = _tile(t, 1024)
    tn = _tile(f, 512)

    def body(g_ref, w_ref, act_ref, o_ref):
        dact = _dot_nt(g_ref[...], w_ref[...])
        o_ref[...] = (dact * (2.0 * jnp.sqrt(act_ref[...].astype(F32)))).astype(BF16)

    return pl.pallas_call(
        body, name="dact_bwd",
        grid=(t // tm, f // tn),
        in_specs=[pl.BlockSpec((tm, d), lambda i, j: (i, 0)),
                  pl.BlockSpec((tn, d), lambda i, j: (j, 0)),
                  pl.BlockSpec((tm, tn), lambda i, j: (i, j))],
        out_specs=pl.BlockSpec((tm, tn), lambda i, j: (i, j)),
        out_shape=jax.ShapeDtypeStruct((t, f), BF16),
        compiler_params=_params("parallel", "parallel"),
    )(dx3b, w_down, act)


def _wgrad_call(name, a, b, out_blocks, out_block_cols):
    t, k1 = a.shape
    k2 = b.shape[1]
    tt = _tile(t, 1024)
    t1 = _tile(k1, 1024)
    t2 = _tile(k2 if out_blocks is None else out_block_cols, 1024)
    nk = t // tt

    def body(a_ref, b_ref, o_ref, acc_ref):
        k = pl.program_id(2)

        @pl.when(k == 0)
        def _():
            acc_ref[...] = jnp.zeros_like(acc_ref)

        acc_ref[...] += _dot_tn(a_ref[...], b_ref[...])

        @pl.when(k == nk - 1)
        def _():
            o_ref[...] = acc_ref[...].astype(BF16)

    if out_blocks is None:
        out_spec = pl.BlockSpec((t1, t2), lambda i, j, k: (i, j))
        out_shape = jax.ShapeDtypeStruct((k1, k2), BF16)
    else:
        per = out_block_cols // t2
        out_spec = pl.BlockSpec((None, t1, t2), lambda i, j, k: (j // per, i, j % per))
        out_shape = jax.ShapeDtypeStruct((out_blocks, k1, out_block_cols), BF16)
    return pl.pallas_call(
        body, name=name,
        grid=(k1 // t1, k2 // t2, nk),
        in_specs=[pl.BlockSpec((tt, t1), lambda i, j, k: (k, i)),
                  pl.BlockSpec((tt, t2), lambda i, j, k: (k, j))],
        out_specs=out_spec, out_shape=out_shape,
        scratch_shapes=[pltpu.VMEM((t1, t2), F32)],
        compiler_params=_params("parallel", "parallel", "arbitrary"),
    )(a, b)


def _dgrad_blocked_call(name, g, w_g):
    t = g.shape[0]
    nb, d, cb = w_g.shape
    tm = _tile(t, 1024)
    tn = _tile(d, 1024)
    tk = _tile(cb, 1024)
    per = cb // tk

    def body(g_ref, w_ref, o_ref):
        @pl.when(pl.program_id(2) == 0)
        def _():
            o_ref[...] = jnp.zeros_like(o_ref)

        o_ref[...] += _dot_nt(g_ref[...], w_ref[...])

    return pl.pallas_call(
        body, name=name,
        grid=(t // tm, d // tn, nb * per),
        in_specs=[pl.BlockSpec((tm, tk), lambda i, j, k: (i, k)),
                  pl.BlockSpec((None, tn, tk), lambda i, j, k: (k // per, j, k % per))],
        out_specs=pl.BlockSpec((tm, tn), lambda i, j, k: (i, j)),
        out_shape=jax.ShapeDtypeStruct((t, d), F32),
        compiler_params=_params("parallel", "parallel", "arbitrary"),
    )(g, w_g)


def _dmixed_call(dx2b, w_out):
    t, d = dx2b.shape
    e = w_out.shape[0]
    tm = _tile(t, 1024)
    tn = _tile(e, 1024)

    def body(g_ref, w_ref, o_ref):
        o_ref[...] = _dot_nt(g_ref[...], w_ref[...])

    return pl.pallas_call(
        body, name="dmixed_bwd",
        grid=(t // tm, e // tn),
        in_specs=[pl.BlockSpec((tm, d), lambda i, j: (i, 0)),
                  pl.BlockSpec((tn, d), lambda i, j: (j, 0))],
        out_specs=pl.BlockSpec((tm, tn), lambda i, j: (i, j)),
        out_shape=jax.ShapeDtypeStruct((t, e), F32),
        compiler_params=_params("parallel", "parallel"),
    )(dx2b, w_out)


def _mixer_bwd_call(proj, dmixed, w_s, bs_t, g_v, w_pool_g, pool_scale):
    t = proj.shape[0]
    tt = _tile(t, 512)
    nchunk = tt // CHUNK
    hb = tt // HALO
    last_halo = t // HALO - 1
    nsteps = t // tt
    rb = GROUP // N_DEV

    def body(pu_ref, pv_ref, z_ref, zp_ref, da_ref, db_ref, dbn_ref, ws_ref, bs_ref, gv_ref, wp_ref, ps_ref,
             dproj_ref, dws_ref, dbs_ref, dgv_ref, dps_ref, dwp_ref):
        i = pl.program_id(0)

        @pl.when(i == 0)
        def _():
            dws_ref[...] = jnp.zeros_like(dws_ref)
            dbs_ref[...] = jnp.zeros_like(dbs_ref)
            dgv_ref[...] = jnp.zeros_like(dgv_ref)
            dps_ref[...] = jnp.zeros_like(dps_ref)
            dwp_ref[...] = jnp.zeros_like(dwp_ref)

        tril = (lax.broadcasted_iota(jnp.int32, (CHUNK, CHUNK), 0)
                >= lax.broadcasted_iota(jnp.int32, (CHUNK, CHUNK), 1))
        for h in range(N_HEADS):
            cols = slice(h * CHUNK, (h + 1) * CHUNK)
            v, dv_dpv = _gelu_and_grad(pv_ref[:, cols])
            vhat, rv = _rms_rows(v)
            gv = gv_ref[:, cols]
            vn = (vhat * gv).astype(BF16)
            u, du_dpu = _gelu_and_grad(pu_ref[:, cols])
            w = jnp.where(tril, ws_ref[h], 0.0).astype(BF16)
            bcol = bs_ref[:, h:h + 1]
            dout = da_ref[:, cols]
            dmix = dout * u
            dmix_b = dmix.astype(BF16)
            dws = jnp.zeros((CHUNK, CHUNK), F32)
            dbs = jnp.zeros((CHUNK, 1), F32)
            dvn_parts = []
            du_parts = []
            for c in range(nchunk):
                rows = slice(c * CHUNK, (c + 1) * CHUNK)
                mixed = _dot_nn(w, vn[rows]) + bcol
                du_parts.append(dout[rows] * mixed)
                dvn_parts.append(_dot_tn(w, dmix_b[rows]))
                dws = dws + _dot_nt(dmix_b[rows], vn[rows])
                dbs = dbs + jnp.sum(dmix[rows], axis=1, keepdims=True)
            dws_ref[h] += jnp.where(tril, dws, 0.0)
            dbs_ref[:, h:h + 1] += dbs
            dvn = jnp.concatenate(dvn_parts, axis=0)
            du = jnp.concatenate(du_parts, axis=0)
            dgv_ref[:, cols] += jnp.sum(dvn * vhat, axis=0, keepdims=True)
            dv = _rms_bwd_rows(dvn * gv, vhat, rv)
            dproj_ref[:, cols] = (du * du_dpu).astype(BF16)
            dproj_ref[:, A_WIDTH + h * CHUNK:A_WIDTH + (h + 1) * CHUNK] = (dv * dv_dpv).astype(BF16)

        zprev = jnp.where(i > 0, zp_ref[...], 0.0)
        ext = jnp.concatenate([zprev, z_ref[...]], axis=0)
        dnext = jnp.where(i < nsteps - 1, dbn_ref[...], 0.0)
        dext = jnp.concatenate([db_ref[...], dnext], axis=0)
        for g, win in enumerate(POOL_WINDOWS):
            cols = slice(g * GROUP, (g + 1) * GROUP)
            zg = ext[:, cols]
            pooled = _window_sum_back(zg, win)[HALO:] / _pool_counts(i * tt, tt, win) - zg[HALO:]
            pooled_b = pooled.astype(BF16)
            wp = wp_ref[:, g].reshape(GROUP, GROUP)
            y = _dot_nn(pooled_b, wp)
            dout = dext[:, cols]
            dps_ref[:, cols] += jnp.sum(dout[:tt] * y, axis=0, keepdims=True)
            dy_b = (dout * ps_ref[:, cols]).astype(BF16)
            dwp_ref[:, g] += _dot_tn(pooled_b, dy_b[:tt]).reshape(N_DEV, rb, GROUP)
            dpooled = _dot_nt(dy_b, wp)
            q = dpooled / _pool_counts(i * tt, tt + HALO, win)
            dz = _window_sum_fwd(q, win)[:tt] - dpooled[:tt]
            dproj_ref[:, 2 * A_WIDTH + g * GROUP:2 * A_WIDTH + (g + 1) * GROUP] = dz.astype(BF16)

    def full(shape):
        return pl.BlockSpec(shape, lambda i: (0,) * len(shape))

    return pl.pallas_call(
        body, name="mixer_bwd",
        grid=(nsteps,),
        in_specs=[pl.BlockSpec((tt, A_WIDTH), lambda i: (i, 0)),
                  pl.BlockSpec((tt, A_WIDTH), lambda i: (i, 1)),
                  pl.BlockSpec((tt, B_WIDTH), lambda i: (i, 2)),
                  pl.BlockSpec((HALO, B_WIDTH), lambda i: (jnp.maximum(i * hb - 1, 0), 2)),
                  pl.BlockSpec((tt, A_WIDTH), lambda i: (i, 0)),
                  pl.BlockSpec((tt, B_WIDTH), lambda i: (i, 1)),
                  pl.BlockSpec((HALO, B_WIDTH), lambda i: (jnp.minimum((i + 1) * hb, last_halo), 1)),
                  full((N_HEADS, CHUNK, CHUNK)), full((CHUNK, N_HEADS)), full((1, A_WIDTH)),
                  full((N_DEV, 4, rb, GROUP)), full((1, B_WIDTH))],
        out_specs=[pl.BlockSpec((tt, 2 * A_WIDTH + B_WIDTH), lambda i: (i, 0)),
                   full((N_HEADS, CHUNK, CHUNK)), full((CHUNK, N_HEADS)), full((1, A_WIDTH)),
                   full((1, B_WIDTH)), full((N_DEV, 4, rb, GROUP))],
        out_shape=[jax.ShapeDtypeStruct((t, 2 * A_WIDTH + B_WIDTH), BF16),
                   jax.ShapeDtypeStruct((N_HEADS, CHUNK, CHUNK), F32),
                   jax.ShapeDtypeStruct((CHUNK, N_HEADS), F32),
                   jax.ShapeDtypeStruct((1, A_WIDTH), F32),
                   jax.ShapeDtypeStruct((1, B_WIDTH), F32),
                   jax.ShapeDtypeStruct((N_DEV, 4, rb, GROUP), F32)],
        compiler_params=_params("arbitrary"),
    )(proj, proj, proj, proj, dmixed, dmixed, dmixed, w_s, bs_t, g_v, w_pool_g, pool_scale)


def _position():
    return lax.axis_index("x"), lax.axis_index("y"), lax.axis_index("c")


_ANY = pl.BlockSpec(memory_space=pl.ANY)


def _all_gather_call(shards):
    n = len(shards)

    def body(*refs):
        ins, outs = refs[:n], refs[n:2 * n]
        send_sems, recv_sems, local_sems = refs[2 * n:]
        x, y, c = _position()
        me, sibling = (x, y, c), (x, y, 1 - c)
        chips = [(1 - x, y), (x, 1 - y), (1 - x, 1 - y)]

        def slot(out, px, py, pc):
            return out.at[4 * px + 2 * py + pc]

        def copy(w, k, block, to, src=None):
            dst = slot(outs[w], *block)
            return pltpu.make_async_remote_copy(
                src_ref=dst if src is None else src, dst_ref=dst,
                send_sem=send_sems.at[w, k], recv_sem=recv_sems.at[w, k],
                device_id=to, device_id_type=MESH)

        mine = [pltpu.make_async_copy(ins[w], slot(outs[w], *me), local_sems.at[w]) for w in range(n)]
        for cp in mine:
            cp.start()
        first = []
        for w in range(n):
            first.append(copy(w, 0, me, sibling, src=ins[w]))
            first += [copy(w, 1 + j, me, (*chip, c), src=ins[w]) for j, chip in enumerate(chips)]
        for cp in first:
            cp.start()
        passed = []
        for w in range(n):
            for j, chip in enumerate(chips):
                copy(w, 1 + j, (*chip, c), me).wait_recv()
                fwd = copy(w, 4 + j, (*chip, c), sibling)
                fwd.start()
                passed.append(fwd)
        for w in range(n):
            copy(w, 0, sibling, me).wait_recv()
            for j, chip in enumerate(chips):
                copy(w, 4 + j, (*chip, 1 - c), me).wait_recv()
        for cp in first + passed:
            cp.wait_send()
        for cp in mine:
            cp.wait()

    return pl.pallas_call(
        body, name="weights_all_gather",
        in_specs=[_ANY] * n, out_specs=[_ANY] * n,
        out_shape=[jax.ShapeDtypeStruct((N_DEV,) + s.shape, s.dtype) for s in shards],
        scratch_shapes=[pltpu.SemaphoreType.DMA((n, 7)), pltpu.SemaphoreType.DMA((n, 7)),
                        pltpu.SemaphoreType.DMA((n,))],
    )(*shards)


def _pair_exchange_call(grads):
    n = len(grads)

    def body(*refs):
        ins, outs = refs[:n], refs[n:2 * n]
        send_sems, recv_sems = refs[2 * n:]
        x, y, c = _position()
        copies = []
        for w in range(n):
            for chip in range(4):
                copies.append(pltpu.make_async_remote_copy(
                    src_ref=ins[w].at[2 * chip + (1 - c)], dst_ref=outs[w].at[chip],
                    send_sem=send_sems.at[w, chip], recv_sem=recv_sems.at[w, chip],
                    device_id=(x, y, 1 - c), device_id_type=MESH))
        for cp in copies:
            cp.start()
        for cp in copies:
            cp.wait()

    return pl.pallas_call(
        body, name="grads_pair_exchange",
        in_specs=[_ANY] * n, out_specs=[_ANY] * n,
        out_shape=[jax.ShapeDtypeStruct((4,) + g.shape[1:], g.dtype) for g in grads],
        scratch_shapes=[pltpu.SemaphoreType.DMA((n, 4)), pltpu.SemaphoreType.DMA((n, 4))],
    )(*grads)


def _chip_exchange_call(sums):
    n = len(sums)

    def body(*refs):
        ins, outs = refs[:n], refs[n:2 * n]
        send_sems, recv_sems = refs[2 * n:]
        x, y, c = _position()
        chips = [(1 - x, y), (x, 1 - y), (1 - x, 1 - y)]
        copies = []
        for w in range(n):
            for j, chip in enumerate(chips):
                copies.append(pltpu.make_async_remote_copy(
                    src_ref=ins[w].at[j], dst_ref=outs[w].at[j],
                    send_sem=send_sems.at[w, j], recv_sem=recv_sems.at[w, j],
                    device_id=(*chip, c), device_id_type=MESH))
        for cp in copies:
            cp.start()
        for cp in copies:
            cp.wait()

    return pl.pallas_call(
        body, name="grads_chip_exchange",
        in_specs=[_ANY] * n, out_specs=[_ANY] * n,
        out_shape=[jax.ShapeDtypeStruct(s.shape, s.dtype) for s in sums],
        scratch_shapes=[pltpu.SemaphoreType.DMA((n, 3)), pltpu.SemaphoreType.DMA((n, 3))],
    )(*sums)


def _small_all_gather_call(packed):
    def body(in_ref, out_ref, send_sems, recv_sems, local_sem):
        x, y, c = _position()
        me = 4 * x + 2 * y + c
        mine = pltpu.make_async_copy(in_ref, out_ref.at[me], local_sem)
        mine.start()
        copies = []
        for k in range(1, N_DEV):
            px, py, pc = x ^ (k >> 2), y ^ ((k >> 1) & 1), c ^ (k & 1)
            copies.append(pltpu.make_async_remote_copy(
                src_ref=in_ref, dst_ref=out_ref.at[me],
                send_sem=send_sems.at[k - 1], recv_sem=recv_sems.at[k - 1],
                device_id=(px, py, pc), device_id_type=MESH))
        for cp in copies:
            cp.start()
        for k in range(1, N_DEV):
            peer = 4 * (x ^ (k >> 2)) + 2 * (y ^ ((k >> 1) & 1)) + (c ^ (k & 1))
            pltpu.make_async_remote_copy(
                src_ref=in_ref, dst_ref=out_ref.at[peer],
                send_sem=send_sems.at[k - 1], recv_sem=recv_sems.at[k - 1],
                device_id=(x, y, c), device_id_type=MESH).wait_recv()
        for cp in copies:
            cp.wait_send()
        mine.wait()

    return pl.pallas_call(
        body, name="small_grads_all_gather",
        in_specs=[_ANY], out_specs=_ANY,
        out_shape=jax.ShapeDtypeStruct((N_DEV,) + packed.shape, packed.dtype),
        scratch_shapes=[pltpu.SemaphoreType.DMA((N_DEV - 1,)), pltpu.SemaphoreType.DMA((N_DEV - 1,)),
                        pltpu.SemaphoreType.DMA],
    )(packed)


def _adamw(w, g, m, v):
    m = ADAM_B1 * m + (1.0 - ADAM_B1) * g
    v = ADAM_B2 * v + (1.0 - ADAM_B2) * (g * g)
    m_hat = m / ADAM_C1
    v_hat = v / ADAM_C2
    delta = -ADAM_LR * (m_hat / (jnp.sqrt(v_hat) + ADAM_EPS) + ADAM_WD * w)
    return delta, m, v


ROW_TILE_ELEMS = 256 * 1024


def _row_tile(r, c):
    t = r
    while t * c > ROW_TILE_ELEMS and t % 32 == 0:
        t //= 2
    return t


def _pair_sum_call(name, pos, grad, got):
    _, r, c = grad.shape
    tr = _row_tile(r, c)

    def chip_of(rel, pos_ref):
        px = jnp.where((rel == 0) | (rel == 2), 1 - pos_ref[0], pos_ref[0])
        py = jnp.where((rel == 1) | (rel == 2), 1 - pos_ref[1], pos_ref[1])
        return 2 * px + py

    def own_body(pos_ref, own_ref, got_ref, mine_ref):
        mine_ref[...] = own_ref[...].astype(F32) + got_ref[...].astype(F32)

    mine = pl.pallas_call(
        own_body, name=name + "_own",
        grid_spec=pltpu.PrefetchScalarGridSpec(
            num_scalar_prefetch=1, grid=(r // tr,),
            in_specs=[pl.BlockSpec((None, tr, c), lambda i, p: (4 * p[0] + 2 * p[1] + p[2], i, 0)),
                      pl.BlockSpec((None, tr, c), lambda i, p: (2 * p[0] + p[1], i, 0))],
            out_specs=pl.BlockSpec((tr, c), lambda i, p: (i, 0))),
        out_shape=jax.ShapeDtypeStruct((r, c), F32),
        compiler_params=_params("parallel"),
    )(pos, grad, got)

    def others_body(pos_ref, own_ref, got_ref, out_ref):
        out_ref[...] = (own_ref[...].astype(F32) + got_ref[...].astype(F32)).astype(BF16)

    out = pl.pallas_call(
        others_body, name=name + "_others",
        grid_spec=pltpu.PrefetchScalarGridSpec(
            num_scalar_prefetch=1, grid=(3, r // tr),
            in_specs=[pl.BlockSpec((None, tr, c), lambda k, i, p: (2 * chip_of(k, p) + p[2], i, 0)),
                      pl.BlockSpec((None, tr, c), lambda k, i, p: (chip_of(k, p), i, 0))],
            out_specs=pl.BlockSpec((None, tr, c), lambda k, i, p: (k, i, 0))),
        out_shape=jax.ShapeDtypeStruct((3, r, c), BF16),
        compiler_params=_params("parallel", "parallel"),
    )(pos, grad, got)
    return mine, out


def _final_call(name, mine, got, w, m, v):
    r, c = mine.shape
    tr = _row_tile(r, c)

    def body(mine_ref, got_ref, w_ref, m_ref, v_ref, g_out, d_out, m_out, v_out):
        g = mine_ref[...]
        for j in range(3):
            g = g + got_ref[j].astype(F32)
        delta, m_new, v_new = _adamw(w_ref[...], g, m_ref[...], v_ref[...])
        g_out[...] = g
        d_out[...] = delta
        m_out[...] = m_new
        v_out[...] = v_new

    row = pl.BlockSpec((tr, c), lambda i: (i, 0))
    return pl.pallas_call(
        body, name=name,
        grid=(r // tr,),
        in_specs=[row, pl.BlockSpec((3, tr, c), lambda i: (0, i, 0)), row, row, row],
        out_specs=[row] * 4,
        out_shape=[jax.ShapeDtypeStruct((r, c), F32)] * 4,
        compiler_params=_params("parallel"),
    )(mine, got, w, m, v)


def _small_final_call(parts, w, m, v):
    _, r, c = parts.shape

    def body(p_ref, w_ref, m_ref, v_ref, g_out, d_out, m_out, v_out):
        g = p_ref[0]
        for k in range(1, N_DEV):
            g = g + p_ref[k]
        delta, m_new, v_new = _adamw(w_ref[...], g, m_ref[...], v_ref[...])
        g_out[...] = g
        d_out[...] = delta
        m_out[...] = m_new
        v_out[...] = v_new

    return pl.pallas_call(
        body, name="small_final",
        out_shape=[jax.ShapeDtypeStruct((r, c), F32)] * 4,
        compiler_params=pltpu.CompilerParams(vmem_limit_bytes=VMEM_LIMIT),
    )(parts, w, m, v)


_SMALL = ("g_mix", "g_v", "w_s", "b_s", "pool_scale", "g_ffn", "g_final")
_BIG = ("w_in", "w_pool", "w_out", "w_up", "w_down")
_ORDER = ("g_mix", "w_in", "g_v", "w_s", "b_s", "w_pool", "pool_scale", "w_out", "g_ffn", "w_up", "w_down", "g_final")


def _pack(parts):
    return jnp.concatenate([p.reshape(-1, LANES) for p in parts], axis=0)


def _unpack(packed, like):
    out, row = [], 0
    for a in like:
        rows = a.size // LANES
        out.append(packed[row:row + rows].reshape(a.shape))
        row += rows
    return out


def kernel(x, g_mix, w_in, g_v, w_s, b_s, w_pool, pool_scale, w_out, g_ffn, w_up, w_down, g_final, loss_target, m_g_mix, m_w_in, m_g_v, m_w_s, m_b_s, m_w_pool, m_pool_scale, m_w_out, m_g_ffn, m_w_up, m_w_down, m_g_final, v_g_mix, v_w_in, v_g_v, v_w_s, v_b_s, v_w_pool, v_pool_scale, v_w_out, v_g_ffn, v_w_up, v_w_down, v_g_final):
    weights = dict(g_mix=g_mix, w_in=w_in, g_v=g_v, w_s=w_s, b_s=b_s, w_pool=w_pool, pool_scale=pool_scale,
                   w_out=w_out, g_ffn=g_ffn, w_up=w_up, w_down=w_down, g_final=g_final)
    mom = dict(g_mix=m_g_mix, w_in=m_w_in, g_v=m_g_v, w_s=m_w_s, b_s=m_b_s, w_pool=m_w_pool,
               pool_scale=m_pool_scale, w_out=m_w_out, g_ffn=m_g_ffn, w_up=m_w_up, w_down=m_w_down,
               g_final=m_g_final)
    var = dict(g_mix=v_g_mix, w_in=v_w_in, g_v=v_g_v, w_s=v_w_s, b_s=v_b_s, w_pool=v_w_pool,
               pool_scale=v_pool_scale, w_out=v_w_out, g_ffn=v_g_ffn, w_up=v_w_up, w_down=v_w_down,
               g_final=v_g_final)

    t, d = x.shape[1], x.shape[2]
    xs = x.reshape(t, d)
    target = loss_target.reshape(t, d)

    shard2d = dict(w_in=w_in.reshape(d, -1), w_pool=w_pool.reshape(-1, GROUP), w_out=w_out.reshape(-1, d),
                   w_up=w_up.reshape(d, -1), w_down=w_down.reshape(-1, d))
    gathered = _all_gather_call([shard2d[k].astype(BF16) for k in _BIG])
    w_in_g, w_pool_g, w_out_g, w_up_g, w_down_g = gathered
    w_pool_g = w_pool_g.reshape(N_DEV, 4, GROUP // N_DEV, GROUP)
    w_out_f = w_out_g.reshape(-1, d)
    w_down_f = w_down_g.reshape(-1, d)

    g_mix2, g_ffn2, g_final2 = g_mix.reshape(1, d), g_ffn.reshape(1, d), g_final.reshape(1, d)
    g_v2, ps2 = g_v.reshape(1, A_WIDTH), pool_scale.reshape(1, B_WIDTH)
    w_s3 = w_s.reshape(N_HEADS, CHUNK, CHUNK)
    bs_t = b_s.reshape(N_HEADS, CHUNK).T

    proj, h1 = _proj_call(xs, g_mix2, w_in_g)
    mixed = _mixer_fwd_call(proj, w_s3, bs_t, g_v2, w_pool_g, ps2)
    x2 = _out_proj_call(mixed, w_out_f, xs)
    act, h2 = _up_call(x2, g_ffn2, w_up_g)
    x3 = _down_call(act, w_down_f, x2)
    loss_part, dx3, dx3b, dg_final = _loss_call(x3, target, g_final2)

    da = _dact_call(dx3b, w_down_f, act)
    gw_down = _wgrad_call("w_down_grad", act, dx3b, None, None)
    gw_up = _wgrad_call("w_up_grad", h2, da, N_DEV, w_up_g.shape[2])
    dh2 = _dgrad_blocked_call("dh2_bwd", da, w_up_g)
    dx2, dx2b, dg_ffn = _norm_bwd_call("ffn_norm_bwd", dh2, x2, dx3, g_ffn2, True)
    dmixed = _dmixed_call(dx2b, w_out_f)
    gw_out = _wgrad_call("w_out_grad", mixed, dx2b, None, None)
    dproj, dw_s, dbs_t, dg_v, dps, dw_pool = _mixer_bwd_call(proj, dmixed, w_s3, bs_t, g_v2, w_pool_g, ps2)
    gw_in = _wgrad_call("w_in_grad", h1, dproj, N_DEV, w_in_g.shape[2])
    dh1 = _dgrad_blocked_call("dh1_bwd", dproj, w_in_g)
    grad_x, dg_mix = _norm_bwd_call("mix_norm_bwd", dh1, xs, dx2, g_mix2, False)

    loss = lax.psum(loss_part[0, 0], ("x", "y", "c"))

    big_grads = dict(
        w_in=gw_in,
        w_pool=dw_pool.astype(BF16).reshape(N_DEV, -1, GROUP),
        w_out=gw_out.reshape(N_DEV, -1, d),
        w_up=gw_up,
        w_down=gw_down.reshape(N_DEV, -1, d),
    )
    xi, yi, ci = _position()
    pos = jnp.stack([xi, yi, ci]).astype(jnp.int32)
    from_sibling = _pair_exchange_call([big_grads[k] for k in _BIG])
    mine, to_chips = {}, []
    for k, got in zip(_BIG, from_sibling):
        mine[k], out = _pair_sum_call(k + "_pair_sum", pos, big_grads[k], got)
        to_chips.append(out)
    from_chips = _chip_exchange_call(to_chips)
    result = {}
    for k, got in zip(_BIG, from_chips):
        shape = weights[k].shape
        outs = _final_call(k + "_adamw", mine[k], got, shard2d[k], mom[k].reshape(shard2d[k].shape),
                           var[k].reshape(shard2d[k].shape))
        result[k] = [o.reshape(shape) for o in outs]

    small_parts = dict(g_mix=dg_mix, g_v=dg_v, w_s=dw_s, b_s=dbs_t.T, pool_scale=dps, g_ffn=dg_ffn, g_final=dg_final)
    parts = _small_all_gather_call(_pack([small_parts[k] for k in _SMALL]))
    outs = _small_final_call(parts, _pack([weights[k] for k in _SMALL]), _pack([mom[k] for k in _SMALL]),
                             _pack([var[k] for k in _SMALL]))
    like = [weights[k] for k in _SMALL]
    unpacked = [_unpack(o, like) for o in outs]
    for idx, k in enumerate(_SMALL):
        result[k] = [unpacked[q][idx] for q in range(4)]

    grads = [result[k][0] for k in _ORDER]
    deltas = [result[k][1] for k in _ORDER]
    new_m = [result[k][2] for k in _ORDER]
    new_v = [result[k][3] for k in _ORDER]
    return (loss, grad_x.reshape(x.shape), *grads, *deltas, *new_m, *new_v)
```

```python
import functools
import math

import jax
import jax.numpy as jnp
from jax import lax
from jax.experimental import pallas as pl
from jax.experimental.pallas import tpu as pltpu

F32 = jnp.float32
BF16 = jnp.bfloat16
MESH = pl.DeviceIdType.MESH

N_DEV = 8
EPS = 1e-6
CHUNK = 128
N_HEADS = 8
A_WIDTH = 1024
B_WIDTH = 1024
POOL_WINDOWS = (2, 4, 8, 16)
GROUP = 256
HALO = 16
LANES = 128

ADAM_LR = 0.001
ADAM_B1 = 0.9
ADAM_B2 = 0.999
ADAM_EPS = 1e-08
ADAM_WD = 0.01
ADAM_STEP = 10
ADAM_C1 = 1.0 - ADAM_B1 ** ADAM_STEP
ADAM_C2 = 1.0 - ADAM_B2 ** ADAM_STEP

VMEM_LIMIT = 56 * 1024 * 1024

_GELU_C = math.sqrt(2.0 / math.pi)


def _params(*sem):
    return pltpu.CompilerParams(dimension_semantics=sem, vmem_limit_bytes=VMEM_LIMIT)


def _gelu(x):
    return 0.5 * x * (1.0 + jnp.tanh(_GELU_C * (x + 0.044715 * x * x * x)))


def _gelu_and_grad(x):
    t = jnp.tanh(_GELU_C * (x + 0.044715 * x * x * x))
    g = 0.5 * x * (1.0 + t)
    dg = 0.5 * (1.0 + t) + 0.5 * x * (1.0 - t * t) * (_GELU_C * (1.0 + 3.0 * 0.044715 * x * x))
    return g, dg


def _dot_nn(a, b):
    return lax.dot_general(a, b, (((1,), (0,)), ((), ())), preferred_element_type=F32)


def _dot_nt(a, b):
    return lax.dot_general(a, b, (((1,), (1,)), ((), ())), preferred_element_type=F32)


def _dot_tn(a, b):
    return lax.dot_general(a, b, (((0,), (0,)), ((), ())), preferred_element_type=F32)


def _rms_rows(x):
    r = lax.rsqrt(jnp.mean(x * x, axis=-1, keepdims=True) + EPS)
    return x * r, r


def _rms_bwd_rows(dn, n, r):
    return r * (dn - n * jnp.mean(dn * n, axis=-1, keepdims=True))


def _tile(n, want):
    t = min(n, want)
    assert n % t == 0, (n, want)
    return t


_ANY = pl.BlockSpec(memory_space=pl.ANY)

SIBLING = 1
CHIPS = (4, 2, 6)


def _position():
    return lax.axis_index("x"), lax.axis_index("y"), lax.axis_index("c")


def _me():
    x, y, c = _position()
    return 4 * x + 2 * y + c


def _peer(rel):
    x, y, c = _position()
    return (x ^ ((rel >> 2) & 1), y ^ ((rel >> 1) & 1), c ^ (rel & 1))


class _Comm:
    def __init__(self, srcs, lands, new, plan):
        self.srcs, self.lands, self.new, self.plan = list(srcs), list(lands), list(new), plan


def _make_copies(phases, send_sems, recv_sems, local_sems):
    out, nr, nl = [], 0, 0
    for phase in phases:
        cps = []
        for item in phase:
            if item[0] == "local":
                cps.append(pltpu.make_async_copy(item[1], item[2], local_sems.at[nl]))
                nl += 1
            else:
                cps.append(pltpu.make_async_remote_copy(
                    src_ref=item[1], dst_ref=item[2], send_sem=send_sems.at[nr], recv_sem=recv_sems.at[nr],
                    device_id=_peer(item[3]), device_id_type=MESH))
                nr += 1
        out.append(cps)
    return out


def _count_copies(comm):
    phases = comm.plan([_FakeRef() for _ in comm.srcs], [_FakeRef() for _ in range(len(comm.lands) + len(comm.new))])
    items = [it for ph in phases for it in ph]
    return sum(it[0] == "remote" for it in items), sum(it[0] == "local" for it in items)


class _FakeRef:
    def __getitem__(self, idx):
        return self

    @property
    def at(self):
        return self


def _carrier_call(body, args, comm, *, name, grid, in_specs, out_specs, out_shape, scratch_shapes=(), sem):
    if not isinstance(out_shape, (list, tuple)):
        out_specs, out_shape = [out_specs], [out_shape]
    out_specs, out_shape, scratch_shapes = list(out_specs), list(out_shape), list(scratch_shapes)
    if comm is None:
        res = pl.pallas_call(body, name=name, grid=grid, in_specs=list(in_specs), out_specs=out_specs,
                             out_shape=out_shape, scratch_shapes=scratch_shapes, compiler_params=_params(*sem))(*args)
        return list(res)
    n_in, n_out, n_scr = len(args), len(out_shape), len(scratch_shapes)
    ns, nl, nn = len(comm.srcs), len(comm.lands), len(comm.new)
    n_remote, n_local = _count_copies(comm)

    def wrapped(*refs):
        ins, srcs = refs[:n_in], refs[n_in:n_in + ns]
        o = n_in + ns + nl
        outs, lands = refs[o:o + n_out], refs[o + n_out:o + n_out + nl + nn]
        scr = refs[o + n_out + nl + nn:]
        (copies,) = _make_copies(comm.plan(srcs, lands), *scr[n_scr:])
        ids = [pl.program_id(a) for a in range(len(grid))]
        first = functools.reduce(jnp.logical_and, [i == 0 for i in ids])
        last = functools.reduce(jnp.logical_and, [i == g - 1 for i, g in zip(ids, grid)])

        @pl.when(first)
        def _():
            for cp in copies:
                cp.start()

        body(*ins, *outs, *scr[:n_scr])

        @pl.when(last)
        def _():
            for cp in copies:
                cp.wait()

    land_shapes = [jax.ShapeDtypeStruct(a.shape, a.dtype) for a in comm.lands] + comm.new
    sems = [pltpu.SemaphoreType.DMA((max(n_remote, 1),)), pltpu.SemaphoreType.DMA((max(n_remote, 1),)),
            pltpu.SemaphoreType.DMA((max(n_local, 1),))]
    res = pl.pallas_call(
        wrapped, name=name, grid=grid,
        in_specs=list(in_specs) + [_ANY] * (ns + nl), out_specs=out_specs + [_ANY] * (nl + nn),
        out_shape=out_shape + land_shapes, scratch_shapes=scratch_shapes + sems,
        input_output_aliases={n_in + ns + k: n_out + k for k in range(nl)},
        compiler_params=_params(*sem))(*args, *comm.srcs, *comm.lands)
    return list(res)


def _comm_call(name, comm):
    ns, nl, nn = len(comm.srcs), len(comm.lands), len(comm.new)
    n_remote, n_local = _count_copies(comm)

    def body(*refs):
        srcs, lands, sems = refs[:ns], refs[ns + nl:ns + nl + nl + nn], refs[ns + nl + nl + nn:]
        for copies in _make_copies(comm.plan(srcs, lands), *sems):
            for cp in copies:
                cp.start()
            for cp in copies:
                cp.wait()

    land_shapes = [jax.ShapeDtypeStruct(a.shape, a.dtype) for a in comm.lands] + comm.new
    res = pl.pallas_call(
        body, name=name,
        in_specs=[_ANY] * (ns + nl), out_specs=[_ANY] * (nl + nn), out_shape=land_shapes,
        scratch_shapes=[pltpu.SemaphoreType.DMA((max(n_remote, 1),)), pltpu.SemaphoreType.DMA((max(n_remote, 1),)),
                        pltpu.SemaphoreType.DMA((max(n_local, 1),))],
        input_output_aliases={ns + k: k for k in range(nl)},
    )(*comm.srcs, *comm.lands)
    return list(res)


def _rows(ref, block, r0, r1):
    return ref.at[block, pl.ds(r0, r1 - r0)]


def _gather_first(shard, land, r0, r1):
    src = shard.at[pl.ds(r0, r1 - r0)]
    dst = _rows(land, _me(), r0, r1)
    return [("local", src, dst)] + [("remote", src, dst, rel) for rel in (SIBLING,) + CHIPS]


def _gather_pass_on(land, r0, r1):
    return [("remote", _rows(land, _me() ^ rel, r0, r1), _rows(land, _me() ^ rel, r0, r1), SIBLING) for rel in CHIPS]


def _pair_exchange(grad, land):
    _, _, c = _position()
    return [("remote", grad.at[2 * chip + (1 - c)], land.at[chip], SIBLING) for chip in range(4)]


def _chip_exchange(sums, land, r0, r1):
    return [("remote", _rows(sums, j, r0, r1), _rows(land, j, r0, r1), rel) for j, rel in enumerate(CHIPS)]


def _everyone(packed, land):
    dst = land.at[_me()]
    return [("local", packed, dst)] + [("remote", packed, dst, rel) for rel in range(1, N_DEV)]


def _proj_call(x, g_mix, w_in_g, comm=None):
    t, d = x.shape
    nb, _, cb = w_in_g.shape
    tm = _tile(t, 1024)

    def body(x_ref, g_ref, w_ref, proj_ref, h_ref):
        @pl.when(pl.program_id(1) == 0)
        def _():
            n, _ = _rms_rows(x_ref[...])
            h_ref[...] = (n * g_ref[...]).astype(BF16)

        proj_ref[...] = _dot_nn(h_ref[...], w_ref[...])

    return _carrier_call(
        body, (x, g_mix, w_in_g), comm, name="proj_fwd",
        grid=(t // tm, nb),
        in_specs=[pl.BlockSpec((tm, d), lambda i, j: (i, 0)),
                  pl.BlockSpec((1, d), lambda i, j: (0, 0)),
                  pl.BlockSpec((None, d, cb), lambda i, j: (j, 0, 0))],
        out_specs=[pl.BlockSpec((tm, cb), lambda i, j: (i, j)),
                   pl.BlockSpec((tm, d), lambda i, j: (i, 0))],
        out_shape=[jax.ShapeDtypeStruct((t, nb * cb), F32), jax.ShapeDtypeStruct((t, d), BF16)],
        sem=("parallel", "arbitrary"))


def _pool_counts(row0, rows, win):
    pos = row0 + lax.broadcasted_iota(jnp.int32, (rows, 1), 0)
    return jnp.minimum(pos + 1, win).astype(F32)


def _window_sum_back(ext, win):
    s = ext
    k = 1
    while k < win:
        s = s + pltpu.roll(s, k, 0)
        k *= 2
    return s


def _window_sum_fwd(ext, win):
    n = ext.shape[0]
    s = ext
    k = 1
    while k < win:
        s = s + pltpu.roll(s, n - k, 0)
        k *= 2
    return s


def _mixer_fwd_call(proj, w_s, bs_t, g_v, w_pool_g, pool_scale, comm=None):
    t = proj.shape[0]
    tt = _tile(t, 512)
    nchunk = tt // CHUNK
    hb = tt // HALO

    def body(pu_ref, pv_ref, z_ref, zp_ref, ws_ref, bs_ref, gv_ref, wp_ref, ps_ref, out_ref):
        i = pl.program_id(0)
        tril = (lax.broadcasted_iota(jnp.int32, (CHUNK, CHUNK), 0)
                >= lax.broadcasted_iota(jnp.int32, (CHUNK, CHUNK), 1))
        for h in range(N_HEADS):
            cols = slice(h * CHUNK, (h + 1) * CHUNK)
            vhat, _ = _rms_rows(_gelu(pv_ref[:, cols]))
            vn = (vhat * gv_ref[:, cols]).astype(BF16)
            u = _gelu(pu_ref[:, cols])
            w = jnp.where(tril, ws_ref[h], 0.0).astype(BF16)
            bcol = bs_ref[:, h:h + 1]
            for c in range(nchunk):
                rows = slice(c * CHUNK, (c + 1) * CHUNK)
                mixed = _dot_nn(w, vn[rows]) + bcol
                out_ref[rows, cols] = (u[rows] * mixed).astype(BF16)

        zprev = jnp.where(i > 0, zp_ref[...], 0.0)
        ext = jnp.concatenate([zprev, z_ref[...]], axis=0)
        for g, win in enumerate(POOL_WINDOWS):
            cols = slice(g * GROUP, (g + 1) * GROUP)
            zg = ext[:, cols]
            s = _window_sum_back(zg, win)
            pooled = s[HALO:] / _pool_counts(i * tt, tt, win) - zg[HALO:]
            wp = wp_ref[:, g].reshape(GROUP, GROUP)
            y = _dot_nn(pooled.astype(BF16), wp)
            out_ref[:, A_WIDTH + g * GROUP:A_WIDTH + (g + 1) * GROUP] = (y * ps_ref[:, cols]).astype(BF16)

    return _carrier_call(
        body, (proj, proj, proj, proj, w_s, bs_t, g_v, w_pool_g, pool_scale), comm, name="mixer_fwd",
        grid=(t // tt,),
        in_specs=[pl.BlockSpec((tt, A_WIDTH), lambda i: (i, 0)),
                  pl.BlockSpec((tt, A_WIDTH), lambda i: (i, 1)),
                  pl.BlockSpec((tt, B_WIDTH), lambda i: (i, 2)),
                  pl.BlockSpec((HALO, B_WIDTH), lambda i: (jnp.maximum(i * hb - 1, 0), 2)),
                  pl.BlockSpec((N_HEADS, CHUNK, CHUNK), lambda i: (0, 0, 0)),
                  pl.BlockSpec((CHUNK, N_HEADS), lambda i: (0, 0)),
                  pl.BlockSpec((1, A_WIDTH), lambda i: (0, 0)),
                  pl.BlockSpec((N_DEV, 4, GROUP // N_DEV, GROUP), lambda i: (0, 0, 0, 0)),
                  pl.BlockSpec((1, B_WIDTH), lambda i: (0, 0))],
        out_specs=pl.BlockSpec((tt, A_WIDTH + B_WIDTH), lambda i: (i, 0)),
        out_shape=jax.ShapeDtypeStruct((t, A_WIDTH + B_WIDTH), BF16),
        sem=("parallel",))


def _out_proj_call(mixed, w_out, x, comm=None):
    t, d = x.shape
    k = mixed.shape[1]
    tm = _tile(t, 1024)
    tn = _tile(d, 1024)

    def body(a_ref, w_ref, x_ref, o_ref):
        o_ref[...] = x_ref[...] + _dot_nn(a_ref[...], w_ref[...])

    return _carrier_call(
        body, (mixed, w_out, x), comm, name="out_proj_fwd",
        grid=(t // tm, d // tn),
        in_specs=[pl.BlockSpec((tm, k), lambda i, j: (i, 0)),
                  pl.BlockSpec((k, tn), lambda i, j: (0, j)),
                  pl.BlockSpec((tm, tn), lambda i, j: (i, j))],
        out_specs=pl.BlockSpec((tm, tn), lambda i, j: (i, j)),
        out_shape=jax.ShapeDtypeStruct((t, d), F32),
        sem=("parallel", "parallel"))


def _up_call(x2, g_ffn, w_up_g, comm=None):
    t, d = x2.shape
    nb, _, fb = w_up_g.shape
    tm = _tile(t, 1024)
    tn = _tile(fb, 512)
    per = fb // tn

    def body(x_ref, g_ref, w_ref, act_ref, h_ref):
        @pl.when(pl.program_id(1) == 0)
        def _():
            n, _ = _rms_rows(x_ref[...])
            h_ref[...] = (n * g_ref[...]).astype(BF16)

        a = jnp.maximum(_dot_nn(h_ref[...], w_ref[...]), 0.0)
        act_ref[...] = (a * a).astype(BF16)

    return _carrier_call(
        body, (x2, g_ffn, w_up_g), comm, name="up_fwd",
        grid=(t // tm, nb * per),
        in_specs=[pl.BlockSpec((tm, d), lambda i, j: (i, 0)),
                  pl.BlockSpec((1, d), lambda i, j: (0, 0)),
                  pl.BlockSpec((None, d, tn), lambda i, j: (j // per, 0, j % per))],
        out_specs=[pl.BlockSpec((tm, tn), lambda i, j: (i, j)),
                   pl.BlockSpec((tm, d), lambda i, j: (i, 0))],
        out_shape=[jax.ShapeDtypeStruct((t, nb * fb), BF16), jax.ShapeDtypeStruct((t, d), BF16)],
        sem=("parallel", "arbitrary"))


def _down_call(act, w_down, x2):
    t, f = act.shape
    d = x2.shape[1]
    tm = _tile(t, 1024)
    tn = _tile(d, 1024)
    tk = _tile(f, 1024)

    def body(a_ref, w_ref, x_ref, o_ref):
        @pl.when(pl.program_id(2) == 0)
        def _():
            o_ref[...] = x_ref[...]

        o_ref[...] += _dot_nn(a_ref[...], w_ref[...])

    return pl.pallas_call(
        body, name="down_fwd",
        grid=(t // tm, d // tn, f // tk),
        in_specs=[pl.BlockSpec((tm, tk), lambda i, j, k: (i, k)),
                  pl.BlockSpec((tk, tn), lambda i, j, k: (k, j)),
                  pl.BlockSpec((tm, tn), lambda i, j, k: (i, j))],
        out_specs=pl.BlockSpec((tm, tn), lambda i, j, k: (i, j)),
        out_shape=jax.ShapeDtypeStruct((t, d), F32),
        compiler_params=_params("parallel", "parallel", "arbitrary"),
    )(act, w_down, x2)


def _loss_call(x3, target, g_final):
    t, d = x3.shape
    tr = _tile(t, 256)

    def body(x_ref, tg_ref, g_ref, loss_ref, dx_ref, dxb_ref, dg_ref):
        @pl.when(pl.program_id(0) == 0)
        def _():
            loss_ref[...] = jnp.zeros_like(loss_ref)
            dg_ref[...] = jnp.zeros_like(dg_ref)

        n, r = _rms_rows(x_ref[...])
        err = n * g_ref[...] - tg_ref[...]
        loss_ref[...] += 0.5 * jnp.sum(jnp.mean(err * err, axis=-1, keepdims=True))
        dy = err * (1.0 / d)
        dg_ref[...] += jnp.sum(dy * n, axis=0, keepdims=True)
        dx = _rms_bwd_rows(dy * g_ref[...], n, r)
        dx_ref[...] = dx
        dxb_ref[...] = dx.astype(BF16)

    return pl.pallas_call(
        body, name="loss_head",
        grid=(t // tr,),
        in_specs=[pl.BlockSpec((tr, d), lambda i: (i, 0)),
                  pl.BlockSpec((tr, d), lambda i: (i, 0)),
                  pl.BlockSpec((1, d), lambda i: (0, 0))],
        out_specs=[pl.BlockSpec((8, LANES), lambda i: (0, 0)),
                   pl.BlockSpec((tr, d), lambda i: (i, 0)),
                   pl.BlockSpec((tr, d), lambda i: (i, 0)),
                   pl.BlockSpec((1, d), lambda i: (0, 0))],
        out_shape=[jax.ShapeDtypeStruct((8, LANES), F32), jax.ShapeDtypeStruct((t, d), F32),
                   jax.ShapeDtypeStruct((t, d), BF16), jax.ShapeDtypeStruct((1, d), F32)],
        compiler_params=_params("arbitrary"),
    )(x3, target, g_final)


def _norm_bwd_call(name, dh, x, dres, g, want_bf16, comm=None):
    t, d = x.shape
    tr = _tile(t, 256)

    def body(dh_ref, x_ref, dres_ref, g_ref, dx_ref, *rest):
        dg_ref = rest[-1]

        @pl.when(pl.program_id(0) == 0)
        def _():
            dg_ref[...] = jnp.zeros_like(dg_ref)

        n, r = _rms_rows(x_ref[...])
        dh = dh_ref[...]
        dg_ref[...] += jnp.sum(dh * n, axis=0, keepdims=True)
        dx = dres_ref[...] + _rms_bwd_rows(dh * g_ref[...], n, r)
        dx_ref[...] = dx
        if want_bf16:
            rest[0][...] = dx.astype(BF16)

    row = pl.BlockSpec((tr, d), lambda i: (i, 0))
    vec = pl.BlockSpec((1, d), lambda i: (0, 0))
    out_specs = [row] + ([row] if want_bf16 else []) + [vec]
    out_shape = ([jax.ShapeDtypeStruct((t, d), F32)]
                 + ([jax.ShapeDtypeStruct((t, d), BF16)] if want_bf16 else [])
                 + [jax.ShapeDtypeStruct((1, d), F32)])
    return _carrier_call(
        body, (dh, x, dres, g), comm, name=name,
        grid=(t // tr,),
        in_specs=[row, row, row, vec],
        out_specs=out_specs, out_shape=out_shape,
        sem=("arbitrary",))


def _dact_call(dx3b, w_down, act, comm=None):
    t, d = dx3b.shape
    f = w_down.shape[0]
    tm = _tile(t, 1024)
    tn = _tile(f, 512)

    def body(g_ref, w_ref, act_ref, o_ref):
        dact = _dot_nt(g_ref[...], w_ref[...])
        o_ref[...] = (dact * (2.0 * jnp.sqrt(act_ref[...].astype(F32)))).astype(BF16)

    return _carrier_call(
        body, (dx3b, w_down, act), comm, name="dact_bwd",
        grid=(t // tm, f // tn),
        in_specs=[pl.BlockSpec((tm, d), lambda i, j: (i, 0)),
                  pl.BlockSpec((tn, d), lambda i, j: (j, 0)),
                  pl.BlockSpec((tm, tn), lambda i, j: (i, j))],
        out_specs=pl.BlockSpec((tm, tn), lambda i, j: (i, j)),
        out_shape=jax.ShapeDtypeStruct((t, f), BF16),
        sem=("parallel", "parallel"))


def _wgrad_call(name, a, b, out_blocks, out_block_cols, comm=None):
    t, k1 = a.shape
    k2 = b.shape[1]
    tt = _tile(t, 1024)
    t1 = _tile(k1, 1024)
    t2 = _tile(k2 if out_blocks is None else out_block_cols, 1024)
    nk = t // tt

    def body(a_ref, b_ref, o_ref, acc_ref):
        k = pl.program_id(2)

        @pl.when(k == 0)
        def _():
            acc_ref[...] = jnp.zeros_like(acc_ref)

        acc_ref[...] += _dot_tn(a_ref[...], b_ref[...])

        @pl.when(k == nk - 1)
        def _():
            o_ref[...] = acc_ref[...].astype(BF16)

    if out_blocks is None:
        out_spec = pl.BlockSpec((t1, t2), lambda i, j, k: (i, j))
        out_shape = jax.ShapeDtypeStruct((k1, k2), BF16)
    else:
        per = out_block_cols // t2
        out_spec = pl.BlockSpec((None, t1, t2), lambda i, j, k: (j // per, i, j % per))
        out_shape = jax.ShapeDtypeStruct((out_blocks, k1, out_block_cols), BF16)
    return _carrier_call(
        body, (a, b), comm, name=name,
        grid=(k1 // t1, k2 // t2, nk),
        in_specs=[pl.BlockSpec((tt, t1), lambda i, j, k: (k, i)),
                  pl.BlockSpec((tt, t2), lambda i, j, k: (k, j))],
        out_specs=out_spec, out_shape=out_shape,
        scratch_shapes=[pltpu.VMEM((t1, t2), F32)],
        sem=("parallel", "parallel", "arbitrary"))


def _dgrad_blocked_call(name, g, w_g, comm=None):
    t = g.shape[0]
    nb, d, cb = w_g.shape
    tm = _tile(t, 1024)
    tn = _tile(d, 1024)
    tk = _tile(cb, 1024)
    per = cb // tk

    def body(g_ref, w_ref, o_ref):
        @pl.when(pl.program_id(2) == 0)
        def _():
            o_ref[...] = jnp.zeros_like(o_ref)

        o_ref[...] += _dot_nt(g_ref[...], w_ref[...])

    return _carrier_call(
        body, (g, w_g), comm, name=name,
        grid=(t // tm, d // tn, nb * per),
        in_specs=[pl.BlockSpec((tm, tk), lambda i, j, k: (i, k)),
                  pl.BlockSpec((None, tn, tk), lambda i, j, k: (k // per, j, k % per))],
        out_specs=pl.BlockSpec((tm, tn), lambda i, j, k: (i, j)),
        out_shape=jax.ShapeDtypeStruct((t, d), F32),
        sem=("parallel", "parallel", "arbitrary"))


def _dmixed_call(dx2b, w_out, comm=None):
    t, d = dx2b.shape
    e = w_out.shape[0]
    tm = _tile(t, 1024)
    tn = _tile(e, 1024)

    def body(g_ref, w_ref, o_ref):
        o_ref[...] = _dot_nt(g_ref[...], w_ref[...])

    return _carrier_call(
        body, (dx2b, w_out), comm, name="dmixed_bwd",
        grid=(t // tm, e // tn),
        in_specs=[pl.BlockSpec((tm, d), lambda i, j: (i, 0)),
                  pl.BlockSpec((tn, d), lambda i, j: (j, 0))],
        out_specs=pl.BlockSpec((tm, tn), lambda i, j: (i, j)),
        out_shape=jax.ShapeDtypeStruct((t, e), F32),
        sem=("parallel", "parallel"))


def _mixer_bwd_call(proj, dmixed, w_s, bs_t, g_v, w_pool_g, pool_scale, comm=None):
    t = proj.shape[0]
    tt = _tile(t, 512)
    nchunk = tt // CHUNK
    hb = tt // HALO
    last_halo = t // HALO - 1
    nsteps = t // tt
    rb = GROUP // N_DEV

    def body(pu_ref, pv_ref, z_ref, zp_ref, da_ref, db_ref, dbn_ref, ws_ref, bs_ref, gv_ref, wp_ref, ps_ref,
             dproj_ref, dws_ref, dbs_ref, dgv_ref, dps_ref, dwp_ref):
        i = pl.program_id(0)

        @pl.when(i == 0)
        def _():
            dws_ref[...] = jnp.zeros_like(dws_ref)
            dbs_ref[...] = jnp.zeros_like(dbs_ref)
            dgv_ref[...] = jnp.zeros_like(dgv_ref)
            dps_ref[...] = jnp.zeros_like(dps_ref)
            dwp_ref[...] = jnp.zeros_like(dwp_ref)

        tril = (lax.broadcasted_iota(jnp.int32, (CHUNK, CHUNK), 0)
                >= lax.broadcasted_iota(jnp.int32, (CHUNK, CHUNK), 1))
        for h in range(N_HEADS):
            cols = slice(h * CHUNK, (h + 1) * CHUNK)
            v, dv_dpv = _gelu_and_grad(pv_ref[:, cols])
            vhat, rv = _rms_rows(v)
            gv = gv_ref[:, cols]
            vn = (vhat * gv).astype(BF16)
            u, du_dpu = _gelu_and_grad(pu_ref[:, cols])
            w = jnp.where(tril, ws_ref[h], 0.0).astype(BF16)
            bcol = bs_ref[:, h:h + 1]
            dout = da_ref[:, cols]
            dmix = dout * u
            dmix_b = dmix.astype(BF16)
            dws = jnp.zeros((CHUNK, CHUNK), F32)
            dbs = jnp.zeros((CHUNK, 1), F32)
            dvn_parts = []
            du_parts = []
            for c in range(nchunk):
                rows = slice(c * CHUNK, (c + 1) * CHUNK)
                mixed = _dot_nn(w, vn[rows]) + bcol
                du_parts.append(dout[rows] * mixed)
                dvn_parts.append(_dot_tn(w, dmix_b[rows]))
                dws = dws + _dot_nt(dmix_b[rows], vn[rows])
                dbs = dbs + jnp.sum(dmix[rows], axis=1, keepdims=True)
            dws_ref[h] += jnp.where(tril, dws, 0.0)
            dbs_ref[:, h:h + 1] += dbs
            dvn = jnp.concatenate(dvn_parts, axis=0)
            du = jnp.concatenate(du_parts, axis=0)
            dgv_ref[:, cols] += jnp.sum(dvn * vhat, axis=0, keepdims=True)
            dv = _rms_bwd_rows(dvn * gv, vhat, rv)
            dproj_ref[:, cols] = (du * du_dpu).astype(BF16)
            dproj_ref[:, A_WIDTH + h * CHUNK:A_WIDTH + (h + 1) * CHUNK] = (dv * dv_dpv).astype(BF16)

        zprev = jnp.where(i > 0, zp_ref[...], 0.0)
        ext = jnp.concatenate([zprev, z_ref[...]], axis=0)
        dnext = jnp.where(i < nsteps - 1, dbn_ref[...], 0.0)
        dext = jnp.concatenate([db_ref[...], dnext], axis=0)
        for g, win in enumerate(POOL_WINDOWS):
            cols = slice(g * GROUP, (g + 1) * GROUP)
            zg = ext[:, cols]
            pooled = _window_sum_back(zg, win)[HALO:] / _pool_counts(i * tt, tt, win) - zg[HALO:]
            pooled_b = pooled.astype(BF16)
            wp = wp_ref[:, g].reshape(GROUP, GROUP)
            y = _dot_nn(pooled_b, wp)
            dout = dext[:, cols]
            dps_ref[:, cols] += jnp.sum(dout[:tt] * y, axis=0, keepdims=True)
            dy_b = (dout * ps_ref[:, cols]).astype(BF16)
            dwp_ref[:, g] += _dot_tn(pooled_b, dy_b[:tt]).reshape(N_DEV, rb, GROUP)
            dpooled = _dot_nt(dy_b, wp)
            q = dpooled / _pool_counts(i * tt, tt + HALO, win)
            dz = _window_sum_fwd(q, win)[:tt] - dpooled[:tt]
            dproj_ref[:, 2 * A_WIDTH + g * GROUP:2 * A_WIDTH + (g + 1) * GROUP] = dz.astype(BF16)

    def full(shape):
        return pl.BlockSpec(shape, lambda i: (0,) * len(shape))

    return _carrier_call(
        body, (proj, proj, proj, proj, dmixed, dmixed, dmixed, w_s, bs_t, g_v, w_pool_g, pool_scale), comm,
        name="mixer_bwd",
        grid=(nsteps,),
        in_specs=[pl.BlockSpec((tt, A_WIDTH), lambda i: (i, 0)),
                  pl.BlockSpec((tt, A_WIDTH), lambda i: (i, 1)),
                  pl.BlockSpec((tt, B_WIDTH), lambda i: (i, 2)),
                  pl.BlockSpec((HALO, B_WIDTH), lambda i: (jnp.maximum(i * hb - 1, 0), 2)),
                  pl.BlockSpec((tt, A_WIDTH), lambda i: (i, 0)),
                  pl.BlockSpec((tt, B_WIDTH), lambda i: (i, 1)),
                  pl.BlockSpec((HALO, B_WIDTH), lambda i: (jnp.minimum((i + 1) * hb, last_halo), 1)),
                  full((N_HEADS, CHUNK, CHUNK)), full((CHUNK, N_HEADS)), full((1, A_WIDTH)),
                  full((N_DEV, 4, rb, GROUP)), full((1, B_WIDTH))],
        out_specs=[pl.BlockSpec((tt, 2 * A_WIDTH + B_WIDTH), lambda i: (i, 0)),
                   full((N_HEADS, CHUNK, CHUNK)), full((CHUNK, N_HEADS)), full((1, A_WIDTH)),
                   full((1, B_WIDTH)), full((N_DEV, 4, rb, GROUP))],
        out_shape=[jax.ShapeDtypeStruct((t, 2 * A_WIDTH + B_WIDTH), BF16),
                   jax.ShapeDtypeStruct((N_HEADS, CHUNK, CHUNK), F32),
                   jax.ShapeDtypeStruct((CHUNK, N_HEADS), F32),
                   jax.ShapeDtypeStruct((1, A_WIDTH), F32),
                   jax.ShapeDtypeStruct((1, B_WIDTH), F32),
                   jax.ShapeDtypeStruct((N_DEV, 4, rb, GROUP), F32)],
        sem=("arbitrary",))


def _adamw(w, g, m, v):
    m = ADAM_B1 * m + (1.0 - ADAM_B1) * g
    v = ADAM_B2 * v + (1.0 - ADAM_B2) * (g * g)
    m_hat = m / ADAM_C1
    v_hat = v / ADAM_C2
    delta = -ADAM_LR * (m_hat / (jnp.sqrt(v_hat) + ADAM_EPS) + ADAM_WD * w)
    return delta, m, v


ROW_TILE_ELEMS = 256 * 1024


def _row_tile(r, c):
    t = r
    while t * c > ROW_TILE_ELEMS and t % 32 == 0:
        t //= 2
    return t


def _pair_sum_call(name, pos, grad, got):
    _, r, c = grad.shape
    tr = _row_tile(r, c)

    def chip_of(rel, pos_ref):
        px = jnp.where((rel == 0) | (rel == 2), 1 - pos_ref[0], pos_ref[0])
        py = jnp.where((rel == 1) | (rel == 2), 1 - pos_ref[1], pos_ref[1])
        return 2 * px + py

    def own_body(pos_ref, own_ref, got_ref, mine_ref):
        mine_ref[...] = own_ref[...].astype(F32) + got_ref[...].astype(F32)

    mine = pl.pallas_call(
        own_body, name=name + "_own",
        grid_spec=pltpu.PrefetchScalarGridSpec(
            num_scalar_prefetch=1, grid=(r // tr,),
            in_specs=[pl.BlockSpec((None, tr, c), lambda i, p: (4 * p[0] + 2 * p[1] + p[2], i, 0)),
                      pl.BlockSpec((None, tr, c), lambda i, p: (2 * p[0] + p[1], i, 0))],
            out_specs=pl.BlockSpec((tr, c), lambda i, p: (i, 0))),
        out_shape=jax.ShapeDtypeStruct((r, c), F32),
        compiler_params=_params("parallel"),
    )(pos, grad, got)

    def others_body(pos_ref, own_ref, got_ref, out_ref):
        out_ref[...] = (own_ref[...].astype(F32) + got_ref[...].astype(F32)).astype(BF16)

    out = pl.pallas_call(
        others_body, name=name + "_others",
        grid_spec=pltpu.PrefetchScalarGridSpec(
            num_scalar_prefetch=1, grid=(3, r // tr),
            in_specs=[pl.BlockSpec((None, tr, c), lambda k, i, p: (2 * chip_of(k, p) + p[2], i, 0)),
                      pl.BlockSpec((None, tr, c), lambda k, i, p: (chip_of(k, p), i, 0))],
            out_specs=pl.BlockSpec((None, tr, c), lambda k, i, p: (k, i, 0))),
        out_shape=jax.ShapeDtypeStruct((3, r, c), BF16),
        compiler_params=_params("parallel", "parallel"),
    )(pos, grad, got)
    return mine, out


def _final_call(name, mine, got, w, m, v):
    r, c = mine.shape
    tr = _row_tile(r, c)

    def body(mine_ref, got_ref, w_ref, m_ref, v_ref, g_out, d_out, m_out, v_out):
        g = mine_ref[...]
        for j in range(3):
            g = g + got_ref[j].astype(F32)
        delta, m_new, v_new = _adamw(w_ref[...], g, m_ref[...], v_ref[...])
        g_out[...] = g
        d_out[...] = delta
        m_out[...] = m_new
        v_out[...] = v_new

    row = pl.BlockSpec((tr, c), lambda i: (i, 0))
    return pl.pallas_call(
        body, name=name,
        grid=(r // tr,),
        in_specs=[row, pl.BlockSpec((3, tr, c), lambda i: (0, i, 0)), row, row, row],
        out_specs=[row] * 4,
        out_shape=[jax.ShapeDtypeStruct((r, c), F32)] * 4,
        compiler_params=_params("parallel"),
    )(mine, got, w, m, v)


def _small_final_call(name, parts, w, m, v):
    _, r, c = parts.shape

    def body(p_ref, w_ref, m_ref, v_ref, g_out, d_out, m_out, v_out):
        g = p_ref[0]
        for k in range(1, N_DEV):
            g = g + p_ref[k]
        delta, m_new, v_new = _adamw(w_ref[...], g, m_ref[...], v_ref[...])
        g_out[...] = g
        d_out[...] = delta
        m_out[...] = m_new
        v_out[...] = v_new

    return pl.pallas_call(
        body, name=name,
        out_shape=[jax.ShapeDtypeStruct((r, c), F32)] * 4,
        compiler_params=pltpu.CompilerParams(vmem_limit_bytes=VMEM_LIMIT),
    )(parts, w, m, v)


_SMALL_EARLY = ("g_v", "w_s", "b_s", "pool_scale", "g_ffn", "g_final")
_BIG = ("w_in", "w_pool", "w_out", "w_up", "w_down")
_ORDER = ("g_mix", "w_in", "g_v", "w_s", "b_s", "w_pool", "pool_scale", "w_out", "g_ffn", "w_up", "w_down", "g_final")


def _pack(parts):
    return jnp.concatenate([p.reshape(-1, LANES) for p in parts], axis=0)


def _unpack(packed, like):
    out, row = [], 0
    for a in like:
        rows = a.size // LANES
        out.append(packed[row:row + rows].reshape(a.shape))
        row += rows
    return out


def kernel(x, g_mix, w_in, g_v, w_s, b_s, w_pool, pool_scale, w_out, g_ffn, w_up, w_down, g_final, loss_target, m_g_mix, m_w_in, m_g_v, m_w_s, m_b_s, m_w_pool, m_pool_scale, m_w_out, m_g_ffn, m_w_up, m_w_down, m_g_final, v_g_mix, v_w_in, v_g_v, v_w_s, v_b_s, v_w_pool, v_pool_scale, v_w_out, v_g_ffn, v_w_up, v_w_down, v_g_final):
    weights = dict(g_mix=g_mix, w_in=w_in, g_v=g_v, w_s=w_s, b_s=b_s, w_pool=w_pool, pool_scale=pool_scale,
                   w_out=w_out, g_ffn=g_ffn, w_up=w_up, w_down=w_down, g_final=g_final)
    mom = dict(g_mix=m_g_mix, w_in=m_w_in, g_v=m_g_v, w_s=m_w_s, b_s=m_b_s, w_pool=m_w_pool,
               pool_scale=m_pool_scale, w_out=m_w_out, g_ffn=m_g_ffn, w_up=m_w_up, w_down=m_w_down,
               g_final=m_g_final)
    var = dict(g_mix=v_g_mix, w_in=v_w_in, g_v=v_g_v, w_s=v_w_s, b_s=v_b_s, w_pool=v_w_pool,
               pool_scale=v_pool_scale, w_out=v_w_out, g_ffn=v_g_ffn, w_up=v_w_up, w_down=v_w_down,
               g_final=v_g_final)

    t, d = x.shape[1], x.shape[2]
    xs = x.reshape(t, d)
    target = loss_target.reshape(t, d)

    shard2d = dict(w_in=w_in.reshape(d, -1), w_pool=w_pool.reshape(-1, GROUP), w_out=w_out.reshape(-1, d),
                   w_up=w_up.reshape(d, -1), w_down=w_down.reshape(-1, d))
    sb = {k: shard2d[k].astype(BF16) for k in _BIG}
    rows = {k: sb[k].shape[0] for k in _BIG}

    def gathered_shape(k):
        return jax.ShapeDtypeStruct((N_DEV,) + sb[k].shape, BF16)

    def landing(n, like):
        return jax.ShapeDtypeStruct((n,) + like.shape[1:], like.dtype)

    def from_everyone(block):
        return jax.ShapeDtypeStruct((N_DEV,) + block.shape, block.dtype)

    def cuts(r, fractions):
        return [0] + [int(r * f) // 16 * 16 for f in fractions] + [r]

    g_mix2, g_ffn2, g_final2 = g_mix.reshape(1, d), g_ffn.reshape(1, d), g_final.reshape(1, d)
    g_v2, ps2 = g_v.reshape(1, A_WIDTH), pool_scale.reshape(1, B_WIDTH)
    w_s3 = w_s.reshape(N_HEADS, CHUNK, CHUNK)
    bs_t = b_s.reshape(N_HEADS, CHUNK).T
    xi, yi, ci = _position()
    pos = jnp.stack([xi, yi, ci]).astype(jnp.int32)

    w_in_g, w_pool_g = _comm_call("gather_w_in_w_pool", _Comm(
        [sb["w_in"], sb["w_pool"]], [], [gathered_shape("w_in"), gathered_shape("w_pool")],
        lambda s, l: [_gather_first(s[0], l[0], 0, rows["w_in"]) + _gather_first(s[1], l[1], 0, rows["w_pool"]),
                      _gather_pass_on(l[0], 0, rows["w_in"]) + _gather_pass_on(l[1], 0, rows["w_pool"])]))
    w_pool_g = w_pool_g.reshape(N_DEV, 4, GROUP // N_DEV, GROUP)

    u = cuts(rows["w_up"], (0.31, 0.47, 0.72))
    proj, h1, w_out_g, w_up_g = _proj_call(xs, g_mix2, w_in_g, _Comm(
        [sb["w_out"], sb["w_up"]], [], [gathered_shape("w_out"), gathered_shape("w_up")],
        lambda s, l: [_gather_first(s[0], l[0], 0, rows["w_out"]) + _gather_first(s[1], l[1], u[0], u[1])]))
    mixed, w_out_g, w_up_g = _mixer_fwd_call(proj, w_s3, bs_t, g_v2, w_pool_g, ps2, _Comm(
        [sb["w_up"]], [w_out_g, w_up_g], [],
        lambda s, l: [_gather_pass_on(l[0], 0, rows["w_out"]) + _gather_pass_on(l[1], u[0], u[1])
                      + _gather_first(s[0], l[1], u[1], u[2])]))
    w_out_f = w_out_g.reshape(-1, d)
    x2, w_up_g = _out_proj_call(mixed, w_out_f, xs, _Comm(
        [sb["w_up"]], [w_up_g], [],
        lambda s, l: [_gather_pass_on(l[0], u[1], u[2]) + _gather_first(s[0], l[0], u[2], u[3])]))
    (w_up_g,) = _comm_call("gather_w_up_rest", _Comm(
        [sb["w_up"]], [w_up_g], [],
        lambda s, l: [_gather_pass_on(l[0], u[2], u[3]) + _gather_first(s[0], l[0], u[3], u[4]),
                      _gather_pass_on(l[0], u[3], u[4])]))
    act, h2, w_down_g = _up_call(x2, g_ffn2, w_up_g, _Comm(
        [sb["w_down"]], [], [gathered_shape("w_down")],
        lambda s, l: [_gather_first(s[0], l[0], 0, rows["w_down"])]))
    (w_down_g,) = _comm_call("pass_on_w_down", _Comm(
        [], [w_down_g], [], lambda s, l: [_gather_pass_on(l[0], 0, rows["w_down"])]))
    w_down_f = w_down_g.reshape(-1, d)
    x3 = _down_call(act, w_down_f, x2)
    loss_part, dx3, dx3b, dg_final = _loss_call(x3, target, g_final2)
    loss = lax.psum(loss_part[0, 0], ("x", "y", "c"))

    def pair_sum(k, grad, got):
        return _pair_sum_call(k + "_pair_sum", pos, grad, got)

    def finish(k, mine, got):
        s = shard2d[k]
        outs = _final_call(k + "_adamw", mine, got, s, mom[k].reshape(s.shape), var[k].reshape(s.shape))
        return [o.reshape(weights[k].shape) for o in outs]

    result = {}
    (gw_down,) = _wgrad_call("w_down_grad", act, dx3b, None, None)
    gw_down = gw_down.reshape(N_DEV, -1, d)
    da, got = _dact_call(dx3b, w_down_f, act, _Comm(
        [gw_down], [], [landing(4, gw_down)], lambda s, l: [_pair_exchange(s[0], l[0])]))
    mine_down, sums_down = pair_sum("w_down", gw_down, got)
    gw_up, got = _wgrad_call("w_up_grad", h2, da, N_DEV, w_up_g.shape[2], _Comm(
        [sums_down], [], [landing(3, sums_down)],
        lambda s, l: [_chip_exchange(s[0], l[0], 0, rows["w_down"])]))
    result["w_down"] = finish("w_down", mine_down, got)
    dh2, got = _dgrad_blocked_call("dh2_bwd", da, w_up_g, _Comm(
        [gw_up], [], [landing(4, gw_up)], lambda s, l: [_pair_exchange(s[0], l[0])]))
    mine_up, sums_up = pair_sum("w_up", gw_up, got)
    v = cuts(rows["w_up"], (0.28, 0.52, 0.84))
    dx2, dx2b, dg_ffn, got_up = _norm_bwd_call("ffn_norm_bwd", dh2, x2, dx3, g_ffn2, True, _Comm(
        [sums_up], [], [landing(3, sums_up)], lambda s, l: [_chip_exchange(s[0], l[0], v[0], v[1])]))
    dmixed, got_up = _dmixed_call(dx2b, w_out_f, _Comm(
        [sums_up], [got_up], [], lambda s, l: [_chip_exchange(s[0], l[0], v[1], v[2])]))
    gw_out, got_up = _wgrad_call("w_out_grad", mixed, dx2b, None, None, _Comm(
        [sums_up], [got_up], [], lambda s, l: [_chip_exchange(s[0], l[0], v[2], v[3])]))
    gw_out = gw_out.reshape(N_DEV, -1, d)
    dproj, dw_s, dbs_t, dg_v, dps, dw_pool, got_up, got = _mixer_bwd_call(
        proj, dmixed, w_s3, bs_t, g_v2, w_pool_g, ps2, _Comm(
            [sums_up, gw_out], [got_up], [landing(4, gw_out)],
            lambda s, l: [_chip_exchange(s[0], l[0], v[3], v[4]) + _pair_exchange(s[1], l[1])]))
    result["w_up"] = finish("w_up", mine_up, got_up)
    mine_out, sums_out = pair_sum("w_out", gw_out, got)
    gw_pool = dw_pool.astype(BF16).reshape(N_DEV, -1, GROUP)
    early = dict(g_v=dg_v, w_s=dw_s, b_s=dbs_t.T, pool_scale=dps, g_ffn=dg_ffn, g_final=dg_final)
    packed = _pack([early[k] for k in _SMALL_EARLY])
    gw_in, got, got_pool, parts_early = _wgrad_call("w_in_grad", h1, dproj, N_DEV, w_in_g.shape[2], _Comm(
        [sums_out, gw_pool, packed], [], [landing(3, sums_out), landing(4, gw_pool), from_everyone(packed)],
        lambda s, l: [_chip_exchange(s[0], l[0], 0, rows["w_out"]) + _pair_exchange(s[1], l[1])
                      + _everyone(s[2], l[2])]))
    result["w_out"] = finish("w_out", mine_out, got)
    mine_pool, sums_pool = pair_sum("w_pool", gw_pool, got_pool)
    (got,) = _comm_call("pair_exchange_w_in", _Comm(
        [gw_in], [], [landing(4, gw_in)], lambda s, l: [_pair_exchange(s[0], l[0])]))
    mine_in, sums_in = pair_sum("w_in", gw_in, got)
    dh1, got, got_pool = _dgrad_blocked_call("dh1_bwd", dproj, w_in_g, _Comm(
        [sums_in, sums_pool], [], [landing(3, sums_in), landing(3, sums_pool)],
        lambda s, l: [_chip_exchange(s[0], l[0], 0, rows["w_in"]) + _chip_exchange(s[1], l[1], 0, rows["w_pool"])]))
    result["w_in"] = finish("w_in", mine_in, got)
    result["w_pool"] = finish("w_pool", mine_pool, got_pool)
    grad_x, dg_mix = _norm_bwd_call("mix_norm_bwd", dh1, xs, dx2, g_mix2, False)
    packed = _pack([dg_mix])
    (parts_late,) = _comm_call("gather_g_mix_grad", _Comm(
        [packed], [], [from_everyone(packed)], lambda s, l: [_everyone(s[0], l[0])]))

    for names, parts, tag in ((_SMALL_EARLY, parts_early, "small_adamw"), (("g_mix",), parts_late, "g_mix_adamw")):
        outs = _small_final_call(tag, parts, _pack([weights[k] for k in names]), _pack([mom[k] for k in names]),
                                 _pack([var[k] for k in names]))
        like = [weights[k] for k in names]
        unpacked = [_unpack(o, like) for o in outs]
        for idx, k in enumerate(names):
            result[k] = [unpacked[q][idx] for q in range(4)]

    grads = [result[k][0] for k in _ORDER]
    deltas = [result[k][1] for k in _ORDER]
    new_m = [result[k][2] for k in _ORDER]
    new_v = [result[k][3] for k in _ORDER]
    return (loss, grad_x.reshape(x.shape), *grads, *deltas, *new_m, *new_v)
```

```python
import functools
import math

import jax
import jax.numpy as jnp
from jax import lax
from jax.experimental import pallas as pl
from jax.experimental.pallas import tpu as pltpu

F32 = jnp.float32
BF16 = jnp.bfloat16
MESH = pl.DeviceIdType.MESH

N_DEV = 8
EPS = 1e-6
CHUNK = 128
N_HEADS = 8
A_WIDTH = 1024
B_WIDTH = 1024
POOL_WINDOWS = (2, 4, 8, 16)
GROUP = 256
HALO = 16
LANES = 128

ADAM_LR = 0.001
ADAM_B1 = 0.9
ADAM_B2 = 0.999
ADAM_EPS = 1e-08
ADAM_WD = 0.01
ADAM_STEP = 10
ADAM_C1 = 1.0 - ADAM_B1 ** ADAM_STEP
ADAM_C2 = 1.0 - ADAM_B2 ** ADAM_STEP

VMEM_LIMIT = 56 * 1024 * 1024
MERGE_W_IN = 2

_GELU_C = math.sqrt(2.0 / math.pi)


def _params(*sem):
    return pltpu.CompilerParams(dimension_semantics=sem, vmem_limit_bytes=VMEM_LIMIT)


def _gelu(x):
    return 0.5 * x * (1.0 + jnp.tanh(_GELU_C * (x + 0.044715 * x * x * x)))


def _gelu_and_grad(x):
    t = jnp.tanh(_GELU_C * (x + 0.044715 * x * x * x))
    g = 0.5 * x * (1.0 + t)
    dg = 0.5 * (1.0 + t) + 0.5 * x * (1.0 - t * t) * (_GELU_C * (1.0 + 3.0 * 0.044715 * x * x))
    return g, dg


def _dot_nn(a, b):
    return lax.dot_general(a, b, (((1,), (0,)), ((), ())), preferred_element_type=F32)


def _dot_nt(a, b):
    return lax.dot_general(a, b, (((1,), (1,)), ((), ())), preferred_element_type=F32)


def _dot_tn(a, b):
    return lax.dot_general(a, b, (((0,), (0,)), ((), ())), preferred_element_type=F32)


def _rms_rows(x):
    r = lax.rsqrt(jnp.mean(x * x, axis=-1, keepdims=True) + EPS)
    return x * r, r


def _rms_bwd_rows(dn, n, r):
    return r * (dn - n * jnp.mean(dn * n, axis=-1, keepdims=True))


def _tile(n, want):
    t = min(n, want)
    assert n % t == 0, (n, want)
    return t


_ANY = pl.BlockSpec(memory_space=pl.ANY)

SIBLING = 1
CHIPS = (4, 2, 6)


def _position():
    return lax.axis_index("x"), lax.axis_index("y"), lax.axis_index("c")


def _me():
    x, y, c = _position()
    return 4 * x + 2 * y + c


def _peer(rel):
    x, y, c = _position()
    return (x ^ ((rel >> 2) & 1), y ^ ((rel >> 1) & 1), c ^ (rel & 1))


class _Comm:
    def __init__(self, srcs, lands, new, plan):
        self.srcs, self.lands, self.new, self.plan = list(srcs), list(lands), list(new), plan


def _make_copies(phases, send_sems, recv_sems, local_sems):
    out, nr, nl = [], 0, 0
    for phase in phases:
        cps = []
        for item in phase:
            if item[0] == "local":
                cps.append(pltpu.make_async_copy(item[1], item[2], local_sems.at[nl]))
                nl += 1
            else:
                cps.append(pltpu.make_async_remote_copy(
                    src_ref=item[1], dst_ref=item[2], send_sem=send_sems.at[nr], recv_sem=recv_sems.at[nr],
                    device_id=_peer(item[3]), device_id_type=MESH))
                nr += 1
        out.append(cps)
    return out


def _count_copies(comm):
    phases = comm.plan([_FakeRef() for _ in comm.srcs], [_FakeRef() for _ in range(len(comm.lands) + len(comm.new))])
    items = [it for ph in phases for it in ph]
    return sum(it[0] == "remote" for it in items), sum(it[0] == "local" for it in items)


class _FakeRef:
    def __getitem__(self, idx):
        return self

    @property
    def at(self):
        return self


def _carrier_call(body, args, comm, *, name, grid, in_specs, out_specs, out_shape, scratch_shapes=(), sem):
    if not isinstance(out_shape, (list, tuple)):
        out_specs, out_shape = [out_specs], [out_shape]
    out_specs, out_shape, scratch_shapes = list(out_specs), list(out_shape), list(scratch_shapes)
    if comm is None:
        res = pl.pallas_call(body, name=name, grid=grid, in_specs=list(in_specs), out_specs=out_specs,
                             out_shape=out_shape, scratch_shapes=scratch_shapes, compiler_params=_params(*sem))(*args)
        return list(res)
    n_in, n_out, n_scr = len(args), len(out_shape), len(scratch_shapes)
    ns, nl, nn = len(comm.srcs), len(comm.lands), len(comm.new)
    n_remote, n_local = _count_copies(comm)

    def wrapped(*refs):
        ins, srcs = refs[:n_in], refs[n_in:n_in + ns]
        o = n_in + ns + nl
        outs, lands = refs[o:o + n_out], refs[o + n_out:o + n_out + nl + nn]
        scr = refs[o + n_out + nl + nn:]
        (copies,) = _make_copies(comm.plan(srcs, lands), *scr[n_scr:])
        ids = [pl.program_id(a) for a in range(len(grid))]
        first = functools.reduce(jnp.logical_and, [i == 0 for i in ids])
        last = functools.reduce(jnp.logical_and, [i == g - 1 for i, g in zip(ids, grid)])

        @pl.when(first)
        def _():
            for cp in copies:
                cp.start()

        body(*ins, *outs, *scr[:n_scr])

        @pl.when(last)
        def _():
            for cp in copies:
                cp.wait()

    land_shapes = [jax.ShapeDtypeStruct(a.shape, a.dtype) for a in comm.lands] + comm.new
    sems = [pltpu.SemaphoreType.DMA((max(n_remote, 1),)), pltpu.SemaphoreType.DMA((max(n_remote, 1),)),
            pltpu.SemaphoreType.DMA((max(n_local, 1),))]
    res = pl.pallas_call(
        wrapped, name=name, grid=grid,
        in_specs=list(in_specs) + [_ANY] * (ns + nl), out_specs=out_specs + [_ANY] * (nl + nn),
        out_shape=out_shape + land_shapes, scratch_shapes=scratch_shapes + sems,
        input_output_aliases={n_in + ns + k: n_out + k for k in range(nl)},
        compiler_params=_params(*sem))(*args, *comm.srcs, *comm.lands)
    return list(res)


def _comm_call(name, comm):
    ns, nl, nn = len(comm.srcs), len(comm.lands), len(comm.new)
    n_remote, n_local = _count_copies(comm)

    def body(*refs):
        srcs, lands, sems = refs[:ns], refs[ns + nl:ns + nl + nl + nn], refs[ns + nl + nl + nn:]
        for copies in _make_copies(comm.plan(srcs, lands), *sems):
            for cp in copies:
                cp.start()
            for cp in copies:
                cp.wait()

    land_shapes = [jax.ShapeDtypeStruct(a.shape, a.dtype) for a in comm.lands] + comm.new
    res = pl.pallas_call(
        body, name=name,
        in_specs=[_ANY] * (ns + nl), out_specs=[_ANY] * (nl + nn), out_shape=land_shapes,
        scratch_shapes=[pltpu.SemaphoreType.DMA((max(n_remote, 1),)), pltpu.SemaphoreType.DMA((max(n_remote, 1),)),
                        pltpu.SemaphoreType.DMA((max(n_local, 1),))],
        input_output_aliases={ns + k: k for k in range(nl)},
    )(*comm.srcs, *comm.lands)
    return list(res)


def _rows(ref, block, r0, r1):
    return ref.at[block, pl.ds(r0, r1 - r0)]


def _gather_first(shard, land, r0, r1):
    src = shard.at[pl.ds(r0, r1 - r0)]
    dst = _rows(land, _me(), r0, r1)
    return [("local", src, dst)] + [("remote", src, dst, rel) for rel in (SIBLING,) + CHIPS]


def _gather_pass_on(land, r0, r1):
    return [("remote", _rows(land, _me() ^ rel, r0, r1), _rows(land, _me() ^ rel, r0, r1), SIBLING) for rel in CHIPS]


def _pair_exchange(grad, land):
    _, _, c = _position()
    return [("remote", grad.at[2 * chip + (1 - c)], land.at[chip], SIBLING) for chip in range(4)]


def _chip_exchange(sums, land, r0, r1):
    return [("remote", _rows(sums, j, r0, r1), _rows(land, j, r0, r1), rel) for j, rel in enumerate(CHIPS)]


def _everyone(packed, land):
    dst = land.at[_me()]
    return [("local", packed, dst)] + [("remote", packed, dst, rel) for rel in range(1, N_DEV)]


def _proj_call(x, g_mix, w_in_g, comm=None):
    t, d = x.shape
    nb, _, cb = w_in_g.shape
    tm = _tile(t, 1024)
    mg = MERGE_W_IN

    def body(x_ref, g_ref, w_ref, proj_ref, h_ref):
        @pl.when(pl.program_id(1) == 0)
        def _():
            n, _ = _rms_rows(x_ref[...])
            h_ref[...] = (n * g_ref[...]).astype(BF16)

        w = jnp.concatenate([w_ref[b] for b in range(mg)], axis=1)
        proj_ref[...] = _dot_nn(h_ref[...], w)

    return _carrier_call(
        body, (x, g_mix, w_in_g), comm, name="proj_fwd",
        grid=(t // tm, nb // mg),
        in_specs=[pl.BlockSpec((tm, d), lambda i, j: (i, 0)),
                  pl.BlockSpec((1, d), lambda i, j: (0, 0)),
                  pl.BlockSpec((mg, d, cb), lambda i, j: (j, 0, 0))],
        out_specs=[pl.BlockSpec((tm, mg * cb), lambda i, j: (i, j)),
                   pl.BlockSpec((tm, d), lambda i, j: (i, 0))],
        out_shape=[jax.ShapeDtypeStruct((t, nb * cb), F32), jax.ShapeDtypeStruct((t, d), BF16)],
        sem=("parallel", "arbitrary"))


def _pool_counts(row0, rows, win):
    pos = row0 + lax.broadcasted_iota(jnp.int32, (rows, 1), 0)
    return jnp.minimum(pos + 1, win).astype(F32)


def _window_sum_back(ext, win):
    s = ext
    k = 1
    while k < win:
        s = s + pltpu.roll(s, k, 0)
        k *= 2
    return s


def _window_sum_fwd(ext, win):
    n = ext.shape[0]
    s = ext
    k = 1
    while k < win:
        s = s + pltpu.roll(s, n - k, 0)
        k *= 2
    return s


def _mixer_fwd_call(proj, w_s, bs_t, g_v, w_pool_g, pool_scale, comm=None):
    t = proj.shape[0]
    tt = _tile(t, 512)
    nchunk = tt // CHUNK
    hb = tt // HALO

    def body(pu_ref, pv_ref, z_ref, zp_ref, ws_ref, bs_ref, gv_ref, wp_ref, ps_ref, out_ref):
        i = pl.program_id(0)
        tril = (lax.broadcasted_iota(jnp.int32, (CHUNK, CHUNK), 0)
                >= lax.broadcasted_iota(jnp.int32, (CHUNK, CHUNK), 1))
        for h in range(N_HEADS):
            cols = slice(h * CHUNK, (h + 1) * CHUNK)
            vhat, _ = _rms_rows(_gelu(pv_ref[:, cols]))
            vn = (vhat * gv_ref[:, cols]).astype(BF16)
            u = _gelu(pu_ref[:, cols])
            w = jnp.where(tril, ws_ref[h], 0.0).astype(BF16)
            bcol = bs_ref[:, h:h + 1]
            for c in range(nchunk):
                rows = slice(c * CHUNK, (c + 1) * CHUNK)
                mixed = _dot_nn(w, vn[rows]) + bcol
                out_ref[rows, cols] = (u[rows] * mixed).astype(BF16)

        zprev = jnp.where(i > 0, zp_ref[...], 0.0)
        ext = jnp.concatenate([zprev, z_ref[...]], axis=0)
        for g, win in enumerate(POOL_WINDOWS):
            cols = slice(g * GROUP, (g + 1) * GROUP)
            zg = ext[:, cols]
            s = _window_sum_back(zg, win)
            pooled = s[HALO:] / _pool_counts(i * tt, tt, win) - zg[HALO:]
            wp = wp_ref[:, g].reshape(GROUP, GROUP)
            y = _dot_nn(pooled.astype(BF16), wp)
            out_ref[:, A_WIDTH + g * GROUP:A_WIDTH + (g + 1) * GROUP] = (y * ps_ref[:, cols]).astype(BF16)

    return _carrier_call(
        body, (proj, proj, proj, proj, w_s, bs_t, g_v, w_pool_g, pool_scale), comm, name="mixer_fwd",
        grid=(t // tt,),
        in_specs=[pl.BlockSpec((tt, A_WIDTH), lambda i: (i, 0)),
                  pl.BlockSpec((tt, A_WIDTH), lambda i: (i, 1)),
                  pl.BlockSpec((tt, B_WIDTH), lambda i: (i, 2)),
                  pl.BlockSpec((HALO, B_WIDTH), lambda i: (jnp.maximum(i * hb - 1, 0), 2)),
                  pl.BlockSpec((N_HEADS, CHUNK, CHUNK), lambda i: (0, 0, 0)),
                  pl.BlockSpec((CHUNK, N_HEADS), lambda i: (0, 0)),
                  pl.BlockSpec((1, A_WIDTH), lambda i: (0, 0)),
                  pl.BlockSpec((N_DEV, 4, GROUP // N_DEV, GROUP), lambda i: (0, 0, 0, 0)),
                  pl.BlockSpec((1, B_WIDTH), lambda i: (0, 0))],
        out_specs=pl.BlockSpec((tt, A_WIDTH + B_WIDTH), lambda i: (i, 0)),
        out_shape=jax.ShapeDtypeStruct((t, A_WIDTH + B_WIDTH), BF16),
        sem=("parallel",))


def _out_proj_call(mixed, w_out, x, comm=None):
    t, d = x.shape
    k = mixed.shape[1]
    tm = _tile(t, 1024)
    tn = _tile(d, 1024)

    def body(a_ref, w_ref, x_ref, o_ref):
        o_ref[...] = x_ref[...] + _dot_nn(a_ref[...], w_ref[...])

    return _carrier_call(
        body, (mixed, w_out, x), comm, name="out_proj_fwd",
        grid=(t // tm, d // tn),
        in_specs=[pl.BlockSpec((tm, k), lambda i, j: (i, 0)),
                  pl.BlockSpec((k, tn), lambda i, j: (0, j)),
                  pl.BlockSpec((tm, tn), lambda i, j: (i, j))],
        out_specs=pl.BlockSpec((tm, tn), lambda i, j: (i, j)),
        out_shape=jax.ShapeDtypeStruct((t, d), F32),
        sem=("parallel", "parallel"))


def _up_call(x2, g_ffn, w_up_g, comm=None):
    t, d = x2.shape
    nb, _, fb = w_up_g.shape
    tm = _tile(t, 1024)
    tn = _tile(fb, 1024)
    per = fb // tn

    def body(x_ref, g_ref, w_ref, act_ref, h_ref):
        @pl.when(pl.program_id(1) == 0)
        def _():
            n, _ = _rms_rows(x_ref[...])
            h_ref[...] = (n * g_ref[...]).astype(BF16)

        a = jnp.maximum(_dot_nn(h_ref[...], w_ref[...]), 0.0)
        act_ref[...] = (a * a).astype(BF16)

    return _carrier_call(
        body, (x2, g_ffn, w_up_g), comm, name="up_fwd",
        grid=(t // tm, nb * per),
        in_specs=[pl.BlockSpec((tm, d), lambda i, j: (i, 0)),
                  pl.BlockSpec((1, d), lambda i, j: (0, 0)),
                  pl.BlockSpec((None, d, tn), lambda i, j: (j // per, 0, j % per))],
        out_specs=[pl.BlockSpec((tm, tn), lambda i, j: (i, j)),
                   pl.BlockSpec((tm, d), lambda i, j: (i, 0))],
        out_shape=[jax.ShapeDtypeStruct((t, nb * fb), BF16), jax.ShapeDtypeStruct((t, d), BF16)],
        sem=("parallel", "arbitrary"))


def _down_call(act, w_down, x2):
    t, f = act.shape
    d = x2.shape[1]
    tm = _tile(t, 1024)
    tn = _tile(d, 2048)
    tk = _tile(f, 1024)

    def body(a_ref, w_ref, x_ref, o_ref):
        @pl.when(pl.program_id(2) == 0)
        def _():
            o_ref[...] = x_ref[...]

        o_ref[...] += _dot_nn(a_ref[...], w_ref[...])

    return pl.pallas_call(
        body, name="down_fwd",
        grid=(t // tm, d // tn, f // tk),
        in_specs=[pl.BlockSpec((tm, tk), lambda i, j, k: (i, k)),
                  pl.BlockSpec((tk, tn), lambda i, j, k: (k, j)),
                  pl.BlockSpec((tm, tn), lambda i, j, k: (i, j))],
        out_specs=pl.BlockSpec((tm, tn), lambda i, j, k: (i, j)),
        out_shape=jax.ShapeDtypeStruct((t, d), F32),
        compiler_params=_params("parallel", "parallel", "arbitrary"),
    )(act, w_down, x2)


def _loss_call(x3, target, g_final):
    t, d = x3.shape
    tr = _tile(t, 256)

    def body(x_ref, tg_ref, g_ref, loss_ref, dx_ref, dxb_ref, dg_ref):
        @pl.when(pl.program_id(0) == 0)
        def _():
            loss_ref[...] = jnp.zeros_like(loss_ref)
            dg_ref[...] = jnp.zeros_like(dg_ref)

        n, r = _rms_rows(x_ref[...])
        err = n * g_ref[...] - tg_ref[...]
        loss_ref[...] += 0.5 * jnp.sum(jnp.mean(err * err, axis=-1, keepdims=True))
        dy = err * (1.0 / d)
        dg_ref[...] += jnp.sum(dy * n, axis=0, keepdims=True)
        dx = _rms_bwd_rows(dy * g_ref[...], n, r)
        dx_ref[...] = dx
        dxb_ref[...] = dx.astype(BF16)

    return pl.pallas_call(
        body, name="loss_head",
        grid=(t // tr,),
        in_specs=[pl.BlockSpec((tr, d), lambda i: (i, 0)),
                  pl.BlockSpec((tr, d), lambda i: (i, 0)),
                  pl.BlockSpec((1, d), lambda i: (0, 0))],
        out_specs=[pl.BlockSpec((8, LANES), lambda i: (0, 0)),
                   pl.BlockSpec((tr, d), lambda i: (i, 0)),
                   pl.BlockSpec((tr, d), lambda i: (i, 0)),
                   pl.BlockSpec((1, d), lambda i: (0, 0))],
        out_shape=[jax.ShapeDtypeStruct((8, LANES), F32), jax.ShapeDtypeStruct((t, d), F32),
                   jax.ShapeDtypeStruct((t, d), BF16), jax.ShapeDtypeStruct((1, d), F32)],
        compiler_params=_params("arbitrary"),
    )(x3, target, g_final)


def _norm_bwd_call(name, dh, x, dres, g, want_bf16, comm=None):
    t, d = x.shape
    tr = _tile(t, 256)

    def body(dh_ref, x_ref, dres_ref, g_ref, dx_ref, *rest):
        dg_ref = rest[-1]

        @pl.when(pl.program_id(0) == 0)
        def _():
            dg_ref[...] = jnp.zeros_like(dg_ref)

        n, r = _rms_rows(x_ref[...])
        dh = dh_ref[...]
        dg_ref[...] += jnp.sum(dh * n, axis=0, keepdims=True)
        dx = dres_ref[...] + _rms_bwd_rows(dh * g_ref[...], n, r)
        dx_ref[...] = dx
        if want_bf16:
            rest[0][...] = dx.astype(BF16)

    row = pl.BlockSpec((tr, d), lambda i: (i, 0))
    vec = pl.BlockSpec((1, d), lambda i: (0, 0))
    out_specs = [row] + ([row] if want_bf16 else []) + [vec]
    out_shape = ([jax.ShapeDtypeStruct((t, d), F32)]
                 + ([jax.ShapeDtypeStruct((t, d), BF16)] if want_bf16 else [])
                 + [jax.ShapeDtypeStruct((1, d), F32)])
    return _carrier_call(
        body, (dh, x, dres, g), comm, name=name,
        grid=(t // tr,),
        in_specs=[row, row, row, vec],
        out_specs=out_specs, out_shape=out_shape,
        sem=("arbitrary",))


def _dact_call(dx3b, w_down, act, comm=None):
    t, d = dx3b.shape
    f = w_down.shape[0]
    tm = _tile(t, 1024)
    tn = _tile(f, 1024)

    def body(g_ref, w_ref, act_ref, o_ref):
        dact = _dot_nt(g_ref[...], w_ref[...])
        o_ref[...] = (dact * (2.0 * jnp.sqrt(act_ref[...].astype(F32)))).astype(BF16)

    return _carrier_call(
        body, (dx3b, w_down, act), comm, name="dact_bwd",
        grid=(t // tm, f // tn),
        in_specs=[pl.BlockSpec((tm, d), lambda i, j: (i, 0)),
                  pl.BlockSpec((tn, d), lambda i, j: (j, 0)),
                  pl.BlockSpec((tm, tn), lambda i, j: (i, j))],
        out_specs=pl.BlockSpec((tm, tn), lambda i, j: (i, j)),
        out_shape=jax.ShapeDtypeStruct((t, f), BF16),
        sem=("parallel", "parallel"))


def _wgrad_call(name, a, b, out_blocks, out_block_cols, comm=None, *, t1, t2=None, merge=1):
    t, k1 = a.shape
    k2 = b.shape[1]
    tt = _tile(t, 1024)
    t1 = _tile(k1, t1)
    t2 = _tile(k2, t2) if out_blocks is None else merge * out_block_cols
    nk = t // tt

    def body(a_ref, b_ref, o_ref, acc_ref):
        k = pl.program_id(2)

        @pl.when(k == 0)
        def _():
            acc_ref[...] = jnp.zeros_like(acc_ref)

        acc_ref[...] += _dot_tn(a_ref[...], b_ref[...])

        @pl.when(k == nk - 1)
        def _():
            if out_blocks is None:
                o_ref[...] = acc_ref[...].astype(BF16)
            else:
                for blk in range(merge):
                    o_ref[blk] = acc_ref[:, blk * out_block_cols:(blk + 1) * out_block_cols].astype(BF16)

    if out_blocks is None:
        out_spec = pl.BlockSpec((t1, t2), lambda i, j, k: (i, j))
        out_shape = jax.ShapeDtypeStruct((k1, k2), BF16)
    else:
        out_spec = pl.BlockSpec((merge, t1, out_block_cols), lambda i, j, k: (j, i, 0))
        out_shape = jax.ShapeDtypeStruct((out_blocks, k1, out_block_cols), BF16)
    return _carrier_call(
        body, (a, b), comm, name=name,
        grid=(k1 // t1, k2 // t2, nk),
        in_specs=[pl.BlockSpec((tt, t1), lambda i, j, k: (k, i)),
                  pl.BlockSpec((tt, t2), lambda i, j, k: (k, j))],
        out_specs=out_spec, out_shape=out_shape,
        scratch_shapes=[pltpu.VMEM((t1, t2), F32)],
        sem=("parallel", "parallel", "arbitrary"))


def _dgrad_blocked_call(name, g, w_g, comm=None, *, merge=1):
    t = g.shape[0]
    nb, d, cb = w_g.shape
    tm = _tile(t, 1024)
    tn = _tile(d, 2048)
    tk = merge * cb

    def body(g_ref, w_ref, o_ref):
        @pl.when(pl.program_id(2) == 0)
        def _():
            o_ref[...] = jnp.zeros_like(o_ref)

        w = w_ref[0] if merge == 1 else jnp.concatenate([w_ref[b] for b in range(merge)], axis=1)
        o_ref[...] += _dot_nt(g_ref[...], w)

    return _carrier_call(
        body, (g, w_g), comm, name=name,
        grid=(t // tm, d // tn, nb // merge),
        in_specs=[pl.BlockSpec((tm, tk), lambda i, j, k: (i, k)),
                  pl.BlockSpec((merge, tn, cb), lambda i, j, k: (k, j, 0))],
        out_specs=pl.BlockSpec((tm, tn), lambda i, j, k: (i, j)),
        out_shape=jax.ShapeDtypeStruct((t, d), F32),
        sem=("parallel", "parallel", "arbitrary"))


def _dmixed_call(dx2b, w_out, comm=None):
    t, d = dx2b.shape
    e = w_out.shape[0]
    tm = _tile(t, 1024)
    tn = _tile(e, 1024)

    def body(g_ref, w_ref, o_ref):
        o_ref[...] = _dot_nt(g_ref[...], w_ref[...])

    return _carrier_call(
        body, (dx2b, w_out), comm, name="dmixed_bwd",
        grid=(t // tm, e // tn),
        in_specs=[pl.BlockSpec((tm, d), lambda i, j: (i, 0)),
                  pl.BlockSpec((tn, d), lambda i, j: (j, 0))],
        out_specs=pl.BlockSpec((tm, tn), lambda i, j: (i, j)),
        out_shape=jax.ShapeDtypeStruct((t, e), F32),
        sem=("parallel", "parallel"))


def _mixer_bwd_call(proj, dmixed, w_s, bs_t, g_v, w_pool_g, pool_scale, comm=None):
    t = proj.shape[0]
    tt = _tile(t, 512)
    nchunk = tt // CHUNK
    hb = tt // HALO
    last_halo = t // HALO - 1
    nsteps = t // tt
    rb = GROUP // N_DEV

    def body(pu_ref, pv_ref, z_ref, zp_ref, da_ref, db_ref, dbn_ref, ws_ref, bs_ref, gv_ref, wp_ref, ps_ref,
             dproj_ref, dws_ref, dbs_ref, dgv_ref, dps_ref, dwp_ref):
        i = pl.program_id(0)

        @pl.when(i == 0)
        def _():
            dws_ref[...] = jnp.zeros_like(dws_ref)
            dbs_ref[...] = jnp.zeros_like(dbs_ref)
            dgv_ref[...] = jnp.zeros_like(dgv_ref)
            dps_ref[...] = jnp.zeros_like(dps_ref)
            dwp_ref[...] = jnp.zeros_like(dwp_ref)

        tril = (lax.broadcasted_iota(jnp.int32, (CHUNK, CHUNK), 0)
                >= lax.broadcasted_iota(jnp.int32, (CHUNK, CHUNK), 1))
        for h in range(N_HEADS):
            cols = slice(h * CHUNK, (h + 1) * CHUNK)
            v, dv_dpv = _gelu_and_grad(pv_ref[:, cols])
            vhat, rv = _rms_rows(v)
            gv = gv_ref[:, cols]
            vn = (vhat * gv).astype(BF16)
            u, du_dpu = _gelu_and_grad(pu_ref[:, cols])
            w = jnp.where(tril, ws_ref[h], 0.0).astype(BF16)
            bcol = bs_ref[:, h:h + 1]
            dout = da_ref[:, cols]
            dmix = dout * u
            dmix_b = dmix.astype(BF16)
            dws = jnp.zeros((CHUNK, CHUNK), F32)
            dbs = jnp.zeros((CHUNK, 1), F32)
            dvn_parts = []
            du_parts = []
            for c in range(nchunk):
                rows = slice(c * CHUNK, (c + 1) * CHUNK)
                mixed = _dot_nn(w, vn[rows]) + bcol
                du_parts.append(dout[rows] * mixed)
                dvn_parts.append(_dot_tn(w, dmix_b[rows]))
                dws = dws + _dot_nt(dmix_b[rows], vn[rows])
                dbs = dbs + jnp.sum(dmix[rows], axis=1, keepdims=True)
            dws_ref[h] += jnp.where(tril, dws, 0.0)
            dbs_ref[:, h:h + 1] += dbs
            dvn = jnp.concatenate(dvn_parts, axis=0)
            du = jnp.concatenate(du_parts, axis=0)
            dgv_ref[:, cols] += jnp.sum(dvn * vhat, axis=0, keepdims=True)
            dv = _rms_bwd_rows(dvn * gv, vhat, rv)
            dproj_ref[:, cols] = (du * du_dpu).astype(BF16)
            dproj_ref[:, A_WIDTH + h * CHUNK:A_WIDTH + (h + 1) * CHUNK] = (dv * dv_dpv).astype(BF16)

        zprev = jnp.where(i > 0, zp_ref[...], 0.0)
        ext = jnp.concatenate([zprev, z_ref[...]], axis=0)
        dnext = jnp.where(i < nsteps - 1, dbn_ref[...], 0.0)
        dext = jnp.concatenate([db_ref[...], dnext], axis=0)
        for g, win in enumerate(POOL_WINDOWS):
            cols = slice(g * GROUP, (g + 1) * GROUP)
            zg = ext[:, cols]
            pooled = _window_sum_back(zg, win)[HALO:] / _pool_counts(i * tt, tt, win) - zg[HALO:]
            pooled_b = pooled.astype(BF16)
            wp = wp_ref[:, g].reshape(GROUP, GROUP)
            y = _dot_nn(pooled_b, wp)
            dout = dext[:, cols]
            dps_ref[:, cols] += jnp.sum(dout[:tt] * y, axis=0, keepdims=True)
            dy_b = (dout * ps_ref[:, cols]).astype(BF16)
            dwp_ref[:, g] += _dot_tn(pooled_b, dy_b[:tt]).reshape(N_DEV, rb, GROUP)
            dpooled = _dot_nt(dy_b, wp)
            q = dpooled / _pool_counts(i * tt, tt + HALO, win)
            dz = _window_sum_fwd(q, win)[:tt] - dpooled[:tt]
            dproj_ref[:, 2 * A_WIDTH + g * GROUP:2 * A_WIDTH + (g + 1) * GROUP] = dz.astype(BF16)

    def full(shape):
        return pl.BlockSpec(shape, lambda i: (0,) * len(shape))

    return _carrier_call(
        body, (proj, proj, proj, proj, dmixed, dmixed, dmixed, w_s, bs_t, g_v, w_pool_g, pool_scale), comm,
        name="mixer_bwd",
        grid=(nsteps,),
        in_specs=[pl.BlockSpec((tt, A_WIDTH), lambda i: (i, 0)),
                  pl.BlockSpec((tt, A_WIDTH), lambda i: (i, 1)),
                  pl.BlockSpec((tt, B_WIDTH), lambda i: (i, 2)),
                  pl.BlockSpec((HALO, B_WIDTH), lambda i: (jnp.maximum(i * hb - 1, 0), 2)),
                  pl.BlockSpec((tt, A_WIDTH), lambda i: (i, 0)),
                  pl.BlockSpec((tt, B_WIDTH), lambda i: (i, 1)),
                  pl.BlockSpec((HALO, B_WIDTH), lambda i: (jnp.minimum((i + 1) * hb, last_halo), 1)),
                  full((N_HEADS, CHUNK, CHUNK)), full((CHUNK, N_HEADS)), full((1, A_WIDTH)),
                  full((N_DEV, 4, rb, GROUP)), full((1, B_WIDTH))],
        out_specs=[pl.BlockSpec((tt, 2 * A_WIDTH + B_WIDTH), lambda i: (i, 0)),
                   full((N_HEADS, CHUNK, CHUNK)), full((CHUNK, N_HEADS)), full((1, A_WIDTH)),
                   full((1, B_WIDTH)), full((N_DEV, 4, rb, GROUP))],
        out_shape=[jax.ShapeDtypeStruct((t, 2 * A_WIDTH + B_WIDTH), BF16),
                   jax.ShapeDtypeStruct((N_HEADS, CHUNK, CHUNK), F32),
                   jax.ShapeDtypeStruct((CHUNK, N_HEADS), F32),
                   jax.ShapeDtypeStruct((1, A_WIDTH), F32),
                   jax.ShapeDtypeStruct((1, B_WIDTH), F32),
                   jax.ShapeDtypeStruct((N_DEV, 4, rb, GROUP), F32)],
        sem=("arbitrary",))


def _adamw(w, g, m, v):
    m = ADAM_B1 * m + (1.0 - ADAM_B1) * g
    v = ADAM_B2 * v + (1.0 - ADAM_B2) * (g * g)
    m_hat = m / ADAM_C1
    v_hat = v / ADAM_C2
    delta = -ADAM_LR * (m_hat / (jnp.sqrt(v_hat) + ADAM_EPS) + ADAM_WD * w)
    return delta, m, v


PAIR_SUM_TILE_ELEMS = 1024 * 1024
ADAMW_TILE_ELEMS = 512 * 1024


def _row_tile(r, c, elems):
    t = r
    while t * c > elems and t % 32 == 0:
        t //= 2
    return t


def _pair_sum_call(name, pos, grad, got):
    _, r, c = grad.shape
    tr = _row_tile(r, c, PAIR_SUM_TILE_ELEMS)

    def chip_of(rel, pos_ref):
        px = jnp.where((rel == 0) | (rel == 2), 1 - pos_ref[0], pos_ref[0])
        py = jnp.where((rel == 1) | (rel == 2), 1 - pos_ref[1], pos_ref[1])
        return 2 * px + py

    def body(pos_ref, own_ref, got_ref, out_ref):
        out_ref[...] = (own_ref[...].astype(F32) + got_ref[...].astype(F32)).astype(BF16)

    return pl.pallas_call(
        body, name=name,
        grid_spec=pltpu.PrefetchScalarGridSpec(
            num_scalar_prefetch=1, grid=(3, r // tr),
            in_specs=[pl.BlockSpec((None, tr, c), lambda k, i, p: (2 * chip_of(k, p) + p[2], i, 0)),
                      pl.BlockSpec((None, tr, c), lambda k, i, p: (chip_of(k, p), i, 0))],
            out_specs=pl.BlockSpec((None, tr, c), lambda k, i, p: (k, i, 0))),
        out_shape=jax.ShapeDtypeStruct((3, r, c), BF16),
        compiler_params=_params("parallel", "parallel"),
    )(pos, grad, got)


def _final_call(name, pos, grad, got_pair, got_chips, w, m, v):
    _, r, c = grad.shape
    tr = _row_tile(r, c, ADAMW_TILE_ELEMS)

    def body(pos_ref, own_ref, pair_ref, chips_ref, w_ref, m_ref, v_ref, g_out, d_out, m_out, v_out):
        g = own_ref[...].astype(F32) + pair_ref[...].astype(F32)
        for j in range(3):
            g = g + chips_ref[j].astype(F32)
        delta, m_new, v_new = _adamw(w_ref[...], g, m_ref[...], v_ref[...])
        g_out[...] = g
        d_out[...] = delta
        m_out[...] = m_new
        v_out[...] = v_new

    row = pl.BlockSpec((tr, c), lambda i, p: (i, 0))
    return pl.pallas_call(
        body, name=name,
        grid_spec=pltpu.PrefetchScalarGridSpec(
            num_scalar_prefetch=1, grid=(r // tr,),
            in_specs=[pl.BlockSpec((None, tr, c), lambda i, p: (4 * p[0] + 2 * p[1] + p[2], i, 0)),
                      pl.BlockSpec((None, tr, c), lambda i, p: (2 * p[0] + p[1], i, 0)),
                      pl.BlockSpec((3, tr, c), lambda i, p: (0, i, 0)), row, row, row],
            out_specs=[row] * 4),
        out_shape=[jax.ShapeDtypeStruct((r, c), F32)] * 4,
        compiler_params=_params("parallel"),
    )(pos, grad, got_pair, got_chips, w, m, v)


def _small_final_call(name, parts, w, m, v):
    _, rows, c = parts.shape
    r = w.shape[0]

    def body(p_ref, w_ref, m_ref, v_ref, g_out, d_out, m_out, v_out):
        g = p_ref[0]
        for k in range(1, N_DEV):
            g = g + p_ref[k]
        delta, m_new, v_new = _adamw(w_ref[...], g[:r], m_ref[...], v_ref[...])
        g_out[...] = g
        d_out[...] = delta
        m_out[...] = m_new
        v_out[...] = v_new

    return pl.pallas_call(
        body, name=name,
        out_shape=[jax.ShapeDtypeStruct((rows, c), F32)] + [jax.ShapeDtypeStruct((r, c), F32)] * 3,
        compiler_params=pltpu.CompilerParams(vmem_limit_bytes=VMEM_LIMIT),
    )(parts, w, m, v)


_SMALL_EARLY = ("g_v", "w_s", "b_s", "pool_scale", "g_ffn", "g_final")
_BIG = ("w_in", "w_pool", "w_out", "w_up", "w_down")
_ORDER = ("g_mix", "w_in", "g_v", "w_s", "b_s", "w_pool", "pool_scale", "w_out", "g_ffn", "w_up", "w_down", "g_final")


def _pack(parts):
    return jnp.concatenate([p.reshape(-1, LANES) for p in parts], axis=0)


def _unpack(packed, like):
    out, row = [], 0
    for a in like:
        rows = a.size // LANES
        out.append(packed[row:row + rows].reshape(a.shape))
        row += rows
    return out


def kernel(x, g_mix, w_in, g_v, w_s, b_s, w_pool, pool_scale, w_out, g_ffn, w_up, w_down, g_final, loss_target, m_g_mix, m_w_in, m_g_v, m_w_s, m_b_s, m_w_pool, m_pool_scale, m_w_out, m_g_ffn, m_w_up, m_w_down, m_g_final, v_g_mix, v_w_in, v_g_v, v_w_s, v_b_s, v_w_pool, v_pool_scale, v_w_out, v_g_ffn, v_w_up, v_w_down, v_g_final):
    weights = dict(g_mix=g_mix, w_in=w_in, g_v=g_v, w_s=w_s, b_s=b_s, w_pool=w_pool, pool_scale=pool_scale,
                   w_out=w_out, g_ffn=g_ffn, w_up=w_up, w_down=w_down, g_final=g_final)
    mom = dict(g_mix=m_g_mix, w_in=m_w_in, g_v=m_g_v, w_s=m_w_s, b_s=m_b_s, w_pool=m_w_pool,
               pool_scale=m_pool_scale, w_out=m_w_out, g_ffn=m_g_ffn, w_up=m_w_up, w_down=m_w_down,
               g_final=m_g_final)
    var = dict(g_mix=v_g_mix, w_in=v_w_in, g_v=v_g_v, w_s=v_w_s, b_s=v_b_s, w_pool=v_w_pool,
               pool_scale=v_pool_scale, w_out=v_w_out, g_ffn=v_g_ffn, w_up=v_w_up, w_down=v_w_down,
               g_final=v_g_final)

    t, d = x.shape[1], x.shape[2]
    xs = x.reshape(t, d)
    target = loss_target.reshape(t, d)

    shard2d = dict(w_in=w_in.reshape(d, -1), w_pool=w_pool.reshape(-1, GROUP), w_out=w_out.reshape(-1, d),
                   w_up=w_up.reshape(d, -1), w_down=w_down.reshape(-1, d))
    sb = {k: shard2d[k].astype(BF16) for k in _BIG}
    rows = {k: sb[k].shape[0] for k in _BIG}

    def gathered_shape(k):
        return jax.ShapeDtypeStruct((N_DEV,) + sb[k].shape, BF16)

    def landing(n, like):
        return jax.ShapeDtypeStruct((n,) + like.shape[1:], like.dtype)

    def from_everyone(block):
        return jax.ShapeDtypeStruct((N_DEV,) + block.shape, block.dtype)

    def cuts(r, fractions):
        return [0] + [int(r * f) // 16 * 16 for f in fractions] + [r]

    g_mix2, g_ffn2, g_final2 = g_mix.reshape(1, d), g_ffn.reshape(1, d), g_final.reshape(1, d)
    g_v2, ps2 = g_v.reshape(1, A_WIDTH), pool_scale.reshape(1, B_WIDTH)
    w_s3 = w_s.reshape(N_HEADS, CHUNK, CHUNK)
    bs_t = b_s.reshape(N_HEADS, CHUNK).T
    xi, yi, ci = _position()
    pos = jnp.stack([xi, yi, ci]).astype(jnp.int32)

    w_in_g, w_pool_g = _comm_call("gather_w_in_w_pool", _Comm(
        [sb["w_in"], sb["w_pool"]], [], [gathered_shape("w_in"), gathered_shape("w_pool")],
        lambda s, l: [_gather_first(s[0], l[0], 0, rows["w_in"]) + _gather_first(s[1], l[1], 0, rows["w_pool"]),
                      _gather_pass_on(l[0], 0, rows["w_in"]) + _gather_pass_on(l[1], 0, rows["w_pool"])]))
    w_pool_g = w_pool_g.reshape(N_DEV, 4, GROUP // N_DEV, GROUP)

    u = cuts(rows["w_up"], (0.31, 0.47, 0.72))
    proj, h1, w_out_g, w_up_g = _proj_call(xs, g_mix2, w_in_g, _Comm(
        [sb["w_out"], sb["w_up"]], [], [gathered_shape("w_out"), gathered_shape("w_up")],
        lambda s, l: [_gather_first(s[0], l[0], 0, rows["w_out"]) + _gather_first(s[1], l[1], u[0], u[1])]))
    mixed, w_out_g, w_up_g = _mixer_fwd_call(proj, w_s3, bs_t, g_v2, w_pool_g, ps2, _Comm(
        [sb["w_up"]], [w_out_g, w_up_g], [],
        lambda s, l: [_gather_pass_on(l[0], 0, rows["w_out"]) + _gather_pass_on(l[1], u[0], u[1])
                      + _gather_first(s[0], l[1], u[1], u[2])]))
    w_out_f = w_out_g.reshape(-1, d)
    x2, w_up_g = _out_proj_call(mixed, w_out_f, xs, _Comm(
        [sb["w_up"]], [w_up_g], [],
        lambda s, l: [_gather_pass_on(l[0], u[1], u[2]) + _gather_first(s[0], l[0], u[2], u[3])]))
    (w_up_g,) = _comm_call("gather_w_up_rest", _Comm(
        [sb["w_up"]], [w_up_g], [],
        lambda s, l: [_gather_pass_on(l[0], u[2], u[3]) + _gather_first(s[0], l[0], u[3], u[4]),
                      _gather_pass_on(l[0], u[3], u[4])]))
    act, h2, w_down_g = _up_call(x2, g_ffn2, w_up_g, _Comm(
        [sb["w_down"]], [], [gathered_shape("w_down")],
        lambda s, l: [_gather_first(s[0], l[0], 0, rows["w_down"])]))
    (w_down_g,) = _comm_call("pass_on_w_down", _Comm(
        [], [w_down_g], [], lambda s, l: [_gather_pass_on(l[0], 0, rows["w_down"])]))
    w_down_f = w_down_g.reshape(-1, d)
    x3 = _down_call(act, w_down_f, x2)
    loss_part, dx3, dx3b, dg_final = _loss_call(x3, target, g_final2)

    def pair_sum(k, grad, got):
        return _pair_sum_call(k + "_pair_sum", pos, grad, got)

    def finish(k, grad, got_pair, got_chips):
        s = shard2d[k]
        outs = _final_call(k + "_adamw", pos, grad, got_pair, got_chips, s, mom[k].reshape(s.shape),
                           var[k].reshape(s.shape))
        return [o.reshape(weights[k].shape) for o in outs]

    result = {}
    (gw_down,) = _wgrad_call("w_down_grad", act, dx3b, None, None, t1=1024, t2=2048)
    gw_down = gw_down.reshape(N_DEV, -1, d)
    da, pair_down = _dact_call(dx3b, w_down_f, act, _Comm(
        [gw_down], [], [landing(4, gw_down)], lambda s, l: [_pair_exchange(s[0], l[0])]))
    sums_down = pair_sum("w_down", gw_down, pair_down)
    gw_up, got = _wgrad_call("w_up_grad", h2, da, N_DEV, w_up_g.shape[2], _Comm(
        [sums_down], [], [landing(3, sums_down)],
        lambda s, l: [_chip_exchange(s[0], l[0], 0, rows["w_down"])]), t1=2048)
    result["w_down"] = finish("w_down", gw_down, pair_down, got)
    dh2, pair_up = _dgrad_blocked_call("dh2_bwd", da, w_up_g, _Comm(
        [gw_up], [], [landing(4, gw_up)], lambda s, l: [_pair_exchange(s[0], l[0])]))
    sums_up = pair_sum("w_up", gw_up, pair_up)
    v = cuts(rows["w_up"], (0.28, 0.52, 0.84))
    dx2, dx2b, dg_ffn, got_up = _norm_bwd_call("ffn_norm_bwd", dh2, x2, dx3, g_ffn2, True, _Comm(
        [sums_up], [], [landing(3, sums_up)], lambda s, l: [_chip_exchange(s[0], l[0], v[0], v[1])]))
    dmixed, got_up = _dmixed_call(dx2b, w_out_f, _Comm(
        [sums_up], [got_up], [], lambda s, l: [_chip_exchange(s[0], l[0], v[1], v[2])]))
    gw_out, got_up = _wgrad_call("w_out_grad", mixed, dx2b, None, None, _Comm(
        [sums_up], [got_up], [], lambda s, l: [_chip_exchange(s[0], l[0], v[2], v[3])]), t1=2048, t2=1024)
    gw_out = gw_out.reshape(N_DEV, -1, d)
    dproj, dw_s, dbs_t, dg_v, dps, dw_pool, got_up, pair_out = _mixer_bwd_call(
        proj, dmixed, w_s3, bs_t, g_v2, w_pool_g, ps2, _Comm(
            [sums_up, gw_out], [got_up], [landing(4, gw_out)],
            lambda s, l: [_chip_exchange(s[0], l[0], v[3], v[4]) + _pair_exchange(s[1], l[1])]))
    result["w_up"] = finish("w_up", gw_up, pair_up, got_up)
    sums_out = pair_sum("w_out", gw_out, pair_out)
    gw_pool = dw_pool.astype(BF16).reshape(N_DEV, -1, GROUP)
    early = dict(g_v=dg_v, w_s=dw_s, b_s=dbs_t.T, pool_scale=dps, g_ffn=dg_ffn, g_final=dg_final)
    packed = _pack([early[k] for k in _SMALL_EARLY] + [loss_part])
    gw_in, got, pair_pool, parts_early = _wgrad_call("w_in_grad", h1, dproj, N_DEV, w_in_g.shape[2], _Comm(
        [sums_out, gw_pool, packed], [], [landing(3, sums_out), landing(4, gw_pool), from_everyone(packed)],
        lambda s, l: [_chip_exchange(s[0], l[0], 0, rows["w_out"]) + _pair_exchange(s[1], l[1])
                      + _everyone(s[2], l[2])]), t1=2048, merge=MERGE_W_IN)
    result["w_out"] = finish("w_out", gw_out, pair_out, got)
    sums_pool = pair_sum("w_pool", gw_pool, pair_pool)
    (pair_in,) = _comm_call("pair_exchange_w_in", _Comm(
        [gw_in], [], [landing(4, gw_in)], lambda s, l: [_pair_exchange(s[0], l[0])]))
    sums_in = pair_sum("w_in", gw_in, pair_in)
    dh1, got, got_pool = _dgrad_blocked_call("dh1_bwd", dproj, w_in_g, _Comm(
        [sums_in, sums_pool], [], [landing(3, sums_in), landing(3, sums_pool)],
        lambda s, l: [_chip_exchange(s[0], l[0], 0, rows["w_in"]) + _chip_exchange(s[1], l[1], 0, rows["w_pool"])]),
        merge=MERGE_W_IN)
    result["w_in"] = finish("w_in", gw_in, pair_in, got)
    result["w_pool"] = finish("w_pool", gw_pool, pair_pool, got_pool)
    grad_x, dg_mix = _norm_bwd_call("mix_norm_bwd", dh1, xs, dx2, g_mix2, False)
    packed = _pack([dg_mix])
    (parts_late,) = _comm_call("gather_g_mix_grad", _Comm(
        [packed], [], [from_everyone(packed)], lambda s, l: [_everyone(s[0], l[0])]))

    for names, parts, tag in ((_SMALL_EARLY, parts_early, "small_adamw"), (("g_mix",), parts_late, "g_mix_adamw")):
        outs = _small_final_call(tag, parts, _pack([weights[k] for k in names]), _pack([mom[k] for k in names]),
                                 _pack([var[k] for k in names]))
        if tag == "small_adamw":
            loss = outs[0][-1, 0]
        like = [weights[k] for k in names]
        unpacked = [_unpack(o, like) for o in outs]
        for idx, k in enumerate(names):
            result[k] = [unpacked[q][idx] for q in range(4)]

    grads = [result[k][0] for k in _ORDER]
    deltas = [result[k][1] for k in _ORDER]
    new_m = [result[k][2] for k in _ORDER]
    new_v = [result[k][3] for k in _ORDER]
    return (loss, grad_x.reshape(x.shape), *grads, *deltas, *new_m, *new_v)
```

```python
import functools
import math

import jax
import jax.numpy as jnp
from jax import lax
from jax.experimental import pallas as pl
from jax.experimental.pallas import tpu as pltpu

F32 = jnp.float32
BF16 = jnp.bfloat16
MESH = pl.DeviceIdType.MESH

N_DEV = 8
EPS = 1e-6
CHUNK = 128
N_HEADS = 8
A_WIDTH = 1024
B_WIDTH = 1024
POOL_WINDOWS = (2, 4, 8, 16)
GROUP = 256
HALO = 16
LANES = 128

ADAM_LR = 0.001
ADAM_B1 = 0.9
ADAM_B2 = 0.999
ADAM_EPS = 1e-08
ADAM_WD = 0.01
ADAM_STEP = 10
ADAM_C1 = 1.0 - ADAM_B1 ** ADAM_STEP
ADAM_C2 = 1.0 - ADAM_B2 ** ADAM_STEP

VMEM_LIMIT = 56 * 1024 * 1024
EPILOGUE_COLS = 256
MERGE_W_IN = 2

_GELU_C = math.sqrt(2.0 / math.pi)


def _params(*sem):
    return pltpu.CompilerParams(dimension_semantics=sem, vmem_limit_bytes=VMEM_LIMIT)


def _gelu(x):
    return 0.5 * x * (1.0 + jnp.tanh(_GELU_C * (x + 0.044715 * x * x * x)))


def _gelu_and_grad(x):
    t = jnp.tanh(_GELU_C * (x + 0.044715 * x * x * x))
    g = 0.5 * x * (1.0 + t)
    dg = 0.5 * (1.0 + t) + 0.5 * x * (1.0 - t * t) * (_GELU_C * (1.0 + 3.0 * 0.044715 * x * x))
    return g, dg


def _dot_nn(a, b):
    return lax.dot_general(a, b, (((1,), (0,)), ((), ())), preferred_element_type=F32)


def _dot_nt(a, b):
    return lax.dot_general(a, b, (((1,), (1,)), ((), ())), preferred_element_type=F32)


def _dot_tn(a, b):
    return lax.dot_general(a, b, (((0,), (0,)), ((), ())), preferred_element_type=F32)


def _rms_rows(x):
    r = lax.rsqrt(jnp.mean(x * x, axis=-1, keepdims=True) + EPS)
    return x * r, r


def _rms_bwd_rows(dn, n, r):
    return r * (dn - n * jnp.mean(dn * n, axis=-1, keepdims=True))


def _tile(n, want):
    t = min(n, want)
    assert n % t == 0, (n, want)
    return t


_ANY = pl.BlockSpec(memory_space=pl.ANY)

SIBLING = 1
CHIPS = (4, 2, 6)


def _position():
    return lax.axis_index("x"), lax.axis_index("y"), lax.axis_index("c")


def _me():
    x, y, c = _position()
    return 4 * x + 2 * y + c


def _peer(rel):
    x, y, c = _position()
    return (x ^ ((rel >> 2) & 1), y ^ ((rel >> 1) & 1), c ^ (rel & 1))


class _Comm:
    def __init__(self, srcs, lands, new, plan):
        self.srcs, self.lands, self.new, self.plan = list(srcs), list(lands), list(new), plan


def _make_copies(phases, send_sems, recv_sems, local_sems):
    out, nr, nl = [], 0, 0
    for phase in phases:
        cps = []
        for item in phase:
            if item[0] == "local":
                cps.append(pltpu.make_async_copy(item[1], item[2], local_sems.at[nl]))
                nl += 1
            else:
                cps.append(pltpu.make_async_remote_copy(
                    src_ref=item[1], dst_ref=item[2], send_sem=send_sems.at[nr], recv_sem=recv_sems.at[nr],
                    device_id=_peer(item[3]), device_id_type=MESH))
                nr += 1
        out.append(cps)
    return out


def _count_copies(comm):
    phases = comm.plan([_FakeRef() for _ in comm.srcs], [_FakeRef() for _ in range(len(comm.lands) + len(comm.new))])
    items = [it for ph in phases for it in ph]
    return sum(it[0] == "remote" for it in items), sum(it[0] == "local" for it in items)


class _FakeRef:
    def __getitem__(self, idx):
        return self

    @property
    def at(self):
        return self


def _carrier_call(body, args, comm, *, name, grid, in_specs, out_specs, out_shape, scratch_shapes=(), sem):
    if not isinstance(out_shape, (list, tuple)):
        out_specs, out_shape = [out_specs], [out_shape]
    out_specs, out_shape, scratch_shapes = list(out_specs), list(out_shape), list(scratch_shapes)
    if comm is None:
        res = pl.pallas_call(body, name=name, grid=grid, in_specs=list(in_specs), out_specs=out_specs,
                             out_shape=out_shape, scratch_shapes=scratch_shapes, compiler_params=_params(*sem))(*args)
        return list(res)
    n_in, n_out, n_scr = len(args), len(out_shape), len(scratch_shapes)
    ns, nl, nn = len(comm.srcs), len(comm.lands), len(comm.new)
    n_remote, n_local = _count_copies(comm)

    def wrapped(*refs):
        ins, srcs = refs[:n_in], refs[n_in:n_in + ns]
        o = n_in + ns + nl
        outs, lands = refs[o:o + n_out], refs[o + n_out:o + n_out + nl + nn]
        scr = refs[o + n_out + nl + nn:]
        (copies,) = _make_copies(comm.plan(srcs, lands), *scr[n_scr:])
        ids = [pl.program_id(a) for a in range(len(grid))]
        first = functools.reduce(jnp.logical_and, [i == 0 for i in ids])
        last = functools.reduce(jnp.logical_and, [i == g - 1 for i, g in zip(ids, grid)])

        @pl.when(first)
        def _():
            for cp in copies:
                cp.start()

        body(*ins, *outs, *scr[:n_scr])

        @pl.when(last)
        def _():
            for cp in copies:
                cp.wait()

    land_shapes = [jax.ShapeDtypeStruct(a.shape, a.dtype) for a in comm.lands] + comm.new
    sems = [pltpu.SemaphoreType.DMA((max(n_remote, 1),)), pltpu.SemaphoreType.DMA((max(n_remote, 1),)),
            pltpu.SemaphoreType.DMA((max(n_local, 1),))]
    res = pl.pallas_call(
        wrapped, name=name, grid=grid,
        in_specs=list(in_specs) + [_ANY] * (ns + nl), out_specs=out_specs + [_ANY] * (nl + nn),
        out_shape=out_shape + land_shapes, scratch_shapes=scratch_shapes + sems,
        input_output_aliases={n_in + ns + k: n_out + k for k in range(nl)},
        compiler_params=_params(*sem))(*args, *comm.srcs, *comm.lands)
    return list(res)


def _comm_call(name, comm):
    ns, nl, nn = len(comm.srcs), len(comm.lands), len(comm.new)
    n_remote, n_local = _count_copies(comm)

    def body(*refs):
        srcs, lands, sems = refs[:ns], refs[ns + nl:ns + nl + nl + nn], refs[ns + nl + nl + nn:]
        for copies in _make_copies(comm.plan(srcs, lands), *sems):
            for cp in copies:
                cp.start()
            for cp in copies:
                cp.wait()

    land_shapes = [jax.ShapeDtypeStruct(a.shape, a.dtype) for a in comm.lands] + comm.new
    res = pl.pallas_call(
        body, name=name,
        in_specs=[_ANY] * (ns + nl), out_specs=[_ANY] * (nl + nn), out_shape=land_shapes,
        scratch_shapes=[pltpu.SemaphoreType.DMA((max(n_remote, 1),)), pltpu.SemaphoreType.DMA((max(n_remote, 1),)),
                        pltpu.SemaphoreType.DMA((max(n_local, 1),))],
        input_output_aliases={ns + k: k for k in range(nl)},
    )(*comm.srcs, *comm.lands)
    return list(res)


def _rows(ref, block, r0, r1):
    return ref.at[block, pl.ds(r0, r1 - r0)]


def _gather_first(shard, land, r0, r1):
    src = shard.at[pl.ds(r0, r1 - r0)]
    dst = _rows(land, _me(), r0, r1)
    return [("local", src, dst)] + [("remote", src, dst, rel) for rel in (SIBLING,) + CHIPS]


def _gather_pass_on(land, r0, r1):
    return [("remote", _rows(land, _me() ^ rel, r0, r1), _rows(land, _me() ^ rel, r0, r1), SIBLING) for rel in CHIPS]


def _pair_exchange(grad, land):
    _, _, c = _position()
    return [("remote", grad.at[2 * chip + (1 - c)], land.at[chip], SIBLING) for chip in range(4)]


def _chip_exchange(sums, land, r0, r1):
    return [("remote", _rows(sums, j, r0, r1), _rows(land, j, r0, r1), rel) for j, rel in enumerate(CHIPS)]


def _everyone(packed, land):
    dst = land.at[_me()]
    return [("local", packed, dst)] + [("remote", packed, dst, rel) for rel in range(1, N_DEV)]


def _proj_call(x, g_mix, w_in_g, comm=None):
    t, d = x.shape
    nb, _, cb = w_in_g.shape
    tm = _tile(t, 1024)
    mg = MERGE_W_IN

    def body(x_ref, g_ref, w_ref, proj_ref, h_ref):
        @pl.when(pl.program_id(1) == 0)
        def _():
            n, _ = _rms_rows(x_ref[...])
            h_ref[...] = (n * g_ref[...]).astype(BF16)

        w = jnp.concatenate([w_ref[b] for b in range(mg)], axis=1)
        proj_ref[...] = _dot_nn(h_ref[...], w)

    return _carrier_call(
        body, (x, g_mix, w_in_g), comm, name="proj_fwd",
        grid=(t // tm, nb // mg),
        in_specs=[pl.BlockSpec((tm, d), lambda i, j: (i, 0)),
                  pl.BlockSpec((1, d), lambda i, j: (0, 0)),
                  pl.BlockSpec((mg, d, cb), lambda i, j: (j, 0, 0))],
        out_specs=[pl.BlockSpec((tm, mg * cb), lambda i, j: (i, j)),
                   pl.BlockSpec((tm, d), lambda i, j: (i, 0))],
        out_shape=[jax.ShapeDtypeStruct((t, nb * cb), F32), jax.ShapeDtypeStruct((t, d), BF16)],
        sem=("parallel", "arbitrary"))


def _pool_counts(row0, rows, win):
    pos = row0 + lax.broadcasted_iota(jnp.int32, (rows, 1), 0)
    return jnp.minimum(pos + 1, win).astype(F32)


def _window_sum_back(ext, win):
    s = ext
    k = 1
    while k < win:
        s = s + pltpu.roll(s, k, 0)
        k *= 2
    return s


def _window_sum_fwd(ext, win):
    n = ext.shape[0]
    s = ext
    k = 1
    while k < win:
        s = s + pltpu.roll(s, n - k, 0)
        k *= 2
    return s


def _mixer_fwd_call(proj, w_s, bs_t, g_v, w_pool_g, pool_scale, comm=None):
    t = proj.shape[0]
    tt = _tile(t, 512)
    nchunk = tt // CHUNK
    hb = tt // HALO

    def body(pu_ref, pv_ref, z_ref, zp_ref, ws_ref, bs_ref, gv_ref, wp_ref, ps_ref, out_ref):
        i = pl.program_id(0)
        tril = (lax.broadcasted_iota(jnp.int32, (CHUNK, CHUNK), 0)
                >= lax.broadcasted_iota(jnp.int32, (CHUNK, CHUNK), 1))
        for h in range(N_HEADS):
            cols = slice(h * CHUNK, (h + 1) * CHUNK)
            vhat, _ = _rms_rows(_gelu(pv_ref[:, cols]))
            vn = (vhat * gv_ref[:, cols]).astype(BF16)
            u = _gelu(pu_ref[:, cols])
            w = jnp.where(tril, ws_ref[h], 0.0).astype(BF16)
            bcol = bs_ref[:, h:h + 1]
            for c in range(nchunk):
                rows = slice(c * CHUNK, (c + 1) * CHUNK)
                mixed = _dot_nn(w, vn[rows]) + bcol
                out_ref[rows, cols] = (u[rows] * mixed).astype(BF16)

        zprev = jnp.where(i > 0, zp_ref[...], 0.0)
        ext = jnp.concatenate([zprev, z_ref[...]], axis=0)
        for g, win in enumerate(POOL_WINDOWS):
            cols = slice(g * GROUP, (g + 1) * GROUP)
            zg = ext[:, cols]
            s = _window_sum_back(zg, win)
            pooled = s[HALO:] / _pool_counts(i * tt, tt, win) - zg[HALO:]
            wp = wp_ref[:, g].reshape(GROUP, GROUP)
            y = _dot_nn(pooled.astype(BF16), wp)
            out_ref[:, A_WIDTH + g * GROUP:A_WIDTH + (g + 1) * GROUP] = (y * ps_ref[:, cols]).astype(BF16)

    return _carrier_call(
        body, (proj, proj, proj, proj, w_s, bs_t, g_v, w_pool_g, pool_scale), comm, name="mixer_fwd",
        grid=(t // tt,),
        in_specs=[pl.BlockSpec((tt, A_WIDTH), lambda i: (i, 0)),
                  pl.BlockSpec((tt, A_WIDTH), lambda i: (i, 1)),
                  pl.BlockSpec((tt, B_WIDTH), lambda i: (i, 2)),
                  pl.BlockSpec((HALO, B_WIDTH), lambda i: (jnp.maximum(i * hb - 1, 0), 2)),
                  pl.BlockSpec((N_HEADS, CHUNK, CHUNK), lambda i: (0, 0, 0)),
                  pl.BlockSpec((CHUNK, N_HEADS), lambda i: (0, 0)),
                  pl.BlockSpec((1, A_WIDTH), lambda i: (0, 0)),
                  pl.BlockSpec((N_DEV, 4, GROUP // N_DEV, GROUP), lambda i: (0, 0, 0, 0)),
                  pl.BlockSpec((1, B_WIDTH), lambda i: (0, 0))],
        out_specs=pl.BlockSpec((tt, A_WIDTH + B_WIDTH), lambda i: (i, 0)),
        out_shape=jax.ShapeDtypeStruct((t, A_WIDTH + B_WIDTH), BF16),
        sem=("parallel",))


def _out_proj_call(mixed, w_out, x, comm=None):
    t, d = x.shape
    k = mixed.shape[1]
    tm = _tile(t, 1024)
    tn = _tile(d, 1024)

    def body(a_ref, w_ref, x_ref, o_ref):
        o_ref[...] = x_ref[...] + _dot_nn(a_ref[...], w_ref[...])

    return _carrier_call(
        body, (mixed, w_out, x), comm, name="out_proj_fwd",
        grid=(t // tm, d // tn),
        in_specs=[pl.BlockSpec((tm, k), lambda i, j: (i, 0)),
                  pl.BlockSpec((k, tn), lambda i, j: (0, j)),
                  pl.BlockSpec((tm, tn), lambda i, j: (i, j))],
        out_specs=pl.BlockSpec((tm, tn), lambda i, j: (i, j)),
        out_shape=jax.ShapeDtypeStruct((t, d), F32),
        sem=("parallel", "parallel"))


ARRIVAL_ORDER = (0, 1, 4, 5, 2, 3, 6, 7)


class _StreamedGather:
    def __init__(self, shard_ref, land_ref, wbuf, r0, send_sems, recv_sems, local_sem, fetch_sems):
        self.shard, self.land, self.wbuf, self.fetch_sems = shard_ref, land_ref, wbuf, fetch_sems
        n = shard_ref.shape[0] - r0
        me = _me()
        self.me = me

        def remote(k, src, dst, rel):
            return pltpu.make_async_remote_copy(src_ref=src, dst_ref=dst, send_sem=send_sems.at[k],
                                                recv_sem=recv_sems.at[k], device_id=_peer(rel), device_id_type=MESH)

        src = shard_ref.at[pl.ds(r0, n)]
        dst = land_ref.at[me, pl.ds(r0, n)]
        self.mine = pltpu.make_async_copy(src, dst, local_sem)
        self.first = [remote(k, src, dst, rel) for k, rel in enumerate((SIBLING,) + CHIPS)]
        self.passed = []
        for j, rel in enumerate(CHIPS):
            rows = land_ref.at[me ^ rel, pl.ds(r0, n)]
            self.passed.append(remote(4 + j, rows, rows, SIBLING))

    def _fetch(self, q):
        src = self.shard if q == 0 else self.land.at[self.me ^ ARRIVAL_ORDER[q]]
        return pltpu.make_async_copy(src, self.wbuf.at[q % 2], self.fetch_sems.at[q % 2])

    def start(self):
        self.mine.start()
        for cp in self.first:
            cp.start()
        self._fetch(0).start()

    def arrive(self, q):
        if q == 1:
            self.first[0].wait_recv()
        elif q % 2 == 0:
            j = q // 2 - 1
            self.first[1 + j].wait_recv()
            self.passed[j].start()
        else:
            self.passed[q // 2 - 1].wait_recv()
        self._fetch(q).start()

    def wait_fetch(self, slot):
        pltpu.make_async_copy(self.shard, self.wbuf.at[slot], self.fetch_sems.at[slot]).wait()

    def finish(self):
        for cp in self.first + self.passed:
            cp.wait_send()
        self.mine.wait()


_STREAM_SEMS = [pltpu.SemaphoreType.DMA((7,)), pltpu.SemaphoreType.DMA((7,)), pltpu.SemaphoreType.DMA,
                pltpu.SemaphoreType.DMA((2,))]


def _stream_steps(gather, p, i, ni):
    @pl.when((p == 0) & (i == 0))
    def _():
        gather.start()

    @pl.when(i == 0)
    def _():
        gather.wait_fetch(p % 2)

    @pl.when(i == ni - 1)
    def _():
        for q in range(1, N_DEV):
            @pl.when(p == q - 1)
            def _():
                gather.arrive(q)


def _up_call(x2, g_ffn, shard, order, land, r0):
    t, d = x2.shape
    fb = shard.shape[1]
    tm = _tile(t, 1024)
    ni = t // tm

    def body(order_ref, x_ref, g_ref, shard_ref, land_in_ref, act_ref, h_ref, land_ref, wbuf, *sems):
        p, i = pl.program_id(0), pl.program_id(1)
        gather = _StreamedGather(shard_ref, land_ref, wbuf, r0, *sems)
        rows = pl.ds(pl.multiple_of(i * tm, tm), tm)
        _stream_steps(gather, p, i, ni)

        @pl.when(p == 0)
        def _():
            n, _ = _rms_rows(x_ref[...])
            h_ref[rows, :] = (n * g_ref[...]).astype(BF16)

        h = h_ref[rows, :]
        for c in range(fb // EPILOGUE_COLS):
            cols = slice(c * EPILOGUE_COLS, (c + 1) * EPILOGUE_COLS)
            a = jnp.maximum(_dot_nn(h, wbuf[p % 2, :, cols]), 0.0)
            act_ref[:, cols] = (a * a).astype(BF16)

        @pl.when((p == N_DEV - 1) & (i == ni - 1))
        def _():
            gather.finish()

    return pl.pallas_call(
        body, name="up_fwd",
        grid_spec=pltpu.PrefetchScalarGridSpec(
            num_scalar_prefetch=1, grid=(N_DEV, ni),
            in_specs=[pl.BlockSpec((tm, d), lambda p, i, o: (jnp.where(p == 0, i, ni - 1), 0)),
                      pl.BlockSpec((1, d), lambda p, i, o: (0, 0)),
                      _ANY, _ANY],
            out_specs=[pl.BlockSpec((tm, fb), lambda p, i, o: (i, o[p])),
                       pl.BlockSpec(memory_space=pltpu.VMEM),
                       _ANY],
            scratch_shapes=[pltpu.VMEM((2, d, fb), BF16)] + _STREAM_SEMS),
        out_shape=[jax.ShapeDtypeStruct((t, N_DEV * fb), BF16), jax.ShapeDtypeStruct((t, d), BF16),
                   jax.ShapeDtypeStruct((N_DEV, d, fb), BF16)],
        input_output_aliases={4: 2},
        compiler_params=_params("arbitrary", "arbitrary"),
    )(order, x2, g_ffn, shard, land)


def _down_call(act, shard, order):
    t = act.shape[0]
    rb, d = shard.shape
    tm = _tile(t, 1024)
    ni = t // tm

    def body(order_ref, a_ref, shard_ref, y_ref, land_ref, wbuf, *sems):
        p, i = pl.program_id(0), pl.program_id(1)
        gather = _StreamedGather(shard_ref, land_ref, wbuf, 0, *sems)
        rows = pl.ds(pl.multiple_of(i * tm, tm), tm)
        _stream_steps(gather, p, i, ni)
        part = _dot_nn(a_ref[...], wbuf[p % 2])

        @pl.when(p == 0)
        def _():
            y_ref[rows, :] = part

        @pl.when(p > 0)
        def _():
            y_ref[rows, :] += part

        @pl.when((p == N_DEV - 1) & (i == ni - 1))
        def _():
            gather.finish()

    return pl.pallas_call(
        body, name="down_fwd",
        grid_spec=pltpu.PrefetchScalarGridSpec(
            num_scalar_prefetch=1, grid=(N_DEV, ni),
            in_specs=[pl.BlockSpec((tm, rb), lambda p, i, o: (i, o[p])), _ANY],
            out_specs=[pl.BlockSpec(memory_space=pltpu.VMEM), _ANY],
            scratch_shapes=[pltpu.VMEM((2, rb, d), BF16)] + _STREAM_SEMS),
        out_shape=[jax.ShapeDtypeStruct((t, d), F32), jax.ShapeDtypeStruct((N_DEV, rb, d), BF16)],
        compiler_params=_params("arbitrary", "arbitrary"),
    )(order, act, shard)


def _loss_call(y, x2, target, g_final):
    t, d = y.shape
    tr = _tile(t, 256)

    def body(y_ref, x_ref, tg_ref, g_ref, loss_ref, dx_ref, dxb_ref, dg_ref):
        @pl.when(pl.program_id(0) == 0)
        def _():
            loss_ref[...] = jnp.zeros_like(loss_ref)
            dg_ref[...] = jnp.zeros_like(dg_ref)

        n, r = _rms_rows(x_ref[...] + y_ref[...])
        err = n * g_ref[...] - tg_ref[...]
        loss_ref[...] += 0.5 * jnp.sum(jnp.mean(err * err, axis=-1, keepdims=True))
        dy = err * (1.0 / d)
        dg_ref[...] += jnp.sum(dy * n, axis=0, keepdims=True)
        dx = _rms_bwd_rows(dy * g_ref[...], n, r)
        dx_ref[...] = dx
        dxb_ref[...] = dx.astype(BF16)

    return pl.pallas_call(
        body, name="loss_head",
        grid=(t // tr,),
        in_specs=[pl.BlockSpec((tr, d), lambda i: (i, 0)),
                  pl.BlockSpec((tr, d), lambda i: (i, 0)),
                  pl.BlockSpec((tr, d), lambda i: (i, 0)),
                  pl.BlockSpec((1, d), lambda i: (0, 0))],
        out_specs=[pl.BlockSpec((8, LANES), lambda i: (0, 0)),
                   pl.BlockSpec((tr, d), lambda i: (i, 0)),
                   pl.BlockSpec((tr, d), lambda i: (i, 0)),
                   pl.BlockSpec((1, d), lambda i: (0, 0))],
        out_shape=[jax.ShapeDtypeStruct((8, LANES), F32), jax.ShapeDtypeStruct((t, d), F32),
                   jax.ShapeDtypeStruct((t, d), BF16), jax.ShapeDtypeStruct((1, d), F32)],
        compiler_params=_params("arbitrary"),
    )(y, x2, target, g_final)


def _norm_bwd_call(name, dh, x, dres, g, want_bf16, comm=None):
    t, d = x.shape
    tr = _tile(t, 256)

    def body(dh_ref, x_ref, dres_ref, g_ref, dx_ref, *rest):
        dg_ref = rest[-1]

        @pl.when(pl.program_id(0) == 0)
        def _():
            dg_ref[...] = jnp.zeros_like(dg_ref)

        n, r = _rms_rows(x_ref[...])
        dh = dh_ref[...]
        dg_ref[...] += jnp.sum(dh * n, axis=0, keepdims=True)
        dx = dres_ref[...] + _rms_bwd_rows(dh * g_ref[...], n, r)
        dx_ref[...] = dx
        if want_bf16:
            rest[0][...] = dx.astype(BF16)

    row = pl.BlockSpec((tr, d), lambda i: (i, 0))
    vec = pl.BlockSpec((1, d), lambda i: (0, 0))
    out_specs = [row] + ([row] if want_bf16 else []) + [vec]
    out_shape = ([jax.ShapeDtypeStruct((t, d), F32)]
                 + ([jax.ShapeDtypeStruct((t, d), BF16)] if want_bf16 else [])
                 + [jax.ShapeDtypeStruct((1, d), F32)])
    return _carrier_call(
        body, (dh, x, dres, g), comm, name=name,
        grid=(t // tr,),
        in_specs=[row, row, row, vec],
        out_specs=out_specs, out_shape=out_shape,
        sem=("arbitrary",))


def _dact_call(dx3b, w_down, act, comm=None):
    t, d = dx3b.shape
    f = w_down.shape[0]
    tm = _tile(t, 1024)
    tn = _tile(f, 1024)

    def body(g_ref, w_ref, act_ref, o_ref):
        for c in range(tn // EPILOGUE_COLS):
            cols = slice(c * EPILOGUE_COLS, (c + 1) * EPILOGUE_COLS)
            dact = _dot_nt(g_ref[...], w_ref[cols, :])
            o_ref[:, cols] = (dact * (2.0 * jnp.sqrt(act_ref[:, cols].astype(F32)))).astype(BF16)

    return _carrier_call(
        body, (dx3b, w_down, act), comm, name="dact_bwd",
        grid=(t // tm, f // tn),
        in_specs=[pl.BlockSpec((tm, d), lambda i, j: (i, 0)),
                  pl.BlockSpec((tn, d), lambda i, j: (j, 0)),
                  pl.BlockSpec((tm, tn), lambda i, j: (i, j))],
        out_specs=pl.BlockSpec((tm, tn), lambda i, j: (i, j)),
        out_shape=jax.ShapeDtypeStruct((t, f), BF16),
        sem=("parallel", "parallel"))


def _wgrad_call(name, a, b, out_blocks, out_block_cols, comm=None, *, t1, t2=None, merge=1):
    t, k1 = a.shape
    k2 = b.shape[1]
    tt = _tile(t, 1024)
    t1 = _tile(k1, t1)
    t2 = _tile(k2, t2) if out_blocks is None else merge * out_block_cols
    nk = t // tt

    def body(a_ref, b_ref, o_ref, acc_ref):
        k = pl.program_id(2)

        @pl.when(k == 0)
        def _():
            acc_ref[...] = jnp.zeros_like(acc_ref)

        acc_ref[...] += _dot_tn(a_ref[...], b_ref[...])

        @pl.when(k == nk - 1)
        def _():
            if out_blocks is None:
                o_ref[...] = acc_ref[...].astype(BF16)
            else:
                for blk in range(merge):
                    o_ref[blk] = acc_ref[:, blk * out_block_cols:(blk + 1) * out_block_cols].astype(BF16)

    if out_blocks is None:
        out_spec = pl.BlockSpec((t1, t2), lambda i, j, k: (i, j))
        out_shape = jax.ShapeDtypeStruct((k1, k2), BF16)
    else:
        out_spec = pl.BlockSpec((merge, t1, out_block_cols), lambda i, j, k: (j, i, 0))
        out_shape = jax.ShapeDtypeStruct((out_blocks, k1, out_block_cols), BF16)
    return _carrier_call(
        body, (a, b), comm, name=name,
        grid=(k1 // t1, k2 // t2, nk),
        in_specs=[pl.BlockSpec((tt, t1), lambda i, j, k: (k, i)),
                  pl.BlockSpec((tt, t2), lambda i, j, k: (k, j))],
        out_specs=out_spec, out_shape=out_shape,
        scratch_shapes=[pltpu.VMEM((t1, t2), F32)],
        sem=("parallel", "parallel", "arbitrary"))


def _dgrad_blocked_call(name, g, w_g, comm=None, *, merge=1):
    t = g.shape[0]
    nb, d, cb = w_g.shape
    tm = _tile(t, 1024)
    tn = _tile(d, 2048)
    tk = merge * cb

    def body(g_ref, w_ref, o_ref):
        @pl.when(pl.program_id(2) == 0)
        def _():
            o_ref[...] = jnp.zeros_like(o_ref)

        w = w_ref[0] if merge == 1 else jnp.concatenate([w_ref[b] for b in range(merge)], axis=1)
        o_ref[...] += _dot_nt(g_ref[...], w)

    return _carrier_call(
        body, (g, w_g), comm, name=name,
        grid=(t // tm, d // tn, nb // merge),
        in_specs=[pl.BlockSpec((tm, tk), lambda i, j, k: (i, k)),
                  pl.BlockSpec((merge, tn, cb), lambda i, j, k: (k, j, 0))],
        out_specs=pl.BlockSpec((tm, tn), lambda i, j, k: (i, j)),
        out_shape=jax.ShapeDtypeStruct((t, d), F32),
        sem=("parallel", "parallel", "arbitrary"))


def _dmixed_call(dx2b, w_out, comm=None):
    t, d = dx2b.shape
    e = w_out.shape[0]
    tm = _tile(t, 1024)
    tn = _tile(e, 1024)

    def body(g_ref, w_ref, o_ref):
        o_ref[...] = _dot_nt(g_ref[...], w_ref[...])

    return _carrier_call(
        body, (dx2b, w_out), comm, name="dmixed_bwd",
        grid=(t // tm, e // tn),
        in_specs=[pl.BlockSpec((tm, d), lambda i, j: (i, 0)),
                  pl.BlockSpec((tn, d), lambda i, j: (j, 0))],
        out_specs=pl.BlockSpec((tm, tn), lambda i, j: (i, j)),
        out_shape=jax.ShapeDtypeStruct((t, e), F32),
        sem=("parallel", "parallel"))


def _mixer_bwd_call(proj, dmixed, w_s, bs_t, g_v, w_pool_g, pool_scale, comm=None):
    t = proj.shape[0]
    tt = _tile(t, 512)
    nchunk = tt // CHUNK
    hb = tt // HALO
    last_halo = t // HALO - 1
    nsteps = t // tt
    rb = GROUP // N_DEV

    def body(pu_ref, pv_ref, z_ref, zp_ref, da_ref, db_ref, dbn_ref, ws_ref, bs_ref, gv_ref, wp_ref, ps_ref,
             dproj_ref, dws_ref, dbs_ref, dgv_ref, dps_ref, dwp_ref):
        i = pl.program_id(0)

        @pl.when(i == 0)
        def _():
            dws_ref[...] = jnp.zeros_like(dws_ref)
            dbs_ref[...] = jnp.zeros_like(dbs_ref)
            dgv_ref[...] = jnp.zeros_like(dgv_ref)
            dps_ref[...] = jnp.zeros_like(dps_ref)
            dwp_ref[...] = jnp.zeros_like(dwp_ref)

        tril = (lax.broadcasted_iota(jnp.int32, (CHUNK, CHUNK), 0)
                >= lax.broadcasted_iota(jnp.int32, (CHUNK, CHUNK), 1))
        for h in range(N_HEADS):
            cols = slice(h * CHUNK, (h + 1) * CHUNK)
            v, dv_dpv = _gelu_and_grad(pv_ref[:, cols])
            vhat, rv = _rms_rows(v)
            gv = gv_ref[:, cols]
            vn = (vhat * gv).astype(BF16)
            u, du_dpu = _gelu_and_grad(pu_ref[:, cols])
            w = jnp.where(tril, ws_ref[h], 0.0).astype(BF16)
            bcol = bs_ref[:, h:h + 1]
            dout = da_ref[:, cols]
            dmix = dout * u
            dmix_b = dmix.astype(BF16)
            dws = jnp.zeros((CHUNK, CHUNK), F32)
            dbs = jnp.zeros((CHUNK, 1), F32)
            dvn_parts = []
            du_parts = []
            for c in range(nchunk):
                rows = slice(c * CHUNK, (c + 1) * CHUNK)
                mixed = _dot_nn(w, vn[rows]) + bcol
                du_parts.append(dout[rows] * mixed)
                dvn_parts.append(_dot_tn(w, dmix_b[rows]))
                dws = dws + _dot_nt(dmix_b[rows], vn[rows])
                dbs = dbs + jnp.sum(dmix[rows], axis=1, keepdims=True)
            dws_ref[h] += jnp.where(tril, dws, 0.0)
            dbs_ref[:, h:h + 1] += dbs
            dvn = jnp.concatenate(dvn_parts, axis=0)
            du = jnp.concatenate(du_parts, axis=0)
            dgv_ref[:, cols] += jnp.sum(dvn * vhat, axis=0, keepdims=True)
            dv = _rms_bwd_rows(dvn * gv, vhat, rv)
            dproj_ref[:, cols] = (du * du_dpu).astype(BF16)
            dproj_ref[:, A_WIDTH + h * CHUNK:A_WIDTH + (h + 1) * CHUNK] = (dv * dv_dpv).astype(BF16)

        zprev = jnp.where(i > 0, zp_ref[...], 0.0)
        ext = jnp.concatenate([zprev, z_ref[...]], axis=0)
        dnext = jnp.where(i < nsteps - 1, dbn_ref[...], 0.0)
        dext = jnp.concatenate([db_ref[...], dnext], axis=0)
        for g, win in enumerate(POOL_WINDOWS):
            cols = slice(g * GROUP, (g + 1) * GROUP)
            zg = ext[:, cols]
            pooled = _window_sum_back(zg, win)[HALO:] / _pool_counts(i * tt, tt, win) - zg[HALO:]
            pooled_b = pooled.astype(BF16)
            wp = wp_ref[:, g].reshape(GROUP, GROUP)
            y = _dot_nn(pooled_b, wp)
            dout = dext[:, cols]
            dps_ref[:, cols] += jnp.sum(dout[:tt] * y, axis=0, keepdims=True)
            dy_b = (dout * ps_ref[:, cols]).astype(BF16)
            dwp_ref[:, g] += _dot_tn(pooled_b, dy_b[:tt]).reshape(N_DEV, rb, GROUP)
            dpooled = _dot_nt(dy_b, wp)
            q = dpooled / _pool_counts(i * tt, tt + HALO, win)
            dz = _window_sum_fwd(q, win)[:tt] - dpooled[:tt]
            dproj_ref[:, 2 * A_WIDTH + g * GROUP:2 * A_WIDTH + (g + 1) * GROUP] = dz.astype(BF16)

    def full(shape):
        return pl.BlockSpec(shape, lambda i: (0,) * len(shape))

    return _carrier_call(
        body, (proj, proj, proj, proj, dmixed, dmixed, dmixed, w_s, bs_t, g_v, w_pool_g, pool_scale), comm,
        name="mixer_bwd",
        grid=(nsteps,),
        in_specs=[pl.BlockSpec((tt, A_WIDTH), lambda i: (i, 0)),
                  pl.BlockSpec((tt, A_WIDTH), lambda i: (i, 1)),
                  pl.BlockSpec((tt, B_WIDTH), lambda i: (i, 2)),
                  pl.BlockSpec((HALO, B_WIDTH), lambda i: (jnp.maximum(i * hb - 1, 0), 2)),
                  pl.BlockSpec((tt, A_WIDTH), lambda i: (i, 0)),
                  pl.BlockSpec((tt, B_WIDTH), lambda i: (i, 1)),
                  pl.BlockSpec((HALO, B_WIDTH), lambda i: (jnp.minimum((i + 1) * hb, last_halo), 1)),
                  full((N_HEADS, CHUNK, CHUNK)), full((CHUNK, N_HEADS)), full((1, A_WIDTH)),
                  full((N_DEV, 4, rb, GROUP)), full((1, B_WIDTH))],
        out_specs=[pl.BlockSpec((tt, 2 * A_WIDTH + B_WIDTH), lambda i: (i, 0)),
                   full((N_HEADS, CHUNK, CHUNK)), full((CHUNK, N_HEADS)), full((1, A_WIDTH)),
                   full((1, B_WIDTH)), full((N_DEV, 4, rb, GROUP))],
        out_shape=[jax.ShapeDtypeStruct((t, 2 * A_WIDTH + B_WIDTH), BF16),
                   jax.ShapeDtypeStruct((N_HEADS, CHUNK, CHUNK), F32),
                   jax.ShapeDtypeStruct((CHUNK, N_HEADS), F32),
                   jax.ShapeDtypeStruct((1, A_WIDTH), F32),
                   jax.ShapeDtypeStruct((1, B_WIDTH), F32),
                   jax.ShapeDtypeStruct((N_DEV, 4, rb, GROUP), F32)],
        sem=("arbitrary",))


def _adamw(w, g, m, v):
    m = ADAM_B1 * m + (1.0 - ADAM_B1) * g
    v = ADAM_B2 * v + (1.0 - ADAM_B2) * (g * g)
    m_hat = m / ADAM_C1
    v_hat = v / ADAM_C2
    delta = -ADAM_LR * (m_hat / (jnp.sqrt(v_hat) + ADAM_EPS) + ADAM_WD * w)
    return delta, m, v


PAIR_SUM_TILE_ELEMS = 1024 * 1024
ADAMW_TILE_ELEMS = 512 * 1024


def _row_tile(r, c, elems):
    t = r
    while t * c > elems and t % 32 == 0:
        t //= 2
    return t


def _pair_sum_call(name, pos, grad, got):
    _, r, c = grad.shape
    tr = _row_tile(r, c, PAIR_SUM_TILE_ELEMS)

    def chip_of(rel, pos_ref):
        px = jnp.where((rel == 0) | (rel == 2), 1 - pos_ref[0], pos_ref[0])
        py = jnp.where((rel == 1) | (rel == 2), 1 - pos_ref[1], pos_ref[1])
        return 2 * px + py

    def body(pos_ref, own_ref, got_ref, out_ref):
        out_ref[...] = (own_ref[...].astype(F32) + got_ref[...].astype(F32)).astype(BF16)

    return pl.pallas_call(
        body, name=name,
        grid_spec=pltpu.PrefetchScalarGridSpec(
            num_scalar_prefetch=1, grid=(3, r // tr),
            in_specs=[pl.BlockSpec((None, tr, c), lambda k, i, p: (2 * chip_of(k, p) + p[2], i, 0)),
                      pl.BlockSpec((None, tr, c), lambda k, i, p: (chip_of(k, p), i, 0))],
            out_specs=pl.BlockSpec((None, tr, c), lambda k, i, p: (k, i, 0))),
        out_shape=jax.ShapeDtypeStruct((3, r, c), BF16),
        compiler_params=_params("parallel", "parallel"),
    )(pos, grad, got)


def _final_call(name, pos, grad, got_pair, got_chips, w, m, v):
    _, r, c = grad.shape
    tr = _row_tile(r, c, ADAMW_TILE_ELEMS)

    def body(pos_ref, own_ref, pair_ref, chips_ref, w_ref, m_ref, v_ref, g_out, d_out, m_out, v_out):
        g = own_ref[...].astype(F32) + pair_ref[...].astype(F32)
        for j in range(3):
            g = g + chips_ref[j].astype(F32)
        delta, m_new, v_new = _adamw(w_ref[...], g, m_ref[...], v_ref[...])
        g_out[...] = g
        d_out[...] = delta
        m_out[...] = m_new
        v_out[...] = v_new

    row = pl.BlockSpec((tr, c), lambda i, p: (i, 0))
    return pl.pallas_call(
        body, name=name,
        grid_spec=pltpu.PrefetchScalarGridSpec(
            num_scalar_prefetch=1, grid=(r // tr,),
            in_specs=[pl.BlockSpec((None, tr, c), lambda i, p: (4 * p[0] + 2 * p[1] + p[2], i, 0)),
                      pl.BlockSpec((None, tr, c), lambda i, p: (2 * p[0] + p[1], i, 0)),
                      pl.BlockSpec((3, tr, c), lambda i, p: (0, i, 0)), row, row, row],
            out_specs=[row] * 4),
        out_shape=[jax.ShapeDtypeStruct((r, c), F32)] * 4,
        compiler_params=_params("parallel"),
    )(pos, grad, got_pair, got_chips, w, m, v)


def _small_final_call(name, parts, w, m, v):
    _, rows, c = parts.shape
    r = w.shape[0]

    def body(p_ref, w_ref, m_ref, v_ref, g_out, d_out, m_out, v_out):
        g = p_ref[0]
        for k in range(1, N_DEV):
            g = g + p_ref[k]
        delta, m_new, v_new = _adamw(w_ref[...], g[:r], m_ref[...], v_ref[...])
        g_out[...] = g
        d_out[...] = delta
        m_out[...] = m_new
        v_out[...] = v_new

    return pl.pallas_call(
        body, name=name,
        out_shape=[jax.ShapeDtypeStruct((rows, c), F32)] + [jax.ShapeDtypeStruct((r, c), F32)] * 3,
        compiler_params=pltpu.CompilerParams(vmem_limit_bytes=VMEM_LIMIT),
    )(parts, w, m, v)


_SMALL_EARLY = ("g_v", "w_s", "b_s", "pool_scale", "g_ffn", "g_final")
_BIG = ("w_in", "w_pool", "w_out", "w_up", "w_down")
_ORDER = ("g_mix", "w_in", "g_v", "w_s", "b_s", "w_pool", "pool_scale", "w_out", "g_ffn", "w_up", "w_down", "g_final")


def _pack(parts):
    return jnp.concatenate([p.reshape(-1, LANES) for p in parts], axis=0)


def _unpack(packed, like):
    out, row = [], 0
    for a in like:
        rows = a.size // LANES
        out.append(packed[row:row + rows].reshape(a.shape))
        row += rows
    return out


def kernel(x, g_mix, w_in, g_v, w_s, b_s, w_pool, pool_scale, w_out, g_ffn, w_up, w_down, g_final, loss_target, m_g_mix, m_w_in, m_g_v, m_w_s, m_b_s, m_w_pool, m_pool_scale, m_w_out, m_g_ffn, m_w_up, m_w_down, m_g_final, v_g_mix, v_w_in, v_g_v, v_w_s, v_b_s, v_w_pool, v_pool_scale, v_w_out, v_g_ffn, v_w_up, v_w_down, v_g_final):
    weights = dict(g_mix=g_mix, w_in=w_in, g_v=g_v, w_s=w_s, b_s=b_s, w_pool=w_pool, pool_scale=pool_scale,
                   w_out=w_out, g_ffn=g_ffn, w_up=w_up, w_down=w_down, g_final=g_final)
    mom = dict(g_mix=m_g_mix, w_in=m_w_in, g_v=m_g_v, w_s=m_w_s, b_s=m_b_s, w_pool=m_w_pool,
               pool_scale=m_pool_scale, w_out=m_w_out, g_ffn=m_g_ffn, w_up=m_w_up, w_down=m_w_down,
               g_final=m_g_final)
    var = dict(g_mix=v_g_mix, w_in=v_w_in, g_v=v_g_v, w_s=v_w_s, b_s=v_b_s, w_pool=v_w_pool,
               pool_scale=v_pool_scale, w_out=v_w_out, g_ffn=v_g_ffn, w_up=v_w_up, w_down=v_w_down,
               g_final=v_g_final)

    t, d = x.shape[1], x.shape[2]
    xs = x.reshape(t, d)
    target = loss_target.reshape(t, d)

    shard2d = dict(w_in=w_in.reshape(d, -1), w_pool=w_pool.reshape(-1, GROUP), w_out=w_out.reshape(-1, d),
                   w_up=w_up.reshape(d, -1), w_down=w_down.reshape(-1, d))
    sb = {k: shard2d[k].astype(BF16) for k in _BIG}
    rows = {k: sb[k].shape[0] for k in _BIG}

    def gathered_shape(k):
        return jax.ShapeDtypeStruct((N_DEV,) + sb[k].shape, BF16)

    def landing(n, like):
        return jax.ShapeDtypeStruct((n,) + like.shape[1:], like.dtype)

    def from_everyone(block):
        return jax.ShapeDtypeStruct((N_DEV,) + block.shape, block.dtype)

    def cuts(r, fractions):
        return [0] + [int(r * f) // 16 * 16 for f in fractions] + [r]

    g_mix2, g_ffn2, g_final2 = g_mix.reshape(1, d), g_ffn.reshape(1, d), g_final.reshape(1, d)
    g_v2, ps2 = g_v.reshape(1, A_WIDTH), pool_scale.reshape(1, B_WIDTH)
    w_s3 = w_s.reshape(N_HEADS, CHUNK, CHUNK)
    bs_t = b_s.reshape(N_HEADS, CHUNK).T
    xi, yi, ci = _position()
    pos = jnp.stack([xi, yi, ci]).astype(jnp.int32)

    w_in_g, w_pool_g = _comm_call("gather_w_in_w_pool", _Comm(
        [sb["w_in"], sb["w_pool"]], [], [gathered_shape("w_in"), gathered_shape("w_pool")],
        lambda s, l: [_gather_first(s[0], l[0], 0, rows["w_in"]) + _gather_first(s[1], l[1], 0, rows["w_pool"]),
                      _gather_pass_on(l[0], 0, rows["w_in"]) + _gather_pass_on(l[1], 0, rows["w_pool"])]))
    w_pool_g = w_pool_g.reshape(N_DEV, 4, GROUP // N_DEV, GROUP)

    u = cuts(rows["w_up"], (0.15, 0.35))[:3]
    proj, h1, w_out_g, w_up_g = _proj_call(xs, g_mix2, w_in_g, _Comm(
        [sb["w_out"], sb["w_up"]], [], [gathered_shape("w_out"), gathered_shape("w_up")],
        lambda s, l: [_gather_first(s[0], l[0], 0, rows["w_out"]) + _gather_first(s[1], l[1], u[0], u[1])]))
    mixed, w_out_g, w_up_g = _mixer_fwd_call(proj, w_s3, bs_t, g_v2, w_pool_g, ps2, _Comm(
        [sb["w_up"]], [w_out_g, w_up_g], [],
        lambda s, l: [_gather_pass_on(l[0], 0, rows["w_out"]) + _gather_pass_on(l[1], u[0], u[1])
                      + _gather_first(s[0], l[1], u[1], u[2])]))
    w_out_f = w_out_g.reshape(-1, d)
    x2, w_up_g = _out_proj_call(mixed, w_out_f, xs, _Comm(
        [], [w_up_g], [], lambda s, l: [_gather_pass_on(l[0], u[1], u[2])]))
    order = (4 * xi + 2 * yi + ci) ^ jnp.array(ARRIVAL_ORDER, jnp.int32)
    act, h2, w_up_g = _up_call(x2, g_ffn2, sb["w_up"], order, w_up_g, u[2])
    y, w_down_g = _down_call(act, sb["w_down"], order)
    w_down_f = w_down_g.reshape(-1, d)
    loss_part, dx3, dx3b, dg_final = _loss_call(y, x2, target, g_final2)

    def pair_sum(k, grad, got):
        return _pair_sum_call(k + "_pair_sum", pos, grad, got)

    def finish(k, grad, got_pair, got_chips):
        s = shard2d[k]
        outs = _final_call(k + "_adamw", pos, grad, got_pair, got_chips, s, mom[k].reshape(s.shape),
                           var[k].reshape(s.shape))
        return [o.reshape(weights[k].shape) for o in outs]

    result = {}
    (gw_down,) = _wgrad_call("w_down_grad", act, dx3b, None, None, t1=1024, t2=2048)
    gw_down = gw_down.reshape(N_DEV, -1, d)
    da, pair_down = _dact_call(dx3b, w_down_f, act, _Comm(
        [gw_down], [], [landing(4, gw_down)], lambda s, l: [_pair_exchange(s[0], l[0])]))
    sums_down = pair_sum("w_down", gw_down, pair_down)
    dn = cuts(rows["w_down"], (0.8,))
    gw_up, got = _wgrad_call("w_up_grad", h2, da, N_DEV, w_up_g.shape[2], _Comm(
        [sums_down], [], [landing(3, sums_down)],
        lambda s, l: [_chip_exchange(s[0], l[0], dn[0], dn[1])]), t1=2048)
    dh2, got, pair_up = _dgrad_blocked_call("dh2_bwd", da, w_up_g, _Comm(
        [sums_down, gw_up], [got], [landing(4, gw_up)],
        lambda s, l: [_chip_exchange(s[0], l[0], dn[1], dn[2]) + _pair_exchange(s[1], l[1])]))
    result["w_down"] = finish("w_down", gw_down, pair_down, got)
    sums_up = pair_sum("w_up", gw_up, pair_up)
    v = cuts(rows["w_up"], (0.28, 0.52, 0.84))
    dx2, dx2b, dg_ffn, got_up = _norm_bwd_call("ffn_norm_bwd", dh2, x2, dx3, g_ffn2, True, _Comm(
        [sums_up], [], [landing(3, sums_up)], lambda s, l: [_chip_exchange(s[0], l[0], v[0], v[1])]))
    dmixed, got_up = _dmixed_call(dx2b, w_out_f, _Comm(
        [sums_up], [got_up], [], lambda s, l: [_chip_exchange(s[0], l[0], v[1], v[2])]))
    gw_out, got_up = _wgrad_call("w_out_grad", mixed, dx2b, None, None, _Comm(
        [sums_up], [got_up], [], lambda s, l: [_chip_exchange(s[0], l[0], v[2], v[3])]), t1=2048, t2=1024)
    gw_out = gw_out.reshape(N_DEV, -1, d)
    dproj, dw_s, dbs_t, dg_v, dps, dw_pool, got_up, pair_out = _mixer_bwd_call(
        proj, dmixed, w_s3, bs_t, g_v2, w_pool_g, ps2, _Comm(
            [sums_up, gw_out], [got_up], [landing(4, gw_out)],
            lambda s, l: [_chip_exchange(s[0], l[0], v[3], v[4]) + _pair_exchange(s[1], l[1])]))
    result["w_up"] = finish("w_up", gw_up, pair_up, got_up)
    sums_out = pair_sum("w_out", gw_out, pair_out)
    gw_pool = dw_pool.astype(BF16).reshape(N_DEV, -1, GROUP)
    early = dict(g_v=dg_v, w_s=dw_s, b_s=dbs_t.T, pool_scale=dps, g_ffn=dg_ffn, g_final=dg_final)
    packed = _pack([early[k] for k in _SMALL_EARLY] + [loss_part])
    gw_in, got, pair_pool, parts_early = _wgrad_call("w_in_grad", h1, dproj, N_DEV, w_in_g.shape[2], _Comm(
        [sums_out, gw_pool, packed], [], [landing(3, sums_out), landing(4, gw_pool), from_everyone(packed)],
        lambda s, l: [_chip_exchange(s[0], l[0], 0, rows["w_out"]) + _pair_exchange(s[1], l[1])
                      + _everyone(s[2], l[2])]), t1=2048, merge=MERGE_W_IN)
    result["w_out"] = finish("w_out", gw_out, pair_out, got)
    sums_pool = pair_sum("w_pool", gw_pool, pair_pool)
    (pair_in,) = _comm_call("pair_exchange_w_in", _Comm(
        [gw_in], [], [landing(4, gw_in)], lambda s, l: [_pair_exchange(s[0], l[0])]))
    sums_in = pair_sum("w_in", gw_in, pair_in)
    dh1, got, got_pool = _dgrad_blocked_call("dh1_bwd", dproj, w_in_g, _Comm(
        [sums_in, sums_pool], [], [landing(3, sums_in), landing(3, sums_pool)],
        lambda s, l: [_chip_exchange(s[0], l[0], 0, rows["w_in"]) + _chip_exchange(s[1], l[1], 0, rows["w_pool"])]),
        merge=MERGE_W_IN)
    result["w_in"] = finish("w_in", gw_in, pair_in, got)
    result["w_pool"] = finish("w_pool", gw_pool, pair_pool, got_pool)
    grad_x, dg_mix = _norm_bwd_call("mix_norm_bwd", dh1, xs, dx2, g_mix2, False)
    packed = _pack([dg_mix])
    (parts_late,) = _comm_call("gather_g_mix_grad", _Comm(
        [packed], [], [from_everyone(packed)], lambda s, l: [_everyone(s[0], l[0])]))

    for names, parts, tag in ((_SMALL_EARLY, parts_early, "small_adamw"), (("g_mix",), parts_late, "g_mix_adamw")):
        outs = _small_final_call(tag, parts, _pack([weights[k] for k in names]), _pack([mom[k] for k in names]),
                                 _pack([var[k] for k in names]))
        if tag == "small_adamw":
            loss = outs[0][-1, 0]
        like = [weights[k] for k in names]
        unpacked = [_unpack(o, like) for o in outs]
        for idx, k in enumerate(names):
            result[k] = [unpacked[q][idx] for q in range(4)]

    grads = [result[k][0] for k in _ORDER]
    deltas = [result[k][1] for k in _ORDER]
    new_m = [result[k][2] for k in _ORDER]
    new_v = [result[k][3] for k in _ORDER]
    return (loss, grad_x.reshape(x.shape), *grads, *deltas, *new_m, *new_v)
```

```python
import functools
import math

import jax
import jax.numpy as jnp
from jax import lax
from jax.experimental import pallas as pl
from jax.experimental.pallas import tpu as pltpu

F32 = jnp.float32
BF16 = jnp.bfloat16
MESH = pl.DeviceIdType.MESH

N_DEV = 8
EPS = 1e-6
CHUNK = 128
N_HEADS = 8
A_WIDTH = 1024
B_WIDTH = 1024
POOL_WINDOWS = (2, 4, 8, 16)
GROUP = 256
HALO = 16
LANES = 128

ADAM_LR = 0.001
ADAM_B1 = 0.9
ADAM_B2 = 0.999
ADAM_EPS = 1e-08
ADAM_WD = 0.01
ADAM_STEP = 10
ADAM_C1 = 1.0 - ADAM_B1 ** ADAM_STEP
ADAM_C2 = 1.0 - ADAM_B2 ** ADAM_STEP

VMEM_LIMIT = 56 * 1024 * 1024
MERGE_W_IN = 2

_GELU_C = math.sqrt(2.0 / math.pi)


def _params(*sem):
    return pltpu.CompilerParams(dimension_semantics=sem, vmem_limit_bytes=VMEM_LIMIT)


def _gelu(x):
    return 0.5 * x * (1.0 + jnp.tanh(_GELU_C * (x + 0.044715 * x * x * x)))


def _gelu_and_grad(x):
    t = jnp.tanh(_GELU_C * (x + 0.044715 * x * x * x))
    g = 0.5 * x * (1.0 + t)
    dg = 0.5 * (1.0 + t) + 0.5 * x * (1.0 - t * t) * (_GELU_C * (1.0 + 3.0 * 0.044715 * x * x))
    return g, dg


def _dot_nn(a, b):
    return lax.dot_general(a, b, (((1,), (0,)), ((), ())), preferred_element_type=F32)


def _dot_nt(a, b):
    return lax.dot_general(a, b, (((1,), (1,)), ((), ())), preferred_element_type=F32)


def _dot_tn(a, b):
    return lax.dot_general(a, b, (((0,), (0,)), ((), ())), preferred_element_type=F32)


def _rms_rows(x):
    r = lax.rsqrt(jnp.mean(x * x, axis=-1, keepdims=True) + EPS)
    return x * r, r


def _rms_bwd_rows(dn, n, r):
    return r * (dn - n * jnp.mean(dn * n, axis=-1, keepdims=True))


def _tile(n, want):
    t = min(n, want)
    assert n % t == 0, (n, want)
    return t


_ANY = pl.BlockSpec(memory_space=pl.ANY)

SIBLING = 1
CHIPS = (4, 2, 6)


def _position():
    return lax.axis_index("x"), lax.axis_index("y"), lax.axis_index("c")


def _me():
    x, y, c = _position()
    return 4 * x + 2 * y + c


def _peer(rel):
    x, y, c = _position()
    return (x ^ ((rel >> 2) & 1), y ^ ((rel >> 1) & 1), c ^ (rel & 1))


class _Comm:
    def __init__(self, srcs, lands, new, plan):
        self.srcs, self.lands, self.new, self.plan = list(srcs), list(lands), list(new), plan


def _make_copies(phases, send_sems, recv_sems, local_sems):
    out, nr, nl = [], 0, 0
    for phase in phases:
        cps = []
        for item in phase:
            if item[0] == "local":
                cps.append(pltpu.make_async_copy(item[1], item[2], local_sems.at[nl]))
                nl += 1
            else:
                cps.append(pltpu.make_async_remote_copy(
                    src_ref=item[1], dst_ref=item[2], send_sem=send_sems.at[nr], recv_sem=recv_sems.at[nr],
                    device_id=_peer(item[3]), device_id_type=MESH))
                nr += 1
        out.append(cps)
    return out


def _count_copies(comm):
    phases = comm.plan([_FakeRef() for _ in comm.srcs], [_FakeRef() for _ in range(len(comm.lands) + len(comm.new))])
    items = [it for ph in phases for it in ph]
    return sum(it[0] == "remote" for it in items), sum(it[0] == "local" for it in items)


class _FakeRef:
    def __getitem__(self, idx):
        return self

    @property
    def at(self):
        return self


def _carrier_call(body, args, comm, *, name, grid, in_specs, out_specs, out_shape, scratch_shapes=(), sem):
    if not isinstance(out_shape, (list, tuple)):
        out_specs, out_shape = [out_specs], [out_shape]
    out_specs, out_shape, scratch_shapes = list(out_specs), list(out_shape), list(scratch_shapes)
    if comm is None:
        res = pl.pallas_call(body, name=name, grid=grid, in_specs=list(in_specs), out_specs=out_specs,
                             out_shape=out_shape, scratch_shapes=scratch_shapes, compiler_params=_params(*sem))(*args)
        return list(res)
    n_in, n_out, n_scr = len(args), len(out_shape), len(scratch_shapes)
    ns, nl, nn = len(comm.srcs), len(comm.lands), len(comm.new)
    n_remote, n_local = _count_copies(comm)

    def wrapped(*refs):
        ins, srcs = refs[:n_in], refs[n_in:n_in + ns]
        o = n_in + ns + nl
        outs, lands = refs[o:o + n_out], refs[o + n_out:o + n_out + nl + nn]
        scr = refs[o + n_out + nl + nn:]
        (copies,) = _make_copies(comm.plan(srcs, lands), *scr[n_scr:])
        ids = [pl.program_id(a) for a in range(len(grid))]
        first = functools.reduce(jnp.logical_and, [i == 0 for i in ids])
        last = functools.reduce(jnp.logical_and, [i == g - 1 for i, g in zip(ids, grid)])

        @pl.when(first)
        def _():
            for cp in copies:
                cp.start()

        body(*ins, *outs, *scr[:n_scr])

        @pl.when(last)
        def _():
            for cp in copies:
                cp.wait()

    land_shapes = [jax.ShapeDtypeStruct(a.shape, a.dtype) for a in comm.lands] + comm.new
    sems = [pltpu.SemaphoreType.DMA((max(n_remote, 1),)), pltpu.SemaphoreType.DMA((max(n_remote, 1),)),
            pltpu.SemaphoreType.DMA((max(n_local, 1),))]
    res = pl.pallas_call(
        wrapped, name=name, grid=grid,
        in_specs=list(in_specs) + [_ANY] * (ns + nl), out_specs=out_specs + [_ANY] * (nl + nn),
        out_shape=out_shape + land_shapes, scratch_shapes=scratch_shapes + sems,
        input_output_aliases={n_in + ns + k: n_out + k for k in range(nl)},
        compiler_params=_params(*sem))(*args, *comm.srcs, *comm.lands)
    return list(res)


def _comm_call(name, comm):
    ns, nl, nn = len(comm.srcs), len(comm.lands), len(comm.new)
    n_remote, n_local = _count_copies(comm)

    def body(*refs):
        srcs, lands, sems = refs[:ns], refs[ns + nl:ns + nl + nl + nn], refs[ns + nl + nl + nn:]
        for copies in _make_copies(comm.plan(srcs, lands), *sems):
            for cp in copies:
                cp.start()
            for cp in copies:
                cp.wait()

    land_shapes = [jax.ShapeDtypeStruct(a.shape, a.dtype) for a in comm.lands] + comm.new
    res = pl.pallas_call(
        body, name=name,
        in_specs=[_ANY] * (ns + nl), out_specs=[_ANY] * (nl + nn), out_shape=land_shapes,
        scratch_shapes=[pltpu.SemaphoreType.DMA((max(n_remote, 1),)), pltpu.SemaphoreType.DMA((max(n_remote, 1),)),
                        pltpu.SemaphoreType.DMA((max(n_local, 1),))],
        input_output_aliases={ns + k: k for k in range(nl)},
    )(*comm.srcs, *comm.lands)
    return list(res)


def _rows(ref, block, r0, r1):
    return ref.at[block, pl.ds(r0, r1 - r0)]


def _gather_first(shard, land, r0, r1):
    src = shard.at[pl.ds(r0, r1 - r0)]
    dst = _rows(land, _me(), r0, r1)
    return [("local", src, dst)] + [("remote", src, dst, rel) for rel in (SIBLING,) + CHIPS]


def _gather_pass_on(land, r0, r1):
    return [("remote", _rows(land, _me() ^ rel, r0, r1), _rows(land, _me() ^ rel, r0, r1), SIBLING) for rel in CHIPS]


def _pair_exchange(grad, land):
    _, _, c = _position()
    return [("remote", grad.at[2 * chip + (1 - c)], land.at[chip], SIBLING) for chip in range(4)]


def _chip_exchange(sums, land, r0, r1):
    return [("remote", _rows(sums, j, r0, r1), _rows(land, j, r0, r1), rel) for j, rel in enumerate(CHIPS)]


def _everyone(packed, land):
    dst = land.at[_me()]
    return [("local", packed, dst)] + [("remote", packed, dst, rel) for rel in range(1, N_DEV)]


def _proj_call(x, g_mix, w_in_g, comm=None):
    t, d = x.shape
    nb, _, cb = w_in_g.shape
    tm = _tile(t, 1024)
    mg = MERGE_W_IN

    def body(x_ref, g_ref, w_ref, proj_ref, h_ref):
        @pl.when(pl.program_id(1) == 0)
        def _():
            n, _ = _rms_rows(x_ref[...])
            h_ref[...] = (n * g_ref[...]).astype(BF16)

        w = jnp.concatenate([w_ref[b] for b in range(mg)], axis=1)
        proj_ref[...] = _dot_nn(h_ref[...], w)

    return _carrier_call(
        body, (x, g_mix, w_in_g), comm, name="proj_fwd",
        grid=(t // tm, nb // mg),
        in_specs=[pl.BlockSpec((tm, d), lambda i, j: (i, 0)),
                  pl.BlockSpec((1, d), lambda i, j: (0, 0)),
                  pl.BlockSpec((mg, d, cb), lambda i, j: (j, 0, 0))],
        out_specs=[pl.BlockSpec((tm, mg * cb), lambda i, j: (i, j)),
                   pl.BlockSpec((tm, d), lambda i, j: (i, 0))],
        out_shape=[jax.ShapeDtypeStruct((t, nb * cb), F32), jax.ShapeDtypeStruct((t, d), BF16)],
        sem=("parallel", "arbitrary"))


def _pool_counts(row0, rows, win):
    pos = row0 + lax.broadcasted_iota(jnp.int32, (rows, 1), 0)
    return jnp.minimum(pos + 1, win).astype(F32)


def _window_sum_back(ext, win):
    s = ext
    k = 1
    while k < win:
        s = s + pltpu.roll(s, k, 0)
        k *= 2
    return s


def _window_sum_fwd(ext, win):
    n = ext.shape[0]
    s = ext
    k = 1
    while k < win:
        s = s + pltpu.roll(s, n - k, 0)
        k *= 2
    return s


def _mixer_fwd_call(proj, w_s, bs_t, g_v, w_pool_g, pool_scale, comm=None):
    t = proj.shape[0]
    tt = _tile(t, 512)
    nchunk = tt // CHUNK
    hb = tt // HALO

    def body(pu_ref, pv_ref, z_ref, zp_ref, ws_ref, bs_ref, gv_ref, wp_ref, ps_ref, out_ref):
        i = pl.program_id(0)
        tril = (lax.broadcasted_iota(jnp.int32, (CHUNK, CHUNK), 0)
                >= lax.broadcasted_iota(jnp.int32, (CHUNK, CHUNK), 1))
        for h in range(N_HEADS):
            cols = slice(h * CHUNK, (h + 1) * CHUNK)
            vhat, _ = _rms_rows(_gelu(pv_ref[:, cols]))
            vn = (vhat * gv_ref[:, cols]).astype(BF16)
            u = _gelu(pu_ref[:, cols])
            w = jnp.where(tril, ws_ref[h], 0.0).astype(BF16)
            bcol = bs_ref[:, h:h + 1]
            for c in range(nchunk):
                rows = slice(c * CHUNK, (c + 1) * CHUNK)
                mixed = _dot_nn(w, vn[rows]) + bcol
                out_ref[rows, cols] = (u[rows] * mixed).astype(BF16)

        zprev = jnp.where(i > 0, zp_ref[...], 0.0)
        ext = jnp.concatenate([zprev, z_ref[...]], axis=0)
        for g, win in enumerate(POOL_WINDOWS):
            cols = slice(g * GROUP, (g + 1) * GROUP)
            zg = ext[:, cols]
            s = _window_sum_back(zg, win)
            pooled = s[HALO:] / _pool_counts(i * tt, tt, win) - zg[HALO:]
            wp = wp_ref[:, g].reshape(GROUP, GROUP)
            y = _dot_nn(pooled.astype(BF16), wp)
            out_ref[:, A_WIDTH + g * GROUP:A_WIDTH + (g + 1) * GROUP] = (y * ps_ref[:, cols]).astype(BF16)

    return _carrier_call(
        body, (proj, proj, proj, proj, w_s, bs_t, g_v, w_pool_g, pool_scale), comm, name="mixer_fwd",
        grid=(t // tt,),
        in_specs=[pl.BlockSpec((tt, A_WIDTH), lambda i: (i, 0)),
                  pl.BlockSpec((tt, A_WIDTH), lambda i: (i, 1)),
                  pl.BlockSpec((tt, B_WIDTH), lambda i: (i, 2)),
                  pl.BlockSpec((HALO, B_WIDTH), lambda i: (jnp.maximum(i * hb - 1, 0), 2)),
                  pl.BlockSpec((N_HEADS, CHUNK, CHUNK), lambda i: (0, 0, 0)),
                  pl.BlockSpec((CHUNK, N_HEADS), lambda i: (0, 0)),
                  pl.BlockSpec((1, A_WIDTH), lambda i: (0, 0)),
                  pl.BlockSpec((N_DEV, 4, GROUP // N_DEV, GROUP), lambda i: (0, 0, 0, 0)),
                  pl.BlockSpec((1, B_WIDTH), lambda i: (0, 0))],
        out_specs=pl.BlockSpec((tt, A_WIDTH + B_WIDTH), lambda i: (i, 0)),
        out_shape=jax.ShapeDtypeStruct((t, A_WIDTH + B_WIDTH), BF16),
        sem=("parallel",))


def _out_proj_call(mixed, w_out, x, comm=None):
    t, d = x.shape
    k = mixed.shape[1]
    tm = _tile(t, 1024)
    tn = _tile(d, 1024)

    def body(a_ref, w_ref, x_ref, o_ref):
        o_ref[...] = x_ref[...] + _dot_nn(a_ref[...], w_ref[...])

    return _carrier_call(
        body, (mixed, w_out, x), comm, name="out_proj_fwd",
        grid=(t // tm, d // tn),
        in_specs=[pl.BlockSpec((tm, k), lambda i, j: (i, 0)),
                  pl.BlockSpec((k, tn), lambda i, j: (0, j)),
                  pl.BlockSpec((tm, tn), lambda i, j: (i, j))],
        out_specs=pl.BlockSpec((tm, tn), lambda i, j: (i, j)),
        out_shape=jax.ShapeDtypeStruct((t, d), F32),
        sem=("parallel", "parallel"))


ARRIVAL_ORDER = (0, 1, 4, 5, 2, 3, 6, 7)


class _StreamedGather:
    def __init__(self, shard_ref, land_ref, wbuf, pre0, r0, send_sems, recv_sems, local_sem, fetch_sems):
        self.shard, self.land, self.wbuf, self.fetch_sems = shard_ref, land_ref, wbuf, fetch_sems
        n = shard_ref.shape[0] - r0
        me = _me()
        self.me = me

        def remote(k, src, dst, rel):
            return pltpu.make_async_remote_copy(src_ref=src, dst_ref=dst, send_sem=send_sems.at[k],
                                                recv_sem=recv_sems.at[k], device_id=_peer(rel), device_id_type=MESH)

        src = shard_ref.at[pl.ds(r0, n)]
        dst = land_ref.at[me, pl.ds(r0, n)]
        self.mine = pltpu.make_async_copy(src, dst, local_sem)
        self.first = [remote(k, src, dst, rel) for k, rel in enumerate((SIBLING,) + CHIPS)]
        self.passed, self.early = [], []
        for j, rel in enumerate(CHIPS):
            rows = land_ref.at[me ^ rel, pl.ds(r0, n)]
            self.passed.append(remote(4 + j, rows, rows, SIBLING))
            if r0 > pre0:
                rows = land_ref.at[me ^ rel, pl.ds(pre0, r0 - pre0)]
                self.early.append(remote(7 + j, rows, rows, SIBLING))

    def _fetch(self, q):
        src = self.shard if q == 0 else self.land.at[self.me ^ ARRIVAL_ORDER[q]]
        return pltpu.make_async_copy(src, self.wbuf.at[q % 2], self.fetch_sems.at[q % 2])

    def start(self):
        self.mine.start()
        for cp in self.early + self.first:
            cp.start()
        self._fetch(0).start()

    def arrive(self, q):
        if q == 1:
            self.first[0].wait_recv()
        elif q % 2 == 0:
            j = q // 2 - 1
            self.first[1 + j].wait_recv()
            self.passed[j].start()
        else:
            if q == 3:
                for cp in self.early:
                    cp.wait_recv()
            self.passed[q // 2 - 1].wait_recv()
        self._fetch(q).start()

    def wait_fetch(self, slot):
        pltpu.make_async_copy(self.shard, self.wbuf.at[slot], self.fetch_sems.at[slot]).wait()

    def finish(self):
        for cp in self.early + self.first + self.passed:
            cp.wait_send()
        self.mine.wait()


_STREAM_SEMS = [pltpu.SemaphoreType.DMA((10,)), pltpu.SemaphoreType.DMA((10,)), pltpu.SemaphoreType.DMA,
                pltpu.SemaphoreType.DMA((2,))]


def _stream_steps(gather, p, i, ni):
    @pl.when((p == 0) & (i == 0))
    def _():
        gather.start()

    @pl.when(i == 0)
    def _():
        gather.wait_fetch(p % 2)

    @pl.when(i == ni - 1)
    def _():
        for q in range(1, N_DEV):
            @pl.when(p == q - 1)
            def _():
                gather.arrive(q)


def _up_call(x2, g_ffn, shard, order, land, pre0, r0, next_shard, next_rows):
    t, d = x2.shape
    fb = shard.shape[1]
    tm = _tile(t, 1024)
    ni = t // tm
    n_sems = len(_STREAM_SEMS)

    def body(order_ref, x_ref, g_ref, shard_ref, land_in_ref, next_ref, act_ref, h_ref, land_ref, next_land_ref,
             wbuf, *sems):
        p, i = pl.program_id(0), pl.program_id(1)
        gather = _StreamedGather(shard_ref, land_ref, wbuf, pre0, r0, *sems[:n_sems])
        (ahead,) = _make_copies([_gather_first(next_ref, next_land_ref, 0, next_rows)], *sems[n_sems:])
        rows = pl.ds(pl.multiple_of(i * tm, tm), tm)
        _stream_steps(gather, p, i, ni)

        @pl.when((p == 0) & (i == 0))
        def _():
            for cp in ahead:
                cp.start()

        @pl.when(p == 0)
        def _():
            n, _ = _rms_rows(x_ref[...])
            h_ref[rows, :] = (n * g_ref[...]).astype(BF16)

        a = jnp.maximum(_dot_nn(h_ref[rows, :], wbuf[p % 2]), 0.0)
        act_ref[...] = (a * a).astype(BF16)

        @pl.when((p == N_DEV - 1) & (i == ni - 1))
        def _():
            gather.finish()
            for cp in ahead:
                cp.wait()

    ahead_sems = [pltpu.SemaphoreType.DMA((4,)), pltpu.SemaphoreType.DMA((4,)), pltpu.SemaphoreType.DMA((1,))]
    return pl.pallas_call(
        body, name="up_fwd",
        grid_spec=pltpu.PrefetchScalarGridSpec(
            num_scalar_prefetch=1, grid=(N_DEV, ni),
            in_specs=[pl.BlockSpec((tm, d), lambda p, i, o: (jnp.where(p == 0, i, ni - 1), 0)),
                      pl.BlockSpec((1, d), lambda p, i, o: (0, 0)),
                      _ANY, _ANY, _ANY],
            out_specs=[pl.BlockSpec((tm, fb), lambda p, i, o: (i, o[p])),
                       pl.BlockSpec(memory_space=pltpu.VMEM),
                       _ANY, _ANY],
            scratch_shapes=[pltpu.VMEM((2, d, fb), BF16)] + _STREAM_SEMS + ahead_sems),
        out_shape=[jax.ShapeDtypeStruct((t, N_DEV * fb), BF16), jax.ShapeDtypeStruct((t, d), BF16),
                   jax.ShapeDtypeStruct((N_DEV, d, fb), BF16),
                   jax.ShapeDtypeStruct((N_DEV,) + next_shard.shape, BF16)],
        input_output_aliases={4: 2},
        compiler_params=_params("arbitrary", "arbitrary"),
    )(order, x2, g_ffn, shard, land, next_shard)


def _down_call(act, shard, order, land, r0):
    t = act.shape[0]
    rb, d = shard.shape
    tm = _tile(t, 1024)
    ni = t // tm

    def body(order_ref, a_ref, shard_ref, land_in_ref, y_ref, land_ref, wbuf, *sems):
        p, i = pl.program_id(0), pl.program_id(1)
        gather = _StreamedGather(shard_ref, land_ref, wbuf, 0, r0, *sems)
        rows = pl.ds(pl.multiple_of(i * tm, tm), tm)
        _stream_steps(gather, p, i, ni)
        part = _dot_nn(a_ref[...], wbuf[p % 2])

        @pl.when(p == 0)
        def _():
            y_ref[rows, :] = part

        @pl.when(p > 0)
        def _():
            y_ref[rows, :] += part

        @pl.when((p == N_DEV - 1) & (i == ni - 1))
        def _():
            gather.finish()

    return pl.pallas_call(
        body, name="down_fwd",
        grid_spec=pltpu.PrefetchScalarGridSpec(
            num_scalar_prefetch=1, grid=(N_DEV, ni),
            in_specs=[pl.BlockSpec((tm, rb), lambda p, i, o: (i, o[p])), _ANY, _ANY],
            out_specs=[pl.BlockSpec(memory_space=pltpu.VMEM), _ANY],
            scratch_shapes=[pltpu.VMEM((2, rb, d), BF16)] + _STREAM_SEMS),
        out_shape=[jax.ShapeDtypeStruct((t, d), F32), jax.ShapeDtypeStruct((N_DEV, rb, d), BF16)],
        input_output_aliases={3: 1},
        compiler_params=_params("arbitrary", "arbitrary"),
    )(order, act, shard, land)


def _loss_call(y, x2, target, g_final):
    t, d = y.shape
    tr = _tile(t, 256)

    def body(y_ref, x_ref, tg_ref, g_ref, loss_ref, dx_ref, dxb_ref, dg_ref):
        @pl.when(pl.program_id(0) == 0)
        def _():
            loss_ref[...] = jnp.zeros_like(loss_ref)
            dg_ref[...] = jnp.zeros_like(dg_ref)

        n, r = _rms_rows(x_ref[...] + y_ref[...])
        err = n * g_ref[...] - tg_ref[...]
        loss_ref[...] += 0.5 * jnp.sum(jnp.mean(err * err, axis=-1, keepdims=True))
        dy = err * (1.0 / d)
        dg_ref[...] += jnp.sum(dy * n, axis=0, keepdims=True)
        dx = _rms_bwd_rows(dy * g_ref[...], n, r)
        dx_ref[...] = dx
        dxb_ref[...] = dx.astype(BF16)

    return pl.pallas_call(
        body, name="loss_head",
        grid=(t // tr,),
        in_specs=[pl.BlockSpec((tr, d), lambda i: (i, 0)),
                  pl.BlockSpec((tr, d), lambda i: (i, 0)),
                  pl.BlockSpec((tr, d), lambda i: (i, 0)),
                  pl.BlockSpec((1, d), lambda i: (0, 0))],
        out_specs=[pl.BlockSpec((8, LANES), lambda i: (0, 0)),
                   pl.BlockSpec((tr, d), lambda i: (i, 0)),
                   pl.BlockSpec((tr, d), lambda i: (i, 0)),
                   pl.BlockSpec((1, d), lambda i: (0, 0))],
        out_shape=[jax.ShapeDtypeStruct((8, LANES), F32), jax.ShapeDtypeStruct((t, d), F32),
                   jax.ShapeDtypeStruct((t, d), BF16), jax.ShapeDtypeStruct((1, d), F32)],
        compiler_params=_params("arbitrary"),
    )(y, x2, target, g_final)


def _norm_bwd_call(name, dh, x, dres, g, want_bf16, comm=None):
    t, d = x.shape
    tr = _tile(t, 256)

    def body(dh_ref, x_ref, dres_ref, g_ref, dx_ref, *rest):
        dg_ref = rest[-1]

        @pl.when(pl.program_id(0) == 0)
        def _():
            dg_ref[...] = jnp.zeros_like(dg_ref)

        n, r = _rms_rows(x_ref[...])
        dh = dh_ref[...]
        dg_ref[...] += jnp.sum(dh * n, axis=0, keepdims=True)
        dx = dres_ref[...] + _rms_bwd_rows(dh * g_ref[...], n, r)
        dx_ref[...] = dx
        if want_bf16:
            rest[0][...] = dx.astype(BF16)

    row = pl.BlockSpec((tr, d), lambda i: (i, 0))
    vec = pl.BlockSpec((1, d), lambda i: (0, 0))
    out_specs = [row] + ([row] if want_bf16 else []) + [vec]
    out_shape = ([jax.ShapeDtypeStruct((t, d), F32)]
                 + ([jax.ShapeDtypeStruct((t, d), BF16)] if want_bf16 else [])
                 + [jax.ShapeDtypeStruct((1, d), F32)])
    return _carrier_call(
        body, (dh, x, dres, g), comm, name=name,
        grid=(t // tr,),
        in_specs=[row, row, row, vec],
        out_specs=out_specs, out_shape=out_shape,
        sem=("arbitrary",))


def _dact_call(dx3b, w_down, act, comm=None):
    t, d = dx3b.shape
    f = w_down.shape[0]
    tm = _tile(t, 1024)
    tn = _tile(f, 1024)

    def body(g_ref, w_ref, act_ref, o_ref):
        dact = _dot_nt(g_ref[...], w_ref[...])
        o_ref[...] = (dact * (2.0 * jnp.sqrt(act_ref[...].astype(F32)))).astype(BF16)

    return _carrier_call(
        body, (dx3b, w_down, act), comm, name="dact_bwd",
        grid=(t // tm, f // tn),
        in_specs=[pl.BlockSpec((tm, d), lambda i, j: (i, 0)),
                  pl.BlockSpec((tn, d), lambda i, j: (j, 0)),
                  pl.BlockSpec((tm, tn), lambda i, j: (i, j))],
        out_specs=pl.BlockSpec((tm, tn), lambda i, j: (i, j)),
        out_shape=jax.ShapeDtypeStruct((t, f), BF16),
        sem=("parallel", "parallel"))


def _wgrad_call(name, a, b, out_blocks, out_block_cols, comm=None, *, t1, t2=None, merge=1):
    t, k1 = a.shape
    k2 = b.shape[1]
    tt = _tile(t, 1024)
    t1 = _tile(k1, t1)
    t2 = _tile(k2, t2) if out_blocks is None else merge * out_block_cols
    nk = t // tt

    def body(a_ref, b_ref, o_ref, acc_ref):
        k = pl.program_id(2)

        @pl.when(k == 0)
        def _():
            acc_ref[...] = jnp.zeros_like(acc_ref)

        acc_ref[...] += _dot_tn(a_ref[...], b_ref[...])

        @pl.when(k == nk - 1)
        def _():
            if out_blocks is None:
                o_ref[...] = acc_ref[...].astype(BF16)
            else:
                for blk in range(merge):
                    o_ref[blk] = acc_ref[:, blk * out_block_cols:(blk + 1) * out_block_cols].astype(BF16)

    if out_blocks is None:
        out_spec = pl.BlockSpec((t1, t2), lambda i, j, k: (i, j))
        out_shape = jax.ShapeDtypeStruct((k1, k2), BF16)
    else:
        out_spec = pl.BlockSpec((merge, t1, out_block_cols), lambda i, j, k: (j, i, 0))
        out_shape = jax.ShapeDtypeStruct((out_blocks, k1, out_block_cols), BF16)
    return _carrier_call(
        body, (a, b), comm, name=name,
        grid=(k1 // t1, k2 // t2, nk),
        in_specs=[pl.BlockSpec((tt, t1), lambda i, j, k: (k, i)),
                  pl.BlockSpec((tt, t2), lambda i, j, k: (k, j))],
        out_specs=out_spec, out_shape=out_shape,
        scratch_shapes=[pltpu.VMEM((t1, t2), F32)],
        sem=("parallel", "parallel", "arbitrary"))


def _dgrad_blocked_call(name, g, w_g, comm=None, *, merge=1):
    t = g.shape[0]
    nb, d, cb = w_g.shape
    tm = _tile(t, 1024)
    tn = _tile(d, 2048)
    tk = merge * cb

    def body(g_ref, w_ref, o_ref):
        @pl.when(pl.program_id(2) == 0)
        def _():
            o_ref[...] = jnp.zeros_like(o_ref)

        w = w_ref[0] if merge == 1 else jnp.concatenate([w_ref[b] for b in range(merge)], axis=1)
        o_ref[...] += _dot_nt(g_ref[...], w)

    return _carrier_call(
        body, (g, w_g), comm, name=name,
        grid=(t // tm, d // tn, nb // merge),
        in_specs=[pl.BlockSpec((tm, tk), lambda i, j, k: (i, k)),
                  pl.BlockSpec((merge, tn, cb), lambda i, j, k: (k, j, 0))],
        out_specs=pl.BlockSpec((tm, tn), lambda i, j, k: (i, j)),
        out_shape=jax.ShapeDtypeStruct((t, d), F32),
        sem=("parallel", "parallel", "arbitrary"))


def _dmixed_call(dx2b, w_out, comm=None):
    t, d = dx2b.shape
    e = w_out.shape[0]
    tm = _tile(t, 1024)
    tn = _tile(e, 1024)

    def body(g_ref, w_ref, o_ref):
        o_ref[...] = _dot_nt(g_ref[...], w_ref[...])

    return _carrier_call(
        body, (dx2b, w_out), comm, name="dmixed_bwd",
        grid=(t // tm, e // tn),
        in_specs=[pl.BlockSpec((tm, d), lambda i, j: (i, 0)),
                  pl.BlockSpec((tn, d), lambda i, j: (j, 0))],
        out_specs=pl.BlockSpec((tm, tn), lambda i, j: (i, j)),
        out_shape=jax.ShapeDtypeStruct((t, e), F32),
        sem=("parallel", "parallel"))


def _mixer_bwd_call(proj, dmixed, w_s, bs_t, g_v, w_pool_g, pool_scale, comm=None):
    t = proj.shape[0]
    tt = _tile(t, 512)
    nchunk = tt // CHUNK
    hb = tt // HALO
    last_halo = t // HALO - 1
    nsteps = t // tt
    rb = GROUP // N_DEV

    def body(pu_ref, pv_ref, z_ref, zp_ref, da_ref, db_ref, dbn_ref, ws_ref, bs_ref, gv_ref, wp_ref, ps_ref,
             dproj_ref, dws_ref, dbs_ref, dgv_ref, dps_ref, dwp_ref):
        i = pl.program_id(0)

        @pl.when(i == 0)
        def _():
            dws_ref[...] = jnp.zeros_like(dws_ref)
            dbs_ref[...] = jnp.zeros_like(dbs_ref)
            dgv_ref[...] = jnp.zeros_like(dgv_ref)
            dps_ref[...] = jnp.zeros_like(dps_ref)
            dwp_ref[...] = jnp.zeros_like(dwp_ref)

        tril = (lax.broadcasted_iota(jnp.int32, (CHUNK, CHUNK), 0)
                >= lax.broadcasted_iota(jnp.int32, (CHUNK, CHUNK), 1))
        for h in range(N_HEADS):
            cols = slice(h * CHUNK, (h + 1) * CHUNK)
            v, dv_dpv = _gelu_and_grad(pv_ref[:, cols])
            vhat, rv = _rms_rows(v)
            gv = gv_ref[:, cols]
            vn = (vhat * gv).astype(BF16)
            u, du_dpu = _gelu_and_grad(pu_ref[:, cols])
            w = jnp.where(tril, ws_ref[h], 0.0).astype(BF16)
            bcol = bs_ref[:, h:h + 1]
            dout = da_ref[:, cols]
            dmix = dout * u
            dmix_b = dmix.astype(BF16)
            dws = jnp.zeros((CHUNK, CHUNK), F32)
            dbs = jnp.zeros((CHUNK, 1), F32)
            dvn_parts = []
            du_parts = []
            for c in range(nchunk):
                rows = slice(c * CHUNK, (c + 1) * CHUNK)
                mixed = _dot_nn(w, vn[rows]) + bcol
                du_parts.append(dout[rows] * mixed)
                dvn_parts.append(_dot_tn(w, dmix_b[rows]))
                dws = dws + _dot_nt(dmix_b[rows], vn[rows])
                dbs = dbs + jnp.sum(dmix[rows], axis=1, keepdims=True)
            dws_ref[h] += jnp.where(tril, dws, 0.0)
            dbs_ref[:, h:h + 1] += dbs
            dvn = jnp.concatenate(dvn_parts, axis=0)
            du = jnp.concatenate(du_parts, axis=0)
            dgv_ref[:, cols] += jnp.sum(dvn * vhat, axis=0, keepdims=True)
            dv = _rms_bwd_rows(dvn * gv, vhat, rv)
            dproj_ref[:, cols] = (du * du_dpu).astype(BF16)
            dproj_ref[:, A_WIDTH + h * CHUNK:A_WIDTH + (h + 1) * CHUNK] = (dv * dv_dpv).astype(BF16)

        zprev = jnp.where(i > 0, zp_ref[...], 0.0)
        ext = jnp.concatenate([zprev, z_ref[...]], axis=0)
        dnext = jnp.where(i < nsteps - 1, dbn_ref[...], 0.0)
        dext = jnp.concatenate([db_ref[...], dnext], axis=0)
        for g, win in enumerate(POOL_WINDOWS):
            cols = slice(g * GROUP, (g + 1) * GROUP)
            zg = ext[:, cols]
            pooled = _window_sum_back(zg, win)[HALO:] / _pool_counts(i * tt, tt, win) - zg[HALO:]
            pooled_b = pooled.astype(BF16)
            wp = wp_ref[:, g].reshape(GROUP, GROUP)
            y = _dot_nn(pooled_b, wp)
            dout = dext[:, cols]
            dps_ref[:, cols] += jnp.sum(dout[:tt] * y, axis=0, keepdims=True)
            dy_b = (dout * ps_ref[:, cols]).astype(BF16)
            dwp_ref[:, g] += _dot_tn(pooled_b, dy_b[:tt]).reshape(N_DEV, rb, GROUP)
            dpooled = _dot_nt(dy_b, wp)
            q = dpooled / _pool_counts(i * tt, tt + HALO, win)
            dz = _window_sum_fwd(q, win)[:tt] - dpooled[:tt]
            dproj_ref[:, 2 * A_WIDTH + g * GROUP:2 * A_WIDTH + (g + 1) * GROUP] = dz.astype(BF16)

    def full(shape):
        return pl.BlockSpec(shape, lambda i: (0,) * len(shape))

    return _carrier_call(
        body, (proj, proj, proj, proj, dmixed, dmixed, dmixed, w_s, bs_t, g_v, w_pool_g, pool_scale), comm,
        name="mixer_bwd",
        grid=(nsteps,),
        in_specs=[pl.BlockSpec((tt, A_WIDTH), lambda i: (i, 0)),
                  pl.BlockSpec((tt, A_WIDTH), lambda i: (i, 1)),
                  pl.BlockSpec((tt, B_WIDTH), lambda i: (i, 2)),
                  pl.BlockSpec((HALO, B_WIDTH), lambda i: (jnp.maximum(i * hb - 1, 0), 2)),
                  pl.BlockSpec((tt, A_WIDTH), lambda i: (i, 0)),
                  pl.BlockSpec((tt, B_WIDTH), lambda i: (i, 1)),
                  pl.BlockSpec((HALO, B_WIDTH), lambda i: (jnp.minimum((i + 1) * hb, last_halo), 1)),
                  full((N_HEADS, CHUNK, CHUNK)), full((CHUNK, N_HEADS)), full((1, A_WIDTH)),
                  full((N_DEV, 4, rb, GROUP)), full((1, B_WIDTH))],
        out_specs=[pl.BlockSpec((tt, 2 * A_WIDTH + B_WIDTH), lambda i: (i, 0)),
                   full((N_HEADS, CHUNK, CHUNK)), full((CHUNK, N_HEADS)), full((1, A_WIDTH)),
                   full((1, B_WIDTH)), full((N_DEV, 4, rb, GROUP))],
        out_shape=[jax.ShapeDtypeStruct((t, 2 * A_WIDTH + B_WIDTH), BF16),
                   jax.ShapeDtypeStruct((N_HEADS, CHUNK, CHUNK), F32),
                   jax.ShapeDtypeStruct((CHUNK, N_HEADS), F32),
                   jax.ShapeDtypeStruct((1, A_WIDTH), F32),
                   jax.ShapeDtypeStruct((1, B_WIDTH), F32),
                   jax.ShapeDtypeStruct((N_DEV, 4, rb, GROUP), F32)],
        sem=("arbitrary",))


def _adamw(w, g, m, v):
    m = ADAM_B1 * m + (1.0 - ADAM_B1) * g
    v = ADAM_B2 * v + (1.0 - ADAM_B2) * (g * g)
    m_hat = m / ADAM_C1
    v_hat = v / ADAM_C2
    delta = -ADAM_LR * (m_hat / (jnp.sqrt(v_hat) + ADAM_EPS) + ADAM_WD * w)
    return delta, m, v


PAIR_SUM_TILE_ELEMS = 1024 * 1024
ADAMW_TILE_ELEMS = 512 * 1024


def _row_tile(r, c, elems):
    t = r
    while t * c > elems and t % 32 == 0:
        t //= 2
    return t


def _pair_sum_call(name, pos, grad, got):
    _, r, c = grad.shape
    tr = _row_tile(r, c, PAIR_SUM_TILE_ELEMS)

    def chip_of(rel, pos_ref):
        px = jnp.where((rel == 0) | (rel == 2), 1 - pos_ref[0], pos_ref[0])
        py = jnp.where((rel == 1) | (rel == 2), 1 - pos_ref[1], pos_ref[1])
        return 2 * px + py

    def body(pos_ref, own_ref, got_ref, out_ref):
        out_ref[...] = (own_ref[...].astype(F32) + got_ref[...].astype(F32)).astype(BF16)

    return pl.pallas_call(
        body, name=name,
        grid_spec=pltpu.PrefetchScalarGridSpec(
            num_scalar_prefetch=1, grid=(3, r // tr),
            in_specs=[pl.BlockSpec((None, tr, c), lambda k, i, p: (2 * chip_of(k, p) + p[2], i, 0)),
                      pl.BlockSpec((None, tr, c), lambda k, i, p: (chip_of(k, p), i, 0))],
            out_specs=pl.BlockSpec((None, tr, c), lambda k, i, p: (k, i, 0))),
        out_shape=jax.ShapeDtypeStruct((3, r, c), BF16),
        compiler_params=_params("parallel", "parallel"),
    )(pos, grad, got)


def _final_call(name, pos, grad, got_pair, got_chips, w, m, v):
    _, r, c = grad.shape
    tr = _row_tile(r, c, ADAMW_TILE_ELEMS)

    def body(pos_ref, own_ref, pair_ref, chips_ref, w_ref, m_ref, v_ref, g_out, d_out, m_out, v_out):
        g = own_ref[...].astype(F32) + pair_ref[...].astype(F32)
        for j in range(3):
            g = g + chips_ref[j].astype(F32)
        delta, m_new, v_new = _adamw(w_ref[...], g, m_ref[...], v_ref[...])
        g_out[...] = g
        d_out[...] = delta
        m_out[...] = m_new
        v_out[...] = v_new

    row = pl.BlockSpec((tr, c), lambda i, p: (i, 0))
    return pl.pallas_call(
        body, name=name,
        grid_spec=pltpu.PrefetchScalarGridSpec(
            num_scalar_prefetch=1, grid=(r // tr,),
            in_specs=[pl.BlockSpec((None, tr, c), lambda i, p: (4 * p[0] + 2 * p[1] + p[2], i, 0)),
                      pl.BlockSpec((None, tr, c), lambda i, p: (2 * p[0] + p[1], i, 0)),
                      pl.BlockSpec((3, tr, c), lambda i, p: (0, i, 0)), row, row, row],
            out_specs=[row] * 4),
        out_shape=[jax.ShapeDtypeStruct((r, c), F32)] * 4,
        compiler_params=_params("parallel"),
    )(pos, grad, got_pair, got_chips, w, m, v)


def _small_final_call(name, parts, w, m, v):
    _, rows, c = parts.shape
    r = w.shape[0]

    def body(p_ref, w_ref, m_ref, v_ref, g_out, d_out, m_out, v_out):
        g = p_ref[0]
        for k in range(1, N_DEV):
            g = g + p_ref[k]
        delta, m_new, v_new = _adamw(w_ref[...], g[:r], m_ref[...], v_ref[...])
        g_out[...] = g
        d_out[...] = delta
        m_out[...] = m_new
        v_out[...] = v_new

    return pl.pallas_call(
        body, name=name,
        out_shape=[jax.ShapeDtypeStruct((rows, c), F32)] + [jax.ShapeDtypeStruct((r, c), F32)] * 3,
        compiler_params=pltpu.CompilerParams(vmem_limit_bytes=VMEM_LIMIT),
    )(parts, w, m, v)


_SMALL_EARLY = ("g_v", "w_s", "b_s", "pool_scale", "g_ffn", "g_final")
_BIG = ("w_in", "w_pool", "w_out", "w_up", "w_down")
_ORDER = ("g_mix", "w_in", "g_v", "w_s", "b_s", "w_pool", "pool_scale", "w_out", "g_ffn", "w_up", "w_down", "g_final")


def _pack(parts):
    return jnp.concatenate([p.reshape(-1, LANES) for p in parts], axis=0)


def _unpack(packed, like):
    out, row = [], 0
    for a in like:
        rows = a.size // LANES
        out.append(packed[row:row + rows].reshape(a.shape))
        row += rows
    return out


def kernel(x, g_mix, w_in, g_v, w_s, b_s, w_pool, pool_scale, w_out, g_ffn, w_up, w_down, g_final, loss_target, m_g_mix, m_w_in, m_g_v, m_w_s, m_b_s, m_w_pool, m_pool_scale, m_w_out, m_g_ffn, m_w_up, m_w_down, m_g_final, v_g_mix, v_w_in, v_g_v, v_w_s, v_b_s, v_w_pool, v_pool_scale, v_w_out, v_g_ffn, v_w_up, v_w_down, v_g_final):
    weights = dict(g_mix=g_mix, w_in=w_in, g_v=g_v, w_s=w_s, b_s=b_s, w_pool=w_pool, pool_scale=pool_scale,
                   w_out=w_out, g_ffn=g_ffn, w_up=w_up, w_down=w_down, g_final=g_final)
    mom = dict(g_mix=m_g_mix, w_in=m_w_in, g_v=m_g_v, w_s=m_w_s, b_s=m_b_s, w_pool=m_w_pool,
               pool_scale=m_pool_scale, w_out=m_w_out, g_ffn=m_g_ffn, w_up=m_w_up, w_down=m_w_down,
               g_final=m_g_final)
    var = dict(g_mix=v_g_mix, w_in=v_w_in, g_v=v_g_v, w_s=v_w_s, b_s=v_b_s, w_pool=v_w_pool,
               pool_scale=v_pool_scale, w_out=v_w_out, g_ffn=v_g_ffn, w_up=v_w_up, w_down=v_w_down,
               g_final=v_g_final)

    t, d = x.shape[1], x.shape[2]
    xs = x.reshape(t, d)
    target = loss_target.reshape(t, d)

    shard2d = dict(w_in=w_in.reshape(d, -1), w_pool=w_pool.reshape(-1, GROUP), w_out=w_out.reshape(-1, d),
                   w_up=w_up.reshape(d, -1), w_down=w_down.reshape(-1, d))
    sb = {k: shard2d[k].astype(BF16) for k in _BIG}
    rows = {k: sb[k].shape[0] for k in _BIG}

    def gathered_shape(k):
        return jax.ShapeDtypeStruct((N_DEV,) + sb[k].shape, BF16)

    def landing(n, like):
        return jax.ShapeDtypeStruct((n,) + like.shape[1:], like.dtype)

    def from_everyone(block):
        return jax.ShapeDtypeStruct((N_DEV,) + block.shape, block.dtype)

    def cuts(r, fractions):
        return [0] + [int(r * f) // 16 * 16 for f in fractions] + [r]

    g_mix2, g_ffn2, g_final2 = g_mix.reshape(1, d), g_ffn.reshape(1, d), g_final.reshape(1, d)
    g_v2, ps2 = g_v.reshape(1, A_WIDTH), pool_scale.reshape(1, B_WIDTH)
    w_s3 = w_s.reshape(N_HEADS, CHUNK, CHUNK)
    bs_t = b_s.reshape(N_HEADS, CHUNK).T
    xi, yi, ci = _position()
    pos = jnp.stack([xi, yi, ci]).astype(jnp.int32)

    w_in_g, w_pool_g = _comm_call("gather_w_in_w_pool", _Comm(
        [sb["w_in"], sb["w_pool"]], [], [gathered_shape("w_in"), gathered_shape("w_pool")],
        lambda s, l: [_gather_first(s[0], l[0], 0, rows["w_in"]) + _gather_first(s[1], l[1], 0, rows["w_pool"]),
                      _gather_pass_on(l[0], 0, rows["w_in"]) + _gather_pass_on(l[1], 0, rows["w_pool"])]))
    w_pool_g = w_pool_g.reshape(N_DEV, 4, GROUP // N_DEV, GROUP)

    u = cuts(rows["w_up"], (0.15, 0.35, 0.60))
    ahead = cuts(rows["w_down"], (0.4,))[1]
    proj, h1, w_out_g, w_up_g = _proj_call(xs, g_mix2, w_in_g, _Comm(
        [sb["w_out"], sb["w_up"]], [], [gathered_shape("w_out"), gathered_shape("w_up")],
        lambda s, l: [_gather_first(s[0], l[0], 0, rows["w_out"]) + _gather_first(s[1], l[1], u[0], u[1])]))
    mixed, w_out_g, w_up_g = _mixer_fwd_call(proj, w_s3, bs_t, g_v2, w_pool_g, ps2, _Comm(
        [sb["w_up"]], [w_out_g, w_up_g], [],
        lambda s, l: [_gather_pass_on(l[0], 0, rows["w_out"]) + _gather_pass_on(l[1], u[0], u[1])
                      + _gather_first(s[0], l[1], u[1], u[2])]))
    w_out_f = w_out_g.reshape(-1, d)
    x2, w_up_g = _out_proj_call(mixed, w_out_f, xs, _Comm(
        [sb["w_up"]], [w_up_g], [],
        lambda s, l: [_gather_pass_on(l[0], u[1], u[2]) + _gather_first(s[0], l[0], u[2], u[3])]))
    order = (4 * xi + 2 * yi + ci) ^ jnp.array(ARRIVAL_ORDER, jnp.int32)
    act, h2, w_up_g, w_down_g = _up_call(x2, g_ffn2, sb["w_up"], order, w_up_g, u[2], u[3], sb["w_down"], ahead)
    y, w_down_g = _down_call(act, sb["w_down"], order, w_down_g, ahead)
    w_down_f = w_down_g.reshape(-1, d)
    loss_part, dx3, dx3b, dg_final = _loss_call(y, x2, target, g_final2)

    def pair_sum(k, grad, got):
        return _pair_sum_call(k + "_pair_sum", pos, grad, got)

    def finish(k, grad, got_pair, got_chips):
        s = shard2d[k]
        outs = _final_call(k + "_adamw", pos, grad, got_pair, got_chips, s, mom[k].reshape(s.shape),
                           var[k].reshape(s.shape))
        return [o.reshape(weights[k].shape) for o in outs]

    result = {}
    (gw_down,) = _wgrad_call("w_down_grad", act, dx3b, None, None, t1=1024, t2=2048)
    gw_down = gw_down.reshape(N_DEV, -1, d)
    da, pair_down = _dact_call(dx3b, w_down_f, act, _Comm(
        [gw_down], [], [landing(4, gw_down)], lambda s, l: [_pair_exchange(s[0], l[0])]))
    sums_down = pair_sum("w_down", gw_down, pair_down)
    dn = cuts(rows["w_down"], (0.8,))
    gw_up, got = _wgrad_call("w_up_grad", h2, da, N_DEV, w_up_g.shape[2], _Comm(
        [sums_down], [], [landing(3, sums_down)],
        lambda s, l: [_chip_exchange(s[0], l[0], dn[0], dn[1])]), t1=2048)
    dh2, got, pair_up = _dgrad_blocked_call("dh2_bwd", da, w_up_g, _Comm(
        [sums_down, gw_up], [got], [landing(4, gw_up)],
        lambda s, l: [_chip_exchange(s[0], l[0], dn[1], dn[2]) + _pair_exchange(s[1], l[1])]))
    result["w_down"] = finish("w_down", gw_down, pair_down, got)
    sums_up = pair_sum("w_up", gw_up, pair_up)
    v = cuts(rows["w_up"], (0.28, 0.52, 0.84))
    dx2, dx2b, dg_ffn, got_up = _norm_bwd_call("ffn_norm_bwd", dh2, x2, dx3, g_ffn2, True, _Comm(
        [sums_up], [], [landing(3, sums_up)], lambda s, l: [_chip_exchange(s[0], l[0], v[0], v[1])]))
    dmixed, got_up = _dmixed_call(dx2b, w_out_f, _Comm(
        [sums_up], [got_up], [], lambda s, l: [_chip_exchange(s[0], l[0], v[1], v[2])]))
    gw_out, got_up = _wgrad_call("w_out_grad", mixed, dx2b, None, None, _Comm(
        [sums_up], [got_up], [], lambda s, l: [_chip_exchange(s[0], l[0], v[2], v[3])]), t1=2048, t2=1024)
    gw_out = gw_out.reshape(N_DEV, -1, d)
    dproj, dw_s, dbs_t, dg_v, dps, dw_pool, got_up, pair_out = _mixer_bwd_call(
        proj, dmixed, w_s3, bs_t, g_v2, w_pool_g, ps2, _Comm(
            [sums_up, gw_out], [got_up], [landing(4, gw_out)],
            lambda s, l: [_chip_exchange(s[0], l[0], v[3], v[4]) + _pair_exchange(s[1], l[1])]))
    result["w_up"] = finish("w_up", gw_up, pair_up, got_up)
    sums_out = pair_sum("w_out", gw_out, pair_out)
    gw_pool = dw_pool.astype(BF16).reshape(N_DEV, -1, GROUP)
    early = dict(g_v=dg_v, w_s=dw_s, b_s=dbs_t.T, pool_scale=dps, g_ffn=dg_ffn, g_final=dg_final)
    packed = _pack([early[k] for k in _SMALL_EARLY] + [loss_part])
    gw_in, got, pair_pool, parts_early = _wgrad_call("w_in_grad", h1, dproj, N_DEV, w_in_g.shape[2], _Comm(
        [sums_out, gw_pool, packed], [], [landing(3, sums_out), landing(4, gw_pool), from_everyone(packed)],
        lambda s, l: [_chip_exchange(s[0], l[0], 0, rows["w_out"]) + _pair_exchange(s[1], l[1])
                      + _everyone(s[2], l[2])]), t1=2048, merge=MERGE_W_IN)
    result["w_out"] = finish("w_out", gw_out, pair_out, got)
    sums_pool = pair_sum("w_pool", gw_pool, pair_pool)
    (pair_in,) = _comm_call("pair_exchange_w_in", _Comm(
        [gw_in], [], [landing(4, gw_in)], lambda s, l: [_pair_exchange(s[0], l[0])]))
    sums_in = pair_sum("w_in", gw_in, pair_in)
    dh1, got, got_pool = _dgrad_blocked_call("dh1_bwd", dproj, w_in_g, _Comm(
        [sums_in, sums_pool], [], [landing(3, sums_in), landing(3, sums_pool)],
        lambda s, l: [_chip_exchange(s[0], l[0], 0, rows["w_in"]) + _chip_exchange(s[1], l[1], 0, rows["w_pool"])]),
        merge=MERGE_W_IN)
    result["w_in"] = finish("w_in", gw_in, pair_in, got)
    result["w_pool"] = finish("w_pool", gw_pool, pair_pool, got_pool)
    grad_x, dg_mix = _norm_bwd_call("mix_norm_bwd", dh1, xs, dx2, g_mix2, False)
    packed = _pack([dg_mix])
    (parts_late,) = _comm_call("gather_g_mix_grad", _Comm(
        [packed], [], [from_everyone(packed)], lambda s, l: [_everyone(s[0], l[0])]))

    for names, parts, tag in ((_SMALL_EARLY, parts_early, "small_adamw"), (("g_mix",), parts_late, "g_mix_adamw")):
        outs = _small_final_call(tag, parts, _pack([weights[k] for k in names]), _pack([mom[k] for k in names]),
                                 _pack([var[k] for k in names]))
        if tag == "small_adamw":
            loss = outs[0][-1, 0]
        like = [weights[k] for k in names]
        unpacked = [_unpack(o, like) for o in outs]
        for idx, k in enumerate(names):
            result[k] = [unpacked[q][idx] for q in range(4)]

    grads = [result[k][0] for k in _ORDER]
    deltas = [result[k][1] for k in _ORDER]
    new_m = [result[k][2] for k in _ORDER]
    new_v = [result[k][3] for k in _ORDER]
    return (loss, grad_x.reshape(x.shape), *grads, *deltas, *new_m, *new_v)
```

```python
import functools
import math

import jax
import jax.numpy as jnp
from jax import lax
from jax.experimental import pallas as pl
from jax.experimental.pallas import tpu as pltpu

F32 = jnp.float32
BF16 = jnp.bfloat16
MESH = pl.DeviceIdType.MESH

N_DEV = 8
EPS = 1e-6
CHUNK = 128
N_HEADS = 8
A_WIDTH = 1024
B_WIDTH = 1024
POOL_WINDOWS = (2, 4, 8, 16)
GROUP = 256
HALO = 16
LANES = 128

ADAM_LR = 0.001
ADAM_B1 = 0.9
ADAM_B2 = 0.999
ADAM_EPS = 1e-08
ADAM_WD = 0.01
ADAM_STEP = 10
ADAM_C1 = 1.0 - ADAM_B1 ** ADAM_STEP
ADAM_C2 = 1.0 - ADAM_B2 ** ADAM_STEP

VMEM_LIMIT = 56 * 1024 * 1024
MERGE_W_IN = 2

_GELU_C = math.sqrt(2.0 / math.pi)


def _params(*sem):
    return pltpu.CompilerParams(dimension_semantics=sem, vmem_limit_bytes=VMEM_LIMIT)


def _gelu(x):
    return 0.5 * x * (1.0 + jnp.tanh(_GELU_C * (x + 0.044715 * x * x * x)))


def _gelu_and_grad(x):
    t = jnp.tanh(_GELU_C * (x + 0.044715 * x * x * x))
    g = 0.5 * x * (1.0 + t)
    dg = 0.5 * (1.0 + t) + 0.5 * x * (1.0 - t * t) * (_GELU_C * (1.0 + 3.0 * 0.044715 * x * x))
    return g, dg


def _dot_nn(a, b):
    return lax.dot_general(a, b, (((1,), (0,)), ((), ())), preferred_element_type=F32)


def _dot_nt(a, b):
    return lax.dot_general(a, b, (((1,), (1,)), ((), ())), preferred_element_type=F32)


def _dot_tn(a, b):
    return lax.dot_general(a, b, (((0,), (0,)), ((), ())), preferred_element_type=F32)


def _rms_rows(x):
    r = lax.rsqrt(jnp.mean(x * x, axis=-1, keepdims=True) + EPS)
    return x * r, r


def _rms_bwd_rows(dn, n, r):
    return r * (dn - n * jnp.mean(dn * n, axis=-1, keepdims=True))


def _tile(n, want):
    t = min(n, want)
    assert n % t == 0, (n, want)
    return t


_ANY = pl.BlockSpec(memory_space=pl.ANY)

SIBLING = 1
CHIPS = (4, 2, 6)


def _position():
    return lax.axis_index("x"), lax.axis_index("y"), lax.axis_index("c")


def _me():
    x, y, c = _position()
    return 4 * x + 2 * y + c


def _peer(rel):
    x, y, c = _position()
    return (x ^ ((rel >> 2) & 1), y ^ ((rel >> 1) & 1), c ^ (rel & 1))


class _Comm:
    def __init__(self, srcs, lands, new, plan):
        self.srcs, self.lands, self.new, self.plan = list(srcs), list(lands), list(new), plan


def _make_copies(phases, send_sems, recv_sems, local_sems):
    out, nr, nl = [], 0, 0
    for phase in phases:
        cps = []
        for item in phase:
            if item[0] == "local":
                cps.append(pltpu.make_async_copy(item[1], item[2], local_sems.at[nl]))
                nl += 1
            else:
                cps.append(pltpu.make_async_remote_copy(
                    src_ref=item[1], dst_ref=item[2], send_sem=send_sems.at[nr], recv_sem=recv_sems.at[nr],
                    device_id=_peer(item[3]), device_id_type=MESH))
                nr += 1
        out.append(cps)
    return out


def _count_copies(comm):
    phases = comm.plan([_FakeRef() for _ in comm.srcs], [_FakeRef() for _ in range(len(comm.lands) + len(comm.new))])
    items = [it for ph in phases for it in ph]
    return sum(it[0] == "remote" for it in items), sum(it[0] == "local" for it in items)


class _FakeRef:
    def __getitem__(self, idx):
        return self

    @property
    def at(self):
        return self


def _carrier_call(body, args, comm, *, name, grid, in_specs, out_specs, out_shape, scratch_shapes=(), sem):
    if not isinstance(out_shape, (list, tuple)):
        out_specs, out_shape = [out_specs], [out_shape]
    out_specs, out_shape, scratch_shapes = list(out_specs), list(out_shape), list(scratch_shapes)
    if comm is None:
        res = pl.pallas_call(body, name=name, grid=grid, in_specs=list(in_specs), out_specs=out_specs,
                             out_shape=out_shape, scratch_shapes=scratch_shapes, compiler_params=_params(*sem))(*args)
        return list(res)
    n_in, n_out, n_scr = len(args), len(out_shape), len(scratch_shapes)
    ns, nl, nn = len(comm.srcs), len(comm.lands), len(comm.new)
    n_remote, n_local = _count_copies(comm)

    def wrapped(*refs):
        ins, srcs = refs[:n_in], refs[n_in:n_in + ns]
        o = n_in + ns + nl
        outs, lands = refs[o:o + n_out], refs[o + n_out:o + n_out + nl + nn]
        scr = refs[o + n_out + nl + nn:]
        (copies,) = _make_copies(comm.plan(srcs, lands), *scr[n_scr:])
        ids = [pl.program_id(a) for a in range(len(grid))]
        first = functools.reduce(jnp.logical_and, [i == 0 for i in ids])
        last = functools.reduce(jnp.logical_and, [i == g - 1 for i, g in zip(ids, grid)])

        @pl.when(first)
        def _():
            for cp in copies:
                cp.start()

        body(*ins, *outs, *scr[:n_scr])

        @pl.when(last)
        def _():
            for cp in copies:
                cp.wait()

    land_shapes = [jax.ShapeDtypeStruct(a.shape, a.dtype) for a in comm.lands] + comm.new
    sems = [pltpu.SemaphoreType.DMA((max(n_remote, 1),)), pltpu.SemaphoreType.DMA((max(n_remote, 1),)),
            pltpu.SemaphoreType.DMA((max(n_local, 1),))]
    res = pl.pallas_call(
        wrapped, name=name, grid=grid,
        in_specs=list(in_specs) + [_ANY] * (ns + nl), out_specs=out_specs + [_ANY] * (nl + nn),
        out_shape=out_shape + land_shapes, scratch_shapes=scratch_shapes + sems,
        input_output_aliases={n_in + ns + k: n_out + k for k in range(nl)},
        compiler_params=_params(*sem))(*args, *comm.srcs, *comm.lands)
    return list(res)


def _comm_call(name, comm):
    ns, nl, nn = len(comm.srcs), len(comm.lands), len(comm.new)
    n_remote, n_local = _count_copies(comm)

    def body(*refs):
        srcs, lands, sems = refs[:ns], refs[ns + nl:ns + nl + nl + nn], refs[ns + nl + nl + nn:]
        for copies in _make_copies(comm.plan(srcs, lands), *sems):
            for cp in copies:
                cp.start()
            for cp in copies:
                cp.wait()

    land_shapes = [jax.ShapeDtypeStruct(a.shape, a.dtype) for a in comm.lands] + comm.new
    res = pl.pallas_call(
        body, name=name,
        in_specs=[_ANY] * (ns + nl), out_specs=[_ANY] * (nl + nn), out_shape=land_shapes,
        scratch_shapes=[pltpu.SemaphoreType.DMA((max(n_remote, 1),)), pltpu.SemaphoreType.DMA((max(n_remote, 1),)),
                        pltpu.SemaphoreType.DMA((max(n_local, 1),))],
        input_output_aliases={ns + k: k for k in range(nl)},
    )(*comm.srcs, *comm.lands)
    return list(res)


def _rows(ref, block, r0, r1):
    return ref.at[block, pl.ds(r0, r1 - r0)]


def _gather_first(shard, land, r0, r1):
    src = shard.at[pl.ds(r0, r1 - r0)]
    dst = _rows(land, _me(), r0, r1)
    return [("local", src, dst)] + [("remote", src, dst, rel) for rel in (SIBLING,) + CHIPS]


def _gather_pass_on(land, r0, r1):
    return [("remote", _rows(land, _me() ^ rel, r0, r1), _rows(land, _me() ^ rel, r0, r1), SIBLING) for rel in CHIPS]


def _pair_exchange(grad, land):
    _, _, c = _position()
    return [("remote", grad.at[2 * chip + (1 - c)], land.at[chip], SIBLING) for chip in range(4)]


def _chip_exchange(sums, land, r0, r1):
    return [("remote", _rows(sums, j, r0, r1), _rows(land, j, r0, r1), rel) for j, rel in enumerate(CHIPS)]


def _everyone(packed, land):
    dst = land.at[_me()]
    return [("local", packed, dst)] + [("remote", packed, dst, rel) for rel in range(1, N_DEV)]


def _pool_counts(row0, rows, win):
    pos = row0 + lax.broadcasted_iota(jnp.int32, (rows, 1), 0)
    return jnp.minimum(pos + 1, win).astype(F32)


def _window_sum_back(ext, win):
    s = ext
    k = 1
    while k < win:
        s = s + pltpu.roll(s, k, 0)
        k *= 2
    return s


def _window_sum_fwd(ext, win):
    n = ext.shape[0]
    s = ext
    k = 1
    while k < win:
        s = s + pltpu.roll(s, n - k, 0)
        k *= 2
    return s


def _mixer_fwd_call(proj, w_s, bs_t, g_v, w_pool_g, pool_scale, comm=None):
    t = proj.shape[0]
    tt = _tile(t, 512)
    nchunk = tt // CHUNK
    hb = tt // HALO

    def body(pu_ref, pv_ref, z_ref, zp_ref, ws_ref, bs_ref, gv_ref, wp_ref, ps_ref, out_ref):
        i = pl.program_id(0)
        tril = (lax.broadcasted_iota(jnp.int32, (CHUNK, CHUNK), 0)
                >= lax.broadcasted_iota(jnp.int32, (CHUNK, CHUNK), 1))
        for h in range(N_HEADS):
            cols = slice(h * CHUNK, (h + 1) * CHUNK)
            vhat, _ = _rms_rows(_gelu(pv_ref[:, cols]))
            vn = (vhat * gv_ref[:, cols]).astype(BF16)
            u = _gelu(pu_ref[:, cols])
            w = jnp.where(tril, ws_ref[h], 0.0).astype(BF16)
            bcol = bs_ref[:, h:h + 1]
            for c in range(nchunk):
                rows = slice(c * CHUNK, (c + 1) * CHUNK)
                mixed = _dot_nn(w, vn[rows]) + bcol
                out_ref[rows, cols] = (u[rows] * mixed).astype(BF16)

        zprev = jnp.where(i > 0, zp_ref[...], 0.0)
        ext = jnp.concatenate([zprev, z_ref[...]], axis=0)
        for g, win in enumerate(POOL_WINDOWS):
            cols = slice(g * GROUP, (g + 1) * GROUP)
            zg = ext[:, cols]
            s = _window_sum_back(zg, win)
            pooled = s[HALO:] / _pool_counts(i * tt, tt, win) - zg[HALO:]
            wp = wp_ref[:, g].reshape(GROUP, GROUP)
            y = _dot_nn(pooled.astype(BF16), wp)
            out_ref[:, A_WIDTH + g * GROUP:A_WIDTH + (g + 1) * GROUP] = (y * ps_ref[:, cols]).astype(BF16)

    return _carrier_call(
        body, (proj, proj, proj, proj, w_s, bs_t, g_v, w_pool_g, pool_scale), comm, name="mixer_fwd",
        grid=(t // tt,),
        in_specs=[pl.BlockSpec((tt, A_WIDTH), lambda i: (i, 0)),
                  pl.BlockSpec((tt, A_WIDTH), lambda i: (i, 1)),
                  pl.BlockSpec((tt, B_WIDTH), lambda i: (i, 2)),
                  pl.BlockSpec((HALO, B_WIDTH), lambda i: (jnp.maximum(i * hb - 1, 0), 2)),
                  pl.BlockSpec((N_HEADS, CHUNK, CHUNK), lambda i: (0, 0, 0)),
                  pl.BlockSpec((CHUNK, N_HEADS), lambda i: (0, 0)),
                  pl.BlockSpec((1, A_WIDTH), lambda i: (0, 0)),
                  pl.BlockSpec((N_DEV, 4, GROUP // N_DEV, GROUP), lambda i: (0, 0, 0, 0)),
                  pl.BlockSpec((1, B_WIDTH), lambda i: (0, 0))],
        out_specs=pl.BlockSpec((tt, A_WIDTH + B_WIDTH), lambda i: (i, 0)),
        out_shape=jax.ShapeDtypeStruct((t, A_WIDTH + B_WIDTH), BF16),
        sem=("parallel",))


def _out_proj_call(mixed, w_out, x, comm=None):
    t, d = x.shape
    k = mixed.shape[1]
    tm = _tile(t, 1024)
    tn = _tile(d, 1024)

    def body(a_ref, w_ref, x_ref, o_ref):
        o_ref[...] = x_ref[...] + _dot_nn(a_ref[...], w_ref[...])

    return _carrier_call(
        body, (mixed, w_out, x), comm, name="out_proj_fwd",
        grid=(t // tm, d // tn),
        in_specs=[pl.BlockSpec((tm, k), lambda i, j: (i, 0)),
                  pl.BlockSpec((k, tn), lambda i, j: (0, j)),
                  pl.BlockSpec((tm, tn), lambda i, j: (i, j))],
        out_specs=pl.BlockSpec((tm, tn), lambda i, j: (i, j)),
        out_shape=jax.ShapeDtypeStruct((t, d), F32),
        sem=("parallel", "parallel"))


ARRIVAL_ORDER = (0, 1, 4, 5, 2, 3, 6, 7)


class _StreamedGather:
    def __init__(self, shard_ref, land_ref, wbuf, pre0, r0, send_sems, recv_sems, local_sem, fetch_sems):
        self.shard, self.land, self.wbuf, self.fetch_sems = shard_ref, land_ref, wbuf, fetch_sems
        n = shard_ref.shape[0] - r0
        me = _me()
        self.me = me

        def remote(k, src, dst, rel):
            return pltpu.make_async_remote_copy(src_ref=src, dst_ref=dst, send_sem=send_sems.at[k],
                                                recv_sem=recv_sems.at[k], device_id=_peer(rel), device_id_type=MESH)

        src = shard_ref.at[pl.ds(r0, n)]
        dst = land_ref.at[me, pl.ds(r0, n)]
        self.mine = pltpu.make_async_copy(src, dst, local_sem)
        self.first = [remote(k, src, dst, rel) for k, rel in enumerate((SIBLING,) + CHIPS)]
        self.passed, self.early = [], []
        for j, rel in enumerate(CHIPS):
            rows = land_ref.at[me ^ rel, pl.ds(r0, n)]
            self.passed.append(remote(4 + j, rows, rows, SIBLING))
            if r0 > pre0:
                rows = land_ref.at[me ^ rel, pl.ds(pre0, r0 - pre0)]
                self.early.append(remote(7 + j, rows, rows, SIBLING))

    def _fetch(self, q):
        src = self.shard if q == 0 else self.land.at[self.me ^ ARRIVAL_ORDER[q]]
        return pltpu.make_async_copy(src, self.wbuf.at[q % 2], self.fetch_sems.at[q % 2])

    def start(self):
        self.mine.start()
        for cp in self.early + self.first:
            cp.start()
        self._fetch(0).start()

    def arrive(self, q):
        if q == 1:
            self.first[0].wait_recv()
        elif q % 2 == 0:
            j = q // 2 - 1
            self.first[1 + j].wait_recv()
            self.passed[j].start()
        else:
            if q == 3:
                for cp in self.early:
                    cp.wait_recv()
            self.passed[q // 2 - 1].wait_recv()
        self._fetch(q).start()

    def wait_fetch(self, slot):
        pltpu.make_async_copy(self.shard, self.wbuf.at[slot], self.fetch_sems.at[slot]).wait()

    def finish(self):
        for cp in self.early + self.first + self.passed:
            cp.wait_send()
        self.mine.wait()


_STREAM_SEMS = [pltpu.SemaphoreType.DMA((10,)), pltpu.SemaphoreType.DMA((10,)), pltpu.SemaphoreType.DMA,
                pltpu.SemaphoreType.DMA((2,))]


def _stream_steps(gather, p, i, ni):
    @pl.when((p == 0) & (i == 0))
    def _():
        gather.start()

    @pl.when(i == 0)
    def _():
        gather.wait_fetch(p % 2)

    @pl.when(i == ni - 1)
    def _():
        for q in range(1, N_DEV):
            @pl.when(p == q - 1)
            def _():
                gather.arrive(q)


def _norm_matmul_stream_call(name, x, g, shard, order, land, pre0, r0, comm, epilogue, out_dtype):
    t, d = x.shape
    cb = shard.shape[1]
    tm = _tile(t, 1024)
    ni = t // tm
    n_sems = len(_STREAM_SEMS)
    assert not comm.lands
    ns, nn = len(comm.srcs), len(comm.new)
    n_remote, n_local = _count_copies(comm)
    has_land = land is not None

    def body(order_ref, x_ref, g_ref, shard_ref, *refs):
        refs = refs[has_land:]
        srcs, (out_ref, h_ref, land_ref), new = refs[:ns], refs[ns:ns + 3], refs[ns + 3:ns + 3 + nn]
        wbuf, sems = refs[ns + 3 + nn], refs[ns + 4 + nn:]
        p, i = pl.program_id(0), pl.program_id(1)
        gather = _StreamedGather(shard_ref, land_ref, wbuf, pre0, r0, *sems[:n_sems])
        (carried,) = _make_copies(comm.plan(srcs, new), *sems[n_sems:])
        rows = pl.ds(pl.multiple_of(i * tm, tm), tm)
        _stream_steps(gather, p, i, ni)

        @pl.when((p == 0) & (i == 0))
        def _():
            for cp in carried:
                cp.start()

        @pl.when(p == 0)
        def _():
            n, _ = _rms_rows(x_ref[...])
            h_ref[rows, :] = (n * g_ref[...]).astype(BF16)

        out_ref[...] = epilogue(_dot_nn(h_ref[rows, :], wbuf[p % 2])).astype(out_dtype)

        @pl.when((p == N_DEV - 1) & (i == ni - 1))
        def _():
            gather.finish()
            for cp in carried:
                cp.wait()

    carried_sems = [pltpu.SemaphoreType.DMA((max(n_remote, 1),)), pltpu.SemaphoreType.DMA((max(n_remote, 1),)),
                    pltpu.SemaphoreType.DMA((max(n_local, 1),))]
    return pl.pallas_call(
        body, name=name,
        grid_spec=pltpu.PrefetchScalarGridSpec(
            num_scalar_prefetch=1, grid=(N_DEV, ni),
            in_specs=[pl.BlockSpec((tm, d), lambda p, i, o: (jnp.where(p == 0, i, ni - 1), 0)),
                      pl.BlockSpec((1, d), lambda p, i, o: (0, 0)),
                      _ANY] + [_ANY] * (has_land + ns),
            out_specs=[pl.BlockSpec((tm, cb), lambda p, i, o: (i, o[p])),
                       pl.BlockSpec(memory_space=pltpu.VMEM),
                       _ANY] + [_ANY] * nn,
            scratch_shapes=[pltpu.VMEM((2, d, cb), BF16)] + _STREAM_SEMS + carried_sems),
        out_shape=[jax.ShapeDtypeStruct((t, N_DEV * cb), out_dtype), jax.ShapeDtypeStruct((t, d), BF16),
                   jax.ShapeDtypeStruct((N_DEV, d, cb), BF16)] + comm.new,
        input_output_aliases={4: 2} if has_land else {},
        compiler_params=_params("arbitrary", "arbitrary"),
    )(order, x, g, shard, *([land] if has_land else []), *comm.srcs)


def _down_call(act, shard, order, land, r0):
    t = act.shape[0]
    rb, d = shard.shape
    tm = _tile(t, 1024)
    ni = t // tm

    def body(order_ref, a_ref, shard_ref, land_in_ref, y_ref, land_ref, wbuf, *sems):
        p, i = pl.program_id(0), pl.program_id(1)
        gather = _StreamedGather(shard_ref, land_ref, wbuf, 0, r0, *sems)
        rows = pl.ds(pl.multiple_of(i * tm, tm), tm)
        _stream_steps(gather, p, i, ni)
        part = _dot_nn(a_ref[...], wbuf[p % 2])

        @pl.when(p == 0)
        def _():
            y_ref[rows, :] = part

        @pl.when(p > 0)
        def _():
            y_ref[rows, :] += part

        @pl.when((p == N_DEV - 1) & (i == ni - 1))
        def _():
            gather.finish()

    return pl.pallas_call(
        body, name="down_fwd",
        grid_spec=pltpu.PrefetchScalarGridSpec(
            num_scalar_prefetch=1, grid=(N_DEV, ni),
            in_specs=[pl.BlockSpec((tm, rb), lambda p, i, o: (i, o[p])), _ANY, _ANY],
            out_specs=[pl.BlockSpec(memory_space=pltpu.VMEM), _ANY],
            scratch_shapes=[pltpu.VMEM((2, rb, d), BF16)] + _STREAM_SEMS),
        out_shape=[jax.ShapeDtypeStruct((t, d), F32), jax.ShapeDtypeStruct((N_DEV, rb, d), BF16)],
        input_output_aliases={3: 1},
        compiler_params=_params("arbitrary", "arbitrary"),
    )(order, act, shard, land)


def _loss_call(y, x2, target, g_final):
    t, d = y.shape
    tr = _tile(t, 256)

    def body(y_ref, x_ref, tg_ref, g_ref, loss_ref, dx_ref, dxb_ref, dg_ref):
        @pl.when(pl.program_id(0) == 0)
        def _():
            loss_ref[...] = jnp.zeros_like(loss_ref)
            dg_ref[...] = jnp.zeros_like(dg_ref)

        n, r = _rms_rows(x_ref[...] + y_ref[...])
        err = n * g_ref[...] - tg_ref[...]
        loss_ref[...] += 0.5 * jnp.sum(jnp.mean(err * err, axis=-1, keepdims=True))
        dy = err * (1.0 / d)
        dg_ref[...] += jnp.sum(dy * n, axis=0, keepdims=True)
        dx = _rms_bwd_rows(dy * g_ref[...], n, r)
        dx_ref[...] = dx
        dxb_ref[...] = dx.astype(BF16)

    return pl.pallas_call(
        body, name="loss_head",
        grid=(t // tr,),
        in_specs=[pl.BlockSpec((tr, d), lambda i: (i, 0)),
                  pl.BlockSpec((tr, d), lambda i: (i, 0)),
                  pl.BlockSpec((tr, d), lambda i: (i, 0)),
                  pl.BlockSpec((1, d), lambda i: (0, 0))],
        out_specs=[pl.BlockSpec((8, LANES), lambda i: (0, 0)),
                   pl.BlockSpec((tr, d), lambda i: (i, 0)),
                   pl.BlockSpec((tr, d), lambda i: (i, 0)),
                   pl.BlockSpec((1, d), lambda i: (0, 0))],
        out_shape=[jax.ShapeDtypeStruct((8, LANES), F32), jax.ShapeDtypeStruct((t, d), F32),
                   jax.ShapeDtypeStruct((t, d), BF16), jax.ShapeDtypeStruct((1, d), F32)],
        compiler_params=_params("arbitrary"),
    )(y, x2, target, g_final)


def _norm_bwd_call(name, dh, x, dres, g, want_bf16, comm=None):
    t, d = x.shape
    tr = _tile(t, 256)

    def body(dh_ref, x_ref, dres_ref, g_ref, dx_ref, *rest):
        dg_ref = rest[-1]

        @pl.when(pl.program_id(0) == 0)
        def _():
            dg_ref[...] = jnp.zeros_like(dg_ref)

        n, r = _rms_rows(x_ref[...])
        dh = dh_ref[...]
        dg_ref[...] += jnp.sum(dh * n, axis=0, keepdims=True)
        dx = dres_ref[...] + _rms_bwd_rows(dh * g_ref[...], n, r)
        dx_ref[...] = dx
        if want_bf16:
            rest[0][...] = dx.astype(BF16)

    row = pl.BlockSpec((tr, d), lambda i: (i, 0))
    vec = pl.BlockSpec((1, d), lambda i: (0, 0))
    out_specs = [row] + ([row] if want_bf16 else []) + [vec]
    out_shape = ([jax.ShapeDtypeStruct((t, d), F32)]
                 + ([jax.ShapeDtypeStruct((t, d), BF16)] if want_bf16 else [])
                 + [jax.ShapeDtypeStruct((1, d), F32)])
    return _carrier_call(
        body, (dh, x, dres, g), comm, name=name,
        grid=(t // tr,),
        in_specs=[row, row, row, vec],
        out_specs=out_specs, out_shape=out_shape,
        sem=("arbitrary",))


def _dact_call(dx3b, w_down, act, comm=None):
    t, d = dx3b.shape
    f = w_down.shape[0]
    tm = _tile(t, 1024)
    tn = _tile(f, 1024)

    def body(g_ref, w_ref, act_ref, o_ref):
        dact = _dot_nt(g_ref[...], w_ref[...])
        o_ref[...] = (dact * (2.0 * jnp.sqrt(act_ref[...].astype(F32)))).astype(BF16)

    return _carrier_call(
        body, (dx3b, w_down, act), comm, name="dact_bwd",
        grid=(t // tm, f // tn),
        in_specs=[pl.BlockSpec((tm, d), lambda i, j: (i, 0)),
                  pl.BlockSpec((tn, d), lambda i, j: (j, 0)),
                  pl.BlockSpec((tm, tn), lambda i, j: (i, j))],
        out_specs=pl.BlockSpec((tm, tn), lambda i, j: (i, j)),
        out_shape=jax.ShapeDtypeStruct((t, f), BF16),
        sem=("parallel", "parallel"))


def _wgrad_call(name, a, b, out_blocks, out_block_cols, comm=None, *, t1, t2=None, merge=1):
    t, k1 = a.shape
    k2 = b.shape[1]
    tt = _tile(t, 1024)
    t1 = _tile(k1, t1)
    t2 = _tile(k2, t2) if out_blocks is None else merge * out_block_cols
    nk = t // tt

    def body(a_ref, b_ref, o_ref, acc_ref):
        k = pl.program_id(2)

        @pl.when(k == 0)
        def _():
            acc_ref[...] = jnp.zeros_like(acc_ref)

        acc_ref[...] += _dot_tn(a_ref[...], b_ref[...])

        @pl.when(k == nk - 1)
        def _():
            if out_blocks is None:
                o_ref[...] = acc_ref[...].astype(BF16)
            else:
                for blk in range(merge):
                    o_ref[blk] = acc_ref[:, blk * out_block_cols:(blk + 1) * out_block_cols].astype(BF16)

    if out_blocks is None:
        out_spec = pl.BlockSpec((t1, t2), lambda i, j, k: (i, j))
        out_shape = jax.ShapeDtypeStruct((k1, k2), BF16)
    else:
        out_spec = pl.BlockSpec((merge, t1, out_block_cols), lambda i, j, k: (j, i, 0))
        out_shape = jax.ShapeDtypeStruct((out_blocks, k1, out_block_cols), BF16)
    return _carrier_call(
        body, (a, b), comm, name=name,
        grid=(k1 // t1, k2 // t2, nk),
        in_specs=[pl.BlockSpec((tt, t1), lambda i, j, k: (k, i)),
                  pl.BlockSpec((tt, t2), lambda i, j, k: (k, j))],
        out_specs=out_spec, out_shape=out_shape,
        scratch_shapes=[pltpu.VMEM((t1, t2), F32)],
        sem=("parallel", "parallel", "arbitrary"))


def _dgrad_blocked_call(name, g, w_g, comm=None, *, merge=1):
    t = g.shape[0]
    nb, d, cb = w_g.shape
    tm = _tile(t, 1024)
    tn = _tile(d, 2048)
    tk = merge * cb

    def body(g_ref, w_ref, o_ref):
        @pl.when(pl.program_id(2) == 0)
        def _():
            o_ref[...] = jnp.zeros_like(o_ref)

        w = w_ref[0] if merge == 1 else jnp.concatenate([w_ref[b] for b in range(merge)], axis=1)
        o_ref[...] += _dot_nt(g_ref[...], w)

    return _carrier_call(
        body, (g, w_g), comm, name=name,
        grid=(t // tm, d // tn, nb // merge),
        in_specs=[pl.BlockSpec((tm, tk), lambda i, j, k: (i, k)),
                  pl.BlockSpec((merge, tn, cb), lambda i, j, k: (k, j, 0))],
        out_specs=pl.BlockSpec((tm, tn), lambda i, j, k: (i, j)),
        out_shape=jax.ShapeDtypeStruct((t, d), F32),
        sem=("parallel", "parallel", "arbitrary"))


def _dmixed_call(dx2b, w_out, comm=None):
    t, d = dx2b.shape
    e = w_out.shape[0]
    tm = _tile(t, 1024)
    tn = _tile(e, 1024)

    def body(g_ref, w_ref, o_ref):
        o_ref[...] = _dot_nt(g_ref[...], w_ref[...])

    return _carrier_call(
        body, (dx2b, w_out), comm, name="dmixed_bwd",
        grid=(t // tm, e // tn),
        in_specs=[pl.BlockSpec((tm, d), lambda i, j: (i, 0)),
                  pl.BlockSpec((tn, d), lambda i, j: (j, 0))],
        out_specs=pl.BlockSpec((tm, tn), lambda i, j: (i, j)),
        out_shape=jax.ShapeDtypeStruct((t, e), F32),
        sem=("parallel", "parallel"))


def _mixer_bwd_call(proj, dmixed, w_s, bs_t, g_v, w_pool_g, pool_scale, comm=None):
    t = proj.shape[0]
    tt = _tile(t, 512)
    nchunk = tt // CHUNK
    hb = tt // HALO
    last_halo = t // HALO - 1
    nsteps = t // tt
    rb = GROUP // N_DEV

    def body(pu_ref, pv_ref, z_ref, zp_ref, da_ref, db_ref, dbn_ref, ws_ref, bs_ref, gv_ref, wp_ref, ps_ref,
             dproj_ref, dws_ref, dbs_ref, dgv_ref, dps_ref, dwp_ref):
        i = pl.program_id(0)

        @pl.when(i == 0)
        def _():
            dws_ref[...] = jnp.zeros_like(dws_ref)
            dbs_ref[...] = jnp.zeros_like(dbs_ref)
            dgv_ref[...] = jnp.zeros_like(dgv_ref)
            dps_ref[...] = jnp.zeros_like(dps_ref)
            dwp_ref[...] = jnp.zeros_like(dwp_ref)

        tril = (lax.broadcasted_iota(jnp.int32, (CHUNK, CHUNK), 0)
                >= lax.broadcasted_iota(jnp.int32, (CHUNK, CHUNK), 1))
        for h in range(N_HEADS):
            cols = slice(h * CHUNK, (h + 1) * CHUNK)
            v, dv_dpv = _gelu_and_grad(pv_ref[:, cols])
            vhat, rv = _rms_rows(v)
            gv = gv_ref[:, cols]
            vn = (vhat * gv).astype(BF16)
            u, du_dpu = _gelu_and_grad(pu_ref[:, cols])
            w = jnp.where(tril, ws_ref[h], 0.0).astype(BF16)
            bcol = bs_ref[:, h:h + 1]
            dout = da_ref[:, cols]
            dmix = dout * u
            dmix_b = dmix.astype(BF16)
            dws = jnp.zeros((CHUNK, CHUNK), F32)
            dbs = jnp.zeros((CHUNK, 1), F32)
            dvn_parts = []
            du_parts = []
            for c in range(nchunk):
                rows = slice(c * CHUNK, (c + 1) * CHUNK)
                mixed = _dot_nn(w, vn[rows]) + bcol
                du_parts.append(dout[rows] * mixed)
                dvn_parts.append(_dot_tn(w, dmix_b[rows]))
                dws = dws + _dot_nt(dmix_b[rows], vn[rows])
                dbs = dbs + jnp.sum(dmix[rows], axis=1, keepdims=True)
            dws_ref[h] += jnp.where(tril, dws, 0.0)
            dbs_ref[:, h:h + 1] += dbs
            dvn = jnp.concatenate(dvn_parts, axis=0)
            du = jnp.concatenate(du_parts, axis=0)
            dgv_ref[:, cols] += jnp.sum(dvn * vhat, axis=0, keepdims=True)
            dv = _rms_bwd_rows(dvn * gv, vhat, rv)
            dproj_ref[:, cols] = (du * du_dpu).astype(BF16)
            dproj_ref[:, A_WIDTH + h * CHUNK:A_WIDTH + (h + 1) * CHUNK] = (dv * dv_dpv).astype(BF16)

        zprev = jnp.where(i > 0, zp_ref[...], 0.0)
        ext = jnp.concatenate([zprev, z_ref[...]], axis=0)
        dnext = jnp.where(i < nsteps - 1, dbn_ref[...], 0.0)
        dext = jnp.concatenate([db_ref[...], dnext], axis=0)
        for g, win in enumerate(POOL_WINDOWS):
            cols = slice(g * GROUP, (g + 1) * GROUP)
            zg = ext[:, cols]
            pooled = _window_sum_back(zg, win)[HALO:] / _pool_counts(i * tt, tt, win) - zg[HALO:]
            pooled_b = pooled.astype(BF16)
            wp = wp_ref[:, g].reshape(GROUP, GROUP)
            y = _dot_nn(pooled_b, wp)
            dout = dext[:, cols]
            dps_ref[:, cols] += jnp.sum(dout[:tt] * y, axis=0, keepdims=True)
            dy_b = (dout * ps_ref[:, cols]).astype(BF16)
            dwp_ref[:, g] += _dot_tn(pooled_b, dy_b[:tt]).reshape(N_DEV, rb, GROUP)
            dpooled = _dot_nt(dy_b, wp)
            q = dpooled / _pool_counts(i * tt, tt + HALO, win)
            dz = _window_sum_fwd(q, win)[:tt] - dpooled[:tt]
            dproj_ref[:, 2 * A_WIDTH + g * GROUP:2 * A_WIDTH + (g + 1) * GROUP] = dz.astype(BF16)

    def full(shape):
        return pl.BlockSpec(shape, lambda i: (0,) * len(shape))

    return _carrier_call(
        body, (proj, proj, proj, proj, dmixed, dmixed, dmixed, w_s, bs_t, g_v, w_pool_g, pool_scale), comm,
        name="mixer_bwd",
        grid=(nsteps,),
        in_specs=[pl.BlockSpec((tt, A_WIDTH), lambda i: (i, 0)),
                  pl.BlockSpec((tt, A_WIDTH), lambda i: (i, 1)),
                  pl.BlockSpec((tt, B_WIDTH), lambda i: (i, 2)),
                  pl.BlockSpec((HALO, B_WIDTH), lambda i: (jnp.maximum(i * hb - 1, 0), 2)),
                  pl.BlockSpec((tt, A_WIDTH), lambda i: (i, 0)),
                  pl.BlockSpec((tt, B_WIDTH), lambda i: (i, 1)),
                  pl.BlockSpec((HALO, B_WIDTH), lambda i: (jnp.minimum((i + 1) * hb, last_halo), 1)),
                  full((N_HEADS, CHUNK, CHUNK)), full((CHUNK, N_HEADS)), full((1, A_WIDTH)),
                  full((N_DEV, 4, rb, GROUP)), full((1, B_WIDTH))],
        out_specs=[pl.BlockSpec((tt, 2 * A_WIDTH + B_WIDTH), lambda i: (i, 0)),
                   full((N_HEADS, CHUNK, CHUNK)), full((CHUNK, N_HEADS)), full((1, A_WIDTH)),
                   full((1, B_WIDTH)), full((N_DEV, 4, rb, GROUP))],
        out_shape=[jax.ShapeDtypeStruct((t, 2 * A_WIDTH + B_WIDTH), BF16),
                   jax.ShapeDtypeStruct((N_HEADS, CHUNK, CHUNK), F32),
                   jax.ShapeDtypeStruct((CHUNK, N_HEADS), F32),
                   jax.ShapeDtypeStruct((1, A_WIDTH), F32),
                   jax.ShapeDtypeStruct((1, B_WIDTH), F32),
                   jax.ShapeDtypeStruct((N_DEV, 4, rb, GROUP), F32)],
        sem=("arbitrary",))


def _adamw(w, g, m, v):
    m = ADAM_B1 * m + (1.0 - ADAM_B1) * g
    v = ADAM_B2 * v + (1.0 - ADAM_B2) * (g * g)
    m_hat = m / ADAM_C1
    v_hat = v / ADAM_C2
    delta = -ADAM_LR * (m_hat / (jnp.sqrt(v_hat) + ADAM_EPS) + ADAM_WD * w)
    return delta, m, v


PAIR_SUM_TILE_ELEMS = 1024 * 1024
ADAMW_TILE_ELEMS = 512 * 1024


def _row_tile(r, c, elems):
    t = r
    while t * c > elems and t % 32 == 0:
        t //= 2
    return t


def _pair_sum_call(name, pos, grad, got):
    _, r, c = grad.shape
    tr = _row_tile(r, c, PAIR_SUM_TILE_ELEMS)

    def chip_of(rel, pos_ref):
        px = jnp.where((rel == 0) | (rel == 2), 1 - pos_ref[0], pos_ref[0])
        py = jnp.where((rel == 1) | (rel == 2), 1 - pos_ref[1], pos_ref[1])
        return 2 * px + py

    def body(pos_ref, own_ref, got_ref, out_ref):
        out_ref[...] = (own_ref[...].astype(F32) + got_ref[...].astype(F32)).astype(BF16)

    return pl.pallas_call(
        body, name=name,
        grid_spec=pltpu.PrefetchScalarGridSpec(
            num_scalar_prefetch=1, grid=(3, r // tr),
            in_specs=[pl.BlockSpec((None, tr, c), lambda k, i, p: (2 * chip_of(k, p) + p[2], i, 0)),
                      pl.BlockSpec((None, tr, c), lambda k, i, p: (chip_of(k, p), i, 0))],
            out_specs=pl.BlockSpec((None, tr, c), lambda k, i, p: (k, i, 0))),
        out_shape=jax.ShapeDtypeStruct((3, r, c), BF16),
        compiler_params=_params("parallel", "parallel"),
    )(pos, grad, got)


def _final_call(name, pos, grad, got_pair, got_chips, w, m, v):
    _, r, c = grad.shape
    tr = _row_tile(r, c, ADAMW_TILE_ELEMS)

    def body(pos_ref, own_ref, pair_ref, chips_ref, w_ref, m_ref, v_ref, g_out, d_out, m_out, v_out):
        g = own_ref[...].astype(F32) + pair_ref[...].astype(F32)
        for j in range(3):
            g = g + chips_ref[j].astype(F32)
        delta, m_new, v_new = _adamw(w_ref[...], g, m_ref[...], v_ref[...])
        g_out[...] = g
        d_out[...] = delta
        m_out[...] = m_new
        v_out[...] = v_new

    row = pl.BlockSpec((tr, c), lambda i, p: (i, 0))
    return pl.pallas_call(
        body, name=name,
        grid_spec=pltpu.PrefetchScalarGridSpec(
            num_scalar_prefetch=1, grid=(r // tr,),
            in_specs=[pl.BlockSpec((None, tr, c), lambda i, p: (4 * p[0] + 2 * p[1] + p[2], i, 0)),
                      pl.BlockSpec((None, tr, c), lambda i, p: (2 * p[0] + p[1], i, 0)),
                      pl.BlockSpec((3, tr, c), lambda i, p: (0, i, 0)), row, row, row],
            out_specs=[row] * 4),
        out_shape=[jax.ShapeDtypeStruct((r, c), F32)] * 4,
        compiler_params=_params("parallel"),
    )(pos, grad, got_pair, got_chips, w, m, v)


def _small_final_call(name, parts, w, m, v):
    _, rows, c = parts.shape
    r = w.shape[0]

    def body(p_ref, w_ref, m_ref, v_ref, g_out, d_out, m_out, v_out):
        g = p_ref[0]
        for k in range(1, N_DEV):
            g = g + p_ref[k]
        delta, m_new, v_new = _adamw(w_ref[...], g[:r], m_ref[...], v_ref[...])
        g_out[...] = g
        d_out[...] = delta
        m_out[...] = m_new
        v_out[...] = v_new

    return pl.pallas_call(
        body, name=name,
        out_shape=[jax.ShapeDtypeStruct((rows, c), F32)] + [jax.ShapeDtypeStruct((r, c), F32)] * 3,
        compiler_params=pltpu.CompilerParams(vmem_limit_bytes=VMEM_LIMIT),
    )(parts, w, m, v)


_SMALL_EARLY = ("g_v", "w_s", "b_s", "pool_scale", "g_ffn", "g_final")
_BIG = ("w_in", "w_pool", "w_out", "w_up", "w_down")
_ORDER = ("g_mix", "w_in", "g_v", "w_s", "b_s", "w_pool", "pool_scale", "w_out", "g_ffn", "w_up", "w_down", "g_final")


def _pack(parts):
    return jnp.concatenate([p.reshape(-1, LANES) for p in parts], axis=0)


def _unpack(packed, like):
    out, row = [], 0
    for a in like:
        rows = a.size // LANES
        out.append(packed[row:row + rows].reshape(a.shape))
        row += rows
    return out


def kernel(x, g_mix, w_in, g_v, w_s, b_s, w_pool, pool_scale, w_out, g_ffn, w_up, w_down, g_final, loss_target, m_g_mix, m_w_in, m_g_v, m_w_s, m_b_s, m_w_pool, m_pool_scale, m_w_out, m_g_ffn, m_w_up, m_w_down, m_g_final, v_g_mix, v_w_in, v_g_v, v_w_s, v_b_s, v_w_pool, v_pool_scale, v_w_out, v_g_ffn, v_w_up, v_w_down, v_g_final):
    weights = dict(g_mix=g_mix, w_in=w_in, g_v=g_v, w_s=w_s, b_s=b_s, w_pool=w_pool, pool_scale=pool_scale,
                   w_out=w_out, g_ffn=g_ffn, w_up=w_up, w_down=w_down, g_final=g_final)
    mom = dict(g_mix=m_g_mix, w_in=m_w_in, g_v=m_g_v, w_s=m_w_s, b_s=m_b_s, w_pool=m_w_pool,
               pool_scale=m_pool_scale, w_out=m_w_out, g_ffn=m_g_ffn, w_up=m_w_up, w_down=m_w_down,
               g_final=m_g_final)
    var = dict(g_mix=v_g_mix, w_in=v_w_in, g_v=v_g_v, w_s=v_w_s, b_s=v_b_s, w_pool=v_w_pool,
               pool_scale=v_pool_scale, w_out=v_w_out, g_ffn=v_g_ffn, w_up=v_w_up, w_down=v_w_down,
               g_final=v_g_final)

    t, d = x.shape[1], x.shape[2]
    xs = x.reshape(t, d)
    target = loss_target.reshape(t, d)

    shard2d = dict(w_in=w_in.reshape(d, -1), w_pool=w_pool.reshape(-1, GROUP), w_out=w_out.reshape(-1, d),
                   w_up=w_up.reshape(d, -1), w_down=w_down.reshape(-1, d))
    sb = {k: shard2d[k].astype(BF16) for k in _BIG}
    rows = {k: sb[k].shape[0] for k in _BIG}

    def gathered_shape(k):
        return jax.ShapeDtypeStruct((N_DEV,) + sb[k].shape, BF16)

    def landing(n, like):
        return jax.ShapeDtypeStruct((n,) + like.shape[1:], like.dtype)

    def from_everyone(block):
        return jax.ShapeDtypeStruct((N_DEV,) + block.shape, block.dtype)

    def cuts(r, fractions):
        return [0] + [int(r * f) // 16 * 16 for f in fractions] + [r]

    g_mix2, g_ffn2, g_final2 = g_mix.reshape(1, d), g_ffn.reshape(1, d), g_final.reshape(1, d)
    g_v2, ps2 = g_v.reshape(1, A_WIDTH), pool_scale.reshape(1, B_WIDTH)
    w_s3 = w_s.reshape(N_HEADS, CHUNK, CHUNK)
    bs_t = b_s.reshape(N_HEADS, CHUNK).T
    xi, yi, ci = _position()
    pos = jnp.stack([xi, yi, ci]).astype(jnp.int32)

    order = (4 * xi + 2 * yi + ci) ^ jnp.array(ARRIVAL_ORDER, jnp.int32)
    u = cuts(rows["w_up"], (0.25, 0.55))
    ahead = cuts(rows["w_down"], (0.4,))[1]
    proj, h1, w_in_g, w_out_g, w_pool_g = _norm_matmul_stream_call(
        "proj_fwd", xs, g_mix2, sb["w_in"], order, None, 0, 0, _Comm(
            [sb["w_out"], sb["w_pool"]], [], [gathered_shape("w_out"), gathered_shape("w_pool")],
            lambda s, l: [_gather_first(s[0], l[0], 0, rows["w_out"]) + _everyone(s[1], l[1])]),
        lambda a: a, F32)
    w_pool_g = w_pool_g.reshape(N_DEV, 4, GROUP // N_DEV, GROUP)
    mixed, w_out_g, w_up_g = _mixer_fwd_call(proj, w_s3, bs_t, g_v2, w_pool_g, ps2, _Comm(
        [sb["w_up"]], [w_out_g], [gathered_shape("w_up")],
        lambda s, l: [_gather_pass_on(l[0], 0, rows["w_out"]) + _gather_first(s[0], l[1], u[0], u[1])]))
    w_out_f = w_out_g.reshape(-1, d)
    x2, w_up_g = _out_proj_call(mixed, w_out_f, xs, _Comm(
        [sb["w_up"]], [w_up_g], [],
        lambda s, l: [_gather_pass_on(l[0], u[0], u[1]) + _gather_first(s[0], l[0], u[1], u[2])]))
    act, h2, w_up_g, w_down_g = _norm_matmul_stream_call(
        "up_fwd", x2, g_ffn2, sb["w_up"], order, w_up_g, u[1], u[2], _Comm(
            [sb["w_down"]], [], [gathered_shape("w_down")],
            lambda s, l: [_gather_first(s[0], l[0], 0, ahead)]),
        lambda a: jnp.square(jnp.maximum(a, 0.0)), BF16)
    y, w_down_g = _down_call(act, sb["w_down"], order, w_down_g, ahead)
    w_down_f = w_down_g.reshape(-1, d)
    loss_part, dx3, dx3b, dg_final = _loss_call(y, x2, target, g_final2)

    def pair_sum(k, grad, got):
        return _pair_sum_call(k + "_pair_sum", pos, grad, got)

    def finish(k, grad, got_pair, got_chips):
        s = shard2d[k]
        outs = _final_call(k + "_adamw", pos, grad, got_pair, got_chips, s, mom[k].reshape(s.shape),
                           var[k].reshape(s.shape))
        return [o.reshape(weights[k].shape) for o in outs]

    result = {}
    (gw_down,) = _wgrad_call("w_down_grad", act, dx3b, None, None, t1=1024, t2=2048)
    gw_down = gw_down.reshape(N_DEV, -1, d)
    da, pair_down = _dact_call(dx3b, w_down_f, act, _Comm(
        [gw_down], [], [landing(4, gw_down)], lambda s, l: [_pair_exchange(s[0], l[0])]))
    sums_down = pair_sum("w_down", gw_down, pair_down)
    dn = cuts(rows["w_down"], (0.8,))
    gw_up, got = _wgrad_call("w_up_grad", h2, da, N_DEV, w_up_g.shape[2], _Comm(
        [sums_down], [], [landing(3, sums_down)],
        lambda s, l: [_chip_exchange(s[0], l[0], dn[0], dn[1])]), t1=2048)
    dh2, got, pair_up = _dgrad_blocked_call("dh2_bwd", da, w_up_g, _Comm(
        [sums_down, gw_up], [got], [landing(4, gw_up)],
        lambda s, l: [_chip_exchange(s[0], l[0], dn[1], dn[2]) + _pair_exchange(s[1], l[1])]))
    result["w_down"] = finish("w_down", gw_down, pair_down, got)
    sums_up = pair_sum("w_up", gw_up, pair_up)
    v = cuts(rows["w_up"], (0.28, 0.52, 0.84))
    dx2, dx2b, dg_ffn, got_up = _norm_bwd_call("ffn_norm_bwd", dh2, x2, dx3, g_ffn2, True, _Comm(
        [sums_up], [], [landing(3, sums_up)], lambda s, l: [_chip_exchange(s[0], l[0], v[0], v[1])]))
    dmixed, got_up = _dmixed_call(dx2b, w_out_f, _Comm(
        [sums_up], [got_up], [], lambda s, l: [_chip_exchange(s[0], l[0], v[1], v[2])]))
    gw_out, got_up = _wgrad_call("w_out_grad", mixed, dx2b, None, None, _Comm(
        [sums_up], [got_up], [], lambda s, l: [_chip_exchange(s[0], l[0], v[2], v[3])]), t1=2048, t2=1024)
    gw_out = gw_out.reshape(N_DEV, -1, d)
    dproj, dw_s, dbs_t, dg_v, dps, dw_pool, got_up, pair_out = _mixer_bwd_call(
        proj, dmixed, w_s3, bs_t, g_v2, w_pool_g, ps2, _Comm(
            [sums_up, gw_out], [got_up], [landing(4, gw_out)],
            lambda s, l: [_chip_exchange(s[0], l[0], v[3], v[4]) + _pair_exchange(s[1], l[1])]))
    result["w_up"] = finish("w_up", gw_up, pair_up, got_up)
    sums_out = pair_sum("w_out", gw_out, pair_out)
    gw_pool = dw_pool.astype(BF16).reshape(N_DEV, -1, GROUP)
    early = dict(g_v=dg_v, w_s=dw_s, b_s=dbs_t.T, pool_scale=dps, g_ffn=dg_ffn, g_final=dg_final)
    packed = _pack([early[k] for k in _SMALL_EARLY] + [loss_part])
    early_rows = packed.shape[0]
    gw_in, got, pair_pool, parts_early = _wgrad_call("w_in_grad", h1, dproj, N_DEV, w_in_g.shape[2], _Comm(
        [sums_out, gw_pool, packed], [], [landing(3, sums_out), landing(4, gw_pool), from_everyone(packed)],
        lambda s, l: [_chip_exchange(s[0], l[0], 0, rows["w_out"]) + _pair_exchange(s[1], l[1])
                      + _gather_first(s[2], l[2], 0, early_rows)]), t1=2048, merge=MERGE_W_IN)
    result["w_out"] = finish("w_out", gw_out, pair_out, got)
    sums_pool = pair_sum("w_pool", gw_pool, pair_pool)
    (pair_in,) = _comm_call("pair_exchange_w_in", _Comm(
        [gw_in], [], [landing(4, gw_in)], lambda s, l: [_pair_exchange(s[0], l[0])]))
    sums_in = pair_sum("w_in", gw_in, pair_in)
    dh1, parts_early, got, got_pool = _dgrad_blocked_call("dh1_bwd", dproj, w_in_g, _Comm(
        [sums_in, sums_pool], [parts_early], [landing(3, sums_in), landing(3, sums_pool)],
        lambda s, l: [_chip_exchange(s[0], l[1], 0, rows["w_in"]) + _chip_exchange(s[1], l[2], 0, rows["w_pool"])
                      + _gather_pass_on(l[0], 0, early_rows)]),
        merge=MERGE_W_IN)
    result["w_in"] = finish("w_in", gw_in, pair_in, got)
    result["w_pool"] = finish("w_pool", gw_pool, pair_pool, got_pool)
    grad_x, dg_mix = _norm_bwd_call("mix_norm_bwd", dh1, xs, dx2, g_mix2, False)
    packed = _pack([dg_mix])
    (parts_late,) = _comm_call("gather_g_mix_grad", _Comm(
        [packed], [], [from_everyone(packed)], lambda s, l: [_everyone(s[0], l[0])]))

    for names, parts, tag in ((_SMALL_EARLY, parts_early, "small_adamw"), (("g_mix",), parts_late, "g_mix_adamw")):
        outs = _small_final_call(tag, parts, _pack([weights[k] for k in names]), _pack([mom[k] for k in names]),
                                 _pack([var[k] for k in names]))
        if tag == "small_adamw":
            loss = outs[0][-1, 0]
        like = [weights[k] for k in names]
        unpacked = [_unpack(o, like) for o in outs]
        for idx, k in enumerate(names):
            result[k] = [unpacked[q][idx] for q in range(4)]

    grads = [result[k][0] for k in _ORDER]
    deltas = [result[k][1] for k in _ORDER]
    new_m = [result[k][2] for k in _ORDER]
    new_v = [result[k][3] for k in _ORDER]
    return (loss, grad_x.reshape(x.shape), *grads, *deltas, *new_m, *new_v)
```

```python
import functools
import math

import jax
import jax.numpy as jnp
from jax import lax
from jax.experimental import pallas as pl
from jax.experimental.pallas import tpu as pltpu

F32 = jnp.float32
BF16 = jnp.bfloat16
MESH = pl.DeviceIdType.MESH

N_DEV = 8
EPS = 1e-6
CHUNK = 128
N_HEADS = 8
A_WIDTH = 1024
B_WIDTH = 1024
POOL_WINDOWS = (2, 4, 8, 16)
GROUP = 256
HALO = 16
LANES = 128

ADAM_LR = 0.001
ADAM_B1 = 0.9
ADAM_B2 = 0.999
ADAM_EPS = 1e-08
ADAM_WD = 0.01
ADAM_STEP = 10
ADAM_C1 = 1.0 - ADAM_B1 ** ADAM_STEP
ADAM_C2 = 1.0 - ADAM_B2 ** ADAM_STEP

VMEM_LIMIT = 56 * 1024 * 1024
MERGE_W_IN = 2

_GELU_C = math.sqrt(2.0 / math.pi)


def _params(*sem):
    return pltpu.CompilerParams(dimension_semantics=sem, vmem_limit_bytes=VMEM_LIMIT)


def _gelu(x):
    return 0.5 * x * (1.0 + jnp.tanh(_GELU_C * (x + 0.044715 * x * x * x)))


def _gelu_and_grad(x):
    t = jnp.tanh(_GELU_C * (x + 0.044715 * x * x * x))
    g = 0.5 * x * (1.0 + t)
    dg = 0.5 * (1.0 + t) + 0.5 * x * (1.0 - t * t) * (_GELU_C * (1.0 + 3.0 * 0.044715 * x * x))
    return g, dg


def _dot_nn(a, b):
    return lax.dot_general(a, b, (((1,), (0,)), ((), ())), preferred_element_type=F32)


def _dot_nt(a, b):
    return lax.dot_general(a, b, (((1,), (1,)), ((), ())), preferred_element_type=F32)


def _dot_tn(a, b):
    return lax.dot_general(a, b, (((0,), (0,)), ((), ())), preferred_element_type=F32)


def _rms_rows(x):
    r = lax.rsqrt(jnp.mean(x * x, axis=-1, keepdims=True) + EPS)
    return x * r, r


def _rms_bwd_rows(dn, n, r):
    return r * (dn - n * jnp.mean(dn * n, axis=-1, keepdims=True))


def _tile(n, want):
    t = min(n, want)
    assert n % t == 0, (n, want)
    return t


_ANY = pl.BlockSpec(memory_space=pl.ANY)

SIBLING = 1
CHIPS = (4, 2, 6)


def _position():
    return lax.axis_index("x"), lax.axis_index("y"), lax.axis_index("c")


def _me():
    x, y, c = _position()
    return 4 * x + 2 * y + c


def _peer(rel):
    x, y, c = _position()
    return (x ^ ((rel >> 2) & 1), y ^ ((rel >> 1) & 1), c ^ (rel & 1))


class _Comm:
    def __init__(self, srcs, lands, new, plan):
        self.srcs, self.lands, self.new, self.plan = list(srcs), list(lands), list(new), plan


def _make_copies(phases, send_sems, recv_sems, local_sems):
    out, nr, nl = [], 0, 0
    for phase in phases:
        cps = []
        for item in phase:
            if item[0] == "local":
                cps.append(pltpu.make_async_copy(item[1], item[2], local_sems.at[nl]))
                nl += 1
            else:
                cps.append(pltpu.make_async_remote_copy(
                    src_ref=item[1], dst_ref=item[2], send_sem=send_sems.at[nr], recv_sem=recv_sems.at[nr],
                    device_id=_peer(item[3]), device_id_type=MESH))
                nr += 1
        out.append(cps)
    return out


def _count_copies(comm):
    phases = comm.plan([_FakeRef() for _ in comm.srcs], [_FakeRef() for _ in range(len(comm.lands) + len(comm.new))])
    items = [it for ph in phases for it in ph]
    return sum(it[0] == "remote" for it in items), sum(it[0] == "local" for it in items)


class _FakeRef:
    def __getitem__(self, idx):
        return self

    @property
    def at(self):
        return self


def _carrier_call(body, args, comm, *, name, grid, in_specs, out_specs, out_shape, scratch_shapes=(), sem):
    if not isinstance(out_shape, (list, tuple)):
        out_specs, out_shape = [out_specs], [out_shape]
    out_specs, out_shape, scratch_shapes = list(out_specs), list(out_shape), list(scratch_shapes)
    if comm is None:
        res = pl.pallas_call(body, name=name, grid=grid, in_specs=list(in_specs), out_specs=out_specs,
                             out_shape=out_shape, scratch_shapes=scratch_shapes, compiler_params=_params(*sem))(*args)
        return list(res)
    n_in, n_out, n_scr = len(args), len(out_shape), len(scratch_shapes)
    ns, nl, nn = len(comm.srcs), len(comm.lands), len(comm.new)
    n_remote, n_local = _count_copies(comm)

    def wrapped(*refs):
        ins, srcs = refs[:n_in], refs[n_in:n_in + ns]
        o = n_in + ns + nl
        outs, lands = refs[o:o + n_out], refs[o + n_out:o + n_out + nl + nn]
        scr = refs[o + n_out + nl + nn:]
        (copies,) = _make_copies(comm.plan(srcs, lands), *scr[n_scr:])
        ids = [pl.program_id(a) for a in range(len(grid))]
        first = functools.reduce(jnp.logical_and, [i == 0 for i in ids])
        last = functools.reduce(jnp.logical_and, [i == g - 1 for i, g in zip(ids, grid)])

        @pl.when(first)
        def _():
            for cp in copies:
                cp.start()

        body(*ins, *outs, *scr[:n_scr])

        @pl.when(last)
        def _():
            for cp in copies:
                cp.wait()

    land_shapes = [jax.ShapeDtypeStruct(a.shape, a.dtype) for a in comm.lands] + comm.new
    sems = [pltpu.SemaphoreType.DMA((max(n_remote, 1),)), pltpu.SemaphoreType.DMA((max(n_remote, 1),)),
            pltpu.SemaphoreType.DMA((max(n_local, 1),))]
    res = pl.pallas_call(
        wrapped, name=name, grid=grid,
        in_specs=list(in_specs) + [_ANY] * (ns + nl), out_specs=out_specs + [_ANY] * (nl + nn),
        out_shape=out_shape + land_shapes, scratch_shapes=scratch_shapes + sems,
        input_output_aliases={n_in + ns + k: n_out + k for k in range(nl)},
        compiler_params=_params(*sem))(*args, *comm.srcs, *comm.lands)
    return list(res)


def _comm_call(name, comm):
    ns, nl, nn = len(comm.srcs), len(comm.lands), len(comm.new)
    n_remote, n_local = _count_copies(comm)

    def body(*refs):
        srcs, lands, sems = refs[:ns], refs[ns + nl:ns + nl + nl + nn], refs[ns + nl + nl + nn:]
        for copies in _make_copies(comm.plan(srcs, lands), *sems):
            for cp in copies:
                cp.start()
            for cp in copies:
                cp.wait()

    land_shapes = [jax.ShapeDtypeStruct(a.shape, a.dtype) for a in comm.lands] + comm.new
    res = pl.pallas_call(
        body, name=name,
        in_specs=[_ANY] * (ns + nl), out_specs=[_ANY] * (nl + nn), out_shape=land_shapes,
        scratch_shapes=[pltpu.SemaphoreType.DMA((max(n_remote, 1),)), pltpu.SemaphoreType.DMA((max(n_remote, 1),)),
                        pltpu.SemaphoreType.DMA((max(n_local, 1),))],
        input_output_aliases={ns + k: k for k in range(nl)},
    )(*comm.srcs, *comm.lands)
    return list(res)


def _rows(ref, block, r0, r1):
    return ref.at[block, pl.ds(r0, r1 - r0)]


def _gather_first(shard, land, r0, r1):
    src = shard.at[pl.ds(r0, r1 - r0)]
    dst = _rows(land, _me(), r0, r1)
    return [("local", src, dst)] + [("remote", src, dst, rel) for rel in (SIBLING,) + CHIPS]


def _gather_pass_on(land, r0, r1):
    return [("remote", _rows(land, _me() ^ rel, r0, r1), _rows(land, _me() ^ rel, r0, r1), SIBLING) for rel in CHIPS]


def _pair_exchange(grad, land):
    _, _, c = _position()
    return [("remote", grad.at[2 * chip + (1 - c)], land.at[chip], SIBLING) for chip in range(4)]


def _chip_exchange(sums, land, r0, r1):
    return [("remote", _rows(sums, j, r0, r1), _rows(land, j, r0, r1), rel) for j, rel in enumerate(CHIPS)]


def _everyone(packed, land):
    dst = land.at[_me()]
    return [("local", packed, dst)] + [("remote", packed, dst, rel) for rel in range(1, N_DEV)]


def _pool_counts(row0, rows, win):
    pos = row0 + lax.broadcasted_iota(jnp.int32, (rows, 1), 0)
    return jnp.minimum(pos + 1, win).astype(F32)


def _window_sum_back(ext, win):
    s = ext
    k = 1
    while k < win:
        s = s + pltpu.roll(s, k, 0)
        k *= 2
    return s


def _window_sum_fwd(ext, win):
    n = ext.shape[0]
    s = ext
    k = 1
    while k < win:
        s = s + pltpu.roll(s, n - k, 0)
        k *= 2
    return s


def _mixer_fwd_call(proj, w_s, bs_t, g_v, w_pool_g, pool_scale, comm=None):
    t = proj.shape[0]
    tt = _tile(t, 512)
    nchunk = tt // CHUNK
    hb = tt // HALO

    def body(pu_ref, pv_ref, z_ref, zp_ref, ws_ref, bs_ref, gv_ref, wp_ref, ps_ref, out_ref):
        i = pl.program_id(0)
        tril = (lax.broadcasted_iota(jnp.int32, (CHUNK, CHUNK), 0)
                >= lax.broadcasted_iota(jnp.int32, (CHUNK, CHUNK), 1))
        for h in range(N_HEADS):
            cols = slice(h * CHUNK, (h + 1) * CHUNK)
            vhat, _ = _rms_rows(_gelu(pv_ref[:, cols]))
            vn = (vhat * gv_ref[:, cols]).astype(BF16)
            u = _gelu(pu_ref[:, cols])
            w = jnp.where(tril, ws_ref[h], 0.0).astype(BF16)
            bcol = bs_ref[:, h:h + 1]
            for c in range(nchunk):
                rows = slice(c * CHUNK, (c + 1) * CHUNK)
                mixed = _dot_nn(w, vn[rows]) + bcol
                out_ref[rows, cols] = (u[rows] * mixed).astype(BF16)

        zprev = jnp.where(i > 0, zp_ref[...], 0.0)
        ext = jnp.concatenate([zprev, z_ref[...]], axis=0)
        for g, win in enumerate(POOL_WINDOWS):
            cols = slice(g * GROUP, (g + 1) * GROUP)
            zg = ext[:, cols]
            s = _window_sum_back(zg, win)
            pooled = s[HALO:] / _pool_counts(i * tt, tt, win) - zg[HALO:]
            wp = wp_ref[:, g].reshape(GROUP, GROUP)
            y = _dot_nn(pooled.astype(BF16), wp)
            out_ref[:, A_WIDTH + g * GROUP:A_WIDTH + (g + 1) * GROUP] = (y * ps_ref[:, cols]).astype(BF16)

    return _carrier_call(
        body, (proj, proj, proj, proj, w_s, bs_t, g_v, w_pool_g, pool_scale), comm, name="mixer_fwd",
        grid=(t // tt,),
        in_specs=[pl.BlockSpec((tt, A_WIDTH), lambda i: (i, 0)),
                  pl.BlockSpec((tt, A_WIDTH), lambda i: (i, 1)),
                  pl.BlockSpec((tt, B_WIDTH), lambda i: (i, 2)),
                  pl.BlockSpec((HALO, B_WIDTH), lambda i: (jnp.maximum(i * hb - 1, 0), 2)),
                  pl.BlockSpec((N_HEADS, CHUNK, CHUNK), lambda i: (0, 0, 0)),
                  pl.BlockSpec((CHUNK, N_HEADS), lambda i: (0, 0)),
                  pl.BlockSpec((1, A_WIDTH), lambda i: (0, 0)),
                  pl.BlockSpec((N_DEV, 4, GROUP // N_DEV, GROUP), lambda i: (0, 0, 0, 0)),
                  pl.BlockSpec((1, B_WIDTH), lambda i: (0, 0))],
        out_specs=pl.BlockSpec((tt, A_WIDTH + B_WIDTH), lambda i: (i, 0)),
        out_shape=jax.ShapeDtypeStruct((t, A_WIDTH + B_WIDTH), BF16),
        sem=("parallel",))


def _out_proj_call(mixed, w_out, x, comm=None):
    t, d = x.shape
    k = mixed.shape[1]
    tm = _tile(t, 1024)
    tn = _tile(d, 1024)

    def body(a_ref, w_ref, x_ref, o_ref):
        o_ref[...] = x_ref[...] + _dot_nn(a_ref[...], w_ref[...])

    return _carrier_call(
        body, (mixed, w_out, x), comm, name="out_proj_fwd",
        grid=(t // tm, d // tn),
        in_specs=[pl.BlockSpec((tm, k), lambda i, j: (i, 0)),
                  pl.BlockSpec((k, tn), lambda i, j: (0, j)),
                  pl.BlockSpec((tm, tn), lambda i, j: (i, j))],
        out_specs=pl.BlockSpec((tm, tn), lambda i, j: (i, j)),
        out_shape=jax.ShapeDtypeStruct((t, d), F32),
        sem=("parallel", "parallel"))


ARRIVAL_ORDER = (0, 1, 4, 5, 2, 3, 6, 7)


class _StreamedGather:
    def __init__(self, shard_ref, land_ref, wbuf, pre0, r0, send_sems, recv_sems, local_sem, fetch_sems):
        self.shard, self.land, self.wbuf, self.fetch_sems = shard_ref, land_ref, wbuf, fetch_sems
        n = shard_ref.shape[0] - r0
        me = _me()
        self.me = me

        def remote(k, src, dst, rel):
            return pltpu.make_async_remote_copy(src_ref=src, dst_ref=dst, send_sem=send_sems.at[k],
                                                recv_sem=recv_sems.at[k], device_id=_peer(rel), device_id_type=MESH)

        src = shard_ref.at[pl.ds(r0, n)]
        dst = land_ref.at[me, pl.ds(r0, n)]
        self.mine = pltpu.make_async_copy(src, dst, local_sem)
        self.first = [remote(k, src, dst, rel) for k, rel in enumerate((SIBLING,) + CHIPS)]
        self.passed, self.early = [], []
        for j, rel in enumerate(CHIPS):
            rows = land_ref.at[me ^ rel, pl.ds(r0, n)]
            self.passed.append(remote(4 + j, rows, rows, SIBLING))
            if r0 > pre0:
                rows = land_ref.at[me ^ rel, pl.ds(pre0, r0 - pre0)]
                self.early.append(remote(7 + j, rows, rows, SIBLING))

    def _fetch(self, q):
        src = self.shard if q == 0 else self.land.at[self.me ^ ARRIVAL_ORDER[q]]
        return pltpu.make_async_copy(src, self.wbuf.at[q % 2], self.fetch_sems.at[q % 2])

    def start(self):
        self.mine.start()
        for cp in self.early + self.first:
            cp.start()
        self._fetch(0).start()

    def arrive(self, q):
        if q == 1:
            self.first[0].wait_recv()
        elif q % 2 == 0:
            j = q // 2 - 1
            self.first[1 + j].wait_recv()
            self.passed[j].start()
        else:
            if q == 3:
                for cp in self.early:
                    cp.wait_recv()
            self.passed[q // 2 - 1].wait_recv()
        self._fetch(q).start()

    def wait_fetch(self, slot):
        pltpu.make_async_copy(self.shard, self.wbuf.at[slot], self.fetch_sems.at[slot]).wait()

    def finish(self):
        for cp in self.early + self.first + self.passed:
            cp.wait_send()
        self.mine.wait()


_STREAM_SEMS = [pltpu.SemaphoreType.DMA((10,)), pltpu.SemaphoreType.DMA((10,)), pltpu.SemaphoreType.DMA,
                pltpu.SemaphoreType.DMA((2,))]


def _stream_steps(gather, p, i, ni):
    @pl.when((p == 0) & (i == 0))
    def _():
        gather.start()

    @pl.when(i == 0)
    def _():
        gather.wait_fetch(p % 2)

    @pl.when(i == ni - 1)
    def _():
        for q in range(1, N_DEV):
            @pl.when(p == q - 1)
            def _():
                gather.arrive(q)


def _norm_matmul_stream_call(name, x, g, shard, order, land, pre0, r0, comm, epilogue, out_dtypes):
    t, d = x.shape
    cb = shard.shape[1]
    tm = _tile(t, 1024)
    ni = t // tm
    n_sems = len(_STREAM_SEMS)
    assert not comm.lands
    ns, nn, no = len(comm.srcs), len(comm.new), len(out_dtypes)
    n_remote, n_local = _count_copies(comm)
    has_land = land is not None

    def body(order_ref, x_ref, g_ref, shard_ref, *refs):
        refs = refs[has_land:]
        srcs, out_refs, (h_ref, land_ref) = refs[:ns], refs[ns:ns + no], refs[ns + no:ns + no + 2]
        new = refs[ns + no + 2:ns + no + 2 + nn]
        wbuf, sems = refs[ns + no + 2 + nn], refs[ns + no + 3 + nn:]
        p, i = pl.program_id(0), pl.program_id(1)
        gather = _StreamedGather(shard_ref, land_ref, wbuf, pre0, r0, *sems[:n_sems])
        (carried,) = _make_copies(comm.plan(srcs, new), *sems[n_sems:])
        rows = pl.ds(pl.multiple_of(i * tm, tm), tm)
        _stream_steps(gather, p, i, ni)

        @pl.when((p == 0) & (i == 0))
        def _():
            for cp in carried:
                cp.start()

        @pl.when(p == 0)
        def _():
            n, _ = _rms_rows(x_ref[...])
            h_ref[rows, :] = (n * g_ref[...]).astype(BF16)

        tails = epilogue(_dot_nn(h_ref[rows, :], wbuf[p % 2]))
        for out_ref, tail, dt in zip(out_refs, tails, out_dtypes):
            out_ref[...] = tail.astype(dt)

        @pl.when((p == N_DEV - 1) & (i == ni - 1))
        def _():
            gather.finish()
            for cp in carried:
                cp.wait()

    carried_sems = [pltpu.SemaphoreType.DMA((max(n_remote, 1),)), pltpu.SemaphoreType.DMA((max(n_remote, 1),)),
                    pltpu.SemaphoreType.DMA((max(n_local, 1),))]
    return pl.pallas_call(
        body, name=name,
        grid_spec=pltpu.PrefetchScalarGridSpec(
            num_scalar_prefetch=1, grid=(N_DEV, ni),
            in_specs=[pl.BlockSpec((tm, d), lambda p, i, o: (jnp.where(p == 0, i, ni - 1), 0)),
                      pl.BlockSpec((1, d), lambda p, i, o: (0, 0)),
                      _ANY] + [_ANY] * (has_land + ns),
            out_specs=[pl.BlockSpec((tm, cb), lambda p, i, o: (i, o[p]))] * no
                      + [pl.BlockSpec(memory_space=pltpu.VMEM), _ANY] + [_ANY] * nn,
            scratch_shapes=[pltpu.VMEM((2, d, cb), BF16)] + _STREAM_SEMS + carried_sems),
        out_shape=[jax.ShapeDtypeStruct((t, N_DEV * cb), dt) for dt in out_dtypes]
                  + [jax.ShapeDtypeStruct((t, d), BF16), jax.ShapeDtypeStruct((N_DEV, d, cb), BF16)] + comm.new,
        input_output_aliases={4: no + 1} if has_land else {},
        compiler_params=_params("arbitrary", "arbitrary"),
    )(order, x, g, shard, *([land] if has_land else []), *comm.srcs)


def _down_call(act, shard, order, land, r0):
    t = act.shape[0]
    rb, d = shard.shape
    tm = _tile(t, 1024)
    ni = t // tm

    def body(order_ref, a_ref, shard_ref, land_in_ref, y_ref, land_ref, wbuf, *sems):
        p, i = pl.program_id(0), pl.program_id(1)
        gather = _StreamedGather(shard_ref, land_ref, wbuf, 0, r0, *sems)
        rows = pl.ds(pl.multiple_of(i * tm, tm), tm)
        _stream_steps(gather, p, i, ni)
        part = _dot_nn(a_ref[...], wbuf[p % 2])

        @pl.when(p == 0)
        def _():
            y_ref[rows, :] = part

        @pl.when(p > 0)
        def _():
            y_ref[rows, :] += part

        @pl.when((p == N_DEV - 1) & (i == ni - 1))
        def _():
            gather.finish()

    return pl.pallas_call(
        body, name="down_fwd",
        grid_spec=pltpu.PrefetchScalarGridSpec(
            num_scalar_prefetch=1, grid=(N_DEV, ni),
            in_specs=[pl.BlockSpec((tm, rb), lambda p, i, o: (i, o[p])), _ANY, _ANY],
            out_specs=[pl.BlockSpec(memory_space=pltpu.VMEM), _ANY],
            scratch_shapes=[pltpu.VMEM((2, rb, d), BF16)] + _STREAM_SEMS),
        out_shape=[jax.ShapeDtypeStruct((t, d), F32), jax.ShapeDtypeStruct((N_DEV, rb, d), BF16)],
        input_output_aliases={3: 1},
        compiler_params=_params("arbitrary", "arbitrary"),
    )(order, act, shard, land)


def _loss_call(y, x2, target, g_final):
    t, d = y.shape
    tr = _tile(t, 256)

    def body(y_ref, x_ref, tg_ref, g_ref, loss_ref, dx_ref, dxb_ref, dg_ref):
        @pl.when(pl.program_id(0) == 0)
        def _():
            loss_ref[...] = jnp.zeros_like(loss_ref)
            dg_ref[...] = jnp.zeros_like(dg_ref)

        n, r = _rms_rows(x_ref[...] + y_ref[...])
        err = n * g_ref[...] - tg_ref[...]
        loss_ref[...] += 0.5 * jnp.sum(jnp.mean(err * err, axis=-1, keepdims=True))
        dy = err * (1.0 / d)
        dg_ref[...] += jnp.sum(dy * n, axis=0, keepdims=True)
        dx = _rms_bwd_rows(dy * g_ref[...], n, r)
        dx_ref[...] = dx
        dxb_ref[...] = dx.astype(BF16)

    return pl.pallas_call(
        body, name="loss_head",
        grid=(t // tr,),
        in_specs=[pl.BlockSpec((tr, d), lambda i: (i, 0)),
                  pl.BlockSpec((tr, d), lambda i: (i, 0)),
                  pl.BlockSpec((tr, d), lambda i: (i, 0)),
                  pl.BlockSpec((1, d), lambda i: (0, 0))],
        out_specs=[pl.BlockSpec((8, LANES), lambda i: (0, 0)),
                   pl.BlockSpec((tr, d), lambda i: (i, 0)),
                   pl.BlockSpec((tr, d), lambda i: (i, 0)),
                   pl.BlockSpec((1, d), lambda i: (0, 0))],
        out_shape=[jax.ShapeDtypeStruct((8, LANES), F32), jax.ShapeDtypeStruct((t, d), F32),
                   jax.ShapeDtypeStruct((t, d), BF16), jax.ShapeDtypeStruct((1, d), F32)],
        compiler_params=_params("arbitrary"),
    )(y, x2, target, g_final)


def _norm_bwd_call(name, dh, x, dres, g, want_bf16, comm=None):
    t, d = x.shape
    tr = _tile(t, 256)

    def body(dh_ref, x_ref, dres_ref, g_ref, dx_ref, *rest):
        dg_ref = rest[-1]

        @pl.when(pl.program_id(0) == 0)
        def _():
            dg_ref[...] = jnp.zeros_like(dg_ref)

        n, r = _rms_rows(x_ref[...])
        dh = dh_ref[...]
        dg_ref[...] += jnp.sum(dh * n, axis=0, keepdims=True)
        dx = dres_ref[...] + _rms_bwd_rows(dh * g_ref[...], n, r)
        dx_ref[...] = dx
        if want_bf16:
            rest[0][...] = dx.astype(BF16)

    row = pl.BlockSpec((tr, d), lambda i: (i, 0))
    vec = pl.BlockSpec((1, d), lambda i: (0, 0))
    out_specs = [row] + ([row] if want_bf16 else []) + [vec]
    out_shape = ([jax.ShapeDtypeStruct((t, d), F32)]
                 + ([jax.ShapeDtypeStruct((t, d), BF16)] if want_bf16 else [])
                 + [jax.ShapeDtypeStruct((1, d), F32)])
    return _carrier_call(
        body, (dh, x, dres, g), comm, name=name,
        grid=(t // tr,),
        in_specs=[row, row, row, vec],
        out_specs=out_specs, out_shape=out_shape,
        sem=("arbitrary",))


def _dact_call(dx3b, w_down, act, comm=None):
    t, d = dx3b.shape
    f = w_down.shape[0]
    tm = _tile(t, 1024)
    tn = _tile(f, 2048)

    def body(g_ref, w_ref, act_ref, o_ref):
        dact = _dot_nt(g_ref[...], w_ref[...])
        o_ref[...] = (dact * act_ref[...].astype(F32)).astype(BF16)

    return _carrier_call(
        body, (dx3b, w_down, act), comm, name="dact_bwd",
        grid=(t // tm, f // tn),
        in_specs=[pl.BlockSpec((tm, d), lambda i, j: (i, 0)),
                  pl.BlockSpec((tn, d), lambda i, j: (j, 0)),
                  pl.BlockSpec((tm, tn), lambda i, j: (i, j))],
        out_specs=pl.BlockSpec((tm, tn), lambda i, j: (i, j)),
        out_shape=jax.ShapeDtypeStruct((t, f), BF16),
        sem=("parallel", "parallel"))


def _wgrad_call(name, a, b, out_blocks, out_block_cols, comm=None, *, t1, t2=None, merge=1):
    t, k1 = a.shape
    k2 = b.shape[1]
    tt = _tile(t, 2048)
    t1 = _tile(k1, t1)
    t2 = _tile(k2, t2) if out_blocks is None else merge * out_block_cols
    nk = t // tt

    def body(a_ref, b_ref, o_ref, acc_ref):
        k = pl.program_id(2)

        @pl.when(k == 0)
        def _():
            acc_ref[...] = jnp.zeros_like(acc_ref)

        acc_ref[...] += _dot_tn(a_ref[...], b_ref[...])

        @pl.when(k == nk - 1)
        def _():
            if out_blocks is None:
                o_ref[...] = acc_ref[...].astype(BF16)
            else:
                for blk in range(merge):
                    o_ref[blk] = acc_ref[:, blk * out_block_cols:(blk + 1) * out_block_cols].astype(BF16)

    if out_blocks is None:
        out_spec = pl.BlockSpec((t1, t2), lambda i, j, k: (i, j))
        out_shape = jax.ShapeDtypeStruct((k1, k2), BF16)
    else:
        out_spec = pl.BlockSpec((merge, t1, out_block_cols), lambda i, j, k: (j, i, 0))
        out_shape = jax.ShapeDtypeStruct((out_blocks, k1, out_block_cols), BF16)
    return _carrier_call(
        body, (a, b), comm, name=name,
        grid=(k1 // t1, k2 // t2, nk),
        in_specs=[pl.BlockSpec((tt, t1), lambda i, j, k: (k, i)),
                  pl.BlockSpec((tt, t2), lambda i, j, k: (k, j))],
        out_specs=out_spec, out_shape=out_shape,
        scratch_shapes=[pltpu.VMEM((t1, t2), F32)],
        sem=("parallel", "parallel", "arbitrary"))


def _dgrad_blocked_call(name, g, w_g, comm=None, *, merge=1):
    t = g.shape[0]
    nb, d, cb = w_g.shape
    tm = _tile(t, 1024)
    tn = _tile(d, 2048)
    tk = merge * cb

    def body(g_ref, w_ref, o_ref):
        @pl.when(pl.program_id(2) == 0)
        def _():
            o_ref[...] = jnp.zeros_like(o_ref)

        w = w_ref[0] if merge == 1 else jnp.concatenate([w_ref[b] for b in range(merge)], axis=1)
        o_ref[...] += _dot_nt(g_ref[...], w)

    return _carrier_call(
        body, (g, w_g), comm, name=name,
        grid=(t // tm, d // tn, nb // merge),
        in_specs=[pl.BlockSpec((tm, tk), lambda i, j, k: (i, k)),
                  pl.BlockSpec((merge, tn, cb), lambda i, j, k: (k, j, 0))],
        out_specs=pl.BlockSpec((tm, tn), lambda i, j, k: (i, j)),
        out_shape=jax.ShapeDtypeStruct((t, d), F32),
        sem=("parallel", "parallel", "arbitrary"))


def _dmixed_call(dx2b, w_out, comm=None):
    t, d = dx2b.shape
    e = w_out.shape[0]
    tm = _tile(t, 1024)
    tn = _tile(e, 1024)

    def body(g_ref, w_ref, o_ref):
        o_ref[...] = _dot_nt(g_ref[...], w_ref[...])

    return _carrier_call(
        body, (dx2b, w_out), comm, name="dmixed_bwd",
        grid=(t // tm, e // tn),
        in_specs=[pl.BlockSpec((tm, d), lambda i, j: (i, 0)),
                  pl.BlockSpec((tn, d), lambda i, j: (j, 0))],
        out_specs=pl.BlockSpec((tm, tn), lambda i, j: (i, j)),
        out_shape=jax.ShapeDtypeStruct((t, e), F32),
        sem=("parallel", "parallel"))


def _mixer_bwd_call(proj, dmixed, w_s, bs_t, g_v, w_pool_g, pool_scale, comm=None):
    t = proj.shape[0]
    tt = _tile(t, 512)
    nchunk = tt // CHUNK
    hb = tt // HALO
    last_halo = t // HALO - 1
    nsteps = t // tt
    rb = GROUP // N_DEV

    def body(pu_ref, pv_ref, z_ref, zp_ref, da_ref, db_ref, dbn_ref, ws_ref, bs_ref, gv_ref, wp_ref, ps_ref,
             dproj_ref, dws_ref, dbs_ref, dgv_ref, dps_ref, dwp_ref):
        i = pl.program_id(0)

        @pl.when(i == 0)
        def _():
            dws_ref[...] = jnp.zeros_like(dws_ref)
            dbs_ref[...] = jnp.zeros_like(dbs_ref)
            dgv_ref[...] = jnp.zeros_like(dgv_ref)
            dps_ref[...] = jnp.zeros_like(dps_ref)
            dwp_ref[...] = jnp.zeros_like(dwp_ref)

        tril = (lax.broadcasted_iota(jnp.int32, (CHUNK, CHUNK), 0)
                >= lax.broadcasted_iota(jnp.int32, (CHUNK, CHUNK), 1))
        for h in range(N_HEADS):
            cols = slice(h * CHUNK, (h + 1) * CHUNK)
            v, dv_dpv = _gelu_and_grad(pv_ref[:, cols])
            vhat, rv = _rms_rows(v)
            gv = gv_ref[:, cols]
            vn = (vhat * gv).astype(BF16)
            u, du_dpu = _gelu_and_grad(pu_ref[:, cols])
            w = jnp.where(tril, ws_ref[h], 0.0).astype(BF16)
            bcol = bs_ref[:, h:h + 1]
            dout = da_ref[:, cols]
            dmix = dout * u
            dmix_b = dmix.astype(BF16)
            dws = jnp.zeros((CHUNK, CHUNK), F32)
            dbs = jnp.zeros((CHUNK, 1), F32)
            dvn_parts = []
            du_parts = []
            for c in range(nchunk):
                rows = slice(c * CHUNK, (c + 1) * CHUNK)
                mixed = _dot_nn(w, vn[rows]) + bcol
                du_parts.append(dout[rows] * mixed)
                dvn_parts.append(_dot_tn(w, dmix_b[rows]))
                dws = dws + _dot_nt(dmix_b[rows], vn[rows])
                dbs = dbs + jnp.sum(dmix[rows], axis=1, keepdims=True)
            dws_ref[h] += jnp.where(tril, dws, 0.0)
            dbs_ref[:, h:h + 1] += dbs
            dvn = jnp.concatenate(dvn_parts, axis=0)
            du = jnp.concatenate(du_parts, axis=0)
            dgv_ref[:, cols] += jnp.sum(dvn * vhat, axis=0, keepdims=True)
            dv = _rms_bwd_rows(dvn * gv, vhat, rv)
            dproj_ref[:, cols] = (du * du_dpu).astype(BF16)
            dproj_ref[:, A_WIDTH + h * CHUNK:A_WIDTH + (h + 1) * CHUNK] = (dv * dv_dpv).astype(BF16)

        zprev = jnp.where(i > 0, zp_ref[...], 0.0)
        ext = jnp.concatenate([zprev, z_ref[...]], axis=0)
        dnext = jnp.where(i < nsteps - 1, dbn_ref[...], 0.0)
        dext = jnp.concatenate([db_ref[...], dnext], axis=0)
        for g, win in enumerate(POOL_WINDOWS):
            cols = slice(g * GROUP, (g + 1) * GROUP)
            zg = ext[:, cols]
            pooled = _window_sum_back(zg, win)[HALO:] / _pool_counts(i * tt, tt, win) - zg[HALO:]
            pooled_b = pooled.astype(BF16)
            wp = wp_ref[:, g].reshape(GROUP, GROUP)
            y = _dot_nn(pooled_b, wp)
            dout = dext[:, cols]
            dps_ref[:, cols] += jnp.sum(dout[:tt] * y, axis=0, keepdims=True)
            dy_b = (dout * ps_ref[:, cols]).astype(BF16)
            dwp_ref[:, g] += _dot_tn(pooled_b, dy_b[:tt]).reshape(N_DEV, rb, GROUP)
            dpooled = _dot_nt(dy_b, wp)
            q = dpooled / _pool_counts(i * tt, tt + HALO, win)
            dz = _window_sum_fwd(q, win)[:tt] - dpooled[:tt]
            dproj_ref[:, 2 * A_WIDTH + g * GROUP:2 * A_WIDTH + (g + 1) * GROUP] = dz.astype(BF16)

    def full(shape):
        return pl.BlockSpec(shape, lambda i: (0,) * len(shape))

    return _carrier_call(
        body, (proj, proj, proj, proj, dmixed, dmixed, dmixed, w_s, bs_t, g_v, w_pool_g, pool_scale), comm,
        name="mixer_bwd",
        grid=(nsteps,),
        in_specs=[pl.BlockSpec((tt, A_WIDTH), lambda i: (i, 0)),
                  pl.BlockSpec((tt, A_WIDTH), lambda i: (i, 1)),
                  pl.BlockSpec((tt, B_WIDTH), lambda i: (i, 2)),
                  pl.BlockSpec((HALO, B_WIDTH), lambda i: (jnp.maximum(i * hb - 1, 0), 2)),
                  pl.BlockSpec((tt, A_WIDTH), lambda i: (i, 0)),
                  pl.BlockSpec((tt, B_WIDTH), lambda i: (i, 1)),
                  pl.BlockSpec((HALO, B_WIDTH), lambda i: (jnp.minimum((i + 1) * hb, last_halo), 1)),
                  full((N_HEADS, CHUNK, CHUNK)), full((CHUNK, N_HEADS)), full((1, A_WIDTH)),
                  full((N_DEV, 4, rb, GROUP)), full((1, B_WIDTH))],
        out_specs=[pl.BlockSpec((tt, 2 * A_WIDTH + B_WIDTH), lambda i: (i, 0)),
                   full((N_HEADS, CHUNK, CHUNK)), full((CHUNK, N_HEADS)), full((1, A_WIDTH)),
                   full((1, B_WIDTH)), full((N_DEV, 4, rb, GROUP))],
        out_shape=[jax.ShapeDtypeStruct((t, 2 * A_WIDTH + B_WIDTH), BF16),
                   jax.ShapeDtypeStruct((N_HEADS, CHUNK, CHUNK), F32),
                   jax.ShapeDtypeStruct((CHUNK, N_HEADS), F32),
                   jax.ShapeDtypeStruct((1, A_WIDTH), F32),
                   jax.ShapeDtypeStruct((1, B_WIDTH), F32),
                   jax.ShapeDtypeStruct((N_DEV, 4, rb, GROUP), F32)],
        sem=("arbitrary",))


def _adamw(w, g, m, v):
    m = ADAM_B1 * m + (1.0 - ADAM_B1) * g
    v = ADAM_B2 * v + (1.0 - ADAM_B2) * (g * g)
    m_hat = m / ADAM_C1
    v_hat = v / ADAM_C2
    delta = -ADAM_LR * (m_hat / (jnp.sqrt(v_hat) + ADAM_EPS) + ADAM_WD * w)
    return delta, m, v


PAIR_SUM_TILE_ELEMS = 1024 * 1024
ADAMW_TILE_ELEMS = 512 * 1024


def _row_tile(r, c, elems):
    t = r
    while t * c > elems and t % 32 == 0:
        t //= 2
    return t


def _pair_sum_call(name, pos, grad, got):
    _, r, c = grad.shape
    tr = _row_tile(r, c, PAIR_SUM_TILE_ELEMS)

    def chip_of(rel, pos_ref):
        px = jnp.where((rel == 0) | (rel == 2), 1 - pos_ref[0], pos_ref[0])
        py = jnp.where((rel == 1) | (rel == 2), 1 - pos_ref[1], pos_ref[1])
        return 2 * px + py

    def body(pos_ref, own_ref, got_ref, out_ref):
        out_ref[...] = (own_ref[...].astype(F32) + got_ref[...].astype(F32)).astype(BF16)

    return pl.pallas_call(
        body, name=name,
        grid_spec=pltpu.PrefetchScalarGridSpec(
            num_scalar_prefetch=1, grid=(3, r // tr),
            in_specs=[pl.BlockSpec((None, tr, c), lambda k, i, p: (2 * chip_of(k, p) + p[2], i, 0)),
                      pl.BlockSpec((None, tr, c), lambda k, i, p: (chip_of(k, p), i, 0))],
            out_specs=pl.BlockSpec((None, tr, c), lambda k, i, p: (k, i, 0))),
        out_shape=jax.ShapeDtypeStruct((3, r, c), BF16),
        compiler_params=_params("parallel", "parallel"),
    )(pos, grad, got)


def _final_call(name, pos, grad, got_pair, got_chips, w, m, v):
    _, r, c = grad.shape
    tr = _row_tile(r, c, ADAMW_TILE_ELEMS)

    def body(pos_ref, own_ref, pair_ref, chips_ref, w_ref, m_ref, v_ref, g_out, d_out, m_out, v_out):
        g = own_ref[...].astype(F32) + pair_ref[...].astype(F32)
        for j in range(3):
            g = g + chips_ref[j].astype(F32)
        delta, m_new, v_new = _adamw(w_ref[...], g, m_ref[...], v_ref[...])
        g_out[...] = g
        d_out[...] = delta
        m_out[...] = m_new
        v_out[...] = v_new

    row = pl.BlockSpec((tr, c), lambda i, p: (i, 0))
    return pl.pallas_call(
        body, name=name,
        grid_spec=pltpu.PrefetchScalarGridSpec(
            num_scalar_prefetch=1, grid=(r // tr,),
            in_specs=[pl.BlockSpec((None, tr, c), lambda i, p: (4 * p[0] + 2 * p[1] + p[2], i, 0)),
                      pl.BlockSpec((None, tr, c), lambda i, p: (2 * p[0] + p[1], i, 0)),
                      pl.BlockSpec((3, tr, c), lambda i, p: (0, i, 0)), row, row, row],
            out_specs=[row] * 4),
        out_shape=[jax.ShapeDtypeStruct((r, c), F32)] * 4,
        compiler_params=_params("parallel"),
    )(pos, grad, got_pair, got_chips, w, m, v)


def _small_final_call(name, parts, w, m, v):
    _, rows, c = parts.shape
    r = w.shape[0]

    def body(p_ref, w_ref, m_ref, v_ref, g_out, d_out, m_out, v_out):
        g = p_ref[0]
        for k in range(1, N_DEV):
            g = g + p_ref[k]
        delta, m_new, v_new = _adamw(w_ref[...], g[:r], m_ref[...], v_ref[...])
        g_out[...] = g
        d_out[...] = delta
        m_out[...] = m_new
        v_out[...] = v_new

    return pl.pallas_call(
        body, name=name,
        out_shape=[jax.ShapeDtypeStruct((rows, c), F32)] + [jax.ShapeDtypeStruct((r, c), F32)] * 3,
        compiler_params=pltpu.CompilerParams(vmem_limit_bytes=VMEM_LIMIT),
    )(parts, w, m, v)


_SMALL_EARLY = ("g_v", "w_s", "b_s", "pool_scale", "g_ffn", "g_final")
_BIG = ("w_in", "w_pool", "w_out", "w_up", "w_down")
_ORDER = ("g_mix", "w_in", "g_v", "w_s", "b_s", "w_pool", "pool_scale", "w_out", "g_ffn", "w_up", "w_down", "g_final")


def _pack(parts):
    return jnp.concatenate([p.reshape(-1, LANES) for p in parts], axis=0)


def _unpack(packed, like):
    out, row = [], 0
    for a in like:
        rows = a.size // LANES
        out.append(packed[row:row + rows].reshape(a.shape))
        row += rows
    return out


def kernel(x, g_mix, w_in, g_v, w_s, b_s, w_pool, pool_scale, w_out, g_ffn, w_up, w_down, g_final, loss_target, m_g_mix, m_w_in, m_g_v, m_w_s, m_b_s, m_w_pool, m_pool_scale, m_w_out, m_g_ffn, m_w_up, m_w_down, m_g_final, v_g_mix, v_w_in, v_g_v, v_w_s, v_b_s, v_w_pool, v_pool_scale, v_w_out, v_g_ffn, v_w_up, v_w_down, v_g_final):
    weights = dict(g_mix=g_mix, w_in=w_in, g_v=g_v, w_s=w_s, b_s=b_s, w_pool=w_pool, pool_scale=pool_scale,
                   w_out=w_out, g_ffn=g_ffn, w_up=w_up, w_down=w_down, g_final=g_final)
    mom = dict(g_mix=m_g_mix, w_in=m_w_in, g_v=m_g_v, w_s=m_w_s, b_s=m_b_s, w_pool=m_w_pool,
               pool_scale=m_pool_scale, w_out=m_w_out, g_ffn=m_g_ffn, w_up=m_w_up, w_down=m_w_down,
               g_final=m_g_final)
    var = dict(g_mix=v_g_mix, w_in=v_w_in, g_v=v_g_v, w_s=v_w_s, b_s=v_b_s, w_pool=v_w_pool,
               pool_scale=v_pool_scale, w_out=v_w_out, g_ffn=v_g_ffn, w_up=v_w_up, w_down=v_w_down,
               g_final=v_g_final)

    t, d = x.shape[1], x.shape[2]
    xs = x.reshape(t, d)
    target = loss_target.reshape(t, d)

    shard2d = dict(w_in=w_in.reshape(d, -1), w_pool=w_pool.reshape(-1, GROUP), w_out=w_out.reshape(-1, d),
                   w_up=w_up.reshape(d, -1), w_down=w_down.reshape(-1, d))
    sb = {k: shard2d[k].astype(BF16) for k in _BIG}
    rows = {k: sb[k].shape[0] for k in _BIG}

    def gathered_shape(k):
        return jax.ShapeDtypeStruct((N_DEV,) + sb[k].shape, BF16)

    def landing(n, like):
        return jax.ShapeDtypeStruct((n,) + like.shape[1:], like.dtype)

    def from_everyone(block):
        return jax.ShapeDtypeStruct((N_DEV,) + block.shape, block.dtype)

    def cuts(r, fractions):
        return [0] + [int(r * f) // 16 * 16 for f in fractions] + [r]

    g_mix2, g_ffn2, g_final2 = g_mix.reshape(1, d), g_ffn.reshape(1, d), g_final.reshape(1, d)
    g_v2, ps2 = g_v.reshape(1, A_WIDTH), pool_scale.reshape(1, B_WIDTH)
    w_s3 = w_s.reshape(N_HEADS, CHUNK, CHUNK)
    bs_t = b_s.reshape(N_HEADS, CHUNK).T
    xi, yi, ci = _position()
    pos = jnp.stack([xi, yi, ci]).astype(jnp.int32)

    order = (4 * xi + 2 * yi + ci) ^ jnp.array(ARRIVAL_ORDER, jnp.int32)
    u = cuts(rows["w_up"], (0.25, 0.55))
    ahead = cuts(rows["w_down"], (0.4,))[1]
    proj, h1, w_in_g, w_out_g, w_pool_g = _norm_matmul_stream_call(
        "proj_fwd", xs, g_mix2, sb["w_in"], order, None, 0, 0, _Comm(
            [sb["w_out"], sb["w_pool"]], [], [gathered_shape("w_out"), gathered_shape("w_pool")],
            lambda s, l: [_gather_first(s[0], l[0], 0, rows["w_out"]) + _everyone(s[1], l[1])]),
        lambda a: (a,), (F32,))
    w_pool_g = w_pool_g.reshape(N_DEV, 4, GROUP // N_DEV, GROUP)
    mixed, w_out_g, w_up_g = _mixer_fwd_call(proj, w_s3, bs_t, g_v2, w_pool_g, ps2, _Comm(
        [sb["w_up"]], [w_out_g], [gathered_shape("w_up")],
        lambda s, l: [_gather_pass_on(l[0], 0, rows["w_out"]) + _gather_first(s[0], l[1], u[0], u[1])]))
    w_out_f = w_out_g.reshape(-1, d)
    x2, w_up_g = _out_proj_call(mixed, w_out_f, xs, _Comm(
        [sb["w_up"]], [w_up_g], [],
        lambda s, l: [_gather_pass_on(l[0], u[0], u[1]) + _gather_first(s[0], l[0], u[1], u[2])]))
    def relu2_and_slope(a):
        r = jnp.maximum(a, 0.0)
        return r * r, 2.0 * r

    act, dact_da, h2, w_up_g, w_down_g = _norm_matmul_stream_call(
        "up_fwd", x2, g_ffn2, sb["w_up"], order, w_up_g, u[1], u[2], _Comm(
            [sb["w_down"]], [], [gathered_shape("w_down")],
            lambda s, l: [_gather_first(s[0], l[0], 0, ahead)]),
        relu2_and_slope, (BF16, BF16))
    y, w_down_g = _down_call(act, sb["w_down"], order, w_down_g, ahead)
    w_down_f = w_down_g.reshape(-1, d)
    loss_part, dx3, dx3b, dg_final = _loss_call(y, x2, target, g_final2)

    def pair_sum(k, grad, got):
        return _pair_sum_call(k + "_pair_sum", pos, grad, got)

    def finish(k, grad, got_pair, got_chips):
        s = shard2d[k]
        outs = _final_call(k + "_adamw", pos, grad, got_pair, got_chips, s, mom[k].reshape(s.shape),
                           var[k].reshape(s.shape))
        return [o.reshape(weights[k].shape) for o in outs]

    result = {}
    (gw_down,) = _wgrad_call("w_down_grad", act, dx3b, None, None, t1=1024, t2=2048)
    gw_down = gw_down.reshape(N_DEV, -1, d)
    da, pair_down = _dact_call(dx3b, w_down_f, dact_da, _Comm(
        [gw_down], [], [landing(4, gw_down)], lambda s, l: [_pair_exchange(s[0], l[0])]))
    sums_down = pair_sum("w_down", gw_down, pair_down)
    dn = cuts(rows["w_down"], (0.8,))
    gw_up, got = _wgrad_call("w_up_grad", h2, da, N_DEV, w_up_g.shape[2], _Comm(
        [sums_down], [], [landing(3, sums_down)],
        lambda s, l: [_chip_exchange(s[0], l[0], dn[0], dn[1])]), t1=2048)
    dh2, got, pair_up = _dgrad_blocked_call("dh2_bwd", da, w_up_g, _Comm(
        [sums_down, gw_up], [got], [landing(4, gw_up)],
        lambda s, l: [_chip_exchange(s[0], l[0], dn[1], dn[2]) + _pair_exchange(s[1], l[1])]), merge=2)
    result["w_down"] = finish("w_down", gw_down, pair_down, got)
    sums_up = pair_sum("w_up", gw_up, pair_up)
    v = cuts(rows["w_up"], (0.28, 0.52, 0.84))
    dx2, dx2b, dg_ffn, got_up = _norm_bwd_call("ffn_norm_bwd", dh2, x2, dx3, g_ffn2, True, _Comm(
        [sums_up], [], [landing(3, sums_up)], lambda s, l: [_chip_exchange(s[0], l[0], v[0], v[1])]))
    dmixed, got_up = _dmixed_call(dx2b, w_out_f, _Comm(
        [sums_up], [got_up], [], lambda s, l: [_chip_exchange(s[0], l[0], v[1], v[2])]))
    gw_out, got_up = _wgrad_call("w_out_grad", mixed, dx2b, None, None, _Comm(
        [sums_up], [got_up], [], lambda s, l: [_chip_exchange(s[0], l[0], v[2], v[3])]), t1=2048, t2=1024)
    gw_out = gw_out.reshape(N_DEV, -1, d)
    dproj, dw_s, dbs_t, dg_v, dps, dw_pool, got_up, pair_out = _mixer_bwd_call(
        proj, dmixed, w_s3, bs_t, g_v2, w_pool_g, ps2, _Comm(
            [sums_up, gw_out], [got_up], [landing(4, gw_out)],
            lambda s, l: [_chip_exchange(s[0], l[0], v[3], v[4]) + _pair_exchange(s[1], l[1])]))
    result["w_up"] = finish("w_up", gw_up, pair_up, got_up)
    sums_out = pair_sum("w_out", gw_out, pair_out)
    gw_pool = dw_pool.astype(BF16).reshape(N_DEV, -1, GROUP)
    early = dict(g_v=dg_v, w_s=dw_s, b_s=dbs_t.T, pool_scale=dps, g_ffn=dg_ffn, g_final=dg_final)
    packed = _pack([early[k] for k in _SMALL_EARLY] + [loss_part])
    early_rows = packed.shape[0]
    gw_in, got, pair_pool, parts_early = _wgrad_call("w_in_grad", h1, dproj, N_DEV, w_in_g.shape[2], _Comm(
        [sums_out, gw_pool, packed], [], [landing(3, sums_out), landing(4, gw_pool), from_everyone(packed)],
        lambda s, l: [_chip_exchange(s[0], l[0], 0, rows["w_out"]) + _pair_exchange(s[1], l[1])
                      + _gather_first(s[2], l[2], 0, early_rows)]), t1=2048, merge=MERGE_W_IN)
    result["w_out"] = finish("w_out", gw_out, pair_out, got)
    sums_pool = pair_sum("w_pool", gw_pool, pair_pool)
    (pair_in,) = _comm_call("pair_exchange_w_in", _Comm(
        [gw_in], [], [landing(4, gw_in)], lambda s, l: [_pair_exchange(s[0], l[0])]))
    sums_in = pair_sum("w_in", gw_in, pair_in)
    dh1, parts_early, got, got_pool = _dgrad_blocked_call("dh1_bwd", dproj, w_in_g, _Comm(
        [sums_in, sums_pool], [parts_early], [landing(3, sums_in), landing(3, sums_pool)],
        lambda s, l: [_chip_exchange(s[0], l[1], 0, rows["w_in"]) + _chip_exchange(s[1], l[2], 0, rows["w_pool"])
                      + _gather_pass_on(l[0], 0, early_rows)]),
        merge=MERGE_W_IN)
    result["w_in"] = finish("w_in", gw_in, pair_in, got)
    result["w_pool"] = finish("w_pool", gw_pool, pair_pool, got_pool)
    grad_x, dg_mix = _norm_bwd_call("mix_norm_bwd", dh1, xs, dx2, g_mix2, False)
    packed = _pack([dg_mix])
    (parts_late,) = _comm_call("gather_g_mix_grad", _Comm(
        [packed], [], [from_everyone(packed)], lambda s, l: [_everyone(s[0], l[0])]))

    for names, parts, tag in ((_SMALL_EARLY, parts_early, "small_adamw"), (("g_mix",), parts_late, "g_mix_adamw")):
        outs = _small_final_call(tag, parts, _pack([weights[k] for k in names]), _pack([mom[k] for k in names]),
                                 _pack([var[k] for k in names]))
        if tag == "small_adamw":
            loss = outs[0][-1, 0]
        like = [weights[k] for k in names]
        unpacked = [_unpack(o, like) for o in outs]
        for idx, k in enumerate(names):
            result[k] = [unpacked[q][idx] for q in range(4)]

    grads = [result[k][0] for k in _ORDER]
    deltas = [result[k][1] for k in _ORDER]
    new_m = [result[k][2] for k in _ORDER]
    new_v = [result[k][3] for k in _ORDER]
    return (loss, grad_x.reshape(x.shape), *grads, *deltas, *new_m, *new_v)
```

```python
import functools
import math

import jax
import jax.numpy as jnp
from jax import lax
from jax.experimental import pallas as pl
from jax.experimental.pallas import tpu as pltpu

F32 = jnp.float32
BF16 = jnp.bfloat16
MESH = pl.DeviceIdType.MESH

N_DEV = 8
EPS = 1e-6
CHUNK = 128
N_HEADS = 8
A_WIDTH = 1024
B_WIDTH = 1024
POOL_WINDOWS = (2, 4, 8, 16)
GROUP = 256
HALO = 16
LANES = 128

ADAM_LR = 0.001
ADAM_B1 = 0.9
ADAM_B2 = 0.999
ADAM_EPS = 1e-08
ADAM_WD = 0.01
ADAM_STEP = 10
ADAM_C1 = 1.0 - ADAM_B1 ** ADAM_STEP
ADAM_C2 = 1.0 - ADAM_B2 ** ADAM_STEP

VMEM_LIMIT = 56 * 1024 * 1024
MERGE_W_IN = 2

_GELU_C = math.sqrt(2.0 / math.pi)


def _params(*sem):
    return pltpu.CompilerParams(dimension_semantics=sem, vmem_limit_bytes=VMEM_LIMIT)


def _gelu(x):
    return 0.5 * x * (1.0 + jnp.tanh(_GELU_C * (x + 0.044715 * x * x * x)))


def _gelu_and_grad(x):
    t = jnp.tanh(_GELU_C * (x + 0.044715 * x * x * x))
    g = 0.5 * x * (1.0 + t)
    dg = 0.5 * (1.0 + t) + 0.5 * x * (1.0 - t * t) * (_GELU_C * (1.0 + 3.0 * 0.044715 * x * x))
    return g, dg


def _dot_nn(a, b):
    return lax.dot_general(a, b, (((1,), (0,)), ((), ())), preferred_element_type=F32)


def _dot_nt(a, b):
    return lax.dot_general(a, b, (((1,), (1,)), ((), ())), preferred_element_type=F32)


def _dot_tn(a, b):
    return lax.dot_general(a, b, (((0,), (0,)), ((), ())), preferred_element_type=F32)


def _rms_rows(x):
    r = lax.rsqrt(jnp.mean(x * x, axis=-1, keepdims=True) + EPS)
    return x * r, r


def _rms_bwd_rows(dn, n, r):
    return r * (dn - n * jnp.mean(dn * n, axis=-1, keepdims=True))


def _tile(n, want):
    t = min(n, want)
    assert n % t == 0, (n, want)
    return t


_ANY = pl.BlockSpec(memory_space=pl.ANY)

SIBLING = 1
CHIPS = (4, 2, 6)


def _position():
    return lax.axis_index("x"), lax.axis_index("y"), lax.axis_index("c")


def _me():
    x, y, c = _position()
    return 4 * x + 2 * y + c


def _peer(rel):
    x, y, c = _position()
    return (x ^ ((rel >> 2) & 1), y ^ ((rel >> 1) & 1), c ^ (rel & 1))


class _Comm:
    def __init__(self, srcs, lands, new, plan):
        self.srcs, self.lands, self.new, self.plan = list(srcs), list(lands), list(new), plan


def _make_copies(phases, send_sems, recv_sems, local_sems):
    out, nr, nl = [], 0, 0
    for phase in phases:
        cps = []
        for item in phase:
            if item[0] == "local":
                cps.append(pltpu.make_async_copy(item[1], item[2], local_sems.at[nl]))
                nl += 1
            else:
                cps.append(pltpu.make_async_remote_copy(
                    src_ref=item[1], dst_ref=item[2], send_sem=send_sems.at[nr], recv_sem=recv_sems.at[nr],
                    device_id=_peer(item[3]), device_id_type=MESH))
                nr += 1
        out.append(cps)
    return out


def _count_copies(comm):
    phases = comm.plan([_FakeRef() for _ in comm.srcs], [_FakeRef() for _ in range(len(comm.lands) + len(comm.new))])
    items = [it for ph in phases for it in ph]
    return sum(it[0] == "remote" for it in items), sum(it[0] == "local" for it in items)


class _FakeRef:
    def __getitem__(self, idx):
        return self

    @property
    def at(self):
        return self


def _carrier_call(body, args, comm, *, name, grid, in_specs, out_specs, out_shape, scratch_shapes=(), sem):
    if not isinstance(out_shape, (list, tuple)):
        out_specs, out_shape = [out_specs], [out_shape]
    out_specs, out_shape, scratch_shapes = list(out_specs), list(out_shape), list(scratch_shapes)
    if comm is None:
        res = pl.pallas_call(body, name=name, grid=grid, in_specs=list(in_specs), out_specs=out_specs,
                             out_shape=out_shape, scratch_shapes=scratch_shapes, compiler_params=_params(*sem))(*args)
        return list(res)
    n_in, n_out, n_scr = len(args), len(out_shape), len(scratch_shapes)
    ns, nl, nn = len(comm.srcs), len(comm.lands), len(comm.new)
    n_remote, n_local = _count_copies(comm)
    steps = math.prod(grid)

    def wrapped(*refs):
        ins, srcs = refs[:n_in], refs[n_in:n_in + ns]
        o = n_in + ns + nl
        outs, lands = refs[o:o + n_out], refs[o + n_out:o + n_out + nl + nn]
        scr = refs[o + n_out + nl + nn:]
        phases = _make_copies(comm.plan(srcs, lands), *scr[n_scr:])
        assert len(phases) == 1 or (len(phases) == 2 and steps >= 3)
        step = functools.reduce(lambda acc, a: acc * grid[a] + pl.program_id(a), range(len(grid)), 0)

        @pl.when(step == 0)
        def _():
            for cp in phases[0]:
                cp.start()

        if len(phases) == 2:
            @pl.when(step == steps // 2)
            def _():
                for cp in phases[0]:
                    cp.wait()
                for cp in phases[1]:
                    cp.start()

        body(*ins, *outs, *scr[:n_scr])

        @pl.when(step == steps - 1)
        def _():
            for cp in phases[-1]:
                cp.wait()

    land_shapes = [jax.ShapeDtypeStruct(a.shape, a.dtype) for a in comm.lands] + comm.new
    sems = [pltpu.SemaphoreType.DMA((max(n_remote, 1),)), pltpu.SemaphoreType.DMA((max(n_remote, 1),)),
            pltpu.SemaphoreType.DMA((max(n_local, 1),))]
    res = pl.pallas_call(
        wrapped, name=name, grid=grid,
        in_specs=list(in_specs) + [_ANY] * (ns + nl), out_specs=out_specs + [_ANY] * (nl + nn),
        out_shape=out_shape + land_shapes, scratch_shapes=scratch_shapes + sems,
        input_output_aliases={n_in + ns + k: n_out + k for k in range(nl)},
        compiler_params=_params(*sem))(*args, *comm.srcs, *comm.lands)
    return list(res)


def _comm_call(name, comm):
    ns, nl, nn = len(comm.srcs), len(comm.lands), len(comm.new)
    n_remote, n_local = _count_copies(comm)

    def body(*refs):
        srcs, lands, sems = refs[:ns], refs[ns + nl:ns + nl + nl + nn], refs[ns + nl + nl + nn:]
        for copies in _make_copies(comm.plan(srcs, lands), *sems):
            for cp in copies:
                cp.start()
            for cp in copies:
                cp.wait()

    land_shapes = [jax.ShapeDtypeStruct(a.shape, a.dtype) for a in comm.lands] + comm.new
    res = pl.pallas_call(
        body, name=name,
        in_specs=[_ANY] * (ns + nl), out_specs=[_ANY] * (nl + nn), out_shape=land_shapes,
        scratch_shapes=[pltpu.SemaphoreType.DMA((max(n_remote, 1),)), pltpu.SemaphoreType.DMA((max(n_remote, 1),)),
                        pltpu.SemaphoreType.DMA((max(n_local, 1),))],
        input_output_aliases={ns + k: k for k in range(nl)},
    )(*comm.srcs, *comm.lands)
    return list(res)


def _rows(ref, block, r0, r1):
    return ref.at[block, pl.ds(r0, r1 - r0)]


def _gather_first(shard, land, r0, r1):
    src = shard.at[pl.ds(r0, r1 - r0)]
    dst = _rows(land, _me(), r0, r1)
    return [("local", src, dst)] + [("remote", src, dst, rel) for rel in (SIBLING,) + CHIPS]


def _gather_pass_on(land, r0, r1):
    return [("remote", _rows(land, _me() ^ rel, r0, r1), _rows(land, _me() ^ rel, r0, r1), SIBLING) for rel in CHIPS]


def _split_rows(r0, r1):
    m = (r0 + r1) // 2 // 16 * 16
    return (r0, m), (m, r1)


def _gather_to_neighbours(shard, land, r0, r1):
    src = shard.at[pl.ds(r0, r1 - r0)]
    dst = _rows(land, _me(), r0, r1)
    return [("local", src, dst)] + [("remote", src, dst, rel) for rel in (SIBLING, 4, 2)]


def _gather_relay(land, r0, r1):
    lo, hi = _split_rows(r0, r1)
    x_block, y_block = _me() ^ 4, _me() ^ 2
    return [("remote", _rows(land, x_block, *lo), _rows(land, x_block, *lo), 2),
            ("remote", _rows(land, y_block, *hi), _rows(land, y_block, *hi), 4),
            ("remote", _rows(land, x_block, r0, r1), _rows(land, x_block, r0, r1), SIBLING),
            ("remote", _rows(land, y_block, r0, r1), _rows(land, y_block, r0, r1), SIBLING)]


def _gather_diagonal_pass_on(land, r0, r1):
    rows = _rows(land, _me() ^ 6, r0, r1)
    return [("remote", rows, rows, SIBLING)]


def _pair_exchange(grad, land):
    _, _, c = _position()
    return [("remote", grad.at[2 * chip + (1 - c)], land.at[chip], SIBLING) for chip in range(4)]


def _chip_exchange(sums, land, r0, r1):
    return [("remote", _rows(sums, j, r0, r1), _rows(land, j, r0, r1), rel) for j, rel in enumerate(CHIPS)]


def _everyone(packed, land):
    dst = land.at[_me()]
    return [("local", packed, dst)] + [("remote", packed, dst, rel) for rel in range(1, N_DEV)]


def _pool_counts(row0, rows, win):
    pos = row0 + lax.broadcasted_iota(jnp.int32, (rows, 1), 0)
    return jnp.minimum(pos + 1, win).astype(F32)


def _window_sum_back(ext, win):
    s = ext
    k = 1
    while k < win:
        s = s + pltpu.roll(s, k, 0)
        k *= 2
    return s


def _window_sum_fwd(ext, win):
    n = ext.shape[0]
    s = ext
    k = 1
    while k < win:
        s = s + pltpu.roll(s, n - k, 0)
        k *= 2
    return s


def _mixer_fwd_call(proj, w_s, bs_t, g_v, w_pool_g, pool_scale, comm=None):
    t = proj.shape[0]
    tt = _tile(t, 512)
    nchunk = tt // CHUNK
    hb = tt // HALO

    def body(pu_ref, pv_ref, z_ref, zp_ref, ws_ref, bs_ref, gv_ref, wp_ref, ps_ref, out_ref):
        i = pl.program_id(0)
        tril = (lax.broadcasted_iota(jnp.int32, (CHUNK, CHUNK), 0)
                >= lax.broadcasted_iota(jnp.int32, (CHUNK, CHUNK), 1))
        for h in range(N_HEADS):
            cols = slice(h * CHUNK, (h + 1) * CHUNK)
            vhat, _ = _rms_rows(_gelu(pv_ref[:, cols]))
            vn = (vhat * gv_ref[:, cols]).astype(BF16)
            u = _gelu(pu_ref[:, cols])
            w = jnp.where(tril, ws_ref[h], 0.0).astype(BF16)
            bcol = bs_ref[:, h:h + 1]
            for c in range(nchunk):
                rows = slice(c * CHUNK, (c + 1) * CHUNK)
                mixed = _dot_nn(w, vn[rows]) + bcol
                out_ref[rows, cols] = (u[rows] * mixed).astype(BF16)

        zprev = jnp.where(i > 0, zp_ref[...], 0.0)
        ext = jnp.concatenate([zprev, z_ref[...]], axis=0)
        for g, win in enumerate(POOL_WINDOWS):
            cols = slice(g * GROUP, (g + 1) * GROUP)
            zg = ext[:, cols]
            s = _window_sum_back(zg, win)
            pooled = s[HALO:] / _pool_counts(i * tt, tt, win) - zg[HALO:]
            wp = wp_ref[:, g].reshape(GROUP, GROUP)
            y = _dot_nn(pooled.astype(BF16), wp)
            out_ref[:, A_WIDTH + g * GROUP:A_WIDTH + (g + 1) * GROUP] = (y * ps_ref[:, cols]).astype(BF16)

    return _carrier_call(
        body, (proj, proj, proj, proj, w_s, bs_t, g_v, w_pool_g, pool_scale), comm, name="mixer_fwd",
        grid=(t // tt,),
        in_specs=[pl.BlockSpec((tt, A_WIDTH), lambda i: (i, 0)),
                  pl.BlockSpec((tt, A_WIDTH), lambda i: (i, 1)),
                  pl.BlockSpec((tt, B_WIDTH), lambda i: (i, 2)),
                  pl.BlockSpec((HALO, B_WIDTH), lambda i: (jnp.maximum(i * hb - 1, 0), 2)),
                  pl.BlockSpec((N_HEADS, CHUNK, CHUNK), lambda i: (0, 0, 0)),
                  pl.BlockSpec((CHUNK, N_HEADS), lambda i: (0, 0)),
                  pl.BlockSpec((1, A_WIDTH), lambda i: (0, 0)),
                  pl.BlockSpec((N_DEV, 4, GROUP // N_DEV, GROUP), lambda i: (0, 0, 0, 0)),
                  pl.BlockSpec((1, B_WIDTH), lambda i: (0, 0))],
        out_specs=pl.BlockSpec((tt, A_WIDTH + B_WIDTH), lambda i: (i, 0)),
        out_shape=jax.ShapeDtypeStruct((t, A_WIDTH + B_WIDTH), BF16),
        sem=("parallel",))


def _out_proj_call(mixed, w_out, x, comm=None):
    t, d = x.shape
    k = mixed.shape[1]
    tm = _tile(t, 1024)
    tn = _tile(d, 1024)

    def body(a_ref, w_ref, x_ref, o_ref):
        o_ref[...] = x_ref[...] + _dot_nn(a_ref[...], w_ref[...])

    return _carrier_call(
        body, (mixed, w_out, x), comm, name="out_proj_fwd",
        grid=(t // tm, d // tn),
        in_specs=[pl.BlockSpec((tm, k), lambda i, j: (i, 0)),
                  pl.BlockSpec((k, tn), lambda i, j: (0, j)),
                  pl.BlockSpec((tm, tn), lambda i, j: (i, j))],
        out_specs=pl.BlockSpec((tm, tn), lambda i, j: (i, j)),
        out_shape=jax.ShapeDtypeStruct((t, d), F32),
        sem=("parallel", "parallel"))


ARRIVAL_ORDER = (0, 1, 4, 5, 2, 3, 6, 7)
CARRIED_AFTER = 3


class _StreamedGather:
    def __init__(self, shard_ref, land_ref, wbuf, pre0, r0, send_sems, recv_sems, local_sem, fetch_sems):
        self.shard, self.land, self.wbuf, self.fetch_sems = shard_ref, land_ref, wbuf, fetch_sems
        end = shard_ref.shape[0]
        me = _me()
        self.me = me

        def remote(k, src, dst, rel):
            return pltpu.make_async_remote_copy(src_ref=src, dst_ref=dst, send_sem=send_sems.at[k],
                                                recv_sem=recv_sems.at[k], device_id=_peer(rel), device_id_type=MESH)

        def same_rows(k, block, a, b, rel):
            ref = land_ref.at[block, pl.ds(a, b - a)]
            return remote(k, ref, ref, rel)

        src = shard_ref.at[pl.ds(r0, end - r0)]
        dst = land_ref.at[me, pl.ds(r0, end - r0)]
        self.mine = pltpu.make_async_copy(src, dst, local_sem)
        self.first = [remote(k, src, dst, rel) for k, rel in enumerate((SIBLING, 4, 2))]
        lo, hi = _split_rows(r0, end)
        self.relay = [same_rows(3, me ^ 4, *lo, 2), same_rows(4, me ^ 2, *hi, 4)]
        self.passed = [same_rows(5, me ^ 4, r0, end, SIBLING), same_rows(6, me ^ 2, r0, end, SIBLING),
                       same_rows(7, me ^ 6, pre0, end, SIBLING)]
        self.early_relay, self.early_passed = [], []
        if r0 > pre0:
            lo, hi = _split_rows(pre0, r0)
            self.early_relay = [same_rows(8, me ^ 4, *lo, 2), same_rows(9, me ^ 2, *hi, 4)]
            self.early_passed = [same_rows(10, me ^ 4, pre0, r0, SIBLING), same_rows(11, me ^ 2, pre0, r0, SIBLING)]

    def _fetch(self, q):
        src = self.shard if q == 0 else self.land.at[self.me ^ ARRIVAL_ORDER[q]]
        return pltpu.make_async_copy(src, self.wbuf.at[q % 2], self.fetch_sems.at[q % 2])

    def start(self):
        self.mine.start()
        for cp in self.first + self.early_relay + self.early_passed:
            cp.start()
        self._fetch(0).start()

    def arrive(self, q):
        if q == 1:
            self.first[0].wait_recv()
        elif q in (2, 4):
            j = q // 2 - 1
            self.first[1 + j].wait_recv()
            self.relay[j].start()
            self.passed[j].start()
        elif q in (3, 5):
            j = q // 2 - 1
            self.passed[j].wait_recv()
            if self.early_passed:
                self.early_passed[j].wait_recv()
        elif q == 6:
            for cp in self.relay + self.early_relay:
                cp.wait_recv()
            self.passed[2].start()
        else:
            self.passed[2].wait_recv()
        self._fetch(q).start()

    def wait_fetch(self, slot):
        pltpu.make_async_copy(self.shard, self.wbuf.at[slot], self.fetch_sems.at[slot]).wait()

    def finish(self):
        for cp in self.first + self.relay + self.passed + self.early_relay + self.early_passed:
            cp.wait_send()
        self.mine.wait()


_STREAM_SEMS = [pltpu.SemaphoreType.DMA((12,)), pltpu.SemaphoreType.DMA((12,)), pltpu.SemaphoreType.DMA,
                pltpu.SemaphoreType.DMA((2,))]


def _stream_steps(gather, p, i, ni):
    @pl.when((p == 0) & (i == 0))
    def _():
        gather.start()

    @pl.when(i == 0)
    def _():
        gather.wait_fetch(p % 2)

    @pl.when(i == ni - 1)
    def _():
        for q in range(1, N_DEV):
            @pl.when(p == q - 1)
            def _():
                gather.arrive(q)


def _norm_matmul_stream_call(name, x, g, shard, order, land, pre0, r0, comm, epilogue, out_dtypes):
    t, d = x.shape
    cb = shard.shape[1]
    tm = _tile(t, 1024)
    ni = t // tm
    n_sems = len(_STREAM_SEMS)
    assert not comm.lands
    ns, nn, no = len(comm.srcs), len(comm.new), len(out_dtypes)
    n_remote, n_local = _count_copies(comm)
    has_land = land is not None

    def body(order_ref, x_ref, g_ref, shard_ref, *refs):
        refs = refs[has_land:]
        srcs, out_refs, (h_ref, land_ref) = refs[:ns], refs[ns:ns + no], refs[ns + no:ns + no + 2]
        new = refs[ns + no + 2:ns + no + 2 + nn]
        wbuf, sems = refs[ns + no + 2 + nn], refs[ns + no + 3 + nn:]
        p, i = pl.program_id(0), pl.program_id(1)
        gather = _StreamedGather(shard_ref, land_ref, wbuf, pre0, r0, *sems[:n_sems])
        (carried,) = _make_copies(comm.plan(srcs, new), *sems[n_sems:])
        rows = pl.ds(pl.multiple_of(i * tm, tm), tm)
        _stream_steps(gather, p, i, ni)

        @pl.when((p == CARRIED_AFTER) & (i == ni - 1))
        def _():
            for cp in carried:
                cp.start()

        @pl.when(p == 0)
        def _():
            n, _ = _rms_rows(x_ref[...])
            h_ref[rows, :] = (n * g_ref[...]).astype(BF16)

        tails = epilogue(_dot_nn(h_ref[rows, :], wbuf[p % 2]))
        for out_ref, tail, dt in zip(out_refs, tails, out_dtypes):
            out_ref[...] = tail.astype(dt)

        @pl.when((p == N_DEV - 1) & (i == ni - 1))
        def _():
            gather.finish()
            for cp in carried:
                cp.wait()

    carried_sems = [pltpu.SemaphoreType.DMA((max(n_remote, 1),)), pltpu.SemaphoreType.DMA((max(n_remote, 1),)),
                    pltpu.SemaphoreType.DMA((max(n_local, 1),))]
    return pl.pallas_call(
        body, name=name,
        grid_spec=pltpu.PrefetchScalarGridSpec(
            num_scalar_prefetch=1, grid=(N_DEV, ni),
            in_specs=[pl.BlockSpec((tm, d), lambda p, i, o: (jnp.where(p == 0, i, ni - 1), 0)),
                      pl.BlockSpec((1, d), lambda p, i, o: (0, 0)),
                      _ANY] + [_ANY] * (has_land + ns),
            out_specs=[pl.BlockSpec((tm, cb), lambda p, i, o: (i, o[p]))] * no
                      + [pl.BlockSpec(memory_space=pltpu.VMEM), _ANY] + [_ANY] * nn,
            scratch_shapes=[pltpu.VMEM((2, d, cb), BF16)] + _STREAM_SEMS + carried_sems),
        out_shape=[jax.ShapeDtypeStruct((t, N_DEV * cb), dt) for dt in out_dtypes]
                  + [jax.ShapeDtypeStruct((t, d), BF16), jax.ShapeDtypeStruct((N_DEV, d, cb), BF16)] + comm.new,
        input_output_aliases={4: no + 1} if has_land else {},
        compiler_params=_params("arbitrary", "arbitrary"),
    )(order, x, g, shard, *([land] if has_land else []), *comm.srcs)


def _down_call(act, shard, order, land, r0):
    t = act.shape[0]
    rb, d = shard.shape
    tm = _tile(t, 1024)
    ni = t // tm

    def body(order_ref, a_ref, shard_ref, land_in_ref, y_ref, land_ref, wbuf, *sems):
        p, i = pl.program_id(0), pl.program_id(1)
        gather = _StreamedGather(shard_ref, land_ref, wbuf, 0, r0, *sems)
        rows = pl.ds(pl.multiple_of(i * tm, tm), tm)
        _stream_steps(gather, p, i, ni)
        @pl.when(p == 0)
        def _():
            y_ref[rows, :] = _dot_nn(a_ref[...], wbuf[0])

        @pl.when(p > 0)
        def _():
            y_ref[rows, :] += _dot_nn(a_ref[...], wbuf[p % 2])

        @pl.when((p == N_DEV - 1) & (i == ni - 1))
        def _():
            gather.finish()

    return pl.pallas_call(
        body, name="down_fwd",
        grid_spec=pltpu.PrefetchScalarGridSpec(
            num_scalar_prefetch=1, grid=(N_DEV, ni),
            in_specs=[pl.BlockSpec((tm, rb), lambda p, i, o: (i, o[p])), _ANY, _ANY],
            out_specs=[pl.BlockSpec(memory_space=pltpu.VMEM), _ANY],
            scratch_shapes=[pltpu.VMEM((2, rb, d), BF16)] + _STREAM_SEMS),
        out_shape=[jax.ShapeDtypeStruct((t, d), F32), jax.ShapeDtypeStruct((N_DEV, rb, d), BF16)],
        input_output_aliases={3: 1},
        compiler_params=_params("arbitrary", "arbitrary"),
    )(order, act, shard, land)


def _loss_call(y, x2, target, g_final):
    t, d = y.shape
    tr = _tile(t, 256)

    def body(y_ref, x_ref, tg_ref, g_ref, loss_ref, dx_ref, dxb_ref, dg_ref):
        @pl.when(pl.program_id(0) == 0)
        def _():
            loss_ref[...] = jnp.zeros_like(loss_ref)
            dg_ref[...] = jnp.zeros_like(dg_ref)

        n, r = _rms_rows(x_ref[...] + y_ref[...])
        err = n * g_ref[...] - tg_ref[...]
        loss_ref[...] += 0.5 * jnp.sum(jnp.mean(err * err, axis=-1, keepdims=True))
        dy = err * (1.0 / d)
        dg_ref[...] += jnp.sum(dy * n, axis=0, keepdims=True)
        dx = _rms_bwd_rows(dy * g_ref[...], n, r)
        dx_ref[...] = dx
        dxb_ref[...] = dx.astype(BF16)

    return pl.pallas_call(
        body, name="loss_head",
        grid=(t // tr,),
        in_specs=[pl.BlockSpec((tr, d), lambda i: (i, 0)),
                  pl.BlockSpec((tr, d), lambda i: (i, 0)),
                  pl.BlockSpec((tr, d), lambda i: (i, 0)),
                  pl.BlockSpec((1, d), lambda i: (0, 0))],
        out_specs=[pl.BlockSpec((8, LANES), lambda i: (0, 0)),
                   pl.BlockSpec((tr, d), lambda i: (i, 0)),
                   pl.BlockSpec((tr, d), lambda i: (i, 0)),
                   pl.BlockSpec((1, d), lambda i: (0, 0))],
        out_shape=[jax.ShapeDtypeStruct((8, LANES), F32), jax.ShapeDtypeStruct((t, d), F32),
                   jax.ShapeDtypeStruct((t, d), BF16), jax.ShapeDtypeStruct((1, d), F32)],
        compiler_params=_params("arbitrary"),
    )(y, x2, target, g_final)


def _norm_bwd_call(name, dh, x, dres, g, want_bf16, comm=None):
    t, d = x.shape
    tr = _tile(t, 256)

    def body(dh_ref, x_ref, dres_ref, g_ref, dx_ref, *rest):
        dg_ref = rest[-1]

        @pl.when(pl.program_id(0) == 0)
        def _():
            dg_ref[...] = jnp.zeros_like(dg_ref)

        n, r = _rms_rows(x_ref[...])
        dh = dh_ref[...]
        dg_ref[...] += jnp.sum(dh * n, axis=0, keepdims=True)
        dx = dres_ref[...] + _rms_bwd_rows(dh * g_ref[...], n, r)
        dx_ref[...] = dx
        if want_bf16:
            rest[0][...] = dx.astype(BF16)

    row = pl.BlockSpec((tr, d), lambda i: (i, 0))
    vec = pl.BlockSpec((1, d), lambda i: (0, 0))
    out_specs = [row] + ([row] if want_bf16 else []) + [vec]
    out_shape = ([jax.ShapeDtypeStruct((t, d), F32)]
                 + ([jax.ShapeDtypeStruct((t, d), BF16)] if want_bf16 else [])
                 + [jax.ShapeDtypeStruct((1, d), F32)])
    return _carrier_call(
        body, (dh, x, dres, g), comm, name=name,
        grid=(t // tr,),
        in_specs=[row, row, row, vec],
        out_specs=out_specs, out_shape=out_shape,
        sem=("arbitrary",))


def _dact_call(dx3b, w_down, act, comm=None):
    t, d = dx3b.shape
    f = w_down.shape[0]
    tm = _tile(t, 1024)
    tn = _tile(f, 2048)

    def body(g_ref, w_ref, act_ref, o_ref):
        dact = _dot_nt(g_ref[...], w_ref[...])
        o_ref[...] = (dact * act_ref[...].astype(F32)).astype(BF16)

    return _carrier_call(
        body, (dx3b, w_down, act), comm, name="dact_bwd",
        grid=(t // tm, f // tn),
        in_specs=[pl.BlockSpec((tm, d), lambda i, j: (i, 0)),
                  pl.BlockSpec((tn, d), lambda i, j: (j, 0)),
                  pl.BlockSpec((tm, tn), lambda i, j: (i, j))],
        out_specs=pl.BlockSpec((tm, tn), lambda i, j: (i, j)),
        out_shape=jax.ShapeDtypeStruct((t, f), BF16),
        sem=("parallel", "parallel"))


def _wgrad_call(name, a, b, out_blocks, out_block_cols, comm=None, *, t1, t2=None, merge=1):
    t, k1 = a.shape
    k2 = b.shape[1]
    tt = _tile(t, 2048)
    t1 = _tile(k1, t1)
    t2 = _tile(k2, t2) if out_blocks is None else merge * out_block_cols
    nk = t // tt

    def body(a_ref, b_ref, o_ref, acc_ref):
        k = pl.program_id(2)

        @pl.when(k == 0)
        def _():
            acc_ref[...] = jnp.zeros_like(acc_ref)

        acc_ref[...] += _dot_tn(a_ref[...], b_ref[...])

        @pl.when(k == nk - 1)
        def _():
            if out_blocks is None:
                o_ref[...] = acc_ref[...].astype(BF16)
            else:
                for blk in range(merge):
                    o_ref[blk] = acc_ref[:, blk * out_block_cols:(blk + 1) * out_block_cols].astype(BF16)

    if out_blocks is None:
        out_spec = pl.BlockSpec((t1, t2), lambda i, j, k: (i, j))
        out_shape = jax.ShapeDtypeStruct((k1, k2), BF16)
    else:
        out_spec = pl.BlockSpec((merge, t1, out_block_cols), lambda i, j, k: (j, i, 0))
        out_shape = jax.ShapeDtypeStruct((out_blocks, k1, out_block_cols), BF16)
    return _carrier_call(
        body, (a, b), comm, name=name,
        grid=(k1 // t1, k2 // t2, nk),
        in_specs=[pl.BlockSpec((tt, t1), lambda i, j, k: (k, i)),
                  pl.BlockSpec((tt, t2), lambda i, j, k: (k, j))],
        out_specs=out_spec, out_shape=out_shape,
        scratch_shapes=[pltpu.VMEM((t1, t2), F32)],
        sem=("parallel", "parallel", "arbitrary"))


def _dgrad_blocked_call(name, g, w_g, comm=None, *, merge=1):
    t = g.shape[0]
    nb, d, cb = w_g.shape
    tm = _tile(t, 1024)
    tn = _tile(d, 2048)
    tk = merge * cb

    def body(g_ref, w_ref, o_ref):
        @pl.when(pl.program_id(2) == 0)
        def _():
            o_ref[...] = jnp.zeros_like(o_ref)

        w = w_ref[0] if merge == 1 else jnp.concatenate([w_ref[b] for b in range(merge)], axis=1)
        o_ref[...] += _dot_nt(g_ref[...], w)

    return _carrier_call(
        body, (g, w_g), comm, name=name,
        grid=(t // tm, d // tn, nb // merge),
        in_specs=[pl.BlockSpec((tm, tk), lambda i, j, k: (i, k)),
                  pl.BlockSpec((merge, tn, cb), lambda i, j, k: (k, j, 0))],
        out_specs=pl.BlockSpec((tm, tn), lambda i, j, k: (i, j)),
        out_shape=jax.ShapeDtypeStruct((t, d), F32),
        sem=("parallel", "parallel", "arbitrary"))


def _dmixed_call(dx2b, w_out, comm=None):
    t, d = dx2b.shape
    e = w_out.shape[0]
    tm = _tile(t, 1024)
    tn = _tile(e, 1024)

    def body(g_ref, w_ref, o_ref):
        o_ref[...] = _dot_nt(g_ref[...], w_ref[...])

    return _carrier_call(
        body, (dx2b, w_out), comm, name="dmixed_bwd",
        grid=(t // tm, e // tn),
        in_specs=[pl.BlockSpec((tm, d), lambda i, j: (i, 0)),
                  pl.BlockSpec((tn, d), lambda i, j: (j, 0))],
        out_specs=pl.BlockSpec((tm, tn), lambda i, j: (i, j)),
        out_shape=jax.ShapeDtypeStruct((t, e), F32),
        sem=("parallel", "parallel"))


def _mixer_bwd_call(proj, dmixed, w_s, bs_t, g_v, w_pool_g, pool_scale, comm=None):
    t = proj.shape[0]
    tt = _tile(t, 512)
    nchunk = tt // CHUNK
    hb = tt // HALO
    last_halo = t // HALO - 1
    nsteps = t // tt
    rb = GROUP // N_DEV

    def body(pu_ref, pv_ref, z_ref, zp_ref, da_ref, db_ref, dbn_ref, ws_ref, bs_ref, gv_ref, wp_ref, ps_ref,
             dproj_ref, dws_ref, dbs_ref, dgv_ref, dps_ref, dwp_ref):
        i = pl.program_id(0)

        @pl.when(i == 0)
        def _():
            dws_ref[...] = jnp.zeros_like(dws_ref)
            dbs_ref[...] = jnp.zeros_like(dbs_ref)
            dgv_ref[...] = jnp.zeros_like(dgv_ref)
            dps_ref[...] = jnp.zeros_like(dps_ref)
            dwp_ref[...] = jnp.zeros_like(dwp_ref)

        tril = (lax.broadcasted_iota(jnp.int32, (CHUNK, CHUNK), 0)
                >= lax.broadcasted_iota(jnp.int32, (CHUNK, CHUNK), 1))
        for h in range(N_HEADS):
            cols = slice(h * CHUNK, (h + 1) * CHUNK)
            v, dv_dpv = _gelu_and_grad(pv_ref[:, cols])
            vhat, rv = _rms_rows(v)
            gv = gv_ref[:, cols]
            vn = (vhat * gv).astype(BF16)
            u, du_dpu = _gelu_and_grad(pu_ref[:, cols])
            w = jnp.where(tril, ws_ref[h], 0.0).astype(BF16)
            bcol = bs_ref[:, h:h + 1]
            dout = da_ref[:, cols]
            dmix = dout * u
            dmix_b = dmix.astype(BF16)
            dws = jnp.zeros((CHUNK, CHUNK), F32)
            dbs = jnp.zeros((CHUNK, 1), F32)
            dvn_parts = []
            du_parts = []
            for c in range(nchunk):
                rows = slice(c * CHUNK, (c + 1) * CHUNK)
                mixed = _dot_nn(w, vn[rows]) + bcol
                du_parts.append(dout[rows] * mixed)
                dvn_parts.append(_dot_tn(w, dmix_b[rows]))
                dws = dws + _dot_nt(dmix_b[rows], vn[rows])
                dbs = dbs + jnp.sum(dmix[rows], axis=1, keepdims=True)
            dws_ref[h] += jnp.where(tril, dws, 0.0)
            dbs_ref[:, h:h + 1] += dbs
            dvn = jnp.concatenate(dvn_parts, axis=0)
            du = jnp.concatenate(du_parts, axis=0)
            dgv_ref[:, cols] += jnp.sum(dvn * vhat, axis=0, keepdims=True)
            dv = _rms_bwd_rows(dvn * gv, vhat, rv)
            dproj_ref[:, cols] = (du * du_dpu).astype(BF16)
            dproj_ref[:, A_WIDTH + h * CHUNK:A_WIDTH + (h + 1) * CHUNK] = (dv * dv_dpv).astype(BF16)

        zprev = jnp.where(i > 0, zp_ref[...], 0.0)
        ext = jnp.concatenate([zprev, z_ref[...]], axis=0)
        dnext = jnp.where(i < nsteps - 1, dbn_ref[...], 0.0)
        dext = jnp.concatenate([db_ref[...], dnext], axis=0)
        for g, win in enumerate(POOL_WINDOWS):
            cols = slice(g * GROUP, (g + 1) * GROUP)
            zg = ext[:, cols]
            pooled = _window_sum_back(zg, win)[HALO:] / _pool_counts(i * tt, tt, win) - zg[HALO:]
            pooled_b = pooled.astype(BF16)
            wp = wp_ref[:, g].reshape(GROUP, GROUP)
            y = _dot_nn(pooled_b, wp)
            dout = dext[:, cols]
            dps_ref[:, cols] += jnp.sum(dout[:tt] * y, axis=0, keepdims=True)
            dy_b = (dout * ps_ref[:, cols]).astype(BF16)
            dwp_ref[:, g] += _dot_tn(pooled_b, dy_b[:tt]).reshape(N_DEV, rb, GROUP)
            dpooled = _dot_nt(dy_b, wp)
            q = dpooled / _pool_counts(i * tt, tt + HALO, win)
            dz = _window_sum_fwd(q, win)[:tt] - dpooled[:tt]
            dproj_ref[:, 2 * A_WIDTH + g * GROUP:2 * A_WIDTH + (g + 1) * GROUP] = dz.astype(BF16)

    def full(shape):
        return pl.BlockSpec(shape, lambda i: (0,) * len(shape))

    return _carrier_call(
        body, (proj, proj, proj, proj, dmixed, dmixed, dmixed, w_s, bs_t, g_v, w_pool_g, pool_scale), comm,
        name="mixer_bwd",
        grid=(nsteps,),
        in_specs=[pl.BlockSpec((tt, A_WIDTH), lambda i: (i, 0)),
                  pl.BlockSpec((tt, A_WIDTH), lambda i: (i, 1)),
                  pl.BlockSpec((tt, B_WIDTH), lambda i: (i, 2)),
                  pl.BlockSpec((HALO, B_WIDTH), lambda i: (jnp.maximum(i * hb - 1, 0), 2)),
                  pl.BlockSpec((tt, A_WIDTH), lambda i: (i, 0)),
                  pl.BlockSpec((tt, B_WIDTH), lambda i: (i, 1)),
                  pl.BlockSpec((HALO, B_WIDTH), lambda i: (jnp.minimum((i + 1) * hb, last_halo), 1)),
                  full((N_HEADS, CHUNK, CHUNK)), full((CHUNK, N_HEADS)), full((1, A_WIDTH)),
                  full((N_DEV, 4, rb, GROUP)), full((1, B_WIDTH))],
        out_specs=[pl.BlockSpec((tt, 2 * A_WIDTH + B_WIDTH), lambda i: (i, 0)),
                   full((N_HEADS, CHUNK, CHUNK)), full((CHUNK, N_HEADS)), full((1, A_WIDTH)),
                   full((1, B_WIDTH)), full((N_DEV, 4, rb, GROUP))],
        out_shape=[jax.ShapeDtypeStruct((t, 2 * A_WIDTH + B_WIDTH), BF16),
                   jax.ShapeDtypeStruct((N_HEADS, CHUNK, CHUNK), F32),
                   jax.ShapeDtypeStruct((CHUNK, N_HEADS), F32),
                   jax.ShapeDtypeStruct((1, A_WIDTH), F32),
                   jax.ShapeDtypeStruct((1, B_WIDTH), F32),
                   jax.ShapeDtypeStruct((N_DEV, 4, rb, GROUP), F32)],
        sem=("arbitrary",))


def _adamw(w, g, m, v):
    m = ADAM_B1 * m + (1.0 - ADAM_B1) * g
    v = ADAM_B2 * v + (1.0 - ADAM_B2) * (g * g)
    m_hat = m / ADAM_C1
    v_hat = v / ADAM_C2
    delta = -ADAM_LR * (m_hat / (jnp.sqrt(v_hat) + ADAM_EPS) + ADAM_WD * w)
    return delta, m, v


PAIR_SUM_TILE_ELEMS = 1024 * 1024
ADAMW_TILE_ELEMS = 512 * 1024


def _row_tile(r, c, elems):
    t = r
    while t * c > elems and t % 32 == 0:
        t //= 2
    return t


def _pair_sum_call(name, pos, grad, got):
    _, r, c = grad.shape
    tr = _row_tile(r, c, PAIR_SUM_TILE_ELEMS)

    def chip_of(rel, pos_ref):
        px = jnp.where((rel == 0) | (rel == 2), 1 - pos_ref[0], pos_ref[0])
        py = jnp.where((rel == 1) | (rel == 2), 1 - pos_ref[1], pos_ref[1])
        return 2 * px + py

    def body(pos_ref, own_ref, got_ref, out_ref):
        out_ref[...] = (own_ref[...].astype(F32) + got_ref[...].astype(F32)).astype(BF16)

    return pl.pallas_call(
        body, name=name,
        grid_spec=pltpu.PrefetchScalarGridSpec(
            num_scalar_prefetch=1, grid=(3, r // tr),
            in_specs=[pl.BlockSpec((None, tr, c), lambda k, i, p: (2 * chip_of(k, p) + p[2], i, 0)),
                      pl.BlockSpec((None, tr, c), lambda k, i, p: (chip_of(k, p), i, 0))],
            out_specs=pl.BlockSpec((None, tr, c), lambda k, i, p: (k, i, 0))),
        out_shape=jax.ShapeDtypeStruct((3, r, c), BF16),
        compiler_params=_params("parallel", "parallel"),
    )(pos, grad, got)


def _final_call(name, pos, grad, got_pair, got_chips, w, m, v):
    _, r, c = grad.shape
    tr = _row_tile(r, c, ADAMW_TILE_ELEMS)

    def body(pos_ref, own_ref, pair_ref, chips_ref, w_ref, m_ref, v_ref, g_out, d_out, m_out, v_out):
        g = own_ref[...].astype(F32) + pair_ref[...].astype(F32)
        for j in range(3):
            g = g + chips_ref[j].astype(F32)
        delta, m_new, v_new = _adamw(w_ref[...], g, m_ref[...], v_ref[...])
        g_out[...] = g
        d_out[...] = delta
        m_out[...] = m_new
        v_out[...] = v_new

    row = pl.BlockSpec((tr, c), lambda i, p: (i, 0))
    return pl.pallas_call(
        body, name=name,
        grid_spec=pltpu.PrefetchScalarGridSpec(
            num_scalar_prefetch=1, grid=(r // tr,),
            in_specs=[pl.BlockSpec((None, tr, c), lambda i, p: (4 * p[0] + 2 * p[1] + p[2], i, 0)),
                      pl.BlockSpec((None, tr, c), lambda i, p: (2 * p[0] + p[1], i, 0)),
                      pl.BlockSpec((3, tr, c), lambda i, p: (0, i, 0)), row, row, row],
            out_specs=[row] * 4),
        out_shape=[jax.ShapeDtypeStruct((r, c), F32)] * 4,
        compiler_params=_params("parallel"),
    )(pos, grad, got_pair, got_chips, w, m, v)


def _small_final_call(name, parts, w, m, v):
    _, rows, c = parts.shape
    r = w.shape[0]

    def body(p_ref, w_ref, m_ref, v_ref, g_out, d_out, m_out, v_out):
        g = p_ref[0]
        for k in range(1, N_DEV):
            g = g + p_ref[k]
        delta, m_new, v_new = _adamw(w_ref[...], g[:r], m_ref[...], v_ref[...])
        g_out[...] = g
        d_out[...] = delta
        m_out[...] = m_new
        v_out[...] = v_new

    return pl.pallas_call(
        body, name=name,
        out_shape=[jax.ShapeDtypeStruct((rows, c), F32)] + [jax.ShapeDtypeStruct((r, c), F32)] * 3,
        compiler_params=pltpu.CompilerParams(vmem_limit_bytes=VMEM_LIMIT),
    )(parts, w, m, v)


_SMALL_EARLY = ("g_v", "w_s", "b_s", "pool_scale", "g_ffn", "g_final")
_BIG = ("w_in", "w_pool", "w_out", "w_up", "w_down")
_ORDER = ("g_mix", "w_in", "g_v", "w_s", "b_s", "w_pool", "pool_scale", "w_out", "g_ffn", "w_up", "w_down", "g_final")


def _pack(parts):
    return jnp.concatenate([p.reshape(-1, LANES) for p in parts], axis=0)


def _unpack(packed, like):
    out, row = [], 0
    for a in like:
        rows = a.size // LANES
        out.append(packed[row:row + rows].reshape(a.shape))
        row += rows
    return out


def kernel(x, g_mix, w_in, g_v, w_s, b_s, w_pool, pool_scale, w_out, g_ffn, w_up, w_down, g_final, loss_target, m_g_mix, m_w_in, m_g_v, m_w_s, m_b_s, m_w_pool, m_pool_scale, m_w_out, m_g_ffn, m_w_up, m_w_down, m_g_final, v_g_mix, v_w_in, v_g_v, v_w_s, v_b_s, v_w_pool, v_pool_scale, v_w_out, v_g_ffn, v_w_up, v_w_down, v_g_final):
    weights = dict(g_mix=g_mix, w_in=w_in, g_v=g_v, w_s=w_s, b_s=b_s, w_pool=w_pool, pool_scale=pool_scale,
                   w_out=w_out, g_ffn=g_ffn, w_up=w_up, w_down=w_down, g_final=g_final)
    mom = dict(g_mix=m_g_mix, w_in=m_w_in, g_v=m_g_v, w_s=m_w_s, b_s=m_b_s, w_pool=m_w_pool,
               pool_scale=m_pool_scale, w_out=m_w_out, g_ffn=m_g_ffn, w_up=m_w_up, w_down=m_w_down,
               g_final=m_g_final)
    var = dict(g_mix=v_g_mix, w_in=v_w_in, g_v=v_g_v, w_s=v_w_s, b_s=v_b_s, w_pool=v_w_pool,
               pool_scale=v_pool_scale, w_out=v_w_out, g_ffn=v_g_ffn, w_up=v_w_up, w_down=v_w_down,
               g_final=v_g_final)

    t, d = x.shape[1], x.shape[2]
    xs = x.reshape(t, d)
    target = loss_target.reshape(t, d)

    shard2d = dict(w_in=w_in.reshape(d, -1), w_pool=w_pool.reshape(-1, GROUP), w_out=w_out.reshape(-1, d),
                   w_up=w_up.reshape(d, -1), w_down=w_down.reshape(-1, d))
    sb = {k: shard2d[k].astype(BF16) for k in _BIG}
    rows = {k: sb[k].shape[0] for k in _BIG}

    def gathered_shape(k):
        return jax.ShapeDtypeStruct((N_DEV,) + sb[k].shape, BF16)

    def landing(n, like):
        return jax.ShapeDtypeStruct((n,) + like.shape[1:], like.dtype)

    def from_everyone(block):
        return jax.ShapeDtypeStruct((N_DEV,) + block.shape, block.dtype)

    def cuts(r, fractions):
        return [0] + [int(r * f) // 16 * 16 for f in fractions] + [r]

    g_mix2, g_ffn2, g_final2 = g_mix.reshape(1, d), g_ffn.reshape(1, d), g_final.reshape(1, d)
    g_v2, ps2 = g_v.reshape(1, A_WIDTH), pool_scale.reshape(1, B_WIDTH)
    w_s3 = w_s.reshape(N_HEADS, CHUNK, CHUNK)
    bs_t = b_s.reshape(N_HEADS, CHUNK).T
    xi, yi, ci = _position()
    pos = jnp.stack([xi, yi, ci]).astype(jnp.int32)

    order = (4 * xi + 2 * yi + ci) ^ jnp.array(ARRIVAL_ORDER, jnp.int32)
    early_up = cuts(rows["w_up"], (0.3,))[1]
    ahead = cuts(rows["w_down"], (0.45,))[1]
    proj, h1, w_in_g, w_out_g, w_pool_g = _norm_matmul_stream_call(
        "proj_fwd", xs, g_mix2, sb["w_in"], order, None, 0, 0, _Comm(
            [sb["w_out"], sb["w_pool"]], [], [gathered_shape("w_out"), gathered_shape("w_pool")],
            lambda s, l: [_gather_to_neighbours(s[0], l[0], 0, rows["w_out"]) + _everyone(s[1], l[1])]),
        lambda a: (a,), (F32,))
    w_pool_g = w_pool_g.reshape(N_DEV, 4, GROUP // N_DEV, GROUP)
    mixed, w_out_g, w_up_g = _mixer_fwd_call(proj, w_s3, bs_t, g_v2, w_pool_g, ps2, _Comm(
        [sb["w_up"]], [w_out_g], [gathered_shape("w_up")],
        lambda s, l: [_gather_relay(l[0], 0, rows["w_out"]) + _gather_to_neighbours(s[0], l[1], 0, early_up),
                      _gather_diagonal_pass_on(l[0], 0, rows["w_out"])]))
    w_out_f = w_out_g.reshape(-1, d)
    x2, w_up_g = _out_proj_call(mixed, w_out_f, xs, _Comm(
        [], [w_up_g], [],
        lambda s, l: [_gather_relay(l[0], 0, early_up), _gather_diagonal_pass_on(l[0], 0, early_up)]))

    def relu2_and_slope(a):
        r = jnp.maximum(a, 0.0)
        return r * r, 2.0 * r

    act, dact_da, h2, w_up_g, w_down_g = _norm_matmul_stream_call(
        "up_fwd", x2, g_ffn2, sb["w_up"], order, w_up_g, early_up, early_up, _Comm(
            [sb["w_down"]], [], [gathered_shape("w_down")],
            lambda s, l: [_gather_to_neighbours(s[0], l[0], 0, ahead)]),
        relu2_and_slope, (BF16, BF16))
    y, w_down_g = _down_call(act, sb["w_down"], order, w_down_g, ahead)
    w_down_f = w_down_g.reshape(-1, d)
    loss_part, dx3, dx3b, dg_final = _loss_call(y, x2, target, g_final2)

    def pair_sum(k, grad, got):
        return _pair_sum_call(k + "_pair_sum", pos, grad, got)

    def finish(k, grad, got_pair, got_chips):
        s = shard2d[k]
        outs = _final_call(k + "_adamw", pos, grad, got_pair, got_chips, s, mom[k].reshape(s.shape),
                           var[k].reshape(s.shape))
        return [o.reshape(weights[k].shape) for o in outs]

    result = {}
    (gw_down,) = _wgrad_call("w_down_grad", act, dx3b, None, None, t1=1024, t2=2048)
    gw_down = gw_down.reshape(N_DEV, -1, d)
    da, pair_down = _dact_call(dx3b, w_down_f, dact_da, _Comm(
        [gw_down], [], [landing(4, gw_down)], lambda s, l: [_pair_exchange(s[0], l[0])]))
    sums_down = pair_sum("w_down", gw_down, pair_down)
    dn = cuts(rows["w_down"], (0.8,))
    gw_up, got = _wgrad_call("w_up_grad", h2, da, N_DEV, w_up_g.shape[2], _Comm(
        [sums_down], [], [landing(3, sums_down)],
        lambda s, l: [_chip_exchange(s[0], l[0], dn[0], dn[1])]), t1=2048)
    dh2, got, pair_up = _dgrad_blocked_call("dh2_bwd", da, w_up_g, _Comm(
        [sums_down, gw_up], [got], [landing(4, gw_up)],
        lambda s, l: [_chip_exchange(s[0], l[0], dn[1], dn[2]) + _pair_exchange(s[1], l[1])]), merge=2)
    result["w_down"] = finish("w_down", gw_down, pair_down, got)
    sums_up = pair_sum("w_up", gw_up, pair_up)
    v = cuts(rows["w_up"], (0.28, 0.52, 0.84))
    dx2, dx2b, dg_ffn, got_up = _norm_bwd_call("ffn_norm_bwd", dh2, x2, dx3, g_ffn2, True, _Comm(
        [sums_up], [], [landing(3, sums_up)], lambda s, l: [_chip_exchange(s[0], l[0], v[0], v[1])]))
    dmixed, got_up = _dmixed_call(dx2b, w_out_f, _Comm(
        [sums_up], [got_up], [], lambda s, l: [_chip_exchange(s[0], l[0], v[1], v[2])]))
    gw_out, got_up = _wgrad_call("w_out_grad", mixed, dx2b, None, None, _Comm(
        [sums_up], [got_up], [], lambda s, l: [_chip_exchange(s[0], l[0], v[2], v[3])]), t1=2048, t2=1024)
    gw_out = gw_out.reshape(N_DEV, -1, d)
    dproj, dw_s, dbs_t, dg_v, dps, dw_pool, got_up, pair_out = _mixer_bwd_call(
        proj, dmixed, w_s3, bs_t, g_v2, w_pool_g, ps2, _Comm(
            [sums_up, gw_out], [got_up], [landing(4, gw_out)],
            lambda s, l: [_chip_exchange(s[0], l[0], v[3], v[4]) + _pair_exchange(s[1], l[1])]))
    result["w_up"] = finish("w_up", gw_up, pair_up, got_up)
    sums_out = pair_sum("w_out", gw_out, pair_out)
    gw_pool = dw_pool.astype(BF16).reshape(N_DEV, -1, GROUP)
    early = dict(g_v=dg_v, w_s=dw_s, b_s=dbs_t.T, pool_scale=dps, g_ffn=dg_ffn, g_final=dg_final)
    packed = _pack([early[k] for k in _SMALL_EARLY] + [loss_part])
    early_rows = packed.shape[0]
    gw_in, got, pair_pool, parts_early = _wgrad_call("w_in_grad", h1, dproj, N_DEV, w_in_g.shape[2], _Comm(
        [sums_out, gw_pool, packed], [], [landing(3, sums_out), landing(4, gw_pool), from_everyone(packed)],
        lambda s, l: [_chip_exchange(s[0], l[0], 0, rows["w_out"]) + _pair_exchange(s[1], l[1])
                      + _gather_first(s[2], l[2], 0, early_rows)]), t1=2048, merge=MERGE_W_IN)
    result["w_out"] = finish("w_out", gw_out, pair_out, got)
    sums_pool = pair_sum("w_pool", gw_pool, pair_pool)
    (pair_in,) = _comm_call("pair_exchange_w_in", _Comm(
        [gw_in], [], [landing(4, gw_in)], lambda s, l: [_pair_exchange(s[0], l[0])]))
    sums_in = pair_sum("w_in", gw_in, pair_in)
    dh1, parts_early, got, got_pool = _dgrad_blocked_call("dh1_bwd", dproj, w_in_g, _Comm(
        [sums_in, sums_pool], [parts_early], [landing(3, sums_in), landing(3, sums_pool)],
        lambda s, l: [_chip_exchange(s[0], l[1], 0, rows["w_in"]) + _chip_exchange(s[1], l[2], 0, rows["w_pool"])
                      + _gather_pass_on(l[0], 0, early_rows)]),
        merge=MERGE_W_IN)
    result["w_in"] = finish("w_in", gw_in, pair_in, got)
    result["w_pool"] = finish("w_pool", gw_pool, pair_pool, got_pool)
    grad_x, dg_mix = _norm_bwd_call("mix_norm_bwd", dh1, xs, dx2, g_mix2, False)
    packed = _pack([dg_mix])
    (parts_late,) = _comm_call("gather_g_mix_grad", _Comm(
        [packed], [], [from_everyone(packed)], lambda s, l: [_everyone(s[0], l[0])]))

    for names, parts, tag in ((_SMALL_EARLY, parts_early, "small_adamw"), (("g_mix",), parts_late, "g_mix_adamw")):
        outs = _small_final_call(tag, parts, _pack([weights[k] for k in names]), _pack([mom[k] for k in names]),
                                 _pack([var[k] for k in names]))
        if tag == "small_adamw":
            loss = outs[0][-1, 0]
        like = [weights[k] for k in names]
        unpacked = [_unpack(o, like) for o in outs]
        for idx, k in enumerate(names):
            result[k] = [unpacked[q][idx] for q in range(4)]

    grads = [result[k][0] for k in _ORDER]
    deltas = [result[k][1] for k in _ORDER]
    new_m = [result[k][2] for k in _ORDER]
    new_v = [result[k][3] for k in _ORDER]
    return (loss, grad_x.reshape(x.shape), *grads, *deltas, *new_m, *new_v)
```

```python
import functools
import math

import jax
import jax.numpy as jnp
from jax import lax
from jax.experimental import pallas as pl
from jax.experimental.pallas import tpu as pltpu

F32 = jnp.float32
BF16 = jnp.bfloat16
MESH = pl.DeviceIdType.MESH

N_DEV = 8
EPS = 1e-6
CHUNK = 128
N_HEADS = 8
A_WIDTH = 1024
B_WIDTH = 1024
POOL_WINDOWS = (2, 4, 8, 16)
GROUP = 256
HALO = 16
LANES = 128

ADAM_LR = 0.001
ADAM_B1 = 0.9
ADAM_B2 = 0.999
ADAM_EPS = 1e-08
ADAM_WD = 0.01
ADAM_STEP = 10
ADAM_C1 = 1.0 - ADAM_B1 ** ADAM_STEP
ADAM_C2 = 1.0 - ADAM_B2 ** ADAM_STEP

VMEM_LIMIT = 56 * 1024 * 1024
MERGE_W_IN = 2

_GELU_C = math.sqrt(2.0 / math.pi)


def _params(*sem):
    return pltpu.CompilerParams(dimension_semantics=sem, vmem_limit_bytes=VMEM_LIMIT)


def _gelu(x):
    return 0.5 * x * (1.0 + jnp.tanh(_GELU_C * (x + 0.044715 * x * x * x)))


def _gelu_and_grad(x):
    t = jnp.tanh(_GELU_C * (x + 0.044715 * x * x * x))
    g = 0.5 * x * (1.0 + t)
    dg = 0.5 * (1.0 + t) + 0.5 * x * (1.0 - t * t) * (_GELU_C * (1.0 + 3.0 * 0.044715 * x * x))
    return g, dg


def _dot_nn(a, b):
    return lax.dot_general(a, b, (((1,), (0,)), ((), ())), preferred_element_type=F32)


def _dot_nt(a, b):
    return lax.dot_general(a, b, (((1,), (1,)), ((), ())), preferred_element_type=F32)


def _dot_tn(a, b):
    return lax.dot_general(a, b, (((0,), (0,)), ((), ())), preferred_element_type=F32)


def _rms_rows(x):
    r = lax.rsqrt(jnp.mean(x * x, axis=-1, keepdims=True) + EPS)
    return x * r, r


def _rms_bwd_rows(dn, n, r):
    return r * (dn - n * jnp.mean(dn * n, axis=-1, keepdims=True))


def _tile(n, want):
    t = min(n, want)
    assert n % t == 0, (n, want)
    return t


_ANY = pl.BlockSpec(memory_space=pl.ANY)

SIBLING = 1
CHIPS = (4, 2, 6)


def _position():
    return lax.axis_index("x"), lax.axis_index("y"), lax.axis_index("c")


def _me():
    x, y, c = _position()
    return 4 * x + 2 * y + c


def _peer(rel):
    x, y, c = _position()
    return (x ^ ((rel >> 2) & 1), y ^ ((rel >> 1) & 1), c ^ (rel & 1))


class _Comm:
    def __init__(self, srcs, lands, new, plan):
        self.srcs, self.lands, self.new, self.plan = list(srcs), list(lands), list(new), plan


def _make_copies(phases, send_sems, recv_sems, local_sems):
    out, nr, nl = [], 0, 0
    for phase in phases:
        cps = []
        for item in phase:
            if item[0] == "local":
                cps.append(pltpu.make_async_copy(item[1], item[2], local_sems.at[nl]))
                nl += 1
            else:
                cps.append(pltpu.make_async_remote_copy(
                    src_ref=item[1], dst_ref=item[2], send_sem=send_sems.at[nr], recv_sem=recv_sems.at[nr],
                    device_id=_peer(item[3]), device_id_type=MESH))
                nr += 1
        out.append(cps)
    return out


def _count_copies(comm):
    phases = comm.plan([_FakeRef() for _ in comm.srcs], [_FakeRef() for _ in range(len(comm.lands) + len(comm.new))])
    items = [it for ph in phases for it in ph]
    return sum(it[0] == "remote" for it in items), sum(it[0] == "local" for it in items)


class _FakeRef:
    def __getitem__(self, idx):
        return self

    @property
    def at(self):
        return self


def _carrier_call(body, args, comm, *, name, grid, in_specs, out_specs, out_shape, scratch_shapes=(), sem):
    if not isinstance(out_shape, (list, tuple)):
        out_specs, out_shape = [out_specs], [out_shape]
    out_specs, out_shape, scratch_shapes = list(out_specs), list(out_shape), list(scratch_shapes)
    if comm is None:
        res = pl.pallas_call(body, name=name, grid=grid, in_specs=list(in_specs), out_specs=out_specs,
                             out_shape=out_shape, scratch_shapes=scratch_shapes, compiler_params=_params(*sem))(*args)
        return list(res)
    n_in, n_out, n_scr = len(args), len(out_shape), len(scratch_shapes)
    ns, nl, nn = len(comm.srcs), len(comm.lands), len(comm.new)
    n_remote, n_local = _count_copies(comm)
    steps = math.prod(grid)

    def wrapped(*refs):
        ins, srcs = refs[:n_in], refs[n_in:n_in + ns]
        o = n_in + ns + nl
        outs, lands = refs[o:o + n_out], refs[o + n_out:o + n_out + nl + nn]
        scr = refs[o + n_out + nl + nn:]
        phases = _make_copies(comm.plan(srcs, lands), *scr[n_scr:])
        assert len(phases) == 1 or (len(phases) == 2 and steps >= 3)
        step = functools.reduce(lambda acc, a: acc * grid[a] + pl.program_id(a), range(len(grid)), 0)

        @pl.when(step == 0)
        def _():
            for cp in phases[0]:
                cp.start()

        if len(phases) == 2:
            @pl.when(step == steps * 3 // 4)
            def _():
                for cp in phases[0]:
                    cp.wait()
                for cp in phases[1]:
                    cp.start()

        body(*ins, *outs, *scr[:n_scr])

        @pl.when(step == steps - 1)
        def _():
            for cp in phases[-1]:
                cp.wait()

    land_shapes = [jax.ShapeDtypeStruct(a.shape, a.dtype) for a in comm.lands] + comm.new
    sems = [pltpu.SemaphoreType.DMA((max(n_remote, 1),)), pltpu.SemaphoreType.DMA((max(n_remote, 1),)),
            pltpu.SemaphoreType.DMA((max(n_local, 1),))]
    res = pl.pallas_call(
        wrapped, name=name, grid=grid,
        in_specs=list(in_specs) + [_ANY] * (ns + nl), out_specs=out_specs + [_ANY] * (nl + nn),
        out_shape=out_shape + land_shapes, scratch_shapes=scratch_shapes + sems,
        input_output_aliases={n_in + ns + k: n_out + k for k in range(nl)},
        compiler_params=_params(*sem))(*args, *comm.srcs, *comm.lands)
    return list(res)


def _comm_call(name, comm):
    ns, nl, nn = len(comm.srcs), len(comm.lands), len(comm.new)
    n_remote, n_local = _count_copies(comm)

    def body(*refs):
        srcs, lands, sems = refs[:ns], refs[ns + nl:ns + nl + nl + nn], refs[ns + nl + nl + nn:]
        for copies in _make_copies(comm.plan(srcs, lands), *sems):
            for cp in copies:
                cp.start()
            for cp in copies:
                cp.wait()

    land_shapes = [jax.ShapeDtypeStruct(a.shape, a.dtype) for a in comm.lands] + comm.new
    res = pl.pallas_call(
        body, name=name,
        in_specs=[_ANY] * (ns + nl), out_specs=[_ANY] * (nl + nn), out_shape=land_shapes,
        scratch_shapes=[pltpu.SemaphoreType.DMA((max(n_remote, 1),)), pltpu.SemaphoreType.DMA((max(n_remote, 1),)),
                        pltpu.SemaphoreType.DMA((max(n_local, 1),))],
        input_output_aliases={ns + k: k for k in range(nl)},
    )(*comm.srcs, *comm.lands)
    return list(res)


def _rows(ref, block, r0, r1):
    return ref.at[block, pl.ds(r0, r1 - r0)]


def _gather_first(shard, land, r0, r1):
    src = shard.at[pl.ds(r0, r1 - r0)]
    dst = _rows(land, _me(), r0, r1)
    return [("local", src, dst)] + [("remote", src, dst, rel) for rel in (SIBLING,) + CHIPS]


def _gather_pass_on(land, r0, r1):
    return [("remote", _rows(land, _me() ^ rel, r0, r1), _rows(land, _me() ^ rel, r0, r1), SIBLING) for rel in CHIPS]


def _split_rows(r0, r1):
    m = (r0 + r1) // 2 // 16 * 16
    return (r0, m), (m, r1)


def _gather_to_neighbours(shard, land, r0, r1):
    src = shard.at[pl.ds(r0, r1 - r0)]
    dst = _rows(land, _me(), r0, r1)
    return [("local", src, dst)] + [("remote", src, dst, rel) for rel in (SIBLING, 4, 2)]


def _gather_relay(land, r0, r1):
    lo, hi = _split_rows(r0, r1)
    x_block, y_block = _me() ^ 4, _me() ^ 2
    return [("remote", _rows(land, x_block, *lo), _rows(land, x_block, *lo), 2),
            ("remote", _rows(land, y_block, *hi), _rows(land, y_block, *hi), 4),
            ("remote", _rows(land, x_block, r0, r1), _rows(land, x_block, r0, r1), SIBLING),
            ("remote", _rows(land, y_block, r0, r1), _rows(land, y_block, r0, r1), SIBLING)]


def _gather_diagonal_pass_on(land, r0, r1):
    rows = _rows(land, _me() ^ 6, r0, r1)
    return [("remote", rows, rows, SIBLING)]


def _pair_exchange(grad, land):
    _, _, c = _position()
    return [("remote", grad.at[2 * chip + (1 - c)], land.at[chip], SIBLING) for chip in range(4)]


def _chip_exchange(sums, land, r0, r1):
    return [("remote", _rows(sums, j, r0, r1), _rows(land, j, r0, r1), rel) for j, rel in enumerate(CHIPS)]


def _everyone(packed, land):
    dst = land.at[_me()]
    return [("local", packed, dst)] + [("remote", packed, dst, rel) for rel in range(1, N_DEV)]


def _pool_counts(row0, rows, win):
    pos = row0 + lax.broadcasted_iota(jnp.int32, (rows, 1), 0)
    return jnp.minimum(pos + 1, win).astype(F32)


def _window_sum_back(ext, win):
    s = ext
    k = 1
    while k < win:
        s = s + pltpu.roll(s, k, 0)
        k *= 2
    return s


def _window_sum_fwd(ext, win):
    n = ext.shape[0]
    s = ext
    k = 1
    while k < win:
        s = s + pltpu.roll(s, n - k, 0)
        k *= 2
    return s


def _mixer_fwd_call(proj, w_s, bs_t, g_v, w_pool_g, pool_scale, comm=None):
    t = proj.shape[0]
    tt = _tile(t, 512)
    nchunk = tt // CHUNK
    hb = tt // HALO

    def body(pu_ref, pv_ref, z_ref, zp_ref, ws_ref, bs_ref, gv_ref, wp_ref, ps_ref, out_ref):
        i = pl.program_id(0)
        tril = (lax.broadcasted_iota(jnp.int32, (CHUNK, CHUNK), 0)
                >= lax.broadcasted_iota(jnp.int32, (CHUNK, CHUNK), 1))
        for h in range(N_HEADS):
            cols = slice(h * CHUNK, (h + 1) * CHUNK)
            vhat, _ = _rms_rows(_gelu(pv_ref[:, cols]))
            vn = (vhat * gv_ref[:, cols]).astype(BF16)
            u = _gelu(pu_ref[:, cols])
            w = jnp.where(tril, ws_ref[h], 0.0).astype(BF16)
            bcol = bs_ref[:, h:h + 1]
            for c in range(nchunk):
                rows = slice(c * CHUNK, (c + 1) * CHUNK)
                mixed = _dot_nn(w, vn[rows]) + bcol
                out_ref[rows, cols] = (u[rows] * mixed).astype(BF16)

        zprev = jnp.where(i > 0, zp_ref[...], 0.0)
        ext = jnp.concatenate([zprev, z_ref[...]], axis=0)
        for g, win in enumerate(POOL_WINDOWS):
            cols = slice(g * GROUP, (g + 1) * GROUP)
            zg = ext[:, cols]
            s = _window_sum_back(zg, win)
            pooled = s[HALO:] / _pool_counts(i * tt, tt, win) - zg[HALO:]
            wp = wp_ref[:, g].reshape(GROUP, GROUP)
            y = _dot_nn(pooled.astype(BF16), wp)
            out_ref[:, A_WIDTH + g * GROUP:A_WIDTH + (g + 1) * GROUP] = (y * ps_ref[:, cols]).astype(BF16)

    return _carrier_call(
        body, (proj, proj, proj, proj, w_s, bs_t, g_v, w_pool_g, pool_scale), comm, name="mixer_fwd",
        grid=(t // tt,),
        in_specs=[pl.BlockSpec((tt, A_WIDTH), lambda i: (i, 0)),
                  pl.BlockSpec((tt, A_WIDTH), lambda i: (i, 1)),
                  pl.BlockSpec((tt, B_WIDTH), lambda i: (i, 2)),
                  pl.BlockSpec((HALO, B_WIDTH), lambda i: (jnp.maximum(i * hb - 1, 0), 2)),
                  pl.BlockSpec((N_HEADS, CHUNK, CHUNK), lambda i: (0, 0, 0)),
                  pl.BlockSpec((CHUNK, N_HEADS), lambda i: (0, 0)),
                  pl.BlockSpec((1, A_WIDTH), lambda i: (0, 0)),
                  pl.BlockSpec((N_DEV, 4, GROUP // N_DEV, GROUP), lambda i: (0, 0, 0, 0)),
                  pl.BlockSpec((1, B_WIDTH), lambda i: (0, 0))],
        out_specs=pl.BlockSpec((tt, A_WIDTH + B_WIDTH), lambda i: (i, 0)),
        out_shape=jax.ShapeDtypeStruct((t, A_WIDTH + B_WIDTH), BF16),
        sem=("parallel",))


def _out_proj_call(mixed, w_out, x, comm=None):
    t, d = x.shape
    k = mixed.shape[1]
    tm = _tile(t, 1024)
    tn = _tile(d, 1024)

    def body(a_ref, w_ref, x_ref, o_ref):
        o_ref[...] = x_ref[...] + _dot_nn(a_ref[...], w_ref[...])

    return _carrier_call(
        body, (mixed, w_out, x), comm, name="out_proj_fwd",
        grid=(t // tm, d // tn),
        in_specs=[pl.BlockSpec((tm, k), lambda i, j: (i, 0)),
                  pl.BlockSpec((k, tn), lambda i, j: (0, j)),
                  pl.BlockSpec((tm, tn), lambda i, j: (i, j))],
        out_specs=pl.BlockSpec((tm, tn), lambda i, j: (i, j)),
        out_shape=jax.ShapeDtypeStruct((t, d), F32),
        sem=("parallel", "parallel"))


ARRIVAL_ORDER = (0, 1, 4, 5, 2, 3, 6, 7)
CARRIED_AFTER = 3


class _StreamedGather:
    def __init__(self, shard_ref, land_ref, wbuf, pre0, r0, send_sems, recv_sems, local_sem, fetch_sems):
        self.shard, self.land, self.wbuf, self.fetch_sems = shard_ref, land_ref, wbuf, fetch_sems
        end = shard_ref.shape[0]
        me = _me()
        self.me = me

        def remote(k, src, dst, rel):
            return pltpu.make_async_remote_copy(src_ref=src, dst_ref=dst, send_sem=send_sems.at[k],
                                                recv_sem=recv_sems.at[k], device_id=_peer(rel), device_id_type=MESH)

        def same_rows(k, block, a, b, rel):
            ref = land_ref.at[block, pl.ds(a, b - a)]
            return remote(k, ref, ref, rel)

        src = shard_ref.at[pl.ds(r0, end - r0)]
        dst = land_ref.at[me, pl.ds(r0, end - r0)]
        self.mine = pltpu.make_async_copy(src, dst, local_sem)
        self.first = [remote(k, src, dst, rel) for k, rel in enumerate((SIBLING, 4, 2))]
        lo, hi = _split_rows(r0, end)
        self.relay = [same_rows(3, me ^ 4, *lo, 2), same_rows(4, me ^ 2, *hi, 4)]
        self.passed = [same_rows(5, me ^ 4, r0, end, SIBLING), same_rows(6, me ^ 2, r0, end, SIBLING),
                       same_rows(7, me ^ 6, pre0, end, SIBLING)]
        self.early_relay, self.early_passed = [], []
        if r0 > pre0:
            lo, hi = _split_rows(pre0, r0)
            self.early_relay = [same_rows(8, me ^ 4, *lo, 2), same_rows(9, me ^ 2, *hi, 4)]
            self.early_passed = [same_rows(10, me ^ 4, pre0, r0, SIBLING), same_rows(11, me ^ 2, pre0, r0, SIBLING)]

    def _fetch(self, q):
        src = self.shard if q == 0 else self.land.at[self.me ^ ARRIVAL_ORDER[q]]
        return pltpu.make_async_copy(src, self.wbuf.at[q % 2], self.fetch_sems.at[q % 2])

    def start(self):
        self.mine.start()
        for cp in self.first + self.early_relay + self.early_passed:
            cp.start()
        self._fetch(0).start()

    def arrive(self, q):
        if q == 1:
            self.first[0].wait_recv()
        elif q in (2, 4):
            j = q // 2 - 1
            self.first[1 + j].wait_recv()
            self.relay[j].start()
            self.passed[j].start()
        elif q in (3, 5):
            j = q // 2 - 1
            self.passed[j].wait_recv()
            if self.early_passed:
                self.early_passed[j].wait_recv()
        elif q == 6:
            for cp in self.relay + self.early_relay:
                cp.wait_recv()
            self.passed[2].start()
        else:
            self.passed[2].wait_recv()
        self._fetch(q).start()

    def wait_fetch(self, slot):
        pltpu.make_async_copy(self.shard, self.wbuf.at[slot], self.fetch_sems.at[slot]).wait()

    def finish(self):
        for cp in self.first + self.relay + self.passed + self.early_relay + self.early_passed:
            cp.wait_send()
        self.mine.wait()


_STREAM_SEMS = [pltpu.SemaphoreType.DMA((12,)), pltpu.SemaphoreType.DMA((12,)), pltpu.SemaphoreType.DMA,
                pltpu.SemaphoreType.DMA((2,))]


def _stream_steps(gather, p, i, ni):
    @pl.when((p == 0) & (i == 0))
    def _():
        gather.start()

    @pl.when(i == 0)
    def _():
        gather.wait_fetch(p % 2)

    @pl.when(i == ni - 1)
    def _():
        for q in range(1, N_DEV):
            @pl.when(p == q - 1)
            def _():
                gather.arrive(q)


def _norm_matmul_stream_call(name, x, g, shard, order, land, pre0, r0, comm, epilogue, out_dtypes):
    t, d = x.shape
    cb = shard.shape[1]
    tm = _tile(t, 1024)
    ni = t // tm
    n_sems = len(_STREAM_SEMS)
    assert not comm.lands
    ns, nn, no = len(comm.srcs), len(comm.new), len(out_dtypes)
    n_remote, n_local = _count_copies(comm)
    has_land = land is not None

    def body(order_ref, x_ref, g_ref, shard_ref, *refs):
        refs = refs[has_land:]
        srcs, out_refs, (h_ref, land_ref) = refs[:ns], refs[ns:ns + no], refs[ns + no:ns + no + 2]
        new = refs[ns + no + 2:ns + no + 2 + nn]
        wbuf, sems = refs[ns + no + 2 + nn], refs[ns + no + 3 + nn:]
        p, i = pl.program_id(0), pl.program_id(1)
        gather = _StreamedGather(shard_ref, land_ref, wbuf, pre0, r0, *sems[:n_sems])
        (carried,) = _make_copies(comm.plan(srcs, new), *sems[n_sems:])
        rows = pl.ds(pl.multiple_of(i * tm, tm), tm)
        _stream_steps(gather, p, i, ni)

        @pl.when((p == CARRIED_AFTER) & (i == ni - 1))
        def _():
            for cp in carried:
                cp.start()

        @pl.when(p == 0)
        def _():
            n, _ = _rms_rows(x_ref[...])
            h_ref[rows, :] = (n * g_ref[...]).astype(BF16)

        tails = epilogue(_dot_nn(h_ref[rows, :], wbuf[p % 2]))
        for out_ref, tail, dt in zip(out_refs, tails, out_dtypes):
            out_ref[...] = tail.astype(dt)

        @pl.when((p == N_DEV - 1) & (i == ni - 1))
        def _():
            gather.finish()
            for cp in carried:
                cp.wait()

    carried_sems = [pltpu.SemaphoreType.DMA((max(n_remote, 1),)), pltpu.SemaphoreType.DMA((max(n_remote, 1),)),
                    pltpu.SemaphoreType.DMA((max(n_local, 1),))]
    return pl.pallas_call(
        body, name=name,
        grid_spec=pltpu.PrefetchScalarGridSpec(
            num_scalar_prefetch=1, grid=(N_DEV, ni),
            in_specs=[pl.BlockSpec((tm, d), lambda p, i, o: (jnp.where(p == 0, i, ni - 1), 0)),
                      pl.BlockSpec((1, d), lambda p, i, o: (0, 0)),
                      _ANY] + [_ANY] * (has_land + ns),
            out_specs=[pl.BlockSpec((tm, cb), lambda p, i, o: (i, o[p]))] * no
                      + [pl.BlockSpec(memory_space=pltpu.VMEM), _ANY] + [_ANY] * nn,
            scratch_shapes=[pltpu.VMEM((2, d, cb), BF16)] + _STREAM_SEMS + carried_sems),
        out_shape=[jax.ShapeDtypeStruct((t, N_DEV * cb), dt) for dt in out_dtypes]
                  + [jax.ShapeDtypeStruct((t, d), BF16), jax.ShapeDtypeStruct((N_DEV, d, cb), BF16)] + comm.new,
        input_output_aliases={4: no + 1} if has_land else {},
        compiler_params=_params("arbitrary", "arbitrary"),
    )(order, x, g, shard, *([land] if has_land else []), *comm.srcs)


def _down_call(act, shard, order, land, r0):
    t = act.shape[0]
    rb, d = shard.shape
    tm = _tile(t, 1024)
    ni = t // tm

    def body(order_ref, a_ref, shard_ref, land_in_ref, y_ref, land_ref, wbuf, *sems):
        p, i = pl.program_id(0), pl.program_id(1)
        gather = _StreamedGather(shard_ref, land_ref, wbuf, 0, r0, *sems)
        rows = pl.ds(pl.multiple_of(i * tm, tm), tm)
        _stream_steps(gather, p, i, ni)
        @pl.when(p == 0)
        def _():
            y_ref[rows, :] = _dot_nn(a_ref[...], wbuf[0])

        @pl.when(p > 0)
        def _():
            y_ref[rows, :] += _dot_nn(a_ref[...], wbuf[p % 2])

        @pl.when((p == N_DEV - 1) & (i == ni - 1))
        def _():
            gather.finish()

    return pl.pallas_call(
        body, name="down_fwd",
        grid_spec=pltpu.PrefetchScalarGridSpec(
            num_scalar_prefetch=1, grid=(N_DEV, ni),
            in_specs=[pl.BlockSpec((tm, rb), lambda p, i, o: (i, o[p])), _ANY, _ANY],
            out_specs=[pl.BlockSpec(memory_space=pltpu.VMEM), _ANY],
            scratch_shapes=[pltpu.VMEM((2, rb, d), BF16)] + _STREAM_SEMS),
        out_shape=[jax.ShapeDtypeStruct((t, d), F32), jax.ShapeDtypeStruct((N_DEV, rb, d), BF16)],
        input_output_aliases={3: 1},
        compiler_params=_params("arbitrary", "arbitrary"),
    )(order, act, shard, land)


def _loss_call(y, x2, target, g_final):
    t, d = y.shape
    tr = _tile(t, 256)

    def body(y_ref, x_ref, tg_ref, g_ref, loss_ref, dx_ref, dxb_ref, dg_ref):
        @pl.when(pl.program_id(0) == 0)
        def _():
            loss_ref[...] = jnp.zeros_like(loss_ref)
            dg_ref[...] = jnp.zeros_like(dg_ref)

        n, r = _rms_rows(x_ref[...] + y_ref[...])
        err = n * g_ref[...] - tg_ref[...]
        loss_ref[...] += 0.5 * jnp.sum(jnp.mean(err * err, axis=-1, keepdims=True))
        dy = err * (1.0 / d)
        dg_ref[...] += jnp.sum(dy * n, axis=0, keepdims=True)
        dx = _rms_bwd_rows(dy * g_ref[...], n, r)
        dx_ref[...] = dx
        dxb_ref[...] = dx.astype(BF16)

    return pl.pallas_call(
        body, name="loss_head",
        grid=(t // tr,),
        in_specs=[pl.BlockSpec((tr, d), lambda i: (i, 0)),
                  pl.BlockSpec((tr, d), lambda i: (i, 0)),
                  pl.BlockSpec((tr, d), lambda i: (i, 0)),
                  pl.BlockSpec((1, d), lambda i: (0, 0))],
        out_specs=[pl.BlockSpec((8, LANES), lambda i: (0, 0)),
                   pl.BlockSpec((tr, d), lambda i: (i, 0)),
                   pl.BlockSpec((tr, d), lambda i: (i, 0)),
                   pl.BlockSpec((1, d), lambda i: (0, 0))],
        out_shape=[jax.ShapeDtypeStruct((8, LANES), F32), jax.ShapeDtypeStruct((t, d), F32),
                   jax.ShapeDtypeStruct((t, d), BF16), jax.ShapeDtypeStruct((1, d), F32)],
        compiler_params=_params("arbitrary"),
    )(y, x2, target, g_final)


def _norm_bwd_call(name, dh, x, dres, g, want_bf16, comm=None):
    t, d = x.shape
    tr = _tile(t, 256)

    def body(dh_ref, x_ref, dres_ref, g_ref, dx_ref, *rest):
        dg_ref = rest[-1]

        @pl.when(pl.program_id(0) == 0)
        def _():
            dg_ref[...] = jnp.zeros_like(dg_ref)

        n, r = _rms_rows(x_ref[...])
        dh = dh_ref[...]
        dg_ref[...] += jnp.sum(dh * n, axis=0, keepdims=True)
        dx = dres_ref[...] + _rms_bwd_rows(dh * g_ref[...], n, r)
        dx_ref[...] = dx
        if want_bf16:
            rest[0][...] = dx.astype(BF16)

    row = pl.BlockSpec((tr, d), lambda i: (i, 0))
    vec = pl.BlockSpec((1, d), lambda i: (0, 0))
    out_specs = [row] + ([row] if want_bf16 else []) + [vec]
    out_shape = ([jax.ShapeDtypeStruct((t, d), F32)]
                 + ([jax.ShapeDtypeStruct((t, d), BF16)] if want_bf16 else [])
                 + [jax.ShapeDtypeStruct((1, d), F32)])
    return _carrier_call(
        body, (dh, x, dres, g), comm, name=name,
        grid=(t // tr,),
        in_specs=[row, row, row, vec],
        out_specs=out_specs, out_shape=out_shape,
        sem=("arbitrary",))


def _dact_call(dx3b, w_down, act, comm=None):
    t, d = dx3b.shape
    f = w_down.shape[0]
    tm = _tile(t, 1024)
    tn = _tile(f, 2048)

    def body(g_ref, w_ref, act_ref, o_ref):
        dact = _dot_nt(g_ref[...], w_ref[...])
        o_ref[...] = (dact * act_ref[...].astype(F32)).astype(BF16)

    return _carrier_call(
        body, (dx3b, w_down, act), comm, name="dact_bwd",
        grid=(t // tm, f // tn),
        in_specs=[pl.BlockSpec((tm, d), lambda i, j: (i, 0)),
                  pl.BlockSpec((tn, d), lambda i, j: (j, 0)),
                  pl.BlockSpec((tm, tn), lambda i, j: (i, j))],
        out_specs=pl.BlockSpec((tm, tn), lambda i, j: (i, j)),
        out_shape=jax.ShapeDtypeStruct((t, f), BF16),
        sem=("parallel", "parallel"))


def _wgrad_call(name, a, b, out_blocks, out_block_cols, comm=None, *, t1, t2=None, merge=1):
    t, k1 = a.shape
    k2 = b.shape[1]
    tt = _tile(t, 2048)
    t1 = _tile(k1, t1)
    t2 = _tile(k2, t2) if out_blocks is None else merge * out_block_cols
    nk = t // tt

    def body(a_ref, b_ref, o_ref, acc_ref):
        k = pl.program_id(2)

        @pl.when(k == 0)
        def _():
            acc_ref[...] = jnp.zeros_like(acc_ref)

        acc_ref[...] += _dot_tn(a_ref[...], b_ref[...])

        @pl.when(k == nk - 1)
        def _():
            if out_blocks is None:
                o_ref[...] = acc_ref[...].astype(BF16)
            else:
                for blk in range(merge):
                    o_ref[blk] = acc_ref[:, blk * out_block_cols:(blk + 1) * out_block_cols].astype(BF16)

    if out_blocks is None:
        out_spec = pl.BlockSpec((t1, t2), lambda i, j, k: (i, j))
        out_shape = jax.ShapeDtypeStruct((k1, k2), BF16)
    else:
        out_spec = pl.BlockSpec((merge, t1, out_block_cols), lambda i, j, k: (j, i, 0))
        out_shape = jax.ShapeDtypeStruct((out_blocks, k1, out_block_cols), BF16)
    return _carrier_call(
        body, (a, b), comm, name=name,
        grid=(k1 // t1, k2 // t2, nk),
        in_specs=[pl.BlockSpec((tt, t1), lambda i, j, k: (k, i)),
                  pl.BlockSpec((tt, t2), lambda i, j, k: (k, j))],
        out_specs=out_spec, out_shape=out_shape,
        scratch_shapes=[pltpu.VMEM((t1, t2), F32)],
        sem=("parallel", "parallel", "arbitrary"))


def _dgrad_blocked_call(name, g, w_g, comm=None, *, merge=1):
    t = g.shape[0]
    nb, d, cb = w_g.shape
    tm = _tile(t, 1024)
    tn = _tile(d, 2048)
    tk = merge * cb

    def body(g_ref, w_ref, o_ref):
        @pl.when(pl.program_id(2) == 0)
        def _():
            o_ref[...] = jnp.zeros_like(o_ref)

        w = w_ref[0] if merge == 1 else jnp.concatenate([w_ref[b] for b in range(merge)], axis=1)
        o_ref[...] += _dot_nt(g_ref[...], w)

    return _carrier_call(
        body, (g, w_g), comm, name=name,
        grid=(t // tm, d // tn, nb // merge),
        in_specs=[pl.BlockSpec((tm, tk), lambda i, j, k: (i, k)),
                  pl.BlockSpec((merge, tn, cb), lambda i, j, k: (k, j, 0))],
        out_specs=pl.BlockSpec((tm, tn), lambda i, j, k: (i, j)),
        out_shape=jax.ShapeDtypeStruct((t, d), F32),
        sem=("parallel", "parallel", "arbitrary"))


def _dmixed_call(dx2b, w_out, comm=None):
    t, d = dx2b.shape
    e = w_out.shape[0]
    tm = _tile(t, 1024)
    tn = _tile(e, 1024)

    def body(g_ref, w_ref, o_ref):
        o_ref[...] = _dot_nt(g_ref[...], w_ref[...])

    return _carrier_call(
        body, (dx2b, w_out), comm, name="dmixed_bwd",
        grid=(t // tm, e // tn),
        in_specs=[pl.BlockSpec((tm, d), lambda i, j: (i, 0)),
                  pl.BlockSpec((tn, d), lambda i, j: (j, 0))],
        out_specs=pl.BlockSpec((tm, tn), lambda i, j: (i, j)),
        out_shape=jax.ShapeDtypeStruct((t, e), F32),
        sem=("parallel", "parallel"))


def _mixer_bwd_call(proj, dmixed, w_s, bs_t, g_v, w_pool_g, pool_scale, comm=None):
    t = proj.shape[0]
    tt = _tile(t, 512)
    nchunk = tt // CHUNK
    hb = tt // HALO
    last_halo = t // HALO - 1
    nsteps = t // tt
    rb = GROUP // N_DEV

    def body(pu_ref, pv_ref, z_ref, zp_ref, da_ref, db_ref, dbn_ref, ws_ref, bs_ref, gv_ref, wp_ref, ps_ref,
             dproj_ref, dws_ref, dbs_ref, dgv_ref, dps_ref, dwp_ref):
        i = pl.program_id(0)

        @pl.when(i == 0)
        def _():
            dws_ref[...] = jnp.zeros_like(dws_ref)
            dbs_ref[...] = jnp.zeros_like(dbs_ref)
            dgv_ref[...] = jnp.zeros_like(dgv_ref)
            dps_ref[...] = jnp.zeros_like(dps_ref)
            dwp_ref[...] = jnp.zeros_like(dwp_ref)

        tril = (lax.broadcasted_iota(jnp.int32, (CHUNK, CHUNK), 0)
                >= lax.broadcasted_iota(jnp.int32, (CHUNK, CHUNK), 1))
        for h in range(N_HEADS):
            cols = slice(h * CHUNK, (h + 1) * CHUNK)
            v, dv_dpv = _gelu_and_grad(pv_ref[:, cols])
            vhat, rv = _rms_rows(v)
            gv = gv_ref[:, cols]
            vn = (vhat * gv).astype(BF16)
            u, du_dpu = _gelu_and_grad(pu_ref[:, cols])
            w = jnp.where(tril, ws_ref[h], 0.0).astype(BF16)
            bcol = bs_ref[:, h:h + 1]
            dout = da_ref[:, cols]
            dmix = dout * u
            dmix_b = dmix.astype(BF16)
            dws = jnp.zeros((CHUNK, CHUNK), F32)
            dbs = jnp.zeros((CHUNK, 1), F32)
            dvn_parts = []
            du_parts = []
            for c in range(nchunk):
                rows = slice(c * CHUNK, (c + 1) * CHUNK)
                mixed = _dot_nn(w, vn[rows]) + bcol
                du_parts.append(dout[rows] * mixed)
                dvn_parts.append(_dot_tn(w, dmix_b[rows]))
                dws = dws + _dot_nt(dmix_b[rows], vn[rows])
                dbs = dbs + jnp.sum(dmix[rows], axis=1, keepdims=True)
            dws_ref[h] += jnp.where(tril, dws, 0.0)
            dbs_ref[:, h:h + 1] += dbs
            dvn = jnp.concatenate(dvn_parts, axis=0)
            du = jnp.concatenate(du_parts, axis=0)
            dgv_ref[:, cols] += jnp.sum(dvn * vhat, axis=0, keepdims=True)
            dv = _rms_bwd_rows(dvn * gv, vhat, rv)
            dproj_ref[:, cols] = (du * du_dpu).astype(BF16)
            dproj_ref[:, A_WIDTH + h * CHUNK:A_WIDTH + (h + 1) * CHUNK] = (dv * dv_dpv).astype(BF16)

        zprev = jnp.where(i > 0, zp_ref[...], 0.0)
        ext = jnp.concatenate([zprev, z_ref[...]], axis=0)
        dnext = jnp.where(i < nsteps - 1, dbn_ref[...], 0.0)
        dext = jnp.concatenate([db_ref[...], dnext], axis=0)
        for g, win in enumerate(POOL_WINDOWS):
            cols = slice(g * GROUP, (g + 1) * GROUP)
            zg = ext[:, cols]
            pooled = _window_sum_back(zg, win)[HALO:] / _pool_counts(i * tt, tt, win) - zg[HALO:]
            pooled_b = pooled.astype(BF16)
            wp = wp_ref[:, g].reshape(GROUP, GROUP)
            y = _dot_nn(pooled_b, wp)
            dout = dext[:, cols]
            dps_ref[:, cols] += jnp.sum(dout[:tt] * y, axis=0, keepdims=True)
            dy_b = (dout * ps_ref[:, cols]).astype(BF16)
            dwp_ref[:, g] += _dot_tn(pooled_b, dy_b[:tt]).reshape(N_DEV, rb, GROUP)
            dpooled = _dot_nt(dy_b, wp)
            q = dpooled / _pool_counts(i * tt, tt + HALO, win)
            dz = _window_sum_fwd(q, win)[:tt] - dpooled[:tt]
            dproj_ref[:, 2 * A_WIDTH + g * GROUP:2 * A_WIDTH + (g + 1) * GROUP] = dz.astype(BF16)

    def full(shape):
        return pl.BlockSpec(shape, lambda i: (0,) * len(shape))

    return _carrier_call(
        body, (proj, proj, proj, proj, dmixed, dmixed, dmixed, w_s, bs_t, g_v, w_pool_g, pool_scale), comm,
        name="mixer_bwd",
        grid=(nsteps,),
        in_specs=[pl.BlockSpec((tt, A_WIDTH), lambda i: (i, 0)),
                  pl.BlockSpec((tt, A_WIDTH), lambda i: (i, 1)),
                  pl.BlockSpec((tt, B_WIDTH), lambda i: (i, 2)),
                  pl.BlockSpec((HALO, B_WIDTH), lambda i: (jnp.maximum(i * hb - 1, 0), 2)),
                  pl.BlockSpec((tt, A_WIDTH), lambda i: (i, 0)),
                  pl.BlockSpec((tt, B_WIDTH), lambda i: (i, 1)),
                  pl.BlockSpec((HALO, B_WIDTH), lambda i: (jnp.minimum((i + 1) * hb, last_halo), 1)),
                  full((N_HEADS, CHUNK, CHUNK)), full((CHUNK, N_HEADS)), full((1, A_WIDTH)),
                  full((N_DEV, 4, rb, GROUP)), full((1, B_WIDTH))],
        out_specs=[pl.BlockSpec((tt, 2 * A_WIDTH + B_WIDTH), lambda i: (i, 0)),
                   full((N_HEADS, CHUNK, CHUNK)), full((CHUNK, N_HEADS)), full((1, A_WIDTH)),
                   full((1, B_WIDTH)), full((N_DEV, 4, rb, GROUP))],
        out_shape=[jax.ShapeDtypeStruct((t, 2 * A_WIDTH + B_WIDTH), BF16),
                   jax.ShapeDtypeStruct((N_HEADS, CHUNK, CHUNK), F32),
                   jax.ShapeDtypeStruct((CHUNK, N_HEADS), F32),
                   jax.ShapeDtypeStruct((1, A_WIDTH), F32),
                   jax.ShapeDtypeStruct((1, B_WIDTH), F32),
                   jax.ShapeDtypeStruct((N_DEV, 4, rb, GROUP), F32)],
        sem=("arbitrary",))


def _adamw(w, g, m, v):
    m = ADAM_B1 * m + (1.0 - ADAM_B1) * g
    v = ADAM_B2 * v + (1.0 - ADAM_B2) * (g * g)
    m_hat = m / ADAM_C1
    v_hat = v / ADAM_C2
    delta = -ADAM_LR * (m_hat / (jnp.sqrt(v_hat) + ADAM_EPS) + ADAM_WD * w)
    return delta, m, v


PAIR_SUM_TILE_ELEMS = 1024 * 1024
ADAMW_TILE_ELEMS = 512 * 1024


def _row_tile(r, c, elems):
    t = r
    while t * c > elems and t % 32 == 0:
        t //= 2
    return t


def _pair_sum_call(name, pos, grad, got):
    _, r, c = grad.shape
    tr = _row_tile(r, c, PAIR_SUM_TILE_ELEMS)

    def chip_of(rel, pos_ref):
        px = jnp.where((rel == 0) | (rel == 2), 1 - pos_ref[0], pos_ref[0])
        py = jnp.where((rel == 1) | (rel == 2), 1 - pos_ref[1], pos_ref[1])
        return 2 * px + py

    def body(pos_ref, own_ref, got_ref, out_ref):
        out_ref[...] = (own_ref[...].astype(F32) + got_ref[...].astype(F32)).astype(BF16)

    return pl.pallas_call(
        body, name=name,
        grid_spec=pltpu.PrefetchScalarGridSpec(
            num_scalar_prefetch=1, grid=(3, r // tr),
            in_specs=[pl.BlockSpec((None, tr, c), lambda k, i, p: (2 * chip_of(k, p) + p[2], i, 0)),
                      pl.BlockSpec((None, tr, c), lambda k, i, p: (chip_of(k, p), i, 0))],
            out_specs=pl.BlockSpec((None, tr, c), lambda k, i, p: (k, i, 0))),
        out_shape=jax.ShapeDtypeStruct((3, r, c), BF16),
        compiler_params=_params("parallel", "parallel"),
    )(pos, grad, got)


def _final_call(name, pos, grad, got_pair, got_chips, w, m, v):
    _, r, c = grad.shape
    tr = _row_tile(r, c, ADAMW_TILE_ELEMS)

    def body(pos_ref, own_ref, pair_ref, chips_ref, w_ref, m_ref, v_ref, g_out, d_out, m_out, v_out):
        g = own_ref[...].astype(F32) + pair_ref[...].astype(F32)
        for j in range(3):
            g = g + chips_ref[j].astype(F32)
        delta, m_new, v_new = _adamw(w_ref[...], g, m_ref[...], v_ref[...])
        g_out[...] = g
        d_out[...] = delta
        m_out[...] = m_new
        v_out[...] = v_new

    row = pl.BlockSpec((tr, c), lambda i, p: (i, 0))
    return pl.pallas_call(
        body, name=name,
        grid_spec=pltpu.PrefetchScalarGridSpec(
            num_scalar_prefetch=1, grid=(r // tr,),
            in_specs=[pl.BlockSpec((None, tr, c), lambda i, p: (4 * p[0] + 2 * p[1] + p[2], i, 0)),
                      pl.BlockSpec((None, tr, c), lambda i, p: (2 * p[0] + p[1], i, 0)),
                      pl.BlockSpec((3, tr, c), lambda i, p: (0, i, 0)), row, row, row],
            out_specs=[row] * 4),
        out_shape=[jax.ShapeDtypeStruct((r, c), F32)] * 4,
        compiler_params=_params("parallel"),
    )(pos, grad, got_pair, got_chips, w, m, v)


def _small_final_call(name, parts, w, m, v):
    _, rows, c = parts.shape
    r = w.shape[0]

    def body(p_ref, w_ref, m_ref, v_ref, g_out, d_out, m_out, v_out):
        g = p_ref[0]
        for k in range(1, N_DEV):
            g = g + p_ref[k]
        delta, m_new, v_new = _adamw(w_ref[...], g[:r], m_ref[...], v_ref[...])
        g_out[...] = g
        d_out[...] = delta
        m_out[...] = m_new
        v_out[...] = v_new

    return pl.pallas_call(
        body, name=name,
        out_shape=[jax.ShapeDtypeStruct((rows, c), F32)] + [jax.ShapeDtypeStruct((r, c), F32)] * 3,
        compiler_params=pltpu.CompilerParams(vmem_limit_bytes=VMEM_LIMIT),
    )(parts, w, m, v)


_SMALL_EARLY = ("g_v", "w_s", "b_s", "pool_scale", "g_ffn", "g_final")
_BIG = ("w_in", "w_pool", "w_out", "w_up", "w_down")
_ORDER = ("g_mix", "w_in", "g_v", "w_s", "b_s", "w_pool", "pool_scale", "w_out", "g_ffn", "w_up", "w_down", "g_final")


def _pack(parts):
    return jnp.concatenate([p.reshape(-1, LANES) for p in parts], axis=0)


def _unpack(packed, like):
    out, row = [], 0
    for a in like:
        rows = a.size // LANES
        out.append(packed[row:row + rows].reshape(a.shape))
        row += rows
    return out


def kernel(x, g_mix, w_in, g_v, w_s, b_s, w_pool, pool_scale, w_out, g_ffn, w_up, w_down, g_final, loss_target, m_g_mix, m_w_in, m_g_v, m_w_s, m_b_s, m_w_pool, m_pool_scale, m_w_out, m_g_ffn, m_w_up, m_w_down, m_g_final, v_g_mix, v_w_in, v_g_v, v_w_s, v_b_s, v_w_pool, v_pool_scale, v_w_out, v_g_ffn, v_w_up, v_w_down, v_g_final):
    weights = dict(g_mix=g_mix, w_in=w_in, g_v=g_v, w_s=w_s, b_s=b_s, w_pool=w_pool, pool_scale=pool_scale,
                   w_out=w_out, g_ffn=g_ffn, w_up=w_up, w_down=w_down, g_final=g_final)
    mom = dict(g_mix=m_g_mix, w_in=m_w_in, g_v=m_g_v, w_s=m_w_s, b_s=m_b_s, w_pool=m_w_pool,
               pool_scale=m_pool_scale, w_out=m_w_out, g_ffn=m_g_ffn, w_up=m_w_up, w_down=m_w_down,
               g_final=m_g_final)
    var = dict(g_mix=v_g_mix, w_in=v_w_in, g_v=v_g_v, w_s=v_w_s, b_s=v_b_s, w_pool=v_w_pool,
               pool_scale=v_pool_scale, w_out=v_w_out, g_ffn=v_g_ffn, w_up=v_w_up, w_down=v_w_down,
               g_final=v_g_final)

    t, d = x.shape[1], x.shape[2]
    xs = x.reshape(t, d)
    target = loss_target.reshape(t, d)

    shard2d = dict(w_in=w_in.reshape(d, -1), w_pool=w_pool.reshape(-1, GROUP), w_out=w_out.reshape(-1, d),
                   w_up=w_up.reshape(d, -1), w_down=w_down.reshape(-1, d))
    sb = {k: shard2d[k].astype(BF16) for k in _BIG}
    rows = {k: sb[k].shape[0] for k in _BIG}

    def gathered_shape(k):
        return jax.ShapeDtypeStruct((N_DEV,) + sb[k].shape, BF16)

    def landing(n, like):
        return jax.ShapeDtypeStruct((n,) + like.shape[1:], like.dtype)

    def from_everyone(block):
        return jax.ShapeDtypeStruct((N_DEV,) + block.shape, block.dtype)

    def cuts(r, fractions):
        return [0] + [int(r * f) // 16 * 16 for f in fractions] + [r]

    g_mix2, g_ffn2, g_final2 = g_mix.reshape(1, d), g_ffn.reshape(1, d), g_final.reshape(1, d)
    g_v2, ps2 = g_v.reshape(1, A_WIDTH), pool_scale.reshape(1, B_WIDTH)
    w_s3 = w_s.reshape(N_HEADS, CHUNK, CHUNK)
    bs_t = b_s.reshape(N_HEADS, CHUNK).T
    xi, yi, ci = _position()
    pos = jnp.stack([xi, yi, ci]).astype(jnp.int32)

    order = (4 * xi + 2 * yi + ci) ^ jnp.array(ARRIVAL_ORDER, jnp.int32)
    u = cuts(rows["w_up"], (0.25, 0.55))
    ahead = cuts(rows["w_down"], (0.6,))[1]
    proj, h1, w_in_g, w_out_g, w_pool_g = _norm_matmul_stream_call(
        "proj_fwd", xs, g_mix2, sb["w_in"], order, None, 0, 0, _Comm(
            [sb["w_out"], sb["w_pool"]], [], [gathered_shape("w_out"), gathered_shape("w_pool")],
            lambda s, l: [_gather_to_neighbours(s[0], l[0], 0, rows["w_out"]) + _everyone(s[1], l[1])]),
        lambda a: (a,), (F32,))
    w_pool_g = w_pool_g.reshape(N_DEV, 4, GROUP // N_DEV, GROUP)
    mixed, w_out_g, w_up_g = _mixer_fwd_call(proj, w_s3, bs_t, g_v2, w_pool_g, ps2, _Comm(
        [sb["w_up"]], [w_out_g], [gathered_shape("w_up")],
        lambda s, l: [_gather_relay(l[0], 0, rows["w_out"]) + _gather_to_neighbours(s[0], l[1], u[0], u[1]),
                      _gather_diagonal_pass_on(l[0], 0, rows["w_out"])]))
    w_out_f = w_out_g.reshape(-1, d)
    x2, w_up_g = _out_proj_call(mixed, w_out_f, xs, _Comm(
        [sb["w_up"]], [w_up_g], [],
        lambda s, l: [_gather_relay(l[0], u[0], u[1]) + _gather_to_neighbours(s[0], l[0], u[1], u[2]),
                      _gather_diagonal_pass_on(l[0], u[0], u[1])]))

    def relu2_and_slope(a):
        r = jnp.maximum(a, 0.0)
        return r * r, 2.0 * r

    act, dact_da, h2, w_up_g, w_down_g = _norm_matmul_stream_call(
        "up_fwd", x2, g_ffn2, sb["w_up"], order, w_up_g, u[1], u[2], _Comm(
            [sb["w_down"]], [], [gathered_shape("w_down")],
            lambda s, l: [_gather_to_neighbours(s[0], l[0], 0, ahead)]),
        relu2_and_slope, (BF16, BF16))
    y, w_down_g = _down_call(act, sb["w_down"], order, w_down_g, ahead)
    w_down_f = w_down_g.reshape(-1, d)
    loss_part, dx3, dx3b, dg_final = _loss_call(y, x2, target, g_final2)

    def pair_sum(k, grad, got):
        return _pair_sum_call(k + "_pair_sum", pos, grad, got)

    def finish(k, grad, got_pair, got_chips):
        s = shard2d[k]
        outs = _final_call(k + "_adamw", pos, grad, got_pair, got_chips, s, mom[k].reshape(s.shape),
                           var[k].reshape(s.shape))
        return [o.reshape(weights[k].shape) for o in outs]

    result = {}
    (gw_down,) = _wgrad_call("w_down_grad", act, dx3b, None, None, t1=1024, t2=2048)
    gw_down = gw_down.reshape(N_DEV, -1, d)
    da, pair_down = _dact_call(dx3b, w_down_f, dact_da, _Comm(
        [gw_down], [], [landing(4, gw_down)], lambda s, l: [_pair_exchange(s[0], l[0])]))
    sums_down = pair_sum("w_down", gw_down, pair_down)
    dn = cuts(rows["w_down"], (0.8,))
    gw_up, got = _wgrad_call("w_up_grad", h2, da, N_DEV, w_up_g.shape[2], _Comm(
        [sums_down], [], [landing(3, sums_down)],
        lambda s, l: [_chip_exchange(s[0], l[0], dn[0], dn[1])]), t1=2048)
    dh2, got, pair_up = _dgrad_blocked_call("dh2_bwd", da, w_up_g, _Comm(
        [sums_down, gw_up], [got], [landing(4, gw_up)],
        lambda s, l: [_chip_exchange(s[0], l[0], dn[1], dn[2]) + _pair_exchange(s[1], l[1])]), merge=2)
    result["w_down"] = finish("w_down", gw_down, pair_down, got)
    sums_up = pair_sum("w_up", gw_up, pair_up)
    v = cuts(rows["w_up"], (0.28, 0.52, 0.84))
    dx2, dx2b, dg_ffn, got_up = _norm_bwd_call("ffn_norm_bwd", dh2, x2, dx3, g_ffn2, True, _Comm(
        [sums_up], [], [landing(3, sums_up)], lambda s, l: [_chip_exchange(s[0], l[0], v[0], v[1])]))
    dmixed, got_up = _dmixed_call(dx2b, w_out_f, _Comm(
        [sums_up], [got_up], [], lambda s, l: [_chip_exchange(s[0], l[0], v[1], v[2])]))
    gw_out, got_up = _wgrad_call("w_out_grad", mixed, dx2b, None, None, _Comm(
        [sums_up], [got_up], [], lambda s, l: [_chip_exchange(s[0], l[0], v[2], v[3])]), t1=2048, t2=1024)
    gw_out = gw_out.reshape(N_DEV, -1, d)
    dproj, dw_s, dbs_t, dg_v, dps, dw_pool, got_up, pair_out = _mixer_bwd_call(
        proj, dmixed, w_s3, bs_t, g_v2, w_pool_g, ps2, _Comm(
            [sums_up, gw_out], [got_up], [landing(4, gw_out)],
            lambda s, l: [_chip_exchange(s[0], l[0], v[3], v[4]) + _pair_exchange(s[1], l[1])]))
    result["w_up"] = finish("w_up", gw_up, pair_up, got_up)
    sums_out = pair_sum("w_out", gw_out, pair_out)
    gw_pool = dw_pool.astype(BF16).reshape(N_DEV, -1, GROUP)
    early = dict(g_v=dg_v, w_s=dw_s, b_s=dbs_t.T, pool_scale=dps, g_ffn=dg_ffn, g_final=dg_final)
    packed = _pack([early[k] for k in _SMALL_EARLY] + [loss_part])
    early_rows = packed.shape[0]
    gw_in, got, pair_pool, parts_early = _wgrad_call("w_in_grad", h1, dproj, N_DEV, w_in_g.shape[2], _Comm(
        [sums_out, gw_pool, packed], [], [landing(3, sums_out), landing(4, gw_pool), from_everyone(packed)],
        lambda s, l: [_chip_exchange(s[0], l[0], 0, rows["w_out"]) + _pair_exchange(s[1], l[1])
                      + _gather_first(s[2], l[2], 0, early_rows)]), t1=2048, merge=MERGE_W_IN)
    result["w_out"] = finish("w_out", gw_out, pair_out, got)
    sums_pool = pair_sum("w_pool", gw_pool, pair_pool)
    (pair_in,) = _comm_call("pair_exchange_w_in", _Comm(
        [gw_in], [], [landing(4, gw_in)], lambda s, l: [_pair_exchange(s[0], l[0])]))
    sums_in = pair_sum("w_in", gw_in, pair_in)
    dh1, parts_early, got, got_pool = _dgrad_blocked_call("dh1_bwd", dproj, w_in_g, _Comm(
        [sums_in, sums_pool], [parts_early], [landing(3, sums_in), landing(3, sums_pool)],
        lambda s, l: [_chip_exchange(s[0], l[1], 0, rows["w_in"]) + _chip_exchange(s[1], l[2], 0, rows["w_pool"])
                      + _gather_pass_on(l[0], 0, early_rows)]),
        merge=MERGE_W_IN)
    result["w_in"] = finish("w_in", gw_in, pair_in, got)
    result["w_pool"] = finish("w_pool", gw_pool, pair_pool, got_pool)
    grad_x, dg_mix = _norm_bwd_call("mix_norm_bwd", dh1, xs, dx2, g_mix2, False)
    packed = _pack([dg_mix])
    (parts_late,) = _comm_call("gather_g_mix_grad", _Comm(
        [packed], [], [from_everyone(packed)], lambda s, l: [_everyone(s[0], l[0])]))

    for names, parts, tag in ((_SMALL_EARLY, parts_early, "small_adamw"), (("g_mix",), parts_late, "g_mix_adamw")):
        outs = _small_final_call(tag, parts, _pack([weights[k] for k in names]), _pack([mom[k] for k in names]),
                                 _pack([var[k] for k in names]))
        if tag == "small_adamw":
            loss = outs[0][-1, 0]
        like = [weights[k] for k in names]
        unpacked = [_unpack(o, like) for o in outs]
        for idx, k in enumerate(names):
            result[k] = [unpacked[q][idx] for q in range(4)]

    grads = [result[k][0] for k in _ORDER]
    deltas = [result[k][1] for k in _ORDER]
    new_m = [result[k][2] for k in _ORDER]
    new_v = [result[k][3] for k in _ORDER]
    return (loss, grad_x.reshape(x.shape), *grads, *deltas, *new_m, *new_v)
```

```python
import functools
import math

import jax
import jax.numpy as jnp
from jax import lax
from jax.experimental import pallas as pl
from jax.experimental.pallas import tpu as pltpu

F32 = jnp.float32
BF16 = jnp.bfloat16
MESH = pl.DeviceIdType.MESH

N_DEV = 8
EPS = 1e-6
CHUNK = 128
N_HEADS = 8
A_WIDTH = 1024
B_WIDTH = 1024
POOL_WINDOWS = (2, 4, 8, 16)
GROUP = 256
HALO = 16
LANES = 128

ADAM_LR = 0.001
ADAM_B1 = 0.9
ADAM_B2 = 0.999
ADAM_EPS = 1e-08
ADAM_WD = 0.01
ADAM_STEP = 10
ADAM_C1 = 1.0 - ADAM_B1 ** ADAM_STEP
ADAM_C2 = 1.0 - ADAM_B2 ** ADAM_STEP

VMEM_LIMIT = 56 * 1024 * 1024
MERGE_W_IN = 2

_GELU_C = math.sqrt(2.0 / math.pi)


def _params(*sem):
    return pltpu.CompilerParams(dimension_semantics=sem, vmem_limit_bytes=VMEM_LIMIT)


def _gelu(x):
    return 0.5 * x * (1.0 + jnp.tanh(_GELU_C * (x + 0.044715 * x * x * x)))


def _gelu_and_grad(x):
    t = jnp.tanh(_GELU_C * (x + 0.044715 * x * x * x))
    g = 0.5 * x * (1.0 + t)
    dg = 0.5 * (1.0 + t) + 0.5 * x * (1.0 - t * t) * (_GELU_C * (1.0 + 3.0 * 0.044715 * x * x))
    return g, dg


def _dot_nn(a, b):
    return lax.dot_general(a, b, (((1,), (0,)), ((), ())), preferred_element_type=F32)


def _dot_nt(a, b):
    return lax.dot_general(a, b, (((1,), (1,)), ((), ())), preferred_element_type=F32)


def _dot_tn(a, b):
    return lax.dot_general(a, b, (((0,), (0,)), ((), ())), preferred_element_type=F32)


def _rms_rows(x):
    r = lax.rsqrt(jnp.mean(x * x, axis=-1, keepdims=True) + EPS)
    return x * r, r


def _rms_bwd_rows(dn, n, r):
    return r * (dn - n * jnp.mean(dn * n, axis=-1, keepdims=True))


def _tile(n, want):
    t = min(n, want)
    assert n % t == 0, (n, want)
    return t


_ANY = pl.BlockSpec(memory_space=pl.ANY)

SIBLING = 1
CHIPS = (4, 2, 6)


def _position():
    return lax.axis_index("x"), lax.axis_index("y"), lax.axis_index("c")


def _me():
    x, y, c = _position()
    return 4 * x + 2 * y + c


def _peer(rel):
    x, y, c = _position()
    return (x ^ ((rel >> 2) & 1), y ^ ((rel >> 1) & 1), c ^ (rel & 1))


class _Comm:
    def __init__(self, srcs, lands, new, plan):
        self.srcs, self.lands, self.new, self.plan = list(srcs), list(lands), list(new), plan


def _make_copies(phases, send_sems, recv_sems, local_sems):
    out, nr, nl = [], 0, 0
    for phase in phases:
        cps = []
        for item in phase:
            if item[0] == "local":
                cps.append(pltpu.make_async_copy(item[1], item[2], local_sems.at[nl]))
                nl += 1
            else:
                cps.append(pltpu.make_async_remote_copy(
                    src_ref=item[1], dst_ref=item[2], send_sem=send_sems.at[nr], recv_sem=recv_sems.at[nr],
                    device_id=_peer(item[3]), device_id_type=MESH))
                nr += 1
        out.append(cps)
    return out


def _count_copies(comm):
    phases = comm.plan([_FakeRef() for _ in comm.srcs], [_FakeRef() for _ in range(len(comm.lands) + len(comm.new))])
    items = [it for ph in phases for it in ph]
    return sum(it[0] == "remote" for it in items), sum(it[0] == "local" for it in items)


class _FakeRef:
    def __getitem__(self, idx):
        return self

    @property
    def at(self):
        return self


def _carrier_call(body, args, comm, *, name, grid, in_specs, out_specs, out_shape, scratch_shapes=(), sem):
    if not isinstance(out_shape, (list, tuple)):
        out_specs, out_shape = [out_specs], [out_shape]
    out_specs, out_shape, scratch_shapes = list(out_specs), list(out_shape), list(scratch_shapes)
    if comm is None:
        res = pl.pallas_call(body, name=name, grid=grid, in_specs=list(in_specs), out_specs=out_specs,
                             out_shape=out_shape, scratch_shapes=scratch_shapes, compiler_params=_params(*sem))(*args)
        return list(res)
    n_in, n_out, n_scr = len(args), len(out_shape), len(scratch_shapes)
    ns, nl, nn = len(comm.srcs), len(comm.lands), len(comm.new)
    n_remote, n_local = _count_copies(comm)
    steps = math.prod(grid)

    def wrapped(*refs):
        ins, srcs = refs[:n_in], refs[n_in:n_in + ns]
        o = n_in + ns + nl
        outs, lands = refs[o:o + n_out], refs[o + n_out:o + n_out + nl + nn]
        scr = refs[o + n_out + nl + nn:]
        phases = _make_copies(comm.plan(srcs, lands), *scr[n_scr:])
        assert len(phases) == 1 or (len(phases) == 2 and steps >= 3)
        step = functools.reduce(lambda acc, a: acc * grid[a] + pl.program_id(a), range(len(grid)), 0)

        @pl.when(step == 0)
        def _():
            for cp in phases[0]:
                cp.start()

        if len(phases) == 2:
            @pl.when(step == steps * 3 // 4)
            def _():
                for cp in phases[0]:
                    cp.wait()
                for cp in phases[1]:
                    cp.start()

        body(*ins, *outs, *scr[:n_scr])

        @pl.when(step == steps - 1)
        def _():
            for cp in phases[-1]:
                cp.wait()

    land_shapes = [jax.ShapeDtypeStruct(a.shape, a.dtype) for a in comm.lands] + comm.new
    sems = [pltpu.SemaphoreType.DMA((max(n_remote, 1),)), pltpu.SemaphoreType.DMA((max(n_remote, 1),)),
            pltpu.SemaphoreType.DMA((max(n_local, 1),))]
    res = pl.pallas_call(
        wrapped, name=name, grid=grid,
        in_specs=list(in_specs) + [_ANY] * (ns + nl), out_specs=out_specs + [_ANY] * (nl + nn),
        out_shape=out_shape + land_shapes, scratch_shapes=scratch_shapes + sems,
        input_output_aliases={n_in + ns + k: n_out + k for k in range(nl)},
        compiler_params=_params(*sem))(*args, *comm.srcs, *comm.lands)
    return list(res)


def _comm_call(name, comm):
    ns, nl, nn = len(comm.srcs), len(comm.lands), len(comm.new)
    n_remote, n_local = _count_copies(comm)

    def body(*refs):
        srcs, lands, sems = refs[:ns], refs[ns + nl:ns + nl + nl + nn], refs[ns + nl + nl + nn:]
        for copies in _make_copies(comm.plan(srcs, lands), *sems):
            for cp in copies:
                cp.start()
            for cp in copies:
                cp.wait()

    land_shapes = [jax.ShapeDtypeStruct(a.shape, a.dtype) for a in comm.lands] + comm.new
    res = pl.pallas_call(
        body, name=name,
        in_specs=[_ANY] * (ns + nl), out_specs=[_ANY] * (nl + nn), out_shape=land_shapes,
        scratch_shapes=[pltpu.SemaphoreType.DMA((max(n_remote, 1),)), pltpu.SemaphoreType.DMA((max(n_remote, 1),)),
                        pltpu.SemaphoreType.DMA((max(n_local, 1),))],
        input_output_aliases={ns + k: k for k in range(nl)},
    )(*comm.srcs, *comm.lands)
    return list(res)


def _rows(ref, block, r0, r1):
    return ref.at[block, pl.ds(r0, r1 - r0)]


def _gather_first(shard, land, r0, r1):
    src = shard.at[pl.ds(r0, r1 - r0)]
    dst = _rows(land, _me(), r0, r1)
    return [("local", src, dst)] + [("remote", src, dst, rel) for rel in (SIBLING,) + CHIPS]


def _gather_pass_on(land, r0, r1):
    return [("remote", _rows(land, _me() ^ rel, r0, r1), _rows(land, _me() ^ rel, r0, r1), SIBLING) for rel in CHIPS]


def _split_rows(r0, r1):
    m = (r0 + r1) // 2 // 16 * 16
    return (r0, m), (m, r1)


def _gather_to_neighbours(shard, land, r0, r1):
    src = shard.at[pl.ds(r0, r1 - r0)]
    dst = _rows(land, _me(), r0, r1)
    return [("local", src, dst)] + [("remote", src, dst, rel) for rel in (SIBLING, 4, 2)]


def _gather_relay(land, r0, r1):
    lo, hi = _split_rows(r0, r1)
    x_block, y_block = _me() ^ 4, _me() ^ 2
    return [("remote", _rows(land, x_block, *lo), _rows(land, x_block, *lo), 2),
            ("remote", _rows(land, y_block, *hi), _rows(land, y_block, *hi), 4),
            ("remote", _rows(land, x_block, r0, r1), _rows(land, x_block, r0, r1), SIBLING),
            ("remote", _rows(land, y_block, r0, r1), _rows(land, y_block, r0, r1), SIBLING)]


def _gather_diagonal_pass_on(land, r0, r1):
    rows = _rows(land, _me() ^ 6, r0, r1)
    return [("remote", rows, rows, SIBLING)]


def _pair_exchange(grad, land):
    _, _, c = _position()
    return [("remote", grad.at[2 * chip + (1 - c)], land.at[chip], SIBLING) for chip in range(4)]


def _chip_exchange(sums, land, r0, r1):
    return [("remote", _rows(sums, j, r0, r1), _rows(land, j, r0, r1), rel) for j, rel in enumerate(CHIPS)]


def _everyone(packed, land):
    dst = land.at[_me()]
    return [("local", packed, dst)] + [("remote", packed, dst, rel) for rel in range(1, N_DEV)]


def _pool_counts(row0, rows, win):
    pos = row0 + lax.broadcasted_iota(jnp.int32, (rows, 1), 0)
    return jnp.minimum(pos + 1, win).astype(F32)


def _window_sum_back(ext, win):
    s = ext
    k = 1
    while k < win:
        s = s + pltpu.roll(s, k, 0)
        k *= 2
    return s


def _window_sum_fwd(ext, win):
    n = ext.shape[0]
    s = ext
    k = 1
    while k < win:
        s = s + pltpu.roll(s, n - k, 0)
        k *= 2
    return s


def _mixer_fwd_call(proj, w_s, bs_t, g_v, w_pool_g, pool_scale, comm=None):
    t = proj.shape[0]
    tt = _tile(t, 512)
    nchunk = tt // CHUNK
    hb = tt // HALO

    def body(pu_ref, pv_ref, z_ref, zp_ref, ws_ref, bs_ref, gv_ref, wp_ref, ps_ref, out_ref):
        i = pl.program_id(0)
        tril = (lax.broadcasted_iota(jnp.int32, (CHUNK, CHUNK), 0)
                >= lax.broadcasted_iota(jnp.int32, (CHUNK, CHUNK), 1))
        for h in range(N_HEADS):
            cols = slice(h * CHUNK, (h + 1) * CHUNK)
            vhat, _ = _rms_rows(_gelu(pv_ref[:, cols]))
            vn = (vhat * gv_ref[:, cols]).astype(BF16)
            u = _gelu(pu_ref[:, cols])
            w = jnp.where(tril, ws_ref[h], 0.0).astype(BF16)
            bcol = bs_ref[:, h:h + 1]
            for c in range(nchunk):
                rows = slice(c * CHUNK, (c + 1) * CHUNK)
                mixed = _dot_nn(w, vn[rows]) + bcol
                out_ref[rows, cols] = (u[rows] * mixed).astype(BF16)

        zprev = jnp.where(i > 0, zp_ref[...], 0.0)
        ext = jnp.concatenate([zprev, z_ref[...]], axis=0)
        for g, win in enumerate(POOL_WINDOWS):
            cols = slice(g * GROUP, (g + 1) * GROUP)
            zg = ext[:, cols]
            s = _window_sum_back(zg, win)
            pooled = s[HALO:] / _pool_counts(i * tt, tt, win) - zg[HALO:]
            wp = wp_ref[:, g].reshape(GROUP, GROUP)
            y = _dot_nn(pooled.astype(BF16), wp)
            out_ref[:, A_WIDTH + g * GROUP:A_WIDTH + (g + 1) * GROUP] = (y * ps_ref[:, cols]).astype(BF16)

    return _carrier_call(
        body, (proj, proj, proj, proj, w_s, bs_t, g_v, w_pool_g, pool_scale), comm, name="mixer_fwd",
        grid=(t // tt,),
        in_specs=[pl.BlockSpec((tt, A_WIDTH), lambda i: (i, 0)),
                  pl.BlockSpec((tt, A_WIDTH), lambda i: (i, 1)),
                  pl.BlockSpec((tt, B_WIDTH), lambda i: (i, 2)),
                  pl.BlockSpec((HALO, B_WIDTH), lambda i: (jnp.maximum(i * hb - 1, 0), 2)),
                  pl.BlockSpec((N_HEADS, CHUNK, CHUNK), lambda i: (0, 0, 0)),
                  pl.BlockSpec((CHUNK, N_HEADS), lambda i: (0, 0)),
                  pl.BlockSpec((1, A_WIDTH), lambda i: (0, 0)),
                  pl.BlockSpec((N_DEV, 4, GROUP // N_DEV, GROUP), lambda i: (0, 0, 0, 0)),
                  pl.BlockSpec((1, B_WIDTH), lambda i: (0, 0))],
        out_specs=pl.BlockSpec((tt, A_WIDTH + B_WIDTH), lambda i: (i, 0)),
        out_shape=jax.ShapeDtypeStruct((t, A_WIDTH + B_WIDTH), BF16),
        sem=("parallel",))


def _out_proj_call(mixed, w_out, x, comm=None):
    t, d = x.shape
    k = mixed.shape[1]
    tm = _tile(t, 1024)
    tn = _tile(d, 1024)

    def body(a_ref, w_ref, x_ref, o_ref):
        o_ref[...] = x_ref[...] + _dot_nn(a_ref[...], w_ref[...])

    return _carrier_call(
        body, (mixed, w_out, x), comm, name="out_proj_fwd",
        grid=(t // tm, d // tn),
        in_specs=[pl.BlockSpec((tm, k), lambda i, j: (i, 0)),
                  pl.BlockSpec((k, tn), lambda i, j: (0, j)),
                  pl.BlockSpec((tm, tn), lambda i, j: (i, j))],
        out_specs=pl.BlockSpec((tm, tn), lambda i, j: (i, j)),
        out_shape=jax.ShapeDtypeStruct((t, d), F32),
        sem=("parallel", "parallel"))


ARRIVAL_ORDER = (0, 1, 4, 5, 2, 3, 6, 7)
CARRIED_AFTER = 3


class _StreamedGather:
    def __init__(self, shard_ref, land_ref, wbuf, pre0, r0, send_sems, recv_sems, local_sem, fetch_sems):
        self.shard, self.land, self.wbuf, self.fetch_sems = shard_ref, land_ref, wbuf, fetch_sems
        end = shard_ref.shape[0]
        me = _me()
        self.me = me

        def remote(k, src, dst, rel):
            return pltpu.make_async_remote_copy(src_ref=src, dst_ref=dst, send_sem=send_sems.at[k],
                                                recv_sem=recv_sems.at[k], device_id=_peer(rel), device_id_type=MESH)

        def same_rows(k, block, a, b, rel):
            ref = land_ref.at[block, pl.ds(a, b - a)]
            return remote(k, ref, ref, rel)

        src = shard_ref.at[pl.ds(r0, end - r0)]
        dst = land_ref.at[me, pl.ds(r0, end - r0)]
        self.mine = pltpu.make_async_copy(src, dst, local_sem)
        self.first = [remote(k, src, dst, rel) for k, rel in enumerate((SIBLING, 4, 2))]
        lo, hi = _split_rows(r0, end)
        self.relay = [same_rows(3, me ^ 4, *lo, 2), same_rows(4, me ^ 2, *hi, 4)]
        self.passed = [same_rows(5, me ^ 4, r0, end, SIBLING), same_rows(6, me ^ 2, r0, end, SIBLING),
                       same_rows(7, me ^ 6, pre0, end, SIBLING)]
        self.early_relay, self.early_passed = [], []
        if r0 > pre0:
            lo, hi = _split_rows(pre0, r0)
            self.early_relay = [same_rows(8, me ^ 4, *lo, 2), same_rows(9, me ^ 2, *hi, 4)]
            self.early_passed = [same_rows(10, me ^ 4, pre0, r0, SIBLING), same_rows(11, me ^ 2, pre0, r0, SIBLING)]

    def _fetch(self, q):
        src = self.shard if q == 0 else self.land.at[self.me ^ ARRIVAL_ORDER[q]]
        return pltpu.make_async_copy(src, self.wbuf.at[q % 2], self.fetch_sems.at[q % 2])

    def start(self):
        self.mine.start()
        for cp in self.first + self.early_relay + self.early_passed:
            cp.start()
        self._fetch(0).start()

    def arrive(self, q):
        if q == 1:
            self.first[0].wait_recv()
        elif q in (2, 4):
            j = q // 2 - 1
            self.first[1 + j].wait_recv()
            self.relay[j].start()
            self.passed[j].start()
        elif q in (3, 5):
            j = q // 2 - 1
            self.passed[j].wait_recv()
            if self.early_passed:
                self.early_passed[j].wait_recv()
        elif q == 6:
            for cp in self.relay + self.early_relay:
                cp.wait_recv()
            self.passed[2].start()
        else:
            self.passed[2].wait_recv()
        self._fetch(q).start()

    def wait_fetch(self, slot):
        pltpu.make_async_copy(self.shard, self.wbuf.at[slot], self.fetch_sems.at[slot]).wait()

    def finish(self):
        for cp in self.first + self.relay + self.passed + self.early_relay + self.early_passed:
            cp.wait_send()
        self.mine.wait()


_STREAM_SEMS = [pltpu.SemaphoreType.DMA((12,)), pltpu.SemaphoreType.DMA((12,)), pltpu.SemaphoreType.DMA,
                pltpu.SemaphoreType.DMA((2,))]


def _stream_steps(gather, p, i, ni):
    @pl.when((p == 0) & (i == 0))
    def _():
        gather.start()

    @pl.when(i == 0)
    def _():
        gather.wait_fetch(p % 2)

    @pl.when(i == ni - 1)
    def _():
        for q in range(1, N_DEV):
            @pl.when(p == q - 1)
            def _():
                gather.arrive(q)


def _norm_matmul_stream_call(name, x, g, shard, order, land, pre0, r0, comm, epilogue, out_dtypes):
    t, d = x.shape
    cb = shard.shape[1]
    tm = _tile(t, 1024)
    ni = t // tm
    n_sems = len(_STREAM_SEMS)
    assert not comm.lands
    ns, nn, no = len(comm.srcs), len(comm.new), len(out_dtypes)
    n_remote, n_local = _count_copies(comm)
    has_land = land is not None

    def body(order_ref, x_ref, g_ref, shard_ref, *refs):
        refs = refs[has_land:]
        srcs, out_refs, (h_ref, land_ref) = refs[:ns], refs[ns:ns + no], refs[ns + no:ns + no + 2]
        new = refs[ns + no + 2:ns + no + 2 + nn]
        wbuf, sems = refs[ns + no + 2 + nn], refs[ns + no + 3 + nn:]
        p, i = pl.program_id(0), pl.program_id(1)
        gather = _StreamedGather(shard_ref, land_ref, wbuf, pre0, r0, *sems[:n_sems])
        (carried,) = _make_copies(comm.plan(srcs, new), *sems[n_sems:])
        rows = pl.ds(pl.multiple_of(i * tm, tm), tm)
        _stream_steps(gather, p, i, ni)

        @pl.when((p == CARRIED_AFTER) & (i == ni - 1))
        def _():
            for cp in carried:
                cp.start()

        @pl.when(p == 0)
        def _():
            n, _ = _rms_rows(x_ref[...])
            h_ref[rows, :] = (n * g_ref[...]).astype(BF16)

        tails = epilogue(_dot_nn(h_ref[rows, :], wbuf[p % 2]))
        for out_ref, tail, dt in zip(out_refs, tails, out_dtypes):
            out_ref[...] = tail.astype(dt)

        @pl.when((p == N_DEV - 1) & (i == ni - 1))
        def _():
            gather.finish()
            for cp in carried:
                cp.wait()

    carried_sems = [pltpu.SemaphoreType.DMA((max(n_remote, 1),)), pltpu.SemaphoreType.DMA((max(n_remote, 1),)),
                    pltpu.SemaphoreType.DMA((max(n_local, 1),))]
    return pl.pallas_call(
        body, name=name,
        grid_spec=pltpu.PrefetchScalarGridSpec(
            num_scalar_prefetch=1, grid=(N_DEV, ni),
            in_specs=[pl.BlockSpec((tm, d), lambda p, i, o: (jnp.where(p == 0, i, ni - 1), 0)),
                      pl.BlockSpec((1, d), lambda p, i, o: (0, 0)),
                      _ANY] + [_ANY] * (has_land + ns),
            out_specs=[pl.BlockSpec((tm, cb), lambda p, i, o: (i, o[p]))] * no
                      + [pl.BlockSpec(memory_space=pltpu.VMEM), _ANY] + [_ANY] * nn,
            scratch_shapes=[pltpu.VMEM((2, d, cb), BF16)] + _STREAM_SEMS + carried_sems),
        out_shape=[jax.ShapeDtypeStruct((t, N_DEV * cb), dt) for dt in out_dtypes]
                  + [jax.ShapeDtypeStruct((t, d), BF16), jax.ShapeDtypeStruct((N_DEV, d, cb), BF16)] + comm.new,
        input_output_aliases={4: no + 1} if has_land else {},
        compiler_params=_params("arbitrary", "arbitrary"),
    )(order, x, g, shard, *([land] if has_land else []), *comm.srcs)


def _down_call(act, shard, order, land, r0):
    t = act.shape[0]
    rb, d = shard.shape
    tm = _tile(t, 1024)
    ni = t // tm

    def body(order_ref, a_ref, shard_ref, land_in_ref, y_ref, land_ref, wbuf, *sems):
        p, i = pl.program_id(0), pl.program_id(1)
        gather = _StreamedGather(shard_ref, land_ref, wbuf, 0, r0, *sems)
        rows = pl.ds(pl.multiple_of(i * tm, tm), tm)
        _stream_steps(gather, p, i, ni)
        @pl.when(p == 0)
        def _():
            y_ref[rows, :] = _dot_nn(a_ref[...], wbuf[0])

        @pl.when(p > 0)
        def _():
            y_ref[rows, :] += _dot_nn(a_ref[...], wbuf[p % 2])

        @pl.when((p == N_DEV - 1) & (i == ni - 1))
        def _():
            gather.finish()

    return pl.pallas_call(
        body, name="down_fwd",
        grid_spec=pltpu.PrefetchScalarGridSpec(
            num_scalar_prefetch=1, grid=(N_DEV, ni),
            in_specs=[pl.BlockSpec((tm, rb), lambda p, i, o: (i, o[p])), _ANY, _ANY],
            out_specs=[pl.BlockSpec(memory_space=pltpu.VMEM), _ANY],
            scratch_shapes=[pltpu.VMEM((2, rb, d), BF16)] + _STREAM_SEMS),
        out_shape=[jax.ShapeDtypeStruct((t, d), F32), jax.ShapeDtypeStruct((N_DEV, rb, d), BF16)],
        input_output_aliases={3: 1},
        compiler_params=_params("arbitrary", "arbitrary"),
    )(order, act, shard, land)


def _loss_call(y, x2, target, g_final):
    t, d = y.shape
    tr = _tile(t, 512)

    def body(y_ref, x_ref, tg_ref, g_ref, loss_ref, dx_ref, dxb_ref, dg_ref):
        @pl.when(pl.program_id(0) == 0)
        def _():
            loss_ref[...] = jnp.zeros_like(loss_ref)
            dg_ref[...] = jnp.zeros_like(dg_ref)

        n, r = _rms_rows(x_ref[...] + y_ref[...])
        err = n * g_ref[...] - tg_ref[...]
        loss_ref[...] += 0.5 * jnp.sum(jnp.mean(err * err, axis=-1, keepdims=True))
        dy = err * (1.0 / d)
        dg_ref[...] += jnp.sum(dy * n, axis=0, keepdims=True)
        dx = _rms_bwd_rows(dy * g_ref[...], n, r)
        dx_ref[...] = dx
        dxb_ref[...] = dx.astype(BF16)

    return pl.pallas_call(
        body, name="loss_head",
        grid=(t // tr,),
        in_specs=[pl.BlockSpec((tr, d), lambda i: (i, 0)),
                  pl.BlockSpec((tr, d), lambda i: (i, 0)),
                  pl.BlockSpec((tr, d), lambda i: (i, 0)),
                  pl.BlockSpec((1, d), lambda i: (0, 0))],
        out_specs=[pl.BlockSpec((8, LANES), lambda i: (0, 0)),
                   pl.BlockSpec((tr, d), lambda i: (i, 0)),
                   pl.BlockSpec((tr, d), lambda i: (i, 0)),
                   pl.BlockSpec((1, d), lambda i: (0, 0))],
        out_shape=[jax.ShapeDtypeStruct((8, LANES), F32), jax.ShapeDtypeStruct((t, d), F32),
                   jax.ShapeDtypeStruct((t, d), BF16), jax.ShapeDtypeStruct((1, d), F32)],
        compiler_params=_params("arbitrary"),
    )(y, x2, target, g_final)


def _norm_bwd_call(name, dh, x, dres, g, want_bf16, comm=None):
    t, d = x.shape
    tr = _tile(t, 256)

    def body(dh_ref, x_ref, dres_ref, g_ref, dx_ref, *rest):
        dg_ref = rest[-1]

        @pl.when(pl.program_id(0) == 0)
        def _():
            dg_ref[...] = jnp.zeros_like(dg_ref)

        n, r = _rms_rows(x_ref[...])
        dh = dh_ref[...]
        dg_ref[...] += jnp.sum(dh * n, axis=0, keepdims=True)
        dx = dres_ref[...] + _rms_bwd_rows(dh * g_ref[...], n, r)
        dx_ref[...] = dx
        if want_bf16:
            rest[0][...] = dx.astype(BF16)

    row = pl.BlockSpec((tr, d), lambda i: (i, 0))
    vec = pl.BlockSpec((1, d), lambda i: (0, 0))
    out_specs = [row] + ([row] if want_bf16 else []) + [vec]
    out_shape = ([jax.ShapeDtypeStruct((t, d), F32)]
                 + ([jax.ShapeDtypeStruct((t, d), BF16)] if want_bf16 else [])
                 + [jax.ShapeDtypeStruct((1, d), F32)])
    return _carrier_call(
        body, (dh, x, dres, g), comm, name=name,
        grid=(t // tr,),
        in_specs=[row, row, row, vec],
        out_specs=out_specs, out_shape=out_shape,
        sem=("arbitrary",))


def _dact_call(dx3b, w_down, act, comm=None):
    t, d = dx3b.shape
    f = w_down.shape[0]
    tm = _tile(t, 1024)
    tn = _tile(f, 2048)

    def body(g_ref, w_ref, act_ref, o_ref):
        dact = _dot_nt(g_ref[...], w_ref[...])
        o_ref[...] = (dact * act_ref[...].astype(F32)).astype(BF16)

    return _carrier_call(
        body, (dx3b, w_down, act), comm, name="dact_bwd",
        grid=(t // tm, f // tn),
        in_specs=[pl.BlockSpec((tm, d), lambda i, j: (i, 0)),
                  pl.BlockSpec((tn, d), lambda i, j: (j, 0)),
                  pl.BlockSpec((tm, tn), lambda i, j: (i, j))],
        out_specs=pl.BlockSpec((tm, tn), lambda i, j: (i, j)),
        out_shape=jax.ShapeDtypeStruct((t, f), BF16),
        sem=("parallel", "parallel"))


def _wgrad_call(name, a, b, out_blocks, out_block_cols, comm=None, *, t1, t2=None, merge=1):
    t, k1 = a.shape
    k2 = b.shape[1]
    tt = _tile(t, 2048)
    t1 = _tile(k1, t1)
    t2 = _tile(k2, t2) if out_blocks is None else merge * out_block_cols
    nk = t // tt

    def body(a_ref, b_ref, o_ref, acc_ref):
        k = pl.program_id(2)

        @pl.when(k == 0)
        def _():
            acc_ref[...] = _dot_tn(a_ref[...], b_ref[...])

        @pl.when(k > 0)
        def _():
            acc_ref[...] += _dot_tn(a_ref[...], b_ref[...])

        @pl.when(k == nk - 1)
        def _():
            if out_blocks is None:
                o_ref[...] = acc_ref[...].astype(BF16)
            else:
                for blk in range(merge):
                    o_ref[blk] = acc_ref[:, blk * out_block_cols:(blk + 1) * out_block_cols].astype(BF16)

    if out_blocks is None:
        out_spec = pl.BlockSpec((t1, t2), lambda i, j, k: (i, j))
        out_shape = jax.ShapeDtypeStruct((k1, k2), BF16)
    else:
        out_spec = pl.BlockSpec((merge, t1, out_block_cols), lambda i, j, k: (j, i, 0))
        out_shape = jax.ShapeDtypeStruct((out_blocks, k1, out_block_cols), BF16)
    return _carrier_call(
        body, (a, b), comm, name=name,
        grid=(k1 // t1, k2 // t2, nk),
        in_specs=[pl.BlockSpec((tt, t1), lambda i, j, k: (k, i)),
                  pl.BlockSpec((tt, t2), lambda i, j, k: (k, j))],
        out_specs=out_spec, out_shape=out_shape,
        scratch_shapes=[pltpu.VMEM((t1, t2), F32)],
        sem=("parallel", "parallel", "arbitrary"))


def _dgrad_blocked_call(name, g, w_g, comm=None, *, merge=1):
    t = g.shape[0]
    nb, d, cb = w_g.shape
    tm = _tile(t, 1024)
    tn = _tile(d, 2048)
    tk = merge * cb

    def body(g_ref, w_ref, o_ref):
        def product():
            w = w_ref[0] if merge == 1 else jnp.concatenate([w_ref[b] for b in range(merge)], axis=1)
            return _dot_nt(g_ref[...], w)

        @pl.when(pl.program_id(2) == 0)
        def _():
            o_ref[...] = product()

        @pl.when(pl.program_id(2) > 0)
        def _():
            o_ref[...] += product()

    return _carrier_call(
        body, (g, w_g), comm, name=name,
        grid=(t // tm, d // tn, nb // merge),
        in_specs=[pl.BlockSpec((tm, tk), lambda i, j, k: (i, k)),
                  pl.BlockSpec((merge, tn, cb), lambda i, j, k: (k, j, 0))],
        out_specs=pl.BlockSpec((tm, tn), lambda i, j, k: (i, j)),
        out_shape=jax.ShapeDtypeStruct((t, d), F32),
        sem=("parallel", "parallel", "arbitrary"))


def _dmixed_call(dx2b, w_out, comm=None):
    t, d = dx2b.shape
    e = w_out.shape[0]
    tm = _tile(t, 1024)
    tn = _tile(e, 1024)

    def body(g_ref, w_ref, o_ref):
        o_ref[...] = _dot_nt(g_ref[...], w_ref[...])

    return _carrier_call(
        body, (dx2b, w_out), comm, name="dmixed_bwd",
        grid=(t // tm, e // tn),
        in_specs=[pl.BlockSpec((tm, d), lambda i, j: (i, 0)),
                  pl.BlockSpec((tn, d), lambda i, j: (j, 0))],
        out_specs=pl.BlockSpec((tm, tn), lambda i, j: (i, j)),
        out_shape=jax.ShapeDtypeStruct((t, e), F32),
        sem=("parallel", "parallel"))


def _mixer_bwd_call(proj, dmixed, w_s, bs_t, g_v, w_pool_g, pool_scale, comm=None):
    t = proj.shape[0]
    tt = _tile(t, 512)
    nchunk = tt // CHUNK
    hb = tt // HALO
    last_halo = t // HALO - 1
    nsteps = t // tt
    rb = GROUP // N_DEV

    def body(pu_ref, pv_ref, z_ref, zp_ref, da_ref, db_ref, dbn_ref, ws_ref, bs_ref, gv_ref, wp_ref, ps_ref,
             dproj_ref, dws_ref, dbs_ref, dgv_ref, dps_ref, dwp_ref):
        i = pl.program_id(0)

        @pl.when(i == 0)
        def _():
            dws_ref[...] = jnp.zeros_like(dws_ref)
            dbs_ref[...] = jnp.zeros_like(dbs_ref)
            dgv_ref[...] = jnp.zeros_like(dgv_ref)
            dps_ref[...] = jnp.zeros_like(dps_ref)
            dwp_ref[...] = jnp.zeros_like(dwp_ref)

        tril = (lax.broadcasted_iota(jnp.int32, (CHUNK, CHUNK), 0)
                >= lax.broadcasted_iota(jnp.int32, (CHUNK, CHUNK), 1))
        for h in range(N_HEADS):
            cols = slice(h * CHUNK, (h + 1) * CHUNK)
            v, dv_dpv = _gelu_and_grad(pv_ref[:, cols])
            vhat, rv = _rms_rows(v)
            gv = gv_ref[:, cols]
            vn = (vhat * gv).astype(BF16)
            u, du_dpu = _gelu_and_grad(pu_ref[:, cols])
            w = jnp.where(tril, ws_ref[h], 0.0).astype(BF16)
            bcol = bs_ref[:, h:h + 1]
            dout = da_ref[:, cols]
            dmix = dout * u
            dmix_b = dmix.astype(BF16)
            dws = jnp.zeros((CHUNK, CHUNK), F32)
            dbs = jnp.zeros((CHUNK, 1), F32)
            dvn_parts = []
            du_parts = []
            for c in range(nchunk):
                rows = slice(c * CHUNK, (c + 1) * CHUNK)
                mixed = _dot_nn(w, vn[rows]) + bcol
                du_parts.append(dout[rows] * mixed)
                dvn_parts.append(_dot_tn(w, dmix_b[rows]))
                dws = dws + _dot_nt(dmix_b[rows], vn[rows])
                dbs = dbs + jnp.sum(dmix[rows], axis=1, keepdims=True)
            dws_ref[h] += jnp.where(tril, dws, 0.0)
            dbs_ref[:, h:h + 1] += dbs
            dvn = jnp.concatenate(dvn_parts, axis=0)
            du = jnp.concatenate(du_parts, axis=0)
            dgv_ref[:, cols] += jnp.sum(dvn * vhat, axis=0, keepdims=True)
            dv = _rms_bwd_rows(dvn * gv, vhat, rv)
            dproj_ref[:, cols] = (du * du_dpu).astype(BF16)
            dproj_ref[:, A_WIDTH + h * CHUNK:A_WIDTH + (h + 1) * CHUNK] = (dv * dv_dpv).astype(BF16)

        zprev = jnp.where(i > 0, zp_ref[...], 0.0)
        ext = jnp.concatenate([zprev, z_ref[...]], axis=0)
        dnext = jnp.where(i < nsteps - 1, dbn_ref[...], 0.0)
        dext = jnp.concatenate([db_ref[...], dnext], axis=0)
        for g, win in enumerate(POOL_WINDOWS):
            cols = slice(g * GROUP, (g + 1) * GROUP)
            zg = ext[:, cols]
            pooled = _window_sum_back(zg, win)[HALO:] / _pool_counts(i * tt, tt, win) - zg[HALO:]
            pooled_b = pooled.astype(BF16)
            wp = wp_ref[:, g].reshape(GROUP, GROUP)
            y = _dot_nn(pooled_b, wp)
            dout = dext[:, cols]
            dps_ref[:, cols] += jnp.sum(dout[:tt] * y, axis=0, keepdims=True)
            dy_b = (dout * ps_ref[:, cols]).astype(BF16)
            dwp_ref[:, g] += _dot_tn(pooled_b, dy_b[:tt]).reshape(N_DEV, rb, GROUP)
            dpooled = _dot_nt(dy_b, wp)
            q = dpooled / _pool_counts(i * tt, tt + HALO, win)
            dz = _window_sum_fwd(q, win)[:tt] - dpooled[:tt]
            dproj_ref[:, 2 * A_WIDTH + g * GROUP:2 * A_WIDTH + (g + 1) * GROUP] = dz.astype(BF16)

    def full(shape):
        return pl.BlockSpec(shape, lambda i: (0,) * len(shape))

    return _carrier_call(
        body, (proj, proj, proj, proj, dmixed, dmixed, dmixed, w_s, bs_t, g_v, w_pool_g, pool_scale), comm,
        name="mixer_bwd",
        grid=(nsteps,),
        in_specs=[pl.BlockSpec((tt, A_WIDTH), lambda i: (i, 0)),
                  pl.BlockSpec((tt, A_WIDTH), lambda i: (i, 1)),
                  pl.BlockSpec((tt, B_WIDTH), lambda i: (i, 2)),
                  pl.BlockSpec((HALO, B_WIDTH), lambda i: (jnp.maximum(i * hb - 1, 0), 2)),
                  pl.BlockSpec((tt, A_WIDTH), lambda i: (i, 0)),
                  pl.BlockSpec((tt, B_WIDTH), lambda i: (i, 1)),
                  pl.BlockSpec((HALO, B_WIDTH), lambda i: (jnp.minimum((i + 1) * hb, last_halo), 1)),
                  full((N_HEADS, CHUNK, CHUNK)), full((CHUNK, N_HEADS)), full((1, A_WIDTH)),
                  full((N_DEV, 4, rb, GROUP)), full((1, B_WIDTH))],
        out_specs=[pl.BlockSpec((tt, 2 * A_WIDTH + B_WIDTH), lambda i: (i, 0)),
                   full((N_HEADS, CHUNK, CHUNK)), full((CHUNK, N_HEADS)), full((1, A_WIDTH)),
                   full((1, B_WIDTH)), full((N_DEV, 4, rb, GROUP))],
        out_shape=[jax.ShapeDtypeStruct((t, 2 * A_WIDTH + B_WIDTH), BF16),
                   jax.ShapeDtypeStruct((N_HEADS, CHUNK, CHUNK), F32),
                   jax.ShapeDtypeStruct((CHUNK, N_HEADS), F32),
                   jax.ShapeDtypeStruct((1, A_WIDTH), F32),
                   jax.ShapeDtypeStruct((1, B_WIDTH), F32),
                   jax.ShapeDtypeStruct((N_DEV, 4, rb, GROUP), F32)],
        sem=("arbitrary",))


def _adamw(w, g, m, v):
    m = ADAM_B1 * m + (1.0 - ADAM_B1) * g
    v = ADAM_B2 * v + (1.0 - ADAM_B2) * (g * g)
    m_hat = m / ADAM_C1
    v_hat = v / ADAM_C2
    delta = -ADAM_LR * (m_hat / (jnp.sqrt(v_hat) + ADAM_EPS) + ADAM_WD * w)
    return delta, m, v


PAIR_SUM_TILE_ELEMS = 1024 * 1024
ADAMW_TILE_ELEMS = 512 * 1024


def _row_tile(r, c, elems):
    t = r
    while t * c > elems and t % 32 == 0:
        t //= 2
    return t


def _pair_sum_call(name, pos, grad, got):
    _, r, c = grad.shape
    tr = _row_tile(r, c, PAIR_SUM_TILE_ELEMS)

    def chip_of(rel, pos_ref):
        px = jnp.where((rel == 0) | (rel == 2), 1 - pos_ref[0], pos_ref[0])
        py = jnp.where((rel == 1) | (rel == 2), 1 - pos_ref[1], pos_ref[1])
        return 2 * px + py

    def body(pos_ref, own_ref, got_ref, out_ref):
        out_ref[...] = (own_ref[...].astype(F32) + got_ref[...].astype(F32)).astype(BF16)

    return pl.pallas_call(
        body, name=name,
        grid_spec=pltpu.PrefetchScalarGridSpec(
            num_scalar_prefetch=1, grid=(3, r // tr),
            in_specs=[pl.BlockSpec((None, tr, c), lambda k, i, p: (2 * chip_of(k, p) + p[2], i, 0)),
                      pl.BlockSpec((None, tr, c), lambda k, i, p: (chip_of(k, p), i, 0))],
            out_specs=pl.BlockSpec((None, tr, c), lambda k, i, p: (k, i, 0))),
        out_shape=jax.ShapeDtypeStruct((3, r, c), BF16),
        compiler_params=_params("parallel", "parallel"),
    )(pos, grad, got)


def _final_call(name, pos, grad, got_pair, got_chips, w, m, v):
    _, r, c = grad.shape
    tr = _row_tile(r, c, ADAMW_TILE_ELEMS)

    def body(pos_ref, own_ref, pair_ref, chips_ref, w_ref, m_ref, v_ref, g_out, d_out, m_out, v_out):
        g = own_ref[...].astype(F32) + pair_ref[...].astype(F32)
        for j in range(3):
            g = g + chips_ref[j].astype(F32)
        delta, m_new, v_new = _adamw(w_ref[...], g, m_ref[...], v_ref[...])
        g_out[...] = g
        d_out[...] = delta
        m_out[...] = m_new
        v_out[...] = v_new

    row = pl.BlockSpec((tr, c), lambda i, p: (i, 0))
    return pl.pallas_call(
        body, name=name,
        grid_spec=pltpu.PrefetchScalarGridSpec(
            num_scalar_prefetch=1, grid=(r // tr,),
            in_specs=[pl.BlockSpec((None, tr, c), lambda i, p: (4 * p[0] + 2 * p[1] + p[2], i, 0)),
                      pl.BlockSpec((None, tr, c), lambda i, p: (2 * p[0] + p[1], i, 0)),
                      pl.BlockSpec((3, tr, c), lambda i, p: (0, i, 0)), row, row, row],
            out_specs=[row] * 4),
        out_shape=[jax.ShapeDtypeStruct((r, c), F32)] * 4,
        compiler_params=_params("parallel"),
    )(pos, grad, got_pair, got_chips, w, m, v)


def _small_final_call(name, parts, w, m, v):
    _, rows, c = parts.shape
    r = w.shape[0]

    def body(p_ref, w_ref, m_ref, v_ref, g_out, d_out, m_out, v_out):
        g = p_ref[0]
        for k in range(1, N_DEV):
            g = g + p_ref[k]
        delta, m_new, v_new = _adamw(w_ref[...], g[:r], m_ref[...], v_ref[...])
        g_out[...] = g
        d_out[...] = delta
        m_out[...] = m_new
        v_out[...] = v_new

    return pl.pallas_call(
        body, name=name,
        out_shape=[jax.ShapeDtypeStruct((rows, c), F32)] + [jax.ShapeDtypeStruct((r, c), F32)] * 3,
        compiler_params=pltpu.CompilerParams(vmem_limit_bytes=VMEM_LIMIT),
    )(parts, w, m, v)


_SMALL_EARLY = ("g_v", "w_s", "b_s", "pool_scale", "g_ffn", "g_final")
_BIG = ("w_in", "w_pool", "w_out", "w_up", "w_down")
_ORDER = ("g_mix", "w_in", "g_v", "w_s", "b_s", "w_pool", "pool_scale", "w_out", "g_ffn", "w_up", "w_down", "g_final")


def _pack(parts):
    return jnp.concatenate([p.reshape(-1, LANES) for p in parts], axis=0)


def _unpack(packed, like):
    out, row = [], 0
    for a in like:
        rows = a.size // LANES
        out.append(packed[row:row + rows].reshape(a.shape))
        row += rows
    return out


def kernel(x, g_mix, w_in, g_v, w_s, b_s, w_pool, pool_scale, w_out, g_ffn, w_up, w_down, g_final, loss_target, m_g_mix, m_w_in, m_g_v, m_w_s, m_b_s, m_w_pool, m_pool_scale, m_w_out, m_g_ffn, m_w_up, m_w_down, m_g_final, v_g_mix, v_w_in, v_g_v, v_w_s, v_b_s, v_w_pool, v_pool_scale, v_w_out, v_g_ffn, v_w_up, v_w_down, v_g_final):
    weights = dict(g_mix=g_mix, w_in=w_in, g_v=g_v, w_s=w_s, b_s=b_s, w_pool=w_pool, pool_scale=pool_scale,
                   w_out=w_out, g_ffn=g_ffn, w_up=w_up, w_down=w_down, g_final=g_final)
    mom = dict(g_mix=m_g_mix, w_in=m_w_in, g_v=m_g_v, w_s=m_w_s, b_s=m_b_s, w_pool=m_w_pool,
               pool_scale=m_pool_scale, w_out=m_w_out, g_ffn=m_g_ffn, w_up=m_w_up, w_down=m_w_down,
               g_final=m_g_final)
    var = dict(g_mix=v_g_mix, w_in=v_w_in, g_v=v_g_v, w_s=v_w_s, b_s=v_b_s, w_pool=v_w_pool,
               pool_scale=v_pool_scale, w_out=v_w_out, g_ffn=v_g_ffn, w_up=v_w_up, w_down=v_w_down,
               g_final=v_g_final)

    t, d = x.shape[1], x.shape[2]
    xs = x.reshape(t, d)
    target = loss_target.reshape(t, d)

    shard2d = dict(w_in=w_in.reshape(d, -1), w_pool=w_pool.reshape(-1, GROUP), w_out=w_out.reshape(-1, d),
                   w_up=w_up.reshape(d, -1), w_down=w_down.reshape(-1, d))
    sb = {k: shard2d[k].astype(BF16) for k in _BIG}
    rows = {k: sb[k].shape[0] for k in _BIG}

    def gathered_shape(k):
        return jax.ShapeDtypeStruct((N_DEV,) + sb[k].shape, BF16)

    def landing(n, like):
        return jax.ShapeDtypeStruct((n,) + like.shape[1:], like.dtype)

    def from_everyone(block):
        return jax.ShapeDtypeStruct((N_DEV,) + block.shape, block.dtype)

    def cuts(r, fractions):
        return [0] + [int(r * f) // 16 * 16 for f in fractions] + [r]

    g_mix2, g_ffn2, g_final2 = g_mix.reshape(1, d), g_ffn.reshape(1, d), g_final.reshape(1, d)
    g_v2, ps2 = g_v.reshape(1, A_WIDTH), pool_scale.reshape(1, B_WIDTH)
    w_s3 = w_s.reshape(N_HEADS, CHUNK, CHUNK)
    bs_t = b_s.reshape(N_HEADS, CHUNK).T
    xi, yi, ci = _position()
    pos = jnp.stack([xi, yi, ci]).astype(jnp.int32)

    order = (4 * xi + 2 * yi + ci) ^ jnp.array(ARRIVAL_ORDER, jnp.int32)
    u = cuts(rows["w_up"], (0.25, 0.55))
    ahead = cuts(rows["w_down"], (0.6,))[1]
    proj, h1, w_in_g, w_out_g, w_pool_g = _norm_matmul_stream_call(
        "proj_fwd", xs, g_mix2, sb["w_in"], order, None, 0, 0, _Comm(
            [sb["w_out"], sb["w_pool"]], [], [gathered_shape("w_out"), gathered_shape("w_pool")],
            lambda s, l: [_gather_to_neighbours(s[0], l[0], 0, rows["w_out"]) + _everyone(s[1], l[1])]),
        lambda a: (a,), (F32,))
    w_pool_g = w_pool_g.reshape(N_DEV, 4, GROUP // N_DEV, GROUP)
    mixed, w_out_g, w_up_g = _mixer_fwd_call(proj, w_s3, bs_t, g_v2, w_pool_g, ps2, _Comm(
        [sb["w_up"]], [w_out_g], [gathered_shape("w_up")],
        lambda s, l: [_gather_relay(l[0], 0, rows["w_out"]) + _gather_to_neighbours(s[0], l[1], u[0], u[1]),
                      _gather_diagonal_pass_on(l[0], 0, rows["w_out"])]))
    w_out_f = w_out_g.reshape(-1, d)
    x2, w_up_g = _out_proj_call(mixed, w_out_f, xs, _Comm(
        [sb["w_up"]], [w_up_g], [],
        lambda s, l: [_gather_relay(l[0], u[0], u[1]) + _gather_to_neighbours(s[0], l[0], u[1], u[2]),
                      _gather_diagonal_pass_on(l[0], u[0], u[1])]))

    def relu2_and_slope(a):
        r = jnp.maximum(a, 0.0)
        return r * r, 2.0 * r

    act, dact_da, h2, w_up_g, w_down_g = _norm_matmul_stream_call(
        "up_fwd", x2, g_ffn2, sb["w_up"], order, w_up_g, u[1], u[2], _Comm(
            [sb["w_down"]], [], [gathered_shape("w_down")],
            lambda s, l: [_gather_to_neighbours(s[0], l[0], 0, ahead)]),
        relu2_and_slope, (BF16, BF16))
    y, w_down_g = _down_call(act, sb["w_down"], order, w_down_g, ahead)
    w_down_f = w_down_g.reshape(-1, d)
    loss_part, dx3, dx3b, dg_final = _loss_call(y, x2, target, g_final2)

    def pair_sum(k, grad, got):
        return _pair_sum_call(k + "_pair_sum", pos, grad, got)

    def finish(k, grad, got_pair, got_chips):
        s = shard2d[k]
        outs = _final_call(k + "_adamw", pos, grad, got_pair, got_chips, s, mom[k].reshape(s.shape),
                           var[k].reshape(s.shape))
        return [o.reshape(weights[k].shape) for o in outs]

    result = {}
    (gw_down,) = _wgrad_call("w_down_grad", act, dx3b, None, None, t1=1024, t2=2048)
    gw_down = gw_down.reshape(N_DEV, -1, d)
    da, pair_down = _dact_call(dx3b, w_down_f, dact_da, _Comm(
        [gw_down], [], [landing(4, gw_down)], lambda s, l: [_pair_exchange(s[0], l[0])]))
    sums_down = pair_sum("w_down", gw_down, pair_down)
    dn = cuts(rows["w_down"], (0.8,))
    gw_up, got = _wgrad_call("w_up_grad", h2, da, N_DEV, w_up_g.shape[2], _Comm(
        [sums_down], [], [landing(3, sums_down)],
        lambda s, l: [_chip_exchange(s[0], l[0], dn[0], dn[1])]), t1=2048)
    dh2, got, pair_up = _dgrad_blocked_call("dh2_bwd", da, w_up_g, _Comm(
        [sums_down, gw_up], [got], [landing(4, gw_up)],
        lambda s, l: [_chip_exchange(s[0], l[0], dn[1], dn[2]) + _pair_exchange(s[1], l[1])]), merge=2)
    result["w_down"] = finish("w_down", gw_down, pair_down, got)
    sums_up = pair_sum("w_up", gw_up, pair_up)
    v = cuts(rows["w_up"], (0.28, 0.52, 0.84))
    dx2, dx2b, dg_ffn, got_up = _norm_bwd_call("ffn_norm_bwd", dh2, x2, dx3, g_ffn2, True, _Comm(
        [sums_up], [], [landing(3, sums_up)], lambda s, l: [_chip_exchange(s[0], l[0], v[0], v[1])]))
    dmixed, got_up = _dmixed_call(dx2b, w_out_f, _Comm(
        [sums_up], [got_up], [], lambda s, l: [_chip_exchange(s[0], l[0], v[1], v[2])]))
    gw_out, got_up = _wgrad_call("w_out_grad", mixed, dx2b, None, None, _Comm(
        [sums_up], [got_up], [], lambda s, l: [_chip_exchange(s[0], l[0], v[2], v[3])]), t1=2048, t2=1024)
    gw_out = gw_out.reshape(N_DEV, -1, d)
    dproj, dw_s, dbs_t, dg_v, dps, dw_pool, got_up, pair_out = _mixer_bwd_call(
        proj, dmixed, w_s3, bs_t, g_v2, w_pool_g, ps2, _Comm(
            [sums_up, gw_out], [got_up], [landing(4, gw_out)],
            lambda s, l: [_chip_exchange(s[0], l[0], v[3], v[4]) + _pair_exchange(s[1], l[1])]))
    result["w_up"] = finish("w_up", gw_up, pair_up, got_up)
    sums_out = pair_sum("w_out", gw_out, pair_out)
    gw_pool = dw_pool.astype(BF16).reshape(N_DEV, -1, GROUP)
    early = dict(g_v=dg_v, w_s=dw_s, b_s=dbs_t.T, pool_scale=dps, g_ffn=dg_ffn, g_final=dg_final)
    packed = _pack([early[k] for k in _SMALL_EARLY] + [loss_part])
    early_rows = packed.shape[0]
    gw_in, got, pair_pool, parts_early = _wgrad_call("w_in_grad", h1, dproj, N_DEV, w_in_g.shape[2], _Comm(
        [sums_out, gw_pool, packed], [], [landing(3, sums_out), landing(4, gw_pool), from_everyone(packed)],
        lambda s, l: [_chip_exchange(s[0], l[0], 0, rows["w_out"]) + _pair_exchange(s[1], l[1])
                      + _gather_first(s[2], l[2], 0, early_rows)]), t1=2048, merge=MERGE_W_IN)
    result["w_out"] = finish("w_out", gw_out, pair_out, got)
    sums_pool = pair_sum("w_pool", gw_pool, pair_pool)
    (pair_in,) = _comm_call("pair_exchange_w_in", _Comm(
        [gw_in], [], [landing(4, gw_in)], lambda s, l: [_pair_exchange(s[0], l[0])]))
    sums_in = pair_sum("w_in", gw_in, pair_in)
    dh1, parts_early, got, got_pool = _dgrad_blocked_call("dh1_bwd", dproj, w_in_g, _Comm(
        [sums_in, sums_pool], [parts_early], [landing(3, sums_in), landing(3, sums_pool)],
        lambda s, l: [_chip_exchange(s[0], l[1], 0, rows["w_in"]) + _chip_exchange(s[1], l[2], 0, rows["w_pool"])
                      + _gather_pass_on(l[0], 0, early_rows)]),
        merge=MERGE_W_IN)
    result["w_in"] = finish("w_in", gw_in, pair_in, got)
    result["w_pool"] = finish("w_pool", gw_pool, pair_pool, got_pool)
    grad_x, dg_mix = _norm_bwd_call("mix_norm_bwd", dh1, xs, dx2, g_mix2, False)
    packed = _pack([dg_mix])
    (parts_late,) = _comm_call("gather_g_mix_grad", _Comm(
        [packed], [], [from_everyone(packed)], lambda s, l: [_everyone(s[0], l[0])]))

    for names, parts, tag in ((_SMALL_EARLY, parts_early, "small_adamw"), (("g_mix",), parts_late, "g_mix_adamw")):
        outs = _small_final_call(tag, parts, _pack([weights[k] for k in names]), _pack([mom[k] for k in names]),
                                 _pack([var[k] for k in names]))
        if tag == "small_adamw":
            loss = outs[0][-1, 0]
        like = [weights[k] for k in names]
        unpacked = [_unpack(o, like) for o in outs]
        for idx, k in enumerate(names):
            result[k] = [unpacked[q][idx] for q in range(4)]

    grads = [result[k][0] for k in _ORDER]
    deltas = [result[k][1] for k in _ORDER]
    new_m = [result[k][2] for k in _ORDER]
    new_v = [result[k][3] for k in _ORDER]
    return (loss, grad_x.reshape(x.shape), *grads, *deltas, *new_m, *new_v)
```

```python
import functools
import math

import jax
import jax.numpy as jnp
from jax import lax
from jax.experimental import pallas as pl
from jax.experimental.pallas import tpu as pltpu

F32 = jnp.float32
BF16 = jnp.bfloat16
MESH = pl.DeviceIdType.MESH

N_DEV = 8
EPS = 1e-6
CHUNK = 128
N_HEADS = 8
A_WIDTH = 1024
B_WIDTH = 1024
POOL_WINDOWS = (2, 4, 8, 16)
GROUP = 256
HALO = 16
LANES = 128

ADAM_LR = 0.001
ADAM_B1 = 0.9
ADAM_B2 = 0.999
ADAM_EPS = 1e-08
ADAM_WD = 0.01
ADAM_STEP = 10
ADAM_C1 = 1.0 - ADAM_B1 ** ADAM_STEP
ADAM_C2 = 1.0 - ADAM_B2 ** ADAM_STEP

VMEM_LIMIT = 56 * 1024 * 1024
MERGE_W_IN = 2

_GELU_C = math.sqrt(2.0 / math.pi)


def _params(*sem):
    return pltpu.CompilerParams(dimension_semantics=sem, vmem_limit_bytes=VMEM_LIMIT)


def _gelu(x):
    return 0.5 * x * (1.0 + jnp.tanh(_GELU_C * (x + 0.044715 * x * x * x)))


def _gelu_and_grad(x):
    t = jnp.tanh(_GELU_C * (x + 0.044715 * x * x * x))
    g = 0.5 * x * (1.0 + t)
    dg = 0.5 * (1.0 + t) + 0.5 * x * (1.0 - t * t) * (_GELU_C * (1.0 + 3.0 * 0.044715 * x * x))
    return g, dg


def _dot_nn(a, b):
    return lax.dot_general(a, b, (((1,), (0,)), ((), ())), preferred_element_type=F32)


def _dot_nt(a, b):
    return lax.dot_general(a, b, (((1,), (1,)), ((), ())), preferred_element_type=F32)


def _dot_tn(a, b):
    return lax.dot_general(a, b, (((0,), (0,)), ((), ())), preferred_element_type=F32)


def _rms_rows(x):
    r = lax.rsqrt(jnp.mean(x * x, axis=-1, keepdims=True) + EPS)
    return x * r, r


def _rms_bwd_rows(dn, n, r):
    return r * (dn - n * jnp.mean(dn * n, axis=-1, keepdims=True))


def _tile(n, want):
    t = min(n, want)
    assert n % t == 0, (n, want)
    return t


_ANY = pl.BlockSpec(memory_space=pl.ANY)

SIBLING = 1
CHIPS = (4, 2, 6)


def _position():
    return lax.axis_index("x"), lax.axis_index("y"), lax.axis_index("c")


def _me():
    x, y, c = _position()
    return 4 * x + 2 * y + c


def _peer(rel):
    x, y, c = _position()
    return (x ^ ((rel >> 2) & 1), y ^ ((rel >> 1) & 1), c ^ (rel & 1))


class _Comm:
    def __init__(self, srcs, lands, new, plan):
        self.srcs, self.lands, self.new, self.plan = list(srcs), list(lands), list(new), plan


def _make_copies(phases, send_sems, recv_sems, local_sems):
    out, nr, nl = [], 0, 0
    for phase in phases:
        cps = []
        for item in phase:
            if item[0] == "local":
                cps.append(pltpu.make_async_copy(item[1], item[2], local_sems.at[nl]))
                nl += 1
            else:
                cps.append(pltpu.make_async_remote_copy(
                    src_ref=item[1], dst_ref=item[2], send_sem=send_sems.at[nr], recv_sem=recv_sems.at[nr],
                    device_id=_peer(item[3]), device_id_type=MESH))
                nr += 1
        out.append(cps)
    return out


def _count_copies(comm):
    phases = comm.plan([_FakeRef() for _ in comm.srcs], [_FakeRef() for _ in range(len(comm.lands) + len(comm.new))])
    items = [it for ph in phases for it in ph]
    return sum(it[0] == "remote" for it in items), sum(it[0] == "local" for it in items)


class _FakeRef:
    def __getitem__(self, idx):
        return self

    @property
    def at(self):
        return self


def _carrier_call(body, args, comm, *, name, grid, in_specs, out_specs, out_shape, scratch_shapes=(), sem):
    if not isinstance(out_shape, (list, tuple)):
        out_specs, out_shape = [out_specs], [out_shape]
    out_specs, out_shape, scratch_shapes = list(out_specs), list(out_shape), list(scratch_shapes)
    if comm is None:
        res = pl.pallas_call(body, name=name, grid=grid, in_specs=list(in_specs), out_specs=out_specs,
                             out_shape=out_shape, scratch_shapes=scratch_shapes, compiler_params=_params(*sem))(*args)
        return list(res)
    n_in, n_out, n_scr = len(args), len(out_shape), len(scratch_shapes)
    ns, nl, nn = len(comm.srcs), len(comm.lands), len(comm.new)
    n_remote, n_local = _count_copies(comm)
    steps = math.prod(grid)

    def wrapped(*refs):
        ins, srcs = refs[:n_in], refs[n_in:n_in + ns]
        o = n_in + ns + nl
        outs, lands = refs[o:o + n_out], refs[o + n_out:o + n_out + nl + nn]
        scr = refs[o + n_out + nl + nn:]
        phases = _make_copies(comm.plan(srcs, lands), *scr[n_scr:])
        assert len(phases) == 1 or (len(phases) == 2 and steps >= 3)
        step = functools.reduce(lambda acc, a: acc * grid[a] + pl.program_id(a), range(len(grid)), 0)

        @pl.when(step == 0)
        def _():
            for cp in phases[0]:
                cp.start()

        if len(phases) == 2:
            @pl.when(step == steps * 3 // 4)
            def _():
                for cp in phases[0]:
                    cp.wait()
                for cp in phases[1]:
                    cp.start()

        body(*ins, *outs, *scr[:n_scr])

        @pl.when(step == steps - 1)
        def _():
            for cp in phases[-1]:
                cp.wait()

    land_shapes = [jax.ShapeDtypeStruct(a.shape, a.dtype) for a in comm.lands] + comm.new
    sems = [pltpu.SemaphoreType.DMA((max(n_remote, 1),)), pltpu.SemaphoreType.DMA((max(n_remote, 1),)),
            pltpu.SemaphoreType.DMA((max(n_local, 1),))]
    res = pl.pallas_call(
        wrapped, name=name, grid=grid,
        in_specs=list(in_specs) + [_ANY] * (ns + nl), out_specs=out_specs + [_ANY] * (nl + nn),
        out_shape=out_shape + land_shapes, scratch_shapes=scratch_shapes + sems,
        input_output_aliases={n_in + ns + k: n_out + k for k in range(nl)},
        compiler_params=_params(*sem))(*args, *comm.srcs, *comm.lands)
    return list(res)


def _comm_call(name, comm):
    ns, nl, nn = len(comm.srcs), len(comm.lands), len(comm.new)
    n_remote, n_local = _count_copies(comm)

    def body(*refs):
        srcs, lands, sems = refs[:ns], refs[ns + nl:ns + nl + nl + nn], refs[ns + nl + nl + nn:]
        for copies in _make_copies(comm.plan(srcs, lands), *sems):
            for cp in copies:
                cp.start()
            for cp in copies:
                cp.wait()

    land_shapes = [jax.ShapeDtypeStruct(a.shape, a.dtype) for a in comm.lands] + comm.new
    res = pl.pallas_call(
        body, name=name,
        in_specs=[_ANY] * (ns + nl), out_specs=[_ANY] * (nl + nn), out_shape=land_shapes,
        scratch_shapes=[pltpu.SemaphoreType.DMA((max(n_remote, 1),)), pltpu.SemaphoreType.DMA((max(n_remote, 1),)),
                        pltpu.SemaphoreType.DMA((max(n_local, 1),))],
        input_output_aliases={ns + k: k for k in range(nl)},
    )(*comm.srcs, *comm.lands)
    return list(res)


def _rows(ref, block, r0, r1):
    return ref.at[block, pl.ds(r0, r1 - r0)]


def _gather_first(shard, land, r0, r1):
    src = shard.at[pl.ds(r0, r1 - r0)]
    dst = _rows(land, _me(), r0, r1)
    return [("local", src, dst)] + [("remote", src, dst, rel) for rel in (SIBLING,) + CHIPS]


def _gather_pass_on(land, r0, r1):
    return [("remote", _rows(land, _me() ^ rel, r0, r1), _rows(land, _me() ^ rel, r0, r1), SIBLING) for rel in CHIPS]


def _split_rows(r0, r1):
    m = (r0 + r1) // 2 // 16 * 16
    return (r0, m), (m, r1)


def _gather_to_neighbours(shard, land, r0, r1):
    src = shard.at[pl.ds(r0, r1 - r0)]
    dst = _rows(land, _me(), r0, r1)
    return [("local", src, dst)] + [("remote", src, dst, rel) for rel in (SIBLING, 4, 2)]


def _gather_relay(land, r0, r1):
    lo, hi = _split_rows(r0, r1)
    x_block, y_block = _me() ^ 4, _me() ^ 2
    return [("remote", _rows(land, x_block, *lo), _rows(land, x_block, *lo), 2),
            ("remote", _rows(land, y_block, *hi), _rows(land, y_block, *hi), 4),
            ("remote", _rows(land, x_block, r0, r1), _rows(land, x_block, r0, r1), SIBLING),
            ("remote", _rows(land, y_block, r0, r1), _rows(land, y_block, r0, r1), SIBLING)]


def _gather_diagonal_pass_on(land, r0, r1):
    rows = _rows(land, _me() ^ 6, r0, r1)
    return [("remote", rows, rows, SIBLING)]


def _pair_exchange(grad, land):
    _, _, c = _position()
    return [("remote", grad.at[2 * chip + (1 - c)], land.at[chip], SIBLING) for chip in range(4)]


def _chip_exchange(sums, land, r0, r1):
    return [("remote", _rows(sums, j, r0, r1), _rows(land, j, r0, r1), rel) for j, rel in enumerate(CHIPS)]


def _everyone(packed, land):
    dst = land.at[_me()]
    return [("local", packed, dst)] + [("remote", packed, dst, rel) for rel in range(1, N_DEV)]


def _pool_counts(row0, rows, win):
    pos = row0 + lax.broadcasted_iota(jnp.int32, (rows, 1), 0)
    return jnp.minimum(pos + 1, win).astype(F32)


def _window_sum_back(ext, win):
    s = ext
    k = 1
    while k < win:
        s = s + pltpu.roll(s, k, 0)
        k *= 2
    return s


def _window_sum_fwd(ext, win):
    n = ext.shape[0]
    s = ext
    k = 1
    while k < win:
        s = s + pltpu.roll(s, n - k, 0)
        k *= 2
    return s


def _mixer_fwd_call(proj, w_s, bs_t, g_v, w_pool_g, pool_scale, comm=None):
    t = proj.shape[0]
    tt = _tile(t, 512)
    nchunk = tt // CHUNK
    hb = tt // HALO

    def body(pu_ref, pv_ref, z_ref, zp_ref, ws_ref, bs_ref, gv_ref, wp_ref, ps_ref, out_ref):
        i = pl.program_id(0)
        tril = (lax.broadcasted_iota(jnp.int32, (CHUNK, CHUNK), 0)
                >= lax.broadcasted_iota(jnp.int32, (CHUNK, CHUNK), 1))
        for h in range(N_HEADS):
            cols = slice(h * CHUNK, (h + 1) * CHUNK)
            vhat, _ = _rms_rows(_gelu(pv_ref[:, cols]))
            vn = (vhat * gv_ref[:, cols]).astype(BF16)
            u = _gelu(pu_ref[:, cols])
            w = jnp.where(tril, ws_ref[h], 0.0).astype(BF16)
            bcol = bs_ref[:, h:h + 1]
            for c in range(nchunk):
                rows = slice(c * CHUNK, (c + 1) * CHUNK)
                mixed = _dot_nn(w, vn[rows]) + bcol
                out_ref[rows, cols] = (u[rows] * mixed).astype(BF16)

        zprev = jnp.where(i > 0, zp_ref[...], 0.0)
        ext = jnp.concatenate([zprev, z_ref[...]], axis=0)
        for g, win in enumerate(POOL_WINDOWS):
            cols = slice(g * GROUP, (g + 1) * GROUP)
            zg = ext[:, cols]
            s = _window_sum_back(zg, win)
            pooled = s[HALO:] / _pool_counts(i * tt, tt, win) - zg[HALO:]
            wp = wp_ref[:, g].reshape(GROUP, GROUP)
            y = _dot_nn(pooled.astype(BF16), wp)
            out_ref[:, A_WIDTH + g * GROUP:A_WIDTH + (g + 1) * GROUP] = (y * ps_ref[:, cols]).astype(BF16)

    return _carrier_call(
        body, (proj, proj, proj, proj, w_s, bs_t, g_v, w_pool_g, pool_scale), comm, name="mixer_fwd",
        grid=(t // tt,),
        in_specs=[pl.BlockSpec((tt, A_WIDTH), lambda i: (i, 0)),
                  pl.BlockSpec((tt, A_WIDTH), lambda i: (i, 1)),
                  pl.BlockSpec((tt, B_WIDTH), lambda i: (i, 2)),
                  pl.BlockSpec((HALO, B_WIDTH), lambda i: (jnp.maximum(i * hb - 1, 0), 2)),
                  pl.BlockSpec((N_HEADS, CHUNK, CHUNK), lambda i: (0, 0, 0)),
                  pl.BlockSpec((CHUNK, N_HEADS), lambda i: (0, 0)),
                  pl.BlockSpec((1, A_WIDTH), lambda i: (0, 0)),
                  pl.BlockSpec((N_DEV, 4, GROUP // N_DEV, GROUP), lambda i: (0, 0, 0, 0)),
                  pl.BlockSpec((1, B_WIDTH), lambda i: (0, 0))],
        out_specs=pl.BlockSpec((tt, A_WIDTH + B_WIDTH), lambda i: (i, 0)),
        out_shape=jax.ShapeDtypeStruct((t, A_WIDTH + B_WIDTH), BF16),
        sem=("parallel",))


def _out_proj_call(mixed, w_out, x, comm=None):
    t, d = x.shape
    k = mixed.shape[1]
    tm = _tile(t, 1024)
    tn = _tile(d, 1024)

    def body(a_ref, w_ref, x_ref, o_ref):
        o_ref[...] = x_ref[...] + _dot_nn(a_ref[...], w_ref[...])

    return _carrier_call(
        body, (mixed, w_out, x), comm, name="out_proj_fwd",
        grid=(t // tm, d // tn),
        in_specs=[pl.BlockSpec((tm, k), lambda i, j: (i, 0)),
                  pl.BlockSpec((k, tn), lambda i, j: (0, j)),
                  pl.BlockSpec((tm, tn), lambda i, j: (i, j))],
        out_specs=pl.BlockSpec((tm, tn), lambda i, j: (i, j)),
        out_shape=jax.ShapeDtypeStruct((t, d), F32),
        sem=("parallel", "parallel"))


ARRIVAL_ORDER = (0, 1, 4, 5, 2, 3, 6, 7)
CARRIED_AFTER = 3


class _StreamedGather:
    def __init__(self, shard_ref, land_ref, wbuf, pre0, r0, send_sems, recv_sems, local_sem, fetch_sems):
        self.shard, self.land, self.wbuf, self.fetch_sems = shard_ref, land_ref, wbuf, fetch_sems
        end = shard_ref.shape[0]
        me = _me()
        self.me = me

        def remote(k, src, dst, rel):
            return pltpu.make_async_remote_copy(src_ref=src, dst_ref=dst, send_sem=send_sems.at[k],
                                                recv_sem=recv_sems.at[k], device_id=_peer(rel), device_id_type=MESH)

        def same_rows(k, block, a, b, rel):
            ref = land_ref.at[block, pl.ds(a, b - a)]
            return remote(k, ref, ref, rel)

        src = shard_ref.at[pl.ds(r0, end - r0)]
        dst = land_ref.at[me, pl.ds(r0, end - r0)]
        self.mine = pltpu.make_async_copy(src, dst, local_sem)
        self.first = [remote(k, src, dst, rel) for k, rel in enumerate((SIBLING, 4, 2))]
        lo, hi = _split_rows(r0, end)
        self.relay = [same_rows(3, me ^ 4, *lo, 2), same_rows(4, me ^ 2, *hi, 4)]
        self.passed = [same_rows(5, me ^ 4, r0, end, SIBLING), same_rows(6, me ^ 2, r0, end, SIBLING),
                       same_rows(7, me ^ 6, pre0, end, SIBLING)]
        self.early_relay, self.early_passed = [], []
        if r0 > pre0:
            lo, hi = _split_rows(pre0, r0)
            self.early_relay = [same_rows(8, me ^ 4, *lo, 2), same_rows(9, me ^ 2, *hi, 4)]
            self.early_passed = [same_rows(10, me ^ 4, pre0, r0, SIBLING), same_rows(11, me ^ 2, pre0, r0, SIBLING)]

    def _fetch(self, q):
        src = self.shard if q == 0 else self.land.at[self.me ^ ARRIVAL_ORDER[q]]
        return pltpu.make_async_copy(src, self.wbuf.at[q % 2], self.fetch_sems.at[q % 2])

    def start(self):
        self.mine.start()
        for cp in self.first + self.early_relay + self.early_passed:
            cp.start()
        self._fetch(0).start()

    def arrive(self, q):
        if q == 1:
            self.first[0].wait_recv()
        elif q in (2, 4):
            j = q // 2 - 1
            self.first[1 + j].wait_recv()
            self.relay[j].start()
            self.passed[j].start()
        elif q in (3, 5):
            j = q // 2 - 1
            self.passed[j].wait_recv()
            if self.early_passed:
                self.early_passed[j].wait_recv()
        elif q == 6:
            for cp in self.relay + self.early_relay:
                cp.wait_recv()
            self.passed[2].start()
        else:
            self.passed[2].wait_recv()
        self._fetch(q).start()

    def wait_fetch(self, slot):
        pltpu.make_async_copy(self.shard, self.wbuf.at[slot], self.fetch_sems.at[slot]).wait()

    def finish(self):
        for cp in self.first + self.relay + self.passed + self.early_relay + self.early_passed:
            cp.wait_send()
        self.mine.wait()


_STREAM_SEMS = [pltpu.SemaphoreType.DMA((12,)), pltpu.SemaphoreType.DMA((12,)), pltpu.SemaphoreType.DMA,
                pltpu.SemaphoreType.DMA((2,))]


def _stream_steps(gather, p, i, ni):
    @pl.when((p == 0) & (i == 0))
    def _():
        gather.start()

    @pl.when(i == 0)
    def _():
        gather.wait_fetch(p % 2)

    @pl.when(i == ni - 1)
    def _():
        for q in range(1, N_DEV):
            @pl.when(p == q - 1)
            def _():
                gather.arrive(q)


def _norm_matmul_stream_call(name, x, g, shard, order, land, pre0, r0, comm, epilogue, out_dtypes):
    t, d = x.shape
    cb = shard.shape[1]
    tm = _tile(t, 1024)
    ni = t // tm
    n_sems = len(_STREAM_SEMS)
    assert not comm.lands
    ns, nn, no = len(comm.srcs), len(comm.new), len(out_dtypes)
    n_remote, n_local = _count_copies(comm)
    has_land = land is not None

    def body(order_ref, x_ref, g_ref, shard_ref, *refs):
        refs = refs[has_land:]
        srcs, out_refs, (h_ref, land_ref) = refs[:ns], refs[ns:ns + no], refs[ns + no:ns + no + 2]
        new = refs[ns + no + 2:ns + no + 2 + nn]
        wbuf, sems = refs[ns + no + 2 + nn], refs[ns + no + 3 + nn:]
        p, i = pl.program_id(0), pl.program_id(1)
        gather = _StreamedGather(shard_ref, land_ref, wbuf, pre0, r0, *sems[:n_sems])
        (carried,) = _make_copies(comm.plan(srcs, new), *sems[n_sems:])
        rows = pl.ds(pl.multiple_of(i * tm, tm), tm)
        _stream_steps(gather, p, i, ni)

        @pl.when((p == CARRIED_AFTER) & (i == ni - 1))
        def _():
            for cp in carried:
                cp.start()

        @pl.when(p == 0)
        def _():
            n, _ = _rms_rows(x_ref[...])
            h_ref[rows, :] = (n * g_ref[...]).astype(BF16)

        tails = epilogue(_dot_nn(h_ref[rows, :], wbuf[p % 2]))
        for out_ref, tail, dt in zip(out_refs, tails, out_dtypes):
            out_ref[...] = tail.astype(dt)

        @pl.when((p == N_DEV - 1) & (i == ni - 1))
        def _():
            gather.finish()
            for cp in carried:
                cp.wait()

    carried_sems = [pltpu.SemaphoreType.DMA((max(n_remote, 1),)), pltpu.SemaphoreType.DMA((max(n_remote, 1),)),
                    pltpu.SemaphoreType.DMA((max(n_local, 1),))]
    return pl.pallas_call(
        body, name=name,
        grid_spec=pltpu.PrefetchScalarGridSpec(
            num_scalar_prefetch=1, grid=(N_DEV, ni),
            in_specs=[pl.BlockSpec((tm, d), lambda p, i, o: (jnp.where(p == 0, i, ni - 1), 0)),
                      pl.BlockSpec((1, d), lambda p, i, o: (0, 0)),
                      _ANY] + [_ANY] * (has_land + ns),
            out_specs=[pl.BlockSpec((tm, cb), lambda p, i, o: (i, o[p]))] * no
                      + [pl.BlockSpec(memory_space=pltpu.VMEM), _ANY] + [_ANY] * nn,
            scratch_shapes=[pltpu.VMEM((2, d, cb), BF16)] + _STREAM_SEMS + carried_sems),
        out_shape=[jax.ShapeDtypeStruct((t, N_DEV * cb), dt) for dt in out_dtypes]
                  + [jax.ShapeDtypeStruct((t, d), BF16), jax.ShapeDtypeStruct((N_DEV, d, cb), BF16)] + comm.new,
        input_output_aliases={4: no + 1} if has_land else {},
        compiler_params=_params("arbitrary", "arbitrary"),
    )(order, x, g, shard, *([land] if has_land else []), *comm.srcs)


def _down_call(act, shard, order, land, r0):
    t = act.shape[0]
    rb, d = shard.shape
    tm = _tile(t, 1024)
    ni = t // tm

    def body(order_ref, a_ref, shard_ref, land_in_ref, y_ref, land_ref, wbuf, *sems):
        p, i = pl.program_id(0), pl.program_id(1)
        gather = _StreamedGather(shard_ref, land_ref, wbuf, 0, r0, *sems)
        rows = pl.ds(pl.multiple_of(i * tm, tm), tm)
        _stream_steps(gather, p, i, ni)
        @pl.when(p == 0)
        def _():
            y_ref[rows, :] = _dot_nn(a_ref[...], wbuf[0])

        @pl.when(p > 0)
        def _():
            y_ref[rows, :] += _dot_nn(a_ref[...], wbuf[p % 2])

        @pl.when((p == N_DEV - 1) & (i == ni - 1))
        def _():
            gather.finish()

    return pl.pallas_call(
        body, name="down_fwd",
        grid_spec=pltpu.PrefetchScalarGridSpec(
            num_scalar_prefetch=1, grid=(N_DEV, ni),
            in_specs=[pl.BlockSpec((tm, rb), lambda p, i, o: (i, o[p])), _ANY, _ANY],
            out_specs=[pl.BlockSpec(memory_space=pltpu.VMEM), _ANY],
            scratch_shapes=[pltpu.VMEM((2, rb, d), BF16)] + _STREAM_SEMS),
        out_shape=[jax.ShapeDtypeStruct((t, d), F32), jax.ShapeDtypeStruct((N_DEV, rb, d), BF16)],
        input_output_aliases={3: 1},
        compiler_params=_params("arbitrary", "arbitrary"),
    )(order, act, shard, land)


def _loss_call(y, x2, target, g_final):
    t, d = y.shape
    tr = _tile(t, 512)

    def body(y_ref, x_ref, tg_ref, g_ref, loss_ref, dx_ref, dxb_ref, dg_ref):
        @pl.when(pl.program_id(0) == 0)
        def _():
            loss_ref[...] = jnp.zeros_like(loss_ref)
            dg_ref[...] = jnp.zeros_like(dg_ref)

        n, r = _rms_rows(x_ref[...] + y_ref[...])
        err = n * g_ref[...] - tg_ref[...]
        loss_ref[...] += 0.5 * jnp.sum(jnp.mean(err * err, axis=-1, keepdims=True))
        dy = err * (1.0 / d)
        dg_ref[...] += jnp.sum(dy * n, axis=0, keepdims=True)
        dx = _rms_bwd_rows(dy * g_ref[...], n, r)
        dx_ref[...] = dx
        dxb_ref[...] = dx.astype(BF16)

    return pl.pallas_call(
        body, name="loss_head",
        grid=(t // tr,),
        in_specs=[pl.BlockSpec((tr, d), lambda i: (i, 0)),
                  pl.BlockSpec((tr, d), lambda i: (i, 0)),
                  pl.BlockSpec((tr, d), lambda i: (i, 0)),
                  pl.BlockSpec((1, d), lambda i: (0, 0))],
        out_specs=[pl.BlockSpec((8, LANES), lambda i: (0, 0)),
                   pl.BlockSpec((tr, d), lambda i: (i, 0)),
                   pl.BlockSpec((tr, d), lambda i: (i, 0)),
                   pl.BlockSpec((1, d), lambda i: (0, 0))],
        out_shape=[jax.ShapeDtypeStruct((8, LANES), F32), jax.ShapeDtypeStruct((t, d), F32),
                   jax.ShapeDtypeStruct((t, d), BF16), jax.ShapeDtypeStruct((1, d), F32)],
        compiler_params=_params("arbitrary"),
    )(y, x2, target, g_final)


def _norm_bwd_call(name, dh, x, dres, g, want_bf16, comm=None):
    t, d = x.shape
    tr = _tile(t, 256)

    def body(dh_ref, x_ref, dres_ref, g_ref, dx_ref, *rest):
        dg_ref = rest[-1]

        @pl.when(pl.program_id(0) == 0)
        def _():
            dg_ref[...] = jnp.zeros_like(dg_ref)

        n, r = _rms_rows(x_ref[...])
        dh = dh_ref[...]
        dg_ref[...] += jnp.sum(dh * n, axis=0, keepdims=True)
        dx = dres_ref[...] + _rms_bwd_rows(dh * g_ref[...], n, r)
        dx_ref[...] = dx
        if want_bf16:
            rest[0][...] = dx.astype(BF16)

    row = pl.BlockSpec((tr, d), lambda i: (i, 0))
    vec = pl.BlockSpec((1, d), lambda i: (0, 0))
    out_specs = [row] + ([row] if want_bf16 else []) + [vec]
    out_shape = ([jax.ShapeDtypeStruct((t, d), F32)]
                 + ([jax.ShapeDtypeStruct((t, d), BF16)] if want_bf16 else [])
                 + [jax.ShapeDtypeStruct((1, d), F32)])
    return _carrier_call(
        body, (dh, x, dres, g), comm, name=name,
        grid=(t // tr,),
        in_specs=[row, row, row, vec],
        out_specs=out_specs, out_shape=out_shape,
        sem=("arbitrary",))


def _dact_call(dx3b, w_down, act, comm=None):
    t, d = dx3b.shape
    f = w_down.shape[0]
    tm = _tile(t, 1024)
    tn = _tile(f, 2048)

    def body(g_ref, w_ref, act_ref, o_ref):
        dact = _dot_nt(g_ref[...], w_ref[...])
        o_ref[...] = (dact * act_ref[...].astype(F32)).astype(BF16)

    return _carrier_call(
        body, (dx3b, w_down, act), comm, name="dact_bwd",
        grid=(t // tm, f // tn),
        in_specs=[pl.BlockSpec((tm, d), lambda i, j: (i, 0)),
                  pl.BlockSpec((tn, d), lambda i, j: (j, 0)),
                  pl.BlockSpec((tm, tn), lambda i, j: (i, j))],
        out_specs=pl.BlockSpec((tm, tn), lambda i, j: (i, j)),
        out_shape=jax.ShapeDtypeStruct((t, f), BF16),
        sem=("parallel", "parallel"))


def _wgrad_call(name, a, b, out_blocks, out_block_cols, comm=None, *, t1, t2=None, merge=1):
    t, k1 = a.shape
    k2 = b.shape[1]
    tt = _tile(t, 2048)
    t1 = _tile(k1, t1)
    t2 = _tile(k2, t2) if out_blocks is None else merge * out_block_cols
    nk = t // tt

    def body(a_ref, b_ref, o_ref, acc_ref):
        k = pl.program_id(2)

        @pl.when(k == 0)
        def _():
            acc_ref[...] = _dot_tn(a_ref[...], b_ref[...])

        @pl.when(k > 0)
        def _():
            acc_ref[...] += _dot_tn(a_ref[...], b_ref[...])

        @pl.when(k == nk - 1)
        def _():
            if out_blocks is None:
                o_ref[...] = acc_ref[...].astype(BF16)
            else:
                for blk in range(merge):
                    o_ref[blk] = acc_ref[:, blk * out_block_cols:(blk + 1) * out_block_cols].astype(BF16)

    if out_blocks is None:
        out_spec = pl.BlockSpec((t1, t2), lambda i, j, k: (i, j))
        out_shape = jax.ShapeDtypeStruct((k1, k2), BF16)
    else:
        out_spec = pl.BlockSpec((merge, t1, out_block_cols), lambda i, j, k: (j, i, 0))
        out_shape = jax.ShapeDtypeStruct((out_blocks, k1, out_block_cols), BF16)
    return _carrier_call(
        body, (a, b), comm, name=name,
        grid=(k1 // t1, k2 // t2, nk),
        in_specs=[pl.BlockSpec((tt, t1), lambda i, j, k: (k, i)),
                  pl.BlockSpec((tt, t2), lambda i, j, k: (k, j))],
        out_specs=out_spec, out_shape=out_shape,
        scratch_shapes=[pltpu.VMEM((t1, t2), F32)],
        sem=("parallel", "parallel", "arbitrary"))


def _dgrad_blocked_call(name, g, w_g, comm=None, *, merge=1):
    t = g.shape[0]
    nb, d, cb = w_g.shape
    tm = _tile(t, 1024)
    tn = _tile(d, 2048)
    tk = merge * cb

    def body(g_ref, w_ref, o_ref):
        def product():
            w = w_ref[0] if merge == 1 else jnp.concatenate([w_ref[b] for b in range(merge)], axis=1)
            return _dot_nt(g_ref[...], w)

        @pl.when(pl.program_id(2) == 0)
        def _():
            o_ref[...] = product()

        @pl.when(pl.program_id(2) > 0)
        def _():
            o_ref[...] += product()

    return _carrier_call(
        body, (g, w_g), comm, name=name,
        grid=(t // tm, d // tn, nb // merge),
        in_specs=[pl.BlockSpec((tm, tk), lambda i, j, k: (i, k)),
                  pl.BlockSpec((merge, tn, cb), lambda i, j, k: (k, j, 0))],
        out_specs=pl.BlockSpec((tm, tn), lambda i, j, k: (i, j)),
        out_shape=jax.ShapeDtypeStruct((t, d), F32),
        sem=("parallel", "parallel", "arbitrary"))


def _dmixed_call(dx2b, w_out, comm=None):
    t, d = dx2b.shape
    e = w_out.shape[0]
    tm = _tile(t, 1024)
    tn = _tile(e, 1024)

    def body(g_ref, w_ref, o_ref):
        o_ref[...] = _dot_nt(g_ref[...], w_ref[...])

    return _carrier_call(
        body, (dx2b, w_out), comm, name="dmixed_bwd",
        grid=(t // tm, e // tn),
        in_specs=[pl.BlockSpec((tm, d), lambda i, j: (i, 0)),
                  pl.BlockSpec((tn, d), lambda i, j: (j, 0))],
        out_specs=pl.BlockSpec((tm, tn), lambda i, j: (i, j)),
        out_shape=jax.ShapeDtypeStruct((t, e), F32),
        sem=("parallel", "parallel"))


def _mixer_bwd_call(proj, dmixed, w_s, bs_t, g_v, w_pool_g, pool_scale, comm=None):
    t = proj.shape[0]
    tt = _tile(t, 512)
    nchunk = tt // CHUNK
    hb = tt // HALO
    last_halo = t // HALO - 1
    nsteps = t // tt
    rb = GROUP // N_DEV

    def body(pu_ref, pv_ref, z_ref, zp_ref, da_ref, db_ref, dbn_ref, ws_ref, bs_ref, gv_ref, wp_ref, ps_ref,
             dproj_ref, dws_ref, dbs_ref, dgv_ref, dps_ref, dwp_ref):
        i = pl.program_id(0)

        @pl.when(i == 0)
        def _():
            dws_ref[...] = jnp.zeros_like(dws_ref)
            dbs_ref[...] = jnp.zeros_like(dbs_ref)
            dgv_ref[...] = jnp.zeros_like(dgv_ref)
            dps_ref[...] = jnp.zeros_like(dps_ref)
            dwp_ref[...] = jnp.zeros_like(dwp_ref)

        tril = (lax.broadcasted_iota(jnp.int32, (CHUNK, CHUNK), 0)
                >= lax.broadcasted_iota(jnp.int32, (CHUNK, CHUNK), 1))
        for h in range(N_HEADS):
            cols = slice(h * CHUNK, (h + 1) * CHUNK)
            v, dv_dpv = _gelu_and_grad(pv_ref[:, cols])
            vhat, rv = _rms_rows(v)
            gv = gv_ref[:, cols]
            vn = (vhat * gv).astype(BF16)
            u, du_dpu = _gelu_and_grad(pu_ref[:, cols])
            w = jnp.where(tril, ws_ref[h], 0.0).astype(BF16)
            bcol = bs_ref[:, h:h + 1]
            dout = da_ref[:, cols]
            dmix = dout * u
            dmix_b = dmix.astype(BF16)
            dws = jnp.zeros((CHUNK, CHUNK), F32)
            dbs = jnp.zeros((CHUNK, 1), F32)
            dvn_parts = []
            du_parts = []
            for c in range(nchunk):
                rows = slice(c * CHUNK, (c + 1) * CHUNK)
                mixed = _dot_nn(w, vn[rows]) + bcol
                du_parts.append(dout[rows] * mixed)
                dvn_parts.append(_dot_tn(w, dmix_b[rows]))
                dws = dws + _dot_nt(dmix_b[rows], vn[rows])
                dbs = dbs + jnp.sum(dmix[rows], axis=1, keepdims=True)
            dws_ref[h] += jnp.where(tril, dws, 0.0)
            dbs_ref[:, h:h + 1] += dbs
            dvn = jnp.concatenate(dvn_parts, axis=0)
            du = jnp.concatenate(du_parts, axis=0)
            dgv_ref[:, cols] += jnp.sum(dvn * vhat, axis=0, keepdims=True)
            dv = _rms_bwd_rows(dvn * gv, vhat, rv)
            dproj_ref[:, cols] = (du * du_dpu).astype(BF16)
            dproj_ref[:, A_WIDTH + h * CHUNK:A_WIDTH + (h + 1) * CHUNK] = (dv * dv_dpv).astype(BF16)

        zprev = jnp.where(i > 0, zp_ref[...], 0.0)
        ext = jnp.concatenate([zprev, z_ref[...]], axis=0)
        dnext = jnp.where(i < nsteps - 1, dbn_ref[...], 0.0)
        dext = jnp.concatenate([db_ref[...], dnext], axis=0)
        for g, win in enumerate(POOL_WINDOWS):
            cols = slice(g * GROUP, (g + 1) * GROUP)
            zg = ext[:, cols]
            pooled = _window_sum_back(zg, win)[HALO:] / _pool_counts(i * tt, tt, win) - zg[HALO:]
            pooled_b = pooled.astype(BF16)
            wp = wp_ref[:, g].reshape(GROUP, GROUP)
            y = _dot_nn(pooled_b, wp)
            dout = dext[:, cols]
            dps_ref[:, cols] += jnp.sum(dout[:tt] * y, axis=0, keepdims=True)
            dy_b = (dout * ps_ref[:, cols]).astype(BF16)
            dwp_ref[:, g] += _dot_tn(pooled_b, dy_b[:tt]).reshape(N_DEV, rb, GROUP)
            dpooled = _dot_nt(dy_b, wp)
            q = dpooled / _pool_counts(i * tt, tt + HALO, win)
            dz = _window_sum_fwd(q, win)[:tt] - dpooled[:tt]
            dproj_ref[:, 2 * A_WIDTH + g * GROUP:2 * A_WIDTH + (g + 1) * GROUP] = dz.astype(BF16)

    def full(shape):
        return pl.BlockSpec(shape, lambda i: (0,) * len(shape))

    return _carrier_call(
        body, (proj, proj, proj, proj, dmixed, dmixed, dmixed, w_s, bs_t, g_v, w_pool_g, pool_scale), comm,
        name="mixer_bwd",
        grid=(nsteps,),
        in_specs=[pl.BlockSpec((tt, A_WIDTH), lambda i: (i, 0)),
                  pl.BlockSpec((tt, A_WIDTH), lambda i: (i, 1)),
                  pl.BlockSpec((tt, B_WIDTH), lambda i: (i, 2)),
                  pl.BlockSpec((HALO, B_WIDTH), lambda i: (jnp.maximum(i * hb - 1, 0), 2)),
                  pl.BlockSpec((tt, A_WIDTH), lambda i: (i, 0)),
                  pl.BlockSpec((tt, B_WIDTH), lambda i: (i, 1)),
                  pl.BlockSpec((HALO, B_WIDTH), lambda i: (jnp.minimum((i + 1) * hb, last_halo), 1)),
                  full((N_HEADS, CHUNK, CHUNK)), full((CHUNK, N_HEADS)), full((1, A_WIDTH)),
                  full((N_DEV, 4, rb, GROUP)), full((1, B_WIDTH))],
        out_specs=[pl.BlockSpec((tt, 2 * A_WIDTH + B_WIDTH), lambda i: (i, 0)),
                   full((N_HEADS, CHUNK, CHUNK)), full((CHUNK, N_HEADS)), full((1, A_WIDTH)),
                   full((1, B_WIDTH)), full((N_DEV, 4, rb, GROUP))],
        out_shape=[jax.ShapeDtypeStruct((t, 2 * A_WIDTH + B_WIDTH), BF16),
                   jax.ShapeDtypeStruct((N_HEADS, CHUNK, CHUNK), F32),
                   jax.ShapeDtypeStruct((CHUNK, N_HEADS), F32),
                   jax.ShapeDtypeStruct((1, A_WIDTH), F32),
                   jax.ShapeDtypeStruct((1, B_WIDTH), F32),
                   jax.ShapeDtypeStruct((N_DEV, 4, rb, GROUP), F32)],
        sem=("arbitrary",))


def _adamw(w, g, m, v):
    m = ADAM_B1 * m + (1.0 - ADAM_B1) * g
    v = ADAM_B2 * v + (1.0 - ADAM_B2) * (g * g)
    m_hat = m / ADAM_C1
    v_hat = v / ADAM_C2
    delta = -ADAM_LR * (m_hat / (jnp.sqrt(v_hat) + ADAM_EPS) + ADAM_WD * w)
    return delta, m, v


PAIR_SUM_TILE_ELEMS = 1024 * 1024
ADAMW_TILE_ELEMS = 512 * 1024


def _row_tile(r, c, elems):
    t = r
    while t * c > elems and t % 32 == 0:
        t //= 2
    return t


def _pair_sum_call(name, pos, grad, got):
    _, r, c = grad.shape
    tr = _row_tile(r, c, PAIR_SUM_TILE_ELEMS)

    def chip_of(rel, pos_ref):
        px = jnp.where((rel == 0) | (rel == 2), 1 - pos_ref[0], pos_ref[0])
        py = jnp.where((rel == 1) | (rel == 2), 1 - pos_ref[1], pos_ref[1])
        return 2 * px + py

    def body(pos_ref, own_ref, got_ref, out_ref):
        out_ref[...] = (own_ref[...].astype(F32) + got_ref[...].astype(F32)).astype(BF16)

    return pl.pallas_call(
        body, name=name,
        grid_spec=pltpu.PrefetchScalarGridSpec(
            num_scalar_prefetch=1, grid=(3, r // tr),
            in_specs=[pl.BlockSpec((None, tr, c), lambda k, i, p: (2 * chip_of(k, p) + p[2], i, 0)),
                      pl.BlockSpec((None, tr, c), lambda k, i, p: (chip_of(k, p), i, 0))],
            out_specs=pl.BlockSpec((None, tr, c), lambda k, i, p: (k, i, 0))),
        out_shape=jax.ShapeDtypeStruct((3, r, c), BF16),
        compiler_params=_params("parallel", "parallel"),
    )(pos, grad, got)


def _final_call(name, pos, grad, got_pair, got_chips, w, m, v):
    _, r, c = grad.shape
    tr = _row_tile(r, c, ADAMW_TILE_ELEMS)

    def body(pos_ref, own_ref, pair_ref, chips_ref, w_ref, m_ref, v_ref, g_out, d_out, m_out, v_out):
        g = own_ref[...].astype(F32) + pair_ref[...].astype(F32)
        for j in range(3):
            g = g + chips_ref[j].astype(F32)
        delta, m_new, v_new = _adamw(w_ref[...], g, m_ref[...], v_ref[...])
        g_out[...] = g
        d_out[...] = delta
        m_out[...] = m_new
        v_out[...] = v_new

    row = pl.BlockSpec((tr, c), lambda i, p: (i, 0))
    return pl.pallas_call(
        body, name=name,
        grid_spec=pltpu.PrefetchScalarGridSpec(
            num_scalar_prefetch=1, grid=(r // tr,),
            in_specs=[pl.BlockSpec((None, tr, c), lambda i, p: (4 * p[0] + 2 * p[1] + p[2], i, 0)),
                      pl.BlockSpec((None, tr, c), lambda i, p: (2 * p[0] + p[1], i, 0)),
                      pl.BlockSpec((3, tr, c), lambda i, p: (0, i, 0)), row, row, row],
            out_specs=[row] * 4),
        out_shape=[jax.ShapeDtypeStruct((r, c), F32)] * 4,
        compiler_params=_params("parallel"),
    )(pos, grad, got_pair, got_chips, w, m, v)


def _small_final_call(name, parts, w, m, v):
    _, rows, c = parts.shape
    r = w.shape[0]

    def body(p_ref, w_ref, m_ref, v_ref, g_out, d_out, m_out, v_out):
        g = p_ref[0]
        for k in range(1, N_DEV):
            g = g + p_ref[k]
        delta, m_new, v_new = _adamw(w_ref[...], g[:r], m_ref[...], v_ref[...])
        g_out[...] = g
        d_out[...] = delta
        m_out[...] = m_new
        v_out[...] = v_new

    return pl.pallas_call(
        body, name=name,
        out_shape=[jax.ShapeDtypeStruct((rows, c), F32)] + [jax.ShapeDtypeStruct((r, c), F32)] * 3,
        compiler_params=pltpu.CompilerParams(vmem_limit_bytes=VMEM_LIMIT),
    )(parts, w, m, v)


_SMALL_EARLY = ("g_v", "w_s", "b_s", "pool_scale", "g_ffn", "g_final")
_BIG = ("w_in", "w_pool", "w_out", "w_up", "w_down")
_ORDER = ("g_mix", "w_in", "g_v", "w_s", "b_s", "w_pool", "pool_scale", "w_out", "g_ffn", "w_up", "w_down", "g_final")


def _pack(parts):
    return jnp.concatenate([p.reshape(-1, LANES) for p in parts], axis=0)


def _unpack(packed, like):
    out, row = [], 0
    for a in like:
        rows = a.size // LANES
        out.append(packed[row:row + rows].reshape(a.shape))
        row += rows
    return out


def kernel(x, g_mix, w_in, g_v, w_s, b_s, w_pool, pool_scale, w_out, g_ffn, w_up, w_down, g_final, loss_target, m_g_mix, m_w_in, m_g_v, m_w_s, m_b_s, m_w_pool, m_pool_scale, m_w_out, m_g_ffn, m_w_up, m_w_down, m_g_final, v_g_mix, v_w_in, v_g_v, v_w_s, v_b_s, v_w_pool, v_pool_scale, v_w_out, v_g_ffn, v_w_up, v_w_down, v_g_final):
    weights = dict(g_mix=g_mix, w_in=w_in, g_v=g_v, w_s=w_s, b_s=b_s, w_pool=w_pool, pool_scale=pool_scale,
                   w_out=w_out, g_ffn=g_ffn, w_up=w_up, w_down=w_down, g_final=g_final)
    mom = dict(g_mix=m_g_mix, w_in=m_w_in, g_v=m_g_v, w_s=m_w_s, b_s=m_b_s, w_pool=m_w_pool,
               pool_scale=m_pool_scale, w_out=m_w_out, g_ffn=m_g_ffn, w_up=m_w_up, w_down=m_w_down,
               g_final=m_g_final)
    var = dict(g_mix=v_g_mix, w_in=v_w_in, g_v=v_g_v, w_s=v_w_s, b_s=v_b_s, w_pool=v_w_pool,
               pool_scale=v_pool_scale, w_out=v_w_out, g_ffn=v_g_ffn, w_up=v_w_up, w_down=v_w_down,
               g_final=v_g_final)

    t, d = x.shape[1], x.shape[2]
    xs = x.reshape(t, d)
    target = loss_target.reshape(t, d)

    shard2d = dict(w_in=w_in.reshape(d, -1), w_pool=w_pool.reshape(-1, GROUP), w_out=w_out.reshape(-1, d),
                   w_up=w_up.reshape(d, -1), w_down=w_down.reshape(-1, d))
    sb = {k: shard2d[k].astype(BF16) for k in _BIG}
    rows = {k: sb[k].shape[0] for k in _BIG}

    def gathered_shape(k):
        return jax.ShapeDtypeStruct((N_DEV,) + sb[k].shape, BF16)

    def landing(n, like):
        return jax.ShapeDtypeStruct((n,) + like.shape[1:], like.dtype)

    def from_everyone(block):
        return jax.ShapeDtypeStruct((N_DEV,) + block.shape, block.dtype)

    def cuts(r, fractions):
        return [0] + [int(r * f) // 16 * 16 for f in fractions] + [r]

    g_mix2, g_ffn2, g_final2 = g_mix.reshape(1, d), g_ffn.reshape(1, d), g_final.reshape(1, d)
    g_v2, ps2 = g_v.reshape(1, A_WIDTH), pool_scale.reshape(1, B_WIDTH)
    w_s3 = w_s.reshape(N_HEADS, CHUNK, CHUNK)
    bs_t = b_s.reshape(N_HEADS, CHUNK).T
    xi, yi, ci = _position()
    pos = jnp.stack([xi, yi, ci]).astype(jnp.int32)

    order = (4 * xi + 2 * yi + ci) ^ jnp.array(ARRIVAL_ORDER, jnp.int32)
    u = cuts(rows["w_up"], (0.25, 0.55))
    ahead = cuts(rows["w_down"], (0.6,))[1]
    proj, h1, w_in_g, w_out_g, w_pool_g = _norm_matmul_stream_call(
        "proj_fwd", xs, g_mix2, sb["w_in"], order, None, 0, 0, _Comm(
            [sb["w_out"], sb["w_pool"]], [], [gathered_shape("w_out"), gathered_shape("w_pool")],
            lambda s, l: [_gather_to_neighbours(s[0], l[0], 0, rows["w_out"]) + _everyone(s[1], l[1])]),
        lambda a: (a,), (F32,))
    w_pool_g = w_pool_g.reshape(N_DEV, 4, GROUP // N_DEV, GROUP)
    mixed, w_out_g, w_up_g = _mixer_fwd_call(proj, w_s3, bs_t, g_v2, w_pool_g, ps2, _Comm(
        [sb["w_up"]], [w_out_g], [gathered_shape("w_up")],
        lambda s, l: [_gather_relay(l[0], 0, rows["w_out"]) + _gather_to_neighbours(s[0], l[1], u[0], u[1]),
                      _gather_diagonal_pass_on(l[0], 0, rows["w_out"])]))
    w_out_f = w_out_g.reshape(-1, d)
    x2, w_up_g = _out_proj_call(mixed, w_out_f, xs, _Comm(
        [sb["w_up"]], [w_up_g], [],
        lambda s, l: [_gather_relay(l[0], u[0], u[1]) + _gather_to_neighbours(s[0], l[0], u[1], u[2]),
                      _gather_diagonal_pass_on(l[0], u[0], u[1])]))

    def relu2_and_slope(a):
        r = jnp.maximum(a, 0.0)
        return r * r, 2.0 * r

    act, dact_da, h2, w_up_g, w_down_g = _norm_matmul_stream_call(
        "up_fwd", x2, g_ffn2, sb["w_up"], order, w_up_g, u[1], u[2], _Comm(
            [sb["w_down"]], [], [gathered_shape("w_down")],
            lambda s, l: [_gather_to_neighbours(s[0], l[0], 0, ahead)]),
        relu2_and_slope, (BF16, BF16))
    y, w_down_g = _down_call(act, sb["w_down"], order, w_down_g, ahead)
    w_down_f = w_down_g.reshape(-1, d)
    loss_part, dx3, dx3b, dg_final = _loss_call(y, x2, target, g_final2)

    def pair_sum(k, grad, got):
        return _pair_sum_call(k + "_pair_sum", pos, grad, got)

    def finish(k, grad, got_pair, got_chips):
        s = shard2d[k]
        outs = _final_call(k + "_adamw", pos, grad, got_pair, got_chips, s, mom[k].reshape(s.shape),
                           var[k].reshape(s.shape))
        return [o.reshape(weights[k].shape) for o in outs]

    result = {}
    (gw_down,) = _wgrad_call("w_down_grad", act, dx3b, None, None, t1=1024, t2=2048)
    gw_down = gw_down.reshape(N_DEV, -1, d)
    da, pair_down = _dact_call(dx3b, w_down_f, dact_da, _Comm(
        [gw_down], [], [landing(4, gw_down)], lambda s, l: [_pair_exchange(s[0], l[0])]))
    sums_down = pair_sum("w_down", gw_down, pair_down)
    dn = cuts(rows["w_down"], (0.8,))
    gw_up, got = _wgrad_call("w_up_grad", h2, da, N_DEV, w_up_g.shape[2], _Comm(
        [sums_down], [], [landing(3, sums_down)],
        lambda s, l: [_chip_exchange(s[0], l[0], dn[0], dn[1])]), t1=2048)
    dh2, got, pair_up = _dgrad_blocked_call("dh2_bwd", da, w_up_g, _Comm(
        [sums_down, gw_up], [got], [landing(4, gw_up)],
        lambda s, l: [_chip_exchange(s[0], l[0], dn[1], dn[2]) + _pair_exchange(s[1], l[1])]), merge=2)
    result["w_down"] = finish("w_down", gw_down, pair_down, got)
    sums_up = pair_sum("w_up", gw_up, pair_up)
    v = cuts(rows["w_up"], (0.25, 0.45, 0.66))
    dx2, dx2b, dg_ffn, got_up = _norm_bwd_call("ffn_norm_bwd", dh2, x2, dx3, g_ffn2, True, _Comm(
        [sums_up], [], [landing(3, sums_up)], lambda s, l: [_chip_exchange(s[0], l[0], v[0], v[1])]))
    dmixed, got_up = _dmixed_call(dx2b, w_out_f, _Comm(
        [sums_up], [got_up], [], lambda s, l: [_chip_exchange(s[0], l[0], v[1], v[2])]))
    gw_out, got_up = _wgrad_call("w_out_grad", mixed, dx2b, None, None, _Comm(
        [sums_up], [got_up], [], lambda s, l: [_chip_exchange(s[0], l[0], v[2], v[3])]), t1=2048, t2=1024)
    gw_out = gw_out.reshape(N_DEV, -1, d)
    dproj, dw_s, dbs_t, dg_v, dps, dw_pool, got_up, pair_out = _mixer_bwd_call(
        proj, dmixed, w_s3, bs_t, g_v2, w_pool_g, ps2, _Comm(
            [sums_up, gw_out], [got_up], [landing(4, gw_out)],
            lambda s, l: [_chip_exchange(s[0], l[0], v[3], v[4]) + _pair_exchange(s[1], l[1])]))
    result["w_up"] = finish("w_up", gw_up, pair_up, got_up)
    sums_out = pair_sum("w_out", gw_out, pair_out)
    gw_pool = dw_pool.astype(BF16).reshape(N_DEV, -1, GROUP)
    early = dict(g_v=dg_v, w_s=dw_s, b_s=dbs_t.T, pool_scale=dps, g_ffn=dg_ffn, g_final=dg_final)
    packed = _pack([early[k] for k in _SMALL_EARLY] + [loss_part])
    early_rows = packed.shape[0]
    gw_in, got, pair_pool, parts_early = _wgrad_call("w_in_grad", h1, dproj, N_DEV, w_in_g.shape[2], _Comm(
        [sums_out, gw_pool, packed], [], [landing(3, sums_out), landing(4, gw_pool), from_everyone(packed)],
        lambda s, l: [_chip_exchange(s[0], l[0], 0, rows["w_out"]) + _pair_exchange(s[1], l[1])
                      + _gather_first(s[2], l[2], 0, early_rows)]), t1=2048, merge=MERGE_W_IN)
    result["w_out"] = finish("w_out", gw_out, pair_out, got)
    sums_pool = pair_sum("w_pool", gw_pool, pair_pool)
    (pair_in,) = _comm_call("pair_exchange_w_in", _Comm(
        [gw_in], [], [landing(4, gw_in)], lambda s, l: [_pair_exchange(s[0], l[0])]))
    sums_in = pair_sum("w_in", gw_in, pair_in)
    dh1, parts_early, got, got_pool = _dgrad_blocked_call("dh1_bwd", dproj, w_in_g, _Comm(
        [sums_in, sums_pool], [parts_early], [landing(3, sums_in), landing(3, sums_pool)],
        lambda s, l: [_chip_exchange(s[0], l[1], 0, rows["w_in"]) + _chip_exchange(s[1], l[2], 0, rows["w_pool"])
                      + _gather_pass_on(l[0], 0, early_rows)]),
        merge=MERGE_W_IN)
    result["w_in"] = finish("w_in", gw_in, pair_in, got)
    result["w_pool"] = finish("w_pool", gw_pool, pair_pool, got_pool)
    grad_x, dg_mix = _norm_bwd_call("mix_norm_bwd", dh1, xs, dx2, g_mix2, False)
    packed = _pack([dg_mix])
    (parts_late,) = _comm_call("gather_g_mix_grad", _Comm(
        [packed], [], [from_everyone(packed)], lambda s, l: [_everyone(s[0], l[0])]))

    for names, parts, tag in ((_SMALL_EARLY, parts_early, "small_adamw"), (("g_mix",), parts_late, "g_mix_adamw")):
        outs = _small_final_call(tag, parts, _pack([weights[k] for k in names]), _pack([mom[k] for k in names]),
                                 _pack([var[k] for k in names]))
        if tag == "small_adamw":
            loss = outs[0][-1, 0]
        like = [weights[k] for k in names]
        unpacked = [_unpack(o, like) for o in outs]
        for idx, k in enumerate(names):
            result[k] = [unpacked[q][idx] for q in range(4)]

    grads = [result[k][0] for k in _ORDER]
    deltas = [result[k][1] for k in _ORDER]
    new_m = [result[k][2] for k in _ORDER]
    new_v = [result[k][3] for k in _ORDER]
    return (loss, grad_x.reshape(x.shape), *grads, *deltas, *new_m, *new_v)
```

```python
import functools
import math

import jax
import jax.numpy as jnp
from jax import lax
from jax.experimental import pallas as pl
from jax.experimental.pallas import tpu as pltpu

F32 = jnp.float32
BF16 = jnp.bfloat16
MESH = pl.DeviceIdType.MESH

N_DEV = 8
EPS = 1e-6
CHUNK = 128
N_HEADS = 8
A_WIDTH = 1024
B_WIDTH = 1024
POOL_WINDOWS = (2, 4, 8, 16)
GROUP = 256
HALO = 16
LANES = 128

ADAM_LR = 0.001
ADAM_B1 = 0.9
ADAM_B2 = 0.999
ADAM_EPS = 1e-08
ADAM_WD = 0.01
ADAM_STEP = 10
ADAM_C1 = 1.0 - ADAM_B1 ** ADAM_STEP
ADAM_C2 = 1.0 - ADAM_B2 ** ADAM_STEP

VMEM_LIMIT = 56 * 1024 * 1024
MERGE_W_IN = 2

_GELU_C = math.sqrt(2.0 / math.pi)


def _params(*sem):
    return pltpu.CompilerParams(dimension_semantics=sem, vmem_limit_bytes=VMEM_LIMIT)


def _gelu(x):
    return 0.5 * x * (1.0 + jnp.tanh(_GELU_C * (x + 0.044715 * x * x * x)))


def _gelu_and_grad(x):
    t = jnp.tanh(_GELU_C * (x + 0.044715 * x * x * x))
    g = 0.5 * x * (1.0 + t)
    dg = 0.5 * (1.0 + t) + 0.5 * x * (1.0 - t * t) * (_GELU_C * (1.0 + 3.0 * 0.044715 * x * x))
    return g, dg


def _dot_nn(a, b):
    return lax.dot_general(a, b, (((1,), (0,)), ((), ())), preferred_element_type=F32)


def _dot_nt(a, b):
    return lax.dot_general(a, b, (((1,), (1,)), ((), ())), preferred_element_type=F32)


def _dot_tn(a, b):
    return lax.dot_general(a, b, (((0,), (0,)), ((), ())), preferred_element_type=F32)


def _rms_rows(x):
    r = lax.rsqrt(jnp.mean(x * x, axis=-1, keepdims=True) + EPS)
    return x * r, r


def _rms_bwd_rows(dn, n, r):
    return r * (dn - n * jnp.mean(dn * n, axis=-1, keepdims=True))


def _tile(n, want):
    t = min(n, want)
    assert n % t == 0, (n, want)
    return t


_ANY = pl.BlockSpec(memory_space=pl.ANY)

SIBLING = 1
CHIPS = (4, 2, 6)


def _position():
    return lax.axis_index("x"), lax.axis_index("y"), lax.axis_index("c")


def _me():
    x, y, c = _position()
    return 4 * x + 2 * y + c


def _peer(rel):
    x, y, c = _position()
    return (x ^ ((rel >> 2) & 1), y ^ ((rel >> 1) & 1), c ^ (rel & 1))


class _Comm:
    def __init__(self, srcs, lands, new, plan):
        self.srcs, self.lands, self.new, self.plan = list(srcs), list(lands), list(new), plan


def _make_copies(phases, send_sems, recv_sems, local_sems):
    out, nr, nl = [], 0, 0
    for phase in phases:
        cps = []
        for item in phase:
            if item[0] == "local":
                cps.append(pltpu.make_async_copy(item[1], item[2], local_sems.at[nl]))
                nl += 1
            else:
                cps.append(pltpu.make_async_remote_copy(
                    src_ref=item[1], dst_ref=item[2], send_sem=send_sems.at[nr], recv_sem=recv_sems.at[nr],
                    device_id=_peer(item[3]), device_id_type=MESH))
                nr += 1
        out.append(cps)
    return out


def _count_copies(comm):
    phases = comm.plan([_FakeRef() for _ in comm.srcs], [_FakeRef() for _ in range(len(comm.lands) + len(comm.new))])
    items = [it for ph in phases for it in ph]
    return sum(it[0] == "remote" for it in items), sum(it[0] == "local" for it in items)


class _FakeRef:
    def __getitem__(self, idx):
        return self

    @property
    def at(self):
        return self


def _carrier_call(body, args, comm, *, name, grid, in_specs, out_specs, out_shape, scratch_shapes=(), sem):
    if not isinstance(out_shape, (list, tuple)):
        out_specs, out_shape = [out_specs], [out_shape]
    out_specs, out_shape, scratch_shapes = list(out_specs), list(out_shape), list(scratch_shapes)
    if comm is None:
        res = pl.pallas_call(body, name=name, grid=grid, in_specs=list(in_specs), out_specs=out_specs,
                             out_shape=out_shape, scratch_shapes=scratch_shapes, compiler_params=_params(*sem))(*args)
        return list(res)
    n_in, n_out, n_scr = len(args), len(out_shape), len(scratch_shapes)
    ns, nl, nn = len(comm.srcs), len(comm.lands), len(comm.new)
    n_remote, n_local = _count_copies(comm)
    steps = math.prod(grid)

    def wrapped(*refs):
        ins, srcs = refs[:n_in], refs[n_in:n_in + ns]
        o = n_in + ns + nl
        outs, lands = refs[o:o + n_out], refs[o + n_out:o + n_out + nl + nn]
        scr = refs[o + n_out + nl + nn:]
        phases = _make_copies(comm.plan(srcs, lands), *scr[n_scr:])
        assert len(phases) == 1 or (len(phases) == 2 and steps >= 3)
        step = functools.reduce(lambda acc, a: acc * grid[a] + pl.program_id(a), range(len(grid)), 0)

        @pl.when(step == 0)
        def _():
            for cp in phases[0]:
                cp.start()

        if len(phases) == 2:
            @pl.when(step == steps * 3 // 4)
            def _():
                for cp in phases[0]:
                    cp.wait()
                for cp in phases[1]:
                    cp.start()

        body(*ins, *outs, *scr[:n_scr])

        @pl.when(step == steps - 1)
        def _():
            for cp in phases[-1]:
                cp.wait()

    land_shapes = [jax.ShapeDtypeStruct(a.shape, a.dtype) for a in comm.lands] + comm.new
    sems = [pltpu.SemaphoreType.DMA((max(n_remote, 1),)), pltpu.SemaphoreType.DMA((max(n_remote, 1),)),
            pltpu.SemaphoreType.DMA((max(n_local, 1),))]
    res = pl.pallas_call(
        wrapped, name=name, grid=grid,
        in_specs=list(in_specs) + [_ANY] * (ns + nl), out_specs=out_specs + [_ANY] * (nl + nn),
        out_shape=out_shape + land_shapes, scratch_shapes=scratch_shapes + sems,
        input_output_aliases={n_in + ns + k: n_out + k for k in range(nl)},
        compiler_params=_params(*sem))(*args, *comm.srcs, *comm.lands)
    return list(res)


def _comm_call(name, comm):
    ns, nl, nn = len(comm.srcs), len(comm.lands), len(comm.new)
    n_remote, n_local = _count_copies(comm)

    def body(*refs):
        srcs, lands, sems = refs[:ns], refs[ns + nl:ns + nl + nl + nn], refs[ns + nl + nl + nn:]
        for copies in _make_copies(comm.plan(srcs, lands), *sems):
            for cp in copies:
                cp.start()
            for cp in copies:
                cp.wait()

    land_shapes = [jax.ShapeDtypeStruct(a.shape, a.dtype) for a in comm.lands] + comm.new
    res = pl.pallas_call(
        body, name=name,
        in_specs=[_ANY] * (ns + nl), out_specs=[_ANY] * (nl + nn), out_shape=land_shapes,
        scratch_shapes=[pltpu.SemaphoreType.DMA((max(n_remote, 1),)), pltpu.SemaphoreType.DMA((max(n_remote, 1),)),
                        pltpu.SemaphoreType.DMA((max(n_local, 1),))],
        input_output_aliases={ns + k: k for k in range(nl)},
    )(*comm.srcs, *comm.lands)
    return list(res)


def _rows(ref, block, r0, r1):
    return ref.at[block, pl.ds(r0, r1 - r0)]


def _gather_first(shard, land, r0, r1):
    src = shard.at[pl.ds(r0, r1 - r0)]
    dst = _rows(land, _me(), r0, r1)
    return [("local", src, dst)] + [("remote", src, dst, rel) for rel in (SIBLING,) + CHIPS]


def _gather_pass_on(land, r0, r1):
    return [("remote", _rows(land, _me() ^ rel, r0, r1), _rows(land, _me() ^ rel, r0, r1), SIBLING) for rel in CHIPS]


def _split_rows(r0, r1):
    m = (r0 + r1) // 2 // 16 * 16
    return (r0, m), (m, r1)


def _gather_to_neighbours(shard, land, r0, r1):
    src = shard.at[pl.ds(r0, r1 - r0)]
    dst = _rows(land, _me(), r0, r1)
    return [("local", src, dst)] + [("remote", src, dst, rel) for rel in (SIBLING, 4, 2)]


def _gather_relay(land, r0, r1):
    lo, hi = _split_rows(r0, r1)
    x_block, y_block = _me() ^ 4, _me() ^ 2
    return [("remote", _rows(land, x_block, *lo), _rows(land, x_block, *lo), 2),
            ("remote", _rows(land, y_block, *hi), _rows(land, y_block, *hi), 4),
            ("remote", _rows(land, x_block, r0, r1), _rows(land, x_block, r0, r1), SIBLING),
            ("remote", _rows(land, y_block, r0, r1), _rows(land, y_block, r0, r1), SIBLING)]


def _gather_diagonal_pass_on(land, r0, r1):
    rows = _rows(land, _me() ^ 6, r0, r1)
    return [("remote", rows, rows, SIBLING)]


def _pair_exchange(grad, land):
    _, _, c = _position()
    return [("remote", grad.at[2 * chip + (1 - c)], land.at[chip], SIBLING) for chip in range(4)]


def _chip_exchange(sums, land, r0, r1):
    return [("remote", _rows(sums, j, r0, r1), _rows(land, j, r0, r1), rel) for j, rel in enumerate(CHIPS)]


def _everyone(packed, land):
    dst = land.at[_me()]
    return [("local", packed, dst)] + [("remote", packed, dst, rel) for rel in range(1, N_DEV)]


def _pool_counts(row0, rows, win):
    pos = row0 + lax.broadcasted_iota(jnp.int32, (rows, 1), 0)
    return jnp.minimum(pos + 1, win).astype(F32)


def _window_sum_back(ext, win):
    s = ext
    k = 1
    while k < win:
        s = s + pltpu.roll(s, k, 0)
        k *= 2
    return s


def _window_sum_fwd(ext, win):
    n = ext.shape[0]
    s = ext
    k = 1
    while k < win:
        s = s + pltpu.roll(s, n - k, 0)
        k *= 2
    return s


def _mixer_fwd_call(proj, w_s, bs_t, g_v, w_pool_g, pool_scale, comm=None):
    t = proj.shape[0]
    tt = _tile(t, 512)
    nchunk = tt // CHUNK
    hb = tt // HALO

    def body(pu_ref, pv_ref, z_ref, zp_ref, ws_ref, bs_ref, gv_ref, wp_ref, ps_ref, out_ref):
        i = pl.program_id(0)
        tril = (lax.broadcasted_iota(jnp.int32, (CHUNK, CHUNK), 0)
                >= lax.broadcasted_iota(jnp.int32, (CHUNK, CHUNK), 1))
        for h in range(N_HEADS):
            cols = slice(h * CHUNK, (h + 1) * CHUNK)
            vhat, _ = _rms_rows(_gelu(pv_ref[:, cols]))
            vn = (vhat * gv_ref[:, cols]).astype(BF16)
            u = _gelu(pu_ref[:, cols])
            w = jnp.where(tril, ws_ref[h], 0.0).astype(BF16)
            bcol = bs_ref[:, h:h + 1]
            for c in range(nchunk):
                rows = slice(c * CHUNK, (c + 1) * CHUNK)
                mixed = _dot_nn(w, vn[rows]) + bcol
                out_ref[rows, cols] = (u[rows] * mixed).astype(BF16)

        zprev = jnp.where(i > 0, zp_ref[...], 0.0)
        ext = jnp.concatenate([zprev, z_ref[...]], axis=0)
        for g, win in enumerate(POOL_WINDOWS):
            cols = slice(g * GROUP, (g + 1) * GROUP)
            zg = ext[:, cols]
            s = _window_sum_back(zg, win)
            pooled = s[HALO:] / _pool_counts(i * tt, tt, win) - zg[HALO:]
            wp = wp_ref[:, g].reshape(GROUP, GROUP)
            y = _dot_nn(pooled.astype(BF16), wp)
            out_ref[:, A_WIDTH + g * GROUP:A_WIDTH + (g + 1) * GROUP] = (y * ps_ref[:, cols]).astype(BF16)

    return _carrier_call(
        body, (proj, proj, proj, proj, w_s, bs_t, g_v, w_pool_g, pool_scale), comm, name="mixer_fwd",
        grid=(t // tt,),
        in_specs=[pl.BlockSpec((tt, A_WIDTH), lambda i: (i, 0)),
                  pl.BlockSpec((tt, A_WIDTH), lambda i: (i, 1)),
                  pl.BlockSpec((tt, B_WIDTH), lambda i: (i, 2)),
                  pl.BlockSpec((HALO, B_WIDTH), lambda i: (jnp.maximum(i * hb - 1, 0), 2)),
                  pl.BlockSpec((N_HEADS, CHUNK, CHUNK), lambda i: (0, 0, 0)),
                  pl.BlockSpec((CHUNK, N_HEADS), lambda i: (0, 0)),
                  pl.BlockSpec((1, A_WIDTH), lambda i: (0, 0)),
                  pl.BlockSpec((N_DEV, 4, GROUP // N_DEV, GROUP), lambda i: (0, 0, 0, 0)),
                  pl.BlockSpec((1, B_WIDTH), lambda i: (0, 0))],
        out_specs=pl.BlockSpec((tt, A_WIDTH + B_WIDTH), lambda i: (i, 0)),
        out_shape=jax.ShapeDtypeStruct((t, A_WIDTH + B_WIDTH), BF16),
        sem=("parallel",))


def _out_proj_call(mixed, w_out, x, comm=None):
    t, d = x.shape
    k = mixed.shape[1]
    tm = _tile(t, 1024)
    tn = _tile(d, 1024)

    def body(a_ref, w_ref, x_ref, o_ref):
        o_ref[...] = x_ref[...] + _dot_nn(a_ref[...], w_ref[...])

    return _carrier_call(
        body, (mixed, w_out, x), comm, name="out_proj_fwd",
        grid=(t // tm, d // tn),
        in_specs=[pl.BlockSpec((tm, k), lambda i, j: (i, 0)),
                  pl.BlockSpec((k, tn), lambda i, j: (0, j)),
                  pl.BlockSpec((tm, tn), lambda i, j: (i, j))],
        out_specs=pl.BlockSpec((tm, tn), lambda i, j: (i, j)),
        out_shape=jax.ShapeDtypeStruct((t, d), F32),
        sem=("parallel", "parallel"))


ARRIVAL_ORDER = (0, 1, 4, 5, 2, 3, 6, 7)
CARRIED_AFTER = 3


class _StreamedGather:
    def __init__(self, shard_ref, land_ref, wbuf, pre0, r0, send_sems, recv_sems, local_sem, fetch_sems):
        self.shard, self.land, self.wbuf, self.fetch_sems = shard_ref, land_ref, wbuf, fetch_sems
        end = shard_ref.shape[0]
        me = _me()
        self.me = me

        def remote(k, src, dst, rel):
            return pltpu.make_async_remote_copy(src_ref=src, dst_ref=dst, send_sem=send_sems.at[k],
                                                recv_sem=recv_sems.at[k], device_id=_peer(rel), device_id_type=MESH)

        def same_rows(k, block, a, b, rel):
            ref = land_ref.at[block, pl.ds(a, b - a)]
            return remote(k, ref, ref, rel)

        src = shard_ref.at[pl.ds(r0, end - r0)]
        dst = land_ref.at[me, pl.ds(r0, end - r0)]
        self.mine = pltpu.make_async_copy(src, dst, local_sem)
        self.first = [remote(k, src, dst, rel) for k, rel in enumerate((SIBLING, 4, 2))]
        lo, hi = _split_rows(r0, end)
        self.relay = [same_rows(3, me ^ 4, *lo, 2), same_rows(4, me ^ 2, *hi, 4)]
        self.passed = [same_rows(5, me ^ 4, r0, end, SIBLING), same_rows(6, me ^ 2, r0, end, SIBLING),
                       same_rows(7, me ^ 6, pre0, end, SIBLING)]
        self.early_relay, self.early_passed = [], []
        if r0 > pre0:
            lo, hi = _split_rows(pre0, r0)
            self.early_relay = [same_rows(8, me ^ 4, *lo, 2), same_rows(9, me ^ 2, *hi, 4)]
            self.early_passed = [same_rows(10, me ^ 4, pre0, r0, SIBLING), same_rows(11, me ^ 2, pre0, r0, SIBLING)]

    def _fetch(self, q):
        src = self.shard if q == 0 else self.land.at[self.me ^ ARRIVAL_ORDER[q]]
        return pltpu.make_async_copy(src, self.wbuf.at[q % 2], self.fetch_sems.at[q % 2])

    def start(self):
        self.mine.start()
        for cp in self.first + self.early_relay + self.early_passed:
            cp.start()
        self._fetch(0).start()

    def arrive(self, q):
        if q == 1:
            self.first[0].wait_recv()
        elif q in (2, 4):
            j = q // 2 - 1
            self.first[1 + j].wait_recv()
            self.relay[j].start()
            self.passed[j].start()
        elif q in (3, 5):
            j = q // 2 - 1
            self.passed[j].wait_recv()
            if self.early_passed:
                self.early_passed[j].wait_recv()
        elif q == 6:
            for cp in self.relay + self.early_relay:
                cp.wait_recv()
            self.passed[2].start()
        else:
            self.passed[2].wait_recv()
        self._fetch(q).start()

    def wait_fetch(self, slot):
        pltpu.make_async_copy(self.shard, self.wbuf.at[slot], self.fetch_sems.at[slot]).wait()

    def finish(self):
        for cp in self.first + self.relay + self.passed + self.early_relay + self.early_passed:
            cp.wait_send()
        self.mine.wait()


_STREAM_SEMS = [pltpu.SemaphoreType.DMA((12,)), pltpu.SemaphoreType.DMA((12,)), pltpu.SemaphoreType.DMA,
                pltpu.SemaphoreType.DMA((2,))]


def _stream_steps(gather, p, i, ni):
    @pl.when((p == 0) & (i == 0))
    def _():
        gather.start()

    @pl.when(i == 0)
    def _():
        gather.wait_fetch(p % 2)

    @pl.when(i == ni - 1)
    def _():
        for q in range(1, N_DEV):
            @pl.when(p == q - 1)
            def _():
                gather.arrive(q)


def _norm_matmul_stream_call(name, x, g, shard, order, land, pre0, r0, comm, epilogue, out_dtypes):
    t, d = x.shape
    cb = shard.shape[1]
    tm = _tile(t, 1024)
    ni = t // tm
    n_sems = len(_STREAM_SEMS)
    assert not comm.lands
    ns, nn, no = len(comm.srcs), len(comm.new), len(out_dtypes)
    n_remote, n_local = _count_copies(comm)
    has_land = land is not None

    def body(order_ref, x_ref, g_ref, shard_ref, *refs):
        refs = refs[has_land:]
        srcs, out_refs, (h_ref, land_ref) = refs[:ns], refs[ns:ns + no], refs[ns + no:ns + no + 2]
        new = refs[ns + no + 2:ns + no + 2 + nn]
        wbuf, sems = refs[ns + no + 2 + nn], refs[ns + no + 3 + nn:]
        p, i = pl.program_id(0), pl.program_id(1)
        gather = _StreamedGather(shard_ref, land_ref, wbuf, pre0, r0, *sems[:n_sems])
        (carried,) = _make_copies(comm.plan(srcs, new), *sems[n_sems:])
        rows = pl.ds(pl.multiple_of(i * tm, tm), tm)
        _stream_steps(gather, p, i, ni)

        @pl.when((p == CARRIED_AFTER) & (i == ni - 1))
        def _():
            for cp in carried:
                cp.start()

        @pl.when(p == 0)
        def _():
            n, _ = _rms_rows(x_ref[...])
            h_ref[rows, :] = (n * g_ref[...]).astype(BF16)

        tails = epilogue(_dot_nn(h_ref[rows, :], wbuf[p % 2]))
        for out_ref, tail, dt in zip(out_refs, tails, out_dtypes):
            out_ref[...] = tail.astype(dt)

        @pl.when((p == N_DEV - 1) & (i == ni - 1))
        def _():
            gather.finish()
            for cp in carried:
                cp.wait()

    carried_sems = [pltpu.SemaphoreType.DMA((max(n_remote, 1),)), pltpu.SemaphoreType.DMA((max(n_remote, 1),)),
                    pltpu.SemaphoreType.DMA((max(n_local, 1),))]
    return pl.pallas_call(
        body, name=name,
        grid_spec=pltpu.PrefetchScalarGridSpec(
            num_scalar_prefetch=1, grid=(N_DEV, ni),
            in_specs=[pl.BlockSpec((tm, d), lambda p, i, o: (jnp.where(p == 0, i, ni - 1), 0)),
                      pl.BlockSpec((1, d), lambda p, i, o: (0, 0)),
                      _ANY] + [_ANY] * (has_land + ns),
            out_specs=[pl.BlockSpec((tm, cb), lambda p, i, o: (i, o[p]))] * no
                      + [pl.BlockSpec(memory_space=pltpu.VMEM), _ANY] + [_ANY] * nn,
            scratch_shapes=[pltpu.VMEM((2, d, cb), BF16)] + _STREAM_SEMS + carried_sems),
        out_shape=[jax.ShapeDtypeStruct((t, N_DEV * cb), dt) for dt in out_dtypes]
                  + [jax.ShapeDtypeStruct((t, d), BF16), jax.ShapeDtypeStruct((N_DEV, d, cb), BF16)] + comm.new,
        input_output_aliases={4: no + 1} if has_land else {},
        compiler_params=_params("arbitrary", "arbitrary"),
    )(order, x, g, shard, *([land] if has_land else []), *comm.srcs)


def _down_call(act, shard, order, land, r0):
    t = act.shape[0]
    rb, d = shard.shape
    tm = _tile(t, 1024)
    ni = t // tm

    def body(order_ref, a_ref, shard_ref, land_in_ref, y_ref, land_ref, wbuf, *sems):
        p, i = pl.program_id(0), pl.program_id(1)
        gather = _StreamedGather(shard_ref, land_ref, wbuf, 0, r0, *sems)
        rows = pl.ds(pl.multiple_of(i * tm, tm), tm)
        _stream_steps(gather, p, i, ni)
        @pl.when(p == 0)
        def _():
            y_ref[rows, :] = _dot_nn(a_ref[...], wbuf[0])

        @pl.when(p > 0)
        def _():
            y_ref[rows, :] += _dot_nn(a_ref[...], wbuf[p % 2])

        @pl.when((p == N_DEV - 1) & (i == ni - 1))
        def _():
            gather.finish()

    return pl.pallas_call(
        body, name="down_fwd",
        grid_spec=pltpu.PrefetchScalarGridSpec(
            num_scalar_prefetch=1, grid=(N_DEV, ni),
            in_specs=[pl.BlockSpec((tm, rb), lambda p, i, o: (i, o[p])), _ANY, _ANY],
            out_specs=[pl.BlockSpec(memory_space=pltpu.VMEM), _ANY],
            scratch_shapes=[pltpu.VMEM((2, rb, d), BF16)] + _STREAM_SEMS),
        out_shape=[jax.ShapeDtypeStruct((t, d), F32), jax.ShapeDtypeStruct((N_DEV, rb, d), BF16)],
        input_output_aliases={3: 1},
        compiler_params=_params("arbitrary", "arbitrary"),
    )(order, act, shard, land)


def _loss_call(y, x2, target, g_final):
    t, d = y.shape
    tr = _tile(t, 512)

    def body(y_ref, x_ref, tg_ref, g_ref, loss_ref, dx_ref, dxb_ref, dg_ref):
        @pl.when(pl.program_id(0) == 0)
        def _():
            loss_ref[...] = jnp.zeros_like(loss_ref)
            dg_ref[...] = jnp.zeros_like(dg_ref)

        n, r = _rms_rows(x_ref[...] + y_ref[...])
        err = n * g_ref[...] - tg_ref[...]
        loss_ref[...] += 0.5 * jnp.sum(jnp.mean(err * err, axis=-1, keepdims=True))
        dy = err * (1.0 / d)
        dg_ref[...] += jnp.sum(dy * n, axis=0, keepdims=True)
        dx = _rms_bwd_rows(dy * g_ref[...], n, r)
        dx_ref[...] = dx
        dxb_ref[...] = dx.astype(BF16)

    return pl.pallas_call(
        body, name="loss_head",
        grid=(t // tr,),
        in_specs=[pl.BlockSpec((tr, d), lambda i: (i, 0)),
                  pl.BlockSpec((tr, d), lambda i: (i, 0)),
                  pl.BlockSpec((tr, d), lambda i: (i, 0)),
                  pl.BlockSpec((1, d), lambda i: (0, 0))],
        out_specs=[pl.BlockSpec((8, LANES), lambda i: (0, 0)),
                   pl.BlockSpec((tr, d), lambda i: (i, 0)),
                   pl.BlockSpec((tr, d), lambda i: (i, 0)),
                   pl.BlockSpec((1, d), lambda i: (0, 0))],
        out_shape=[jax.ShapeDtypeStruct((8, LANES), F32), jax.ShapeDtypeStruct((t, d), F32),
                   jax.ShapeDtypeStruct((t, d), BF16), jax.ShapeDtypeStruct((1, d), F32)],
        compiler_params=_params("arbitrary"),
    )(y, x2, target, g_final)


def _norm_bwd_call(name, dh, x, dres, g, want_bf16, comm=None):
    t, d = x.shape
    tr = _tile(t, 256)

    def body(dh_ref, x_ref, dres_ref, g_ref, dx_ref, *rest):
        dg_ref = rest[-1]

        @pl.when(pl.program_id(0) == 0)
        def _():
            dg_ref[...] = jnp.zeros_like(dg_ref)

        n, r = _rms_rows(x_ref[...])
        dh = dh_ref[...]
        dg_ref[...] += jnp.sum(dh * n, axis=0, keepdims=True)
        dx = dres_ref[...] + _rms_bwd_rows(dh * g_ref[...], n, r)
        dx_ref[...] = dx
        if want_bf16:
            rest[0][...] = dx.astype(BF16)

    row = pl.BlockSpec((tr, d), lambda i: (i, 0))
    vec = pl.BlockSpec((1, d), lambda i: (0, 0))
    out_specs = [row] + ([row] if want_bf16 else []) + [vec]
    out_shape = ([jax.ShapeDtypeStruct((t, d), F32)]
                 + ([jax.ShapeDtypeStruct((t, d), BF16)] if want_bf16 else [])
                 + [jax.ShapeDtypeStruct((1, d), F32)])
    return _carrier_call(
        body, (dh, x, dres, g), comm, name=name,
        grid=(t // tr,),
        in_specs=[row, row, row, vec],
        out_specs=out_specs, out_shape=out_shape,
        sem=("arbitrary",))


def _dact_call(dx3b, w_down, act, comm=None):
    t, d = dx3b.shape
    f = w_down.shape[0]
    tm = _tile(t, 1024)
    tn = _tile(f, 2048)

    def body(g_ref, w_ref, act_ref, o_ref):
        dact = _dot_nt(g_ref[...], w_ref[...])
        o_ref[...] = (dact * act_ref[...].astype(F32)).astype(BF16)

    return _carrier_call(
        body, (dx3b, w_down, act), comm, name="dact_bwd",
        grid=(t // tm, f // tn),
        in_specs=[pl.BlockSpec((tm, d), lambda i, j: (i, 0)),
                  pl.BlockSpec((tn, d), lambda i, j: (j, 0)),
                  pl.BlockSpec((tm, tn), lambda i, j: (i, j))],
        out_specs=pl.BlockSpec((tm, tn), lambda i, j: (i, j)),
        out_shape=jax.ShapeDtypeStruct((t, f), BF16),
        sem=("parallel", "parallel"))


def _wgrad_call(name, a, b, out_blocks, out_block_cols, comm=None, *, t1, t2=None, merge=1):
    t, k1 = a.shape
    k2 = b.shape[1]
    tt = _tile(t, 2048)
    t1 = _tile(k1, t1)
    t2 = _tile(k2, t2) if out_blocks is None else merge * out_block_cols
    nk = t // tt

    def body(a_ref, b_ref, o_ref, acc_ref):
        k = pl.program_id(2)

        @pl.when(k == 0)
        def _():
            acc_ref[...] = _dot_tn(a_ref[...], b_ref[...])

        @pl.when(k > 0)
        def _():
            acc_ref[...] += _dot_tn(a_ref[...], b_ref[...])

        @pl.when(k == nk - 1)
        def _():
            if out_blocks is None:
                o_ref[...] = acc_ref[...].astype(BF16)
            else:
                for blk in range(merge):
                    o_ref[blk] = acc_ref[:, blk * out_block_cols:(blk + 1) * out_block_cols].astype(BF16)

    if out_blocks is None:
        out_spec = pl.BlockSpec((t1, t2), lambda i, j, k: (i, j))
        out_shape = jax.ShapeDtypeStruct((k1, k2), BF16)
    else:
        out_spec = pl.BlockSpec((merge, t1, out_block_cols), lambda i, j, k: (j, i, 0))
        out_shape = jax.ShapeDtypeStruct((out_blocks, k1, out_block_cols), BF16)
    return _carrier_call(
        body, (a, b), comm, name=name,
        grid=(k1 // t1, k2 // t2, nk),
        in_specs=[pl.BlockSpec((tt, t1), lambda i, j, k: (k, i)),
                  pl.BlockSpec((tt, t2), lambda i, j, k: (k, j))],
        out_specs=out_spec, out_shape=out_shape,
        scratch_shapes=[pltpu.VMEM((t1, t2), F32)],
        sem=("parallel", "parallel", "arbitrary"))


def _dgrad_blocked_call(name, g, w_g, comm=None, *, merge=1):
    t = g.shape[0]
    nb, d, cb = w_g.shape
    tm = _tile(t, 1024)
    tn = _tile(d, 2048)
    tk = merge * cb

    def body(g_ref, w_ref, o_ref):
        def product():
            w = w_ref[0] if merge == 1 else jnp.concatenate([w_ref[b] for b in range(merge)], axis=1)
            return _dot_nt(g_ref[...], w)

        @pl.when(pl.program_id(2) == 0)
        def _():
            o_ref[...] = product()

        @pl.when(pl.program_id(2) > 0)
        def _():
            o_ref[...] += product()

    return _carrier_call(
        body, (g, w_g), comm, name=name,
        grid=(t // tm, d // tn, nb // merge),
        in_specs=[pl.BlockSpec((tm, tk), lambda i, j, k: (i, k)),
                  pl.BlockSpec((merge, tn, cb), lambda i, j, k: (k, j, 0))],
        out_specs=pl.BlockSpec((tm, tn), lambda i, j, k: (i, j)),
        out_shape=jax.ShapeDtypeStruct((t, d), F32),
        sem=("parallel", "parallel", "arbitrary"))


def _dmixed_call(dx2b, w_out, comm=None):
    t, d = dx2b.shape
    e = w_out.shape[0]
    tm = _tile(t, 1024)
    tn = _tile(e, 1024)

    def body(g_ref, w_ref, o_ref):
        o_ref[...] = _dot_nt(g_ref[...], w_ref[...])

    return _carrier_call(
        body, (dx2b, w_out), comm, name="dmixed_bwd",
        grid=(t // tm, e // tn),
        in_specs=[pl.BlockSpec((tm, d), lambda i, j: (i, 0)),
                  pl.BlockSpec((tn, d), lambda i, j: (j, 0))],
        out_specs=pl.BlockSpec((tm, tn), lambda i, j: (i, j)),
        out_shape=jax.ShapeDtypeStruct((t, e), F32),
        sem=("parallel", "parallel"))


def _mixer_bwd_call(proj, dmixed, w_s, bs_t, g_v, w_pool_g, pool_scale, comm=None):
    t = proj.shape[0]
    tt = _tile(t, 512)
    nchunk = tt // CHUNK
    hb = tt // HALO
    last_halo = t // HALO - 1
    nsteps = t // tt
    rb = GROUP // N_DEV

    def body(pu_ref, pv_ref, z_ref, zp_ref, da_ref, db_ref, dbn_ref, ws_ref, bs_ref, gv_ref, wp_ref, ps_ref,
             dproj_ref, dws_ref, dbs_ref, dgv_ref, dps_ref, dwp_ref):
        i = pl.program_id(0)

        @pl.when(i == 0)
        def _():
            dws_ref[...] = jnp.zeros_like(dws_ref)
            dbs_ref[...] = jnp.zeros_like(dbs_ref)
            dgv_ref[...] = jnp.zeros_like(dgv_ref)
            dps_ref[...] = jnp.zeros_like(dps_ref)
            dwp_ref[...] = jnp.zeros_like(dwp_ref)

        tril = (lax.broadcasted_iota(jnp.int32, (CHUNK, CHUNK), 0)
                >= lax.broadcasted_iota(jnp.int32, (CHUNK, CHUNK), 1))
        for h in range(N_HEADS):
            cols = slice(h * CHUNK, (h + 1) * CHUNK)
            v, dv_dpv = _gelu_and_grad(pv_ref[:, cols])
            vhat, rv = _rms_rows(v)
            gv = gv_ref[:, cols]
            vn = (vhat * gv).astype(BF16)
            u, du_dpu = _gelu_and_grad(pu_ref[:, cols])
            w = jnp.where(tril, ws_ref[h], 0.0).astype(BF16)
            bcol = bs_ref[:, h:h + 1]
            dout = da_ref[:, cols]
            dmix = dout * u
            dmix_b = dmix.astype(BF16)
            dws = jnp.zeros((CHUNK, CHUNK), F32)
            dbs = jnp.zeros((CHUNK, 1), F32)
            dvn_parts = []
            du_parts = []
            for c in range(nchunk):
                rows = slice(c * CHUNK, (c + 1) * CHUNK)
                mixed = _dot_nn(w, vn[rows]) + bcol
                du_parts.append(dout[rows] * mixed)
                dvn_parts.append(_dot_tn(w, dmix_b[rows]))
                dws = dws + _dot_nt(dmix_b[rows], vn[rows])
                dbs = dbs + jnp.sum(dmix[rows], axis=1, keepdims=True)
            dws_ref[h] += jnp.where(tril, dws, 0.0)
            dbs_ref[:, h:h + 1] += dbs
            dvn = jnp.concatenate(dvn_parts, axis=0)
            du = jnp.concatenate(du_parts, axis=0)
            dgv_ref[:, cols] += jnp.sum(dvn * vhat, axis=0, keepdims=True)
            dv = _rms_bwd_rows(dvn * gv, vhat, rv)
            dproj_ref[:, cols] = (du * du_dpu).astype(BF16)
            dproj_ref[:, A_WIDTH + h * CHUNK:A_WIDTH + (h + 1) * CHUNK] = (dv * dv_dpv).astype(BF16)

        zprev = jnp.where(i > 0, zp_ref[...], 0.0)
        ext = jnp.concatenate([zprev, z_ref[...]], axis=0)
        dnext = jnp.where(i < nsteps - 1, dbn_ref[...], 0.0)
        dext = jnp.concatenate([db_ref[...], dnext], axis=0)
        for g, win in enumerate(POOL_WINDOWS):
            cols = slice(g * GROUP, (g + 1) * GROUP)
            zg = ext[:, cols]
            pooled = _window_sum_back(zg, win)[HALO:] / _pool_counts(i * tt, tt, win) - zg[HALO:]
            pooled_b = pooled.astype(BF16)
            wp = wp_ref[:, g].reshape(GROUP, GROUP)
            y = _dot_nn(pooled_b, wp)
            dout = dext[:, cols]
            dps_ref[:, cols] += jnp.sum(dout[:tt] * y, axis=0, keepdims=True)
            dy_b = (dout * ps_ref[:, cols]).astype(BF16)
            dwp_ref[:, g] += _dot_tn(pooled_b, dy_b[:tt]).reshape(N_DEV, rb, GROUP)
            dpooled = _dot_nt(dy_b, wp)
            q = dpooled / _pool_counts(i * tt, tt + HALO, win)
            dz = _window_sum_fwd(q, win)[:tt] - dpooled[:tt]
            dproj_ref[:, 2 * A_WIDTH + g * GROUP:2 * A_WIDTH + (g + 1) * GROUP] = dz.astype(BF16)

    def full(shape):
        return pl.BlockSpec(shape, lambda i: (0,) * len(shape))

    return _carrier_call(
        body, (proj, proj, proj, proj, dmixed, dmixed, dmixed, w_s, bs_t, g_v, w_pool_g, pool_scale), comm,
        name="mixer_bwd",
        grid=(nsteps,),
        in_specs=[pl.BlockSpec((tt, A_WIDTH), lambda i: (i, 0)),
                  pl.BlockSpec((tt, A_WIDTH), lambda i: (i, 1)),
                  pl.BlockSpec((tt, B_WIDTH), lambda i: (i, 2)),
                  pl.BlockSpec((HALO, B_WIDTH), lambda i: (jnp.maximum(i * hb - 1, 0), 2)),
                  pl.BlockSpec((tt, A_WIDTH), lambda i: (i, 0)),
                  pl.BlockSpec((tt, B_WIDTH), lambda i: (i, 1)),
                  pl.BlockSpec((HALO, B_WIDTH), lambda i: (jnp.minimum((i + 1) * hb, last_halo), 1)),
                  full((N_HEADS, CHUNK, CHUNK)), full((CHUNK, N_HEADS)), full((1, A_WIDTH)),
                  full((N_DEV, 4, rb, GROUP)), full((1, B_WIDTH))],
        out_specs=[pl.BlockSpec((tt, 2 * A_WIDTH + B_WIDTH), lambda i: (i, 0)),
                   full((N_HEADS, CHUNK, CHUNK)), full((CHUNK, N_HEADS)), full((1, A_WIDTH)),
                   full((1, B_WIDTH)), full((N_DEV, 4, rb, GROUP))],
        out_shape=[jax.ShapeDtypeStruct((t, 2 * A_WIDTH + B_WIDTH), BF16),
                   jax.ShapeDtypeStruct((N_HEADS, CHUNK, CHUNK), F32),
                   jax.ShapeDtypeStruct((CHUNK, N_HEADS), F32),
                   jax.ShapeDtypeStruct((1, A_WIDTH), F32),
                   jax.ShapeDtypeStruct((1, B_WIDTH), F32),
                   jax.ShapeDtypeStruct((N_DEV, 4, rb, GROUP), F32)],
        sem=("arbitrary",))


def _adamw(w, g, m, v):
    m = ADAM_B1 * m + (1.0 - ADAM_B1) * g
    v = ADAM_B2 * v + (1.0 - ADAM_B2) * (g * g)
    m_hat = m / ADAM_C1
    v_hat = v / ADAM_C2
    delta = -ADAM_LR * (m_hat / (jnp.sqrt(v_hat) + ADAM_EPS) + ADAM_WD * w)
    return delta, m, v


PAIR_SUM_TILE_ELEMS = 1024 * 1024
ADAMW_TILE_ELEMS = 512 * 1024


def _row_tile(r, c, elems):
    t = r
    while t * c > elems and t % 32 == 0:
        t //= 2
    return t


def _pair_sum_call(name, pos, grad, got):
    _, r, c = grad.shape
    tr = _row_tile(r, c, PAIR_SUM_TILE_ELEMS)

    def chip_of(rel, pos_ref):
        px = jnp.where((rel == 0) | (rel == 2), 1 - pos_ref[0], pos_ref[0])
        py = jnp.where((rel == 1) | (rel == 2), 1 - pos_ref[1], pos_ref[1])
        return 2 * px + py

    def body(pos_ref, own_ref, got_ref, out_ref):
        out_ref[...] = (own_ref[...].astype(F32) + got_ref[...].astype(F32)).astype(BF16)

    return pl.pallas_call(
        body, name=name,
        grid_spec=pltpu.PrefetchScalarGridSpec(
            num_scalar_prefetch=1, grid=(3, r // tr),
            in_specs=[pl.BlockSpec((None, tr, c), lambda k, i, p: (2 * chip_of(k, p) + p[2], i, 0)),
                      pl.BlockSpec((None, tr, c), lambda k, i, p: (chip_of(k, p), i, 0))],
            out_specs=pl.BlockSpec((None, tr, c), lambda k, i, p: (k, i, 0))),
        out_shape=jax.ShapeDtypeStruct((3, r, c), BF16),
        compiler_params=_params("parallel", "parallel"),
    )(pos, grad, got)


def _final_call(name, pos, grad, got_pair, got_chips, w, m, v):
    _, r, c = grad.shape
    tr = _row_tile(r, c, ADAMW_TILE_ELEMS)

    def body(pos_ref, own_ref, pair_ref, chips_ref, w_ref, m_ref, v_ref, g_out, d_out, m_out, v_out):
        g = own_ref[...].astype(F32) + pair_ref[...].astype(F32)
        for j in range(3):
            g = g + chips_ref[j].astype(F32)
        delta, m_new, v_new = _adamw(w_ref[...], g, m_ref[...], v_ref[...])
        g_out[...] = g
        d_out[...] = delta
        m_out[...] = m_new
        v_out[...] = v_new

    row = pl.BlockSpec((tr, c), lambda i, p: (i, 0))
    return pl.pallas_call(
        body, name=name,
        grid_spec=pltpu.PrefetchScalarGridSpec(
            num_scalar_prefetch=1, grid=(r // tr,),
            in_specs=[pl.BlockSpec((None, tr, c), lambda i, p: (4 * p[0] + 2 * p[1] + p[2], i, 0)),
                      pl.BlockSpec((None, tr, c), lambda i, p: (2 * p[0] + p[1], i, 0)),
                      pl.BlockSpec((3, tr, c), lambda i, p: (0, i, 0)), row, row, row],
            out_specs=[row] * 4),
        out_shape=[jax.ShapeDtypeStruct((r, c), F32)] * 4,
        compiler_params=_params("parallel"),
    )(pos, grad, got_pair, got_chips, w, m, v)


def _small_final_call(name, parts, w, m, v):
    _, rows, c = parts.shape
    r = w.shape[0]

    def body(p_ref, w_ref, m_ref, v_ref, g_out, d_out, m_out, v_out):
        g = p_ref[0]
        for k in range(1, N_DEV):
            g = g + p_ref[k]
        delta, m_new, v_new = _adamw(w_ref[...], g[:r], m_ref[...], v_ref[...])
        g_out[...] = g
        d_out[...] = delta
        m_out[...] = m_new
        v_out[...] = v_new

    return pl.pallas_call(
        body, name=name,
        out_shape=[jax.ShapeDtypeStruct((rows, c), F32)] + [jax.ShapeDtypeStruct((r, c), F32)] * 3,
        compiler_params=pltpu.CompilerParams(vmem_limit_bytes=VMEM_LIMIT),
    )(parts, w, m, v)


_SMALL_EARLY = ("g_v", "w_s", "b_s", "pool_scale", "g_ffn", "g_final")
_BIG = ("w_in", "w_pool", "w_out", "w_up", "w_down")
_ORDER = ("g_mix", "w_in", "g_v", "w_s", "b_s", "w_pool", "pool_scale", "w_out", "g_ffn", "w_up", "w_down", "g_final")


def _pack(parts):
    return jnp.concatenate([p.reshape(-1, LANES) for p in parts], axis=0)


def _unpack(packed, like):
    out, row = [], 0
    for a in like:
        rows = a.size // LANES
        out.append(packed[row:row + rows].reshape(a.shape))
        row += rows
    return out


def kernel(x, g_mix, w_in, g_v, w_s, b_s, w_pool, pool_scale, w_out, g_ffn, w_up, w_down, g_final, loss_target, m_g_mix, m_w_in, m_g_v, m_w_s, m_b_s, m_w_pool, m_pool_scale, m_w_out, m_g_ffn, m_w_up, m_w_down, m_g_final, v_g_mix, v_w_in, v_g_v, v_w_s, v_b_s, v_w_pool, v_pool_scale, v_w_out, v_g_ffn, v_w_up, v_w_down, v_g_final):
    weights = dict(g_mix=g_mix, w_in=w_in, g_v=g_v, w_s=w_s, b_s=b_s, w_pool=w_pool, pool_scale=pool_scale,
                   w_out=w_out, g_ffn=g_ffn, w_up=w_up, w_down=w_down, g_final=g_final)
    mom = dict(g_mix=m_g_mix, w_in=m_w_in, g_v=m_g_v, w_s=m_w_s, b_s=m_b_s, w_pool=m_w_pool,
               pool_scale=m_pool_scale, w_out=m_w_out, g_ffn=m_g_ffn, w_up=m_w_up, w_down=m_w_down,
               g_final=m_g_final)
    var = dict(g_mix=v_g_mix, w_in=v_w_in, g_v=v_g_v, w_s=v_w_s, b_s=v_b_s, w_pool=v_w_pool,
               pool_scale=v_pool_scale, w_out=v_w_out, g_ffn=v_g_ffn, w_up=v_w_up, w_down=v_w_down,
               g_final=v_g_final)

    t, d = x.shape[1], x.shape[2]
    xs = x.reshape(t, d)
    target = loss_target.reshape(t, d)

    shard2d = dict(w_in=w_in.reshape(d, -1), w_pool=w_pool.reshape(-1, GROUP), w_out=w_out.reshape(-1, d),
                   w_up=w_up.reshape(d, -1), w_down=w_down.reshape(-1, d))
    sb = {k: shard2d[k].astype(BF16) for k in _BIG}
    rows = {k: sb[k].shape[0] for k in _BIG}

    def gathered_shape(k):
        return jax.ShapeDtypeStruct((N_DEV,) + sb[k].shape, BF16)

    def landing(n, like):
        return jax.ShapeDtypeStruct((n,) + like.shape[1:], like.dtype)

    def from_everyone(block):
        return jax.ShapeDtypeStruct((N_DEV,) + block.shape, block.dtype)

    def cuts(r, fractions):
        return [0] + [int(r * f) // 16 * 16 for f in fractions] + [r]

    g_mix2, g_ffn2, g_final2 = g_mix.reshape(1, d), g_ffn.reshape(1, d), g_final.reshape(1, d)
    g_v2, ps2 = g_v.reshape(1, A_WIDTH), pool_scale.reshape(1, B_WIDTH)
    w_s3 = w_s.reshape(N_HEADS, CHUNK, CHUNK)
    bs_t = b_s.reshape(N_HEADS, CHUNK).T
    xi, yi, ci = _position()
    pos = jnp.stack([xi, yi, ci]).astype(jnp.int32)

    order = (4 * xi + 2 * yi + ci) ^ jnp.array(ARRIVAL_ORDER, jnp.int32)
    u = cuts(rows["w_up"], (0.25, 0.55))
    ahead = cuts(rows["w_down"], (0.6,))[1]
    proj, h1, w_in_g, w_out_g, w_pool_g = _norm_matmul_stream_call(
        "proj_fwd", xs, g_mix2, sb["w_in"], order, None, 0, 0, _Comm(
            [sb["w_out"], sb["w_pool"]], [], [gathered_shape("w_out"), gathered_shape("w_pool")],
            lambda s, l: [_gather_to_neighbours(s[0], l[0], 0, rows["w_out"]) + _everyone(s[1], l[1])]),
        lambda a: (a,), (F32,))
    w_pool_g = w_pool_g.reshape(N_DEV, 4, GROUP // N_DEV, GROUP)
    mixed, w_out_g, w_up_g = _mixer_fwd_call(proj, w_s3, bs_t, g_v2, w_pool_g, ps2, _Comm(
        [sb["w_up"]], [w_out_g], [gathered_shape("w_up")],
        lambda s, l: [_gather_relay(l[0], 0, rows["w_out"]) + _gather_to_neighbours(s[0], l[1], u[0], u[1]),
                      _gather_diagonal_pass_on(l[0], 0, rows["w_out"])]))
    w_out_f = w_out_g.reshape(-1, d)
    x2, w_up_g = _out_proj_call(mixed, w_out_f, xs, _Comm(
        [sb["w_up"]], [w_up_g], [],
        lambda s, l: [_gather_relay(l[0], u[0], u[1]) + _gather_to_neighbours(s[0], l[0], u[1], u[2]),
                      _gather_diagonal_pass_on(l[0], u[0], u[1])]))

    def relu2_and_slope(a):
        r = jnp.maximum(a, 0.0)
        return r * r, 2.0 * r

    act, dact_da, h2, w_up_g, w_down_g = _norm_matmul_stream_call(
        "up_fwd", x2, g_ffn2, sb["w_up"], order, w_up_g, u[1], u[2], _Comm(
            [sb["w_down"]], [], [gathered_shape("w_down")],
            lambda s, l: [_gather_to_neighbours(s[0], l[0], 0, ahead)]),
        relu2_and_slope, (BF16, BF16))
    y, w_down_g = _down_call(act, sb["w_down"], order, w_down_g, ahead)
    w_down_f = w_down_g.reshape(-1, d)
    loss_part, dx3, dx3b, dg_final = _loss_call(y, x2, target, g_final2)

    def pair_sum(k, grad, got):
        return _pair_sum_call(k + "_pair_sum", pos, grad, got)

    def finish(k, grad, got_pair, got_chips):
        s = shard2d[k]
        outs = _final_call(k + "_adamw", pos, grad, got_pair, got_chips, s, mom[k].reshape(s.shape),
                           var[k].reshape(s.shape))
        return [o.reshape(weights[k].shape) for o in outs]

    result = {}
    (gw_down,) = _wgrad_call("w_down_grad", act, dx3b, None, None, t1=1024, t2=2048)
    gw_down = gw_down.reshape(N_DEV, -1, d)
    da, pair_down = _dact_call(dx3b, w_down_f, dact_da, _Comm(
        [gw_down], [], [landing(4, gw_down)], lambda s, l: [_pair_exchange(s[0], l[0])]))
    sums_down = pair_sum("w_down", gw_down, pair_down)
    dn = cuts(rows["w_down"], (0.75,))
    gw_up, got = _wgrad_call("w_up_grad", h2, da, N_DEV, w_up_g.shape[2], _Comm(
        [sums_down], [], [landing(3, sums_down)],
        lambda s, l: [_chip_exchange(s[0], l[0], dn[0], dn[1])]), t1=2048)
    dh2, got, pair_up = _dgrad_blocked_call("dh2_bwd", da, w_up_g, _Comm(
        [sums_down, gw_up], [got], [landing(4, gw_up)],
        lambda s, l: [_chip_exchange(s[0], l[0], dn[1], dn[2]) + _pair_exchange(s[1], l[1])]), merge=2)
    result["w_down"] = finish("w_down", gw_down, pair_down, got)
    sums_up = pair_sum("w_up", gw_up, pair_up)
    v = cuts(rows["w_up"], (0.26, 0.47, 0.69))
    dx2, dx2b, dg_ffn, got_up = _norm_bwd_call("ffn_norm_bwd", dh2, x2, dx3, g_ffn2, True, _Comm(
        [sums_up], [], [landing(3, sums_up)], lambda s, l: [_chip_exchange(s[0], l[0], v[0], v[1])]))
    dmixed, got_up = _dmixed_call(dx2b, w_out_f, _Comm(
        [sums_up], [got_up], [], lambda s, l: [_chip_exchange(s[0], l[0], v[1], v[2])]))
    gw_out, got_up = _wgrad_call("w_out_grad", mixed, dx2b, None, None, _Comm(
        [sums_up], [got_up], [], lambda s, l: [_chip_exchange(s[0], l[0], v[2], v[3])]), t1=2048, t2=1024)
    gw_out = gw_out.reshape(N_DEV, -1, d)
    dproj, dw_s, dbs_t, dg_v, dps, dw_pool, got_up, pair_out = _mixer_bwd_call(
        proj, dmixed, w_s3, bs_t, g_v2, w_pool_g, ps2, _Comm(
            [sums_up, gw_out], [got_up], [landing(4, gw_out)],
            lambda s, l: [_chip_exchange(s[0], l[0], v[3], v[4]) + _pair_exchange(s[1], l[1])]))
    result["w_up"] = finish("w_up", gw_up, pair_up, got_up)
    sums_out = pair_sum("w_out", gw_out, pair_out)
    gw_pool = dw_pool.astype(BF16).reshape(N_DEV, -1, GROUP)
    early = dict(g_v=dg_v, w_s=dw_s, b_s=dbs_t.T, pool_scale=dps, g_ffn=dg_ffn, g_final=dg_final)
    packed = _pack([early[k] for k in _SMALL_EARLY] + [loss_part])
    early_rows = packed.shape[0]
    gw_in, got, pair_pool, parts_early = _wgrad_call("w_in_grad", h1, dproj, N_DEV, w_in_g.shape[2], _Comm(
        [sums_out, gw_pool, packed], [], [landing(3, sums_out), landing(4, gw_pool), from_everyone(packed)],
        lambda s, l: [_chip_exchange(s[0], l[0], 0, rows["w_out"]) + _pair_exchange(s[1], l[1])
                      + _gather_first(s[2], l[2], 0, early_rows)]), t1=2048, merge=MERGE_W_IN)
    result["w_out"] = finish("w_out", gw_out, pair_out, got)
    sums_pool = pair_sum("w_pool", gw_pool, pair_pool)
    (pair_in,) = _comm_call("pair_exchange_w_in", _Comm(
        [gw_in], [], [landing(4, gw_in)], lambda s, l: [_pair_exchange(s[0], l[0])]))
    sums_in = pair_sum("w_in", gw_in, pair_in)
    dh1, parts_early, got, got_pool = _dgrad_blocked_call("dh1_bwd", dproj, w_in_g, _Comm(
        [sums_in, sums_pool], [parts_early], [landing(3, sums_in), landing(3, sums_pool)],
        lambda s, l: [_chip_exchange(s[0], l[1], 0, rows["w_in"]) + _chip_exchange(s[1], l[2], 0, rows["w_pool"])
                      + _gather_pass_on(l[0], 0, early_rows)]),
        merge=MERGE_W_IN)
    result["w_in"] = finish("w_in", gw_in, pair_in, got)
    result["w_pool"] = finish("w_pool", gw_pool, pair_pool, got_pool)
    grad_x, dg_mix = _norm_bwd_call("mix_norm_bwd", dh1, xs, dx2, g_mix2, False)
    packed = _pack([dg_mix])
    (parts_late,) = _comm_call("gather_g_mix_grad", _Comm(
        [packed], [], [from_everyone(packed)], lambda s, l: [_everyone(s[0], l[0])]))

    for names, parts, tag in ((_SMALL_EARLY, parts_early, "small_adamw"), (("g_mix",), parts_late, "g_mix_adamw")):
        outs = _small_final_call(tag, parts, _pack([weights[k] for k in names]), _pack([mom[k] for k in names]),
                                 _pack([var[k] for k in names]))
        if tag == "small_adamw":
            loss = outs[0][-1, 0]
        like = [weights[k] for k in names]
        unpacked = [_unpack(o, like) for o in outs]
        for idx, k in enumerate(names):
            result[k] = [unpacked[q][idx] for q in range(4)]

    grads = [result[k][0] for k in _ORDER]
    deltas = [result[k][1] for k in _ORDER]
    new_m = [result[k][2] for k in _ORDER]
    new_v = [result[k][3] for k in _ORDER]
    return (loss, grad_x.reshape(x.shape), *grads, *deltas, *new_m, *new_v)
```

```python
import functools
import math

import jax
import jax.numpy as jnp
from jax import lax
from jax.experimental import pallas as pl
from jax.experimental.pallas import tpu as pltpu

F32 = jnp.float32
BF16 = jnp.bfloat16
MESH = pl.DeviceIdType.MESH

N_DEV = 8
EPS = 1e-6
CHUNK = 128
N_HEADS = 8
A_WIDTH = 1024
B_WIDTH = 1024
POOL_WINDOWS = (2, 4, 8, 16)
GROUP = 256
HALO = 16
LANES = 128

ADAM_LR = 0.001
ADAM_B1 = 0.9
ADAM_B2 = 0.999
ADAM_EPS = 1e-08
ADAM_WD = 0.01
ADAM_STEP = 10
ADAM_C1 = 1.0 - ADAM_B1 ** ADAM_STEP
ADAM_C2 = 1.0 - ADAM_B2 ** ADAM_STEP

VMEM_LIMIT = 56 * 1024 * 1024
MERGE_W_IN = 2

_GELU_C = math.sqrt(2.0 / math.pi)


def _params(*sem):
    return pltpu.CompilerParams(dimension_semantics=sem, vmem_limit_bytes=VMEM_LIMIT)


def _gelu(x):
    return 0.5 * x * (1.0 + jnp.tanh(_GELU_C * (x + 0.044715 * x * x * x)))


def _gelu_and_grad(x):
    t = jnp.tanh(_GELU_C * (x + 0.044715 * x * x * x))
    g = 0.5 * x * (1.0 + t)
    dg = 0.5 * (1.0 + t) + 0.5 * x * (1.0 - t * t) * (_GELU_C * (1.0 + 3.0 * 0.044715 * x * x))
    return g, dg


def _dot_nn(a, b):
    return lax.dot_general(a, b, (((1,), (0,)), ((), ())), preferred_element_type=F32)


def _dot_nt(a, b):
    return lax.dot_general(a, b, (((1,), (1,)), ((), ())), preferred_element_type=F32)


def _dot_tn(a, b):
    return lax.dot_general(a, b, (((0,), (0,)), ((), ())), preferred_element_type=F32)


def _rms_rows(x):
    r = lax.rsqrt(jnp.mean(x * x, axis=-1, keepdims=True) + EPS)
    return x * r, r


def _rms_bwd_rows(dn, n, r):
    return r * (dn - n * jnp.mean(dn * n, axis=-1, keepdims=True))


def _tile(n, want):
    t = min(n, want)
    assert n % t == 0, (n, want)
    return t


_ANY = pl.BlockSpec(memory_space=pl.ANY)

SIBLING = 1
CHIPS = (4, 2, 6)


def _position():
    return lax.axis_index("x"), lax.axis_index("y"), lax.axis_index("c")


def _me():
    x, y, c = _position()
    return 4 * x + 2 * y + c


def _peer(rel):
    x, y, c = _position()
    return (x ^ ((rel >> 2) & 1), y ^ ((rel >> 1) & 1), c ^ (rel & 1))


class _Comm:
    def __init__(self, srcs, lands, new, plan):
        self.srcs, self.lands, self.new, self.plan = list(srcs), list(lands), list(new), plan


def _make_copies(phases, send_sems, recv_sems, local_sems):
    out, nr, nl = [], 0, 0
    for phase in phases:
        cps = []
        for item in phase:
            if item[0] == "local":
                cps.append(pltpu.make_async_copy(item[1], item[2], local_sems.at[nl]))
                nl += 1
            else:
                cps.append(pltpu.make_async_remote_copy(
                    src_ref=item[1], dst_ref=item[2], send_sem=send_sems.at[nr], recv_sem=recv_sems.at[nr],
                    device_id=_peer(item[3]), device_id_type=MESH))
                nr += 1
        out.append(cps)
    return out


def _count_copies(comm):
    phases = comm.plan([_FakeRef() for _ in comm.srcs], [_FakeRef() for _ in range(len(comm.lands) + len(comm.new))])
    items = [it for ph in phases for it in ph]
    return sum(it[0] == "remote" for it in items), sum(it[0] == "local" for it in items)


class _FakeRef:
    def __getitem__(self, idx):
        return self

    @property
    def at(self):
        return self


def _carrier_call(body, args, comm, *, name, grid, in_specs, out_specs, out_shape, scratch_shapes=(), sem):
    if not isinstance(out_shape, (list, tuple)):
        out_specs, out_shape = [out_specs], [out_shape]
    out_specs, out_shape, scratch_shapes = list(out_specs), list(out_shape), list(scratch_shapes)
    if comm is None:
        res = pl.pallas_call(body, name=name, grid=grid, in_specs=list(in_specs), out_specs=out_specs,
                             out_shape=out_shape, scratch_shapes=scratch_shapes, compiler_params=_params(*sem))(*args)
        return list(res)
    n_in, n_out, n_scr = len(args), len(out_shape), len(scratch_shapes)
    ns, nl, nn = len(comm.srcs), len(comm.lands), len(comm.new)
    n_remote, n_local = _count_copies(comm)
    steps = math.prod(grid)

    def wrapped(*refs):
        ins, srcs = refs[:n_in], refs[n_in:n_in + ns]
        o = n_in + ns + nl
        outs, lands = refs[o:o + n_out], refs[o + n_out:o + n_out + nl + nn]
        scr = refs[o + n_out + nl + nn:]
        phases = _make_copies(comm.plan(srcs, lands), *scr[n_scr:])
        assert len(phases) == 1 or (len(phases) == 2 and steps >= 3)
        step = functools.reduce(lambda acc, a: acc * grid[a] + pl.program_id(a), range(len(grid)), 0)

        @pl.when(step == 0)
        def _():
            for cp in phases[0]:
                cp.start()

        if len(phases) == 2:
            @pl.when(step == steps * 7 // 8)
            def _():
                for cp in phases[0]:
                    cp.wait()
                for cp in phases[1]:
                    cp.start()

        body(*ins, *outs, *scr[:n_scr])

        @pl.when(step == steps - 1)
        def _():
            for cp in phases[-1]:
                cp.wait()

    land_shapes = [jax.ShapeDtypeStruct(a.shape, a.dtype) for a in comm.lands] + comm.new
    sems = [pltpu.SemaphoreType.DMA((max(n_remote, 1),)), pltpu.SemaphoreType.DMA((max(n_remote, 1),)),
            pltpu.SemaphoreType.DMA((max(n_local, 1),))]
    res = pl.pallas_call(
        wrapped, name=name, grid=grid,
        in_specs=list(in_specs) + [_ANY] * (ns + nl), out_specs=out_specs + [_ANY] * (nl + nn),
        out_shape=out_shape + land_shapes, scratch_shapes=scratch_shapes + sems,
        input_output_aliases={n_in + ns + k: n_out + k for k in range(nl)},
        compiler_params=_params(*sem))(*args, *comm.srcs, *comm.lands)
    return list(res)


def _comm_call(name, comm):
    ns, nl, nn = len(comm.srcs), len(comm.lands), len(comm.new)
    n_remote, n_local = _count_copies(comm)

    def body(*refs):
        srcs, lands, sems = refs[:ns], refs[ns + nl:ns + nl + nl + nn], refs[ns + nl + nl + nn:]
        for copies in _make_copies(comm.plan(srcs, lands), *sems):
            for cp in copies:
                cp.start()
            for cp in copies:
                cp.wait()

    land_shapes = [jax.ShapeDtypeStruct(a.shape, a.dtype) for a in comm.lands] + comm.new
    res = pl.pallas_call(
        body, name=name,
        in_specs=[_ANY] * (ns + nl), out_specs=[_ANY] * (nl + nn), out_shape=land_shapes,
        scratch_shapes=[pltpu.SemaphoreType.DMA((max(n_remote, 1),)), pltpu.SemaphoreType.DMA((max(n_remote, 1),)),
                        pltpu.SemaphoreType.DMA((max(n_local, 1),))],
        input_output_aliases={ns + k: k for k in range(nl)},
    )(*comm.srcs, *comm.lands)
    return list(res)


def _rows(ref, block, r0, r1):
    return ref.at[block, pl.ds(r0, r1 - r0)]


def _gather_first(shard, land, r0, r1):
    src = shard.at[pl.ds(r0, r1 - r0)]
    dst = _rows(land, _me(), r0, r1)
    return [("local", src, dst)] + [("remote", src, dst, rel) for rel in (SIBLING,) + CHIPS]


def _gather_pass_on(land, r0, r1):
    return [("remote", _rows(land, _me() ^ rel, r0, r1), _rows(land, _me() ^ rel, r0, r1), SIBLING) for rel in CHIPS]


def _split_rows(r0, r1):
    m = (r0 + r1) // 2 // 16 * 16
    return (r0, m), (m, r1)


def _gather_to_neighbours(shard, land, r0, r1):
    src = shard.at[pl.ds(r0, r1 - r0)]
    dst = _rows(land, _me(), r0, r1)
    return [("local", src, dst)] + [("remote", src, dst, rel) for rel in (SIBLING, 4, 2)]


def _gather_relay(land, r0, r1):
    lo, hi = _split_rows(r0, r1)
    x_block, y_block = _me() ^ 4, _me() ^ 2
    return [("remote", _rows(land, x_block, *lo), _rows(land, x_block, *lo), 2),
            ("remote", _rows(land, y_block, *hi), _rows(land, y_block, *hi), 4),
            ("remote", _rows(land, x_block, r0, r1), _rows(land, x_block, r0, r1), SIBLING),
            ("remote", _rows(land, y_block, r0, r1), _rows(land, y_block, r0, r1), SIBLING)]


def _gather_diagonal_pass_on(land, r0, r1):
    rows = _rows(land, _me() ^ 6, r0, r1)
    return [("remote", rows, rows, SIBLING)]


def _pair_exchange(grad, land):
    _, _, c = _position()
    return [("remote", grad.at[2 * chip + (1 - c)], land.at[chip], SIBLING) for chip in range(4)]


def _chip_exchange(sums, land, r0, r1):
    return [("remote", _rows(sums, j, r0, r1), _rows(land, j, r0, r1), rel) for j, rel in enumerate(CHIPS)]


def _everyone(packed, land):
    dst = land.at[_me()]
    return [("local", packed, dst)] + [("remote", packed, dst, rel) for rel in range(1, N_DEV)]


def _pool_counts(row0, rows, win):
    pos = row0 + lax.broadcasted_iota(jnp.int32, (rows, 1), 0)
    return jnp.minimum(pos + 1, win).astype(F32)


def _window_sum_back(ext, win):
    s = ext
    k = 1
    while k < win:
        s = s + pltpu.roll(s, k, 0)
        k *= 2
    return s


def _window_sum_fwd(ext, win):
    n = ext.shape[0]
    s = ext
    k = 1
    while k < win:
        s = s + pltpu.roll(s, n - k, 0)
        k *= 2
    return s


def _mixer_fwd_call(proj, w_s, bs_t, g_v, w_pool_g, pool_scale, comm=None):
    t = proj.shape[0]
    tt = _tile(t, 512)
    nchunk = tt // CHUNK
    hb = tt // HALO

    def body(pu_ref, pv_ref, z_ref, zp_ref, ws_ref, bs_ref, gv_ref, wp_ref, ps_ref, out_ref):
        i = pl.program_id(0)
        tril = (lax.broadcasted_iota(jnp.int32, (CHUNK, CHUNK), 0)
                >= lax.broadcasted_iota(jnp.int32, (CHUNK, CHUNK), 1))
        for h in range(N_HEADS):
            cols = slice(h * CHUNK, (h + 1) * CHUNK)
            vhat, _ = _rms_rows(_gelu(pv_ref[:, cols]))
            vn = (vhat * gv_ref[:, cols]).astype(BF16)
            u = _gelu(pu_ref[:, cols])
            w = jnp.where(tril, ws_ref[h], 0.0).astype(BF16)
            bcol = bs_ref[:, h:h + 1]
            for c in range(nchunk):
                rows = slice(c * CHUNK, (c + 1) * CHUNK)
                mixed = _dot_nn(w, vn[rows]) + bcol
                out_ref[rows, cols] = (u[rows] * mixed).astype(BF16)

        zprev = jnp.where(i > 0, zp_ref[...], 0.0)
        ext = jnp.concatenate([zprev, z_ref[...]], axis=0)
        for g, win in enumerate(POOL_WINDOWS):
            cols = slice(g * GROUP, (g + 1) * GROUP)
            zg = ext[:, cols]
            s = _window_sum_back(zg, win)
            pooled = s[HALO:] / _pool_counts(i * tt, tt, win) - zg[HALO:]
            wp = wp_ref[:, g].reshape(GROUP, GROUP)
            y = _dot_nn(pooled.astype(BF16), wp)
            out_ref[:, A_WIDTH + g * GROUP:A_WIDTH + (g + 1) * GROUP] = (y * ps_ref[:, cols]).astype(BF16)

    return _carrier_call(
        body, (proj, proj, proj, proj, w_s, bs_t, g_v, w_pool_g, pool_scale), comm, name="mixer_fwd",
        grid=(t // tt,),
        in_specs=[pl.BlockSpec((tt, A_WIDTH), lambda i: (i, 0)),
                  pl.BlockSpec((tt, A_WIDTH), lambda i: (i, 1)),
                  pl.BlockSpec((tt, B_WIDTH), lambda i: (i, 2)),
                  pl.BlockSpec((HALO, B_WIDTH), lambda i: (jnp.maximum(i * hb - 1, 0), 2)),
                  pl.BlockSpec((N_HEADS, CHUNK, CHUNK), lambda i: (0, 0, 0)),
                  pl.BlockSpec((CHUNK, N_HEADS), lambda i: (0, 0)),
                  pl.BlockSpec((1, A_WIDTH), lambda i: (0, 0)),
                  pl.BlockSpec((N_DEV, 4, GROUP // N_DEV, GROUP), lambda i: (0, 0, 0, 0)),
                  pl.BlockSpec((1, B_WIDTH), lambda i: (0, 0))],
        out_specs=pl.BlockSpec((tt, A_WIDTH + B_WIDTH), lambda i: (i, 0)),
        out_shape=jax.ShapeDtypeStruct((t, A_WIDTH + B_WIDTH), BF16),
        sem=("parallel",))


def _out_proj_call(mixed, w_out, x, comm=None):
    t, d = x.shape
    k = mixed.shape[1]
    tm = _tile(t, 1024)
    tn = _tile(d, 1024)

    def body(a_ref, w_ref, x_ref, o_ref):
        o_ref[...] = x_ref[...] + _dot_nn(a_ref[...], w_ref[...])

    return _carrier_call(
        body, (mixed, w_out, x), comm, name="out_proj_fwd",
        grid=(t // tm, d // tn),
        in_specs=[pl.BlockSpec((tm, k), lambda i, j: (i, 0)),
                  pl.BlockSpec((k, tn), lambda i, j: (0, j)),
                  pl.BlockSpec((tm, tn), lambda i, j: (i, j))],
        out_specs=pl.BlockSpec((tm, tn), lambda i, j: (i, j)),
        out_shape=jax.ShapeDtypeStruct((t, d), F32),
        sem=("parallel", "parallel"))


ARRIVAL_ORDER = (0, 1, 4, 5, 2, 3, 6, 7)
CARRIED_AFTER = 3


class _StreamedGather:
    def __init__(self, shard_ref, land_ref, wbuf, pre0, r0, send_sems, recv_sems, local_sem, fetch_sems):
        self.shard, self.land, self.wbuf, self.fetch_sems = shard_ref, land_ref, wbuf, fetch_sems
        end = shard_ref.shape[0]
        me = _me()
        self.me = me

        def remote(k, src, dst, rel):
            return pltpu.make_async_remote_copy(src_ref=src, dst_ref=dst, send_sem=send_sems.at[k],
                                                recv_sem=recv_sems.at[k], device_id=_peer(rel), device_id_type=MESH)

        def same_rows(k, block, a, b, rel):
            ref = land_ref.at[block, pl.ds(a, b - a)]
            return remote(k, ref, ref, rel)

        src = shard_ref.at[pl.ds(r0, end - r0)]
        dst = land_ref.at[me, pl.ds(r0, end - r0)]
        self.mine = pltpu.make_async_copy(src, dst, local_sem)
        self.first = [remote(k, src, dst, rel) for k, rel in enumerate((SIBLING, 4, 2))]
        lo, hi = _split_rows(r0, end)
        self.relay = [same_rows(3, me ^ 4, *lo, 2), same_rows(4, me ^ 2, *hi, 4)]
        self.passed = [same_rows(5, me ^ 4, r0, end, SIBLING), same_rows(6, me ^ 2, r0, end, SIBLING),
                       same_rows(7, me ^ 6, pre0, end, SIBLING)]
        self.early_relay, self.early_passed = [], []
        if r0 > pre0:
            lo, hi = _split_rows(pre0, r0)
            self.early_relay = [same_rows(8, me ^ 4, *lo, 2), same_rows(9, me ^ 2, *hi, 4)]
            self.early_passed = [same_rows(10, me ^ 4, pre0, r0, SIBLING), same_rows(11, me ^ 2, pre0, r0, SIBLING)]

    def _fetch(self, q):
        src = self.shard if q == 0 else self.land.at[self.me ^ ARRIVAL_ORDER[q]]
        return pltpu.make_async_copy(src, self.wbuf.at[q % 2], self.fetch_sems.at[q % 2])

    def start(self):
        self.mine.start()
        for cp in self.first + self.early_relay + self.early_passed:
            cp.start()
        self._fetch(0).start()

    def arrive(self, q):
        if q == 1:
            self.first[0].wait_recv()
        elif q in (2, 4):
            j = q // 2 - 1
            self.first[1 + j].wait_recv()
            self.relay[j].start()
            self.passed[j].start()
        elif q in (3, 5):
            j = q // 2 - 1
            self.passed[j].wait_recv()
            if self.early_passed:
                self.early_passed[j].wait_recv()
        elif q == 6:
            for cp in self.relay + self.early_relay:
                cp.wait_recv()
            self.passed[2].start()
        else:
            self.passed[2].wait_recv()
        self._fetch(q).start()

    def wait_fetch(self, slot):
        pltpu.make_async_copy(self.shard, self.wbuf.at[slot], self.fetch_sems.at[slot]).wait()

    def finish(self):
        for cp in self.first + self.relay + self.passed + self.early_relay + self.early_passed:
            cp.wait_send()
        self.mine.wait()


_STREAM_SEMS = [pltpu.SemaphoreType.DMA((12,)), pltpu.SemaphoreType.DMA((12,)), pltpu.SemaphoreType.DMA,
                pltpu.SemaphoreType.DMA((2,))]


def _stream_steps(gather, p, i, ni):
    @pl.when((p == 0) & (i == 0))
    def _():
        gather.start()

    @pl.when(i == 0)
    def _():
        gather.wait_fetch(p % 2)

    @pl.when(i == ni - 1)
    def _():
        for q in range(1, N_DEV):
            @pl.when(p == q - 1)
            def _():
                gather.arrive(q)


def _norm_matmul_stream_call(name, x, g, shard, order, land, pre0, r0, comm, epilogue, out_dtypes):
    t, d = x.shape
    cb = shard.shape[1]
    tm = _tile(t, 1024)
    ni = t // tm
    n_sems = len(_STREAM_SEMS)
    assert not comm.lands
    ns, nn, no = len(comm.srcs), len(comm.new), len(out_dtypes)
    n_remote, n_local = _count_copies(comm)
    has_land = land is not None

    def body(order_ref, x_ref, g_ref, shard_ref, *refs):
        refs = refs[has_land:]
        srcs, out_refs, (h_ref, land_ref) = refs[:ns], refs[ns:ns + no], refs[ns + no:ns + no + 2]
        new = refs[ns + no + 2:ns + no + 2 + nn]
        wbuf, sems = refs[ns + no + 2 + nn], refs[ns + no + 3 + nn:]
        p, i = pl.program_id(0), pl.program_id(1)
        gather = _StreamedGather(shard_ref, land_ref, wbuf, pre0, r0, *sems[:n_sems])
        (carried,) = _make_copies(comm.plan(srcs, new), *sems[n_sems:])
        rows = pl.ds(pl.multiple_of(i * tm, tm), tm)
        _stream_steps(gather, p, i, ni)

        @pl.when((p == CARRIED_AFTER) & (i == ni - 1))
        def _():
            for cp in carried:
                cp.start()

        @pl.when(p == 0)
        def _():
            n, _ = _rms_rows(x_ref[...])
            h_ref[rows, :] = (n * g_ref[...]).astype(BF16)

        tails = epilogue(_dot_nn(h_ref[rows, :], wbuf[p % 2]))
        for out_ref, tail, dt in zip(out_refs, tails, out_dtypes):
            out_ref[...] = tail.astype(dt)

        @pl.when((p == N_DEV - 1) & (i == ni - 1))
        def _():
            gather.finish()
            for cp in carried:
                cp.wait()

    carried_sems = [pltpu.SemaphoreType.DMA((max(n_remote, 1),)), pltpu.SemaphoreType.DMA((max(n_remote, 1),)),
                    pltpu.SemaphoreType.DMA((max(n_local, 1),))]
    return pl.pallas_call(
        body, name=name,
        grid_spec=pltpu.PrefetchScalarGridSpec(
            num_scalar_prefetch=1, grid=(N_DEV, ni),
            in_specs=[pl.BlockSpec((tm, d), lambda p, i, o: (jnp.where(p == 0, i, ni - 1), 0)),
                      pl.BlockSpec((1, d), lambda p, i, o: (0, 0)),
                      _ANY] + [_ANY] * (has_land + ns),
            out_specs=[pl.BlockSpec((tm, cb), lambda p, i, o: (i, o[p]))] * no
                      + [pl.BlockSpec(memory_space=pltpu.VMEM), _ANY] + [_ANY] * nn,
            scratch_shapes=[pltpu.VMEM((2, d, cb), BF16)] + _STREAM_SEMS + carried_sems),
        out_shape=[jax.ShapeDtypeStruct((t, N_DEV * cb), dt) for dt in out_dtypes]
                  + [jax.ShapeDtypeStruct((t, d), BF16), jax.ShapeDtypeStruct((N_DEV, d, cb), BF16)] + comm.new,
        input_output_aliases={4: no + 1} if has_land else {},
        compiler_params=_params("arbitrary", "arbitrary"),
    )(order, x, g, shard, *([land] if has_land else []), *comm.srcs)


def _down_call(act, shard, order, land, r0):
    t = act.shape[0]
    rb, d = shard.shape
    tm = _tile(t, 1024)
    ni = t // tm

    def body(order_ref, a_ref, shard_ref, land_in_ref, y_ref, land_ref, wbuf, *sems):
        p, i = pl.program_id(0), pl.program_id(1)
        gather = _StreamedGather(shard_ref, land_ref, wbuf, 0, r0, *sems)
        rows = pl.ds(pl.multiple_of(i * tm, tm), tm)
        _stream_steps(gather, p, i, ni)
        @pl.when(p == 0)
        def _():
            y_ref[rows, :] = _dot_nn(a_ref[...], wbuf[0])

        @pl.when(p > 0)
        def _():
            y_ref[rows, :] += _dot_nn(a_ref[...], wbuf[p % 2])

        @pl.when((p == N_DEV - 1) & (i == ni - 1))
        def _():
            gather.finish()

    return pl.pallas_call(
        body, name="down_fwd",
        grid_spec=pltpu.PrefetchScalarGridSpec(
            num_scalar_prefetch=1, grid=(N_DEV, ni),
            in_specs=[pl.BlockSpec((tm, rb), lambda p, i, o: (i, o[p])), _ANY, _ANY],
            out_specs=[pl.BlockSpec(memory_space=pltpu.VMEM), _ANY],
            scratch_shapes=[pltpu.VMEM((2, rb, d), BF16)] + _STREAM_SEMS),
        out_shape=[jax.ShapeDtypeStruct((t, d), F32), jax.ShapeDtypeStruct((N_DEV, rb, d), BF16)],
        input_output_aliases={3: 1},
        compiler_params=_params("arbitrary", "arbitrary"),
    )(order, act, shard, land)


def _loss_call(y, x2, target, g_final):
    t, d = y.shape
    tr = _tile(t, 512)

    def body(y_ref, x_ref, tg_ref, g_ref, loss_ref, dx_ref, dxb_ref, dg_ref):
        @pl.when(pl.program_id(0) == 0)
        def _():
            loss_ref[...] = jnp.zeros_like(loss_ref)
            dg_ref[...] = jnp.zeros_like(dg_ref)

        n, r = _rms_rows(x_ref[...] + y_ref[...])
        err = n * g_ref[...] - tg_ref[...]
        loss_ref[...] += 0.5 * jnp.sum(jnp.mean(err * err, axis=-1, keepdims=True))
        dy = err * (1.0 / d)
        dg_ref[...] += jnp.sum(dy * n, axis=0, keepdims=True)
        dx = _rms_bwd_rows(dy * g_ref[...], n, r)
        dx_ref[...] = dx
        dxb_ref[...] = dx.astype(BF16)

    return pl.pallas_call(
        body, name="loss_head",
        grid=(t // tr,),
        in_specs=[pl.BlockSpec((tr, d), lambda i: (i, 0)),
                  pl.BlockSpec((tr, d), lambda i: (i, 0)),
                  pl.BlockSpec((tr, d), lambda i: (i, 0)),
                  pl.BlockSpec((1, d), lambda i: (0, 0))],
        out_specs=[pl.BlockSpec((8, LANES), lambda i: (0, 0)),
                   pl.BlockSpec((tr, d), lambda i: (i, 0)),
                   pl.BlockSpec((tr, d), lambda i: (i, 0)),
                   pl.BlockSpec((1, d), lambda i: (0, 0))],
        out_shape=[jax.ShapeDtypeStruct((8, LANES), F32), jax.ShapeDtypeStruct((t, d), F32),
                   jax.ShapeDtypeStruct((t, d), BF16), jax.ShapeDtypeStruct((1, d), F32)],
        compiler_params=_params("arbitrary"),
    )(y, x2, target, g_final)


def _norm_bwd_call(name, dh, x, dres, g, want_bf16, comm=None):
    t, d = x.shape
    tr = _tile(t, 256)

    def body(dh_ref, x_ref, dres_ref, g_ref, dx_ref, *rest):
        dg_ref = rest[-1]

        @pl.when(pl.program_id(0) == 0)
        def _():
            dg_ref[...] = jnp.zeros_like(dg_ref)

        n, r = _rms_rows(x_ref[...])
        dh = dh_ref[...]
        dg_ref[...] += jnp.sum(dh * n, axis=0, keepdims=True)
        dx = dres_ref[...] + _rms_bwd_rows(dh * g_ref[...], n, r)
        dx_ref[...] = dx
        if want_bf16:
            rest[0][...] = dx.astype(BF16)

    row = pl.BlockSpec((tr, d), lambda i: (i, 0))
    vec = pl.BlockSpec((1, d), lambda i: (0, 0))
    out_specs = [row] + ([row] if want_bf16 else []) + [vec]
    out_shape = ([jax.ShapeDtypeStruct((t, d), F32)]
                 + ([jax.ShapeDtypeStruct((t, d), BF16)] if want_bf16 else [])
                 + [jax.ShapeDtypeStruct((1, d), F32)])
    return _carrier_call(
        body, (dh, x, dres, g), comm, name=name,
        grid=(t // tr,),
        in_specs=[row, row, row, vec],
        out_specs=out_specs, out_shape=out_shape,
        sem=("arbitrary",))


def _dact_call(dx3b, w_down, act, comm=None):
    t, d = dx3b.shape
    f = w_down.shape[0]
    tm = _tile(t, 1024)
    tn = _tile(f, 2048)

    def body(g_ref, w_ref, act_ref, o_ref):
        dact = _dot_nt(g_ref[...], w_ref[...])
        o_ref[...] = (dact * act_ref[...].astype(F32)).astype(BF16)

    return _carrier_call(
        body, (dx3b, w_down, act), comm, name="dact_bwd",
        grid=(t // tm, f // tn),
        in_specs=[pl.BlockSpec((tm, d), lambda i, j: (i, 0)),
                  pl.BlockSpec((tn, d), lambda i, j: (j, 0)),
                  pl.BlockSpec((tm, tn), lambda i, j: (i, j))],
        out_specs=pl.BlockSpec((tm, tn), lambda i, j: (i, j)),
        out_shape=jax.ShapeDtypeStruct((t, f), BF16),
        sem=("parallel", "parallel"))


def _wgrad_call(name, a, b, out_blocks, out_block_cols, comm=None, *, t1, t2=None, merge=1):
    t, k1 = a.shape
    k2 = b.shape[1]
    tt = _tile(t, 2048)
    t1 = _tile(k1, t1)
    t2 = _tile(k2, t2) if out_blocks is None else merge * out_block_cols
    nk = t // tt

    def body(a_ref, b_ref, o_ref, acc_ref):
        k = pl.program_id(2)

        @pl.when(k == 0)
        def _():
            acc_ref[...] = _dot_tn(a_ref[...], b_ref[...])

        @pl.when(k > 0)
        def _():
            acc_ref[...] += _dot_tn(a_ref[...], b_ref[...])

        @pl.when(k == nk - 1)
        def _():
            if out_blocks is None:
                o_ref[...] = acc_ref[...].astype(BF16)
            else:
                for blk in range(merge):
                    o_ref[blk] = acc_ref[:, blk * out_block_cols:(blk + 1) * out_block_cols].astype(BF16)

    if out_blocks is None:
        out_spec = pl.BlockSpec((t1, t2), lambda i, j, k: (i, j))
        out_shape = jax.ShapeDtypeStruct((k1, k2), BF16)
    else:
        out_spec = pl.BlockSpec((merge, t1, out_block_cols), lambda i, j, k: (j, i, 0))
        out_shape = jax.ShapeDtypeStruct((out_blocks, k1, out_block_cols), BF16)
    return _carrier_call(
        body, (a, b), comm, name=name,
        grid=(k1 // t1, k2 // t2, nk),
        in_specs=[pl.BlockSpec((tt, t1), lambda i, j, k: (k, i)),
                  pl.BlockSpec((tt, t2), lambda i, j, k: (k, j))],
        out_specs=out_spec, out_shape=out_shape,
        scratch_shapes=[pltpu.VMEM((t1, t2), F32)],
        sem=("parallel", "parallel", "arbitrary"))


def _dgrad_blocked_call(name, g, w_g, comm=None, *, merge=1):
    t = g.shape[0]
    nb, d, cb = w_g.shape
    tm = _tile(t, 1024)
    tn = _tile(d, 2048)
    tk = merge * cb

    def body(g_ref, w_ref, o_ref):
        def product():
            w = w_ref[0] if merge == 1 else jnp.concatenate([w_ref[b] for b in range(merge)], axis=1)
            return _dot_nt(g_ref[...], w)

        @pl.when(pl.program_id(2) == 0)
        def _():
            o_ref[...] = product()

        @pl.when(pl.program_id(2) > 0)
        def _():
            o_ref[...] += product()

    return _carrier_call(
        body, (g, w_g), comm, name=name,
        grid=(t // tm, d // tn, nb // merge),
        in_specs=[pl.BlockSpec((tm, tk), lambda i, j, k: (i, k)),
                  pl.BlockSpec((merge, tn, cb), lambda i, j, k: (k, j, 0))],
        out_specs=pl.BlockSpec((tm, tn), lambda i, j, k: (i, j)),
        out_shape=jax.ShapeDtypeStruct((t, d), F32),
        sem=("parallel", "parallel", "arbitrary"))


def _dmixed_call(dx2b, w_out, comm=None):
    t, d = dx2b.shape
    e = w_out.shape[0]
    tm = _tile(t, 1024)
    tn = _tile(e, 1024)

    def body(g_ref, w_ref, o_ref):
        o_ref[...] = _dot_nt(g_ref[...], w_ref[...])

    return _carrier_call(
        body, (dx2b, w_out), comm, name="dmixed_bwd",
        grid=(t // tm, e // tn),
        in_specs=[pl.BlockSpec((tm, d), lambda i, j: (i, 0)),
                  pl.BlockSpec((tn, d), lambda i, j: (j, 0))],
        out_specs=pl.BlockSpec((tm, tn), lambda i, j: (i, j)),
        out_shape=jax.ShapeDtypeStruct((t, e), F32),
        sem=("parallel", "parallel"))


def _mixer_bwd_call(proj, dmixed, w_s, bs_t, g_v, w_pool_g, pool_scale, comm=None):
    t = proj.shape[0]
    tt = _tile(t, 512)
    nchunk = tt // CHUNK
    hb = tt // HALO
    last_halo = t // HALO - 1
    nsteps = t // tt
    rb = GROUP // N_DEV

    def body(pu_ref, pv_ref, z_ref, zp_ref, da_ref, db_ref, dbn_ref, ws_ref, bs_ref, gv_ref, wp_ref, ps_ref,
             dproj_ref, dws_ref, dbs_ref, dgv_ref, dps_ref, dwp_ref):
        i = pl.program_id(0)

        @pl.when(i == 0)
        def _():
            dws_ref[...] = jnp.zeros_like(dws_ref)
            dbs_ref[...] = jnp.zeros_like(dbs_ref)
            dgv_ref[...] = jnp.zeros_like(dgv_ref)
            dps_ref[...] = jnp.zeros_like(dps_ref)
            dwp_ref[...] = jnp.zeros_like(dwp_ref)

        tril = (lax.broadcasted_iota(jnp.int32, (CHUNK, CHUNK), 0)
                >= lax.broadcasted_iota(jnp.int32, (CHUNK, CHUNK), 1))
        for h in range(N_HEADS):
            cols = slice(h * CHUNK, (h + 1) * CHUNK)
            v, dv_dpv = _gelu_and_grad(pv_ref[:, cols])
            vhat, rv = _rms_rows(v)
            gv = gv_ref[:, cols]
            vn = (vhat * gv).astype(BF16)
            u, du_dpu = _gelu_and_grad(pu_ref[:, cols])
            w = jnp.where(tril, ws_ref[h], 0.0).astype(BF16)
            bcol = bs_ref[:, h:h + 1]
            dout = da_ref[:, cols]
            dmix = dout * u
            dmix_b = dmix.astype(BF16)
            dws = jnp.zeros((CHUNK, CHUNK), F32)
            dbs = jnp.zeros((CHUNK, 1), F32)
            dvn_parts = []
            du_parts = []
            for c in range(nchunk):
                rows = slice(c * CHUNK, (c + 1) * CHUNK)
                mixed = _dot_nn(w, vn[rows]) + bcol
                du_parts.append(dout[rows] * mixed)
                dvn_parts.append(_dot_tn(w, dmix_b[rows]))
                dws = dws + _dot_nt(dmix_b[rows], vn[rows])
                dbs = dbs + jnp.sum(dmix[rows], axis=1, keepdims=True)
            dws_ref[h] += jnp.where(tril, dws, 0.0)
            dbs_ref[:, h:h + 1] += dbs
            dvn = jnp.concatenate(dvn_parts, axis=0)
            du = jnp.concatenate(du_parts, axis=0)
            dgv_ref[:, cols] += jnp.sum(dvn * vhat, axis=0, keepdims=True)
            dv = _rms_bwd_rows(dvn * gv, vhat, rv)
            dproj_ref[:, cols] = (du * du_dpu).astype(BF16)
            dproj_ref[:, A_WIDTH + h * CHUNK:A_WIDTH + (h + 1) * CHUNK] = (dv * dv_dpv).astype(BF16)

        zprev = jnp.where(i > 0, zp_ref[...], 0.0)
        ext = jnp.concatenate([zprev, z_ref[...]], axis=0)
        dnext = jnp.where(i < nsteps - 1, dbn_ref[...], 0.0)
        dext = jnp.concatenate([db_ref[...], dnext], axis=0)
        for g, win in enumerate(POOL_WINDOWS):
            cols = slice(g * GROUP, (g + 1) * GROUP)
            zg = ext[:, cols]
            pooled = _window_sum_back(zg, win)[HALO:] / _pool_counts(i * tt, tt, win) - zg[HALO:]
            pooled_b = pooled.astype(BF16)
            wp = wp_ref[:, g].reshape(GROUP, GROUP)
            y = _dot_nn(pooled_b, wp)
            dout = dext[:, cols]
            dps_ref[:, cols] += jnp.sum(dout[:tt] * y, axis=0, keepdims=True)
            dy_b = (dout * ps_ref[:, cols]).astype(BF16)
            dwp_ref[:, g] += _dot_tn(pooled_b, dy_b[:tt]).reshape(N_DEV, rb, GROUP)
            dpooled = _dot_nt(dy_b, wp)
            q = dpooled / _pool_counts(i * tt, tt + HALO, win)
            dz = _window_sum_fwd(q, win)[:tt] - dpooled[:tt]
            dproj_ref[:, 2 * A_WIDTH + g * GROUP:2 * A_WIDTH + (g + 1) * GROUP] = dz.astype(BF16)

    def full(shape):
        return pl.BlockSpec(shape, lambda i: (0,) * len(shape))

    return _carrier_call(
        body, (proj, proj, proj, proj, dmixed, dmixed, dmixed, w_s, bs_t, g_v, w_pool_g, pool_scale), comm,
        name="mixer_bwd",
        grid=(nsteps,),
        in_specs=[pl.BlockSpec((tt, A_WIDTH), lambda i: (i, 0)),
                  pl.BlockSpec((tt, A_WIDTH), lambda i: (i, 1)),
                  pl.BlockSpec((tt, B_WIDTH), lambda i: (i, 2)),
                  pl.BlockSpec((HALO, B_WIDTH), lambda i: (jnp.maximum(i * hb - 1, 0), 2)),
                  pl.BlockSpec((tt, A_WIDTH), lambda i: (i, 0)),
                  pl.BlockSpec((tt, B_WIDTH), lambda i: (i, 1)),
                  pl.BlockSpec((HALO, B_WIDTH), lambda i: (jnp.minimum((i + 1) * hb, last_halo), 1)),
                  full((N_HEADS, CHUNK, CHUNK)), full((CHUNK, N_HEADS)), full((1, A_WIDTH)),
                  full((N_DEV, 4, rb, GROUP)), full((1, B_WIDTH))],
        out_specs=[pl.BlockSpec((tt, 2 * A_WIDTH + B_WIDTH), lambda i: (i, 0)),
                   full((N_HEADS, CHUNK, CHUNK)), full((CHUNK, N_HEADS)), full((1, A_WIDTH)),
                   full((1, B_WIDTH)), full((N_DEV, 4, rb, GROUP))],
        out_shape=[jax.ShapeDtypeStruct((t, 2 * A_WIDTH + B_WIDTH), BF16),
                   jax.ShapeDtypeStruct((N_HEADS, CHUNK, CHUNK), F32),
                   jax.ShapeDtypeStruct((CHUNK, N_HEADS), F32),
                   jax.ShapeDtypeStruct((1, A_WIDTH), F32),
                   jax.ShapeDtypeStruct((1, B_WIDTH), F32),
                   jax.ShapeDtypeStruct((N_DEV, 4, rb, GROUP), F32)],
        sem=("arbitrary",))


def _adamw(w, g, m, v):
    m = ADAM_B1 * m + (1.0 - ADAM_B1) * g
    v = ADAM_B2 * v + (1.0 - ADAM_B2) * (g * g)
    m_hat = m / ADAM_C1
    v_hat = v / ADAM_C2
    delta = -ADAM_LR * (m_hat / (jnp.sqrt(v_hat) + ADAM_EPS) + ADAM_WD * w)
    return delta, m, v


PAIR_SUM_TILE_ELEMS = 1024 * 1024
ADAMW_TILE_ELEMS = 512 * 1024


def _row_tile(r, c, elems):
    t = r
    while t * c > elems and t % 32 == 0:
        t //= 2
    return t


def _pair_sum_call(name, pos, grad, got):
    _, r, c = grad.shape
    tr = _row_tile(r, c, PAIR_SUM_TILE_ELEMS)

    def chip_of(rel, pos_ref):
        px = jnp.where((rel == 0) | (rel == 2), 1 - pos_ref[0], pos_ref[0])
        py = jnp.where((rel == 1) | (rel == 2), 1 - pos_ref[1], pos_ref[1])
        return 2 * px + py

    def body(pos_ref, own_ref, got_ref, out_ref):
        out_ref[...] = (own_ref[...].astype(F32) + got_ref[...].astype(F32)).astype(BF16)

    return pl.pallas_call(
        body, name=name,
        grid_spec=pltpu.PrefetchScalarGridSpec(
            num_scalar_prefetch=1, grid=(3, r // tr),
            in_specs=[pl.BlockSpec((None, tr, c), lambda k, i, p: (2 * chip_of(k, p) + p[2], i, 0)),
                      pl.BlockSpec((None, tr, c), lambda k, i, p: (chip_of(k, p), i, 0))],
            out_specs=pl.BlockSpec((None, tr, c), lambda k, i, p: (k, i, 0))),
        out_shape=jax.ShapeDtypeStruct((3, r, c), BF16),
        compiler_params=_params("parallel", "parallel"),
    )(pos, grad, got)


def _final_call(name, pos, grad, got_pair, got_chips, w, m, v):
    _, r, c = grad.shape
    tr = _row_tile(r, c, ADAMW_TILE_ELEMS)

    def body(pos_ref, own_ref, pair_ref, chips_ref, w_ref, m_ref, v_ref, g_out, d_out, m_out, v_out):
        g = own_ref[...].astype(F32) + pair_ref[...].astype(F32)
        for j in range(3):
            g = g + chips_ref[j].astype(F32)
        delta, m_new, v_new = _adamw(w_ref[...], g, m_ref[...], v_ref[...])
        g_out[...] = g
        d_out[...] = delta
        m_out[...] = m_new
        v_out[...] = v_new

    row = pl.BlockSpec((tr, c), lambda i, p: (i, 0))
    return pl.pallas_call(
        body, name=name,
        grid_spec=pltpu.PrefetchScalarGridSpec(
            num_scalar_prefetch=1, grid=(r // tr,),
            in_specs=[pl.BlockSpec((None, tr, c), lambda i, p: (4 * p[0] + 2 * p[1] + p[2], i, 0)),
                      pl.BlockSpec((None, tr, c), lambda i, p: (2 * p[0] + p[1], i, 0)),
                      pl.BlockSpec((3, tr, c), lambda i, p: (0, i, 0)), row, row, row],
            out_specs=[row] * 4),
        out_shape=[jax.ShapeDtypeStruct((r, c), F32)] * 4,
        compiler_params=_params("parallel"),
    )(pos, grad, got_pair, got_chips, w, m, v)


def _small_final_call(name, parts, w, m, v):
    _, rows, c = parts.shape
    r = w.shape[0]

    def body(p_ref, w_ref, m_ref, v_ref, g_out, d_out, m_out, v_out):
        g = p_ref[0]
        for k in range(1, N_DEV):
            g = g + p_ref[k]
        delta, m_new, v_new = _adamw(w_ref[...], g[:r], m_ref[...], v_ref[...])
        g_out[...] = g
        d_out[...] = delta
        m_out[...] = m_new
        v_out[...] = v_new

    return pl.pallas_call(
        body, name=name,
        out_shape=[jax.ShapeDtypeStruct((rows, c), F32)] + [jax.ShapeDtypeStruct((r, c), F32)] * 3,
        compiler_params=pltpu.CompilerParams(vmem_limit_bytes=VMEM_LIMIT),
    )(parts, w, m, v)


_SMALL_EARLY = ("g_v", "w_s", "b_s", "pool_scale", "g_ffn", "g_final")
_BIG = ("w_in", "w_pool", "w_out", "w_up", "w_down")
_ORDER = ("g_mix", "w_in", "g_v", "w_s", "b_s", "w_pool", "pool_scale", "w_out", "g_ffn", "w_up", "w_down", "g_final")


def _pack(parts):
    return jnp.concatenate([p.reshape(-1, LANES) for p in parts], axis=0)


def _unpack(packed, like):
    out, row = [], 0
    for a in like:
        rows = a.size // LANES
        out.append(packed[row:row + rows].reshape(a.shape))
        row += rows
    return out


def kernel(x, g_mix, w_in, g_v, w_s, b_s, w_pool, pool_scale, w_out, g_ffn, w_up, w_down, g_final, loss_target, m_g_mix, m_w_in, m_g_v, m_w_s, m_b_s, m_w_pool, m_pool_scale, m_w_out, m_g_ffn, m_w_up, m_w_down, m_g_final, v_g_mix, v_w_in, v_g_v, v_w_s, v_b_s, v_w_pool, v_pool_scale, v_w_out, v_g_ffn, v_w_up, v_w_down, v_g_final):
    weights = dict(g_mix=g_mix, w_in=w_in, g_v=g_v, w_s=w_s, b_s=b_s, w_pool=w_pool, pool_scale=pool_scale,
                   w_out=w_out, g_ffn=g_ffn, w_up=w_up, w_down=w_down, g_final=g_final)
    mom = dict(g_mix=m_g_mix, w_in=m_w_in, g_v=m_g_v, w_s=m_w_s, b_s=m_b_s, w_pool=m_w_pool,
               pool_scale=m_pool_scale, w_out=m_w_out, g_ffn=m_g_ffn, w_up=m_w_up, w_down=m_w_down,
               g_final=m_g_final)
    var = dict(g_mix=v_g_mix, w_in=v_w_in, g_v=v_g_v, w_s=v_w_s, b_s=v_b_s, w_pool=v_w_pool,
               pool_scale=v_pool_scale, w_out=v_w_out, g_ffn=v_g_ffn, w_up=v_w_up, w_down=v_w_down,
               g_final=v_g_final)

    t, d = x.shape[1], x.shape[2]
    xs = x.reshape(t, d)
    target = loss_target.reshape(t, d)

    shard2d = dict(w_in=w_in.reshape(d, -1), w_pool=w_pool.reshape(-1, GROUP), w_out=w_out.reshape(-1, d),
                   w_up=w_up.reshape(d, -1), w_down=w_down.reshape(-1, d))
    sb = {k: shard2d[k].astype(BF16) for k in _BIG}
    rows = {k: sb[k].shape[0] for k in _BIG}

    def gathered_shape(k):
        return jax.ShapeDtypeStruct((N_DEV,) + sb[k].shape, BF16)

    def landing(n, like):
        return jax.ShapeDtypeStruct((n,) + like.shape[1:], like.dtype)

    def from_everyone(block):
        return jax.ShapeDtypeStruct((N_DEV,) + block.shape, block.dtype)

    def cuts(r, fractions):
        return [0] + [int(r * f) // 16 * 16 for f in fractions] + [r]

    g_mix2, g_ffn2, g_final2 = g_mix.reshape(1, d), g_ffn.reshape(1, d), g_final.reshape(1, d)
    g_v2, ps2 = g_v.reshape(1, A_WIDTH), pool_scale.reshape(1, B_WIDTH)
    w_s3 = w_s.reshape(N_HEADS, CHUNK, CHUNK)
    bs_t = b_s.reshape(N_HEADS, CHUNK).T
    xi, yi, ci = _position()
    pos = jnp.stack([xi, yi, ci]).astype(jnp.int32)

    order = (4 * xi + 2 * yi + ci) ^ jnp.array(ARRIVAL_ORDER, jnp.int32)
    u = cuts(rows["w_up"], (0.25, 0.55))
    ahead = cuts(rows["w_down"], (0.6,))[1]
    proj, h1, w_in_g, w_out_g, w_pool_g = _norm_matmul_stream_call(
        "proj_fwd", xs, g_mix2, sb["w_in"], order, None, 0, 0, _Comm(
            [sb["w_out"], sb["w_pool"]], [], [gathered_shape("w_out"), gathered_shape("w_pool")],
            lambda s, l: [_gather_to_neighbours(s[0], l[0], 0, rows["w_out"]) + _everyone(s[1], l[1])]),
        lambda a: (a,), (F32,))
    w_pool_g = w_pool_g.reshape(N_DEV, 4, GROUP // N_DEV, GROUP)
    mixed, w_out_g, w_up_g = _mixer_fwd_call(proj, w_s3, bs_t, g_v2, w_pool_g, ps2, _Comm(
        [sb["w_up"]], [w_out_g], [gathered_shape("w_up")],
        lambda s, l: [_gather_relay(l[0], 0, rows["w_out"]) + _gather_to_neighbours(s[0], l[1], u[0], u[1]),
                      _gather_diagonal_pass_on(l[0], 0, rows["w_out"])]))
    w_out_f = w_out_g.reshape(-1, d)
    x2, w_up_g = _out_proj_call(mixed, w_out_f, xs, _Comm(
        [sb["w_up"]], [w_up_g], [],
        lambda s, l: [_gather_relay(l[0], u[0], u[1]) + _gather_to_neighbours(s[0], l[0], u[1], u[2]),
                      _gather_diagonal_pass_on(l[0], u[0], u[1])]))

    def relu2_and_slope(a):
        r = jnp.maximum(a, 0.0)
        return r * r, 2.0 * r

    act, dact_da, h2, w_up_g, w_down_g = _norm_matmul_stream_call(
        "up_fwd", x2, g_ffn2, sb["w_up"], order, w_up_g, u[1], u[2], _Comm(
            [sb["w_down"]], [], [gathered_shape("w_down")],
            lambda s, l: [_gather_to_neighbours(s[0], l[0], 0, ahead)]),
        relu2_and_slope, (BF16, BF16))
    y, w_down_g = _down_call(act, sb["w_down"], order, w_down_g, ahead)
    w_down_f = w_down_g.reshape(-1, d)
    loss_part, dx3, dx3b, dg_final = _loss_call(y, x2, target, g_final2)

    def pair_sum(k, grad, got):
        return _pair_sum_call(k + "_pair_sum", pos, grad, got)

    def finish(k, grad, got_pair, got_chips):
        s = shard2d[k]
        outs = _final_call(k + "_adamw", pos, grad, got_pair, got_chips, s, mom[k].reshape(s.shape),
                           var[k].reshape(s.shape))
        return [o.reshape(weights[k].shape) for o in outs]

    result = {}
    (gw_down,) = _wgrad_call("w_down_grad", act, dx3b, None, None, t1=1024, t2=2048)
    gw_down = gw_down.reshape(N_DEV, -1, d)
    da, pair_down = _dact_call(dx3b, w_down_f, dact_da, _Comm(
        [gw_down], [], [landing(4, gw_down)], lambda s, l: [_pair_exchange(s[0], l[0])]))
    sums_down = pair_sum("w_down", gw_down, pair_down)
    dn = cuts(rows["w_down"], (0.75,))
    gw_up, got = _wgrad_call("w_up_grad", h2, da, N_DEV, w_up_g.shape[2], _Comm(
        [sums_down], [], [landing(3, sums_down)],
        lambda s, l: [_chip_exchange(s[0], l[0], dn[0], dn[1])]), t1=2048)
    dh2, got, pair_up = _dgrad_blocked_call("dh2_bwd", da, w_up_g, _Comm(
        [sums_down, gw_up], [got], [landing(4, gw_up)],
        lambda s, l: [_chip_exchange(s[0], l[0], dn[1], dn[2]) + _pair_exchange(s[1], l[1])]), merge=2)
    result["w_down"] = finish("w_down", gw_down, pair_down, got)
    sums_up = pair_sum("w_up", gw_up, pair_up)
    v = cuts(rows["w_up"], (0.26, 0.47, 0.69))
    dx2, dx2b, dg_ffn, got_up = _norm_bwd_call("ffn_norm_bwd", dh2, x2, dx3, g_ffn2, True, _Comm(
        [sums_up], [], [landing(3, sums_up)], lambda s, l: [_chip_exchange(s[0], l[0], v[0], v[1])]))
    dmixed, got_up = _dmixed_call(dx2b, w_out_f, _Comm(
        [sums_up], [got_up], [], lambda s, l: [_chip_exchange(s[0], l[0], v[1], v[2])]))
    gw_out, got_up = _wgrad_call("w_out_grad", mixed, dx2b, None, None, _Comm(
        [sums_up], [got_up], [], lambda s, l: [_chip_exchange(s[0], l[0], v[2], v[3])]), t1=2048, t2=1024)
    gw_out = gw_out.reshape(N_DEV, -1, d)
    dproj, dw_s, dbs_t, dg_v, dps, dw_pool, got_up, pair_out = _mixer_bwd_call(
        proj, dmixed, w_s3, bs_t, g_v2, w_pool_g, ps2, _Comm(
            [sums_up, gw_out], [got_up], [landing(4, gw_out)],
            lambda s, l: [_chip_exchange(s[0], l[0], v[3], v[4]) + _pair_exchange(s[1], l[1])]))
    result["w_up"] = finish("w_up", gw_up, pair_up, got_up)
    sums_out = pair_sum("w_out", gw_out, pair_out)
    gw_pool = dw_pool.astype(BF16).reshape(N_DEV, -1, GROUP)
    early = dict(g_v=dg_v, w_s=dw_s, b_s=dbs_t.T, pool_scale=dps, g_ffn=dg_ffn, g_final=dg_final)
    packed = _pack([early[k] for k in _SMALL_EARLY] + [loss_part])
    early_rows = packed.shape[0]
    gw_in, got, pair_pool, parts_early = _wgrad_call("w_in_grad", h1, dproj, N_DEV, w_in_g.shape[2], _Comm(
        [sums_out, gw_pool, packed], [], [landing(3, sums_out), landing(4, gw_pool), from_everyone(packed)],
        lambda s, l: [_chip_exchange(s[0], l[0], 0, rows["w_out"]) + _pair_exchange(s[1], l[1])
                      + _gather_first(s[2], l[2], 0, early_rows)]), t1=2048, merge=MERGE_W_IN)
    result["w_out"] = finish("w_out", gw_out, pair_out, got)
    sums_pool = pair_sum("w_pool", gw_pool, pair_pool)
    (pair_in,) = _comm_call("pair_exchange_w_in", _Comm(
        [gw_in], [], [landing(4, gw_in)], lambda s, l: [_pair_exchange(s[0], l[0])]))
    sums_in = pair_sum("w_in", gw_in, pair_in)
    dh1, parts_early, got, got_pool = _dgrad_blocked_call("dh1_bwd", dproj, w_in_g, _Comm(
        [sums_in, sums_pool], [parts_early], [landing(3, sums_in), landing(3, sums_pool)],
        lambda s, l: [_chip_exchange(s[0], l[1], 0, rows["w_in"]) + _chip_exchange(s[1], l[2], 0, rows["w_pool"])
                      + _gather_pass_on(l[0], 0, early_rows)]),
        merge=MERGE_W_IN)
    result["w_in"] = finish("w_in", gw_in, pair_in, got)
    result["w_pool"] = finish("w_pool", gw_pool, pair_pool, got_pool)
    grad_x, dg_mix = _norm_bwd_call("mix_norm_bwd", dh1, xs, dx2, g_mix2, False)
    packed = _pack([dg_mix])
    (parts_late,) = _comm_call("gather_g_mix_grad", _Comm(
        [packed], [], [from_everyone(packed)], lambda s, l: [_everyone(s[0], l[0])]))

    for names, parts, tag in ((_SMALL_EARLY, parts_early, "small_adamw"), (("g_mix",), parts_late, "g_mix_adamw")):
        outs = _small_final_call(tag, parts, _pack([weights[k] for k in names]), _pack([mom[k] for k in names]),
                                 _pack([var[k] for k in names]))
        if tag == "small_adamw":
            loss = outs[0][-1, 0]
        like = [weights[k] for k in names]
        unpacked = [_unpack(o, like) for o in outs]
        for idx, k in enumerate(names):
            result[k] = [unpacked[q][idx] for q in range(4)]

    grads = [result[k][0] for k in _ORDER]
    deltas = [result[k][1] for k in _ORDER]
    new_m = [result[k][2] for k in _ORDER]
    new_v = [result[k][3] for k in _ORDER]
    return (loss, grad_x.reshape(x.shape), *grads, *deltas, *new_m, *new_v)
```

```python
import functools
import math

import jax
import jax.numpy as jnp
from jax import lax
from jax.experimental import pallas as pl
from jax.experimental.pallas import tpu as pltpu

F32 = jnp.float32
BF16 = jnp.bfloat16
MESH = pl.DeviceIdType.MESH

N_DEV = 8
EPS = 1e-6
CHUNK = 128
N_HEADS = 8
A_WIDTH = 1024
B_WIDTH = 1024
POOL_WINDOWS = (2, 4, 8, 16)
GROUP = 256
HALO = 16
LANES = 128

ADAM_LR = 0.001
ADAM_B1 = 0.9
ADAM_B2 = 0.999
ADAM_EPS = 1e-08
ADAM_WD = 0.01
ADAM_STEP = 10
ADAM_C1 = 1.0 - ADAM_B1 ** ADAM_STEP
ADAM_C2 = 1.0 - ADAM_B2 ** ADAM_STEP

VMEM_LIMIT = 56 * 1024 * 1024
MERGE_W_IN = 2

_GELU_C = math.sqrt(2.0 / math.pi)


def _params(*sem):
    return pltpu.CompilerParams(dimension_semantics=sem, vmem_limit_bytes=VMEM_LIMIT)


def _gelu(x):
    return 0.5 * x * (1.0 + jnp.tanh(_GELU_C * (x + 0.044715 * x * x * x)))


def _gelu_and_grad(x):
    t = jnp.tanh(_GELU_C * (x + 0.044715 * x * x * x))
    g = 0.5 * x * (1.0 + t)
    dg = 0.5 * (1.0 + t) + 0.5 * x * (1.0 - t * t) * (_GELU_C * (1.0 + 3.0 * 0.044715 * x * x))
    return g, dg


def _dot_nn(a, b):
    return lax.dot_general(a, b, (((1,), (0,)), ((), ())), preferred_element_type=F32)


def _dot_nt(a, b):
    return lax.dot_general(a, b, (((1,), (1,)), ((), ())), preferred_element_type=F32)


def _dot_tn(a, b):
    return lax.dot_general(a, b, (((0,), (0,)), ((), ())), preferred_element_type=F32)


def _rms_rows(x):
    r = lax.rsqrt(jnp.mean(x * x, axis=-1, keepdims=True) + EPS)
    return x * r, r


def _rms_bwd_rows(dn, n, r):
    return r * (dn - n * jnp.mean(dn * n, axis=-1, keepdims=True))


def _tile(n, want):
    t = min(n, want)
    assert n % t == 0, (n, want)
    return t


_ANY = pl.BlockSpec(memory_space=pl.ANY)

SIBLING = 1
CHIPS = (4, 2, 6)


def _position():
    return lax.axis_index("x"), lax.axis_index("y"), lax.axis_index("c")


def _me():
    x, y, c = _position()
    return 4 * x + 2 * y + c


def _peer(rel):
    x, y, c = _position()
    return (x ^ ((rel >> 2) & 1), y ^ ((rel >> 1) & 1), c ^ (rel & 1))


class _Comm:
    def __init__(self, srcs, lands, new, plan):
        self.srcs, self.lands, self.new, self.plan = list(srcs), list(lands), list(new), plan


def _make_copies(phases, send_sems, recv_sems, local_sems):
    out, nr, nl = [], 0, 0
    for phase in phases:
        cps = []
        for item in phase:
            if item[0] == "local":
                cps.append(pltpu.make_async_copy(item[1], item[2], local_sems.at[nl]))
                nl += 1
            else:
                cps.append(pltpu.make_async_remote_copy(
                    src_ref=item[1], dst_ref=item[2], send_sem=send_sems.at[nr], recv_sem=recv_sems.at[nr],
                    device_id=_peer(item[3]), device_id_type=MESH))
                nr += 1
        out.append(cps)
    return out


def _count_copies(comm):
    phases = comm.plan([_FakeRef() for _ in comm.srcs], [_FakeRef() for _ in range(len(comm.lands) + len(comm.new))])
    items = [it for ph in phases for it in ph]
    return sum(it[0] == "remote" for it in items), sum(it[0] == "local" for it in items)


class _FakeRef:
    def __getitem__(self, idx):
        return self

    @property
    def at(self):
        return self


def _carrier_call(body, args, comm, *, name, grid, in_specs, out_specs, out_shape, scratch_shapes=(), sem):
    if not isinstance(out_shape, (list, tuple)):
        out_specs, out_shape = [out_specs], [out_shape]
    out_specs, out_shape, scratch_shapes = list(out_specs), list(out_shape), list(scratch_shapes)
    if comm is None:
        res = pl.pallas_call(body, name=name, grid=grid, in_specs=list(in_specs), out_specs=out_specs,
                             out_shape=out_shape, scratch_shapes=scratch_shapes, compiler_params=_params(*sem))(*args)
        return list(res)
    n_in, n_out, n_scr = len(args), len(out_shape), len(scratch_shapes)
    ns, nl, nn = len(comm.srcs), len(comm.lands), len(comm.new)
    n_remote, n_local = _count_copies(comm)
    steps = math.prod(grid)

    def wrapped(*refs):
        ins, srcs = refs[:n_in], refs[n_in:n_in + ns]
        o = n_in + ns + nl
        outs, lands = refs[o:o + n_out], refs[o + n_out:o + n_out + nl + nn]
        scr = refs[o + n_out + nl + nn:]
        phases = _make_copies(comm.plan(srcs, lands), *scr[n_scr:])
        assert len(phases) == 1 or (len(phases) == 2 and steps >= 3)
        step = functools.reduce(lambda acc, a: acc * grid[a] + pl.program_id(a), range(len(grid)), 0)

        @pl.when(step == 0)
        def _():
            for cp in phases[0]:
                cp.start()

        if len(phases) == 2:
            @pl.when(step == steps * 3 // 4)
            def _():
                for cp in phases[0]:
                    cp.wait()
                for cp in phases[1]:
                    cp.start()

        body(*ins, *outs, *scr[:n_scr])

        @pl.when(step == steps - 1)
        def _():
            for cp in phases[-1]:
                cp.wait()

    land_shapes = [jax.ShapeDtypeStruct(a.shape, a.dtype) for a in comm.lands] + comm.new
    sems = [pltpu.SemaphoreType.DMA((max(n_remote, 1),)), pltpu.SemaphoreType.DMA((max(n_remote, 1),)),
            pltpu.SemaphoreType.DMA((max(n_local, 1),))]
    res = pl.pallas_call(
        wrapped, name=name, grid=grid,
        in_specs=list(in_specs) + [_ANY] * (ns + nl), out_specs=out_specs + [_ANY] * (nl + nn),
        out_shape=out_shape + land_shapes, scratch_shapes=scratch_shapes + sems,
        input_output_aliases={n_in + ns + k: n_out + k for k in range(nl)},
        compiler_params=_params(*sem))(*args, *comm.srcs, *comm.lands)
    return list(res)


def _comm_call(name, comm):
    ns, nl, nn = len(comm.srcs), len(comm.lands), len(comm.new)
    n_remote, n_local = _count_copies(comm)

    def body(*refs):
        srcs, lands, sems = refs[:ns], refs[ns + nl:ns + nl + nl + nn], refs[ns + nl + nl + nn:]
        for copies in _make_copies(comm.plan(srcs, lands), *sems):
            for cp in copies:
                cp.start()
            for cp in copies:
                cp.wait()

    land_shapes = [jax.ShapeDtypeStruct(a.shape, a.dtype) for a in comm.lands] + comm.new
    res = pl.pallas_call(
        body, name=name,
        in_specs=[_ANY] * (ns + nl), out_specs=[_ANY] * (nl + nn), out_shape=land_shapes,
        scratch_shapes=[pltpu.SemaphoreType.DMA((max(n_remote, 1),)), pltpu.SemaphoreType.DMA((max(n_remote, 1),)),
                        pltpu.SemaphoreType.DMA((max(n_local, 1),))],
        input_output_aliases={ns + k: k for k in range(nl)},
    )(*comm.srcs, *comm.lands)
    return list(res)


_HBM = pl.BlockSpec(memory_space=pltpu.HBM)
_SEM = pl.BlockSpec(memory_space=pltpu.SEMAPHORE)


def _pair_exchange_copies(grad_ref, land_ref, send_sems, recv_sems):
    _, _, c = _position()
    return [pltpu.make_async_remote_copy(
        src_ref=grad_ref.at[2 * chip + (1 - c)], dst_ref=land_ref.at[chip], send_sem=send_sems.at[chip],
        recv_sem=recv_sems.at[chip], device_id=_peer(SIBLING), device_id_type=MESH) for chip in range(4)]


def _pair_exchange_start_call(name, grad):
    def body(grad_ref, land_ref, send_sems, recv_sems, grad_thru, land_thru, token):
        for cp in _pair_exchange_copies(grad_ref, land_ref, send_sems, recv_sems):
            cp.start()
        token[...] = jnp.zeros_like(token)

    land = lax.empty((4,) + grad.shape[1:], grad.dtype)
    return pl.pallas_call(
        body, name=name,
        out_shape=(pltpu.SemaphoreType.DMA((4,)), pltpu.SemaphoreType.DMA((4,)), pltpu.HBM(grad.shape, grad.dtype),
                   pltpu.HBM(land.shape, land.dtype), jax.ShapeDtypeStruct((8, LANES), grad.dtype)),
        in_specs=(_HBM, _HBM), out_specs=(_SEM, _SEM, _HBM, _HBM, pl.BlockSpec(memory_space=pltpu.VMEM)),
        input_output_aliases={0: 2, 1: 3},
        compiler_params=pltpu.CompilerParams(has_side_effects=pltpu.SideEffectType.DATAFLOW_SIDE_EFFECTING),
    )(pltpu.with_memory_space_constraint(grad, pltpu.HBM), pltpu.with_memory_space_constraint(land, pltpu.HBM))


def _pair_exchange_wait_call(name, send_sems, recv_sems, grad, land, after):
    def body(grad_ref, land_ref, send_sems, recv_sems, after_ref, grad_out, land_out):
        for cp in _pair_exchange_copies(grad_ref, land_ref, send_sems, recv_sems):
            cp.wait_send()
            cp.wait_recv()

    return pl.pallas_call(
        body, name=name,
        out_shape=(pltpu.HBM(grad.shape, grad.dtype), pltpu.HBM(land.shape, land.dtype)),
        in_specs=(_HBM, _HBM, _SEM, _SEM, _ANY), out_specs=(_HBM, _HBM), input_output_aliases={0: 0, 1: 1},
        compiler_params=pltpu.CompilerParams(has_side_effects=pltpu.SideEffectType.DATAFLOW_SIDE_EFFECTING),
    )(grad, land, send_sems, recv_sems, after)


def _rows(ref, block, r0, r1):
    return ref.at[block, pl.ds(r0, r1 - r0)]


def _gather_first(shard, land, r0, r1):
    src = shard.at[pl.ds(r0, r1 - r0)]
    dst = _rows(land, _me(), r0, r1)
    return [("local", src, dst)] + [("remote", src, dst, rel) for rel in (SIBLING,) + CHIPS]


def _gather_pass_on(land, r0, r1):
    return [("remote", _rows(land, _me() ^ rel, r0, r1), _rows(land, _me() ^ rel, r0, r1), SIBLING) for rel in CHIPS]


def _split_rows(r0, r1):
    m = (r0 + r1) // 2 // 16 * 16
    return (r0, m), (m, r1)


def _gather_to_neighbours(shard, land, r0, r1):
    src = shard.at[pl.ds(r0, r1 - r0)]
    dst = _rows(land, _me(), r0, r1)
    return [("local", src, dst)] + [("remote", src, dst, rel) for rel in (SIBLING, 4, 2)]


def _gather_relay(land, r0, r1):
    lo, hi = _split_rows(r0, r1)
    x_block, y_block = _me() ^ 4, _me() ^ 2
    return [("remote", _rows(land, x_block, *lo), _rows(land, x_block, *lo), 2),
            ("remote", _rows(land, y_block, *hi), _rows(land, y_block, *hi), 4),
            ("remote", _rows(land, x_block, r0, r1), _rows(land, x_block, r0, r1), SIBLING),
            ("remote", _rows(land, y_block, r0, r1), _rows(land, y_block, r0, r1), SIBLING)]


def _gather_diagonal_pass_on(land, r0, r1):
    rows = _rows(land, _me() ^ 6, r0, r1)
    return [("remote", rows, rows, SIBLING)]


def _pair_exchange(grad, land):
    _, _, c = _position()
    return [("remote", grad.at[2 * chip + (1 - c)], land.at[chip], SIBLING) for chip in range(4)]


def _chip_exchange(sums, land, r0, r1):
    return [("remote", _rows(sums, j, r0, r1), _rows(land, j, r0, r1), rel) for j, rel in enumerate(CHIPS)]


def _everyone(packed, land):
    dst = land.at[_me()]
    return [("local", packed, dst)] + [("remote", packed, dst, rel) for rel in range(1, N_DEV)]


def _pool_counts(row0, rows, win):
    pos = row0 + lax.broadcasted_iota(jnp.int32, (rows, 1), 0)
    return jnp.minimum(pos + 1, win).astype(F32)


def _window_sum_back(ext, win):
    s = ext
    k = 1
    while k < win:
        s = s + pltpu.roll(s, k, 0)
        k *= 2
    return s


def _window_sum_fwd(ext, win):
    n = ext.shape[0]
    s = ext
    k = 1
    while k < win:
        s = s + pltpu.roll(s, n - k, 0)
        k *= 2
    return s


def _mixer_fwd_call(proj, w_s, bs_t, g_v, w_pool_g, pool_scale, comm=None):
    t = proj.shape[0]
    tt = _tile(t, 512)
    nchunk = tt // CHUNK
    hb = tt // HALO

    def body(pu_ref, pv_ref, z_ref, zp_ref, ws_ref, bs_ref, gv_ref, wp_ref, ps_ref, out_ref):
        i = pl.program_id(0)
        tril = (lax.broadcasted_iota(jnp.int32, (CHUNK, CHUNK), 0)
                >= lax.broadcasted_iota(jnp.int32, (CHUNK, CHUNK), 1))
        for h in range(N_HEADS):
            cols = slice(h * CHUNK, (h + 1) * CHUNK)
            vhat, _ = _rms_rows(_gelu(pv_ref[:, cols]))
            vn = (vhat * gv_ref[:, cols]).astype(BF16)
            u = _gelu(pu_ref[:, cols])
            w = jnp.where(tril, ws_ref[h], 0.0).astype(BF16)
            bcol = bs_ref[:, h:h + 1]
            for c in range(nchunk):
                rows = slice(c * CHUNK, (c + 1) * CHUNK)
                mixed = _dot_nn(w, vn[rows]) + bcol
                out_ref[rows, cols] = (u[rows] * mixed).astype(BF16)

        zprev = jnp.where(i > 0, zp_ref[...], 0.0)
        ext = jnp.concatenate([zprev, z_ref[...]], axis=0)
        for g, win in enumerate(POOL_WINDOWS):
            cols = slice(g * GROUP, (g + 1) * GROUP)
            zg = ext[:, cols]
            s = _window_sum_back(zg, win)
            pooled = s[HALO:] / _pool_counts(i * tt, tt, win) - zg[HALO:]
            wp = wp_ref[:, g].reshape(GROUP, GROUP)
            y = _dot_nn(pooled.astype(BF16), wp)
            out_ref[:, A_WIDTH + g * GROUP:A_WIDTH + (g + 1) * GROUP] = (y * ps_ref[:, cols]).astype(BF16)

    return _carrier_call(
        body, (proj, proj, proj, proj, w_s, bs_t, g_v, w_pool_g, pool_scale), comm, name="mixer_fwd",
        grid=(t // tt,),
        in_specs=[pl.BlockSpec((tt, A_WIDTH), lambda i: (i, 0)),
                  pl.BlockSpec((tt, A_WIDTH), lambda i: (i, 1)),
                  pl.BlockSpec((tt, B_WIDTH), lambda i: (i, 2)),
                  pl.BlockSpec((HALO, B_WIDTH), lambda i: (jnp.maximum(i * hb - 1, 0), 2)),
                  pl.BlockSpec((N_HEADS, CHUNK, CHUNK), lambda i: (0, 0, 0)),
                  pl.BlockSpec((CHUNK, N_HEADS), lambda i: (0, 0)),
                  pl.BlockSpec((1, A_WIDTH), lambda i: (0, 0)),
                  pl.BlockSpec((N_DEV, 4, GROUP // N_DEV, GROUP), lambda i: (0, 0, 0, 0)),
                  pl.BlockSpec((1, B_WIDTH), lambda i: (0, 0))],
        out_specs=pl.BlockSpec((tt, A_WIDTH + B_WIDTH), lambda i: (i, 0)),
        out_shape=jax.ShapeDtypeStruct((t, A_WIDTH + B_WIDTH), BF16),
        sem=("parallel",))


def _out_proj_call(mixed, w_out, x, comm=None):
    t, d = x.shape
    k = mixed.shape[1]
    tm = _tile(t, 1024)
    tn = _tile(d, 1024)

    def body(a_ref, w_ref, x_ref, o_ref):
        o_ref[...] = x_ref[...] + _dot_nn(a_ref[...], w_ref[...])

    return _carrier_call(
        body, (mixed, w_out, x), comm, name="out_proj_fwd",
        grid=(t // tm, d // tn),
        in_specs=[pl.BlockSpec((tm, k), lambda i, j: (i, 0)),
                  pl.BlockSpec((k, tn), lambda i, j: (0, j)),
                  pl.BlockSpec((tm, tn), lambda i, j: (i, j))],
        out_specs=pl.BlockSpec((tm, tn), lambda i, j: (i, j)),
        out_shape=jax.ShapeDtypeStruct((t, d), F32),
        sem=("parallel", "parallel"))


ARRIVAL_ORDER = (0, 1, 4, 5, 2, 3, 6, 7)
CARRIED_AFTER = 3


class _StreamedGather:
    def __init__(self, shard_ref, land_ref, wbuf, pre0, r0, send_sems, recv_sems, local_sem, fetch_sems):
        self.shard, self.land, self.wbuf, self.fetch_sems = shard_ref, land_ref, wbuf, fetch_sems
        end = shard_ref.shape[0]
        me = _me()
        self.me = me

        def remote(k, src, dst, rel):
            return pltpu.make_async_remote_copy(src_ref=src, dst_ref=dst, send_sem=send_sems.at[k],
                                                recv_sem=recv_sems.at[k], device_id=_peer(rel), device_id_type=MESH)

        def same_rows(k, block, a, b, rel):
            ref = land_ref.at[block, pl.ds(a, b - a)]
            return remote(k, ref, ref, rel)

        src = shard_ref.at[pl.ds(r0, end - r0)]
        dst = land_ref.at[me, pl.ds(r0, end - r0)]
        self.mine = pltpu.make_async_copy(src, dst, local_sem)
        self.first = [remote(k, src, dst, rel) for k, rel in enumerate((SIBLING, 4, 2))]
        lo, hi = _split_rows(r0, end)
        self.relay = [same_rows(3, me ^ 4, *lo, 2), same_rows(4, me ^ 2, *hi, 4)]
        self.passed = [same_rows(5, me ^ 4, r0, end, SIBLING), same_rows(6, me ^ 2, r0, end, SIBLING),
                       same_rows(7, me ^ 6, pre0, end, SIBLING)]
        self.early_relay, self.early_passed = [], []
        if r0 > pre0:
            lo, hi = _split_rows(pre0, r0)
            self.early_relay = [same_rows(8, me ^ 4, *lo, 2), same_rows(9, me ^ 2, *hi, 4)]
            self.early_passed = [same_rows(10, me ^ 4, pre0, r0, SIBLING), same_rows(11, me ^ 2, pre0, r0, SIBLING)]

    def _fetch(self, q):
        src = self.shard if q == 0 else self.land.at[self.me ^ ARRIVAL_ORDER[q]]
        return pltpu.make_async_copy(src, self.wbuf.at[q % 2], self.fetch_sems.at[q % 2])

    def start(self):
        self.mine.start()
        for cp in self.first + self.early_relay + self.early_passed:
            cp.start()
        self._fetch(0).start()

    def arrive(self, q):
        if q == 1:
            self.first[0].wait_recv()
        elif q in (2, 4):
            j = q // 2 - 1
            self.first[1 + j].wait_recv()
            self.relay[j].start()
            self.passed[j].start()
        elif q in (3, 5):
            j = q // 2 - 1
            self.passed[j].wait_recv()
            if self.early_passed:
                self.early_passed[j].wait_recv()
        elif q == 6:
            for cp in self.relay + self.early_relay:
                cp.wait_recv()
            self.passed[2].start()
        else:
            self.passed[2].wait_recv()
        self._fetch(q).start()

    def wait_fetch(self, slot):
        pltpu.make_async_copy(self.shard, self.wbuf.at[slot], self.fetch_sems.at[slot]).wait()

    def finish(self):
        for cp in self.first + self.relay + self.passed + self.early_relay + self.early_passed:
            cp.wait_send()
        self.mine.wait()


_STREAM_SEMS = [pltpu.SemaphoreType.DMA((12,)), pltpu.SemaphoreType.DMA((12,)), pltpu.SemaphoreType.DMA,
                pltpu.SemaphoreType.DMA((2,))]


def _stream_steps(gather, p, i, ni):
    @pl.when((p == 0) & (i == 0))
    def _():
        gather.start()

    @pl.when(i == 0)
    def _():
        gather.wait_fetch(p % 2)

    @pl.when(i == ni - 1)
    def _():
        for q in range(1, N_DEV):
            @pl.when(p == q - 1)
            def _():
                gather.arrive(q)


def _norm_matmul_stream_call(name, x, g, shard, order, land, pre0, r0, comm, epilogue, out_dtypes):
    t, d = x.shape
    cb = shard.shape[1]
    tm = _tile(t, 1024)
    ni = t // tm
    n_sems = len(_STREAM_SEMS)
    assert not comm.lands
    ns, nn, no = len(comm.srcs), len(comm.new), len(out_dtypes)
    n_remote, n_local = _count_copies(comm)
    has_land = land is not None

    def body(order_ref, x_ref, g_ref, shard_ref, *refs):
        refs = refs[has_land:]
        srcs, out_refs, (h_ref, land_ref) = refs[:ns], refs[ns:ns + no], refs[ns + no:ns + no + 2]
        new = refs[ns + no + 2:ns + no + 2 + nn]
        wbuf, sems = refs[ns + no + 2 + nn], refs[ns + no + 3 + nn:]
        p, i = pl.program_id(0), pl.program_id(1)
        gather = _StreamedGather(shard_ref, land_ref, wbuf, pre0, r0, *sems[:n_sems])
        (carried,) = _make_copies(comm.plan(srcs, new), *sems[n_sems:])
        rows = pl.ds(pl.multiple_of(i * tm, tm), tm)
        _stream_steps(gather, p, i, ni)

        @pl.when((p == CARRIED_AFTER) & (i == ni - 1))
        def _():
            for cp in carried:
                cp.start()

        @pl.when(p == 0)
        def _():
            n, _ = _rms_rows(x_ref[...])
            h_ref[rows, :] = (n * g_ref[...]).astype(BF16)

        tails = epilogue(_dot_nn(h_ref[rows, :], wbuf[p % 2]))
        for out_ref, tail, dt in zip(out_refs, tails, out_dtypes):
            out_ref[...] = tail.astype(dt)

        @pl.when((p == N_DEV - 1) & (i == ni - 1))
        def _():
            gather.finish()
            for cp in carried:
                cp.wait()

    carried_sems = [pltpu.SemaphoreType.DMA((max(n_remote, 1),)), pltpu.SemaphoreType.DMA((max(n_remote, 1),)),
                    pltpu.SemaphoreType.DMA((max(n_local, 1),))]
    return pl.pallas_call(
        body, name=name,
        grid_spec=pltpu.PrefetchScalarGridSpec(
            num_scalar_prefetch=1, grid=(N_DEV, ni),
            in_specs=[pl.BlockSpec((tm, d), lambda p, i, o: (jnp.where(p == 0, i, ni - 1), 0)),
                      pl.BlockSpec((1, d), lambda p, i, o: (0, 0)),
                      _ANY] + [_ANY] * (has_land + ns),
            out_specs=[pl.BlockSpec((tm, cb), lambda p, i, o: (i, o[p]))] * no
                      + [pl.BlockSpec(memory_space=pltpu.VMEM), _ANY] + [_ANY] * nn,
            scratch_shapes=[pltpu.VMEM((2, d, cb), BF16)] + _STREAM_SEMS + carried_sems),
        out_shape=[jax.ShapeDtypeStruct((t, N_DEV * cb), dt) for dt in out_dtypes]
                  + [jax.ShapeDtypeStruct((t, d), BF16), jax.ShapeDtypeStruct((N_DEV, d, cb), BF16)] + comm.new,
        input_output_aliases={4: no + 1} if has_land else {},
        compiler_params=_params("arbitrary", "arbitrary"),
    )(order, x, g, shard, *([land] if has_land else []), *comm.srcs)


def _down_call(act, shard, order, land, r0):
    t = act.shape[0]
    rb, d = shard.shape
    tm = _tile(t, 1024)
    ni = t // tm

    def body(order_ref, a_ref, shard_ref, land_in_ref, y_ref, land_ref, wbuf, *sems):
        p, i = pl.program_id(0), pl.program_id(1)
        gather = _StreamedGather(shard_ref, land_ref, wbuf, 0, r0, *sems)
        rows = pl.ds(pl.multiple_of(i * tm, tm), tm)
        _stream_steps(gather, p, i, ni)
        @pl.when(p == 0)
        def _():
            y_ref[rows, :] = _dot_nn(a_ref[...], wbuf[0])

        @pl.when(p > 0)
        def _():
            y_ref[rows, :] += _dot_nn(a_ref[...], wbuf[p % 2])

        @pl.when((p == N_DEV - 1) & (i == ni - 1))
        def _():
            gather.finish()

    return pl.pallas_call(
        body, name="down_fwd",
        grid_spec=pltpu.PrefetchScalarGridSpec(
            num_scalar_prefetch=1, grid=(N_DEV, ni),
            in_specs=[pl.BlockSpec((tm, rb), lambda p, i, o: (i, o[p])), _ANY, _ANY],
            out_specs=[pl.BlockSpec(memory_space=pltpu.VMEM), _ANY],
            scratch_shapes=[pltpu.VMEM((2, rb, d), BF16)] + _STREAM_SEMS),
        out_shape=[jax.ShapeDtypeStruct((t, d), F32), jax.ShapeDtypeStruct((N_DEV, rb, d), BF16)],
        input_output_aliases={3: 1},
        compiler_params=_params("arbitrary", "arbitrary"),
    )(order, act, shard, land)


def _loss_call(y, x2, target, g_final):
    t, d = y.shape
    tr = _tile(t, 512)

    def body(y_ref, x_ref, tg_ref, g_ref, loss_ref, dx_ref, dxb_ref, dg_ref):
        @pl.when(pl.program_id(0) == 0)
        def _():
            loss_ref[...] = jnp.zeros_like(loss_ref)
            dg_ref[...] = jnp.zeros_like(dg_ref)

        n, r = _rms_rows(x_ref[...] + y_ref[...])
        err = n * g_ref[...] - tg_ref[...]
        loss_ref[...] += 0.5 * jnp.sum(jnp.mean(err * err, axis=-1, keepdims=True))
        dy = err * (1.0 / d)
        dg_ref[...] += jnp.sum(dy * n, axis=0, keepdims=True)
        dx = _rms_bwd_rows(dy * g_ref[...], n, r)
        dx_ref[...] = dx
        dxb_ref[...] = dx.astype(BF16)

    return pl.pallas_call(
        body, name="loss_head",
        grid=(t // tr,),
        in_specs=[pl.BlockSpec((tr, d), lambda i: (i, 0)),
                  pl.BlockSpec((tr, d), lambda i: (i, 0)),
                  pl.BlockSpec((tr, d), lambda i: (i, 0)),
                  pl.BlockSpec((1, d), lambda i: (0, 0))],
        out_specs=[pl.BlockSpec((8, LANES), lambda i: (0, 0)),
                   pl.BlockSpec((tr, d), lambda i: (i, 0)),
                   pl.BlockSpec((tr, d), lambda i: (i, 0)),
                   pl.BlockSpec((1, d), lambda i: (0, 0))],
        out_shape=[jax.ShapeDtypeStruct((8, LANES), F32), jax.ShapeDtypeStruct((t, d), F32),
                   jax.ShapeDtypeStruct((t, d), BF16), jax.ShapeDtypeStruct((1, d), F32)],
        compiler_params=_params("arbitrary"),
    )(y, x2, target, g_final)


def _norm_bwd_call(name, dh, x, dres, g, want_bf16, comm=None):
    t, d = x.shape
    tr = _tile(t, 256)

    def body(dh_ref, x_ref, dres_ref, g_ref, dx_ref, *rest):
        dg_ref = rest[-1]

        @pl.when(pl.program_id(0) == 0)
        def _():
            dg_ref[...] = jnp.zeros_like(dg_ref)

        n, r = _rms_rows(x_ref[...])
        dh = dh_ref[...]
        dg_ref[...] += jnp.sum(dh * n, axis=0, keepdims=True)
        dx = dres_ref[...] + _rms_bwd_rows(dh * g_ref[...], n, r)
        dx_ref[...] = dx
        if want_bf16:
            rest[0][...] = dx.astype(BF16)

    row = pl.BlockSpec((tr, d), lambda i: (i, 0))
    vec = pl.BlockSpec((1, d), lambda i: (0, 0))
    out_specs = [row] + ([row] if want_bf16 else []) + [vec]
    out_shape = ([jax.ShapeDtypeStruct((t, d), F32)]
                 + ([jax.ShapeDtypeStruct((t, d), BF16)] if want_bf16 else [])
                 + [jax.ShapeDtypeStruct((1, d), F32)])
    return _carrier_call(
        body, (dh, x, dres, g), comm, name=name,
        grid=(t // tr,),
        in_specs=[row, row, row, vec],
        out_specs=out_specs, out_shape=out_shape,
        sem=("arbitrary",))


def _dact_call(dx3b, w_down, act, comm=None):
    t, d = dx3b.shape
    f = w_down.shape[0]
    tm = _tile(t, 1024)
    tn = _tile(f, 2048)

    def body(g_ref, w_ref, act_ref, o_ref):
        dact = _dot_nt(g_ref[...], w_ref[...])
        o_ref[...] = (dact * act_ref[...].astype(F32)).astype(BF16)

    return _carrier_call(
        body, (dx3b, w_down, act), comm, name="dact_bwd",
        grid=(t // tm, f // tn),
        in_specs=[pl.BlockSpec((tm, d), lambda i, j: (i, 0)),
                  pl.BlockSpec((tn, d), lambda i, j: (j, 0)),
                  pl.BlockSpec((tm, tn), lambda i, j: (i, j))],
        out_specs=pl.BlockSpec((tm, tn), lambda i, j: (i, j)),
        out_shape=jax.ShapeDtypeStruct((t, f), BF16),
        sem=("parallel", "parallel"))


def _wgrad_call(name, a, b, out_blocks, out_block_cols, comm=None, *, t1, t2=None, merge=1):
    t, k1 = a.shape
    k2 = b.shape[1]
    tt = _tile(t, 2048)
    t1 = _tile(k1, t1)
    t2 = _tile(k2, t2) if out_blocks is None else merge * out_block_cols
    nk = t // tt

    def body(a_ref, b_ref, o_ref, acc_ref):
        k = pl.program_id(2)

        @pl.when(k == 0)
        def _():
            acc_ref[...] = _dot_tn(a_ref[...], b_ref[...])

        @pl.when(k > 0)
        def _():
            acc_ref[...] += _dot_tn(a_ref[...], b_ref[...])

        @pl.when(k == nk - 1)
        def _():
            if out_blocks is None:
                o_ref[...] = acc_ref[...].astype(BF16)
            else:
                for blk in range(merge):
                    o_ref[blk] = acc_ref[:, blk * out_block_cols:(blk + 1) * out_block_cols].astype(BF16)

    if out_blocks is None:
        out_spec = pl.BlockSpec((t1, t2), lambda i, j, k: (i, j))
        out_shape = jax.ShapeDtypeStruct((k1, k2), BF16)
    else:
        out_spec = pl.BlockSpec((merge, t1, out_block_cols), lambda i, j, k: (j, i, 0))
        out_shape = jax.ShapeDtypeStruct((out_blocks, k1, out_block_cols), BF16)
    return _carrier_call(
        body, (a, b), comm, name=name,
        grid=(k1 // t1, k2 // t2, nk),
        in_specs=[pl.BlockSpec((tt, t1), lambda i, j, k: (k, i)),
                  pl.BlockSpec((tt, t2), lambda i, j, k: (k, j))],
        out_specs=out_spec, out_shape=out_shape,
        scratch_shapes=[pltpu.VMEM((t1, t2), F32)],
        sem=("parallel", "parallel", "arbitrary"))


def _dgrad_blocked_call(name, g, w_g, comm=None, *, merge=1):
    t = g.shape[0]
    nb, d, cb = w_g.shape
    tm = _tile(t, 1024)
    tn = _tile(d, 2048)
    tk = merge * cb

    def body(g_ref, w_ref, o_ref):
        def product():
            w = w_ref[0] if merge == 1 else jnp.concatenate([w_ref[b] for b in range(merge)], axis=1)
            return _dot_nt(g_ref[...], w)

        @pl.when(pl.program_id(2) == 0)
        def _():
            o_ref[...] = product()

        @pl.when(pl.program_id(2) > 0)
        def _():
            o_ref[...] += product()

    return _carrier_call(
        body, (g, w_g), comm, name=name,
        grid=(t // tm, d // tn, nb // merge),
        in_specs=[pl.BlockSpec((tm, tk), lambda i, j, k: (i, k)),
                  pl.BlockSpec((merge, tn, cb), lambda i, j, k: (k, j, 0))],
        out_specs=pl.BlockSpec((tm, tn), lambda i, j, k: (i, j)),
        out_shape=jax.ShapeDtypeStruct((t, d), F32),
        sem=("parallel", "parallel", "arbitrary"))


def _dmixed_call(dx2b, w_out, comm=None):
    t, d = dx2b.shape
    e = w_out.shape[0]
    tm = _tile(t, 1024)
    tn = _tile(e, 1024)

    def body(g_ref, w_ref, o_ref):
        o_ref[...] = _dot_nt(g_ref[...], w_ref[...])

    return _carrier_call(
        body, (dx2b, w_out), comm, name="dmixed_bwd",
        grid=(t // tm, e // tn),
        in_specs=[pl.BlockSpec((tm, d), lambda i, j: (i, 0)),
                  pl.BlockSpec((tn, d), lambda i, j: (j, 0))],
        out_specs=pl.BlockSpec((tm, tn), lambda i, j: (i, j)),
        out_shape=jax.ShapeDtypeStruct((t, e), F32),
        sem=("parallel", "parallel"))


def _mixer_bwd_call(proj, dmixed, w_s, bs_t, g_v, w_pool_g, pool_scale, comm=None):
    t = proj.shape[0]
    tt = _tile(t, 512)
    nchunk = tt // CHUNK
    hb = tt // HALO
    last_halo = t // HALO - 1
    nsteps = t // tt
    rb = GROUP // N_DEV

    def body(pu_ref, pv_ref, z_ref, zp_ref, da_ref, db_ref, dbn_ref, ws_ref, bs_ref, gv_ref, wp_ref, ps_ref,
             dproj_ref, dws_ref, dbs_ref, dgv_ref, dps_ref, dwp_ref):
        i = pl.program_id(0)

        @pl.when(i == 0)
        def _():
            dws_ref[...] = jnp.zeros_like(dws_ref)
            dbs_ref[...] = jnp.zeros_like(dbs_ref)
            dgv_ref[...] = jnp.zeros_like(dgv_ref)
            dps_ref[...] = jnp.zeros_like(dps_ref)
            dwp_ref[...] = jnp.zeros_like(dwp_ref)

        tril = (lax.broadcasted_iota(jnp.int32, (CHUNK, CHUNK), 0)
                >= lax.broadcasted_iota(jnp.int32, (CHUNK, CHUNK), 1))
        for h in range(N_HEADS):
            cols = slice(h * CHUNK, (h + 1) * CHUNK)
            v, dv_dpv = _gelu_and_grad(pv_ref[:, cols])
            vhat, rv = _rms_rows(v)
            gv = gv_ref[:, cols]
            vn = (vhat * gv).astype(BF16)
            u, du_dpu = _gelu_and_grad(pu_ref[:, cols])
            w = jnp.where(tril, ws_ref[h], 0.0).astype(BF16)
            bcol = bs_ref[:, h:h + 1]
            dout = da_ref[:, cols]
            dmix = dout * u
            dmix_b = dmix.astype(BF16)
            dws = jnp.zeros((CHUNK, CHUNK), F32)
            dbs = jnp.zeros((CHUNK, 1), F32)
            dvn_parts = []
            du_parts = []
            for c in range(nchunk):
                rows = slice(c * CHUNK, (c + 1) * CHUNK)
                mixed = _dot_nn(w, vn[rows]) + bcol
                du_parts.append(dout[rows] * mixed)
                dvn_parts.append(_dot_tn(w, dmix_b[rows]))
                dws = dws + _dot_nt(dmix_b[rows], vn[rows])
                dbs = dbs + jnp.sum(dmix[rows], axis=1, keepdims=True)
            dws_ref[h] += jnp.where(tril, dws, 0.0)
            dbs_ref[:, h:h + 1] += dbs
            dvn = jnp.concatenate(dvn_parts, axis=0)
            du = jnp.concatenate(du_parts, axis=0)
            dgv_ref[:, cols] += jnp.sum(dvn * vhat, axis=0, keepdims=True)
            dv = _rms_bwd_rows(dvn * gv, vhat, rv)
            dproj_ref[:, cols] = (du * du_dpu).astype(BF16)
            dproj_ref[:, A_WIDTH + h * CHUNK:A_WIDTH + (h + 1) * CHUNK] = (dv * dv_dpv).astype(BF16)

        zprev = jnp.where(i > 0, zp_ref[...], 0.0)
        ext = jnp.concatenate([zprev, z_ref[...]], axis=0)
        dnext = jnp.where(i < nsteps - 1, dbn_ref[...], 0.0)
        dext = jnp.concatenate([db_ref[...], dnext], axis=0)
        for g, win in enumerate(POOL_WINDOWS):
            cols = slice(g * GROUP, (g + 1) * GROUP)
            zg = ext[:, cols]
            pooled = _window_sum_back(zg, win)[HALO:] / _pool_counts(i * tt, tt, win) - zg[HALO:]
            pooled_b = pooled.astype(BF16)
            wp = wp_ref[:, g].reshape(GROUP, GROUP)
            y = _dot_nn(pooled_b, wp)
            dout = dext[:, cols]
            dps_ref[:, cols] += jnp.sum(dout[:tt] * y, axis=0, keepdims=True)
            dy_b = (dout * ps_ref[:, cols]).astype(BF16)
            dwp_ref[:, g] += _dot_tn(pooled_b, dy_b[:tt]).reshape(N_DEV, rb, GROUP)
            dpooled = _dot_nt(dy_b, wp)
            q = dpooled / _pool_counts(i * tt, tt + HALO, win)
            dz = _window_sum_fwd(q, win)[:tt] - dpooled[:tt]
            dproj_ref[:, 2 * A_WIDTH + g * GROUP:2 * A_WIDTH + (g + 1) * GROUP] = dz.astype(BF16)

    def full(shape):
        return pl.BlockSpec(shape, lambda i: (0,) * len(shape))

    return _carrier_call(
        body, (proj, proj, proj, proj, dmixed, dmixed, dmixed, w_s, bs_t, g_v, w_pool_g, pool_scale), comm,
        name="mixer_bwd",
        grid=(nsteps,),
        in_specs=[pl.BlockSpec((tt, A_WIDTH), lambda i: (i, 0)),
                  pl.BlockSpec((tt, A_WIDTH), lambda i: (i, 1)),
                  pl.BlockSpec((tt, B_WIDTH), lambda i: (i, 2)),
                  pl.BlockSpec((HALO, B_WIDTH), lambda i: (jnp.maximum(i * hb - 1, 0), 2)),
                  pl.BlockSpec((tt, A_WIDTH), lambda i: (i, 0)),
                  pl.BlockSpec((tt, B_WIDTH), lambda i: (i, 1)),
                  pl.BlockSpec((HALO, B_WIDTH), lambda i: (jnp.minimum((i + 1) * hb, last_halo), 1)),
                  full((N_HEADS, CHUNK, CHUNK)), full((CHUNK, N_HEADS)), full((1, A_WIDTH)),
                  full((N_DEV, 4, rb, GROUP)), full((1, B_WIDTH))],
        out_specs=[pl.BlockSpec((tt, 2 * A_WIDTH + B_WIDTH), lambda i: (i, 0)),
                   full((N_HEADS, CHUNK, CHUNK)), full((CHUNK, N_HEADS)), full((1, A_WIDTH)),
                   full((1, B_WIDTH)), full((N_DEV, 4, rb, GROUP))],
        out_shape=[jax.ShapeDtypeStruct((t, 2 * A_WIDTH + B_WIDTH), BF16),
                   jax.ShapeDtypeStruct((N_HEADS, CHUNK, CHUNK), F32),
                   jax.ShapeDtypeStruct((CHUNK, N_HEADS), F32),
                   jax.ShapeDtypeStruct((1, A_WIDTH), F32),
                   jax.ShapeDtypeStruct((1, B_WIDTH), F32),
                   jax.ShapeDtypeStruct((N_DEV, 4, rb, GROUP), F32)],
        sem=("arbitrary",))


def _adamw(w, g, m, v):
    m = ADAM_B1 * m + (1.0 - ADAM_B1) * g
    v = ADAM_B2 * v + (1.0 - ADAM_B2) * (g * g)
    m_hat = m / ADAM_C1
    v_hat = v / ADAM_C2
    delta = -ADAM_LR * (m_hat / (jnp.sqrt(v_hat) + ADAM_EPS) + ADAM_WD * w)
    return delta, m, v


PAIR_SUM_TILE_ELEMS = 1024 * 1024
ADAMW_TILE_ELEMS = 512 * 1024


def _row_tile(r, c, elems):
    t = r
    while t * c > elems and t % 32 == 0:
        t //= 2
    return t


def _pair_sum_call(name, pos, grad, got):
    _, r, c = grad.shape
    tr = _row_tile(r, c, PAIR_SUM_TILE_ELEMS)

    def chip_of(rel, pos_ref):
        px = jnp.where((rel == 0) | (rel == 2), 1 - pos_ref[0], pos_ref[0])
        py = jnp.where((rel == 1) | (rel == 2), 1 - pos_ref[1], pos_ref[1])
        return 2 * px + py

    def body(pos_ref, own_ref, got_ref, out_ref):
        out_ref[...] = (own_ref[...].astype(F32) + got_ref[...].astype(F32)).astype(BF16)

    return pl.pallas_call(
        body, name=name,
        grid_spec=pltpu.PrefetchScalarGridSpec(
            num_scalar_prefetch=1, grid=(3, r // tr),
            in_specs=[pl.BlockSpec((None, tr, c), lambda k, i, p: (2 * chip_of(k, p) + p[2], i, 0)),
                      pl.BlockSpec((None, tr, c), lambda k, i, p: (chip_of(k, p), i, 0))],
            out_specs=pl.BlockSpec((None, tr, c), lambda k, i, p: (k, i, 0))),
        out_shape=jax.ShapeDtypeStruct((3, r, c), BF16),
        compiler_params=_params("parallel", "parallel"),
    )(pos, grad, got)


def _final_call(name, pos, grad, got_pair, got_chips, w, m, v):
    _, r, c = grad.shape
    tr = _row_tile(r, c, ADAMW_TILE_ELEMS)

    def body(pos_ref, own_ref, pair_ref, chips_ref, w_ref, m_ref, v_ref, g_out, d_out, m_out, v_out):
        g = own_ref[...].astype(F32) + pair_ref[...].astype(F32)
        for j in range(3):
            g = g + chips_ref[j].astype(F32)
        delta, m_new, v_new = _adamw(w_ref[...], g, m_ref[...], v_ref[...])
        g_out[...] = g
        d_out[...] = delta
        m_out[...] = m_new
        v_out[...] = v_new

    row = pl.BlockSpec((tr, c), lambda i, p: (i, 0))
    return pl.pallas_call(
        body, name=name,
        grid_spec=pltpu.PrefetchScalarGridSpec(
            num_scalar_prefetch=1, grid=(r // tr,),
            in_specs=[pl.BlockSpec((None, tr, c), lambda i, p: (4 * p[0] + 2 * p[1] + p[2], i, 0)),
                      pl.BlockSpec((None, tr, c), lambda i, p: (2 * p[0] + p[1], i, 0)),
                      pl.BlockSpec((3, tr, c), lambda i, p: (0, i, 0)), row, row, row],
            out_specs=[row] * 4),
        out_shape=[jax.ShapeDtypeStruct((r, c), F32)] * 4,
        compiler_params=_params("parallel"),
    )(pos, grad, got_pair, got_chips, w, m, v)


def _small_final_call(name, parts, w, m, v):
    _, rows, c = parts.shape
    r = w.shape[0]

    def body(p_ref, w_ref, m_ref, v_ref, g_out, d_out, m_out, v_out):
        g = p_ref[0]
        for k in range(1, N_DEV):
            g = g + p_ref[k]
        delta, m_new, v_new = _adamw(w_ref[...], g[:r], m_ref[...], v_ref[...])
        g_out[...] = g
        d_out[...] = delta
        m_out[...] = m_new
        v_out[...] = v_new

    return pl.pallas_call(
        body, name=name,
        out_shape=[jax.ShapeDtypeStruct((rows, c), F32)] + [jax.ShapeDtypeStruct((r, c), F32)] * 3,
        compiler_params=pltpu.CompilerParams(vmem_limit_bytes=VMEM_LIMIT),
    )(parts, w, m, v)


_SMALL_EARLY = ("g_v", "w_s", "b_s", "pool_scale", "g_ffn", "g_final")
_BIG = ("w_in", "w_pool", "w_out", "w_up", "w_down")
_ORDER = ("g_mix", "w_in", "g_v", "w_s", "b_s", "w_pool", "pool_scale", "w_out", "g_ffn", "w_up", "w_down", "g_final")


def _pack(parts):
    return jnp.concatenate([p.reshape(-1, LANES) for p in parts], axis=0)


def _unpack(packed, like):
    out, row = [], 0
    for a in like:
        rows = a.size // LANES
        out.append(packed[row:row + rows].reshape(a.shape))
        row += rows
    return out


def kernel(x, g_mix, w_in, g_v, w_s, b_s, w_pool, pool_scale, w_out, g_ffn, w_up, w_down, g_final, loss_target, m_g_mix, m_w_in, m_g_v, m_w_s, m_b_s, m_w_pool, m_pool_scale, m_w_out, m_g_ffn, m_w_up, m_w_down, m_g_final, v_g_mix, v_w_in, v_g_v, v_w_s, v_b_s, v_w_pool, v_pool_scale, v_w_out, v_g_ffn, v_w_up, v_w_down, v_g_final):
    weights = dict(g_mix=g_mix, w_in=w_in, g_v=g_v, w_s=w_s, b_s=b_s, w_pool=w_pool, pool_scale=pool_scale,
                   w_out=w_out, g_ffn=g_ffn, w_up=w_up, w_down=w_down, g_final=g_final)
    mom = dict(g_mix=m_g_mix, w_in=m_w_in, g_v=m_g_v, w_s=m_w_s, b_s=m_b_s, w_pool=m_w_pool,
               pool_scale=m_pool_scale, w_out=m_w_out, g_ffn=m_g_ffn, w_up=m_w_up, w_down=m_w_down,
               g_final=m_g_final)
    var = dict(g_mix=v_g_mix, w_in=v_w_in, g_v=v_g_v, w_s=v_w_s, b_s=v_b_s, w_pool=v_w_pool,
               pool_scale=v_pool_scale, w_out=v_w_out, g_ffn=v_g_ffn, w_up=v_w_up, w_down=v_w_down,
               g_final=v_g_final)

    t, d = x.shape[1], x.shape[2]
    xs = x.reshape(t, d)
    target = loss_target.reshape(t, d)

    shard2d = dict(w_in=w_in.reshape(d, -1), w_pool=w_pool.reshape(-1, GROUP), w_out=w_out.reshape(-1, d),
                   w_up=w_up.reshape(d, -1), w_down=w_down.reshape(-1, d))
    sb = {k: shard2d[k].astype(BF16) for k in _BIG}
    rows = {k: sb[k].shape[0] for k in _BIG}

    def gathered_shape(k):
        return jax.ShapeDtypeStruct((N_DEV,) + sb[k].shape, BF16)

    def landing(n, like):
        return jax.ShapeDtypeStruct((n,) + like.shape[1:], like.dtype)

    def from_everyone(block):
        return jax.ShapeDtypeStruct((N_DEV,) + block.shape, block.dtype)

    def cuts(r, fractions):
        return [0] + [int(r * f) // 16 * 16 for f in fractions] + [r]

    g_mix2, g_ffn2, g_final2 = g_mix.reshape(1, d), g_ffn.reshape(1, d), g_final.reshape(1, d)
    g_v2, ps2 = g_v.reshape(1, A_WIDTH), pool_scale.reshape(1, B_WIDTH)
    w_s3 = w_s.reshape(N_HEADS, CHUNK, CHUNK)
    bs_t = b_s.reshape(N_HEADS, CHUNK).T
    xi, yi, ci = _position()
    pos = jnp.stack([xi, yi, ci]).astype(jnp.int32)

    order = (4 * xi + 2 * yi + ci) ^ jnp.array(ARRIVAL_ORDER, jnp.int32)
    u = cuts(rows["w_up"], (0.25, 0.55))
    ahead = cuts(rows["w_down"], (0.6,))[1]
    proj, h1, w_in_g, w_out_g, w_pool_g = _norm_matmul_stream_call(
        "proj_fwd", xs, g_mix2, sb["w_in"], order, None, 0, 0, _Comm(
            [sb["w_out"], sb["w_pool"]], [], [gathered_shape("w_out"), gathered_shape("w_pool")],
            lambda s, l: [_gather_to_neighbours(s[0], l[0], 0, rows["w_out"]) + _everyone(s[1], l[1])]),
        lambda a: (a,), (F32,))
    w_pool_g = w_pool_g.reshape(N_DEV, 4, GROUP // N_DEV, GROUP)
    mixed, w_out_g, w_up_g = _mixer_fwd_call(proj, w_s3, bs_t, g_v2, w_pool_g, ps2, _Comm(
        [sb["w_up"]], [w_out_g], [gathered_shape("w_up")],
        lambda s, l: [_gather_relay(l[0], 0, rows["w_out"]) + _gather_to_neighbours(s[0], l[1], u[0], u[1]),
                      _gather_diagonal_pass_on(l[0], 0, rows["w_out"])]))
    w_out_f = w_out_g.reshape(-1, d)
    x2, w_up_g = _out_proj_call(mixed, w_out_f, xs, _Comm(
        [sb["w_up"]], [w_up_g], [],
        lambda s, l: [_gather_relay(l[0], u[0], u[1]) + _gather_to_neighbours(s[0], l[0], u[1], u[2]),
                      _gather_diagonal_pass_on(l[0], u[0], u[1])]))

    def relu2_and_slope(a):
        r = jnp.maximum(a, 0.0)
        return r * r, 2.0 * r

    act, dact_da, h2, w_up_g, w_down_g = _norm_matmul_stream_call(
        "up_fwd", x2, g_ffn2, sb["w_up"], order, w_up_g, u[1], u[2], _Comm(
            [sb["w_down"]], [], [gathered_shape("w_down")],
            lambda s, l: [_gather_to_neighbours(s[0], l[0], 0, ahead)]),
        relu2_and_slope, (BF16, BF16))
    y, w_down_g = _down_call(act, sb["w_down"], order, w_down_g, ahead)
    w_down_f = w_down_g.reshape(-1, d)
    loss_part, dx3, dx3b, dg_final = _loss_call(y, x2, target, g_final2)

    def pair_sum(k, grad, got):
        return _pair_sum_call(k + "_pair_sum", pos, grad, got)

    def finish(k, grad, got_pair, got_chips):
        s = shard2d[k]
        outs = _final_call(k + "_adamw", pos, grad, got_pair, got_chips, s, mom[k].reshape(s.shape),
                           var[k].reshape(s.shape))
        return [o.reshape(weights[k].shape) for o in outs]

    result = {}
    (gw_down,) = _wgrad_call("w_down_grad", act, dx3b, None, None, t1=1024, t2=2048)
    gw_down = gw_down.reshape(N_DEV, -1, d)
    da, pair_down = _dact_call(dx3b, w_down_f, dact_da, _Comm(
        [gw_down], [], [landing(4, gw_down)], lambda s, l: [_pair_exchange(s[0], l[0])]))
    sums_down = pair_sum("w_down", gw_down, pair_down)
    dn = cuts(rows["w_down"], (0.75,))
    gw_up, got = _wgrad_call("w_up_grad", h2, da, N_DEV, w_up_g.shape[2], _Comm(
        [sums_down], [], [landing(3, sums_down)],
        lambda s, l: [_chip_exchange(s[0], l[0], dn[0], dn[1])]), t1=2048)
    dh2, got, pair_up = _dgrad_blocked_call("dh2_bwd", da, w_up_g, _Comm(
        [sums_down, gw_up], [got], [landing(4, gw_up)],
        lambda s, l: [_chip_exchange(s[0], l[0], dn[1], dn[2]) + _pair_exchange(s[1], l[1])]), merge=2)
    result["w_down"] = finish("w_down", gw_down, pair_down, got)
    sums_up = pair_sum("w_up", gw_up, pair_up)
    v = cuts(rows["w_up"], (0.26, 0.47, 0.69))
    dx2, dx2b, dg_ffn, got_up = _norm_bwd_call("ffn_norm_bwd", dh2, x2, dx3, g_ffn2, True, _Comm(
        [sums_up], [], [landing(3, sums_up)], lambda s, l: [_chip_exchange(s[0], l[0], v[0], v[1])]))
    dmixed, got_up = _dmixed_call(dx2b, w_out_f, _Comm(
        [sums_up], [got_up], [], lambda s, l: [_chip_exchange(s[0], l[0], v[1], v[2])]))
    gw_out, got_up = _wgrad_call("w_out_grad", mixed, dx2b, None, None, _Comm(
        [sums_up], [got_up], [], lambda s, l: [_chip_exchange(s[0], l[0], v[2], v[3])]), t1=2048, t2=1024)
    gw_out = gw_out.reshape(N_DEV, -1, d)
    dproj, dw_s, dbs_t, dg_v, dps, dw_pool, got_up, pair_out = _mixer_bwd_call(
        proj, dmixed, w_s3, bs_t, g_v2, w_pool_g, ps2, _Comm(
            [sums_up, gw_out], [got_up], [landing(4, gw_out)],
            lambda s, l: [_chip_exchange(s[0], l[0], v[3], v[4]) + _pair_exchange(s[1], l[1])]))
    result["w_up"] = finish("w_up", gw_up, pair_up, got_up)
    sums_out = pair_sum("w_out", gw_out, pair_out)
    gw_pool = dw_pool.astype(BF16).reshape(N_DEV, -1, GROUP)
    early = dict(g_v=dg_v, w_s=dw_s, b_s=dbs_t.T, pool_scale=dps, g_ffn=dg_ffn, g_final=dg_final)
    packed = _pack([early[k] for k in _SMALL_EARLY] + [loss_part])
    early_rows = packed.shape[0]
    gw_in, got, pair_pool, parts_early = _wgrad_call("w_in_grad", h1, dproj, N_DEV, w_in_g.shape[2], _Comm(
        [sums_out, gw_pool, packed], [], [landing(3, sums_out), landing(4, gw_pool), from_everyone(packed)],
        lambda s, l: [_chip_exchange(s[0], l[0], 0, rows["w_out"]) + _pair_exchange(s[1], l[1])
                      + _gather_first(s[2], l[2], 0, early_rows)]), t1=2048, merge=MERGE_W_IN)
    send_sems, recv_sems, gw_in, land_in, token = _pair_exchange_start_call("pair_exchange_w_in_start", gw_in)
    result["w_out"] = finish("w_out", gw_out, pair_out, got)
    sums_pool = pair_sum("w_pool", gw_pool + token[0, 0], pair_pool)
    gw_in, pair_in = _pair_exchange_wait_call("pair_exchange_w_in_wait", send_sems, recv_sems, gw_in, land_in,
                                              result["w_out"][0])
    sums_in = pair_sum("w_in", gw_in, pair_in)
    dh1, parts_early, got, got_pool = _dgrad_blocked_call("dh1_bwd", dproj, w_in_g, _Comm(
        [sums_in, sums_pool], [parts_early], [landing(3, sums_in), landing(3, sums_pool)],
        lambda s, l: [_chip_exchange(s[0], l[1], 0, rows["w_in"]) + _chip_exchange(s[1], l[2], 0, rows["w_pool"])
                      + _gather_pass_on(l[0], 0, early_rows)]),
        merge=MERGE_W_IN)
    result["w_in"] = finish("w_in", gw_in, pair_in, got)
    result["w_pool"] = finish("w_pool", gw_pool, pair_pool, got_pool)
    grad_x, dg_mix = _norm_bwd_call("mix_norm_bwd", dh1, xs, dx2, g_mix2, False)
    packed = _pack([dg_mix])
    (parts_late,) = _comm_call("gather_g_mix_grad", _Comm(
        [packed], [], [from_everyone(packed)], lambda s, l: [_everyone(s[0], l[0])]))

    for names, parts, tag in ((_SMALL_EARLY, parts_early, "small_adamw"), (("g_mix",), parts_late, "g_mix_adamw")):
        outs = _small_final_call(tag, parts, _pack([weights[k] for k in names]), _pack([mom[k] for k in names]),
                                 _pack([var[k] for k in names]))
        if tag == "small_adamw":
            loss = outs[0][-1, 0]
        like = [weights[k] for k in names]
        unpacked = [_unpack(o, like) for o in outs]
        for idx, k in enumerate(names):
            result[k] = [unpacked[q][idx] for q in range(4)]

    grads = [result[k][0] for k in _ORDER]
    deltas = [result[k][1] for k in _ORDER]
    new_m = [result[k][2] for k in _ORDER]
    new_v = [result[k][3] for k in _ORDER]
    return (loss, grad_x.reshape(x.shape), *grads, *deltas, *new_m, *new_v)
```

```python
import functools
import math

import jax
import jax.numpy as jnp
from jax import lax
from jax.experimental import pallas as pl
from jax.experimental.pallas import tpu as pltpu

F32 = jnp.float32
BF16 = jnp.bfloat16
MESH = pl.DeviceIdType.MESH

N_DEV = 8
EPS = 1e-6
CHUNK = 128
N_HEADS = 8
A_WIDTH = 1024
B_WIDTH = 1024
POOL_WINDOWS = (2, 4, 8, 16)
GROUP = 256
HALO = 16
LANES = 128

ADAM_LR = 0.001
ADAM_B1 = 0.9
ADAM_B2 = 0.999
ADAM_EPS = 1e-08
ADAM_WD = 0.01
ADAM_STEP = 10
ADAM_C1 = 1.0 - ADAM_B1 ** ADAM_STEP
ADAM_C2 = 1.0 - ADAM_B2 ** ADAM_STEP

VMEM_LIMIT = 56 * 1024 * 1024
MERGE_W_IN = 2

_GELU_C = math.sqrt(2.0 / math.pi)


def _params(*sem):
    return pltpu.CompilerParams(dimension_semantics=sem, vmem_limit_bytes=VMEM_LIMIT)


def _gelu(x):
    return 0.5 * x * (1.0 + jnp.tanh(_GELU_C * (x + 0.044715 * x * x * x)))


def _gelu_and_grad(x):
    t = jnp.tanh(_GELU_C * (x + 0.044715 * x * x * x))
    g = 0.5 * x * (1.0 + t)
    dg = 0.5 * (1.0 + t) + 0.5 * x * (1.0 - t * t) * (_GELU_C * (1.0 + 3.0 * 0.044715 * x * x))
    return g, dg


def _dot_nn(a, b):
    return lax.dot_general(a, b, (((1,), (0,)), ((), ())), preferred_element_type=F32)


def _dot_nt(a, b):
    return lax.dot_general(a, b, (((1,), (1,)), ((), ())), preferred_element_type=F32)


def _dot_tn(a, b):
    return lax.dot_general(a, b, (((0,), (0,)), ((), ())), preferred_element_type=F32)


def _rms_rows(x):
    r = lax.rsqrt(jnp.mean(x * x, axis=-1, keepdims=True) + EPS)
    return x * r, r


def _rms_bwd_rows(dn, n, r):
    return r * (dn - n * jnp.mean(dn * n, axis=-1, keepdims=True))


def _tile(n, want):
    t = min(n, want)
    assert n % t == 0, (n, want)
    return t


_ANY = pl.BlockSpec(memory_space=pl.ANY)

SIBLING = 1
CHIPS = (4, 2, 6)


def _position():
    return lax.axis_index("x"), lax.axis_index("y"), lax.axis_index("c")


def _me():
    x, y, c = _position()
    return 4 * x + 2 * y + c


def _peer(rel):
    x, y, c = _position()
    return (x ^ ((rel >> 2) & 1), y ^ ((rel >> 1) & 1), c ^ (rel & 1))


class _Comm:
    def __init__(self, srcs, lands, new, plan):
        self.srcs, self.lands, self.new, self.plan = list(srcs), list(lands), list(new), plan


def _make_copies(phases, send_sems, recv_sems, local_sems):
    out, nr, nl = [], 0, 0
    for phase in phases:
        cps = []
        for item in phase:
            if item[0] == "local":
                cps.append(pltpu.make_async_copy(item[1], item[2], local_sems.at[nl]))
                nl += 1
            else:
                cps.append(pltpu.make_async_remote_copy(
                    src_ref=item[1], dst_ref=item[2], send_sem=send_sems.at[nr], recv_sem=recv_sems.at[nr],
                    device_id=_peer(item[3]), device_id_type=MESH))
                nr += 1
        out.append(cps)
    return out


def _count_copies(comm):
    phases = comm.plan([_FakeRef() for _ in comm.srcs], [_FakeRef() for _ in range(len(comm.lands) + len(comm.new))])
    items = [it for ph in phases for it in ph]
    return sum(it[0] == "remote" for it in items), sum(it[0] == "local" for it in items)


class _FakeRef:
    def __getitem__(self, idx):
        return self

    @property
    def at(self):
        return self


def _carrier_call(body, args, comm, *, name, grid, in_specs, out_specs, out_shape, scratch_shapes=(), sem):
    if not isinstance(out_shape, (list, tuple)):
        out_specs, out_shape = [out_specs], [out_shape]
    out_specs, out_shape, scratch_shapes = list(out_specs), list(out_shape), list(scratch_shapes)
    if comm is None:
        res = pl.pallas_call(body, name=name, grid=grid, in_specs=list(in_specs), out_specs=out_specs,
                             out_shape=out_shape, scratch_shapes=scratch_shapes, compiler_params=_params(*sem))(*args)
        return list(res)
    n_in, n_out, n_scr = len(args), len(out_shape), len(scratch_shapes)
    ns, nl, nn = len(comm.srcs), len(comm.lands), len(comm.new)
    n_remote, n_local = _count_copies(comm)
    steps = math.prod(grid)

    def wrapped(*refs):
        ins, srcs = refs[:n_in], refs[n_in:n_in + ns]
        o = n_in + ns + nl
        outs, lands = refs[o:o + n_out], refs[o + n_out:o + n_out + nl + nn]
        scr = refs[o + n_out + nl + nn:]
        phases = _make_copies(comm.plan(srcs, lands), *scr[n_scr:])
        assert len(phases) == 1 or (len(phases) == 2 and steps >= 3)
        step = functools.reduce(lambda acc, a: acc * grid[a] + pl.program_id(a), range(len(grid)), 0)

        @pl.when(step == 0)
        def _():
            for cp in phases[0]:
                cp.start()

        if len(phases) == 2:
            @pl.when(step == steps * 3 // 4)
            def _():
                for cp in phases[0]:
                    cp.wait()
                for cp in phases[1]:
                    cp.start()

        body(*ins, *outs, *scr[:n_scr])

        @pl.when(step == steps - 1)
        def _():
            for cp in phases[-1]:
                cp.wait()

    land_shapes = [jax.ShapeDtypeStruct(a.shape, a.dtype) for a in comm.lands] + comm.new
    sems = [pltpu.SemaphoreType.DMA((max(n_remote, 1),)), pltpu.SemaphoreType.DMA((max(n_remote, 1),)),
            pltpu.SemaphoreType.DMA((max(n_local, 1),))]
    res = pl.pallas_call(
        wrapped, name=name, grid=grid,
        in_specs=list(in_specs) + [_ANY] * (ns + nl), out_specs=out_specs + [_ANY] * (nl + nn),
        out_shape=out_shape + land_shapes, scratch_shapes=scratch_shapes + sems,
        input_output_aliases={n_in + ns + k: n_out + k for k in range(nl)},
        compiler_params=_params(*sem))(*args, *comm.srcs, *comm.lands)
    return list(res)


def _comm_call(name, comm):
    ns, nl, nn = len(comm.srcs), len(comm.lands), len(comm.new)
    n_remote, n_local = _count_copies(comm)

    def body(*refs):
        srcs, lands, sems = refs[:ns], refs[ns + nl:ns + nl + nl + nn], refs[ns + nl + nl + nn:]
        for copies in _make_copies(comm.plan(srcs, lands), *sems):
            for cp in copies:
                cp.start()
            for cp in copies:
                cp.wait()

    land_shapes = [jax.ShapeDtypeStruct(a.shape, a.dtype) for a in comm.lands] + comm.new
    res = pl.pallas_call(
        body, name=name,
        in_specs=[_ANY] * (ns + nl), out_specs=[_ANY] * (nl + nn), out_shape=land_shapes,
        scratch_shapes=[pltpu.SemaphoreType.DMA((max(n_remote, 1),)), pltpu.SemaphoreType.DMA((max(n_remote, 1),)),
                        pltpu.SemaphoreType.DMA((max(n_local, 1),))],
        input_output_aliases={ns + k: k for k in range(nl)},
    )(*comm.srcs, *comm.lands)
    return list(res)


_HBM = pl.BlockSpec(memory_space=pltpu.HBM)
_SEM = pl.BlockSpec(memory_space=pltpu.SEMAPHORE)


def _pair_exchange_copies(grad_ref, land_ref, send_sems, recv_sems):
    _, _, c = _position()
    return [pltpu.make_async_remote_copy(
        src_ref=grad_ref.at[2 * chip + (1 - c)], dst_ref=land_ref.at[chip], send_sem=send_sems.at[chip],
        recv_sem=recv_sems.at[chip], device_id=_peer(SIBLING), device_id_type=MESH) for chip in range(4)]


def _chip_exchange_copies(sums_ref, land_ref, send_sems, recv_sems):
    return [pltpu.make_async_remote_copy(
        src_ref=sums_ref.at[j], dst_ref=land_ref.at[j], send_sem=send_sems.at[j], recv_sem=recv_sems.at[j],
        device_id=_peer(rel), device_id_type=MESH) for j, rel in enumerate(CHIPS)]


def _exchange_start_call(name, grad, slots, make_copies):
    def body(grad_ref, land_ref, send_sems, recv_sems, grad_thru, land_thru, token):
        for cp in make_copies(grad_ref, land_ref, send_sems, recv_sems):
            cp.start()
        token[...] = jnp.zeros_like(token)

    land = lax.empty((slots,) + grad.shape[1:], grad.dtype)
    return pl.pallas_call(
        body, name=name,
        out_shape=(pltpu.SemaphoreType.DMA((4,)), pltpu.SemaphoreType.DMA((4,)), pltpu.HBM(grad.shape, grad.dtype),
                   pltpu.HBM(land.shape, land.dtype), jax.ShapeDtypeStruct((8, LANES), grad.dtype)),
        in_specs=(_HBM, _HBM), out_specs=(_SEM, _SEM, _HBM, _HBM, pl.BlockSpec(memory_space=pltpu.VMEM)),
        input_output_aliases={0: 2, 1: 3},
        compiler_params=pltpu.CompilerParams(has_side_effects=pltpu.SideEffectType.DATAFLOW_SIDE_EFFECTING),
    )(pltpu.with_memory_space_constraint(grad, pltpu.HBM), pltpu.with_memory_space_constraint(land, pltpu.HBM))


def _exchange_wait_call(name, send_sems, recv_sems, grad, land, after, make_copies):
    def body(grad_ref, land_ref, send_sems, recv_sems, after_ref, grad_out, land_out):
        for cp in make_copies(grad_ref, land_ref, send_sems, recv_sems):
            cp.wait_send()
            cp.wait_recv()

    return pl.pallas_call(
        body, name=name,
        out_shape=(pltpu.HBM(grad.shape, grad.dtype), pltpu.HBM(land.shape, land.dtype)),
        in_specs=(_HBM, _HBM, _SEM, _SEM, _ANY), out_specs=(_HBM, _HBM), input_output_aliases={0: 0, 1: 1},
        compiler_params=pltpu.CompilerParams(has_side_effects=pltpu.SideEffectType.DATAFLOW_SIDE_EFFECTING),
    )(grad, land, send_sems, recv_sems, after)


def _rows(ref, block, r0, r1):
    return ref.at[block, pl.ds(r0, r1 - r0)]


def _gather_first(shard, land, r0, r1):
    src = shard.at[pl.ds(r0, r1 - r0)]
    dst = _rows(land, _me(), r0, r1)
    return [("local", src, dst)] + [("remote", src, dst, rel) for rel in (SIBLING,) + CHIPS]


def _gather_pass_on(land, r0, r1):
    return [("remote", _rows(land, _me() ^ rel, r0, r1), _rows(land, _me() ^ rel, r0, r1), SIBLING) for rel in CHIPS]


def _split_rows(r0, r1):
    m = (r0 + r1) // 2 // 16 * 16
    return (r0, m), (m, r1)


def _gather_to_neighbours(shard, land, r0, r1):
    src = shard.at[pl.ds(r0, r1 - r0)]
    dst = _rows(land, _me(), r0, r1)
    return [("local", src, dst)] + [("remote", src, dst, rel) for rel in (SIBLING, 4, 2)]


def _gather_relay(land, r0, r1):
    lo, hi = _split_rows(r0, r1)
    x_block, y_block = _me() ^ 4, _me() ^ 2
    return [("remote", _rows(land, x_block, *lo), _rows(land, x_block, *lo), 2),
            ("remote", _rows(land, y_block, *hi), _rows(land, y_block, *hi), 4),
            ("remote", _rows(land, x_block, r0, r1), _rows(land, x_block, r0, r1), SIBLING),
            ("remote", _rows(land, y_block, r0, r1), _rows(land, y_block, r0, r1), SIBLING)]


def _gather_diagonal_pass_on(land, r0, r1):
    rows = _rows(land, _me() ^ 6, r0, r1)
    return [("remote", rows, rows, SIBLING)]


def _pair_exchange(grad, land):
    _, _, c = _position()
    return [("remote", grad.at[2 * chip + (1 - c)], land.at[chip], SIBLING) for chip in range(4)]


def _chip_exchange(sums, land, r0, r1):
    return [("remote", _rows(sums, j, r0, r1), _rows(land, j, r0, r1), rel) for j, rel in enumerate(CHIPS)]


def _everyone(packed, land):
    dst = land.at[_me()]
    return [("local", packed, dst)] + [("remote", packed, dst, rel) for rel in range(1, N_DEV)]


def _pool_counts(row0, rows, win):
    pos = row0 + lax.broadcasted_iota(jnp.int32, (rows, 1), 0)
    return jnp.minimum(pos + 1, win).astype(F32)


def _window_sum_back(ext, win):
    s = ext
    k = 1
    while k < win:
        s = s + pltpu.roll(s, k, 0)
        k *= 2
    return s


def _window_sum_fwd(ext, win):
    n = ext.shape[0]
    s = ext
    k = 1
    while k < win:
        s = s + pltpu.roll(s, n - k, 0)
        k *= 2
    return s


def _mixer_fwd_call(proj, w_s, bs_t, g_v, w_pool_g, pool_scale, comm=None):
    t = proj.shape[0]
    tt = _tile(t, 512)
    nchunk = tt // CHUNK
    hb = tt // HALO

    def body(pu_ref, pv_ref, z_ref, zp_ref, ws_ref, bs_ref, gv_ref, wp_ref, ps_ref, out_ref):
        i = pl.program_id(0)
        tril = (lax.broadcasted_iota(jnp.int32, (CHUNK, CHUNK), 0)
                >= lax.broadcasted_iota(jnp.int32, (CHUNK, CHUNK), 1))
        for h in range(N_HEADS):
            cols = slice(h * CHUNK, (h + 1) * CHUNK)
            vhat, _ = _rms_rows(_gelu(pv_ref[:, cols]))
            vn = (vhat * gv_ref[:, cols]).astype(BF16)
            u = _gelu(pu_ref[:, cols])
            w = jnp.where(tril, ws_ref[h], 0.0).astype(BF16)
            bcol = bs_ref[:, h:h + 1]
            for c in range(nchunk):
                rows = slice(c * CHUNK, (c + 1) * CHUNK)
                mixed = _dot_nn(w, vn[rows]) + bcol
                out_ref[rows, cols] = (u[rows] * mixed).astype(BF16)

        zprev = jnp.where(i > 0, zp_ref[...], 0.0)
        ext = jnp.concatenate([zprev, z_ref[...]], axis=0)
        for g, win in enumerate(POOL_WINDOWS):
            cols = slice(g * GROUP, (g + 1) * GROUP)
            zg = ext[:, cols]
            s = _window_sum_back(zg, win)
            pooled = s[HALO:] / _pool_counts(i * tt, tt, win) - zg[HALO:]
            wp = wp_ref[:, g].reshape(GROUP, GROUP)
            y = _dot_nn(pooled.astype(BF16), wp)
            out_ref[:, A_WIDTH + g * GROUP:A_WIDTH + (g + 1) * GROUP] = (y * ps_ref[:, cols]).astype(BF16)

    return _carrier_call(
        body, (proj, proj, proj, proj, w_s, bs_t, g_v, w_pool_g, pool_scale), comm, name="mixer_fwd",
        grid=(t // tt,),
        in_specs=[pl.BlockSpec((tt, A_WIDTH), lambda i: (i, 0)),
                  pl.BlockSpec((tt, A_WIDTH), lambda i: (i, 1)),
                  pl.BlockSpec((tt, B_WIDTH), lambda i: (i, 2)),
                  pl.BlockSpec((HALO, B_WIDTH), lambda i: (jnp.maximum(i * hb - 1, 0), 2)),
                  pl.BlockSpec((N_HEADS, CHUNK, CHUNK), lambda i: (0, 0, 0)),
                  pl.BlockSpec((CHUNK, N_HEADS), lambda i: (0, 0)),
                  pl.BlockSpec((1, A_WIDTH), lambda i: (0, 0)),
                  pl.BlockSpec((N_DEV, 4, GROUP // N_DEV, GROUP), lambda i: (0, 0, 0, 0)),
                  pl.BlockSpec((1, B_WIDTH), lambda i: (0, 0))],
        out_specs=pl.BlockSpec((tt, A_WIDTH + B_WIDTH), lambda i: (i, 0)),
        out_shape=jax.ShapeDtypeStruct((t, A_WIDTH + B_WIDTH), BF16),
        sem=("parallel",))


def _out_proj_call(mixed, w_out, x, comm=None):
    t, d = x.shape
    k = mixed.shape[1]
    tm = _tile(t, 1024)
    tn = _tile(d, 1024)

    def body(a_ref, w_ref, x_ref, o_ref):
        o_ref[...] = x_ref[...] + _dot_nn(a_ref[...], w_ref[...])

    return _carrier_call(
        body, (mixed, w_out, x), comm, name="out_proj_fwd",
        grid=(t // tm, d // tn),
        in_specs=[pl.BlockSpec((tm, k), lambda i, j: (i, 0)),
                  pl.BlockSpec((k, tn), lambda i, j: (0, j)),
                  pl.BlockSpec((tm, tn), lambda i, j: (i, j))],
        out_specs=pl.BlockSpec((tm, tn), lambda i, j: (i, j)),
        out_shape=jax.ShapeDtypeStruct((t, d), F32),
        sem=("parallel", "parallel"))


ARRIVAL_ORDER = (0, 1, 4, 5, 2, 3, 6, 7)
CARRIED_AFTER = 3


class _StreamedGather:
    def __init__(self, shard_ref, land_ref, wbuf, pre0, r0, send_sems, recv_sems, local_sem, fetch_sems):
        self.shard, self.land, self.wbuf, self.fetch_sems = shard_ref, land_ref, wbuf, fetch_sems
        end = shard_ref.shape[0]
        me = _me()
        self.me = me

        def remote(k, src, dst, rel):
            return pltpu.make_async_remote_copy(src_ref=src, dst_ref=dst, send_sem=send_sems.at[k],
                                                recv_sem=recv_sems.at[k], device_id=_peer(rel), device_id_type=MESH)

        def same_rows(k, block, a, b, rel):
            ref = land_ref.at[block, pl.ds(a, b - a)]
            return remote(k, ref, ref, rel)

        src = shard_ref.at[pl.ds(r0, end - r0)]
        dst = land_ref.at[me, pl.ds(r0, end - r0)]
        self.mine = pltpu.make_async_copy(src, dst, local_sem)
        self.first = [remote(k, src, dst, rel) for k, rel in enumerate((SIBLING, 4, 2))]
        lo, hi = _split_rows(r0, end)
        self.relay = [same_rows(3, me ^ 4, *lo, 2), same_rows(4, me ^ 2, *hi, 4)]
        self.passed = [same_rows(5, me ^ 4, r0, end, SIBLING), same_rows(6, me ^ 2, r0, end, SIBLING),
                       same_rows(7, me ^ 6, pre0, end, SIBLING)]
        self.early_relay, self.early_passed = [], []
        if r0 > pre0:
            lo, hi = _split_rows(pre0, r0)
            self.early_relay = [same_rows(8, me ^ 4, *lo, 2), same_rows(9, me ^ 2, *hi, 4)]
            self.early_passed = [same_rows(10, me ^ 4, pre0, r0, SIBLING), same_rows(11, me ^ 2, pre0, r0, SIBLING)]

    def _fetch(self, q):
        src = self.shard if q == 0 else self.land.at[self.me ^ ARRIVAL_ORDER[q]]
        return pltpu.make_async_copy(src, self.wbuf.at[q % 2], self.fetch_sems.at[q % 2])

    def start(self):
        self.mine.start()
        for cp in self.first + self.early_relay + self.early_passed:
            cp.start()
        self._fetch(0).start()

    def arrive(self, q):
        if q == 1:
            self.first[0].wait_recv()
        elif q in (2, 4):
            j = q // 2 - 1
            self.first[1 + j].wait_recv()
            self.relay[j].start()
            self.passed[j].start()
        elif q in (3, 5):
            j = q // 2 - 1
            self.passed[j].wait_recv()
            if self.early_passed:
                self.early_passed[j].wait_recv()
        elif q == 6:
            for cp in self.relay + self.early_relay:
                cp.wait_recv()
            self.passed[2].start()
        else:
            self.passed[2].wait_recv()
        self._fetch(q).start()

    def wait_fetch(self, slot):
        pltpu.make_async_copy(self.shard, self.wbuf.at[slot], self.fetch_sems.at[slot]).wait()

    def finish(self):
        for cp in self.first + self.relay + self.passed + self.early_relay + self.early_passed:
            cp.wait_send()
        self.mine.wait()


_STREAM_SEMS = [pltpu.SemaphoreType.DMA((12,)), pltpu.SemaphoreType.DMA((12,)), pltpu.SemaphoreType.DMA,
                pltpu.SemaphoreType.DMA((2,))]


def _stream_steps(gather, p, i, ni):
    @pl.when((p == 0) & (i == 0))
    def _():
        gather.start()

    @pl.when(i == 0)
    def _():
        gather.wait_fetch(p % 2)

    @pl.when(i == ni - 1)
    def _():
        for q in range(1, N_DEV):
            @pl.when(p == q - 1)
            def _():
                gather.arrive(q)


def _norm_matmul_stream_call(name, x, g, shard, order, land, pre0, r0, comm, epilogue, out_dtypes):
    t, d = x.shape
    cb = shard.shape[1]
    tm = _tile(t, 1024)
    ni = t // tm
    n_sems = len(_STREAM_SEMS)
    assert not comm.lands
    ns, nn, no = len(comm.srcs), len(comm.new), len(out_dtypes)
    n_remote, n_local = _count_copies(comm)
    has_land = land is not None

    def body(order_ref, x_ref, g_ref, shard_ref, *refs):
        refs = refs[has_land:]
        srcs, out_refs, (h_ref, land_ref) = refs[:ns], refs[ns:ns + no], refs[ns + no:ns + no + 2]
        new = refs[ns + no + 2:ns + no + 2 + nn]
        wbuf, sems = refs[ns + no + 2 + nn], refs[ns + no + 3 + nn:]
        p, i = pl.program_id(0), pl.program_id(1)
        gather = _StreamedGather(shard_ref, land_ref, wbuf, pre0, r0, *sems[:n_sems])
        (carried,) = _make_copies(comm.plan(srcs, new), *sems[n_sems:])
        rows = pl.ds(pl.multiple_of(i * tm, tm), tm)
        _stream_steps(gather, p, i, ni)

        @pl.when((p == CARRIED_AFTER) & (i == ni - 1))
        def _():
            for cp in carried:
                cp.start()

        @pl.when(p == 0)
        def _():
            n, _ = _rms_rows(x_ref[...])
            h_ref[rows, :] = (n * g_ref[...]).astype(BF16)

        tails = epilogue(_dot_nn(h_ref[rows, :], wbuf[p % 2]))
        for out_ref, tail, dt in zip(out_refs, tails, out_dtypes):
            out_ref[...] = tail.astype(dt)

        @pl.when((p == N_DEV - 1) & (i == ni - 1))
        def _():
            gather.finish()
            for cp in carried:
                cp.wait()

    carried_sems = [pltpu.SemaphoreType.DMA((max(n_remote, 1),)), pltpu.SemaphoreType.DMA((max(n_remote, 1),)),
                    pltpu.SemaphoreType.DMA((max(n_local, 1),))]
    return pl.pallas_call(
        body, name=name,
        grid_spec=pltpu.PrefetchScalarGridSpec(
            num_scalar_prefetch=1, grid=(N_DEV, ni),
            in_specs=[pl.BlockSpec((tm, d), lambda p, i, o: (jnp.where(p == 0, i, ni - 1), 0)),
                      pl.BlockSpec((1, d), lambda p, i, o: (0, 0)),
                      _ANY] + [_ANY] * (has_land + ns),
            out_specs=[pl.BlockSpec((tm, cb), lambda p, i, o: (i, o[p]))] * no
                      + [pl.BlockSpec(memory_space=pltpu.VMEM), _ANY] + [_ANY] * nn,
            scratch_shapes=[pltpu.VMEM((2, d, cb), BF16)] + _STREAM_SEMS + carried_sems),
        out_shape=[jax.ShapeDtypeStruct((t, N_DEV * cb), dt) for dt in out_dtypes]
                  + [jax.ShapeDtypeStruct((t, d), BF16), jax.ShapeDtypeStruct((N_DEV, d, cb), BF16)] + comm.new,
        input_output_aliases={4: no + 1} if has_land else {},
        compiler_params=_params("arbitrary", "arbitrary"),
    )(order, x, g, shard, *([land] if has_land else []), *comm.srcs)


def _down_call(act, shard, order, land, r0):
    t = act.shape[0]
    rb, d = shard.shape
    tm = _tile(t, 1024)
    ni = t // tm

    def body(order_ref, a_ref, shard_ref, land_in_ref, y_ref, land_ref, wbuf, *sems):
        p, i = pl.program_id(0), pl.program_id(1)
        gather = _StreamedGather(shard_ref, land_ref, wbuf, 0, r0, *sems)
        rows = pl.ds(pl.multiple_of(i * tm, tm), tm)
        _stream_steps(gather, p, i, ni)
        @pl.when(p == 0)
        def _():
            y_ref[rows, :] = _dot_nn(a_ref[...], wbuf[0])

        @pl.when(p > 0)
        def _():
            y_ref[rows, :] += _dot_nn(a_ref[...], wbuf[p % 2])

        @pl.when((p == N_DEV - 1) & (i == ni - 1))
        def _():
            gather.finish()

    return pl.pallas_call(
        body, name="down_fwd",
        grid_spec=pltpu.PrefetchScalarGridSpec(
            num_scalar_prefetch=1, grid=(N_DEV, ni),
            in_specs=[pl.BlockSpec((tm, rb), lambda p, i, o: (i, o[p])), _ANY, _ANY],
            out_specs=[pl.BlockSpec(memory_space=pltpu.VMEM), _ANY],
            scratch_shapes=[pltpu.VMEM((2, rb, d), BF16)] + _STREAM_SEMS),
        out_shape=[jax.ShapeDtypeStruct((t, d), F32), jax.ShapeDtypeStruct((N_DEV, rb, d), BF16)],
        input_output_aliases={3: 1},
        compiler_params=_params("arbitrary", "arbitrary"),
    )(order, act, shard, land)


def _loss_call(y, x2, target, g_final):
    t, d = y.shape
    tr = _tile(t, 512)

    def body(y_ref, x_ref, tg_ref, g_ref, loss_ref, dx_ref, dxb_ref, dg_ref):
        @pl.when(pl.program_id(0) == 0)
        def _():
            loss_ref[...] = jnp.zeros_like(loss_ref)
            dg_ref[...] = jnp.zeros_like(dg_ref)

        n, r = _rms_rows(x_ref[...] + y_ref[...])
        err = n * g_ref[...] - tg_ref[...]
        loss_ref[...] += 0.5 * jnp.sum(jnp.mean(err * err, axis=-1, keepdims=True))
        dy = err * (1.0 / d)
        dg_ref[...] += jnp.sum(dy * n, axis=0, keepdims=True)
        dx = _rms_bwd_rows(dy * g_ref[...], n, r)
        dx_ref[...] = dx
        dxb_ref[...] = dx.astype(BF16)

    return pl.pallas_call(
        body, name="loss_head",
        grid=(t // tr,),
        in_specs=[pl.BlockSpec((tr, d), lambda i: (i, 0)),
                  pl.BlockSpec((tr, d), lambda i: (i, 0)),
                  pl.BlockSpec((tr, d), lambda i: (i, 0)),
                  pl.BlockSpec((1, d), lambda i: (0, 0))],
        out_specs=[pl.BlockSpec((8, LANES), lambda i: (0, 0)),
                   pl.BlockSpec((tr, d), lambda i: (i, 0)),
                   pl.BlockSpec((tr, d), lambda i: (i, 0)),
                   pl.BlockSpec((1, d), lambda i: (0, 0))],
        out_shape=[jax.ShapeDtypeStruct((8, LANES), F32), jax.ShapeDtypeStruct((t, d), F32),
                   jax.ShapeDtypeStruct((t, d), BF16), jax.ShapeDtypeStruct((1, d), F32)],
        compiler_params=_params("arbitrary"),
    )(y, x2, target, g_final)


def _norm_bwd_call(name, dh, x, dres, g, want_bf16, comm=None):
    t, d = x.shape
    tr = _tile(t, 256)

    def body(dh_ref, x_ref, dres_ref, g_ref, dx_ref, *rest):
        dg_ref = rest[-1]

        @pl.when(pl.program_id(0) == 0)
        def _():
            dg_ref[...] = jnp.zeros_like(dg_ref)

        n, r = _rms_rows(x_ref[...])
        dh = dh_ref[...]
        dg_ref[...] += jnp.sum(dh * n, axis=0, keepdims=True)
        dx = dres_ref[...] + _rms_bwd_rows(dh * g_ref[...], n, r)
        dx_ref[...] = dx
        if want_bf16:
            rest[0][...] = dx.astype(BF16)

    row = pl.BlockSpec((tr, d), lambda i: (i, 0))
    vec = pl.BlockSpec((1, d), lambda i: (0, 0))
    out_specs = [row] + ([row] if want_bf16 else []) + [vec]
    out_shape = ([jax.ShapeDtypeStruct((t, d), F32)]
                 + ([jax.ShapeDtypeStruct((t, d), BF16)] if want_bf16 else [])
                 + [jax.ShapeDtypeStruct((1, d), F32)])
    return _carrier_call(
        body, (dh, x, dres, g), comm, name=name,
        grid=(t // tr,),
        in_specs=[row, row, row, vec],
        out_specs=out_specs, out_shape=out_shape,
        sem=("arbitrary",))


def _dact_call(dx3b, w_down, act, comm=None):
    t, d = dx3b.shape
    f = w_down.shape[0]
    tm = _tile(t, 1024)
    tn = _tile(f, 2048)

    def body(g_ref, w_ref, act_ref, o_ref):
        dact = _dot_nt(g_ref[...], w_ref[...])
        o_ref[...] = (dact * act_ref[...].astype(F32)).astype(BF16)

    return _carrier_call(
        body, (dx3b, w_down, act), comm, name="dact_bwd",
        grid=(t // tm, f // tn),
        in_specs=[pl.BlockSpec((tm, d), lambda i, j: (i, 0)),
                  pl.BlockSpec((tn, d), lambda i, j: (j, 0)),
                  pl.BlockSpec((tm, tn), lambda i, j: (i, j))],
        out_specs=pl.BlockSpec((tm, tn), lambda i, j: (i, j)),
        out_shape=jax.ShapeDtypeStruct((t, f), BF16),
        sem=("parallel", "parallel"))


def _wgrad_call(name, a, b, out_blocks, out_block_cols, comm=None, *, t1, t2=None, merge=1):
    t, k1 = a.shape
    k2 = b.shape[1]
    tt = _tile(t, 2048)
    t1 = _tile(k1, t1)
    t2 = _tile(k2, t2) if out_blocks is None else merge * out_block_cols
    nk = t // tt

    def body(a_ref, b_ref, o_ref, acc_ref):
        k = pl.program_id(2)

        @pl.when(k == 0)
        def _():
            acc_ref[...] = _dot_tn(a_ref[...], b_ref[...])

        @pl.when(k > 0)
        def _():
            acc_ref[...] += _dot_tn(a_ref[...], b_ref[...])

        @pl.when(k == nk - 1)
        def _():
            if out_blocks is None:
                o_ref[...] = acc_ref[...].astype(BF16)
            else:
                for blk in range(merge):
                    o_ref[blk] = acc_ref[:, blk * out_block_cols:(blk + 1) * out_block_cols].astype(BF16)

    if out_blocks is None:
        out_spec = pl.BlockSpec((t1, t2), lambda i, j, k: (i, j))
        out_shape = jax.ShapeDtypeStruct((k1, k2), BF16)
    else:
        out_spec = pl.BlockSpec((merge, t1, out_block_cols), lambda i, j, k: (j, i, 0))
        out_shape = jax.ShapeDtypeStruct((out_blocks, k1, out_block_cols), BF16)
    return _carrier_call(
        body, (a, b), comm, name=name,
        grid=(k1 // t1, k2 // t2, nk),
        in_specs=[pl.BlockSpec((tt, t1), lambda i, j, k: (k, i)),
                  pl.BlockSpec((tt, t2), lambda i, j, k: (k, j))],
        out_specs=out_spec, out_shape=out_shape,
        scratch_shapes=[pltpu.VMEM((t1, t2), F32)],
        sem=("parallel", "parallel", "arbitrary"))


def _dgrad_blocked_call(name, g, w_g, comm=None, *, merge=1):
    t = g.shape[0]
    nb, d, cb = w_g.shape
    tm = _tile(t, 1024)
    tn = _tile(d, 2048)
    tk = merge * cb

    def body(g_ref, w_ref, o_ref):
        def product():
            w = w_ref[0] if merge == 1 else jnp.concatenate([w_ref[b] for b in range(merge)], axis=1)
            return _dot_nt(g_ref[...], w)

        @pl.when(pl.program_id(2) == 0)
        def _():
            o_ref[...] = product()

        @pl.when(pl.program_id(2) > 0)
        def _():
            o_ref[...] += product()

    return _carrier_call(
        body, (g, w_g), comm, name=name,
        grid=(t // tm, d // tn, nb // merge),
        in_specs=[pl.BlockSpec((tm, tk), lambda i, j, k: (i, k)),
                  pl.BlockSpec((merge, tn, cb), lambda i, j, k: (k, j, 0))],
        out_specs=pl.BlockSpec((tm, tn), lambda i, j, k: (i, j)),
        out_shape=jax.ShapeDtypeStruct((t, d), F32),
        sem=("parallel", "parallel", "arbitrary"))


def _dmixed_call(dx2b, w_out, comm=None):
    t, d = dx2b.shape
    e = w_out.shape[0]
    tm = _tile(t, 1024)
    tn = _tile(e, 1024)

    def body(g_ref, w_ref, o_ref):
        o_ref[...] = _dot_nt(g_ref[...], w_ref[...])

    return _carrier_call(
        body, (dx2b, w_out), comm, name="dmixed_bwd",
        grid=(t // tm, e // tn),
        in_specs=[pl.BlockSpec((tm, d), lambda i, j: (i, 0)),
                  pl.BlockSpec((tn, d), lambda i, j: (j, 0))],
        out_specs=pl.BlockSpec((tm, tn), lambda i, j: (i, j)),
        out_shape=jax.ShapeDtypeStruct((t, e), F32),
        sem=("parallel", "parallel"))


def _mixer_bwd_call(proj, dmixed, w_s, bs_t, g_v, w_pool_g, pool_scale, comm=None):
    t = proj.shape[0]
    tt = _tile(t, 512)
    nchunk = tt // CHUNK
    hb = tt // HALO
    last_halo = t // HALO - 1
    nsteps = t // tt
    rb = GROUP // N_DEV

    def body(pu_ref, pv_ref, z_ref, zp_ref, da_ref, db_ref, dbn_ref, ws_ref, bs_ref, gv_ref, wp_ref, ps_ref,
             dproj_ref, dws_ref, dbs_ref, dgv_ref, dps_ref, dwp_ref):
        i = pl.program_id(0)

        @pl.when(i == 0)
        def _():
            dws_ref[...] = jnp.zeros_like(dws_ref)
            dbs_ref[...] = jnp.zeros_like(dbs_ref)
            dgv_ref[...] = jnp.zeros_like(dgv_ref)
            dps_ref[...] = jnp.zeros_like(dps_ref)
            dwp_ref[...] = jnp.zeros_like(dwp_ref)

        tril = (lax.broadcasted_iota(jnp.int32, (CHUNK, CHUNK), 0)
                >= lax.broadcasted_iota(jnp.int32, (CHUNK, CHUNK), 1))
        for h in range(N_HEADS):
            cols = slice(h * CHUNK, (h + 1) * CHUNK)
            v, dv_dpv = _gelu_and_grad(pv_ref[:, cols])
            vhat, rv = _rms_rows(v)
            gv = gv_ref[:, cols]
            vn = (vhat * gv).astype(BF16)
            u, du_dpu = _gelu_and_grad(pu_ref[:, cols])
            w = jnp.where(tril, ws_ref[h], 0.0).astype(BF16)
            bcol = bs_ref[:, h:h + 1]
            dout = da_ref[:, cols]
            dmix = dout * u
            dmix_b = dmix.astype(BF16)
            dws = jnp.zeros((CHUNK, CHUNK), F32)
            dbs = jnp.zeros((CHUNK, 1), F32)
            dvn_parts = []
            du_parts = []
            for c in range(nchunk):
                rows = slice(c * CHUNK, (c + 1) * CHUNK)
                mixed = _dot_nn(w, vn[rows]) + bcol
                du_parts.append(dout[rows] * mixed)
                dvn_parts.append(_dot_tn(w, dmix_b[rows]))
                dws = dws + _dot_nt(dmix_b[rows], vn[rows])
                dbs = dbs + jnp.sum(dmix[rows], axis=1, keepdims=True)
            dws_ref[h] += jnp.where(tril, dws, 0.0)
            dbs_ref[:, h:h + 1] += dbs
            dvn = jnp.concatenate(dvn_parts, axis=0)
            du = jnp.concatenate(du_parts, axis=0)
            dgv_ref[:, cols] += jnp.sum(dvn * vhat, axis=0, keepdims=True)
            dv = _rms_bwd_rows(dvn * gv, vhat, rv)
            dproj_ref[:, cols] = (du * du_dpu).astype(BF16)
            dproj_ref[:, A_WIDTH + h * CHUNK:A_WIDTH + (h + 1) * CHUNK] = (dv * dv_dpv).astype(BF16)

        zprev = jnp.where(i > 0, zp_ref[...], 0.0)
        ext = jnp.concatenate([zprev, z_ref[...]], axis=0)
        dnext = jnp.where(i < nsteps - 1, dbn_ref[...], 0.0)
        dext = jnp.concatenate([db_ref[...], dnext], axis=0)
        for g, win in enumerate(POOL_WINDOWS):
            cols = slice(g * GROUP, (g + 1) * GROUP)
            zg = ext[:, cols]
            pooled = _window_sum_back(zg, win)[HALO:] / _pool_counts(i * tt, tt, win) - zg[HALO:]
            pooled_b = pooled.astype(BF16)
            wp = wp_ref[:, g].reshape(GROUP, GROUP)
            y = _dot_nn(pooled_b, wp)
            dout = dext[:, cols]
            dps_ref[:, cols] += jnp.sum(dout[:tt] * y, axis=0, keepdims=True)
            dy_b = (dout * ps_ref[:, cols]).astype(BF16)
            dwp_ref[:, g] += _dot_tn(pooled_b, dy_b[:tt]).reshape(N_DEV, rb, GROUP)
            dpooled = _dot_nt(dy_b, wp)
            q = dpooled / _pool_counts(i * tt, tt + HALO, win)
            dz = _window_sum_fwd(q, win)[:tt] - dpooled[:tt]
            dproj_ref[:, 2 * A_WIDTH + g * GROUP:2 * A_WIDTH + (g + 1) * GROUP] = dz.astype(BF16)

    def full(shape):
        return pl.BlockSpec(shape, lambda i: (0,) * len(shape))

    return _carrier_call(
        body, (proj, proj, proj, proj, dmixed, dmixed, dmixed, w_s, bs_t, g_v, w_pool_g, pool_scale), comm,
        name="mixer_bwd",
        grid=(nsteps,),
        in_specs=[pl.BlockSpec((tt, A_WIDTH), lambda i: (i, 0)),
                  pl.BlockSpec((tt, A_WIDTH), lambda i: (i, 1)),
                  pl.BlockSpec((tt, B_WIDTH), lambda i: (i, 2)),
                  pl.BlockSpec((HALO, B_WIDTH), lambda i: (jnp.maximum(i * hb - 1, 0), 2)),
                  pl.BlockSpec((tt, A_WIDTH), lambda i: (i, 0)),
                  pl.BlockSpec((tt, B_WIDTH), lambda i: (i, 1)),
                  pl.BlockSpec((HALO, B_WIDTH), lambda i: (jnp.minimum((i + 1) * hb, last_halo), 1)),
                  full((N_HEADS, CHUNK, CHUNK)), full((CHUNK, N_HEADS)), full((1, A_WIDTH)),
                  full((N_DEV, 4, rb, GROUP)), full((1, B_WIDTH))],
        out_specs=[pl.BlockSpec((tt, 2 * A_WIDTH + B_WIDTH), lambda i: (i, 0)),
                   full((N_HEADS, CHUNK, CHUNK)), full((CHUNK, N_HEADS)), full((1, A_WIDTH)),
                   full((1, B_WIDTH)), full((N_DEV, 4, rb, GROUP))],
        out_shape=[jax.ShapeDtypeStruct((t, 2 * A_WIDTH + B_WIDTH), BF16),
                   jax.ShapeDtypeStruct((N_HEADS, CHUNK, CHUNK), F32),
                   jax.ShapeDtypeStruct((CHUNK, N_HEADS), F32),
                   jax.ShapeDtypeStruct((1, A_WIDTH), F32),
                   jax.ShapeDtypeStruct((1, B_WIDTH), F32),
                   jax.ShapeDtypeStruct((N_DEV, 4, rb, GROUP), F32)],
        sem=("arbitrary",))


def _adamw(w, g, m, v):
    m = ADAM_B1 * m + (1.0 - ADAM_B1) * g
    v = ADAM_B2 * v + (1.0 - ADAM_B2) * (g * g)
    m_hat = m / ADAM_C1
    v_hat = v / ADAM_C2
    delta = -ADAM_LR * (m_hat / (jnp.sqrt(v_hat) + ADAM_EPS) + ADAM_WD * w)
    return delta, m, v


PAIR_SUM_TILE_ELEMS = 1024 * 1024
ADAMW_TILE_ELEMS = 512 * 1024


def _row_tile(r, c, elems):
    t = r
    while t * c > elems and t % 32 == 0:
        t //= 2
    return t


def _pair_sum_call(name, pos, grad, got):
    _, r, c = grad.shape
    tr = _row_tile(r, c, PAIR_SUM_TILE_ELEMS)

    def chip_of(rel, pos_ref):
        px = jnp.where((rel == 0) | (rel == 2), 1 - pos_ref[0], pos_ref[0])
        py = jnp.where((rel == 1) | (rel == 2), 1 - pos_ref[1], pos_ref[1])
        return 2 * px + py

    def body(pos_ref, own_ref, got_ref, out_ref):
        out_ref[...] = (own_ref[...].astype(F32) + got_ref[...].astype(F32)).astype(BF16)

    return pl.pallas_call(
        body, name=name,
        grid_spec=pltpu.PrefetchScalarGridSpec(
            num_scalar_prefetch=1, grid=(3, r // tr),
            in_specs=[pl.BlockSpec((None, tr, c), lambda k, i, p: (2 * chip_of(k, p) + p[2], i, 0)),
                      pl.BlockSpec((None, tr, c), lambda k, i, p: (chip_of(k, p), i, 0))],
            out_specs=pl.BlockSpec((None, tr, c), lambda k, i, p: (k, i, 0))),
        out_shape=jax.ShapeDtypeStruct((3, r, c), BF16),
        compiler_params=_params("parallel", "parallel"),
    )(pos, grad, got)


def _final_call(name, pos, grad, got_pair, got_chips, w, m, v):
    _, r, c = grad.shape
    tr = _row_tile(r, c, ADAMW_TILE_ELEMS)

    def body(pos_ref, own_ref, pair_ref, chips_ref, w_ref, m_ref, v_ref, g_out, d_out, m_out, v_out):
        g = own_ref[...].astype(F32) + pair_ref[...].astype(F32)
        for j in range(3):
            g = g + chips_ref[j].astype(F32)
        delta, m_new, v_new = _adamw(w_ref[...], g, m_ref[...], v_ref[...])
        g_out[...] = g
        d_out[...] = delta
        m_out[...] = m_new
        v_out[...] = v_new

    row = pl.BlockSpec((tr, c), lambda i, p: (i, 0))
    return pl.pallas_call(
        body, name=name,
        grid_spec=pltpu.PrefetchScalarGridSpec(
            num_scalar_prefetch=1, grid=(r // tr,),
            in_specs=[pl.BlockSpec((None, tr, c), lambda i, p: (4 * p[0] + 2 * p[1] + p[2], i, 0)),
                      pl.BlockSpec((None, tr, c), lambda i, p: (2 * p[0] + p[1], i, 0)),
                      pl.BlockSpec((3, tr, c), lambda i, p: (0, i, 0)), row, row, row],
            out_specs=[row] * 4),
        out_shape=[jax.ShapeDtypeStruct((r, c), F32)] * 4,
        compiler_params=_params("parallel"),
    )(pos, grad, got_pair, got_chips, w, m, v)


def _small_final_call(name, parts, w, m, v):
    _, rows, c = parts.shape
    r = w.shape[0]

    def body(p_ref, w_ref, m_ref, v_ref, g_out, d_out, m_out, v_out):
        g = p_ref[0]
        for k in range(1, N_DEV):
            g = g + p_ref[k]
        delta, m_new, v_new = _adamw(w_ref[...], g[:r], m_ref[...], v_ref[...])
        g_out[...] = g
        d_out[...] = delta
        m_out[...] = m_new
        v_out[...] = v_new

    return pl.pallas_call(
        body, name=name,
        out_shape=[jax.ShapeDtypeStruct((rows, c), F32)] + [jax.ShapeDtypeStruct((r, c), F32)] * 3,
        compiler_params=pltpu.CompilerParams(vmem_limit_bytes=VMEM_LIMIT),
    )(parts, w, m, v)


_SMALL_EARLY = ("g_v", "w_s", "b_s", "pool_scale", "g_ffn", "g_final")
_BIG = ("w_in", "w_pool", "w_out", "w_up", "w_down")
_ORDER = ("g_mix", "w_in", "g_v", "w_s", "b_s", "w_pool", "pool_scale", "w_out", "g_ffn", "w_up", "w_down", "g_final")


def _pack(parts):
    return jnp.concatenate([p.reshape(-1, LANES) for p in parts], axis=0)


def _unpack(packed, like):
    out, row = [], 0
    for a in like:
        rows = a.size // LANES
        out.append(packed[row:row + rows].reshape(a.shape))
        row += rows
    return out


def kernel(x, g_mix, w_in, g_v, w_s, b_s, w_pool, pool_scale, w_out, g_ffn, w_up, w_down, g_final, loss_target, m_g_mix, m_w_in, m_g_v, m_w_s, m_b_s, m_w_pool, m_pool_scale, m_w_out, m_g_ffn, m_w_up, m_w_down, m_g_final, v_g_mix, v_w_in, v_g_v, v_w_s, v_b_s, v_w_pool, v_pool_scale, v_w_out, v_g_ffn, v_w_up, v_w_down, v_g_final):
    weights = dict(g_mix=g_mix, w_in=w_in, g_v=g_v, w_s=w_s, b_s=b_s, w_pool=w_pool, pool_scale=pool_scale,
                   w_out=w_out, g_ffn=g_ffn, w_up=w_up, w_down=w_down, g_final=g_final)
    mom = dict(g_mix=m_g_mix, w_in=m_w_in, g_v=m_g_v, w_s=m_w_s, b_s=m_b_s, w_pool=m_w_pool,
               pool_scale=m_pool_scale, w_out=m_w_out, g_ffn=m_g_ffn, w_up=m_w_up, w_down=m_w_down,
               g_final=m_g_final)
    var = dict(g_mix=v_g_mix, w_in=v_w_in, g_v=v_g_v, w_s=v_w_s, b_s=v_b_s, w_pool=v_w_pool,
               pool_scale=v_pool_scale, w_out=v_w_out, g_ffn=v_g_ffn, w_up=v_w_up, w_down=v_w_down,
               g_final=v_g_final)

    t, d = x.shape[1], x.shape[2]
    xs = x.reshape(t, d)
    target = loss_target.reshape(t, d)

    shard2d = dict(w_in=w_in.reshape(d, -1), w_pool=w_pool.reshape(-1, GROUP), w_out=w_out.reshape(-1, d),
                   w_up=w_up.reshape(d, -1), w_down=w_down.reshape(-1, d))
    sb = {k: shard2d[k].astype(BF16) for k in _BIG}
    rows = {k: sb[k].shape[0] for k in _BIG}

    def gathered_shape(k):
        return jax.ShapeDtypeStruct((N_DEV,) + sb[k].shape, BF16)

    def landing(n, like):
        return jax.ShapeDtypeStruct((n,) + like.shape[1:], like.dtype)

    def from_everyone(block):
        return jax.ShapeDtypeStruct((N_DEV,) + block.shape, block.dtype)

    def cuts(r, fractions):
        return [0] + [int(r * f) // 16 * 16 for f in fractions] + [r]

    g_mix2, g_ffn2, g_final2 = g_mix.reshape(1, d), g_ffn.reshape(1, d), g_final.reshape(1, d)
    g_v2, ps2 = g_v.reshape(1, A_WIDTH), pool_scale.reshape(1, B_WIDTH)
    w_s3 = w_s.reshape(N_HEADS, CHUNK, CHUNK)
    bs_t = b_s.reshape(N_HEADS, CHUNK).T
    xi, yi, ci = _position()
    pos = jnp.stack([xi, yi, ci]).astype(jnp.int32)

    order = (4 * xi + 2 * yi + ci) ^ jnp.array(ARRIVAL_ORDER, jnp.int32)
    u = cuts(rows["w_up"], (0.25, 0.55))
    ahead = cuts(rows["w_down"], (0.6,))[1]
    proj, h1, w_in_g, w_out_g, w_pool_g = _norm_matmul_stream_call(
        "proj_fwd", xs, g_mix2, sb["w_in"], order, None, 0, 0, _Comm(
            [sb["w_out"], sb["w_pool"]], [], [gathered_shape("w_out"), gathered_shape("w_pool")],
            lambda s, l: [_gather_to_neighbours(s[0], l[0], 0, rows["w_out"]) + _everyone(s[1], l[1])]),
        lambda a: (a,), (F32,))
    w_pool_g = w_pool_g.reshape(N_DEV, 4, GROUP // N_DEV, GROUP)
    mixed, w_out_g, w_up_g = _mixer_fwd_call(proj, w_s3, bs_t, g_v2, w_pool_g, ps2, _Comm(
        [sb["w_up"]], [w_out_g], [gathered_shape("w_up")],
        lambda s, l: [_gather_relay(l[0], 0, rows["w_out"]) + _gather_to_neighbours(s[0], l[1], u[0], u[1]),
                      _gather_diagonal_pass_on(l[0], 0, rows["w_out"])]))
    w_out_f = w_out_g.reshape(-1, d)
    x2, w_up_g = _out_proj_call(mixed, w_out_f, xs, _Comm(
        [sb["w_up"]], [w_up_g], [],
        lambda s, l: [_gather_relay(l[0], u[0], u[1]) + _gather_to_neighbours(s[0], l[0], u[1], u[2]),
                      _gather_diagonal_pass_on(l[0], u[0], u[1])]))

    def relu2_and_slope(a):
        r = jnp.maximum(a, 0.0)
        return r * r, 2.0 * r

    act, dact_da, h2, w_up_g, w_down_g = _norm_matmul_stream_call(
        "up_fwd", x2, g_ffn2, sb["w_up"], order, w_up_g, u[1], u[2], _Comm(
            [sb["w_down"]], [], [gathered_shape("w_down")],
            lambda s, l: [_gather_to_neighbours(s[0], l[0], 0, ahead)]),
        relu2_and_slope, (BF16, BF16))
    y, w_down_g = _down_call(act, sb["w_down"], order, w_down_g, ahead)
    w_down_f = w_down_g.reshape(-1, d)
    loss_part, dx3, dx3b, dg_final = _loss_call(y, x2, target, g_final2)

    def pair_sum(k, grad, got):
        return _pair_sum_call(k + "_pair_sum", pos, grad, got)

    def finish(k, grad, got_pair, got_chips):
        s = shard2d[k]
        outs = _final_call(k + "_adamw", pos, grad, got_pair, got_chips, s, mom[k].reshape(s.shape),
                           var[k].reshape(s.shape))
        return [o.reshape(weights[k].shape) for o in outs]

    result = {}
    (gw_down,) = _wgrad_call("w_down_grad", act, dx3b, None, None, t1=1024, t2=2048)
    gw_down = gw_down.reshape(N_DEV, -1, d)
    da, pair_down = _dact_call(dx3b, w_down_f, dact_da, _Comm(
        [gw_down], [], [landing(4, gw_down)], lambda s, l: [_pair_exchange(s[0], l[0])]))
    sums_down = pair_sum("w_down", gw_down, pair_down)
    dn = cuts(rows["w_down"], (0.75,))
    gw_up, got = _wgrad_call("w_up_grad", h2, da, N_DEV, w_up_g.shape[2], _Comm(
        [sums_down], [], [landing(3, sums_down)],
        lambda s, l: [_chip_exchange(s[0], l[0], dn[0], dn[1])]), t1=2048)
    dh2, got, pair_up = _dgrad_blocked_call("dh2_bwd", da, w_up_g, _Comm(
        [sums_down, gw_up], [got], [landing(4, gw_up)],
        lambda s, l: [_chip_exchange(s[0], l[0], dn[1], dn[2]) + _pair_exchange(s[1], l[1])]), merge=2)
    result["w_down"] = finish("w_down", gw_down, pair_down, got)
    sums_up = pair_sum("w_up", gw_up, pair_up)
    v = cuts(rows["w_up"], (0.26, 0.47, 0.69))
    dx2, dx2b, dg_ffn, got_up = _norm_bwd_call("ffn_norm_bwd", dh2, x2, dx3, g_ffn2, True, _Comm(
        [sums_up], [], [landing(3, sums_up)], lambda s, l: [_chip_exchange(s[0], l[0], v[0], v[1])]))
    dmixed, got_up = _dmixed_call(dx2b, w_out_f, _Comm(
        [sums_up], [got_up], [], lambda s, l: [_chip_exchange(s[0], l[0], v[1], v[2])]))
    gw_out, got_up = _wgrad_call("w_out_grad", mixed, dx2b, None, None, _Comm(
        [sums_up], [got_up], [], lambda s, l: [_chip_exchange(s[0], l[0], v[2], v[3])]), t1=2048, t2=1024)
    gw_out = gw_out.reshape(N_DEV, -1, d)
    dproj, dw_s, dbs_t, dg_v, dps, dw_pool, got_up, pair_out = _mixer_bwd_call(
        proj, dmixed, w_s3, bs_t, g_v2, w_pool_g, ps2, _Comm(
            [sums_up, gw_out], [got_up], [landing(4, gw_out)],
            lambda s, l: [_chip_exchange(s[0], l[0], v[3], v[4]) + _pair_exchange(s[1], l[1])]))
    result["w_up"] = finish("w_up", gw_up, pair_up, got_up)
    sums_out = pair_sum("w_out", gw_out, pair_out)
    gw_pool = dw_pool.astype(BF16).reshape(N_DEV, -1, GROUP)
    early = dict(g_v=dg_v, w_s=dw_s, b_s=dbs_t.T, pool_scale=dps, g_ffn=dg_ffn, g_final=dg_final)
    packed = _pack([early[k] for k in _SMALL_EARLY] + [loss_part])
    early_rows = packed.shape[0]
    gw_in, got, pair_pool, parts_early = _wgrad_call("w_in_grad", h1, dproj, N_DEV, w_in_g.shape[2], _Comm(
        [sums_out, gw_pool, packed], [], [landing(3, sums_out), landing(4, gw_pool), from_everyone(packed)],
        lambda s, l: [_chip_exchange(s[0], l[0], 0, rows["w_out"]) + _pair_exchange(s[1], l[1])
                      + _gather_first(s[2], l[2], 0, early_rows)]), t1=2048, merge=MERGE_W_IN)
    send_sems, recv_sems, gw_in, land_in, token = _exchange_start_call(
        "pair_exchange_w_in_start", gw_in, 4, _pair_exchange_copies)
    result["w_out"] = finish("w_out", gw_out, pair_out, got)
    sums_pool = pair_sum("w_pool", gw_pool + token[0, 0], pair_pool)
    gw_in, pair_in = _exchange_wait_call("pair_exchange_w_in_wait", send_sems, recv_sems, gw_in, land_in,
                                         result["w_out"][0], _pair_exchange_copies)
    sums_in = pair_sum("w_in", gw_in, pair_in)
    send_sems, recv_sems, sums_in, land_in, token = _exchange_start_call(
        "chip_exchange_w_in_start", sums_in, 3, _chip_exchange_copies)
    dh1, parts_early, got_pool = _dgrad_blocked_call("dh1_bwd", dproj, w_in_g, _Comm(
        [sums_pool + token[0, 0]], [parts_early], [landing(3, sums_pool)],
        lambda s, l: [_chip_exchange(s[0], l[1], 0, rows["w_pool"]) + _gather_pass_on(l[0], 0, early_rows)]),
        merge=MERGE_W_IN)
    result["w_pool"] = finish("w_pool", gw_pool, pair_pool, got_pool)
    grad_x, dg_mix = _norm_bwd_call("mix_norm_bwd", dh1, xs, dx2, g_mix2, False)
    _, got = _exchange_wait_call("chip_exchange_w_in_wait", send_sems, recv_sems, sums_in, land_in, dg_mix,
                                 _chip_exchange_copies)
    result["w_in"] = finish("w_in", gw_in, pair_in, got)
    packed = _pack([dg_mix])
    (parts_late,) = _comm_call("gather_g_mix_grad", _Comm(
        [packed], [], [from_everyone(packed)], lambda s, l: [_everyone(s[0], l[0])]))

    for names, parts, tag in ((_SMALL_EARLY, parts_early, "small_adamw"), (("g_mix",), parts_late, "g_mix_adamw")):
        outs = _small_final_call(tag, parts, _pack([weights[k] for k in names]), _pack([mom[k] for k in names]),
                                 _pack([var[k] for k in names]))
        if tag == "small_adamw":
            loss = outs[0][-1, 0]
        like = [weights[k] for k in names]
        unpacked = [_unpack(o, like) for o in outs]
        for idx, k in enumerate(names):
            result[k] = [unpacked[q][idx] for q in range(4)]

    grads = [result[k][0] for k in _ORDER]
    deltas = [result[k][1] for k in _ORDER]
    new_m = [result[k][2] for k in _ORDER]
    new_v = [result[k][3] for k in _ORDER]
    return (loss, grad_x.reshape(x.shape), *grads, *deltas, *new_m, *new_v)
```

```python
import functools
import math

import jax
import jax.numpy as jnp
from jax import lax
from jax.experimental import pallas as pl
from jax.experimental.pallas import tpu as pltpu

F32 = jnp.float32
BF16 = jnp.bfloat16
MESH = pl.DeviceIdType.MESH

N_DEV = 8
EPS = 1e-6
CHUNK = 128
N_HEADS = 8
A_WIDTH = 1024
B_WIDTH = 1024
POOL_WINDOWS = (2, 4, 8, 16)
GROUP = 256
HALO = 16
LANES = 128

ADAM_LR = 0.001
ADAM_B1 = 0.9
ADAM_B2 = 0.999
ADAM_EPS = 1e-08
ADAM_WD = 0.01
ADAM_STEP = 10
ADAM_C1 = 1.0 - ADAM_B1 ** ADAM_STEP
ADAM_C2 = 1.0 - ADAM_B2 ** ADAM_STEP

VMEM_LIMIT = 56 * 1024 * 1024
MERGE_W_IN = 2

_GELU_C = math.sqrt(2.0 / math.pi)


def _params(*sem):
    return pltpu.CompilerParams(dimension_semantics=sem, vmem_limit_bytes=VMEM_LIMIT)


def _gelu(x):
    return 0.5 * x * (1.0 + jnp.tanh(_GELU_C * (x + 0.044715 * x * x * x)))


def _gelu_and_grad(x):
    t = jnp.tanh(_GELU_C * (x + 0.044715 * x * x * x))
    g = 0.5 * x * (1.0 + t)
    dg = 0.5 * (1.0 + t) + 0.5 * x * (1.0 - t * t) * (_GELU_C * (1.0 + 3.0 * 0.044715 * x * x))
    return g, dg


def _dot_nn(a, b):
    return lax.dot_general(a, b, (((1,), (0,)), ((), ())), preferred_element_type=F32)


def _dot_nt(a, b):
    return lax.dot_general(a, b, (((1,), (1,)), ((), ())), preferred_element_type=F32)


def _dot_tn(a, b):
    return lax.dot_general(a, b, (((0,), (0,)), ((), ())), preferred_element_type=F32)


def _rms_rows(x):
    r = lax.rsqrt(jnp.mean(x * x, axis=-1, keepdims=True) + EPS)
    return x * r, r


def _rms_bwd_rows(dn, n, r):
    return r * (dn - n * jnp.mean(dn * n, axis=-1, keepdims=True))


def _tile(n, want):
    t = min(n, want)
    assert n % t == 0, (n, want)
    return t


_ANY = pl.BlockSpec(memory_space=pl.ANY)

SIBLING = 1
CHIPS = (4, 2, 6)


def _position():
    return lax.axis_index("x"), lax.axis_index("y"), lax.axis_index("c")


def _me():
    x, y, c = _position()
    return 4 * x + 2 * y + c


def _peer(rel):
    x, y, c = _position()
    return (x ^ ((rel >> 2) & 1), y ^ ((rel >> 1) & 1), c ^ (rel & 1))


class _Comm:
    def __init__(self, srcs, lands, new, plan):
        self.srcs, self.lands, self.new, self.plan = list(srcs), list(lands), list(new), plan


def _make_copies(phases, send_sems, recv_sems, local_sems):
    out, nr, nl = [], 0, 0
    for phase in phases:
        cps = []
        for item in phase:
            if item[0] == "local":
                cps.append(pltpu.make_async_copy(item[1], item[2], local_sems.at[nl]))
                nl += 1
            else:
                cps.append(pltpu.make_async_remote_copy(
                    src_ref=item[1], dst_ref=item[2], send_sem=send_sems.at[nr], recv_sem=recv_sems.at[nr],
                    device_id=_peer(item[3]), device_id_type=MESH))
                nr += 1
        out.append(cps)
    return out


def _count_copies(comm):
    phases = comm.plan([_FakeRef() for _ in comm.srcs], [_FakeRef() for _ in range(len(comm.lands) + len(comm.new))])
    items = [it for ph in phases for it in ph]
    return sum(it[0] == "remote" for it in items), sum(it[0] == "local" for it in items)


class _FakeRef:
    def __getitem__(self, idx):
        return self

    @property
    def at(self):
        return self


def _carrier_call(body, args, comm, *, name, grid, in_specs, out_specs, out_shape, scratch_shapes=(), sem):
    if not isinstance(out_shape, (list, tuple)):
        out_specs, out_shape = [out_specs], [out_shape]
    out_specs, out_shape, scratch_shapes = list(out_specs), list(out_shape), list(scratch_shapes)
    if comm is None:
        res = pl.pallas_call(body, name=name, grid=grid, in_specs=list(in_specs), out_specs=out_specs,
                             out_shape=out_shape, scratch_shapes=scratch_shapes, compiler_params=_params(*sem))(*args)
        return list(res)
    n_in, n_out, n_scr = len(args), len(out_shape), len(scratch_shapes)
    ns, nl, nn = len(comm.srcs), len(comm.lands), len(comm.new)
    n_remote, n_local = _count_copies(comm)
    steps = math.prod(grid)

    def wrapped(*refs):
        ins, srcs = refs[:n_in], refs[n_in:n_in + ns]
        o = n_in + ns + nl
        outs, lands = refs[o:o + n_out], refs[o + n_out:o + n_out + nl + nn]
        scr = refs[o + n_out + nl + nn:]
        phases = _make_copies(comm.plan(srcs, lands), *scr[n_scr:])
        assert len(phases) == 1 or (len(phases) == 2 and steps >= 3)
        step = functools.reduce(lambda acc, a: acc * grid[a] + pl.program_id(a), range(len(grid)), 0)

        @pl.when(step == 0)
        def _():
            for cp in phases[0]:
                cp.start()

        if len(phases) == 2:
            @pl.when(step == steps * 3 // 4)
            def _():
                for cp in phases[0]:
                    cp.wait()
                for cp in phases[1]:
                    cp.start()

        body(*ins, *outs, *scr[:n_scr])

        @pl.when(step == steps - 1)
        def _():
            for cp in phases[-1]:
                cp.wait()

    land_shapes = [jax.ShapeDtypeStruct(a.shape, a.dtype) for a in comm.lands] + comm.new
    sems = [pltpu.SemaphoreType.DMA((max(n_remote, 1),)), pltpu.SemaphoreType.DMA((max(n_remote, 1),)),
            pltpu.SemaphoreType.DMA((max(n_local, 1),))]
    res = pl.pallas_call(
        wrapped, name=name, grid=grid,
        in_specs=list(in_specs) + [_ANY] * (ns + nl), out_specs=out_specs + [_ANY] * (nl + nn),
        out_shape=out_shape + land_shapes, scratch_shapes=scratch_shapes + sems,
        input_output_aliases={n_in + ns + k: n_out + k for k in range(nl)},
        compiler_params=_params(*sem))(*args, *comm.srcs, *comm.lands)
    return list(res)


def _comm_call(name, comm):
    ns, nl, nn = len(comm.srcs), len(comm.lands), len(comm.new)
    n_remote, n_local = _count_copies(comm)

    def body(*refs):
        srcs, lands, sems = refs[:ns], refs[ns + nl:ns + nl + nl + nn], refs[ns + nl + nl + nn:]
        for copies in _make_copies(comm.plan(srcs, lands), *sems):
            for cp in copies:
                cp.start()
            for cp in copies:
                cp.wait()

    land_shapes = [jax.ShapeDtypeStruct(a.shape, a.dtype) for a in comm.lands] + comm.new
    res = pl.pallas_call(
        body, name=name,
        in_specs=[_ANY] * (ns + nl), out_specs=[_ANY] * (nl + nn), out_shape=land_shapes,
        scratch_shapes=[pltpu.SemaphoreType.DMA((max(n_remote, 1),)), pltpu.SemaphoreType.DMA((max(n_remote, 1),)),
                        pltpu.SemaphoreType.DMA((max(n_local, 1),))],
        input_output_aliases={ns + k: k for k in range(nl)},
    )(*comm.srcs, *comm.lands)
    return list(res)


_HBM = pl.BlockSpec(memory_space=pltpu.HBM)
_SEM = pl.BlockSpec(memory_space=pltpu.SEMAPHORE)


def _pair_exchange_copies(grad_ref, land_ref, send_sems, recv_sems):
    _, _, c = _position()
    return [pltpu.make_async_remote_copy(
        src_ref=grad_ref.at[2 * chip + (1 - c)], dst_ref=land_ref.at[chip], send_sem=send_sems.at[chip],
        recv_sem=recv_sems.at[chip], device_id=_peer(SIBLING), device_id_type=MESH) for chip in range(4)]


def _pair_exchange_start_call(name, grad):
    def body(grad_ref, land_ref, send_sems, recv_sems, grad_thru, land_thru, token):
        for cp in _pair_exchange_copies(grad_ref, land_ref, send_sems, recv_sems):
            cp.start()
        token[...] = jnp.zeros_like(token)

    land = lax.empty((4,) + grad.shape[1:], grad.dtype)
    return pl.pallas_call(
        body, name=name,
        out_shape=(pltpu.SemaphoreType.DMA((4,)), pltpu.SemaphoreType.DMA((4,)), pltpu.HBM(grad.shape, grad.dtype),
                   pltpu.HBM(land.shape, land.dtype), jax.ShapeDtypeStruct((8, LANES), grad.dtype)),
        in_specs=(_HBM, _HBM), out_specs=(_SEM, _SEM, _HBM, _HBM, pl.BlockSpec(memory_space=pltpu.VMEM)),
        input_output_aliases={0: 2, 1: 3},
        compiler_params=pltpu.CompilerParams(has_side_effects=pltpu.SideEffectType.DATAFLOW_SIDE_EFFECTING),
    )(pltpu.with_memory_space_constraint(grad, pltpu.HBM), pltpu.with_memory_space_constraint(land, pltpu.HBM))


def _pair_exchange_wait_call(name, send_sems, recv_sems, grad, land, after):
    def body(grad_ref, land_ref, send_sems, recv_sems, after_ref, grad_out, land_out):
        for cp in _pair_exchange_copies(grad_ref, land_ref, send_sems, recv_sems):
            cp.wait_send()
            cp.wait_recv()

    return pl.pallas_call(
        body, name=name,
        out_shape=(pltpu.HBM(grad.shape, grad.dtype), pltpu.HBM(land.shape, land.dtype)),
        in_specs=(_HBM, _HBM, _SEM, _SEM, _ANY), out_specs=(_HBM, _HBM), input_output_aliases={0: 0, 1: 1},
        compiler_params=pltpu.CompilerParams(has_side_effects=pltpu.SideEffectType.DATAFLOW_SIDE_EFFECTING),
    )(grad, land, send_sems, recv_sems, after)


def _rows(ref, block, r0, r1):
    return ref.at[block, pl.ds(r0, r1 - r0)]


def _gather_first(shard, land, r0, r1):
    src = shard.at[pl.ds(r0, r1 - r0)]
    dst = _rows(land, _me(), r0, r1)
    return [("local", src, dst)] + [("remote", src, dst, rel) for rel in (SIBLING,) + CHIPS]


def _gather_pass_on(land, r0, r1):
    return [("remote", _rows(land, _me() ^ rel, r0, r1), _rows(land, _me() ^ rel, r0, r1), SIBLING) for rel in CHIPS]


def _split_rows(r0, r1):
    m = (r0 + r1) // 2 // 16 * 16
    return (r0, m), (m, r1)


def _gather_to_neighbours(shard, land, r0, r1):
    src = shard.at[pl.ds(r0, r1 - r0)]
    dst = _rows(land, _me(), r0, r1)
    return [("local", src, dst)] + [("remote", src, dst, rel) for rel in (SIBLING, 4, 2)]


def _gather_relay(land, r0, r1):
    lo, hi = _split_rows(r0, r1)
    x_block, y_block = _me() ^ 4, _me() ^ 2
    return [("remote", _rows(land, x_block, *lo), _rows(land, x_block, *lo), 2),
            ("remote", _rows(land, y_block, *hi), _rows(land, y_block, *hi), 4),
            ("remote", _rows(land, x_block, r0, r1), _rows(land, x_block, r0, r1), SIBLING),
            ("remote", _rows(land, y_block, r0, r1), _rows(land, y_block, r0, r1), SIBLING)]


def _gather_diagonal_pass_on(land, r0, r1):
    rows = _rows(land, _me() ^ 6, r0, r1)
    return [("remote", rows, rows, SIBLING)]


def _pair_exchange(grad, land):
    _, _, c = _position()
    return [("remote", grad.at[2 * chip + (1 - c)], land.at[chip], SIBLING) for chip in range(4)]


def _chip_exchange(sums, land, r0, r1):
    return [("remote", _rows(sums, j, r0, r1), _rows(land, j, r0, r1), rel) for j, rel in enumerate(CHIPS)]


def _everyone(packed, land):
    dst = land.at[_me()]
    return [("local", packed, dst)] + [("remote", packed, dst, rel) for rel in range(1, N_DEV)]


def _pool_counts(row0, rows, win):
    pos = row0 + lax.broadcasted_iota(jnp.int32, (rows, 1), 0)
    return jnp.minimum(pos + 1, win).astype(F32)


def _window_sum_back(ext, win):
    s = ext
    k = 1
    while k < win:
        s = s + pltpu.roll(s, k, 0)
        k *= 2
    return s


def _window_sum_fwd(ext, win):
    n = ext.shape[0]
    s = ext
    k = 1
    while k < win:
        s = s + pltpu.roll(s, n - k, 0)
        k *= 2
    return s


def _mixer_fwd_call(proj, w_s, bs_t, g_v, w_pool_g, pool_scale, comm=None):
    t = proj.shape[0]
    tt = _tile(t, 512)
    nchunk = tt // CHUNK
    hb = tt // HALO

    def body(pu_ref, pv_ref, z_ref, zp_ref, ws_ref, bs_ref, gv_ref, wp_ref, ps_ref, out_ref):
        i = pl.program_id(0)
        tril = (lax.broadcasted_iota(jnp.int32, (CHUNK, CHUNK), 0)
                >= lax.broadcasted_iota(jnp.int32, (CHUNK, CHUNK), 1))
        for h in range(N_HEADS):
            cols = slice(h * CHUNK, (h + 1) * CHUNK)
            vhat, _ = _rms_rows(_gelu(pv_ref[:, cols]))
            vn = (vhat * gv_ref[:, cols]).astype(BF16)
            u = _gelu(pu_ref[:, cols])
            w = jnp.where(tril, ws_ref[h], 0.0).astype(BF16)
            bcol = bs_ref[:, h:h + 1]
            for c in range(nchunk):
                rows = slice(c * CHUNK, (c + 1) * CHUNK)
                mixed = _dot_nn(w, vn[rows]) + bcol
                out_ref[rows, cols] = (u[rows] * mixed).astype(BF16)

        zprev = jnp.where(i > 0, zp_ref[...], 0.0)
        ext = jnp.concatenate([zprev, z_ref[...]], axis=0)
        for g, win in enumerate(POOL_WINDOWS):
            cols = slice(g * GROUP, (g + 1) * GROUP)
            zg = ext[:, cols]
            s = _window_sum_back(zg, win)
            pooled = s[HALO:] / _pool_counts(i * tt, tt, win) - zg[HALO:]
            wp = wp_ref[:, g].reshape(GROUP, GROUP)
            y = _dot_nn(pooled.astype(BF16), wp)
            out_ref[:, A_WIDTH + g * GROUP:A_WIDTH + (g + 1) * GROUP] = (y * ps_ref[:, cols]).astype(BF16)

    return _carrier_call(
        body, (proj, proj, proj, proj, w_s, bs_t, g_v, w_pool_g, pool_scale), comm, name="mixer_fwd",
        grid=(t // tt,),
        in_specs=[pl.BlockSpec((tt, A_WIDTH), lambda i: (i, 0)),
                  pl.BlockSpec((tt, A_WIDTH), lambda i: (i, 1)),
                  pl.BlockSpec((tt, B_WIDTH), lambda i: (i, 2)),
                  pl.BlockSpec((HALO, B_WIDTH), lambda i: (jnp.maximum(i * hb - 1, 0), 2)),
                  pl.BlockSpec((N_HEADS, CHUNK, CHUNK), lambda i: (0, 0, 0)),
                  pl.BlockSpec((CHUNK, N_HEADS), lambda i: (0, 0)),
                  pl.BlockSpec((1, A_WIDTH), lambda i: (0, 0)),
                  pl.BlockSpec((N_DEV, 4, GROUP // N_DEV, GROUP), lambda i: (0, 0, 0, 0)),
                  pl.BlockSpec((1, B_WIDTH), lambda i: (0, 0))],
        out_specs=pl.BlockSpec((tt, A_WIDTH + B_WIDTH), lambda i: (i, 0)),
        out_shape=jax.ShapeDtypeStruct((t, A_WIDTH + B_WIDTH), BF16),
        sem=("parallel",))


def _out_proj_call(mixed, w_out, x, comm=None):
    t, d = x.shape
    k = mixed.shape[1]
    tm = _tile(t, 1024)
    tn = _tile(d, 1024)

    def body(a_ref, w_ref, x_ref, o_ref):
        o_ref[...] = x_ref[...] + _dot_nn(a_ref[...], w_ref[...])

    return _carrier_call(
        body, (mixed, w_out, x), comm, name="out_proj_fwd",
        grid=(t // tm, d // tn),
        in_specs=[pl.BlockSpec((tm, k), lambda i, j: (i, 0)),
                  pl.BlockSpec((k, tn), lambda i, j: (0, j)),
                  pl.BlockSpec((tm, tn), lambda i, j: (i, j))],
        out_specs=pl.BlockSpec((tm, tn), lambda i, j: (i, j)),
        out_shape=jax.ShapeDtypeStruct((t, d), F32),
        sem=("parallel", "parallel"))


ARRIVAL_ORDER = (0, 1, 4, 5, 2, 3, 6, 7)
CARRIED_AFTER = 3


class _StreamedGather:
    def __init__(self, shard_ref, land_ref, wbuf, pre0, r0, send_sems, recv_sems, local_sem, fetch_sems):
        self.shard, self.land, self.wbuf, self.fetch_sems = shard_ref, land_ref, wbuf, fetch_sems
        end = shard_ref.shape[0]
        me = _me()
        self.me = me

        def remote(k, src, dst, rel):
            return pltpu.make_async_remote_copy(src_ref=src, dst_ref=dst, send_sem=send_sems.at[k],
                                                recv_sem=recv_sems.at[k], device_id=_peer(rel), device_id_type=MESH)

        def same_rows(k, block, a, b, rel):
            ref = land_ref.at[block, pl.ds(a, b - a)]
            return remote(k, ref, ref, rel)

        src = shard_ref.at[pl.ds(r0, end - r0)]
        dst = land_ref.at[me, pl.ds(r0, end - r0)]
        self.mine = pltpu.make_async_copy(src, dst, local_sem)
        self.first = [remote(k, src, dst, rel) for k, rel in enumerate((SIBLING, 4, 2))]
        lo, hi = _split_rows(r0, end)
        self.relay = [same_rows(3, me ^ 4, *lo, 2), same_rows(4, me ^ 2, *hi, 4)]
        self.passed = [same_rows(5, me ^ 4, r0, end, SIBLING), same_rows(6, me ^ 2, r0, end, SIBLING),
                       same_rows(7, me ^ 6, pre0, end, SIBLING)]
        self.early_relay, self.early_passed = [], []
        if r0 > pre0:
            lo, hi = _split_rows(pre0, r0)
            self.early_relay = [same_rows(8, me ^ 4, *lo, 2), same_rows(9, me ^ 2, *hi, 4)]
            self.early_passed = [same_rows(10, me ^ 4, pre0, r0, SIBLING), same_rows(11, me ^ 2, pre0, r0, SIBLING)]

    def _fetch(self, q):
        src = self.shard if q == 0 else self.land.at[self.me ^ ARRIVAL_ORDER[q]]
        return pltpu.make_async_copy(src, self.wbuf.at[q % 2], self.fetch_sems.at[q % 2])

    def start(self):
        self.mine.start()
        for cp in self.first + self.early_relay + self.early_passed:
            cp.start()
        self._fetch(0).start()

    def arrive(self, q):
        if q == 1:
            self.first[0].wait_recv()
        elif q in (2, 4):
            j = q // 2 - 1
            self.first[1 + j].wait_recv()
            self.relay[j].start()
            self.passed[j].start()
        elif q in (3, 5):
            j = q // 2 - 1
            self.passed[j].wait_recv()
            if self.early_passed:
                self.early_passed[j].wait_recv()
        elif q == 6:
            for cp in self.relay + self.early_relay:
                cp.wait_recv()
            self.passed[2].start()
        else:
            self.passed[2].wait_recv()
        self._fetch(q).start()

    def wait_fetch(self, slot):
        pltpu.make_async_copy(self.shard, self.wbuf.at[slot], self.fetch_sems.at[slot]).wait()

    def finish(self):
        for cp in self.first + self.relay + self.passed + self.early_relay + self.early_passed:
            cp.wait_send()
        self.mine.wait()


_STREAM_SEMS = [pltpu.SemaphoreType.DMA((12,)), pltpu.SemaphoreType.DMA((12,)), pltpu.SemaphoreType.DMA,
                pltpu.SemaphoreType.DMA((2,))]


def _stream_steps(gather, p, i, ni):
    @pl.when((p == 0) & (i == 0))
    def _():
        gather.start()

    @pl.when(i == 0)
    def _():
        gather.wait_fetch(p % 2)

    @pl.when(i == ni - 1)
    def _():
        for q in range(1, N_DEV):
            @pl.when(p == q - 1)
            def _():
                gather.arrive(q)


def _norm_matmul_stream_call(name, x, g, shard, order, land, pre0, r0, comm, epilogue, out_dtypes):
    t, d = x.shape
    cb = shard.shape[1]
    tm = _tile(t, 1024)
    ni = t // tm
    n_sems = len(_STREAM_SEMS)
    assert not comm.lands
    ns, nn, no = len(comm.srcs), len(comm.new), len(out_dtypes)
    n_remote, n_local = _count_copies(comm)
    has_land = land is not None

    def body(order_ref, x_ref, g_ref, shard_ref, *refs):
        refs = refs[has_land:]
        srcs, out_refs, (h_ref, land_ref) = refs[:ns], refs[ns:ns + no], refs[ns + no:ns + no + 2]
        new = refs[ns + no + 2:ns + no + 2 + nn]
        wbuf, sems = refs[ns + no + 2 + nn], refs[ns + no + 3 + nn:]
        p, i = pl.program_id(0), pl.program_id(1)
        gather = _StreamedGather(shard_ref, land_ref, wbuf, pre0, r0, *sems[:n_sems])
        (carried,) = _make_copies(comm.plan(srcs, new), *sems[n_sems:])
        rows = pl.ds(pl.multiple_of(i * tm, tm), tm)
        _stream_steps(gather, p, i, ni)

        @pl.when((p == CARRIED_AFTER) & (i == ni - 1))
        def _():
            for cp in carried:
                cp.start()

        @pl.when(p == 0)
        def _():
            n, _ = _rms_rows(x_ref[...])
            h_ref[rows, :] = (n * g_ref[...]).astype(BF16)

        tails = epilogue(_dot_nn(h_ref[rows, :], wbuf[p % 2]))
        for out_ref, tail, dt in zip(out_refs, tails, out_dtypes):
            out_ref[...] = tail.astype(dt)

        @pl.when((p == N_DEV - 1) & (i == ni - 1))
        def _():
            gather.finish()
            for cp in carried:
                cp.wait()

    carried_sems = [pltpu.SemaphoreType.DMA((max(n_remote, 1),)), pltpu.SemaphoreType.DMA((max(n_remote, 1),)),
                    pltpu.SemaphoreType.DMA((max(n_local, 1),))]
    return pl.pallas_call(
        body, name=name,
        grid_spec=pltpu.PrefetchScalarGridSpec(
            num_scalar_prefetch=1, grid=(N_DEV, ni),
            in_specs=[pl.BlockSpec((tm, d), lambda p, i, o: (jnp.where(p == 0, i, ni - 1), 0)),
                      pl.BlockSpec((1, d), lambda p, i, o: (0, 0)),
                      _ANY] + [_ANY] * (has_land + ns),
            out_specs=[pl.BlockSpec((tm, cb), lambda p, i, o: (i, o[p]))] * no
                      + [pl.BlockSpec(memory_space=pltpu.VMEM), _ANY] + [_ANY] * nn,
            scratch_shapes=[pltpu.VMEM((2, d, cb), BF16)] + _STREAM_SEMS + carried_sems),
        out_shape=[jax.ShapeDtypeStruct((t, N_DEV * cb), dt) for dt in out_dtypes]
                  + [jax.ShapeDtypeStruct((t, d), BF16), jax.ShapeDtypeStruct((N_DEV, d, cb), BF16)] + comm.new,
        input_output_aliases={4: no + 1} if has_land else {},
        compiler_params=_params("arbitrary", "arbitrary"),
    )(order, x, g, shard, *([land] if has_land else []), *comm.srcs)


def _down_call(act, shard, order, land, r0):
    t = act.shape[0]
    rb, d = shard.shape
    tm = _tile(t, 1024)
    ni = t // tm

    def body(order_ref, a_ref, shard_ref, land_in_ref, y_ref, land_ref, wbuf, *sems):
        p, i = pl.program_id(0), pl.program_id(1)
        gather = _StreamedGather(shard_ref, land_ref, wbuf, 0, r0, *sems)
        rows = pl.ds(pl.multiple_of(i * tm, tm), tm)
        _stream_steps(gather, p, i, ni)
        @pl.when(p == 0)
        def _():
            y_ref[rows, :] = _dot_nn(a_ref[...], wbuf[0])

        @pl.when(p > 0)
        def _():
            y_ref[rows, :] += _dot_nn(a_ref[...], wbuf[p % 2])

        @pl.when((p == N_DEV - 1) & (i == ni - 1))
        def _():
            gather.finish()

    return pl.pallas_call(
        body, name="down_fwd",
        grid_spec=pltpu.PrefetchScalarGridSpec(
            num_scalar_prefetch=1, grid=(N_DEV, ni),
            in_specs=[pl.BlockSpec((tm, rb), lambda p, i, o: (i, o[p])), _ANY, _ANY],
            out_specs=[pl.BlockSpec(memory_space=pltpu.VMEM), _ANY],
            scratch_shapes=[pltpu.VMEM((2, rb, d), BF16)] + _STREAM_SEMS),
        out_shape=[jax.ShapeDtypeStruct((t, d), F32), jax.ShapeDtypeStruct((N_DEV, rb, d), BF16)],
        input_output_aliases={3: 1},
        compiler_params=_params("arbitrary", "arbitrary"),
    )(order, act, shard, land)


def _loss_call(y, x2, target, g_final):
    t, d = y.shape
    tr = _tile(t, 512)

    def body(y_ref, x_ref, tg_ref, g_ref, loss_ref, dx_ref, dxb_ref, dg_ref):
        @pl.when(pl.program_id(0) == 0)
        def _():
            loss_ref[...] = jnp.zeros_like(loss_ref)
            dg_ref[...] = jnp.zeros_like(dg_ref)

        n, r = _rms_rows(x_ref[...] + y_ref[...])
        err = n * g_ref[...] - tg_ref[...]
        loss_ref[...] += 0.5 * jnp.sum(jnp.mean(err * err, axis=-1, keepdims=True))
        dy = err * (1.0 / d)
        dg_ref[...] += jnp.sum(dy * n, axis=0, keepdims=True)
        dx = _rms_bwd_rows(dy * g_ref[...], n, r)
        dx_ref[...] = dx
        dxb_ref[...] = dx.astype(BF16)

    return pl.pallas_call(
        body, name="loss_head",
        grid=(t // tr,),
        in_specs=[pl.BlockSpec((tr, d), lambda i: (i, 0)),
                  pl.BlockSpec((tr, d), lambda i: (i, 0)),
                  pl.BlockSpec((tr, d), lambda i: (i, 0)),
                  pl.BlockSpec((1, d), lambda i: (0, 0))],
        out_specs=[pl.BlockSpec((8, LANES), lambda i: (0, 0)),
                   pl.BlockSpec((tr, d), lambda i: (i, 0)),
                   pl.BlockSpec((tr, d), lambda i: (i, 0)),
                   pl.BlockSpec((1, d), lambda i: (0, 0))],
        out_shape=[jax.ShapeDtypeStruct((8, LANES), F32), jax.ShapeDtypeStruct((t, d), F32),
                   jax.ShapeDtypeStruct((t, d), BF16), jax.ShapeDtypeStruct((1, d), F32)],
        compiler_params=_params("arbitrary"),
    )(y, x2, target, g_final)


def _norm_bwd_call(name, dh, x, dres, g, want_bf16, comm=None):
    t, d = x.shape
    tr = _tile(t, 256)

    def body(dh_ref, x_ref, dres_ref, g_ref, dx_ref, *rest):
        dg_ref = rest[-1]

        @pl.when(pl.program_id(0) == 0)
        def _():
            dg_ref[...] = jnp.zeros_like(dg_ref)

        n, r = _rms_rows(x_ref[...])
        dh = dh_ref[...]
        dg_ref[...] += jnp.sum(dh * n, axis=0, keepdims=True)
        dx = dres_ref[...] + _rms_bwd_rows(dh * g_ref[...], n, r)
        dx_ref[...] = dx
        if want_bf16:
            rest[0][...] = dx.astype(BF16)

    row = pl.BlockSpec((tr, d), lambda i: (i, 0))
    vec = pl.BlockSpec((1, d), lambda i: (0, 0))
    out_specs = [row] + ([row] if want_bf16 else []) + [vec]
    out_shape = ([jax.ShapeDtypeStruct((t, d), F32)]
                 + ([jax.ShapeDtypeStruct((t, d), BF16)] if want_bf16 else [])
                 + [jax.ShapeDtypeStruct((1, d), F32)])
    return _carrier_call(
        body, (dh, x, dres, g), comm, name=name,
        grid=(t // tr,),
        in_specs=[row, row, row, vec],
        out_specs=out_specs, out_shape=out_shape,
        sem=("arbitrary",))


def _dact_call(dx3b, w_down, act, comm=None):
    t, d = dx3b.shape
    f = w_down.shape[0]
    tm = _tile(t, 1024)
    tn = _tile(f, 2048)

    def body(g_ref, w_ref, act_ref, o_ref):
        dact = _dot_nt(g_ref[...], w_ref[...])
        o_ref[...] = (dact * act_ref[...].astype(F32)).astype(BF16)

    return _carrier_call(
        body, (dx3b, w_down, act), comm, name="dact_bwd",
        grid=(t // tm, f // tn),
        in_specs=[pl.BlockSpec((tm, d), lambda i, j: (i, 0)),
                  pl.BlockSpec((tn, d), lambda i, j: (j, 0)),
                  pl.BlockSpec((tm, tn), lambda i, j: (i, j))],
        out_specs=pl.BlockSpec((tm, tn), lambda i, j: (i, j)),
        out_shape=jax.ShapeDtypeStruct((t, f), BF16),
        sem=("parallel", "parallel"))


def _wgrad_call(name, a, b, out_blocks, out_block_cols, comm=None, *, t1, t2=None, merge=1):
    t, k1 = a.shape
    k2 = b.shape[1]
    tt = _tile(t, 2048)
    t1 = _tile(k1, t1)
    t2 = _tile(k2, t2) if out_blocks is None else merge * out_block_cols
    nk = t // tt

    def body(a_ref, b_ref, o_ref, acc_ref):
        k = pl.program_id(2)

        @pl.when(k == 0)
        def _():
            acc_ref[...] = _dot_tn(a_ref[...], b_ref[...])

        @pl.when(k > 0)
        def _():
            acc_ref[...] += _dot_tn(a_ref[...], b_ref[...])

        @pl.when(k == nk - 1)
        def _():
            if out_blocks is None:
                o_ref[...] = acc_ref[...].astype(BF16)
            else:
                for blk in range(merge):
                    o_ref[blk] = acc_ref[:, blk * out_block_cols:(blk + 1) * out_block_cols].astype(BF16)

    if out_blocks is None:
        out_spec = pl.BlockSpec((t1, t2), lambda i, j, k: (i, j))
        out_shape = jax.ShapeDtypeStruct((k1, k2), BF16)
    else:
        out_spec = pl.BlockSpec((merge, t1, out_block_cols), lambda i, j, k: (j, i, 0))
        out_shape = jax.ShapeDtypeStruct((out_blocks, k1, out_block_cols), BF16)
    return _carrier_call(
        body, (a, b), comm, name=name,
        grid=(k1 // t1, k2 // t2, nk),
        in_specs=[pl.BlockSpec((tt, t1), lambda i, j, k: (k, i)),
                  pl.BlockSpec((tt, t2), lambda i, j, k: (k, j))],
        out_specs=out_spec, out_shape=out_shape,
        scratch_shapes=[pltpu.VMEM((t1, t2), F32)],
        sem=("parallel", "parallel", "arbitrary"))


def _dgrad_blocked_call(name, g, w_g, comm=None, *, merge=1):
    t = g.shape[0]
    nb, d, cb = w_g.shape
    tm = _tile(t, 1024)
    tn = _tile(d, 2048)
    tk = merge * cb

    def body(g_ref, w_ref, o_ref):
        def product():
            w = w_ref[0] if merge == 1 else jnp.concatenate([w_ref[b] for b in range(merge)], axis=1)
            return _dot_nt(g_ref[...], w)

        @pl.when(pl.program_id(2) == 0)
        def _():
            o_ref[...] = product()

        @pl.when(pl.program_id(2) > 0)
        def _():
            o_ref[...] += product()

    return _carrier_call(
        body, (g, w_g), comm, name=name,
        grid=(t // tm, d // tn, nb // merge),
        in_specs=[pl.BlockSpec((tm, tk), lambda i, j, k: (i, k)),
                  pl.BlockSpec((merge, tn, cb), lambda i, j, k: (k, j, 0))],
        out_specs=pl.BlockSpec((tm, tn), lambda i, j, k: (i, j)),
        out_shape=jax.ShapeDtypeStruct((t, d), F32),
        sem=("parallel", "parallel", "arbitrary"))


def _dmixed_call(dx2b, w_out, comm=None):
    t, d = dx2b.shape
    e = w_out.shape[0]
    tm = _tile(t, 1024)
    tn = _tile(e, 1024)

    def body(g_ref, w_ref, o_ref):
        o_ref[...] = _dot_nt(g_ref[...], w_ref[...])

    return _carrier_call(
        body, (dx2b, w_out), comm, name="dmixed_bwd",
        grid=(t // tm, e // tn),
        in_specs=[pl.BlockSpec((tm, d), lambda i, j: (i, 0)),
                  pl.BlockSpec((tn, d), lambda i, j: (j, 0))],
        out_specs=pl.BlockSpec((tm, tn), lambda i, j: (i, j)),
        out_shape=jax.ShapeDtypeStruct((t, e), F32),
        sem=("parallel", "parallel"))


def _mixer_bwd_call(proj, dmixed, w_s, bs_t, g_v, w_pool_g, pool_scale, comm=None):
    t = proj.shape[0]
    tt = _tile(t, 512)
    nchunk = tt // CHUNK
    hb = tt // HALO
    last_halo = t // HALO - 1
    nsteps = t // tt
    rb = GROUP // N_DEV

    def body(pu_ref, pv_ref, z_ref, zp_ref, da_ref, db_ref, dbn_ref, ws_ref, bs_ref, gv_ref, wp_ref, ps_ref,
             dproj_ref, dws_ref, dbs_ref, dgv_ref, dps_ref, dwp_ref):
        i = pl.program_id(0)

        @pl.when(i == 0)
        def _():
            dws_ref[...] = jnp.zeros_like(dws_ref)
            dbs_ref[...] = jnp.zeros_like(dbs_ref)
            dgv_ref[...] = jnp.zeros_like(dgv_ref)
            dps_ref[...] = jnp.zeros_like(dps_ref)
            dwp_ref[...] = jnp.zeros_like(dwp_ref)

        tril = (lax.broadcasted_iota(jnp.int32, (CHUNK, CHUNK), 0)
                >= lax.broadcasted_iota(jnp.int32, (CHUNK, CHUNK), 1))
        for h in range(N_HEADS):
            cols = slice(h * CHUNK, (h + 1) * CHUNK)
            v, dv_dpv = _gelu_and_grad(pv_ref[:, cols])
            vhat, rv = _rms_rows(v)
            gv = gv_ref[:, cols]
            vn = (vhat * gv).astype(BF16)
            u, du_dpu = _gelu_and_grad(pu_ref[:, cols])
            w = jnp.where(tril, ws_ref[h], 0.0).astype(BF16)
            bcol = bs_ref[:, h:h + 1]
            dout = da_ref[:, cols]
            dmix = dout * u
            dmix_b = dmix.astype(BF16)
            dws = jnp.zeros((CHUNK, CHUNK), F32)
            dbs = jnp.zeros((CHUNK, 1), F32)
            dvn_parts = []
            du_parts = []
            for c in range(nchunk):
                rows = slice(c * CHUNK, (c + 1) * CHUNK)
                mixed = _dot_nn(w, vn[rows]) + bcol
                du_parts.append(dout[rows] * mixed)
                dvn_parts.append(_dot_tn(w, dmix_b[rows]))
                dws = dws + _dot_nt(dmix_b[rows], vn[rows])
                dbs = dbs + jnp.sum(dmix[rows], axis=1, keepdims=True)
            dws_ref[h] += jnp.where(tril, dws, 0.0)
            dbs_ref[:, h:h + 1] += dbs
            dvn = jnp.concatenate(dvn_parts, axis=0)
            du = jnp.concatenate(du_parts, axis=0)
            dgv_ref[:, cols] += jnp.sum(dvn * vhat, axis=0, keepdims=True)
            dv = _rms_bwd_rows(dvn * gv, vhat, rv)
            dproj_ref[:, cols] = (du * du_dpu).astype(BF16)
            dproj_ref[:, A_WIDTH + h * CHUNK:A_WIDTH + (h + 1) * CHUNK] = (dv * dv_dpv).astype(BF16)

        zprev = jnp.where(i > 0, zp_ref[...], 0.0)
        ext = jnp.concatenate([zprev, z_ref[...]], axis=0)
        dnext = jnp.where(i < nsteps - 1, dbn_ref[...], 0.0)
        dext = jnp.concatenate([db_ref[...], dnext], axis=0)
        for g, win in enumerate(POOL_WINDOWS):
            cols = slice(g * GROUP, (g + 1) * GROUP)
            zg = ext[:, cols]
            pooled = _window_sum_back(zg, win)[HALO:] / _pool_counts(i * tt, tt, win) - zg[HALO:]
            pooled_b = pooled.astype(BF16)
            wp = wp_ref[:, g].reshape(GROUP, GROUP)
            y = _dot_nn(pooled_b, wp)
            dout = dext[:, cols]
            dps_ref[:, cols] += jnp.sum(dout[:tt] * y, axis=0, keepdims=True)
            dy_b = (dout * ps_ref[:, cols]).astype(BF16)
            dwp_ref[:, g] += _dot_tn(pooled_b, dy_b[:tt]).reshape(N_DEV, rb, GROUP)
            dpooled = _dot_nt(dy_b, wp)
            q = dpooled / _pool_counts(i * tt, tt + HALO, win)
            dz = _window_sum_fwd(q, win)[:tt] - dpooled[:tt]
            dproj_ref[:, 2 * A_WIDTH + g * GROUP:2 * A_WIDTH + (g + 1) * GROUP] = dz.astype(BF16)

    def full(shape):
        return pl.BlockSpec(shape, lambda i: (0,) * len(shape))

    return _carrier_call(
        body, (proj, proj, proj, proj, dmixed, dmixed, dmixed, w_s, bs_t, g_v, w_pool_g, pool_scale), comm,
        name="mixer_bwd",
        grid=(nsteps,),
        in_specs=[pl.BlockSpec((tt, A_WIDTH), lambda i: (i, 0)),
                  pl.BlockSpec((tt, A_WIDTH), lambda i: (i, 1)),
                  pl.BlockSpec((tt, B_WIDTH), lambda i: (i, 2)),
                  pl.BlockSpec((HALO, B_WIDTH), lambda i: (jnp.maximum(i * hb - 1, 0), 2)),
                  pl.BlockSpec((tt, A_WIDTH), lambda i: (i, 0)),
                  pl.BlockSpec((tt, B_WIDTH), lambda i: (i, 1)),
                  pl.BlockSpec((HALO, B_WIDTH), lambda i: (jnp.minimum((i + 1) * hb, last_halo), 1)),
                  full((N_HEADS, CHUNK, CHUNK)), full((CHUNK, N_HEADS)), full((1, A_WIDTH)),
                  full((N_DEV, 4, rb, GROUP)), full((1, B_WIDTH))],
        out_specs=[pl.BlockSpec((tt, 2 * A_WIDTH + B_WIDTH), lambda i: (i, 0)),
                   full((N_HEADS, CHUNK, CHUNK)), full((CHUNK, N_HEADS)), full((1, A_WIDTH)),
                   full((1, B_WIDTH)), full((N_DEV, 4, rb, GROUP))],
        out_shape=[jax.ShapeDtypeStruct((t, 2 * A_WIDTH + B_WIDTH), BF16),
                   jax.ShapeDtypeStruct((N_HEADS, CHUNK, CHUNK), F32),
                   jax.ShapeDtypeStruct((CHUNK, N_HEADS), F32),
                   jax.ShapeDtypeStruct((1, A_WIDTH), F32),
                   jax.ShapeDtypeStruct((1, B_WIDTH), F32),
                   jax.ShapeDtypeStruct((N_DEV, 4, rb, GROUP), F32)],
        sem=("arbitrary",))


def _adamw(w, g, m, v):
    m = ADAM_B1 * m + (1.0 - ADAM_B1) * g
    v = ADAM_B2 * v + (1.0 - ADAM_B2) * (g * g)
    m_hat = m / ADAM_C1
    v_hat = v / ADAM_C2
    delta = -ADAM_LR * (m_hat / (jnp.sqrt(v_hat) + ADAM_EPS) + ADAM_WD * w)
    return delta, m, v


PAIR_SUM_TILE_ELEMS = 1024 * 1024
ADAMW_TILE_ELEMS = 512 * 1024


def _row_tile(r, c, elems):
    t = r
    while t * c > elems and t % 32 == 0:
        t //= 2
    return t


def _pair_sum_call(name, pos, grad, got):
    _, r, c = grad.shape
    tr = _row_tile(r, c, PAIR_SUM_TILE_ELEMS)

    def chip_of(rel, pos_ref):
        px = jnp.where((rel == 0) | (rel == 2), 1 - pos_ref[0], pos_ref[0])
        py = jnp.where((rel == 1) | (rel == 2), 1 - pos_ref[1], pos_ref[1])
        return 2 * px + py

    def body(pos_ref, own_ref, got_ref, out_ref):
        out_ref[...] = (own_ref[...].astype(F32) + got_ref[...].astype(F32)).astype(BF16)

    return pl.pallas_call(
        body, name=name,
        grid_spec=pltpu.PrefetchScalarGridSpec(
            num_scalar_prefetch=1, grid=(3, r // tr),
            in_specs=[pl.BlockSpec((None, tr, c), lambda k, i, p: (2 * chip_of(k, p) + p[2], i, 0)),
                      pl.BlockSpec((None, tr, c), lambda k, i, p: (chip_of(k, p), i, 0))],
            out_specs=pl.BlockSpec((None, tr, c), lambda k, i, p: (k, i, 0))),
        out_shape=jax.ShapeDtypeStruct((3, r, c), BF16),
        compiler_params=_params("parallel", "parallel"),
    )(pos, grad, got)


def _final_call(name, pos, grad, got_pair, got_chips, w, m, v):
    _, r, c = grad.shape
    tr = _row_tile(r, c, ADAMW_TILE_ELEMS)

    def body(pos_ref, own_ref, pair_ref, chips_ref, w_ref, m_ref, v_ref, g_out, d_out, m_out, v_out):
        g = own_ref[...].astype(F32) + pair_ref[...].astype(F32)
        for j in range(3):
            g = g + chips_ref[j].astype(F32)
        delta, m_new, v_new = _adamw(w_ref[...], g, m_ref[...], v_ref[...])
        g_out[...] = g
        d_out[...] = delta
        m_out[...] = m_new
        v_out[...] = v_new

    row = pl.BlockSpec((tr, c), lambda i, p: (i, 0))
    return pl.pallas_call(
        body, name=name,
        grid_spec=pltpu.PrefetchScalarGridSpec(
            num_scalar_prefetch=1, grid=(r // tr,),
            in_specs=[pl.BlockSpec((None, tr, c), lambda i, p: (4 * p[0] + 2 * p[1] + p[2], i, 0)),
                      pl.BlockSpec((None, tr, c), lambda i, p: (2 * p[0] + p[1], i, 0)),
                      pl.BlockSpec((3, tr, c), lambda i, p: (0, i, 0)), row, row, row],
            out_specs=[row] * 4),
        out_shape=[jax.ShapeDtypeStruct((r, c), F32)] * 4,
        compiler_params=_params("parallel"),
    )(pos, grad, got_pair, got_chips, w, m, v)


def _small_final_call(name, parts, w, m, v):
    _, rows, c = parts.shape
    r = w.shape[0]

    def body(p_ref, w_ref, m_ref, v_ref, g_out, d_out, m_out, v_out):
        g = p_ref[0]
        for k in range(1, N_DEV):
            g = g + p_ref[k]
        delta, m_new, v_new = _adamw(w_ref[...], g[:r], m_ref[...], v_ref[...])
        g_out[...] = g
        d_out[...] = delta
        m_out[...] = m_new
        v_out[...] = v_new

    return pl.pallas_call(
        body, name=name,
        out_shape=[jax.ShapeDtypeStruct((rows, c), F32)] + [jax.ShapeDtypeStruct((r, c), F32)] * 3,
        compiler_params=pltpu.CompilerParams(vmem_limit_bytes=VMEM_LIMIT),
    )(parts, w, m, v)


_SMALL_EARLY = ("g_v", "w_s", "b_s", "pool_scale", "g_ffn", "g_final")
_BIG = ("w_in", "w_pool", "w_out", "w_up", "w_down")
_ORDER = ("g_mix", "w_in", "g_v", "w_s", "b_s", "w_pool", "pool_scale", "w_out", "g_ffn", "w_up", "w_down", "g_final")


def _pack(parts):
    return jnp.concatenate([p.reshape(-1, LANES) for p in parts], axis=0)


def _unpack(packed, like):
    out, row = [], 0
    for a in like:
        rows = a.size // LANES
        out.append(packed[row:row + rows].reshape(a.shape))
        row += rows
    return out


def kernel(x, g_mix, w_in, g_v, w_s, b_s, w_pool, pool_scale, w_out, g_ffn, w_up, w_down, g_final, loss_target, m_g_mix, m_w_in, m_g_v, m_w_s, m_b_s, m_w_pool, m_pool_scale, m_w_out, m_g_ffn, m_w_up, m_w_down, m_g_final, v_g_mix, v_w_in, v_g_v, v_w_s, v_b_s, v_w_pool, v_pool_scale, v_w_out, v_g_ffn, v_w_up, v_w_down, v_g_final):
    weights = dict(g_mix=g_mix, w_in=w_in, g_v=g_v, w_s=w_s, b_s=b_s, w_pool=w_pool, pool_scale=pool_scale,
                   w_out=w_out, g_ffn=g_ffn, w_up=w_up, w_down=w_down, g_final=g_final)
    mom = dict(g_mix=m_g_mix, w_in=m_w_in, g_v=m_g_v, w_s=m_w_s, b_s=m_b_s, w_pool=m_w_pool,
               pool_scale=m_pool_scale, w_out=m_w_out, g_ffn=m_g_ffn, w_up=m_w_up, w_down=m_w_down,
               g_final=m_g_final)
    var = dict(g_mix=v_g_mix, w_in=v_w_in, g_v=v_g_v, w_s=v_w_s, b_s=v_b_s, w_pool=v_w_pool,
               pool_scale=v_pool_scale, w_out=v_w_out, g_ffn=v_g_ffn, w_up=v_w_up, w_down=v_w_down,
               g_final=v_g_final)

    t, d = x.shape[1], x.shape[2]
    xs = x.reshape(t, d)
    target = loss_target.reshape(t, d)

    shard2d = dict(w_in=w_in.reshape(d, -1), w_pool=w_pool.reshape(-1, GROUP), w_out=w_out.reshape(-1, d),
                   w_up=w_up.reshape(d, -1), w_down=w_down.reshape(-1, d))
    sb = {k: shard2d[k].astype(BF16) for k in _BIG}
    rows = {k: sb[k].shape[0] for k in _BIG}

    def gathered_shape(k):
        return jax.ShapeDtypeStruct((N_DEV,) + sb[k].shape, BF16)

    def landing(n, like):
        return jax.ShapeDtypeStruct((n,) + like.shape[1:], like.dtype)

    def from_everyone(block):
        return jax.ShapeDtypeStruct((N_DEV,) + block.shape, block.dtype)

    def cuts(r, fractions):
        return [0] + [int(r * f) // 16 * 16 for f in fractions] + [r]

    g_mix2, g_ffn2, g_final2 = g_mix.reshape(1, d), g_ffn.reshape(1, d), g_final.reshape(1, d)
    g_v2, ps2 = g_v.reshape(1, A_WIDTH), pool_scale.reshape(1, B_WIDTH)
    w_s3 = w_s.reshape(N_HEADS, CHUNK, CHUNK)
    bs_t = b_s.reshape(N_HEADS, CHUNK).T
    xi, yi, ci = _position()
    pos = jnp.stack([xi, yi, ci]).astype(jnp.int32)

    order = (4 * xi + 2 * yi + ci) ^ jnp.array(ARRIVAL_ORDER, jnp.int32)
    u = cuts(rows["w_up"], (0.25, 0.55))
    ahead = cuts(rows["w_down"], (0.6,))[1]
    proj, h1, w_in_g, w_out_g, w_pool_g = _norm_matmul_stream_call(
        "proj_fwd", xs, g_mix2, sb["w_in"], order, None, 0, 0, _Comm(
            [sb["w_out"], sb["w_pool"]], [], [gathered_shape("w_out"), gathered_shape("w_pool")],
            lambda s, l: [_gather_to_neighbours(s[0], l[0], 0, rows["w_out"]) + _everyone(s[1], l[1])]),
        lambda a: (a,), (F32,))
    w_pool_g = w_pool_g.reshape(N_DEV, 4, GROUP // N_DEV, GROUP)
    mixed, w_out_g, w_up_g = _mixer_fwd_call(proj, w_s3, bs_t, g_v2, w_pool_g, ps2, _Comm(
        [sb["w_up"]], [w_out_g], [gathered_shape("w_up")],
        lambda s, l: [_gather_relay(l[0], 0, rows["w_out"]) + _gather_to_neighbours(s[0], l[1], u[0], u[1]),
                      _gather_diagonal_pass_on(l[0], 0, rows["w_out"])]))
    w_out_f = w_out_g.reshape(-1, d)
    x2, w_up_g = _out_proj_call(mixed, w_out_f, xs, _Comm(
        [sb["w_up"]], [w_up_g], [],
        lambda s, l: [_gather_relay(l[0], u[0], u[1]) + _gather_to_neighbours(s[0], l[0], u[1], u[2]),
                      _gather_diagonal_pass_on(l[0], u[0], u[1])]))

    def relu2_and_slope(a):
        r = jnp.maximum(a, 0.0)
        return r * r, 2.0 * r

    act, dact_da, h2, w_up_g, w_down_g = _norm_matmul_stream_call(
        "up_fwd", x2, g_ffn2, sb["w_up"], order, w_up_g, u[1], u[2], _Comm(
            [sb["w_down"]], [], [gathered_shape("w_down")],
            lambda s, l: [_gather_to_neighbours(s[0], l[0], 0, ahead)]),
        relu2_and_slope, (BF16, BF16))
    y, w_down_g = _down_call(act, sb["w_down"], order, w_down_g, ahead)
    w_down_f = w_down_g.reshape(-1, d)
    loss_part, dx3, dx3b, dg_final = _loss_call(y, x2, target, g_final2)

    def pair_sum(k, grad, got):
        return _pair_sum_call(k + "_pair_sum", pos, grad, got)

    def finish(k, grad, got_pair, got_chips):
        s = shard2d[k]
        outs = _final_call(k + "_adamw", pos, grad, got_pair, got_chips, s, mom[k].reshape(s.shape),
                           var[k].reshape(s.shape))
        return [o.reshape(weights[k].shape) for o in outs]

    result = {}
    (gw_down,) = _wgrad_call("w_down_grad", act, dx3b, None, None, t1=1024, t2=2048)
    gw_down = gw_down.reshape(N_DEV, -1, d)
    da, pair_down = _dact_call(dx3b, w_down_f, dact_da, _Comm(
        [gw_down], [], [landing(4, gw_down)], lambda s, l: [_pair_exchange(s[0], l[0])]))
    sums_down = pair_sum("w_down", gw_down, pair_down)
    dn = cuts(rows["w_down"], (0.75,))
    gw_up, got = _wgrad_call("w_up_grad", h2, da, N_DEV, w_up_g.shape[2], _Comm(
        [sums_down], [], [landing(3, sums_down)],
        lambda s, l: [_chip_exchange(s[0], l[0], dn[0], dn[1])]), t1=2048)
    dh2, got, pair_up = _dgrad_blocked_call("dh2_bwd", da, w_up_g, _Comm(
        [sums_down, gw_up], [got], [landing(4, gw_up)],
        lambda s, l: [_chip_exchange(s[0], l[0], dn[1], dn[2]) + _pair_exchange(s[1], l[1])]), merge=2)
    result["w_down"] = finish("w_down", gw_down, pair_down, got)
    sums_up = pair_sum("w_up", gw_up, pair_up)
    v = cuts(rows["w_up"], (0.26, 0.47, 0.69))
    dx2, dx2b, dg_ffn, got_up = _norm_bwd_call("ffn_norm_bwd", dh2, x2, dx3, g_ffn2, True, _Comm(
        [sums_up], [], [landing(3, sums_up)], lambda s, l: [_chip_exchange(s[0], l[0], v[0], v[1])]))
    dmixed, got_up = _dmixed_call(dx2b, w_out_f, _Comm(
        [sums_up], [got_up], [], lambda s, l: [_chip_exchange(s[0], l[0], v[1], v[2])]))
    gw_out, got_up = _wgrad_call("w_out_grad", mixed, dx2b, None, None, _Comm(
        [sums_up], [got_up], [], lambda s, l: [_chip_exchange(s[0], l[0], v[2], v[3])]), t1=2048, t2=1024)
    gw_out = gw_out.reshape(N_DEV, -1, d)
    dproj, dw_s, dbs_t, dg_v, dps, dw_pool, got_up, pair_out = _mixer_bwd_call(
        proj, dmixed, w_s3, bs_t, g_v2, w_pool_g, ps2, _Comm(
            [sums_up, gw_out], [got_up], [landing(4, gw_out)],
            lambda s, l: [_chip_exchange(s[0], l[0], v[3], v[4]) + _pair_exchange(s[1], l[1])]))
    result["w_up"] = finish("w_up", gw_up, pair_up, got_up)
    sums_out = pair_sum("w_out", gw_out, pair_out)
    gw_pool = dw_pool.astype(BF16).reshape(N_DEV, -1, GROUP)
    early = dict(g_v=dg_v, w_s=dw_s, b_s=dbs_t.T, pool_scale=dps, g_ffn=dg_ffn, g_final=dg_final)
    packed = _pack([early[k] for k in _SMALL_EARLY] + [loss_part])
    early_rows = packed.shape[0]
    gw_in, got, pair_pool, parts_early = _wgrad_call("w_in_grad", h1, dproj, N_DEV, w_in_g.shape[2], _Comm(
        [sums_out, gw_pool, packed], [], [landing(3, sums_out), landing(4, gw_pool), from_everyone(packed)],
        lambda s, l: [_chip_exchange(s[0], l[0], 0, rows["w_out"]) + _pair_exchange(s[1], l[1])
                      + _gather_first(s[2], l[2], 0, early_rows)]), t1=2048, merge=MERGE_W_IN)
    send_sems, recv_sems, gw_in, land_in, token = _pair_exchange_start_call("pair_exchange_w_in_start", gw_in)
    result["w_out"] = finish("w_out", gw_out, pair_out, got + token[0, 0])
    sums_pool = pair_sum("w_pool", gw_pool + token[0, 0], pair_pool)
    gw_in, pair_in = _pair_exchange_wait_call("pair_exchange_w_in_wait", send_sems, recv_sems, gw_in, land_in,
                                              result["w_out"][0])
    sums_in = pair_sum("w_in", gw_in, pair_in)
    dh1, parts_early, got, got_pool = _dgrad_blocked_call("dh1_bwd", dproj, w_in_g, _Comm(
        [sums_in, sums_pool], [parts_early], [landing(3, sums_in), landing(3, sums_pool)],
        lambda s, l: [_chip_exchange(s[0], l[1], 0, rows["w_in"]) + _chip_exchange(s[1], l[2], 0, rows["w_pool"])
                      + _gather_pass_on(l[0], 0, early_rows)]),
        merge=MERGE_W_IN)
    result["w_in"] = finish("w_in", gw_in, pair_in, got)
    result["w_pool"] = finish("w_pool", gw_pool, pair_pool, got_pool)
    grad_x, dg_mix = _norm_bwd_call("mix_norm_bwd", dh1, xs, dx2, g_mix2, False)
    packed = _pack([dg_mix])
    (parts_late,) = _comm_call("gather_g_mix_grad", _Comm(
        [packed], [], [from_everyone(packed)], lambda s, l: [_everyone(s[0], l[0])]))

    for names, parts, tag in ((_SMALL_EARLY, parts_early, "small_adamw"), (("g_mix",), parts_late, "g_mix_adamw")):
        outs = _small_final_call(tag, parts, _pack([weights[k] for k in names]), _pack([mom[k] for k in names]),
                                 _pack([var[k] for k in names]))
        if tag == "small_adamw":
            loss = outs[0][-1, 0]
        like = [weights[k] for k in names]
        unpacked = [_unpack(o, like) for o in outs]
        for idx, k in enumerate(names):
            result[k] = [unpacked[q][idx] for q in range(4)]

    grads = [result[k][0] for k in _ORDER]
    deltas = [result[k][1] for k in _ORDER]
    new_m = [result[k][2] for k in _ORDER]
    new_v = [result[k][3] for k in _ORDER]
    return (loss, grad_x.reshape(x.shape), *grads, *deltas, *new_m, *new_v)
```
